```python
import math
import jax, jax.numpy as jnp
from jax import lax
import numpy as np

D_MODEL = 1024
BATCH = 8
SEQ = 4096
DEPTH = 1

MEM_LEN = 256
D_FF = 2752
MLA_HEADS = 4
MLA_Q_RANK = 384
MLA_KV_RANK = 256
MLA_NOPE = 128
MLA_ROPE = 64
MLA_V = 128
MLA_QK = MLA_NOPE + MLA_ROPE
MLA_WIDTH = MLA_HEADS * MLA_V
SSM_WIDTH = D_MODEL - MLA_WIDTH
SSM_GROUP = 16
SSM_GROUPS = SSM_WIDTH // SSM_GROUP
SSM_STATE = 64
DT_MIN = 1e-3
DT_MAX = 1e-1
XATTN_HEADS = 4
XATTN_HEAD_DIM = 128
XATTN_WIDTH = XATTN_HEADS * XATTN_HEAD_DIM
IN_SPLITS = [MLA_Q_RANK, MLA_Q_RANK + MLA_KV_RANK, MLA_Q_RANK + MLA_KV_RANK + MLA_ROPE]
IN_WIDTH = MLA_Q_RANK + MLA_KV_RANK + MLA_ROPE + SSM_WIDTH
Q_BLOCK = 128
ROPE_THETA = 10000.0
EPS = 1e-6

kernel_name = "hymba_mla_s5_macaron_memxattn"


def rms_norm(x, g):
    xf = x.astype(jnp.float32)
    y = xf * lax.rsqrt(jnp.mean(xf * xf, axis=-1, keepdims=True) + EPS)
    return (y * g.astype(jnp.float32)).astype(x.dtype)


def swiglu(h, w_gate, w_up, w_down):
    return (jax.nn.silu(h @ w_gate) * (h @ w_up)) @ w_down


def rope(x, pos):
    half = x.shape[-1] // 2
    inv = ROPE_THETA ** (-jnp.arange(half, dtype=jnp.float32) / half)
    ang = (pos.astype(jnp.float32)[..., None] * inv)[:, :, None, :]
    cos, sin = jnp.cos(ang), jnp.sin(ang)
    x1 = x[..., :half].astype(jnp.float32)
    x2 = x[..., half:].astype(jnp.float32)
    return jnp.concatenate([x1 * cos - x2 * sin, x2 * cos + x1 * sin], axis=-1).astype(x.dtype)


def causal_block_attention(q, k, v):
    B, S, H, Dk = q.shape
    Dv = v.shape[-1]
    nb = S // Q_BLOCK
    scale = Dk ** -0.5
    qb = q.reshape(B, nb, Q_BLOCK, H, Dk).transpose(1, 0, 2, 3, 4)
    kpos = jnp.arange(S)

    def one_block(args):
        q_blk, i = args
        s = jnp.einsum('bqhd,bkhd->bhqk', q_blk, k).astype(jnp.float32) * scale
        qpos = i * Q_BLOCK + jnp.arange(Q_BLOCK)
        s = jnp.where(qpos[:, None] >= kpos[None, :], s, -jnp.inf)
        p = jax.nn.softmax(s, axis=-1).astype(v.dtype)
        return jnp.einsum('bhqk,bkhd->bqhd', p, v)

    out = lax.map(one_block, (qb, jnp.arange(nb)))
    return out.transpose(1, 0, 2, 3, 4).reshape(B, S, H, Dv)


def mla_mixer(c_q_in, c_kv_in, k_r, pos, q_norm, w_uq, kv_norm, w_ukv, qk_norm_q, qk_norm_k):
    B, S, _ = c_q_in.shape
    c_q = rms_norm(c_q_in, q_norm)
    q = (c_q @ w_uq).reshape(B, S, MLA_HEADS, MLA_QK)
    c_kv = rms_norm(c_kv_in, kv_norm)
    kv = (c_kv @ w_ukv).reshape(B, S, MLA_HEADS, MLA_NOPE + MLA_V)
    k_nope, v = kv[..., :MLA_NOPE], kv[..., MLA_NOPE:]
    k_rope = jnp.broadcast_to(k_r[:, :, None, :], (B, S, MLA_HEADS, MLA_ROPE))
    k = jnp.concatenate([k_nope, k_rope], axis=-1)
    q = rms_norm(q, qk_norm_q)
    k = rms_norm(k, qk_norm_k)
    q = jnp.concatenate([q[..., :MLA_NOPE], rope(q[..., MLA_NOPE:], pos)], axis=-1)
    k = jnp.concatenate([k[..., :MLA_NOPE], rope(k[..., MLA_NOPE:], pos)], axis=-1)
    o = causal_block_attention(q, k, v)
    return o.reshape(B, S, MLA_WIDTH)


def _complex_linear_combine(e1, e2):
    a1r, a1i, b1r, b1i = e1
    a2r, a2i, b2r, b2i = e2
    ar = a2r * a1r - a2i * a1i
    ai = a2r * a1i + a2i * a1r
    br = a2r * b1r - a2i * b1i + b2r
    bi = a2r * b1i + a2i * b1r + b2i
    return (ar, ai, br, bi)


def s5_mixer(u, a_re, a_im, log_dt, b_re, b_im, c_re, c_im, d, w_glu, b_glu):
    B, S, _ = u.shape
    f32 = jnp.float32
    uf = u.astype(f32).reshape(B, S, SSM_GROUPS, SSM_GROUP)
    lr, li = a_re.astype(f32), a_im.astype(f32)
    dt = jnp.exp(log_dt.astype(f32))[:, None]
    decay = jnp.exp(lr * dt)
    ar = decay * jnp.cos(li * dt)
    ai = decay * jnp.sin(li * dt)
    den = lr * lr + li * li
    nr = ar - 1.0
    coef_r = (nr * lr + ai * li) / den
    coef_i = (ai * lr - nr * li) / den
    br, bi = b_re.astype(f32), b_im.astype(f32)
    bbar_r = coef_r[..., None] * br - coef_i[..., None] * bi
    bbar_i = coef_r[..., None] * bi + coef_i[..., None] * br
    bu_r = jnp.einsum('bsgh,gph->bsgp', uf, bbar_r)
    bu_i = jnp.einsum('bsgh,gph->bsgp', uf, bbar_i)
    ar_t = jnp.broadcast_to(ar, bu_r.shape)
    ai_t = jnp.broadcast_to(ai, bu_r.shape)
    _, _, xr, xi = lax.associative_scan(_complex_linear_combine, (ar_t, ai_t, bu_r, bu_i), axis=1)
    y = (jnp.einsum('bsgp,ghp->bsgh', xr, c_re.astype(f32))
         - jnp.einsum('bsgp,ghp->bsgh', xi, c_im.astype(f32))
         + d.astype(f32) * uf)
    y = y.reshape(B, S, SSM_WIDTH)
    g = jax.nn.gelu(y)
    out = g * jax.nn.sigmoid(g @ w_glu.astype(f32) + b_glu.astype(f32))
    return out.astype(u.dtype)


def memory_cross_attention(h, mem, mem_norm, w_q, w_kv, qn, kn, w_o):
    B, S, _ = h.shape
    M = mem.shape[1]
    q = (h @ w_q).reshape(B, S, XATTN_HEADS, XATTN_HEAD_DIM)
    m = rms_norm(mem, mem_norm)
    kv = (m @ w_kv).reshape(B, M, 2, XATTN_HEADS, XATTN_HEAD_DIM)
    k, v = kv[:, :, 0], kv[:, :, 1]
    q = rms_norm(q, qn)
    k = rms_norm(k, kn)
    s = jnp.einsum('bshd,bmhd->bhsm', q, k).astype(jnp.float32) * (XATTN_HEAD_DIM ** -0.5)
    p = jax.nn.softmax(s, axis=-1).astype(v.dtype)
    o = jnp.einsum('bhsm,bmhd->bshd', p, v).reshape(B, S, XATTN_WIDTH)
    return o @ w_o


def _fwd_setup_inputs(seed: int = 0) -> dict:
    key = jax.random.key(seed)
    ks = iter(jax.random.split(key, 48))
    f32 = jnp.float32

    def nrm(shape):
        return jax.random.normal(next(ks), (DEPTH,) + shape, f32)

    def w(shape, fan_in):
        return nrm(shape) * (fan_in ** -0.5)

    def gain(dim):
        return 1.0 + 0.02 * nrm((dim,))

    x = jax.random.normal(next(ks), (BATCH, SEQ, D_MODEL), f32)
    mem = jax.random.normal(next(ks), (BATCH, MEM_LEN, D_MODEL), f32)
    positions = (jax.random.randint(next(ks), (BATCH, 1), 0, 1024, jnp.int32)
                 + jnp.arange(SEQ, dtype=jnp.int32)[None, :])
    ffn1_norm = gain(D_MODEL)
    ffn1_w_gate = w((D_MODEL, D_FF), D_MODEL)
    ffn1_w_up = w((D_MODEL, D_FF), D_MODEL)
    ffn1_w_down = w((D_FF, D_MODEL), D_FF)
    mix_norm = gain(D_MODEL)
    w_in = w((D_MODEL, IN_WIDTH), D_MODEL)
    mla_q_norm = gain(MLA_Q_RANK)
    mla_w_uq = w((MLA_Q_RANK, MLA_HEADS * MLA_QK), MLA_Q_RANK)
    mla_kv_norm = gain(MLA_KV_RANK)
    mla_w_ukv = w((MLA_KV_RANK, MLA_HEADS * (MLA_NOPE + MLA_V)), MLA_KV_RANK)
    mla_qk_norm_q = gain(MLA_QK)
    mla_qk_norm_k = gain(MLA_QK)
    n = jnp.arange(SSM_STATE, dtype=f32)
    ssm_a_re = -0.5 + 0.01 * nrm((SSM_GROUPS, SSM_STATE))
    ssm_a_im = math.pi * n + 0.01 * nrm((SSM_GROUPS, SSM_STATE))
    ssm_log_dt = jax.random.uniform(next(ks), (DEPTH, SSM_GROUPS), f32, math.log(DT_MIN), math.log(DT_MAX))
    ssm_b_re = nrm((SSM_GROUPS, SSM_STATE, SSM_GROUP)) * (0.5 / SSM_GROUP) ** 0.5
    ssm_b_im = nrm((SSM_GROUPS, SSM_STATE, SSM_GROUP)) * (0.5 / SSM_GROUP) ** 0.5
    ssm_c_re = nrm((SSM_GROUPS, SSM_GROUP, SSM_STATE)) * (0.5 / SSM_STATE) ** 0.5
    ssm_c_im = nrm((SSM_GROUPS, SSM_GROUP, SSM_STATE)) * (0.5 / SSM_STATE) ** 0.5
    ssm_d = nrm((SSM_GROUPS, SSM_GROUP))
    ssm_w_glu = w((SSM_WIDTH, SSM_WIDTH), SSM_WIDTH)
    ssm_b_glu = 0.01 * nrm((SSM_WIDTH,))
    out_norm_mla = gain(MLA_WIDTH)
    out_norm_ssm = gain(SSM_WIDTH)
    w_o = w((MLA_WIDTH + SSM_WIDTH, D_MODEL), MLA_WIDTH + SSM_WIDTH)
    xattn_norm = gain(D_MODEL)
    mem_norm = gain(D_MODEL)
    xattn_w_q = w((D_MODEL, XATTN_WIDTH), D_MODEL)
    xattn_w_kv = w((D_MODEL, 2 * XATTN_WIDTH), D_MODEL)
    xattn_q_norm = gain(XATTN_HEAD_DIM)
    xattn_k_norm = gain(XATTN_HEAD_DIM)
    xattn_w_o = w((XATTN_WIDTH, D_MODEL), XATTN_WIDTH)
    ffn2_norm = gain(D_MODEL)
    ffn2_w_gate = w((D_MODEL, D_FF), D_MODEL)
    ffn2_w_up = w((D_MODEL, D_FF), D_MODEL)
    ffn2_w_down = w((D_FF, D_MODEL), D_FF)
    return {
        'x': x, 'mem': mem, 'positions': positions,
        'ffn1_norm': ffn1_norm, 'ffn1_w_gate': ffn1_w_gate, 'ffn1_w_up': ffn1_w_up, 'ffn1_w_down': ffn1_w_down,
        'mix_norm': mix_norm, 'w_in': w_in,
        'mla_q_norm': mla_q_norm, 'mla_w_uq': mla_w_uq, 'mla_kv_norm': mla_kv_norm, 'mla_w_ukv': mla_w_ukv,
        'mla_qk_norm_q': mla_qk_norm_q, 'mla_qk_norm_k': mla_qk_norm_k,
        'ssm_a_re': ssm_a_re, 'ssm_a_im': ssm_a_im, 'ssm_log_dt': ssm_log_dt,
        'ssm_b_re': ssm_b_re, 'ssm_b_im': ssm_b_im, 'ssm_c_re': ssm_c_re, 'ssm_c_im': ssm_c_im,
        'ssm_d': ssm_d, 'ssm_w_glu': ssm_w_glu, 'ssm_b_glu': ssm_b_glu,
        'out_norm_mla': out_norm_mla, 'out_norm_ssm': out_norm_ssm, 'w_o': w_o,
        'xattn_norm': xattn_norm, 'mem_norm': mem_norm, 'xattn_w_q': xattn_w_q, 'xattn_w_kv': xattn_w_kv,
        'xattn_q_norm': xattn_q_norm, 'xattn_k_norm': xattn_k_norm, 'xattn_w_o': xattn_w_o,
        'ffn2_norm': ffn2_norm, 'ffn2_w_gate': ffn2_w_gate, 'ffn2_w_up': ffn2_w_up, 'ffn2_w_down': ffn2_w_down,
    }


def _fwd_reference(x, mem, positions,
              ffn1_norm, ffn1_w_gate, ffn1_w_up, ffn1_w_down,
              mix_norm, w_in,
              mla_q_norm, mla_w_uq, mla_kv_norm, mla_w_ukv, mla_qk_norm_q, mla_qk_norm_k,
              ssm_a_re, ssm_a_im, ssm_log_dt, ssm_b_re, ssm_b_im, ssm_c_re, ssm_c_im,
              ssm_d, ssm_w_glu, ssm_b_glu,
              out_norm_mla, out_norm_ssm, w_o,
              xattn_norm, mem_norm, xattn_w_q, xattn_w_kv, xattn_q_norm, xattn_k_norm, xattn_w_o,
              ffn2_norm, ffn2_w_gate, ffn2_w_up, ffn2_w_down):
    for l in range(DEPTH):
        x = x + 0.5 * swiglu(rms_norm(x, ffn1_norm[l]), ffn1_w_gate[l], ffn1_w_up[l], ffn1_w_down[l])
        h = rms_norm(x, mix_norm[l])
        proj = h @ w_in[l]
        c_q_in, c_kv_in, k_r, u = jnp.split(proj, IN_SPLITS, axis=-1)
        y_mla = mla_mixer(c_q_in, c_kv_in, k_r, positions, mla_q_norm[l], mla_w_uq[l],
                          mla_kv_norm[l], mla_w_ukv[l], mla_qk_norm_q[l], mla_qk_norm_k[l])
        y_ssm = s5_mixer(u, ssm_a_re[l], ssm_a_im[l], ssm_log_dt[l], ssm_b_re[l], ssm_b_im[l],
                         ssm_c_re[l], ssm_c_im[l], ssm_d[l], ssm_w_glu[l], ssm_b_glu[l])
        y = jnp.concatenate([rms_norm(y_mla, out_norm_mla[l]), rms_norm(y_ssm, out_norm_ssm[l])], axis=-1)
        x = x + y @ w_o[l]
        x = x + memory_cross_attention(rms_norm(x, xattn_norm[l]), mem, mem_norm[l], xattn_w_q[l],
                                       xattn_w_kv[l], xattn_q_norm[l], xattn_k_norm[l], xattn_w_o[l])
        x = x + 0.5 * swiglu(rms_norm(x, ffn2_norm[l]), ffn2_w_gate[l], ffn2_w_up[l], ffn2_w_down[l])
    return x


import jax as _jax
import jax.numpy as _jnp

TWIN_FORMAT = 'train_step'
FWD_PARAMS = ['x', 'mem', 'positions', 'ffn1_norm', 'ffn1_w_gate', 'ffn1_w_up', 'ffn1_w_down', 'mix_norm', 'w_in', 'mla_q_norm', 'mla_w_uq', 'mla_kv_norm', 'mla_w_ukv', 'mla_qk_norm_q', 'mla_qk_norm_k', 'ssm_a_re', 'ssm_a_im', 'ssm_log_dt', 'ssm_b_re', 'ssm_b_im', 'ssm_c_re', 'ssm_c_im', 'ssm_d', 'ssm_w_glu', 'ssm_b_glu', 'out_norm_mla', 'out_norm_ssm', 'w_o', 'xattn_norm', 'mem_norm', 'xattn_w_q', 'xattn_w_kv', 'xattn_q_norm', 'xattn_k_norm', 'xattn_w_o', 'ffn2_norm', 'ffn2_w_gate', 'ffn2_w_up', 'ffn2_w_down']
TWIN_WEIGHTS = ['ffn1_norm', 'ffn1_w_gate', 'ffn1_w_up', 'ffn1_w_down', 'mix_norm', 'w_in', 'mla_q_norm', 'mla_w_uq', 'mla_kv_norm', 'mla_w_ukv', 'mla_qk_norm_q', 'mla_qk_norm_k', 'ssm_a_re', 'ssm_a_im', 'ssm_log_dt', 'ssm_b_re', 'ssm_b_im', 'ssm_c_re', 'ssm_c_im', 'ssm_d', 'ssm_w_glu', 'ssm_b_glu', 'out_norm_mla', 'out_norm_ssm', 'w_o', 'xattn_norm', 'mem_norm', 'xattn_w_q', 'xattn_w_kv', 'xattn_q_norm', 'xattn_k_norm', 'xattn_w_o', 'ffn2_norm', 'ffn2_w_gate', 'ffn2_w_up', 'ffn2_w_down']
TWIN_DIFF_INPUT = 'x'
TWIN_INPUTS = ['x', 'mem', 'positions', 'ffn1_norm', 'ffn1_w_gate', 'ffn1_w_up', 'ffn1_w_down', 'mix_norm', 'w_in', 'mla_q_norm', 'mla_w_uq', 'mla_kv_norm', 'mla_w_ukv', 'mla_qk_norm_q', 'mla_qk_norm_k', 'ssm_a_re', 'ssm_a_im', 'ssm_log_dt', 'ssm_b_re', 'ssm_b_im', 'ssm_c_re', 'ssm_c_im', 'ssm_d', 'ssm_w_glu', 'ssm_b_glu', 'out_norm_mla', 'out_norm_ssm', 'w_o', 'xattn_norm', 'mem_norm', 'xattn_w_q', 'xattn_w_kv', 'xattn_q_norm', 'xattn_k_norm', 'xattn_w_o', 'ffn2_norm', 'ffn2_w_gate', 'ffn2_w_up', 'ffn2_w_down', 'loss_target', 'm_ffn1_norm', 'm_ffn1_w_gate', 'm_ffn1_w_up', 'm_ffn1_w_down', 'm_mix_norm', 'm_w_in', 'm_mla_q_norm', 'm_mla_w_uq', 'm_mla_kv_norm', 'm_mla_w_ukv', 'm_mla_qk_norm_q', 'm_mla_qk_norm_k', 'm_ssm_a_re', 'm_ssm_a_im', 'm_ssm_log_dt', 'm_ssm_b_re', 'm_ssm_b_im', 'm_ssm_c_re', 'm_ssm_c_im', 'm_ssm_d', 'm_ssm_w_glu', 'm_ssm_b_glu', 'm_out_norm_mla', 'm_out_norm_ssm', 'm_w_o', 'm_xattn_norm', 'm_mem_norm', 'm_xattn_w_q', 'm_xattn_w_kv', 'm_xattn_q_norm', 'm_xattn_k_norm', 'm_xattn_w_o', 'm_ffn2_norm', 'm_ffn2_w_gate', 'm_ffn2_w_up', 'm_ffn2_w_down', 'v_ffn1_norm', 'v_ffn1_w_gate', 'v_ffn1_w_up', 'v_ffn1_w_down', 'v_mix_norm', 'v_w_in', 'v_mla_q_norm', 'v_mla_w_uq', 'v_mla_kv_norm', 'v_mla_w_ukv', 'v_mla_qk_norm_q', 'v_mla_qk_norm_k', 'v_ssm_a_re', 'v_ssm_a_im', 'v_ssm_log_dt', 'v_ssm_b_re', 'v_ssm_b_im', 'v_ssm_c_re', 'v_ssm_c_im', 'v_ssm_d', 'v_ssm_w_glu', 'v_ssm_b_glu', 'v_out_norm_mla', 'v_out_norm_ssm', 'v_w_o', 'v_xattn_norm', 'v_mem_norm', 'v_xattn_w_q', 'v_xattn_w_kv', 'v_xattn_q_norm', 'v_xattn_k_norm', 'v_xattn_w_o', 'v_ffn2_norm', 'v_ffn2_w_gate', 'v_ffn2_w_up', 'v_ffn2_w_down']
TWIN_OUTPUTS = ['loss', 'grad_x', 'grad_ffn1_norm', 'grad_ffn1_w_gate', 'grad_ffn1_w_up', 'grad_ffn1_w_down', 'grad_mix_norm', 'grad_w_in', 'grad_mla_q_norm', 'grad_mla_w_uq', 'grad_mla_kv_norm', 'grad_mla_w_ukv', 'grad_mla_qk_norm_q', 'grad_mla_qk_norm_k', 'grad_ssm_a_re', 'grad_ssm_a_im', 'grad_ssm_log_dt', 'grad_ssm_b_re', 'grad_ssm_b_im', 'grad_ssm_c_re', 'grad_ssm_c_im', 'grad_ssm_d', 'grad_ssm_w_glu', 'grad_ssm_b_glu', 'grad_out_norm_mla', 'grad_out_norm_ssm', 'grad_w_o', 'grad_xattn_norm', 'grad_mem_norm', 'grad_xattn_w_q', 'grad_xattn_w_kv', 'grad_xattn_q_norm', 'grad_xattn_k_norm', 'grad_xattn_w_o', 'grad_ffn2_norm', 'grad_ffn2_w_gate', 'grad_ffn2_w_up', 'grad_ffn2_w_down', 'delta_ffn1_norm', 'delta_ffn1_w_gate', 'delta_ffn1_w_up', 'delta_ffn1_w_down', 'delta_mix_norm', 'delta_w_in', 'delta_mla_q_norm', 'delta_mla_w_uq', 'delta_mla_kv_norm', 'delta_mla_w_ukv', 'delta_mla_qk_norm_q', 'delta_mla_qk_norm_k', 'delta_ssm_a_re', 'delta_ssm_a_im', 'delta_ssm_log_dt', 'delta_ssm_b_re', 'delta_ssm_b_im', 'delta_ssm_c_re', 'delta_ssm_c_im', 'delta_ssm_d', 'delta_ssm_w_glu', 'delta_ssm_b_glu', 'delta_out_norm_mla', 'delta_out_norm_ssm', 'delta_w_o', 'delta_xattn_norm', 'delta_mem_norm', 'delta_xattn_w_q', 'delta_xattn_w_kv', 'delta_xattn_q_norm', 'delta_xattn_k_norm', 'delta_xattn_w_o', 'delta_ffn2_norm', 'delta_ffn2_w_gate', 'delta_ffn2_w_up', 'delta_ffn2_w_down', 'new_m_ffn1_norm', 'new_m_ffn1_w_gate', 'new_m_ffn1_w_up', 'new_m_ffn1_w_down', 'new_m_mix_norm', 'new_m_w_in', 'new_m_mla_q_norm', 'new_m_mla_w_uq', 'new_m_mla_kv_norm', 'new_m_mla_w_ukv', 'new_m_mla_qk_norm_q', 'new_m_mla_qk_norm_k', 'new_m_ssm_a_re', 'new_m_ssm_a_im', 'new_m_ssm_log_dt', 'new_m_ssm_b_re', 'new_m_ssm_b_im', 'new_m_ssm_c_re', 'new_m_ssm_c_im', 'new_m_ssm_d', 'new_m_ssm_w_glu', 'new_m_ssm_b_glu', 'new_m_out_norm_mla', 'new_m_out_norm_ssm', 'new_m_w_o', 'new_m_xattn_norm', 'new_m_mem_norm', 'new_m_xattn_w_q', 'new_m_xattn_w_kv', 'new_m_xattn_q_norm', 'new_m_xattn_k_norm', 'new_m_xattn_w_o', 'new_m_ffn2_norm', 'new_m_ffn2_w_gate', 'new_m_ffn2_w_up', 'new_m_ffn2_w_down', 'new_v_ffn1_norm', 'new_v_ffn1_w_gate', 'new_v_ffn1_w_up', 'new_v_ffn1_w_down', 'new_v_mix_norm', 'new_v_w_in', 'new_v_mla_q_norm', 'new_v_mla_w_uq', 'new_v_mla_kv_norm', 'new_v_mla_w_ukv', 'new_v_mla_qk_norm_q', 'new_v_mla_qk_norm_k', 'new_v_ssm_a_re', 'new_v_ssm_a_im', 'new_v_ssm_log_dt', 'new_v_ssm_b_re', 'new_v_ssm_b_im', 'new_v_ssm_c_re', 'new_v_ssm_c_im', 'new_v_ssm_d', 'new_v_ssm_w_glu', 'new_v_ssm_b_glu', 'new_v_out_norm_mla', 'new_v_out_norm_ssm', 'new_v_w_o', 'new_v_xattn_norm', 'new_v_mem_norm', 'new_v_xattn_w_q', 'new_v_xattn_w_kv', 'new_v_xattn_q_norm', 'new_v_xattn_k_norm', 'new_v_xattn_w_o', 'new_v_ffn2_norm', 'new_v_ffn2_w_gate', 'new_v_ffn2_w_up', 'new_v_ffn2_w_down']
TWIN_LEAF_KINDS = {'loss': 'loss', 'grad_x': 'grad_x', 'grad_ffn1_norm': 'grad_w', 'grad_ffn1_w_gate': 'grad_w', 'grad_ffn1_w_up': 'grad_w', 'grad_ffn1_w_down': 'grad_w', 'grad_mix_norm': 'grad_w', 'grad_w_in': 'grad_w', 'grad_mla_q_norm': 'grad_w', 'grad_mla_w_uq': 'grad_w', 'grad_mla_kv_norm': 'grad_w', 'grad_mla_w_ukv': 'grad_w', 'grad_mla_qk_norm_q': 'grad_w', 'grad_mla_qk_norm_k': 'grad_w', 'grad_ssm_a_re': 'grad_w', 'grad_ssm_a_im': 'grad_w', 'grad_ssm_log_dt': 'grad_w', 'grad_ssm_b_re': 'grad_w', 'grad_ssm_b_im': 'grad_w', 'grad_ssm_c_re': 'grad_w', 'grad_ssm_c_im': 'grad_w', 'grad_ssm_d': 'grad_w', 'grad_ssm_w_glu': 'grad_w', 'grad_ssm_b_glu': 'grad_w', 'grad_out_norm_mla': 'grad_w', 'grad_out_norm_ssm': 'grad_w', 'grad_w_o': 'grad_w', 'grad_xattn_norm': 'grad_w', 'grad_mem_norm': 'grad_w', 'grad_xattn_w_q': 'grad_w', 'grad_xattn_w_kv': 'grad_w', 'grad_xattn_q_norm': 'grad_w', 'grad_xattn_k_norm': 'grad_w', 'grad_xattn_w_o': 'grad_w', 'grad_ffn2_norm': 'grad_w', 'grad_ffn2_w_gate': 'grad_w', 'grad_ffn2_w_up': 'grad_w', 'grad_ffn2_w_down': 'grad_w', 'delta_ffn1_norm': 'delta_w', 'delta_ffn1_w_gate': 'delta_w', 'delta_ffn1_w_up': 'delta_w', 'delta_ffn1_w_down': 'delta_w', 'delta_mix_norm': 'delta_w', 'delta_w_in': 'delta_w', 'delta_mla_q_norm': 'delta_w', 'delta_mla_w_uq': 'delta_w', 'delta_mla_kv_norm': 'delta_w', 'delta_mla_w_ukv': 'delta_w', 'delta_mla_qk_norm_q': 'delta_w', 'delta_mla_qk_norm_k': 'delta_w', 'delta_ssm_a_re': 'delta_w', 'delta_ssm_a_im': 'delta_w', 'delta_ssm_log_dt': 'delta_w', 'delta_ssm_b_re': 'delta_w', 'delta_ssm_b_im': 'delta_w', 'delta_ssm_c_re': 'delta_w', 'delta_ssm_c_im': 'delta_w', 'delta_ssm_d': 'delta_w', 'delta_ssm_w_glu': 'delta_w', 'delta_ssm_b_glu': 'delta_w', 'delta_out_norm_mla': 'delta_w', 'delta_out_norm_ssm': 'delta_w', 'delta_w_o': 'delta_w', 'delta_xattn_norm': 'delta_w', 'delta_mem_norm': 'delta_w', 'delta_xattn_w_q': 'delta_w', 'delta_xattn_w_kv': 'delta_w', 'delta_xattn_q_norm': 'delta_w', 'delta_xattn_k_norm': 'delta_w', 'delta_xattn_w_o': 'delta_w', 'delta_ffn2_norm': 'delta_w', 'delta_ffn2_w_gate': 'delta_w', 'delta_ffn2_w_up': 'delta_w', 'delta_ffn2_w_down': 'delta_w', 'new_m_ffn1_norm': 'new_m', 'new_m_ffn1_w_gate': 'new_m', 'new_m_ffn1_w_up': 'new_m', 'new_m_ffn1_w_down': 'new_m', 'new_m_mix_norm': 'new_m', 'new_m_w_in': 'new_m', 'new_m_mla_q_norm': 'new_m', 'new_m_mla_w_uq': 'new_m', 'new_m_mla_kv_norm': 'new_m', 'new_m_mla_w_ukv': 'new_m', 'new_m_mla_qk_norm_q': 'new_m', 'new_m_mla_qk_norm_k': 'new_m', 'new_m_ssm_a_re': 'new_m', 'new_m_ssm_a_im': 'new_m', 'new_m_ssm_log_dt': 'new_m', 'new_m_ssm_b_re': 'new_m', 'new_m_ssm_b_im': 'new_m', 'new_m_ssm_c_re': 'new_m', 'new_m_ssm_c_im': 'new_m', 'new_m_ssm_d': 'new_m', 'new_m_ssm_w_glu': 'new_m', 'new_m_ssm_b_glu': 'new_m', 'new_m_out_norm_mla': 'new_m', 'new_m_out_norm_ssm': 'new_m', 'new_m_w_o': 'new_m', 'new_m_xattn_norm': 'new_m', 'new_m_mem_norm': 'new_m', 'new_m_xattn_w_q': 'new_m', 'new_m_xattn_w_kv': 'new_m', 'new_m_xattn_q_norm': 'new_m', 'new_m_xattn_k_norm': 'new_m', 'new_m_xattn_w_o': 'new_m', 'new_m_ffn2_norm': 'new_m', 'new_m_ffn2_w_gate': 'new_m', 'new_m_ffn2_w_up': 'new_m', 'new_m_ffn2_w_down': 'new_m', 'new_v_ffn1_norm': 'new_v', 'new_v_ffn1_w_gate': 'new_v', 'new_v_ffn1_w_up': 'new_v', 'new_v_ffn1_w_down': 'new_v', 'new_v_mix_norm': 'new_v', 'new_v_w_in': 'new_v', 'new_v_mla_q_norm': 'new_v', 'new_v_mla_w_uq': 'new_v', 'new_v_mla_kv_norm': 'new_v', 'new_v_mla_w_ukv': 'new_v', 'new_v_mla_qk_norm_q': 'new_v', 'new_v_mla_qk_norm_k': 'new_v', 'new_v_ssm_a_re': 'new_v', 'new_v_ssm_a_im': 'new_v', 'new_v_ssm_log_dt': 'new_v', 'new_v_ssm_b_re': 'new_v', 'new_v_ssm_b_im': 'new_v', 'new_v_ssm_c_re': 'new_v', 'new_v_ssm_c_im': 'new_v', 'new_v_ssm_d': 'new_v', 'new_v_ssm_w_glu': 'new_v', 'new_v_ssm_b_glu': 'new_v', 'new_v_out_norm_mla': 'new_v', 'new_v_out_norm_ssm': 'new_v', 'new_v_w_o': 'new_v', 'new_v_xattn_norm': 'new_v', 'new_v_mem_norm': 'new_v', 'new_v_xattn_w_q': 'new_v', 'new_v_xattn_w_kv': 'new_v', 'new_v_xattn_q_norm': 'new_v', 'new_v_xattn_k_norm': 'new_v', 'new_v_xattn_w_o': 'new_v', 'new_v_ffn2_norm': 'new_v', 'new_v_ffn2_w_gate': 'new_v', 'new_v_ffn2_w_up': 'new_v', 'new_v_ffn2_w_down': 'new_v'}


def _forward(args):
    return _fwd_reference(*[args[k] for k in FWD_PARAMS])


def _output_shape():
    def fwd():
        inp = _fwd_setup_inputs(0)
        return _fwd_reference(*[inp[k] for k in FWD_PARAMS])
    out = _jax.eval_shape(fwd)
    return out.shape, out.dtype

N_MICROBATCH = 1
ADAM_LR = 0.001
ADAM_B1 = 0.9
ADAM_B2 = 0.999
ADAM_EPS = 1e-08
ADAM_WD = 0.01
ADAM_STEP = 10
PER_EXAMPLE_BATCH_AXIS = {'x': 0, 'mem': 0, 'positions': 0, 'loss_target': 0}
SHARED_INPUTS = []
_WEIGHT_DTYPES = {'ffn1_norm': _jnp.float32, 'ffn1_w_gate': _jnp.float32, 'ffn1_w_up': _jnp.float32, 'ffn1_w_down': _jnp.float32, 'mix_norm': _jnp.float32, 'w_in': _jnp.float32, 'mla_q_norm': _jnp.float32, 'mla_w_uq': _jnp.float32, 'mla_kv_norm': _jnp.float32, 'mla_w_ukv': _jnp.float32, 'mla_qk_norm_q': _jnp.float32, 'mla_qk_norm_k': _jnp.float32, 'ssm_a_re': _jnp.float32, 'ssm_a_im': _jnp.float32, 'ssm_log_dt': _jnp.float32, 'ssm_b_re': _jnp.float32, 'ssm_b_im': _jnp.float32, 'ssm_c_re': _jnp.float32, 'ssm_c_im': _jnp.float32, 'ssm_d': _jnp.float32, 'ssm_w_glu': _jnp.float32, 'ssm_b_glu': _jnp.float32, 'out_norm_mla': _jnp.float32, 'out_norm_ssm': _jnp.float32, 'w_o': _jnp.float32, 'xattn_norm': _jnp.float32, 'mem_norm': _jnp.float32, 'xattn_w_q': _jnp.float32, 'xattn_w_kv': _jnp.float32, 'xattn_q_norm': _jnp.float32, 'xattn_k_norm': _jnp.float32, 'xattn_w_o': _jnp.float32, 'ffn2_norm': _jnp.float32, 'ffn2_w_gate': _jnp.float32, 'ffn2_w_up': _jnp.float32, 'ffn2_w_down': _jnp.float32}
MOMENT_SCALE = {'ffn1_norm': 6.246302e+00, 'ffn1_w_gate': 3.867825e-01, 'ffn1_w_up': 3.986868e-01, 'ffn1_w_down': 6.830839e-01, 'mix_norm': 1.626125e+00, 'w_in': 1.406545e+00, 'mla_q_norm': 1.105633e+00, 'mla_w_uq': 7.196378e-01, 'mla_kv_norm': 4.632531e+00, 'mla_w_ukv': 1.766514e+00, 'mla_qk_norm_q': 1.970823e+00, 'mla_qk_norm_k': 1.964046e+00, 'ssm_a_re': 2.169944e-02, 'ssm_a_im': 2.014536e-02, 'ssm_log_dt': 8.435971e+00, 'ssm_b_re': 1.554651e-02, 'ssm_b_im': 1.589438e-02, 'ssm_c_re': 3.091804e-02, 'ssm_c_im': 3.239374e-02, 'ssm_d': 6.157296e+00, 'ssm_w_glu': 9.953687e-01, 'ssm_b_glu': 2.977911e+00, 'out_norm_mla': 3.229295e+01, 'out_norm_ssm': 6.151518e+01, 'w_o': 4.214589e+00, 'xattn_norm': 1.470694e-01, 'mem_norm': 5.245382e-01, 'xattn_w_q': 1.984084e-01, 'xattn_w_kv': 3.589848e-01, 'xattn_q_norm': 2.483792e+00, 'xattn_k_norm': 2.487907e+00, 'xattn_w_o': 3.183619e-01, 'ffn2_norm': 6.197267e+00, 'ffn2_w_gate': 2.484900e-01, 'ffn2_w_up': 2.509468e-01, 'ffn2_w_down': 4.003042e-01}


def _to_microbatches(a, axis):
    t = _jnp.moveaxis(a, axis, 0)
    t = t.reshape((N_MICROBATCH, t.shape[0] // N_MICROBATCH) + t.shape[1:])
    return _jnp.moveaxis(t, 1, axis + 1)


def setup_inputs(seed: int = 0) -> dict:
    inp = _fwd_setup_inputs(seed)
    key = _jax.random.fold_in(_jax.random.key(seed), 7919)
    shape, _ = _output_shape()
    out = dict(inp)
    out["loss_target"] = _jax.random.normal(_jax.random.fold_in(key, 0), shape, _jnp.float32)
    for i, name in enumerate(TWIN_WEIGHTS):
        w = inp[name].astype(_jnp.float32)
        if MOMENT_SCALE is None:
            s = _jnp.sqrt(_jnp.mean(_jnp.square(w)) + 1e-30)
        else:
            s = MOMENT_SCALE[name]
        km, kv = _jax.random.split(_jax.random.fold_in(key, i + 1))
        out[name] = w
        out["m_" + name] = s * _jax.random.normal(km, w.shape, _jnp.float32)
        out["v_" + name] = (s * s) * _jax.random.uniform(kv, w.shape, _jnp.float32, 0.5, 1.5)
    if N_MICROBATCH > 1:
        for name, axis in PER_EXAMPLE_BATCH_AXIS.items():
            out[name] = _to_microbatches(out[name], axis)
    return {'x': out['x'], 'mem': out['mem'], 'positions': out['positions'], 'ffn1_norm': out['ffn1_norm'], 'ffn1_w_gate': out['ffn1_w_gate'], 'ffn1_w_up': out['ffn1_w_up'], 'ffn1_w_down': out['ffn1_w_down'], 'mix_norm': out['mix_norm'], 'w_in': out['w_in'], 'mla_q_norm': out['mla_q_norm'], 'mla_w_uq': out['mla_w_uq'], 'mla_kv_norm': out['mla_kv_norm'], 'mla_w_ukv': out['mla_w_ukv'], 'mla_qk_norm_q': out['mla_qk_norm_q'], 'mla_qk_norm_k': out['mla_qk_norm_k'], 'ssm_a_re': out['ssm_a_re'], 'ssm_a_im': out['ssm_a_im'], 'ssm_log_dt': out['ssm_log_dt'], 'ssm_b_re': out['ssm_b_re'], 'ssm_b_im': out['ssm_b_im'], 'ssm_c_re': out['ssm_c_re'], 'ssm_c_im': out['ssm_c_im'], 'ssm_d': out['ssm_d'], 'ssm_w_glu': out['ssm_w_glu'], 'ssm_b_glu': out['ssm_b_glu'], 'out_norm_mla': out['out_norm_mla'], 'out_norm_ssm': out['out_norm_ssm'], 'w_o': out['w_o'], 'xattn_norm': out['xattn_norm'], 'mem_norm': out['mem_norm'], 'xattn_w_q': out['xattn_w_q'], 'xattn_w_kv': out['xattn_w_kv'], 'xattn_q_norm': out['xattn_q_norm'], 'xattn_k_norm': out['xattn_k_norm'], 'xattn_w_o': out['xattn_w_o'], 'ffn2_norm': out['ffn2_norm'], 'ffn2_w_gate': out['ffn2_w_gate'], 'ffn2_w_up': out['ffn2_w_up'], 'ffn2_w_down': out['ffn2_w_down'], 'loss_target': out['loss_target'], 'm_ffn1_norm': out['m_ffn1_norm'], 'm_ffn1_w_gate': out['m_ffn1_w_gate'], 'm_ffn1_w_up': out['m_ffn1_w_up'], 'm_ffn1_w_down': out['m_ffn1_w_down'], 'm_mix_norm': out['m_mix_norm'], 'm_w_in': out['m_w_in'], 'm_mla_q_norm': out['m_mla_q_norm'], 'm_mla_w_uq': out['m_mla_w_uq'], 'm_mla_kv_norm': out['m_mla_kv_norm'], 'm_mla_w_ukv': out['m_mla_w_ukv'], 'm_mla_qk_norm_q': out['m_mla_qk_norm_q'], 'm_mla_qk_norm_k': out['m_mla_qk_norm_k'], 'm_ssm_a_re': out['m_ssm_a_re'], 'm_ssm_a_im': out['m_ssm_a_im'], 'm_ssm_log_dt': out['m_ssm_log_dt'], 'm_ssm_b_re': out['m_ssm_b_re'], 'm_ssm_b_im': out['m_ssm_b_im'], 'm_ssm_c_re': out['m_ssm_c_re'], 'm_ssm_c_im': out['m_ssm_c_im'], 'm_ssm_d': out['m_ssm_d'], 'm_ssm_w_glu': out['m_ssm_w_glu'], 'm_ssm_b_glu': out['m_ssm_b_glu'], 'm_out_norm_mla': out['m_out_norm_mla'], 'm_out_norm_ssm': out['m_out_norm_ssm'], 'm_w_o': out['m_w_o'], 'm_xattn_norm': out['m_xattn_norm'], 'm_mem_norm': out['m_mem_norm'], 'm_xattn_w_q': out['m_xattn_w_q'], 'm_xattn_w_kv': out['m_xattn_w_kv'], 'm_xattn_q_norm': out['m_xattn_q_norm'], 'm_xattn_k_norm': out['m_xattn_k_norm'], 'm_xattn_w_o': out['m_xattn_w_o'], 'm_ffn2_norm': out['m_ffn2_norm'], 'm_ffn2_w_gate': out['m_ffn2_w_gate'], 'm_ffn2_w_up': out['m_ffn2_w_up'], 'm_ffn2_w_down': out['m_ffn2_w_down'], 'v_ffn1_norm': out['v_ffn1_norm'], 'v_ffn1_w_gate': out['v_ffn1_w_gate'], 'v_ffn1_w_up': out['v_ffn1_w_up'], 'v_ffn1_w_down': out['v_ffn1_w_down'], 'v_mix_norm': out['v_mix_norm'], 'v_w_in': out['v_w_in'], 'v_mla_q_norm': out['v_mla_q_norm'], 'v_mla_w_uq': out['v_mla_w_uq'], 'v_mla_kv_norm': out['v_mla_kv_norm'], 'v_mla_w_ukv': out['v_mla_w_ukv'], 'v_mla_qk_norm_q': out['v_mla_qk_norm_q'], 'v_mla_qk_norm_k': out['v_mla_qk_norm_k'], 'v_ssm_a_re': out['v_ssm_a_re'], 'v_ssm_a_im': out['v_ssm_a_im'], 'v_ssm_log_dt': out['v_ssm_log_dt'], 'v_ssm_b_re': out['v_ssm_b_re'], 'v_ssm_b_im': out['v_ssm_b_im'], 'v_ssm_c_re': out['v_ssm_c_re'], 'v_ssm_c_im': out['v_ssm_c_im'], 'v_ssm_d': out['v_ssm_d'], 'v_ssm_w_glu': out['v_ssm_w_glu'], 'v_ssm_b_glu': out['v_ssm_b_glu'], 'v_out_norm_mla': out['v_out_norm_mla'], 'v_out_norm_ssm': out['v_out_norm_ssm'], 'v_w_o': out['v_w_o'], 'v_xattn_norm': out['v_xattn_norm'], 'v_mem_norm': out['v_mem_norm'], 'v_xattn_w_q': out['v_xattn_w_q'], 'v_xattn_w_kv': out['v_xattn_w_kv'], 'v_xattn_q_norm': out['v_xattn_q_norm'], 'v_xattn_k_norm': out['v_xattn_k_norm'], 'v_xattn_w_o': out['v_xattn_w_o'], 'v_ffn2_norm': out['v_ffn2_norm'], 'v_ffn2_w_gate': out['v_ffn2_w_gate'], 'v_ffn2_w_up': out['v_ffn2_w_up'], 'v_ffn2_w_down': out['v_ffn2_w_down']}


def _loss(weights, diff, rest, loss_target):
    with _jax.named_scope("forward"):
        args = {**rest, TWIN_DIFF_INPUT: diff, **{k: w.astype(_WEIGHT_DTYPES[k]) for k, w in weights.items()}}
        y = _forward(args)
    with _jax.named_scope("loss_head"):
        err = _jnp.square(y.astype(_jnp.float32) - loss_target)
        return 0.5 * _jnp.sum(_jnp.mean(err, axis=-1)) if err.ndim else 0.5 * err


def _adamw(w, g, m, v):
    m = ADAM_B1 * m + (1.0 - ADAM_B1) * g
    v = ADAM_B2 * v + (1.0 - ADAM_B2) * _jnp.square(g)
    m_hat = m / (1.0 - ADAM_B1 ** ADAM_STEP)
    v_hat = v / (1.0 - ADAM_B2 ** ADAM_STEP)
    delta = -ADAM_LR * (m_hat / (_jnp.sqrt(v_hat) + ADAM_EPS) + ADAM_WD * w)
    return delta, m, v


def reference(x, mem, positions, ffn1_norm, ffn1_w_gate, ffn1_w_up, ffn1_w_down, mix_norm, w_in, mla_q_norm, mla_w_uq, mla_kv_norm, mla_w_ukv, mla_qk_norm_q, mla_qk_norm_k, ssm_a_re, ssm_a_im, ssm_log_dt, ssm_b_re, ssm_b_im, ssm_c_re, ssm_c_im, ssm_d, ssm_w_glu, ssm_b_glu, out_norm_mla, out_norm_ssm, w_o, xattn_norm, mem_norm, xattn_w_q, xattn_w_kv, xattn_q_norm, xattn_k_norm, xattn_w_o, ffn2_norm, ffn2_w_gate, ffn2_w_up, ffn2_w_down, loss_target, m_ffn1_norm, m_ffn1_w_gate, m_ffn1_w_up, m_ffn1_w_down, m_mix_norm, m_w_in, m_mla_q_norm, m_mla_w_uq, m_mla_kv_norm, m_mla_w_ukv, m_mla_qk_norm_q, m_mla_qk_norm_k, m_ssm_a_re, m_ssm_a_im, m_ssm_log_dt, m_ssm_b_re, m_ssm_b_im, m_ssm_c_re, m_ssm_c_im, m_ssm_d, m_ssm_w_glu, m_ssm_b_glu, m_out_norm_mla, m_out_norm_ssm, m_w_o, m_xattn_norm, m_mem_norm, m_xattn_w_q, m_xattn_w_kv, m_xattn_q_norm, m_xattn_k_norm, m_xattn_w_o, m_ffn2_norm, m_ffn2_w_gate, m_ffn2_w_up, m_ffn2_w_down, v_ffn1_norm, v_ffn1_w_gate, v_ffn1_w_up, v_ffn1_w_down, v_mix_norm, v_w_in, v_mla_q_norm, v_mla_w_uq, v_mla_kv_norm, v_mla_w_ukv, v_mla_qk_norm_q, v_mla_qk_norm_k, v_ssm_a_re, v_ssm_a_im, v_ssm_log_dt, v_ssm_b_re, v_ssm_b_im, v_ssm_c_re, v_ssm_c_im, v_ssm_d, v_ssm_w_glu, v_ssm_b_glu, v_out_norm_mla, v_out_norm_ssm, v_w_o, v_xattn_norm, v_mem_norm, v_xattn_w_q, v_xattn_w_kv, v_xattn_q_norm, v_xattn_k_norm, v_xattn_w_o, v_ffn2_norm, v_ffn2_w_gate, v_ffn2_w_up, v_ffn2_w_down):
    given = dict(x=x, mem=mem, positions=positions, ffn1_norm=ffn1_norm, ffn1_w_gate=ffn1_w_gate, ffn1_w_up=ffn1_w_up, ffn1_w_down=ffn1_w_down, mix_norm=mix_norm, w_in=w_in, mla_q_norm=mla_q_norm, mla_w_uq=mla_w_uq, mla_kv_norm=mla_kv_norm, mla_w_ukv=mla_w_ukv, mla_qk_norm_q=mla_qk_norm_q, mla_qk_norm_k=mla_qk_norm_k, ssm_a_re=ssm_a_re, ssm_a_im=ssm_a_im, ssm_log_dt=ssm_log_dt, ssm_b_re=ssm_b_re, ssm_b_im=ssm_b_im, ssm_c_re=ssm_c_re, ssm_c_im=ssm_c_im, ssm_d=ssm_d, ssm_w_glu=ssm_w_glu, ssm_b_glu=ssm_b_glu, out_norm_mla=out_norm_mla, out_norm_ssm=out_norm_ssm, w_o=w_o, xattn_norm=xattn_norm, mem_norm=mem_norm, xattn_w_q=xattn_w_q, xattn_w_kv=xattn_w_kv, xattn_q_norm=xattn_q_norm, xattn_k_norm=xattn_k_norm, xattn_w_o=xattn_w_o, ffn2_norm=ffn2_norm, ffn2_w_gate=ffn2_w_gate, ffn2_w_up=ffn2_w_up, ffn2_w_down=ffn2_w_down, loss_target=loss_target, m_ffn1_norm=m_ffn1_norm, m_ffn1_w_gate=m_ffn1_w_gate, m_ffn1_w_up=m_ffn1_w_up, m_ffn1_w_down=m_ffn1_w_down, m_mix_norm=m_mix_norm, m_w_in=m_w_in, m_mla_q_norm=m_mla_q_norm, m_mla_w_uq=m_mla_w_uq, m_mla_kv_norm=m_mla_kv_norm, m_mla_w_ukv=m_mla_w_ukv, m_mla_qk_norm_q=m_mla_qk_norm_q, m_mla_qk_norm_k=m_mla_qk_norm_k, m_ssm_a_re=m_ssm_a_re, m_ssm_a_im=m_ssm_a_im, m_ssm_log_dt=m_ssm_log_dt, m_ssm_b_re=m_ssm_b_re, m_ssm_b_im=m_ssm_b_im, m_ssm_c_re=m_ssm_c_re, m_ssm_c_im=m_ssm_c_im, m_ssm_d=m_ssm_d, m_ssm_w_glu=m_ssm_w_glu, m_ssm_b_glu=m_ssm_b_glu, m_out_norm_mla=m_out_norm_mla, m_out_norm_ssm=m_out_norm_ssm, m_w_o=m_w_o, m_xattn_norm=m_xattn_norm, m_mem_norm=m_mem_norm, m_xattn_w_q=m_xattn_w_q, m_xattn_w_kv=m_xattn_w_kv, m_xattn_q_norm=m_xattn_q_norm, m_xattn_k_norm=m_xattn_k_norm, m_xattn_w_o=m_xattn_w_o, m_ffn2_norm=m_ffn2_norm, m_ffn2_w_gate=m_ffn2_w_gate, m_ffn2_w_up=m_ffn2_w_up, m_ffn2_w_down=m_ffn2_w_down, v_ffn1_norm=v_ffn1_norm, v_ffn1_w_gate=v_ffn1_w_gate, v_ffn1_w_up=v_ffn1_w_up, v_ffn1_w_down=v_ffn1_w_down, v_mix_norm=v_mix_norm, v_w_in=v_w_in, v_mla_q_norm=v_mla_q_norm, v_mla_w_uq=v_mla_w_uq, v_mla_kv_norm=v_mla_kv_norm, v_mla_w_ukv=v_mla_w_ukv, v_mla_qk_norm_q=v_mla_qk_norm_q, v_mla_qk_norm_k=v_mla_qk_norm_k, v_ssm_a_re=v_ssm_a_re, v_ssm_a_im=v_ssm_a_im, v_ssm_log_dt=v_ssm_log_dt, v_ssm_b_re=v_ssm_b_re, v_ssm_b_im=v_ssm_b_im, v_ssm_c_re=v_ssm_c_re, v_ssm_c_im=v_ssm_c_im, v_ssm_d=v_ssm_d, v_ssm_w_glu=v_ssm_w_glu, v_ssm_b_glu=v_ssm_b_glu, v_out_norm_mla=v_out_norm_mla, v_out_norm_ssm=v_out_norm_ssm, v_w_o=v_w_o, v_xattn_norm=v_xattn_norm, v_mem_norm=v_mem_norm, v_xattn_w_q=v_xattn_w_q, v_xattn_w_kv=v_xattn_w_kv, v_xattn_q_norm=v_xattn_q_norm, v_xattn_k_norm=v_xattn_k_norm, v_xattn_w_o=v_xattn_w_o, v_ffn2_norm=v_ffn2_norm, v_ffn2_w_gate=v_ffn2_w_gate, v_ffn2_w_up=v_ffn2_w_up, v_ffn2_w_down=v_ffn2_w_down)
    weights = {n: given[n] for n in TWIN_WEIGHTS}
    shared = {n: given[n] for n in SHARED_INPUTS}
    per_example = {n: given[n] for n in ['x', 'mem', 'positions']}
    grad_fn = _jax.value_and_grad(_loss, argnums=(0, 1))

    def one_microbatch(ex, loss_target):
        ex = dict(ex)
        diff = ex.pop(TWIN_DIFF_INPUT)
        return grad_fn(weights, diff, {**shared, **ex}, loss_target)

    if N_MICROBATCH == 1:
        loss, (grad_w, grad_x) = one_microbatch(per_example, given["loss_target"])
    else:
        def body(carry, xs):
            loss_sum, grad_sum = carry
            l_k, (gw_k, gx_k) = one_microbatch(xs[0], xs[1])
            with _jax.named_scope("update"):
                return (loss_sum + l_k, _jax.tree.map(_jnp.add, grad_sum, gw_k)), gx_k

        init = (_jnp.zeros((), _jnp.float32), _jax.tree.map(_jnp.zeros_like, weights))
        (loss, grad_w), grad_x = _jax.lax.scan(body, init, (per_example, given["loss_target"]))
    with _jax.named_scope("update"):
        delta_w, new_m, new_v = {}, {}, {}
        for n in TWIN_WEIGHTS:
            delta_w[n], new_m[n], new_v[n] = _adamw(weights[n], grad_w[n], given["m_" + n], given["v_" + n])
    return (loss, grad_x, *[grad_w[n] for n in TWIN_WEIGHTS], *[delta_w[n] for n in TWIN_WEIGHTS],
            *[new_m[n] for n in TWIN_WEIGHTS], *[new_v[n] for n in TWIN_WEIGHTS])
```

```python
import functools
import math

import jax
import jax.numpy as jnp
import numpy as np
from jax import lax
from jax.experimental import pallas as pl
from jax.experimental.pallas import tpu as pltpu

F32 = jnp.float32
BF16 = jnp.bfloat16

N_DEV = 8
D = 1024
D_FF = 2752
D_FFP = 2816
MEM_LEN = 256
H = 4
Q_RANK, KV_RANK, NOPE, ROPE, VD = 384, 256, 128, 64, 128
QK = NOPE + ROPE
HQ = 2 * 128
SSM_W, SSM_G, SSM_GRP, SSM_P = 512, 32, 16, 64
SSM_N = SSM_G * SSM_P
IN_W = 1216
IN_WP = 1408
XH = 128
EPS = 1e-6
ROPE_THETA = 10000.0
SCAN_CHUNKS = 8

ADAM_LR, ADAM_B1, ADAM_B2, ADAM_EPS, ADAM_WD, ADAM_STEP = 0.001, 0.9, 0.999, 1e-08, 0.01, 10

VMEM_LIMIT = 56 * 1024 * 1024
LANES = 128

WEIGHTS = ['ffn1_norm', 'ffn1_w_gate', 'ffn1_w_up', 'ffn1_w_down', 'mix_norm', 'w_in', 'mla_q_norm', 'mla_w_uq',
           'mla_kv_norm', 'mla_w_ukv', 'mla_qk_norm_q', 'mla_qk_norm_k', 'ssm_a_re', 'ssm_a_im', 'ssm_log_dt',
           'ssm_b_re', 'ssm_b_im', 'ssm_c_re', 'ssm_c_im', 'ssm_d', 'ssm_w_glu', 'ssm_b_glu', 'out_norm_mla',
           'out_norm_ssm', 'w_o', 'xattn_norm', 'mem_norm', 'xattn_w_q', 'xattn_w_kv', 'xattn_q_norm',
           'xattn_k_norm', 'xattn_w_o', 'ffn2_norm', 'ffn2_w_gate', 'ffn2_w_up', 'ffn2_w_down']
SHARD_AXIS = {'ffn1_w_gate': 1, 'ffn1_w_up': 1, 'ffn1_w_down': 0, 'w_in': 1, 'mla_w_uq': 1, 'mla_w_ukv': 1,
              'ssm_w_glu': 0, 'w_o': 0, 'xattn_w_q': 0, 'xattn_w_kv': 0, 'xattn_w_o': 1,
              'ffn2_w_gate': 1, 'ffn2_w_up': 1, 'ffn2_w_down': 0}
SHARDED = [n for n in WEIGHTS if n in SHARD_AXIS]
SMALL = [n for n in WEIGHTS if n not in SHARD_AXIS]


def _params(sem=None):
    return pltpu.CompilerParams(dimension_semantics=sem, vmem_limit_bytes=VMEM_LIMIT)


def _tile(n, cap):
    if n <= cap:
        return n
    best = None
    for t in range(LANES, cap + 1, LANES):
        if n % t == 0:
            best = t
    assert best is not None, (n, cap)
    return best


def _mm(a, b, *, ta=False, tb=False, out_dtype=F32, res=None, scale=1.0, name, tm_cap=512, tn_cap=1408, tk_cap=2816):
    m, k = (a.shape[1], a.shape[0]) if ta else a.shape
    k2, n = (b.shape[1], b.shape[0]) if tb else b.shape
    assert k == k2, (a.shape, b.shape, ta, tb)
    if ta:
        tk_cap = min(tk_cap, 512)
        tm_cap = 1408
    tm, tn, tk = _tile(m, tm_cap), _tile(n, tn_cap), _tile(k, tk_cap)
    nk = k // tk
    dims = (((0 if ta else 1,), (1 if tb else 0,)), ((), ()))
    has_res = res is not None

    def body(*refs):
        if has_res:
            a_ref, b_ref, r_ref, o_ref, acc_ref = refs
        else:
            a_ref, b_ref, o_ref, acc_ref = refs
        kk = pl.program_id(2)

        @pl.when(kk == 0)
        def _():
            acc_ref[...] = jnp.zeros_like(acc_ref)

        acc_ref[...] += lax.dot_general(a_ref[...].astype(BF16), b_ref[...].astype(BF16), dims,
                                        preferred_element_type=F32)

        @pl.when(kk == nk - 1)
        def _():
            out = acc_ref[...]
            if scale != 1.0:
                out = out * scale
            if has_res:
                out = out + r_ref[...].astype(F32)
            o_ref[...] = out.astype(o_ref.dtype)

    a_spec = pl.BlockSpec((tk, tm), lambda i, j, kk: (kk, i)) if ta else pl.BlockSpec((tm, tk), lambda i, j, kk: (i, kk))
    b_spec = pl.BlockSpec((tn, tk), lambda i, j, kk: (j, kk)) if tb else pl.BlockSpec((tk, tn), lambda i, j, kk: (kk, j))
    o_spec = pl.BlockSpec((tm, tn), lambda i, j, kk: (i, j))
    in_specs = [a_spec, b_spec] + ([o_spec] if has_res else [])
    args = (a, b) + ((res,) if has_res else ())
    return pl.pallas_call(
        body, name=name, grid=(m // tm, n // tn, nk), in_specs=in_specs, out_specs=o_spec,
        out_shape=jax.ShapeDtypeStruct((m, n), out_dtype), scratch_shapes=[pltpu.VMEM((tm, tn), F32)],
        compiler_params=_params(("parallel", "parallel", "arbitrary")),
    )(*args)


def _rowwise(fn, rows, consts, outs, accs=(), *, ts=512, name):
    s = rows[0].shape[0]
    ts = min(ts, s)
    assert s % ts == 0
    n_rows, n_consts, n_outs = len(rows), len(consts), len(outs)

    def body(*refs):
        ins = [r[...] for r in refs[:n_rows + n_consts]]
        res = fn(*ins)
        res = tuple(res) if isinstance(res, (tuple, list)) else (res,)
        out_refs = refs[n_rows + n_consts:]
        for o_ref, val in zip(out_refs[:n_outs], res[:n_outs]):
            o_ref[...] = val.astype(o_ref.dtype)
        if accs:
            first = pl.program_id(0) == 0

            @pl.when(first)
            def _():
                for a_ref, val in zip(out_refs[n_outs:], res[n_outs:]):
                    a_ref[...] = val.astype(F32)

            @pl.when(jnp.logical_not(first))
            def _():
                for a_ref, val in zip(out_refs[n_outs:], res[n_outs:]):
                    a_ref[...] += val.astype(F32)

    in_specs = [pl.BlockSpec((ts, r.shape[1]), lambda i: (i, 0)) for r in rows]
    in_specs += [pl.BlockSpec(c.shape, lambda i: (0, 0)) for c in consts]
    out_specs = [pl.BlockSpec((ts, w), lambda i: (i, 0)) for w, _ in outs]
    out_specs += [pl.BlockSpec(tuple(sh), lambda i: (0, 0)) for sh in accs]
    out_shape = [jax.ShapeDtypeStruct((s, w), dt) for w, dt in outs]
    out_shape += [jax.ShapeDtypeStruct(tuple(sh), F32) for sh in accs]
    res = pl.pallas_call(
        body, name=name, grid=(s // ts,), in_specs=in_specs, out_specs=out_specs, out_shape=out_shape,
        compiler_params=_params(("arbitrary",)),
    )(*rows, *consts)
    return res


def _rowwise_bwd(f, rows, consts, cts, *, row_grads, const_grads, adds=None, ts=512, name):
    adds = adds or {}
    n_rows, n_consts, n_cts = len(rows), len(consts), len(cts)
    add_keys = sorted(adds)
    rg = sorted(row_grads)
    cg = sorted(const_grads)

    def fn(*args):
        r = args[:n_rows]
        c = args[n_rows:n_rows + n_consts]
        ct = args[n_rows + n_consts:n_rows + n_consts + n_cts]
        extra = args[n_rows + n_consts + n_cts:]
        outs, vjp = jax.vjp(f, *r, *c)
        outs = tuple(outs) if isinstance(outs, (tuple, list)) else (outs,)
        cot = tuple(g.astype(o.dtype) for g, o in zip(ct, outs))
        grads = vjp(cot if len(cot) > 1 else cot[0])
        res = []
        for i in rg:
            g = grads[i].astype(F32)
            if i in adds:
                g = g + extra[add_keys.index(i)].astype(F32)
            res.append(g)
        for i in cg:
            res.append(grads[n_rows + i])
        return tuple(res)

    rows_all = list(rows) + list(cts) + [adds[i] for i in add_keys]
    def fn2(*args):
        nr = len(rows_all)
        rr, cc = args[:nr], args[nr:]
        return fn(*rr[:n_rows], *cc, *rr[n_rows:])

    outs = [(rows[i].shape[1], row_grads[i]) for i in rg]
    accs = [consts[i].shape for i in cg]
    return _rowwise(fn2, rows_all, list(consts), outs, accs, ts=ts, name=name)


def _rms(x, g):
    xf = x.astype(F32)
    return xf * lax.rsqrt(jnp.mean(xf * xf, axis=-1, keepdims=True) + EPS) * g.astype(F32)


def _sigmoid(x):
    return 1.0 / (1.0 + jnp.exp(-x))


def _f_norm(x, g):
    return _rms(x, g).astype(BF16)


def _f_swiglu(gu):
    gate, up = gu[:, :D_FFP].astype(F32), gu[:, D_FFP:].astype(F32)
    return (gate * _sigmoid(gate) * up).astype(BF16)


def _f_prep1(proj, gq, gkv):
    return _rms(proj[:, :Q_RANK], gq).astype(BF16), _rms(proj[:, Q_RANK:Q_RANK + KV_RANK], gkv).astype(BF16)


def _f_kr(proj):
    return (proj[:, Q_RANK + KV_RANK + SSM_W:],)


def _f_prep2(qall, kv, kr2, cos, sin, gq, gk):
    kr, krs = kr2[:, :LANES].astype(F32), kr2[:, LANES:].astype(F32)
    k_rot = kr * gk[1:2] * cos + krs * gk[2:3] * sin
    k_ss = jnp.sum(kr * kr, axis=-1, keepdims=True)
    q_scale = QK ** -0.5
    qs, ks, vs = [], [], []
    for h in range(H):
        qn = qall[:, h * LANES:(h + 1) * LANES].astype(F32)
        qr = qall[:, (H + h) * LANES:(H + h + 1) * LANES].astype(F32)
        qrs = qall[:, (2 * H + h) * LANES:(2 * H + h + 1) * LANES].astype(F32)
        rstd = lax.rsqrt((jnp.sum(qn * qn, axis=-1, keepdims=True) + jnp.sum(qr * qr, axis=-1, keepdims=True)) / QK + EPS)
        rstd = rstd * q_scale
        qs += [qn * gq[0:1] * rstd, (qr * gq[1:2] * cos + qrs * gq[2:3] * sin) * rstd]
        kn = kv[:, 2 * h * LANES:(2 * h + 1) * LANES].astype(F32)
        rstd_k = lax.rsqrt((jnp.sum(kn * kn, axis=-1, keepdims=True) + k_ss) / QK + EPS)
        ks += [kn * gk[0:1] * rstd_k, k_rot * rstd_k]
        vs.append(kv[:, (2 * h + 1) * LANES:(2 * h + 2) * LANES])
    return (jnp.concatenate(qs, axis=-1).astype(BF16), jnp.concatenate(ks, axis=-1).astype(BF16),
            jnp.concatenate(vs, axis=-1).astype(BF16))


def _gelu(x):
    return 0.5 * x * (1.0 + jnp.tanh(math.sqrt(2.0 / math.pi) * (x + 0.044715 * (x * x * x))))


def _f_s5_gelu(yc, u, d):
    return _gelu(yc.astype(F32) + d * u.astype(F32))


def _f_outnorm(o_mla, g, z, b_glu, g_om, g_os):
    y_ssm = g * _sigmoid(z + b_glu)
    return jnp.concatenate([_rms(o_mla, g_om), _rms(y_ssm, g_os)], axis=-1).astype(BF16)


def _f_memk(kvm, gk):
    ks = [_rms(kvm[:, h * XH:(h + 1) * XH], gk) for h in range(H)]
    return jnp.concatenate(ks, axis=-1).astype(BF16), kvm[:, H * XH:].astype(BF16)


def _f_disc(lr, li, log_dt, br, bi):
    dt = jnp.exp(log_dt)
    decay = jnp.exp(lr * dt)
    ar = decay * jnp.cos(li * dt)
    ai = decay * jnp.sin(li * dt)
    den = lr * lr + li * li
    nr = ar - 1.0
    coef_r = (nr * lr + ai * li) / den
    coef_i = (ai * lr - nr * li) / den
    return ar, ai, coef_r * br - coef_i * bi, coef_r * bi + coef_i * br


def _causal_mask(i, j, tq, tk):
    qpos = i * tq + lax.broadcasted_iota(jnp.int32, (tq, tk), 0)
    kpos = j * tk + lax.broadcasted_iota(jnp.int32, (tq, tk), 1)
    return qpos >= kpos


def _attn_fwd(q, k, v, *, t=512):
    s = q.shape[0]
    t = min(t, s)
    nb = s // t

    def body(q_ref, k_ref, v_ref, o_ref, lse_ref, m_sc, l_sc, acc_sc):
        i, j = pl.program_id(1), pl.program_id(2)

        @pl.when(j == 0)
        def _():
            m_sc[...] = jnp.full_like(m_sc, -jnp.inf)
            l_sc[...] = jnp.zeros_like(l_sc)
            acc_sc[...] = jnp.zeros_like(acc_sc)

        @pl.when(j <= i)
        def _():
            sc = lax.dot_general(q_ref[...], k_ref[...], (((1,), (1,)), ((), ())), preferred_element_type=F32)
            sc = jnp.where(_causal_mask(i, j, t, t), sc, -jnp.inf)
            m_old = m_sc[...]
            m_new = jnp.maximum(m_old, jnp.max(sc, axis=-1, keepdims=True))
            p = jnp.exp(sc - m_new)
            alpha = jnp.exp(m_old - m_new)
            l_sc[...] = alpha * l_sc[...] + jnp.sum(p, axis=-1, keepdims=True)
            acc_sc[...] = alpha * acc_sc[...] + jnp.dot(p.astype(BF16), v_ref[...], preferred_element_type=F32)
            m_sc[...] = m_new

        @pl.when(j == i)
        def _():
            o_ref[...] = acc_sc[...] / l_sc[...]
            lse_ref[...] = jnp.broadcast_to(m_sc[...] + jnp.log(l_sc[...]), lse_ref.shape)

    kv_map = lambda h, i, j: (jnp.minimum(j, i), h)
    return pl.pallas_call(
        body, name="mla_attn_fwd", grid=(H, nb, nb),
        in_specs=[pl.BlockSpec((t, HQ), lambda h, i, j: (i, h)), pl.BlockSpec((t, HQ), kv_map),
                  pl.BlockSpec((t, VD), kv_map)],
        out_specs=[pl.BlockSpec((t, VD), lambda h, i, j: (i, h)), pl.BlockSpec((t, LANES), lambda h, i, j: (i, h))],
        out_shape=[jax.ShapeDtypeStruct((s, H * VD), F32), jax.ShapeDtypeStruct((s, H * LANES), F32)],
        scratch_shapes=[pltpu.VMEM((t, 1), F32), pltpu.VMEM((t, 1), F32), pltpu.VMEM((t, VD), F32)],
        compiler_params=_params(("parallel", "parallel", "arbitrary")),
    )(q, k, v)


def _attn_probs(q_ref, k_ref, v_ref, do_ref, lse_ref, dl_ref, i, j, t):
    sc = lax.dot_general(q_ref[...], k_ref[...], (((1,), (1,)), ((), ())), preferred_element_type=F32)
    p = jnp.where(_causal_mask(i, j, t, t), jnp.exp(sc - lse_ref[...][:, :1]), 0.0)
    dp = lax.dot_general(do_ref[...], v_ref[...], (((1,), (1,)), ((), ())), preferred_element_type=F32)
    ds = p * (dp - dl_ref[...][:, :1])
    return p, ds


def _attn_bwd(q, k, v, do, lse, delta, *, t=512):
    s = q.shape[0]
    t = min(t, s)
    nb = s // t

    def dq_body(q_ref, k_ref, v_ref, do_ref, lse_ref, dl_ref, dq_ref, acc_sc):
        i, j = pl.program_id(1), pl.program_id(2)

        @pl.when(j == 0)
        def _():
            acc_sc[...] = jnp.zeros_like(acc_sc)

        @pl.when(j <= i)
        def _():
            _, ds = _attn_probs(q_ref, k_ref, v_ref, do_ref, lse_ref, dl_ref, i, j, t)
            acc_sc[...] += jnp.dot(ds.astype(BF16), k_ref[...], preferred_element_type=F32)

        @pl.when(j == i)
        def _():
            dq_ref[...] = acc_sc[...]

    q_map = lambda h, i, j: (i, h)
    kv_map = lambda h, i, j: (jnp.minimum(j, i), h)
    dq = pl.pallas_call(
        dq_body, name="mla_attn_dq", grid=(H, nb, nb),
        in_specs=[pl.BlockSpec((t, HQ), q_map), pl.BlockSpec((t, HQ), kv_map), pl.BlockSpec((t, VD), kv_map),
                  pl.BlockSpec((t, VD), q_map), pl.BlockSpec((t, LANES), q_map), pl.BlockSpec((t, LANES), q_map)],
        out_specs=pl.BlockSpec((t, HQ), q_map),
        out_shape=jax.ShapeDtypeStruct((s, H * HQ), F32),
        scratch_shapes=[pltpu.VMEM((t, HQ), F32)],
        compiler_params=_params(("parallel", "parallel", "arbitrary")),
    )(q, k, v, do, lse, delta)

    def dkv_body(q_ref, k_ref, v_ref, do_ref, lse_ref, dl_ref, dk_ref, dv_ref, dk_sc, dv_sc):
        j, i = pl.program_id(1), pl.program_id(2)

        @pl.when(i == 0)
        def _():
            dk_sc[...] = jnp.zeros_like(dk_sc)
            dv_sc[...] = jnp.zeros_like(dv_sc)

        @pl.when(i >= j)
        def _():
            p, ds = _attn_probs(q_ref, k_ref, v_ref, do_ref, lse_ref, dl_ref, i, j, t)
            dv_sc[...] += lax.dot_general(p.astype(BF16), do_ref[...], (((0,), (0,)), ((), ())), preferred_element_type=F32)
            dk_sc[...] += lax.dot_general(ds.astype(BF16), q_ref[...], (((0,), (0,)), ((), ())), preferred_element_type=F32)

        @pl.when(i == nb - 1)
        def _():
            dk_ref[...] = dk_sc[...]
            dv_ref[...] = dv_sc[...]

    q_map2 = lambda h, j, i: (jnp.maximum(i, j), h)
    kv_map2 = lambda h, j, i: (j, h)
    dk, dv = pl.pallas_call(
        dkv_body, name="mla_attn_dkv", grid=(H, nb, nb),
        in_specs=[pl.BlockSpec((t, HQ), q_map2), pl.BlockSpec((t, HQ), kv_map2), pl.BlockSpec((t, VD), kv_map2),
                  pl.BlockSpec((t, VD), q_map2), pl.BlockSpec((t, LANES), q_map2), pl.BlockSpec((t, LANES), q_map2)],
        out_specs=[pl.BlockSpec((t, HQ), kv_map2), pl.BlockSpec((t, VD), kv_map2)],
        out_shape=[jax.ShapeDtypeStruct((s, H * HQ), F32), jax.ShapeDtypeStruct((s, H * VD), F32)],
        scratch_shapes=[pltpu.VMEM((t, HQ), F32), pltpu.VMEM((t, VD), F32)],
        compiler_params=_params(("parallel", "parallel", "arbitrary")),
    )(q, k, v, do, lse, delta)
    return dq, dk, dv


def _f_delta(do, o):
    prod = do.astype(F32) * o.astype(F32)
    parts = [jnp.broadcast_to(jnp.sum(prod[:, h * VD:(h + 1) * VD], axis=-1, keepdims=True), (do.shape[0], LANES))
             for h in range(H)]
    return jnp.concatenate(parts, axis=-1), do.astype(BF16)


def _xattn_head(qh, kh, gq):
    qn = _rms(qh, gq) * (XH ** -0.5)
    sc = lax.dot_general(qn.astype(BF16), kh, (((1,), (1,)), ((), ())), preferred_element_type=F32)
    sc = sc - jnp.max(sc, axis=-1, keepdims=True)
    e = jnp.exp(sc)
    return qn, e / jnp.sum(e, axis=-1, keepdims=True)


def _xattn_fwd(q, kn, v, gq, *, ts=512):
    def fn(qb, knb, vb, g):
        outs = []
        for h in range(H):
            sl = slice(h * XH, (h + 1) * XH)
            _, p = _xattn_head(qb[:, sl], knb[:, sl], g)
            outs.append(jnp.dot(p.astype(BF16), vb[:, sl], preferred_element_type=F32))
        return (jnp.concatenate(outs, axis=-1),)

    return _rowwise(fn, [q], [kn, v, gq], [(H * XH, BF16)], ts=ts, name="xattn_fwd")[0]


def _xattn_bwd(q, kn, v, gq, do, *, ts=512):
    def fn(qb, dob, knb, vb, g):
        dqs, dks, dvs = [], [], []
        dg = jnp.zeros((1, XH), F32)
        for h in range(H):
            sl = slice(h * XH, (h + 1) * XH)
            qh, kh, vh, doh = qb[:, sl], knb[:, sl], vb[:, sl], dob[:, sl].astype(BF16)
            qn, p = _xattn_head(qh, kh, g)
            dp = lax.dot_general(doh, vh, (((1,), (1,)), ((), ())), preferred_element_type=F32)
            dvs.append(lax.dot_general(p.astype(BF16), doh, (((0,), (0,)), ((), ())), preferred_element_type=F32))
            ds = (p * (dp - jnp.sum(dp * p, axis=-1, keepdims=True))).astype(BF16)
            dqn = jnp.dot(ds, kh, preferred_element_type=F32)
            dks.append(lax.dot_general(ds, qn.astype(BF16), (((0,), (0,)), ((), ())), preferred_element_type=F32))
            _, vjp_n = jax.vjp(lambda a, b: _rms(a, b) * (XH ** -0.5), qh, g)
            dqh, dgh = vjp_n(dqn)
            dqs.append(dqh)
            dg = dg + dgh
        return (jnp.concatenate(dqs, axis=-1), jnp.concatenate(dks, axis=-1), jnp.concatenate(dvs, axis=-1), dg)

    return _rowwise(fn, [q, do], [kn, v, gq], [(H * XH, BF16)], [kn.shape, v.shape, gq.shape], ts=ts, name="xattn_bwd")


def _cmul(ar, ai, xr, xi):
    return ar * xr - ai * xi, ar * xi + ai * xr


def _scan(br, bi, ar, ai, *, reverse, cw=256, name):
    s, n = br.shape
    c = SCAN_CHUNKS
    tt = s // c
    cw = min(cw, n)

    def body(br_ref, bi_ref, ar_ref, ai_ref, xr_ref, xi_ref):
        a_r = jnp.broadcast_to(ar_ref[...], (c, cw))
        a_i = jnp.broadcast_to(ai_ref[...], (c, cw))
        zero = jnp.zeros((c, cw), F32)

        def row(step):
            t = (tt - 1 - step) if reverse else step
            return pl.ds(pl.multiple_of(t * c, c), c)

        def local(step, carry):
            sr, si, qr, qi = carry
            r = row(step)
            nr, ni = _cmul(a_r, a_i, sr, si)
            nr, ni = nr + br_ref[r, :], ni + bi_ref[r, :]
            xr_ref[r, :] = nr
            xi_ref[r, :] = ni
            return (nr, ni) + _cmul(a_r, a_i, qr, qi)

        end_r, end_i, pr, pi = lax.fori_loop(0, tt, local, (zero, zero, jnp.ones((c, cw), F32), zero))

        rows_id = lax.broadcasted_iota(jnp.int32, (c, cw), 0)
        car_r, car_i = zero, zero
        cur_r, cur_i = jnp.zeros((1, cw), F32), jnp.zeros((1, cw), F32)
        order = range(c - 1, -1, -1) if reverse else range(c)
        for kk in order:
            car_r = jnp.where(rows_id == kk, cur_r, car_r)
            car_i = jnp.where(rows_id == kk, cur_i, car_i)
            nr, ni = _cmul(pr[0:1], pi[0:1], cur_r, cur_i)
            cur_r = nr + end_r[kk:kk + 1]
            cur_i = ni + end_i[kk:kk + 1]

        def fix(step, carry):
            qr, qi = _cmul(a_r, a_i, *carry)
            r = row(step)
            dr, di = _cmul(qr, qi, car_r, car_i)
            xr_ref[r, :] += dr
            xi_ref[r, :] += di
            return qr, qi

        lax.fori_loop(0, tt, fix, (jnp.ones((c, cw), F32), zero))

    col = lambda j: (0, j)
    return pl.pallas_call(
        body, name=name, grid=(n // cw,),
        in_specs=[pl.BlockSpec((s, cw), col), pl.BlockSpec((s, cw), col), pl.BlockSpec((1, cw), col), pl.BlockSpec((1, cw), col)],
        out_specs=[pl.BlockSpec((s, cw), col), pl.BlockSpec((s, cw), col)],
        out_shape=[jax.ShapeDtypeStruct((s, n), F32), jax.ShapeDtypeStruct((s, n), F32)],
        compiler_params=_params(("parallel",)),
    )(br, bi, ar, ai)


def _scan_da(lr, li, xr, xi, *, cw=256):
    s, n = lr.shape
    c = SCAN_CHUNKS
    tt = s // c
    cw = min(cw, n)

    def body(lr_ref, li_ref, xr_ref, xi_ref, dar_ref, dai_ref):
        def step(t, carry):
            acc_r, acc_i = carry
            r = pl.ds(pl.multiple_of(t * c, c), c)
            rp = pl.ds(pl.multiple_of((t - 1) * c, c), c)
            l_r, l_i, p_r, p_i = lr_ref[r, :], li_ref[r, :], xr_ref[rp, :], xi_ref[rp, :]
            return acc_r + l_r * p_r + l_i * p_i, acc_i + l_i * p_r - l_r * p_i

        zero = jnp.zeros((c, cw), F32)
        acc_r, acc_i = lax.fori_loop(1, tt, step, (zero, zero))
        last = pl.ds((tt - 1) * c, c)
        rows_id = lax.broadcasted_iota(jnp.int32, (c, cw), 0)
        p_r = jnp.where(rows_id == 0, 0.0, pltpu.roll(xr_ref[last, :], 1, 0))
        p_i = jnp.where(rows_id == 0, 0.0, pltpu.roll(xi_ref[last, :], 1, 0))
        first = pl.ds(0, c)
        l_r, l_i = lr_ref[first, :], li_ref[first, :]
        acc_r = acc_r + l_r * p_r + l_i * p_i
        acc_i = acc_i + l_i * p_r - l_r * p_i
        dar_ref[...] = jnp.sum(acc_r, axis=0, keepdims=True)
        dai_ref[...] = jnp.sum(acc_i, axis=0, keepdims=True)

    col = lambda j: (0, j)
    return pl.pallas_call(
        body, name="s5_scan_da", grid=(n // cw,),
        in_specs=[pl.BlockSpec((s, cw), col)] * 4,
        out_specs=[pl.BlockSpec((1, cw), col)] * 2,
        out_shape=[jax.ShapeDtypeStruct((1, n), F32)] * 2,
        compiler_params=_params(("parallel",)),
    )(lr, li, xr, xi)


def _exchange(send, *, gather, name):
    blk = send.shape if gather else send.shape[1:]

    def body(send_ref, recv_ref, send_sems, recv_sems, local_sem):
        x, y, c = lax.axis_index("x"), lax.axis_index("y"), lax.axis_index("c")
        me = 4 * x + 2 * y + c

        def peer(k):
            return x ^ ((k >> 2) & 1), y ^ ((k >> 1) & 1), c ^ (k & 1)

        def copy(k):
            px, py, pc = peer(k)
            pid = 4 * px + 2 * py + pc
            return pltpu.make_async_remote_copy(
                src_ref=send_ref if gather else send_ref.at[pid], dst_ref=recv_ref.at[me],
                send_sem=send_sems.at[k - 1], recv_sem=recv_sems.at[k - 1],
                device_id=(px, py, pc), device_id_type=pl.DeviceIdType.MESH)

        def landing(k):
            px, py, pc = peer(k)
            pid = 4 * px + 2 * py + pc
            return pltpu.make_async_remote_copy(
                src_ref=send_ref if gather else send_ref.at[pid], dst_ref=recv_ref.at[pid],
                send_sem=send_sems.at[k - 1], recv_sem=recv_sems.at[k - 1],
                device_id=(px, py, pc), device_id_type=pl.DeviceIdType.MESH)

        mine = pltpu.make_async_copy(send_ref if gather else send_ref.at[me], recv_ref.at[me], local_sem)
        mine.start()
        sends = [copy(k) for k in range(1, N_DEV)]
        for cp in sends:
            cp.start()
        for k in range(1, N_DEV):
            landing(k).wait_recv()
        for cp in sends:
            cp.wait_send()
        mine.wait()

    hbm = pl.BlockSpec(memory_space=pltpu.HBM)
    return pl.pallas_call(
        body, name=name, in_specs=[hbm], out_specs=hbm,
        out_shape=jax.ShapeDtypeStruct((N_DEV,) + tuple(blk), send.dtype),
        scratch_shapes=[pltpu.SemaphoreType.DMA((N_DEV - 1,)), pltpu.SemaphoreType.DMA((N_DEV - 1,)),
                        pltpu.SemaphoreType.DMA],
        compiler_params=pltpu.CompilerParams(has_side_effects=True),
    )(send)


def _sum_adamw(parts, w, m, v, *, tr=128):
    _, r, cdim = parts.shape
    bc1 = 1.0 - ADAM_B1 ** ADAM_STEP
    bc2 = 1.0 - ADAM_B2 ** ADAM_STEP
    tr = math.gcd(r, tr) if r % tr else tr

    def body(p_ref, w_ref, m_ref, v_ref, g_ref, d_ref, nm_ref, nv_ref):
        g = p_ref[0]
        for i in range(1, N_DEV):
            g = g + p_ref[i]
        nm = ADAM_B1 * m_ref[...] + (1.0 - ADAM_B1) * g
        nv = ADAM_B2 * v_ref[...] + (1.0 - ADAM_B2) * (g * g)
        m_hat = nm / bc1
        v_hat = nv / bc2
        g_ref[...] = g
        d_ref[...] = -ADAM_LR * (m_hat / (jnp.sqrt(v_hat) + ADAM_EPS) + ADAM_WD * w_ref[...])
        nm_ref[...] = nm
        nv_ref[...] = nv

    blk = pl.BlockSpec((tr, cdim), lambda i: (i, 0))
    return pl.pallas_call(
        body, name="sum_adamw", grid=(r // tr,),
        in_specs=[pl.BlockSpec((N_DEV, tr, cdim), lambda i: (0, i, 0)), blk, blk, blk],
        out_specs=[blk] * 4, out_shape=[jax.ShapeDtypeStruct((r, cdim), F32)] * 4,
        compiler_params=_params(("parallel",)),
    )(parts, w, m, v)


def _full(stack, axis):
    if axis == 0:
        return stack.reshape(stack.shape[0] * stack.shape[1], stack.shape[2])
    return jnp.transpose(stack, (1, 0, 2)).reshape(stack.shape[1], stack.shape[0] * stack.shape[2])


def _pad_cols(w, n):
    return jnp.pad(w, ((0, 0), (0, n - w.shape[1])))


def _layout_big(st):
    full = {n: _full(st[n], SHARD_AXIS[n]) for n in SHARDED}
    out = {}
    for f in ('ffn1', 'ffn2'):
        out[f + '_gu'] = jnp.concatenate([_pad_cols(full[f + '_w_gate'], D_FFP), _pad_cols(full[f + '_w_up'], D_FFP)], axis=1)
        out[f + '_down'] = jnp.pad(full[f + '_w_down'], ((0, D_FFP - D_FF), (0, 0)))
    w_in = full['w_in']
    o = Q_RANK + KV_RANK
    kr1, kr2 = w_in[:, o:o + ROPE // 2], w_in[:, o + ROPE // 2:o + ROPE]
    z = jnp.zeros((D, LANES - ROPE), w_in.dtype)
    out['w_in'] = jnp.concatenate([w_in[:, :o], w_in[:, o + ROPE:], kr1, kr2, z, -kr2, kr1, z], axis=1)
    wq = full['mla_w_uq'].reshape(Q_RANK, H, QK)
    zq = jnp.zeros((Q_RANK, H, LANES - ROPE), wq.dtype)
    q1, q2 = wq[:, :, NOPE:NOPE + ROPE // 2], wq[:, :, NOPE + ROPE // 2:]
    out['w_uq'] = jnp.concatenate([wq[:, :, :NOPE].reshape(Q_RANK, H * NOPE),
                                   jnp.concatenate([q1, q2, zq], axis=2).reshape(Q_RANK, H * LANES),
                                   jnp.concatenate([-q2, q1, zq], axis=2).reshape(Q_RANK, H * LANES)], axis=1)
    out['w_ukv'] = full['mla_w_ukv']
    out['w_glu'] = full['ssm_w_glu']
    out['w_o'] = full['w_o']
    out['x_wq'] = full['xattn_w_q']
    out['x_wkv'] = full['xattn_w_kv']
    out['x_wo'] = full['xattn_w_o']
    return out


def _layout_qk_gain(g):
    g = g.reshape(QK)
    g1, g2, z = g[NOPE:NOPE + ROPE // 2], g[NOPE + ROPE // 2:], jnp.zeros((LANES - ROPE,), g.dtype)
    return jnp.stack([g[:NOPE], jnp.concatenate([g1, g2, z]), jnp.concatenate([g2, g1, z])])


def _rep16(a):
    return jnp.repeat(a, SSM_GRP, axis=0)


def _layout_ssm_in(a_re, a_im, log_dt, b_re, b_im):
    b_r = jnp.transpose(b_re, (0, 2, 1)).reshape(SSM_G * SSM_GRP, SSM_P)
    b_i = jnp.transpose(b_im, (0, 2, 1)).reshape(SSM_G * SSM_GRP, SSM_P)
    ldt = jnp.broadcast_to(log_dt.reshape(SSM_G, 1), (SSM_G, SSM_P))
    return _rep16(a_re), _rep16(a_im), _rep16(ldt), b_r, b_i


def _block_diag_b(bb):
    eye = jnp.eye(SSM_G, dtype=bb.dtype)
    return (bb.reshape(SSM_G, SSM_GRP, 1, SSM_P) * eye[:, None, :, None]).reshape(SSM_W, SSM_N)


def _block_diag_c(cc):
    eye = jnp.eye(SSM_G, dtype=cc.dtype)
    return (jnp.transpose(cc, (0, 2, 1))[:, :, None, :] * eye[:, None, :, None]).reshape(SSM_N, SSM_W)


def _time_perm(a, inverse=False):
    s, w = a.shape
    c = SCAN_CHUNKS
    if inverse:
        return jnp.transpose(a.reshape(s // c, c, w), (1, 0, 2)).reshape(s, w)
    return jnp.transpose(a.reshape(c, s // c, w), (1, 0, 2)).reshape(s, w)


def _ffn_fwd(x, g, w_gu, w_down, tag):
    h = _rowwise(_f_norm, [x], [g], [(D, BF16)], name=tag + "_norm")[0]
    gu = _mm(h, w_gu, name=tag + "_gu")
    act = _rowwise(_f_swiglu, [gu], [], [(D_FFP, BF16)], ts=256, name=tag + "_act")[0]
    x_out = _mm(act, w_down, res=x, scale=0.5, name=tag + "_down")
    return x_out, (h, gu, act)


def _ffn_bwd(x, g, w_gu, w_down, saved, dx_out, tag):
    h, gu, act = saved
    dact = _mm(dx_out, w_down, tb=True, scale=0.5, out_dtype=BF16, name=tag + "_dact")
    d_down = _mm(act, dx_out, ta=True, scale=0.5, name=tag + "_dwdown")
    dgu = _rowwise_bwd(_f_swiglu, [gu], [], [dact], row_grads={0: BF16}, const_grads=[], ts=256, name=tag + "_dact_bwd")[0]
    d_gu = _mm(h, dgu, ta=True, name=tag + "_dwgu")
    dh = _mm(dgu, w_gu, tb=True, out_dtype=BF16, name=tag + "_dh")
    dx, dg = _rowwise_bwd(_f_norm, [x], [g], [dh], row_grads={0: F32}, const_grads=[0], adds={0: dx_out}, name=tag + "_norm_bwd")
    return dx, dg, d_gu, d_down


def _local_step(x, mem, cos, sin, target, wb, ws):
    s = x.shape[0]
    gb, gs = {}, {}

    x1, sv1 = _ffn_fwd(x, ws['ffn1_norm'], wb['ffn1_gu'], wb['ffn1_down'], "ffn1")

    h2 = _rowwise(_f_norm, [x1], [ws['mix_norm']], [(D, BF16)], name="mix_norm")[0]
    proj = _mm(h2, wb['w_in'], name="w_in")
    c_q, c_kv = _rowwise(_f_prep1, [proj], [ws['q_norm'], ws['kv_norm']], [(Q_RANK, BF16), (KV_RANK, BF16)], name="mla_prep1")
    qall = _mm(c_q, wb['w_uq'], name="w_uq")
    kv = _mm(c_kv, wb['w_ukv'], name="w_ukv")
    kr = _rowwise(_f_kr, [proj], [], [(2 * LANES, F32)], name="mla_kr")[0]
    q, k, v = _prep2_fwd(qall, kv, kr, cos, sin, ws['qk_gq'], ws['qk_gk'])
    o_mla, lse = _attn_fwd(q, k, v)

    u = proj[:, Q_RANK + KV_RANK:Q_RANK + KV_RANK + SSM_W]
    u_p = _time_perm(u)
    disc_in = [ws['ssm_lr'], ws['ssm_li'], ws['ssm_ldt'], ws['ssm_br'], ws['ssm_bi']]
    ar16, ai16, bbr, bbi = _rowwise(_f_disc, disc_in, [], [(SSM_P, F32)] * 4, name="s5_disc")
    a_r = ar16[::SSM_GRP].reshape(1, SSM_N)
    a_i = ai16[::SSM_GRP].reshape(1, SSM_N)
    bblk_r, bblk_i = _block_diag_b(bbr).astype(BF16), _block_diag_b(bbi).astype(BF16)
    cblk_r, cblk_i = _block_diag_c(ws['ssm_cr']).astype(BF16), _block_diag_c(-ws['ssm_ci']).astype(BF16)
    bu_r = _mm(u_p, bblk_r, name="s5_bu_r", tn_cap=1024)
    bu_i = _mm(u_p, bblk_i, name="s5_bu_i", tn_cap=1024)
    xr, xi = _scan(bu_r, bu_i, a_r, a_i, reverse=False, name="s5_scan_fwd")
    yc = _mm(xr, cblk_r, name="s5_y_r", tk_cap=2048)
    yc = _mm(xi, cblk_i, res=yc, name="s5_y_i", tk_cap=2048)
    g_p = _rowwise(_f_s5_gelu, [yc, u_p], [ws['ssm_d']], [(SSM_W, F32)], name="s5_gelu")[0]
    z_p = _mm(g_p, wb['w_glu'], name="s5_glu")
    g_t, z_t = _time_perm(g_p, inverse=True), _time_perm(z_p, inverse=True)
    on_consts = [ws['ssm_b_glu'], ws['out_norm_mla'], ws['out_norm_ssm']]
    ycat = _rowwise(_f_outnorm, [o_mla, g_t, z_t], on_consts, [(D, BF16)], name="out_norm")[0]
    x2 = _mm(ycat, wb['w_o'], res=x1, name="w_o")

    hx = _rowwise(_f_norm, [x2], [ws['xattn_norm']], [(D, BF16)], name="xattn_norm")[0]
    xq = _mm(hx, wb['x_wq'], name="xattn_q")
    mn = _rowwise(_f_norm, [mem], [ws['mem_norm']], [(D, BF16)], name="mem_norm")[0]
    kvm = _mm(mn, wb['x_wkv'], name="xattn_kv")
    xkn, xv = _rowwise(_f_memk, [kvm], [ws['xattn_k_norm']], [(H * XH, BF16), (H * XH, BF16)], name="xattn_knorm")
    xo = _xattn_fwd(xq, xkn, xv, ws['xattn_q_norm'])
    x3 = _mm(xo, wb['x_wo'], res=x2, name="xattn_o")

    x4, sv2 = _ffn_fwd(x3, ws['ffn2_norm'], wb['ffn2_gu'], wb['ffn2_down'], "ffn2")

    def f_loss(yb, tb):
        err = yb - tb
        return err * (1.0 / D), jnp.broadcast_to(jnp.sum(jnp.sum(err * err, axis=1, keepdims=True), axis=0, keepdims=True) * (0.5 / D), (1, LANES))

    dx4, loss = _rowwise(f_loss, [x4, target], [], [(D, F32)], [(1, LANES)], name="loss")

    dx3, gs['ffn2_norm'], gb['ffn2_gu'], gb['ffn2_down'] = _ffn_bwd(x3, ws['ffn2_norm'], wb['ffn2_gu'], wb['ffn2_down'], sv2, dx4, "ffn2")

    dxo = _mm(dx3, wb['x_wo'], tb=True, out_dtype=BF16, name="xattn_o_dx")
    gb['x_wo'] = _mm(xo, dx3, ta=True, name="xattn_o_dw")
    dxq, dxkn, dxv, gs['xattn_q_norm'] = _xattn_bwd(xq, xkn, xv, ws['xattn_q_norm'], dxo)
    dkvm, gs['xattn_k_norm'] = _rowwise_bwd(_f_memk, [kvm], [ws['xattn_k_norm']], [dxkn, dxv], row_grads={0: BF16},
                                            const_grads=[0], name="xattn_knorm_bwd")
    gb['x_wkv'] = _mm(mn, dkvm, ta=True, name="xattn_kv_dw")
    dmn = _mm(dkvm, wb['x_wkv'], tb=True, out_dtype=BF16, name="xattn_kv_dx")
    gs['mem_norm'] = _rowwise_bwd(_f_norm, [mem], [ws['mem_norm']], [dmn], row_grads={}, const_grads=[0], name="mem_norm_bwd")[0]
    gb['x_wq'] = _mm(hx, dxq, ta=True, name="xattn_q_dw")
    dhx = _mm(dxq, wb['x_wq'], tb=True, out_dtype=BF16, name="xattn_q_dx")
    dx2, gs['xattn_norm'] = _rowwise_bwd(_f_norm, [x2], [ws['xattn_norm']], [dhx], row_grads={0: F32}, const_grads=[0],
                                         adds={0: dx3}, name="xattn_norm_bwd")

    dycat = _mm(dx2, wb['w_o'], tb=True, out_dtype=BF16, name="w_o_dx")
    gb['w_o'] = _mm(ycat, dx2, ta=True, name="w_o_dw")
    do_mla, dg_t, dz_t, gs['ssm_b_glu'], gs['out_norm_mla'], gs['out_norm_ssm'] = _rowwise_bwd(
        _f_outnorm, [o_mla, g_t, z_t], on_consts, [dycat], row_grads={0: F32, 1: F32, 2: BF16}, const_grads=[0, 1, 2],
        name="out_norm_bwd")

    dz_p, dg_p = _time_perm(dz_t), _time_perm(dg_t)
    gb['w_glu'] = _mm(g_p, dz_p, ta=True, name="s5_glu_dw")
    dg_p = _mm(dz_p, wb['w_glu'], tb=True, res=dg_p, name="s5_glu_dx")
    dyc, du_d, gs['ssm_d'] = _rowwise_bwd(_f_s5_gelu, [yc, u_p], [ws['ssm_d']], [dg_p], row_grads={0: BF16, 1: F32},
                                          const_grads=[0], name="s5_gelu_bwd")
    d_cblk_r = _mm(xr, dyc, ta=True, name="s5_dc_r")
    d_cblk_i = _mm(xi, dyc, ta=True, name="s5_dc_i")
    dxr = _mm(dyc, cblk_r, tb=True, name="s5_dx_r", tn_cap=1024)
    dxi = _mm(dyc, cblk_i, tb=True, name="s5_dx_i", tn_cap=1024)
    lam_r, lam_i = _scan(dxr, dxi, a_r, -a_i, reverse=True, name="s5_scan_bwd")
    d_ar, d_ai = _scan_da(lam_r, lam_i, xr, xi)
    d_bblk_r = _mm(u_p, lam_r, ta=True, name="s5_db_r", tn_cap=1024)
    d_bblk_i = _mm(u_p, lam_i, ta=True, name="s5_db_i", tn_cap=1024)
    du_p = _mm(lam_r, bblk_r, tb=True, res=du_d, name="s5_du_r", tk_cap=2048)
    du_p = _mm(lam_i, bblk_i, tb=True, res=du_p, name="s5_du_i", tk_cap=2048)
    du = _time_perm(du_p, inverse=True)
    gs['ssm_cr'] = jax.linear_transpose(_block_diag_c, ws['ssm_cr'])(d_cblk_r)[0]
    gs['ssm_ci'] = -jax.linear_transpose(_block_diag_c, ws['ssm_ci'])(d_cblk_i)[0]
    d_bbr = jax.linear_transpose(_block_diag_b, bbr)(d_bblk_r)[0]
    d_bbi = jax.linear_transpose(_block_diag_b, bbi)(d_bblk_i)[0]
    d_ar16 = jnp.zeros((SSM_G * SSM_GRP, SSM_P), F32).at[::SSM_GRP].set(d_ar.reshape(SSM_G, SSM_P))
    d_ai16 = jnp.zeros((SSM_G * SSM_GRP, SSM_P), F32).at[::SSM_GRP].set(d_ai.reshape(SSM_G, SSM_P))
    gs['ssm_lr'], gs['ssm_li'], gs['ssm_ldt'], gs['ssm_br'], gs['ssm_bi'] = _rowwise_bwd(
        _f_disc, disc_in, [], [d_ar16, d_ai16, d_bbr, d_bbi], row_grads={i: F32 for i in range(5)}, const_grads=[],
        name="s5_disc_bwd")

    delta, do_b = _rowwise(_f_delta, [do_mla, o_mla], [], [(H * LANES, F32), (H * VD, BF16)], name="mla_delta")
    dq, dk, dv = _attn_bwd(q, k, v, do_b, lse, delta)
    dqall, dkv, dkr, gs['qk_gq'], gs['qk_gk'] = _prep2_bwd(qall, kv, kr, cos, sin, ws['qk_gq'], ws['qk_gk'], dq, dk, dv)
    gb['w_uq'] = _mm(c_q, dqall, ta=True, name="w_uq_dw")
    dc_q = _mm(dqall, wb['w_uq'], tb=True, out_dtype=BF16, name="w_uq_dx")
    gb['w_ukv'] = _mm(c_kv, dkv, ta=True, name="w_ukv_dw")
    dc_kv = _mm(dkv, wb['w_ukv'], tb=True, out_dtype=BF16, name="w_ukv_dx")

    def f_prep1_bwd(pb, dcq, dckv, dub, dkrb, gq, gkv):
        _, vjp = jax.vjp(_f_prep1, pb[:, :Q_RANK + KV_RANK], gq, gkv)
        dpa, dgq, dgkv = vjp((dcq.astype(BF16), dckv.astype(BF16)))
        return jnp.concatenate([dpa, dub, dkrb], axis=-1), dgq, dgkv

    dproj, gs['q_norm'], gs['kv_norm'] = _rowwise(
        f_prep1_bwd, [proj, dc_q, dc_kv, du, dkr], [ws['q_norm'], ws['kv_norm']], [(IN_WP, BF16)],
        [(1, Q_RANK), (1, KV_RANK)], name="mla_prep1_bwd")
    gb['w_in'] = _mm(h2, dproj, ta=True, name="w_in_dw")
    dh2 = _mm(dproj, wb['w_in'], tb=True, out_dtype=BF16, name="w_in_dx")
    dx1, gs['mix_norm'] = _rowwise_bwd(_f_norm, [x1], [ws['mix_norm']], [dh2], row_grads={0: F32}, const_grads=[0],
                                       adds={0: dx2}, name="mix_norm_bwd")

    dx0, gs['ffn1_norm'], gb['ffn1_gu'], gb['ffn1_down'] = _ffn_bwd(x, ws['ffn1_norm'], wb['ffn1_gu'], wb['ffn1_down'], sv1, dx1, "ffn1")
    return loss, dx0, gb, gs


def _prep2_fwd(qall, kv, kr, cos, sin, gq, gk):
    return _rowwise(_f_prep2, [qall, kv, kr, cos, sin], [gq, gk], [(H * HQ, BF16), (H * HQ, BF16), (H * VD, BF16)],
                    ts=256, name="mla_prep2")


def _prep2_bwd(qall, kv, kr, cos, sin, gq, gk, dq, dk, dv):
    return _rowwise_bwd(_f_prep2, [qall, kv, kr, cos, sin], [gq, gk], [dq, dk, dv], row_grads={0: BF16, 1: BF16, 2: F32},
                        const_grads=[0, 1], ts=256, name="mla_prep2_bwd")


def _rope_tables(pos):
    half = ROPE // 2
    inv = ROPE_THETA ** (-jnp.arange(half, dtype=F32) / half)
    ang = pos.astype(F32)[:, None] * inv[None, :]
    z = jnp.zeros((pos.shape[0], LANES - ROPE), F32)
    cos, sin = jnp.cos(ang), jnp.sin(ang)
    return jnp.concatenate([cos, cos, z], axis=-1), jnp.concatenate([sin, sin, z], axis=-1)


def _small_layout(p):
    lr, li, ldt, br, bi = _layout_ssm_in(p['ssm_a_re'], p['ssm_a_im'], p['ssm_log_dt'], p['ssm_b_re'], p['ssm_b_im'])
    return {
        'ffn1_norm': p['ffn1_norm'].reshape(1, D), 'mix_norm': p['mix_norm'].reshape(1, D),
        'q_norm': p['mla_q_norm'].reshape(1, Q_RANK), 'kv_norm': p['mla_kv_norm'].reshape(1, KV_RANK),
        'qk_gq': _layout_qk_gain(p['mla_qk_norm_q']), 'qk_gk': _layout_qk_gain(p['mla_qk_norm_k']),
        'ssm_lr': lr, 'ssm_li': li, 'ssm_ldt': ldt, 'ssm_br': br, 'ssm_bi': bi,
        'ssm_cr': p['ssm_c_re'], 'ssm_ci': p['ssm_c_im'], 'ssm_d': p['ssm_d'].reshape(1, SSM_W),
        'ssm_b_glu': p['ssm_b_glu'].reshape(1, SSM_W),
        'out_norm_mla': p['out_norm_mla'].reshape(1, SSM_W), 'out_norm_ssm': p['out_norm_ssm'].reshape(1, SSM_W),
        'xattn_norm': p['xattn_norm'].reshape(1, D), 'mem_norm': p['mem_norm'].reshape(1, D),
        'xattn_q_norm': p['xattn_q_norm'].reshape(1, XH), 'xattn_k_norm': p['xattn_k_norm'].reshape(1, XH),
        'ffn2_norm': p['ffn2_norm'].reshape(1, D),
    }


def _flatten(arrs, rows):
    flat = jnp.concatenate([a.reshape(-1) for a in arrs])
    return jnp.pad(flat, (0, rows * D - flat.shape[0])).reshape(rows, D)


def _unflatten(flat, shapes):
    flat = flat.reshape(-1)
    out, off = [], 0
    for sh in shapes:
        n = int(np.prod(sh))
        out.append(flat[off:off + n].reshape(sh))
        off += n
    return out


def kernel(x, mem, positions, ffn1_norm, ffn1_w_gate, ffn1_w_up, ffn1_w_down, mix_norm, w_in, mla_q_norm, mla_w_uq, mla_kv_norm, mla_w_ukv, mla_qk_norm_q, mla_qk_norm_k, ssm_a_re, ssm_a_im, ssm_log_dt, ssm_b_re, ssm_b_im, ssm_c_re, ssm_c_im, ssm_d, ssm_w_glu, ssm_b_glu, out_norm_mla, out_norm_ssm, w_o, xattn_norm, mem_norm, xattn_w_q, xattn_w_kv, xattn_q_norm, xattn_k_norm, xattn_w_o, ffn2_norm, ffn2_w_gate, ffn2_w_up, ffn2_w_down, loss_target, m_ffn1_norm, m_ffn1_w_gate, m_ffn1_w_up, m_ffn1_w_down, m_mix_norm, m_w_in, m_mla_q_norm, m_mla_w_uq, m_mla_kv_norm, m_mla_w_ukv, m_mla_qk_norm_q, m_mla_qk_norm_k, m_ssm_a_re, m_ssm_a_im, m_ssm_log_dt, m_ssm_b_re, m_ssm_b_im, m_ssm_c_re, m_ssm_c_im, m_ssm_d, m_ssm_w_glu, m_ssm_b_glu, m_out_norm_mla, m_out_norm_ssm, m_w_o, m_xattn_norm, m_mem_norm, m_xattn_w_q, m_xattn_w_kv, m_xattn_q_norm, m_xattn_k_norm, m_xattn_w_o, m_ffn2_norm, m_ffn2_w_gate, m_ffn2_w_up, m_ffn2_w_down, v_ffn1_norm, v_ffn1_w_gate, v_ffn1_w_up, v_ffn1_w_down, v_mix_norm, v_w_in, v_mla_q_norm, v_mla_w_uq, v_mla_kv_norm, v_mla_w_ukv, v_mla_qk_norm_q, v_mla_qk_norm_k, v_ssm_a_re, v_ssm_a_im, v_ssm_log_dt, v_ssm_b_re, v_ssm_b_im, v_ssm_c_re, v_ssm_c_im, v_ssm_d, v_ssm_w_glu, v_ssm_b_glu, v_out_norm_mla, v_out_norm_ssm, v_w_o, v_xattn_norm, v_mem_norm, v_xattn_w_q, v_xattn_w_kv, v_xattn_q_norm, v_xattn_k_norm, v_xattn_w_o, v_ffn2_norm, v_ffn2_w_gate, v_ffn2_w_up, v_ffn2_w_down):
    args = dict(locals())
    w = {n: args[n] for n in WEIGHTS}
    mom = {n: args['m_' + n] for n in WEIGHTS}
    var = {n: args['v_' + n] for n in WEIGHTS}
    return _step(x, mem, positions, loss_target, w, mom, var)


def _step(x, mem, positions, loss_target, w, mom, var):
    shard_shapes = [w[n].shape for n in SHARDED]
    small_shapes = [w[n].shape for n in SMALL]
    n_big = sum(int(np.prod(sh)) for sh in shard_shapes)
    assert n_big % D == 0
    rows_big = n_big // D
    n_small = sum(int(np.prod(sh)) for sh in small_shapes) + 1
    rows_small = -(-n_small // D)
    rows_small += (-(rows_big + rows_small)) % 8

    mine = _flatten([w[n].astype(BF16) for n in SHARDED], rows_big)
    gathered = _exchange(mine, gather=True, name="gather_weights")
    stacks = dict(zip(SHARDED, _unflatten_stack(gathered, [sh[1:] for sh in shard_shapes])))
    wb = _layout_big(stacks)

    small = {n: w[n][0] for n in SMALL}
    ws = _small_layout(small)
    cos, sin = _rope_tables(positions[0])
    loss, dx, gb, gs = _local_step(x[0], mem[0], cos, sin, loss_target[0], wb, ws)

    stacks_f32 = {n: jax.ShapeDtypeStruct((N_DEV,) + tuple(sh[1:]), F32) for n, sh in zip(SHARDED, shard_shapes)}
    g_stacks = jax.linear_transpose(_layout_big, stacks_f32)(gb)[0]
    g_small = jax.linear_transpose(_small_layout, {n: jax.ShapeDtypeStruct(small[n].shape, F32) for n in SMALL})(gs)[0]
    big = jnp.concatenate([g_stacks[n].reshape(N_DEV, -1) for n in SHARDED], axis=1)
    small_flat = jnp.concatenate([g_small[n].reshape(-1) for n in SMALL] + [loss[0, :1]])
    small_flat = jnp.pad(small_flat, (0, rows_small * D - small_flat.shape[0]))
    send = jnp.concatenate([big, jnp.broadcast_to(small_flat[None], (N_DEV, rows_small * D))], axis=1)
    send = send.reshape(N_DEV, rows_big + rows_small, D)
    parts = _exchange(send, gather=False, name="scatter_grads")

    def flat_state(t):
        return _flatten([t[n] for n in SHARDED] + [t[n] for n in SMALL], rows_big + rows_small)

    g_flat, d_flat, nm_flat, nv_flat = _sum_adamw(parts, flat_state(w), flat_state(mom), flat_state(var))
    shapes = shard_shapes + small_shapes
    order = SHARDED + SMALL
    outs = []
    for flat in (g_flat, d_flat, nm_flat, nv_flat):
        by_name = dict(zip(order, _unflatten(flat, shapes)))
        outs += [by_name[n] for n in WEIGHTS]
    loss_total = g_flat.reshape(-1)[n_big + n_small - 1]
    return (loss_total, dx[None], *outs)


def _unflatten_stack(gathered, shapes):
    flat = gathered.reshape(N_DEV, -1)
    out, off = [], 0
    for sh in shapes:
        n = int(np.prod(sh))
        out.append(flat[:, off:off + n].reshape((N_DEV,) + tuple(sh)))
        off += n
    return out
```

```python
import functools
import math

import jax
import jax.numpy as jnp
import numpy as np
from jax import lax
from jax.experimental import pallas as pl
from jax.experimental.pallas import tpu as pltpu

F32 = jnp.float32
BF16 = jnp.bfloat16

N_DEV = 8
D = 1024
D_FF = 2752
D_FFP = 2816
MEM_LEN = 256
H = 4
Q_RANK, KV_RANK, NOPE, ROPE, VD = 384, 256, 128, 64, 128
QK = NOPE + ROPE
HQ = 2 * 128
SSM_W, SSM_G, SSM_GRP, SSM_P = 512, 32, 16, 64
SSM_N = SSM_G * SSM_P
IN_W = 1216
IN_WP = 1408
XH = 128
EPS = 1e-6
ROPE_THETA = 10000.0
SCAN_CHUNKS = 8

ADAM_LR, ADAM_B1, ADAM_B2, ADAM_EPS, ADAM_WD, ADAM_STEP = 0.001, 0.9, 0.999, 1e-08, 0.01, 10

VMEM_LIMIT = 56 * 1024 * 1024
ACC_BYTES = 6 * 1024 * 1024
LANES = 128
BF16_ROWS = 16
FF_SHARD = D_FF // N_DEV
FF_SHARD_P = 352
IN_SHARD = IN_W // N_DEV
IN_SHARD_P = 160

WEIGHTS = ['ffn1_norm', 'ffn1_w_gate', 'ffn1_w_up', 'ffn1_w_down', 'mix_norm', 'w_in', 'mla_q_norm', 'mla_w_uq',
           'mla_kv_norm', 'mla_w_ukv', 'mla_qk_norm_q', 'mla_qk_norm_k', 'ssm_a_re', 'ssm_a_im', 'ssm_log_dt',
           'ssm_b_re', 'ssm_b_im', 'ssm_c_re', 'ssm_c_im', 'ssm_d', 'ssm_w_glu', 'ssm_b_glu', 'out_norm_mla',
           'out_norm_ssm', 'w_o', 'xattn_norm', 'mem_norm', 'xattn_w_q', 'xattn_w_kv', 'xattn_q_norm',
           'xattn_k_norm', 'xattn_w_o', 'ffn2_norm', 'ffn2_w_gate', 'ffn2_w_up', 'ffn2_w_down']
SHARD_AXIS = {'ffn1_w_gate': 1, 'ffn1_w_up': 1, 'ffn1_w_down': 0, 'w_in': 1, 'mla_w_uq': 1, 'mla_w_ukv': 1,
              'ssm_w_glu': 0, 'w_o': 0, 'xattn_w_q': 0, 'xattn_w_kv': 0, 'xattn_w_o': 1,
              'ffn2_w_gate': 1, 'ffn2_w_up': 1, 'ffn2_w_down': 0}
SHARDED = [n for n in WEIGHTS if n in SHARD_AXIS]
SMALL = [n for n in WEIGHTS if n not in SHARD_AXIS]


def _params(sem=None):
    return pltpu.CompilerParams(dimension_semantics=sem, vmem_limit_bytes=VMEM_LIMIT)


def _tile(n, cap):
    if n <= cap:
        return n
    best = n
    for t in range(LANES, cap + 1, LANES):
        if n % t == 0:
            best = t
    return best


def _mm(a, b, *, ta=False, tb=False, out_dtype=F32, res=None, scale=1.0, name, tm_cap=512, tn_cap=1408, tk_cap=2816):
    m, k = (a.shape[1], a.shape[0]) if ta else a.shape
    k2, n = (b.shape[1], b.shape[0]) if tb else b.shape
    assert k == k2, (a.shape, b.shape, ta, tb)
    if ta:
        tk_cap = min(tk_cap, 512)
        tm_cap = 1408
    tm, tn, tk = _tile(m, tm_cap), _tile(n, tn_cap), _tile(k, tk_cap)
    if tm * tn * 4 > ACC_BYTES:
        tn = _tile(n, max(LANES, ACC_BYTES // (4 * tm) // LANES * LANES))
    nk = k // tk
    dims = (((0 if ta else 1,), (1 if tb else 0,)), ((), ()))
    has_res = res is not None

    def body(*refs):
        if has_res:
            a_ref, b_ref, r_ref, o_ref, acc_ref = refs
        else:
            a_ref, b_ref, o_ref, acc_ref = refs
        kk = pl.program_id(2)

        @pl.when(kk == 0)
        def _():
            acc_ref[...] = jnp.zeros_like(acc_ref)

        acc_ref[...] += lax.dot_general(a_ref[...].astype(BF16), b_ref[...].astype(BF16), dims,
                                        preferred_element_type=F32)

        @pl.when(kk == nk - 1)
        def _():
            out = acc_ref[...]
            if scale != 1.0:
                out = out * scale
            if has_res:
                out = out + r_ref[...].astype(F32)
            o_ref[...] = out.astype(o_ref.dtype)

    a_spec = pl.BlockSpec((tk, tm), lambda i, j, kk: (kk, i)) if ta else pl.BlockSpec((tm, tk), lambda i, j, kk: (i, kk))
    b_spec = pl.BlockSpec((tn, tk), lambda i, j, kk: (j, kk)) if tb else pl.BlockSpec((tk, tn), lambda i, j, kk: (kk, j))
    o_spec = pl.BlockSpec((tm, tn), lambda i, j, kk: (i, j))
    in_specs = [a_spec, b_spec] + ([o_spec] if has_res else [])
    args = (a, b) + ((res,) if has_res else ())
    return pl.pallas_call(
        body, name=name, grid=(m // tm, n // tn, nk), in_specs=in_specs, out_specs=o_spec,
        out_shape=jax.ShapeDtypeStruct((m, n), out_dtype), scratch_shapes=[pltpu.VMEM((tm, tn), F32)],
        compiler_params=_params(("parallel", "parallel", "arbitrary")),
    )(*args)


def _rowwise(fn, rows, consts, outs, accs=(), *, ts=512, name):
    s = rows[0].shape[0]
    ts = min(ts, s)
    assert s % ts == 0
    n_rows, n_consts, n_outs = len(rows), len(consts), len(outs)

    def body(*refs):
        ins = [r[...] for r in refs[:n_rows + n_consts]]
        res = fn(*ins)
        res = tuple(res) if isinstance(res, (tuple, list)) else (res,)
        out_refs = refs[n_rows + n_consts:]
        for o_ref, val in zip(out_refs[:n_outs], res[:n_outs]):
            o_ref[...] = val.astype(o_ref.dtype)
        if accs:
            first = pl.program_id(0) == 0

            @pl.when(first)
            def _():
                for a_ref, val in zip(out_refs[n_outs:], res[n_outs:]):
                    a_ref[...] = val.astype(F32)

            @pl.when(jnp.logical_not(first))
            def _():
                for a_ref, val in zip(out_refs[n_outs:], res[n_outs:]):
                    a_ref[...] += val.astype(F32)

    in_specs = [pl.BlockSpec((ts, r.shape[1]), lambda i: (i, 0)) for r in rows]
    in_specs += [pl.BlockSpec(c.shape, lambda i: (0, 0)) for c in consts]
    out_specs = [pl.BlockSpec((ts, w), lambda i: (i, 0)) for w, _ in outs]
    out_specs += [pl.BlockSpec(tuple(sh), lambda i: (0, 0)) for sh in accs]
    out_shape = [jax.ShapeDtypeStruct((s, w), dt) for w, dt in outs]
    out_shape += [jax.ShapeDtypeStruct(tuple(sh), F32) for sh in accs]
    res = pl.pallas_call(
        body, name=name, grid=(s // ts,), in_specs=in_specs, out_specs=out_specs, out_shape=out_shape,
        compiler_params=_params(("arbitrary",)),
    )(*rows, *consts)
    return res


def _rowwise_bwd(f, rows, consts, cts, *, row_grads, const_grads, adds=None, ts=512, name):
    adds = adds or {}
    n_rows, n_consts, n_cts = len(rows), len(consts), len(cts)
    add_keys = sorted(adds)
    rg = sorted(row_grads)
    cg = sorted(const_grads)

    def fn(*args):
        r = args[:n_rows]
        c = args[n_rows:n_rows + n_consts]
        ct = args[n_rows + n_consts:n_rows + n_consts + n_cts]
        extra = args[n_rows + n_consts + n_cts:]
        outs, vjp = jax.vjp(f, *r, *c)
        outs = tuple(outs) if isinstance(outs, (tuple, list)) else (outs,)
        cot = tuple(g.astype(o.dtype) for g, o in zip(ct, outs))
        grads = vjp(cot if len(cot) > 1 else cot[0])
        res = []
        for i in rg:
            g = grads[i].astype(F32)
            if i in adds:
                g = g + extra[add_keys.index(i)].astype(F32)
            res.append(g)
        for i in cg:
            res.append(grads[n_rows + i])
        return tuple(res)

    rows_all = list(rows) + list(cts) + [adds[i] for i in add_keys]
    def fn2(*args):
        nr = len(rows_all)
        rr, cc = args[:nr], args[nr:]
        return fn(*rr[:n_rows], *cc, *rr[n_rows:])

    outs = [(rows[i].shape[1], row_grads[i]) for i in rg]
    accs = [consts[i].shape for i in cg]
    return _rowwise(fn2, rows_all, list(consts), outs, accs, ts=ts, name=name)


def _rms(x, g):
    xf = x.astype(F32)
    return xf * lax.rsqrt(jnp.mean(xf * xf, axis=-1, keepdims=True) + EPS) * g.astype(F32)


def _sigmoid(x):
    return 1.0 / (1.0 + jnp.exp(-x))


def _f_norm(x, g):
    return _rms(x, g).astype(BF16)


def _f_swiglu(gate, up):
    gate, up = gate.astype(F32), up.astype(F32)
    return (gate * _sigmoid(gate) * up).astype(BF16)


def _f_prep1(proj, gq, gkv):
    return _rms(proj[:, :Q_RANK], gq).astype(BF16), _rms(proj[:, Q_RANK:Q_RANK + KV_RANK], gkv).astype(BF16)


def _f_kr(proj):
    return (proj[:, Q_RANK + KV_RANK + SSM_W:],)


def _f_prep2(qall, kv, kr2, cos, sin, gq, gk):
    kr, krs = kr2[:, :LANES].astype(F32), kr2[:, LANES:].astype(F32)
    k_rot = kr * gk[1:2] * cos + krs * gk[2:3] * sin
    k_ss = jnp.sum(kr * kr, axis=-1, keepdims=True)
    q_scale = QK ** -0.5
    qs, ks, vs = [], [], []
    for h in range(H):
        qn = qall[:, h * LANES:(h + 1) * LANES].astype(F32)
        qr = qall[:, (H + h) * LANES:(H + h + 1) * LANES].astype(F32)
        qrs = qall[:, (2 * H + h) * LANES:(2 * H + h + 1) * LANES].astype(F32)
        rstd = lax.rsqrt((jnp.sum(qn * qn, axis=-1, keepdims=True) + jnp.sum(qr * qr, axis=-1, keepdims=True)) / QK + EPS)
        rstd = rstd * q_scale
        qs += [qn * gq[0:1] * rstd, (qr * gq[1:2] * cos + qrs * gq[2:3] * sin) * rstd]
        kn = kv[:, 2 * h * LANES:(2 * h + 1) * LANES].astype(F32)
        rstd_k = lax.rsqrt((jnp.sum(kn * kn, axis=-1, keepdims=True) + k_ss) / QK + EPS)
        ks += [kn * gk[0:1] * rstd_k, k_rot * rstd_k]
        vs.append(kv[:, (2 * h + 1) * LANES:(2 * h + 2) * LANES])
    return (jnp.concatenate(qs, axis=-1).astype(BF16), jnp.concatenate(ks, axis=-1).astype(BF16),
            jnp.concatenate(vs, axis=-1).astype(BF16))


def _gelu(x):
    return 0.5 * x * (1.0 + jnp.tanh(math.sqrt(2.0 / math.pi) * (x + 0.044715 * (x * x * x))))


def _f_s5_gelu(yc, u, d):
    return _gelu(yc.astype(F32) + d * u.astype(F32))


def _f_outnorm(o_mla, g, z, b_glu, g_om, g_os):
    y_ssm = g * _sigmoid(z + b_glu)
    return jnp.concatenate([_rms(o_mla, g_om), _rms(y_ssm, g_os)], axis=-1).astype(BF16)


def _f_memk(kvm, gk):
    ks = [_rms(kvm[:, h * XH:(h + 1) * XH], gk) for h in range(H)]
    return jnp.concatenate(ks, axis=-1).astype(BF16), kvm[:, H * XH:].astype(BF16)


def _f_disc(lr, li, log_dt, br, bi):
    dt = jnp.exp(log_dt)
    decay = jnp.exp(lr * dt)
    ar = decay * jnp.cos(li * dt)
    ai = decay * jnp.sin(li * dt)
    den = lr * lr + li * li
    nr = ar - 1.0
    coef_r = (nr * lr + ai * li) / den
    coef_i = (ai * lr - nr * li) / den
    return ar, ai, coef_r * br - coef_i * bi, coef_r * bi + coef_i * br


def _causal_mask(i, j, tq, tk):
    qpos = i * tq + lax.broadcasted_iota(jnp.int32, (tq, tk), 0)
    kpos = j * tk + lax.broadcasted_iota(jnp.int32, (tq, tk), 1)
    return qpos >= kpos


def _attn_fwd(q, k, v, *, t=512):
    s = q.shape[0]
    t = min(t, s)
    nb = s // t

    def body(q_ref, k_ref, v_ref, o_ref, lse_ref, m_sc, l_sc, acc_sc):
        i, j = pl.program_id(1), pl.program_id(2)

        @pl.when(j == 0)
        def _():
            m_sc[...] = jnp.full_like(m_sc, -jnp.inf)
            l_sc[...] = jnp.zeros_like(l_sc)
            acc_sc[...] = jnp.zeros_like(acc_sc)

        @pl.when(j <= i)
        def _():
            sc = lax.dot_general(q_ref[...], k_ref[...], (((1,), (1,)), ((), ())), preferred_element_type=F32)
            sc = jnp.where(_causal_mask(i, j, t, t), sc, -jnp.inf)
            m_old = m_sc[...]
            m_new = jnp.maximum(m_old, jnp.max(sc, axis=-1, keepdims=True))
            p = jnp.exp(sc - m_new)
            alpha = jnp.exp(m_old - m_new)
            l_sc[...] = alpha * l_sc[...] + jnp.sum(p, axis=-1, keepdims=True)
            acc_sc[...] = alpha * acc_sc[...] + jnp.dot(p.astype(BF16), v_ref[...], preferred_element_type=F32)
            m_sc[...] = m_new

        @pl.when(j == i)
        def _():
            o_ref[...] = acc_sc[...] / l_sc[...]
            lse_ref[...] = jnp.broadcast_to(m_sc[...] + jnp.log(l_sc[...]), lse_ref.shape)

    kv_map = lambda h, i, j: (jnp.minimum(j, i), h)
    return pl.pallas_call(
        body, name="mla_attn_fwd", grid=(H, nb, nb),
        in_specs=[pl.BlockSpec((t, HQ), lambda h, i, j: (i, h)), pl.BlockSpec((t, HQ), kv_map),
                  pl.BlockSpec((t, VD), kv_map)],
        out_specs=[pl.BlockSpec((t, VD), lambda h, i, j: (i, h)), pl.BlockSpec((t, LANES), lambda h, i, j: (i, h))],
        out_shape=[jax.ShapeDtypeStruct((s, H * VD), F32), jax.ShapeDtypeStruct((s, H * LANES), F32)],
        scratch_shapes=[pltpu.VMEM((t, 1), F32), pltpu.VMEM((t, 1), F32), pltpu.VMEM((t, VD), F32)],
        compiler_params=_params(("parallel", "parallel", "arbitrary")),
    )(q, k, v)


def _attn_probs(q_ref, k_ref, v_ref, do_ref, lse_ref, dl_ref, i, j, t):
    sc = lax.dot_general(q_ref[...], k_ref[...], (((1,), (1,)), ((), ())), preferred_element_type=F32)
    p = jnp.where(_causal_mask(i, j, t, t), jnp.exp(sc - lse_ref[...][:, :1]), 0.0)
    dp = lax.dot_general(do_ref[...], v_ref[...], (((1,), (1,)), ((), ())), preferred_element_type=F32)
    ds = p * (dp - dl_ref[...][:, :1])
    return p, ds


def _attn_bwd(q, k, v, do, lse, delta, *, t=512):
    s = q.shape[0]
    t = min(t, s)
    nb = s // t

    def dq_body(q_ref, k_ref, v_ref, do_ref, lse_ref, dl_ref, dq_ref, acc_sc):
        i, j = pl.program_id(1), pl.program_id(2)

        @pl.when(j == 0)
        def _():
            acc_sc[...] = jnp.zeros_like(acc_sc)

        @pl.when(j <= i)
        def _():
            _, ds = _attn_probs(q_ref, k_ref, v_ref, do_ref, lse_ref, dl_ref, i, j, t)
            acc_sc[...] += jnp.dot(ds.astype(BF16), k_ref[...], preferred_element_type=F32)

        @pl.when(j == i)
        def _():
            dq_ref[...] = acc_sc[...]

    q_map = lambda h, i, j: (i, h)
    kv_map = lambda h, i, j: (jnp.minimum(j, i), h)
    dq = pl.pallas_call(
        dq_body, name="mla_attn_dq", grid=(H, nb, nb),
        in_specs=[pl.BlockSpec((t, HQ), q_map), pl.BlockSpec((t, HQ), kv_map), pl.BlockSpec((t, VD), kv_map),
                  pl.BlockSpec((t, VD), q_map), pl.BlockSpec((t, LANES), q_map), pl.BlockSpec((t, LANES), q_map)],
        out_specs=pl.BlockSpec((t, HQ), q_map),
        out_shape=jax.ShapeDtypeStruct((s, H * HQ), F32),
        scratch_shapes=[pltpu.VMEM((t, HQ), F32)],
        compiler_params=_params(("parallel", "parallel", "arbitrary")),
    )(q, k, v, do, lse, delta)

    def dkv_body(q_ref, k_ref, v_ref, do_ref, lse_ref, dl_ref, dk_ref, dv_ref, dk_sc, dv_sc):
        j, i = pl.program_id(1), pl.program_id(2)

        @pl.when(i == 0)
        def _():
            dk_sc[...] = jnp.zeros_like(dk_sc)
            dv_sc[...] = jnp.zeros_like(dv_sc)

        @pl.when(i >= j)
        def _():
            p, ds = _attn_probs(q_ref, k_ref, v_ref, do_ref, lse_ref, dl_ref, i, j, t)
            dv_sc[...] += lax.dot_general(p.astype(BF16), do_ref[...], (((0,), (0,)), ((), ())), preferred_element_type=F32)
            dk_sc[...] += lax.dot_general(ds.astype(BF16), q_ref[...], (((0,), (0,)), ((), ())), preferred_element_type=F32)

        @pl.when(i == nb - 1)
        def _():
            dk_ref[...] = dk_sc[...]
            dv_ref[...] = dv_sc[...]

    q_map2 = lambda h, j, i: (jnp.maximum(i, j), h)
    kv_map2 = lambda h, j, i: (j, h)
    dk, dv = pl.pallas_call(
        dkv_body, name="mla_attn_dkv", grid=(H, nb, nb),
        in_specs=[pl.BlockSpec((t, HQ), q_map2), pl.BlockSpec((t, HQ), kv_map2), pl.BlockSpec((t, VD), kv_map2),
                  pl.BlockSpec((t, VD), q_map2), pl.BlockSpec((t, LANES), q_map2), pl.BlockSpec((t, LANES), q_map2)],
        out_specs=[pl.BlockSpec((t, HQ), kv_map2), pl.BlockSpec((t, VD), kv_map2)],
        out_shape=[jax.ShapeDtypeStruct((s, H * HQ), F32), jax.ShapeDtypeStruct((s, H * VD), F32)],
        scratch_shapes=[pltpu.VMEM((t, HQ), F32), pltpu.VMEM((t, VD), F32)],
        compiler_params=_params(("parallel", "parallel", "arbitrary")),
    )(q, k, v, do, lse, delta)
    return dq, dk, dv


def _f_delta(do, o):
    prod = do.astype(F32) * o.astype(F32)
    parts = [jnp.broadcast_to(jnp.sum(prod[:, h * VD:(h + 1) * VD], axis=-1, keepdims=True), (do.shape[0], LANES))
             for h in range(H)]
    return jnp.concatenate(parts, axis=-1), do.astype(BF16)


def _xattn_head(qh, kh, gq):
    qn = _rms(qh, gq) * (XH ** -0.5)
    sc = lax.dot_general(qn.astype(BF16), kh, (((1,), (1,)), ((), ())), preferred_element_type=F32)
    sc = sc - jnp.max(sc, axis=-1, keepdims=True)
    e = jnp.exp(sc)
    return qn, e / jnp.sum(e, axis=-1, keepdims=True)


def _xattn_fwd(q, kn, v, gq, *, ts=512):
    def fn(qb, knb, vb, g):
        outs = []
        for h in range(H):
            sl = slice(h * XH, (h + 1) * XH)
            _, p = _xattn_head(qb[:, sl], knb[:, sl], g)
            outs.append(jnp.dot(p.astype(BF16), vb[:, sl], preferred_element_type=F32))
        return (jnp.concatenate(outs, axis=-1),)

    return _rowwise(fn, [q], [kn, v, gq], [(H * XH, BF16)], ts=ts, name="xattn_fwd")[0]


def _xattn_bwd(q, kn, v, gq, do, *, ts=512):
    def fn(qb, dob, knb, vb, g):
        dqs, dks, dvs = [], [], []
        dg = jnp.zeros((1, XH), F32)
        for h in range(H):
            sl = slice(h * XH, (h + 1) * XH)
            qh, kh, vh, doh = qb[:, sl], knb[:, sl], vb[:, sl], dob[:, sl].astype(BF16)
            qn, p = _xattn_head(qh, kh, g)
            dp = lax.dot_general(doh, vh, (((1,), (1,)), ((), ())), preferred_element_type=F32)
            dvs.append(lax.dot_general(p.astype(BF16), doh, (((0,), (0,)), ((), ())), preferred_element_type=F32))
            ds = (p * (dp - jnp.sum(dp * p, axis=-1, keepdims=True))).astype(BF16)
            dqn = jnp.dot(ds, kh, preferred_element_type=F32)
            dks.append(lax.dot_general(ds, qn.astype(BF16), (((0,), (0,)), ((), ())), preferred_element_type=F32))
            _, vjp_n = jax.vjp(lambda a, b: _rms(a, b) * (XH ** -0.5), qh, g)
            dqh, dgh = vjp_n(dqn)
            dqs.append(dqh)
            dg = dg + dgh
        return (jnp.concatenate(dqs, axis=-1), jnp.concatenate(dks, axis=-1), jnp.concatenate(dvs, axis=-1), dg)

    return _rowwise(fn, [q, do], [kn, v, gq], [(H * XH, BF16)], [kn.shape, v.shape, gq.shape], ts=ts, name="xattn_bwd")


def _cmul(ar, ai, xr, xi):
    return ar * xr - ai * xi, ar * xi + ai * xr


def _scan(br, bi, ar, ai, *, reverse, cw=256, name):
    s, n = br.shape
    c = SCAN_CHUNKS
    tt = s // c
    cw = min(cw, n)

    def body(br_ref, bi_ref, ar_ref, ai_ref, xr_ref, xi_ref):
        a_r = jnp.broadcast_to(ar_ref[...], (c, cw))
        a_i = jnp.broadcast_to(ai_ref[...], (c, cw))
        zero = jnp.zeros((c, cw), F32)

        def row(step):
            t = (tt - 1 - step) if reverse else step
            return pl.ds(pl.multiple_of(t * c, c), c)

        def local(step, carry):
            sr, si, qr, qi = carry
            r = row(step)
            nr, ni = _cmul(a_r, a_i, sr, si)
            nr, ni = nr + br_ref[r, :], ni + bi_ref[r, :]
            xr_ref[r, :] = nr
            xi_ref[r, :] = ni
            return (nr, ni) + _cmul(a_r, a_i, qr, qi)

        end_r, end_i, pr, pi = lax.fori_loop(0, tt, local, (zero, zero, jnp.ones((c, cw), F32), zero))

        rows_id = lax.broadcasted_iota(jnp.int32, (c, cw), 0)
        car_r, car_i = zero, zero
        cur_r, cur_i = jnp.zeros((1, cw), F32), jnp.zeros((1, cw), F32)
        order = range(c - 1, -1, -1) if reverse else range(c)
        for kk in order:
            car_r = jnp.where(rows_id == kk, cur_r, car_r)
            car_i = jnp.where(rows_id == kk, cur_i, car_i)
            nr, ni = _cmul(pr[0:1], pi[0:1], cur_r, cur_i)
            cur_r = nr + end_r[kk:kk + 1]
            cur_i = ni + end_i[kk:kk + 1]

        def fix(step, carry):
            qr, qi = _cmul(a_r, a_i, *carry)
            r = row(step)
            dr, di = _cmul(qr, qi, car_r, car_i)
            xr_ref[r, :] += dr
            xi_ref[r, :] += di
            return qr, qi

        lax.fori_loop(0, tt, fix, (jnp.ones((c, cw), F32), zero))

    col = lambda j: (0, j)
    return pl.pallas_call(
        body, name=name, grid=(n // cw,),
        in_specs=[pl.BlockSpec((s, cw), col), pl.BlockSpec((s, cw), col), pl.BlockSpec((1, cw), col), pl.BlockSpec((1, cw), col)],
        out_specs=[pl.BlockSpec((s, cw), col), pl.BlockSpec((s, cw), col)],
        out_shape=[jax.ShapeDtypeStruct((s, n), F32), jax.ShapeDtypeStruct((s, n), F32)],
        compiler_params=_params(("parallel",)),
    )(br, bi, ar, ai)


def _scan_da(lr, li, xr, xi, *, cw=256):
    s, n = lr.shape
    c = SCAN_CHUNKS
    tt = s // c
    cw = min(cw, n)

    def body(lr_ref, li_ref, xr_ref, xi_ref, dar_ref, dai_ref):
        def step(t, carry):
            acc_r, acc_i = carry
            r = pl.ds(pl.multiple_of(t * c, c), c)
            rp = pl.ds(pl.multiple_of((t - 1) * c, c), c)
            l_r, l_i, p_r, p_i = lr_ref[r, :], li_ref[r, :], xr_ref[rp, :], xi_ref[rp, :]
            return acc_r + l_r * p_r + l_i * p_i, acc_i + l_i * p_r - l_r * p_i

        zero = jnp.zeros((c, cw), F32)
        acc_r, acc_i = lax.fori_loop(1, tt, step, (zero, zero))
        last = pl.ds((tt - 1) * c, c)
        rows_id = lax.broadcasted_iota(jnp.int32, (c, cw), 0)
        p_r = jnp.where(rows_id == 0, 0.0, pltpu.roll(xr_ref[last, :], 1, 0))
        p_i = jnp.where(rows_id == 0, 0.0, pltpu.roll(xi_ref[last, :], 1, 0))
        first = pl.ds(0, c)
        l_r, l_i = lr_ref[first, :], li_ref[first, :]
        acc_r = acc_r + l_r * p_r + l_i * p_i
        acc_i = acc_i + l_i * p_r - l_r * p_i
        dar_ref[...] = jnp.sum(acc_r, axis=0, keepdims=True)
        dai_ref[...] = jnp.sum(acc_i, axis=0, keepdims=True)

    col = lambda j: (0, j)
    return pl.pallas_call(
        body, name="s5_scan_da", grid=(n // cw,),
        in_specs=[pl.BlockSpec((s, cw), col)] * 4,
        out_specs=[pl.BlockSpec((1, cw), col)] * 2,
        out_shape=[jax.ShapeDtypeStruct((1, n), F32)] * 2,
        compiler_params=_params(("parallel",)),
    )(lr, li, xr, xi)


def _mesh_place():
    x, y, c = lax.axis_index("x"), lax.axis_index("y"), lax.axis_index("c")
    peers = []
    for k in range(1, N_DEV):
        px, py, pc = x ^ ((k >> 2) & 1), y ^ ((k >> 1) & 1), c ^ (k & 1)
        peers.append(((px, py, pc), 4 * px + 2 * py + pc))
    return 4 * x + 2 * y + c, peers


def _exchange(arrays, rows, *, gather, name):
    n_arr = len(arrays)

    def body(*refs):
        ins, outs = refs[:n_arr], refs[n_arr:2 * n_arr]
        send_sems, recv_sems, local_sems = refs[2 * n_arr:]
        me, peers = _mesh_place()

        def src(i, slot):
            if gather:
                return ins[i]
            stride, n = rows[i]
            return ins[i] if stride is None else ins[i].at[pl.ds(pl.multiple_of(slot * stride, 8), n)]

        def dst(i, slot):
            if gather:
                return outs[i].at[pl.ds(pl.multiple_of(slot * rows[i], BF16_ROWS), rows[i])]
            return outs[i].at[slot]

        def copy(i, k, landing):
            pxyz, pid = peers[k]
            return pltpu.make_async_remote_copy(
                src_ref=src(i, pid), dst_ref=dst(i, pid if landing else me),
                send_sem=send_sems.at[i, k], recv_sem=recv_sems.at[i, k],
                device_id=pxyz, device_id_type=pl.DeviceIdType.MESH)

        own = [pltpu.make_async_copy(src(i, me), dst(i, me), local_sems.at[i]) for i in range(n_arr)]
        sends = [copy(i, k, False) for i in range(n_arr) for k in range(N_DEV - 1)]
        for cp in own + sends:
            cp.start()
        for i in range(n_arr):
            for k in range(N_DEV - 1):
                copy(i, k, True).wait_recv()
        for cp in sends:
            cp.wait_send()
        for cp in own:
            cp.wait()

    if gather:
        assert all(r % BF16_ROWS == 0 for r in rows)
        out_shape = [jax.ShapeDtypeStruct((N_DEV * r, a.shape[1]), a.dtype) for a, r in zip(arrays, rows)]
    else:
        out_shape = [jax.ShapeDtypeStruct((N_DEV, a.shape[0] if st is None else n, a.shape[1]), a.dtype)
                     for a, (st, n) in zip(arrays, rows)]
    hbm = pl.BlockSpec(memory_space=pltpu.HBM)
    return pl.pallas_call(
        body, name=name, in_specs=[hbm] * n_arr, out_specs=[hbm] * n_arr, out_shape=out_shape,
        scratch_shapes=[pltpu.SemaphoreType.DMA((n_arr, N_DEV - 1)), pltpu.SemaphoreType.DMA((n_arr, N_DEV - 1)),
                        pltpu.SemaphoreType.DMA((n_arr,))],
        compiler_params=pltpu.CompilerParams(has_side_effects=True),
    )(*arrays)


def _elementwise_tiles(r, c):
    if r % 128 == 0:
        return 128, c
    return r, (256 if c % 256 == 0 else c)


def _adamw_math(g, w, m, v):
    nm = ADAM_B1 * m + (1.0 - ADAM_B1) * g
    nv = ADAM_B2 * v + (1.0 - ADAM_B2) * (g * g)
    m_hat = nm / (1.0 - ADAM_B1 ** ADAM_STEP)
    v_hat = nv / (1.0 - ADAM_B2 ** ADAM_STEP)
    return -ADAM_LR * (m_hat / (jnp.sqrt(v_hat) + ADAM_EPS) + ADAM_WD * w), nm, nv


def _sum_parts(p_ref):
    g = p_ref[0].astype(F32)
    for i in range(1, N_DEV):
        g = g + p_ref[i].astype(F32)
    return g


def _sum8(parts, *, name):
    _, r, cdim = parts.shape
    tr, tc = _elementwise_tiles(r, cdim)

    def body(p_ref, g_ref):
        g_ref[...] = _sum_parts(p_ref)

    return pl.pallas_call(
        body, name=name, grid=(r // tr, cdim // tc), in_specs=[pl.BlockSpec((N_DEV, tr, tc), lambda i, j: (0, i, j))],
        out_specs=pl.BlockSpec((tr, tc), lambda i, j: (i, j)), out_shape=jax.ShapeDtypeStruct((r, cdim), F32),
        compiler_params=_params(("parallel", "parallel")),
    )(parts)


def _adamw(g, w, m, v, *, parts=None, name):
    r, cdim = w.shape
    tr, tc = _elementwise_tiles(r, cdim)
    summed = parts is not None

    def body(*refs):
        if summed:
            p_ref, w_ref, m_ref, v_ref, g_ref, d_ref, nm_ref, nv_ref = refs
            g = _sum_parts(p_ref)
            g_ref[...] = g
        else:
            gi_ref, w_ref, m_ref, v_ref, d_ref, nm_ref, nv_ref = refs
            g = gi_ref[...]
        d_ref[...], nm_ref[...], nv_ref[...] = _adamw_math(g, w_ref[...], m_ref[...], v_ref[...])

    blk = pl.BlockSpec((tr, tc), lambda i, j: (i, j))
    first = pl.BlockSpec((N_DEV, tr, tc), lambda i, j: (0, i, j)) if summed else blk
    n_out = 4 if summed else 3
    return list(pl.pallas_call(
        body, name=name, grid=(r // tr, cdim // tc), in_specs=[first, blk, blk, blk],
        out_specs=[blk] * n_out, out_shape=[jax.ShapeDtypeStruct((r, cdim), F32)] * n_out,
        compiler_params=_params(("parallel", "parallel")),
    )(parts if summed else g, w, m, v))


SHARD_ROWS_P = {n: (FF_SHARD_P if 'ffn' in n else IN_SHARD_P if n == 'w_in' else None) for n in SHARDED}


def _to_exchange_layout(name, shard):
    t = shard.T if SHARD_AXIS[name] == 1 else shard
    pad = SHARD_ROWS_P[name]
    return t if pad is None else jnp.pad(t, ((0, pad - t.shape[0]), (0, 0)))


def _expand_w_in(wt):
    wt = wt.reshape(N_DEV, IN_SHARD_P, D)[:, :IN_SHARD].reshape(IN_W, D)
    o = Q_RANK + KV_RANK
    kr1, kr2 = wt[o:o + ROPE // 2], wt[o + ROPE // 2:o + ROPE]
    z = jnp.zeros((LANES - ROPE, D), wt.dtype)
    return jnp.concatenate([wt[:o], wt[o + ROPE:], kr1, kr2, z, -kr2, kr1, z], axis=0)


def _expand_w_uq(wt):
    w = wt.reshape(H, QK, Q_RANK)
    z = jnp.zeros((H, LANES - ROPE, Q_RANK), w.dtype)
    q1, q2 = w[:, NOPE:NOPE + ROPE // 2], w[:, NOPE + ROPE // 2:]
    return jnp.concatenate([w[:, :NOPE].reshape(H * NOPE, Q_RANK),
                            jnp.concatenate([q1, q2, z], axis=1).reshape(H * LANES, Q_RANK),
                            jnp.concatenate([-q2, q1, z], axis=1).reshape(H * LANES, Q_RANK)], axis=0)


def _layout_qk_gain(g):
    g = g.reshape(QK)
    g1, g2, z = g[NOPE:NOPE + ROPE // 2], g[NOPE + ROPE // 2:], jnp.zeros((LANES - ROPE,), g.dtype)
    return jnp.stack([g[:NOPE], jnp.concatenate([g1, g2, z]), jnp.concatenate([g2, g1, z])])


def _rep16(a):
    return jnp.repeat(a, SSM_GRP, axis=0)


def _layout_ssm_in(a_re, a_im, log_dt, b_re, b_im):
    b_r = jnp.transpose(b_re, (0, 2, 1)).reshape(SSM_G * SSM_GRP, SSM_P)
    b_i = jnp.transpose(b_im, (0, 2, 1)).reshape(SSM_G * SSM_GRP, SSM_P)
    ldt = jnp.broadcast_to(log_dt.reshape(SSM_G, 1), (SSM_G, SSM_P))
    return _rep16(a_re), _rep16(a_im), _rep16(ldt), b_r, b_i


def _block_diag_b(bb):
    eye = jnp.eye(SSM_G, dtype=bb.dtype)
    return (bb.reshape(SSM_G, SSM_GRP, 1, SSM_P) * eye[:, None, :, None]).reshape(SSM_W, SSM_N)


def _block_diag_c(cc):
    eye = jnp.eye(SSM_G, dtype=cc.dtype)
    return (jnp.transpose(cc, (0, 2, 1))[:, :, None, :] * eye[:, None, :, None]).reshape(SSM_N, SSM_W)


def _time_perm(a, inverse=False):
    s, w = a.shape
    c = SCAN_CHUNKS
    if inverse:
        return jnp.transpose(a.reshape(s // c, c, w), (1, 0, 2)).reshape(s, w)
    return jnp.transpose(a.reshape(c, s // c, w), (1, 0, 2)).reshape(s, w)


def _ffn_fwd(x, g, w_gt, w_ut, w_d, tag):
    h = _rowwise(_f_norm, [x], [g], [(D, BF16)], name=tag + "_norm")[0]
    gate = _mm(h, w_gt, tb=True, name=tag + "_gate")
    up = _mm(h, w_ut, tb=True, name=tag + "_up")
    act = _rowwise(_f_swiglu, [gate, up], [], [(D_FFP, BF16)], ts=256, name=tag + "_act")[0]
    x_out = _mm(act, w_d, res=x, scale=0.5, name=tag + "_down")
    return x_out, (h, gate, up, act)


def _ffn_bwd(x, g, w_gt, w_ut, w_d, saved, dx_out, tag):
    h, gate, up, act = saved
    dact = _mm(dx_out, w_d, tb=True, scale=0.5, out_dtype=BF16, name=tag + "_dact")
    d_d = _mm(act, dx_out, ta=True, scale=0.5, name=tag + "_dwdown")
    dgate, dup = _rowwise_bwd(_f_swiglu, [gate, up], [], [dact], row_grads={0: BF16, 1: BF16}, const_grads=[], ts=256,
                              name=tag + "_act_bwd")
    d_gt = _mm(dgate, h, ta=True, name=tag + "_dwgate")
    d_ut = _mm(dup, h, ta=True, name=tag + "_dwup")
    dh = _mm(dgate, w_gt, name=tag + "_dh_gate")
    dh = _mm(dup, w_ut, res=dh, out_dtype=BF16, name=tag + "_dh_up")
    dx, dg = _rowwise_bwd(_f_norm, [x], [g], [dh], row_grads={0: F32}, const_grads=[0], adds={0: dx_out}, name=tag + "_norm_bwd")
    return dx, dg, d_gt, d_ut, d_d


def _local_step(x, mem, cos, sin, target, wc, ws):
    gb, gs = {}, {}
    w_in_e = _expand_w_in(wc['w_in'])
    w_uq_e = _expand_w_uq(wc['mla_w_uq'])

    x1, sv1 = _ffn_fwd(x, ws['ffn1_norm'], wc['ffn1_w_gate'], wc['ffn1_w_up'], wc['ffn1_w_down'], "ffn1")

    h2 = _rowwise(_f_norm, [x1], [ws['mix_norm']], [(D, BF16)], name="mix_norm")[0]
    proj = _mm(h2, w_in_e, tb=True, name="w_in")
    c_q, c_kv = _rowwise(_f_prep1, [proj], [ws['q_norm'], ws['kv_norm']], [(Q_RANK, BF16), (KV_RANK, BF16)], name="mla_prep1")
    qall = _mm(c_q, w_uq_e, tb=True, name="w_uq")
    kv = _mm(c_kv, wc['mla_w_ukv'], tb=True, name="w_ukv")
    kr = _rowwise(_f_kr, [proj], [], [(2 * LANES, F32)], name="mla_kr")[0]
    q, k, v = _prep2_fwd(qall, kv, kr, cos, sin, ws['qk_gq'], ws['qk_gk'])
    o_mla, lse = _attn_fwd(q, k, v)

    u = proj[:, Q_RANK + KV_RANK:Q_RANK + KV_RANK + SSM_W]
    u_p = _time_perm(u)
    disc_in = [ws['ssm_lr'], ws['ssm_li'], ws['ssm_ldt'], ws['ssm_br'], ws['ssm_bi']]
    ar16, ai16, bbr, bbi = _rowwise(_f_disc, disc_in, [], [(SSM_P, F32)] * 4, name="s5_disc")
    a_r = ar16[::SSM_GRP].reshape(1, SSM_N)
    a_i = ai16[::SSM_GRP].reshape(1, SSM_N)
    bblk_r, bblk_i = _block_diag_b(bbr).astype(BF16), _block_diag_b(bbi).astype(BF16)
    cblk_r, cblk_i = _block_diag_c(ws['ssm_cr']).astype(BF16), _block_diag_c(-ws['ssm_ci']).astype(BF16)
    bu_r = _mm(u_p, bblk_r, name="s5_bu_r", tn_cap=1024)
    bu_i = _mm(u_p, bblk_i, name="s5_bu_i", tn_cap=1024)
    xr, xi = _scan(bu_r, bu_i, a_r, a_i, reverse=False, name="s5_scan_fwd")
    yc = _mm(xr, cblk_r, name="s5_y_r", tk_cap=2048)
    yc = _mm(xi, cblk_i, res=yc, name="s5_y_i", tk_cap=2048)
    g_p = _rowwise(_f_s5_gelu, [yc, u_p], [ws['ssm_d']], [(SSM_W, F32)], name="s5_gelu")[0]
    z_p = _mm(g_p, wc['ssm_w_glu'], name="s5_glu")
    g_t, z_t = _time_perm(g_p, inverse=True), _time_perm(z_p, inverse=True)
    on_consts = [ws['ssm_b_glu'], ws['out_norm_mla'], ws['out_norm_ssm']]
    ycat = _rowwise(_f_outnorm, [o_mla, g_t, z_t], on_consts, [(D, BF16)], name="out_norm")[0]
    x2 = _mm(ycat, wc['w_o'], res=x1, name="w_o")

    hx = _rowwise(_f_norm, [x2], [ws['xattn_norm']], [(D, BF16)], name="xattn_norm")[0]
    xq = _mm(hx, wc['xattn_w_q'], name="xattn_q")
    mn = _rowwise(_f_norm, [mem], [ws['mem_norm']], [(D, BF16)], name="mem_norm")[0]
    kvm = _mm(mn, wc['xattn_w_kv'], name="xattn_kv")
    xkn, xv = _rowwise(_f_memk, [kvm], [ws['xattn_k_norm']], [(H * XH, BF16), (H * XH, BF16)], name="xattn_knorm")
    xo = _xattn_fwd(xq, xkn, xv, ws['xattn_q_norm'])
    x3 = _mm(xo, wc['xattn_w_o'], tb=True, res=x2, name="xattn_o")

    x4, sv2 = _ffn_fwd(x3, ws['ffn2_norm'], wc['ffn2_w_gate'], wc['ffn2_w_up'], wc['ffn2_w_down'], "ffn2")

    def f_loss(yb, tb):
        err = yb - tb
        return err * (1.0 / D), jnp.broadcast_to(jnp.sum(jnp.sum(err * err, axis=1, keepdims=True), axis=0, keepdims=True) * (0.5 / D), (1, LANES))

    dx4, loss = _rowwise(f_loss, [x4, target], [], [(D, F32)], [(1, LANES)], name="loss")

    dx3, gs['ffn2_norm'], gb['ffn2_w_gate'], gb['ffn2_w_up'], gb['ffn2_w_down'] = _ffn_bwd(
        x3, ws['ffn2_norm'], wc['ffn2_w_gate'], wc['ffn2_w_up'], wc['ffn2_w_down'], sv2, dx4, "ffn2")

    dxo = _mm(dx3, wc['xattn_w_o'], out_dtype=BF16, name="xattn_o_dx")
    gb['xattn_w_o'] = _mm(dx3, xo, ta=True, name="xattn_o_dw")
    dxq, dxkn, dxv, gs['xattn_q_norm'] = _xattn_bwd(xq, xkn, xv, ws['xattn_q_norm'], dxo)
    dkvm, gs['xattn_k_norm'] = _rowwise_bwd(_f_memk, [kvm], [ws['xattn_k_norm']], [dxkn, dxv], row_grads={0: BF16},
                                            const_grads=[0], name="xattn_knorm_bwd")
    gb['xattn_w_kv'] = _mm(mn, dkvm, ta=True, name="xattn_kv_dw")
    dmn = _mm(dkvm, wc['xattn_w_kv'], tb=True, out_dtype=BF16, name="xattn_kv_dx")
    gs['mem_norm'] = _rowwise_bwd(_f_norm, [mem], [ws['mem_norm']], [dmn], row_grads={}, const_grads=[0], name="mem_norm_bwd")[0]
    gb['xattn_w_q'] = _mm(hx, dxq, ta=True, name="xattn_q_dw")
    dhx = _mm(dxq, wc['xattn_w_q'], tb=True, out_dtype=BF16, name="xattn_q_dx")
    dx2, gs['xattn_norm'] = _rowwise_bwd(_f_norm, [x2], [ws['xattn_norm']], [dhx], row_grads={0: F32}, const_grads=[0],
                                         adds={0: dx3}, name="xattn_norm_bwd")

    dycat = _mm(dx2, wc['w_o'], tb=True, out_dtype=BF16, name="w_o_dx")
    gb['w_o'] = _mm(ycat, dx2, ta=True, name="w_o_dw")
    do_mla, dg_t, dz_t, gs['ssm_b_glu'], gs['out_norm_mla'], gs['out_norm_ssm'] = _rowwise_bwd(
        _f_outnorm, [o_mla, g_t, z_t], on_consts, [dycat], row_grads={0: F32, 1: F32, 2: BF16}, const_grads=[0, 1, 2],
        name="out_norm_bwd")

    dz_p, dg_p = _time_perm(dz_t), _time_perm(dg_t)
    gb['ssm_w_glu'] = _mm(g_p, dz_p, ta=True, name="s5_glu_dw")
    dg_p = _mm(dz_p, wc['ssm_w_glu'], tb=True, res=dg_p, name="s5_glu_dx")
    dyc, du_d, gs['ssm_d'] = _rowwise_bwd(_f_s5_gelu, [yc, u_p], [ws['ssm_d']], [dg_p], row_grads={0: BF16, 1: F32},
                                          const_grads=[0], name="s5_gelu_bwd")
    d_cblk_r = _mm(xr, dyc, ta=True, name="s5_dc_r")
    d_cblk_i = _mm(xi, dyc, ta=True, name="s5_dc_i")
    dxr = _mm(dyc, cblk_r, tb=True, name="s5_dx_r", tn_cap=1024)
    dxi = _mm(dyc, cblk_i, tb=True, name="s5_dx_i", tn_cap=1024)
    lam_r, lam_i = _scan(dxr, dxi, a_r, -a_i, reverse=True, name="s5_scan_bwd")
    d_ar, d_ai = _scan_da(lam_r, lam_i, xr, xi)
    d_bblk_r = _mm(u_p, lam_r, ta=True, name="s5_db_r", tn_cap=1024)
    d_bblk_i = _mm(u_p, lam_i, ta=True, name="s5_db_i", tn_cap=1024)
    du_p = _mm(lam_r, bblk_r, tb=True, res=du_d, name="s5_du_r", tk_cap=2048)
    du_p = _mm(lam_i, bblk_i, tb=True, res=du_p, name="s5_du_i", tk_cap=2048)
    du = _time_perm(du_p, inverse=True)
    gs['ssm_cr'] = jax.linear_transpose(_block_diag_c, ws['ssm_cr'])(d_cblk_r)[0]
    gs['ssm_ci'] = -jax.linear_transpose(_block_diag_c, ws['ssm_ci'])(d_cblk_i)[0]
    d_bbr = jax.linear_transpose(_block_diag_b, bbr)(d_bblk_r)[0]
    d_bbi = jax.linear_transpose(_block_diag_b, bbi)(d_bblk_i)[0]
    d_ar16 = jnp.zeros((SSM_G * SSM_GRP, SSM_P), F32).at[::SSM_GRP].set(d_ar.reshape(SSM_G, SSM_P))
    d_ai16 = jnp.zeros((SSM_G * SSM_GRP, SSM_P), F32).at[::SSM_GRP].set(d_ai.reshape(SSM_G, SSM_P))
    gs['ssm_lr'], gs['ssm_li'], gs['ssm_ldt'], gs['ssm_br'], gs['ssm_bi'] = _rowwise_bwd(
        _f_disc, disc_in, [], [d_ar16, d_ai16, d_bbr, d_bbi], row_grads={i: F32 for i in range(5)}, const_grads=[],
        name="s5_disc_bwd")

    delta, do_b = _rowwise(_f_delta, [do_mla, o_mla], [], [(H * LANES, F32), (H * VD, BF16)], name="mla_delta")
    dq, dk, dv = _attn_bwd(q, k, v, do_b, lse, delta)
    dqall, dkv, dkr, gs['qk_gq'], gs['qk_gk'] = _prep2_bwd(qall, kv, kr, cos, sin, ws['qk_gq'], ws['qk_gk'], dq, dk, dv)
    d_w_uq_e = _mm(dqall, c_q, ta=True, name="w_uq_dw")
    gb['mla_w_uq'] = jax.linear_transpose(_expand_w_uq, jax.ShapeDtypeStruct(wc['mla_w_uq'].shape, F32))(d_w_uq_e)[0]
    dc_q = _mm(dqall, w_uq_e, out_dtype=BF16, name="w_uq_dx")
    gb['mla_w_ukv'] = _mm(dkv, c_kv, ta=True, name="w_ukv_dw")
    dc_kv = _mm(dkv, wc['mla_w_ukv'], out_dtype=BF16, name="w_ukv_dx")

    def f_prep1_bwd(pb, dcq, dckv, dub, dkrb, gq, gkv):
        _, vjp = jax.vjp(_f_prep1, pb[:, :Q_RANK + KV_RANK], gq, gkv)
        dpa, dgq, dgkv = vjp((dcq.astype(BF16), dckv.astype(BF16)))
        return jnp.concatenate([dpa, dub, dkrb], axis=-1), dgq, dgkv

    dproj, gs['q_norm'], gs['kv_norm'] = _rowwise(
        f_prep1_bwd, [proj, dc_q, dc_kv, du, dkr], [ws['q_norm'], ws['kv_norm']], [(IN_WP, BF16)],
        [(1, Q_RANK), (1, KV_RANK)], name="mla_prep1_bwd")
    d_w_in_e = _mm(dproj, h2, ta=True, name="w_in_dw")
    gb['w_in'] = jax.linear_transpose(_expand_w_in, jax.ShapeDtypeStruct(wc['w_in'].shape, F32))(d_w_in_e)[0]
    dh2 = _mm(dproj, w_in_e, out_dtype=BF16, name="w_in_dx")
    dx1, gs['mix_norm'] = _rowwise_bwd(_f_norm, [x1], [ws['mix_norm']], [dh2], row_grads={0: F32}, const_grads=[0],
                                       adds={0: dx2}, name="mix_norm_bwd")

    dx0, gs['ffn1_norm'], gb['ffn1_w_gate'], gb['ffn1_w_up'], gb['ffn1_w_down'] = _ffn_bwd(
        x, ws['ffn1_norm'], wc['ffn1_w_gate'], wc['ffn1_w_up'], wc['ffn1_w_down'], sv1, dx1, "ffn1")
    return loss, dx0, gb, gs


def _prep2_fwd(qall, kv, kr, cos, sin, gq, gk):
    return _rowwise(_f_prep2, [qall, kv, kr, cos, sin], [gq, gk], [(H * HQ, BF16), (H * HQ, BF16), (H * VD, BF16)],
                    ts=256, name="mla_prep2")


def _prep2_bwd(qall, kv, kr, cos, sin, gq, gk, dq, dk, dv):
    return _rowwise_bwd(_f_prep2, [qall, kv, kr, cos, sin], [gq, gk], [dq, dk, dv], row_grads={0: BF16, 1: BF16, 2: F32},
                        const_grads=[0, 1], ts=256, name="mla_prep2_bwd")


def _rope_tables(pos):
    half = ROPE // 2
    inv = ROPE_THETA ** (-jnp.arange(half, dtype=F32) / half)
    ang = pos.astype(F32)[:, None] * inv[None, :]
    z = jnp.zeros((pos.shape[0], LANES - ROPE), F32)
    cos, sin = jnp.cos(ang), jnp.sin(ang)
    return jnp.concatenate([cos, cos, z], axis=-1), jnp.concatenate([sin, sin, z], axis=-1)


def _small_layout(p):
    lr, li, ldt, br, bi = _layout_ssm_in(p['ssm_a_re'], p['ssm_a_im'], p['ssm_log_dt'], p['ssm_b_re'], p['ssm_b_im'])
    return {
        'ffn1_norm': p['ffn1_norm'].reshape(1, D), 'mix_norm': p['mix_norm'].reshape(1, D),
        'q_norm': p['mla_q_norm'].reshape(1, Q_RANK), 'kv_norm': p['mla_kv_norm'].reshape(1, KV_RANK),
        'qk_gq': _layout_qk_gain(p['mla_qk_norm_q']), 'qk_gk': _layout_qk_gain(p['mla_qk_norm_k']),
        'ssm_lr': lr, 'ssm_li': li, 'ssm_ldt': ldt, 'ssm_br': br, 'ssm_bi': bi,
        'ssm_cr': p['ssm_c_re'], 'ssm_ci': p['ssm_c_im'], 'ssm_d': p['ssm_d'].reshape(1, SSM_W),
        'ssm_b_glu': p['ssm_b_glu'].reshape(1, SSM_W),
        'out_norm_mla': p['out_norm_mla'].reshape(1, SSM_W), 'out_norm_ssm': p['out_norm_ssm'].reshape(1, SSM_W),
        'xattn_norm': p['xattn_norm'].reshape(1, D), 'mem_norm': p['mem_norm'].reshape(1, D),
        'xattn_q_norm': p['xattn_q_norm'].reshape(1, XH), 'xattn_k_norm': p['xattn_k_norm'].reshape(1, XH),
        'ffn2_norm': p['ffn2_norm'].reshape(1, D),
    }


def _pack(arrs, rows):
    flat = jnp.concatenate([a.reshape(-1) for a in arrs])
    return jnp.pad(flat, (0, rows * D - flat.shape[0])).reshape(rows, D)


def _unpack(flat, shapes):
    flat = flat.reshape(-1)
    out, off = [], 0
    for sh in shapes:
        n = int(np.prod(sh))
        out.append(flat[off:off + n].reshape(sh))
        off += n
    return out


def kernel(x, mem, positions, ffn1_norm, ffn1_w_gate, ffn1_w_up, ffn1_w_down, mix_norm, w_in, mla_q_norm, mla_w_uq, mla_kv_norm, mla_w_ukv, mla_qk_norm_q, mla_qk_norm_k, ssm_a_re, ssm_a_im, ssm_log_dt, ssm_b_re, ssm_b_im, ssm_c_re, ssm_c_im, ssm_d, ssm_w_glu, ssm_b_glu, out_norm_mla, out_norm_ssm, w_o, xattn_norm, mem_norm, xattn_w_q, xattn_w_kv, xattn_q_norm, xattn_k_norm, xattn_w_o, ffn2_norm, ffn2_w_gate, ffn2_w_up, ffn2_w_down, loss_target, m_ffn1_norm, m_ffn1_w_gate, m_ffn1_w_up, m_ffn1_w_down, m_mix_norm, m_w_in, m_mla_q_norm, m_mla_w_uq, m_mla_kv_norm, m_mla_w_ukv, m_mla_qk_norm_q, m_mla_qk_norm_k, m_ssm_a_re, m_ssm_a_im, m_ssm_log_dt, m_ssm_b_re, m_ssm_b_im, m_ssm_c_re, m_ssm_c_im, m_ssm_d, m_ssm_w_glu, m_ssm_b_glu, m_out_norm_mla, m_out_norm_ssm, m_w_o, m_xattn_norm, m_mem_norm, m_xattn_w_q, m_xattn_w_kv, m_xattn_q_norm, m_xattn_k_norm, m_xattn_w_o, m_ffn2_norm, m_ffn2_w_gate, m_ffn2_w_up, m_ffn2_w_down, v_ffn1_norm, v_ffn1_w_gate, v_ffn1_w_up, v_ffn1_w_down, v_mix_norm, v_w_in, v_mla_q_norm, v_mla_w_uq, v_mla_kv_norm, v_mla_w_ukv, v_mla_qk_norm_q, v_mla_qk_norm_k, v_ssm_a_re, v_ssm_a_im, v_ssm_log_dt, v_ssm_b_re, v_ssm_b_im, v_ssm_c_re, v_ssm_c_im, v_ssm_d, v_ssm_w_glu, v_ssm_b_glu, v_out_norm_mla, v_out_norm_ssm, v_w_o, v_xattn_norm, v_mem_norm, v_xattn_w_q, v_xattn_w_kv, v_xattn_q_norm, v_xattn_k_norm, v_xattn_w_o, v_ffn2_norm, v_ffn2_w_gate, v_ffn2_w_up, v_ffn2_w_down):
    args = dict(locals())
    w = {n: args[n] for n in WEIGHTS}
    mom = {n: args['m_' + n] for n in WEIGHTS}
    var = {n: args['v_' + n] for n in WEIGHTS}
    return _step(x, mem, positions, loss_target, w, mom, var)


def _step(x, mem, positions, loss_target, w, mom, var):
    blocks = [_to_exchange_layout(n, w[n][0]).astype(BF16) for n in SHARDED]
    gathered = _exchange(blocks, [b.shape[0] for b in blocks], gather=True, name="gather_weights")
    wc = dict(zip(SHARDED, gathered))

    small = {n: w[n][0] for n in SMALL}
    ws = _small_layout(small)
    cos, sin = _rope_tables(positions[0])
    loss, dx, gb, gs = _local_step(x[0], mem[0], cos, sin, loss_target[0], wc, ws)

    g_small = jax.linear_transpose(_small_layout, {n: jax.ShapeDtypeStruct(small[n].shape, F32) for n in SMALL})(gs)[0]
    small_shapes = [small[n].shape for n in SMALL]
    n_small = sum(int(np.prod(sh)) for sh in small_shapes) + 1
    rows_small = -(-n_small // (8 * D)) * 8
    small_pack = _pack([g_small[n] for n in SMALL] + [loss[0, :1]], rows_small)
    rows = [(blk.shape[0], blk.shape[0] if SHARD_ROWS_P[n] is None else w[n][0].shape[SHARD_AXIS[n]])
            for n, blk in zip(SHARDED, blocks)]
    parts = _exchange([gb[n] for n in SHARDED] + [small_pack], rows + [(None, rows_small)], gather=False,
                      name="scatter_grads")

    out = {}
    for n, p in zip(SHARDED, parts):
        if SHARD_AXIS[n] == 0:
            out[n] = _adamw(None, w[n][0], mom[n][0], var[n][0], parts=p, name="adamw_" + n)
        else:
            g = _sum8(p, name="sum_" + n).T
            out[n] = [g] + _adamw(g, w[n][0], mom[n][0], var[n][0], name="adamw_" + n)
    state = [_pack([t[n][0] for n in SMALL], rows_small) for t in (w, mom, var)]
    small_out = _adamw(None, *state, parts=parts[-1], name="adamw_small")
    loss_total = small_out[0].reshape(-1)[n_small - 1]
    for n, vals in zip(SMALL, zip(*[_unpack(flat, small_shapes) for flat in small_out])):
        out[n] = vals
    outs = [out[n][i][None] for i in range(4) for n in WEIGHTS]
    return (loss_total, dx[None], *outs)
```

```python
import math

import jax
import jax.numpy as jnp
import numpy as np
from jax import lax
from jax.experimental import pallas as pl
from jax.experimental.pallas import tpu as pltpu

F32 = jnp.float32
BF16 = jnp.bfloat16

N_DEV = 8
D = 1024
D_FF = 2752
D_FFP = 2816
MEM_LEN = 256
H = 4
Q_RANK, KV_RANK, NOPE, ROPE, VD = 384, 256, 128, 64, 128
QK = NOPE + ROPE
HQ = 2 * 128
SSM_W, SSM_G, SSM_GRP, SSM_P = 512, 32, 16, 64
SSM_N = SSM_G * SSM_P
IN_W = 1216
IN_WP = 1408
XH = 128
EPS = 1e-6
ROPE_THETA = 10000.0
SCAN_CHUNKS = 8

ADAM_LR, ADAM_B1, ADAM_B2, ADAM_EPS, ADAM_WD, ADAM_STEP = 0.001, 0.9, 0.999, 1e-08, 0.01, 10

VMEM_LIMIT = 56 * 1024 * 1024
ACC_BYTES = 6 * 1024 * 1024
LANES = 128
BF16_ROWS = 16
FF_SHARD = D_FF // N_DEV
FF_SHARD_P = 352
IN_SHARD = IN_W // N_DEV
IN_SHARD_P = 160

WEIGHTS = ['ffn1_norm', 'ffn1_w_gate', 'ffn1_w_up', 'ffn1_w_down', 'mix_norm', 'w_in', 'mla_q_norm', 'mla_w_uq',
           'mla_kv_norm', 'mla_w_ukv', 'mla_qk_norm_q', 'mla_qk_norm_k', 'ssm_a_re', 'ssm_a_im', 'ssm_log_dt',
           'ssm_b_re', 'ssm_b_im', 'ssm_c_re', 'ssm_c_im', 'ssm_d', 'ssm_w_glu', 'ssm_b_glu', 'out_norm_mla',
           'out_norm_ssm', 'w_o', 'xattn_norm', 'mem_norm', 'xattn_w_q', 'xattn_w_kv', 'xattn_q_norm',
           'xattn_k_norm', 'xattn_w_o', 'ffn2_norm', 'ffn2_w_gate', 'ffn2_w_up', 'ffn2_w_down']
SHARD_AXIS = {'ffn1_w_gate': 1, 'ffn1_w_up': 1, 'ffn1_w_down': 0, 'w_in': 1, 'mla_w_uq': 1, 'mla_w_ukv': 1,
              'ssm_w_glu': 0, 'w_o': 0, 'xattn_w_q': 0, 'xattn_w_kv': 0, 'xattn_w_o': 1,
              'ffn2_w_gate': 1, 'ffn2_w_up': 1, 'ffn2_w_down': 0}
SHARDED = [n for n in WEIGHTS if n in SHARD_AXIS]
SMALL = [n for n in WEIGHTS if n not in SHARD_AXIS]


def _params(sem=None):
    return pltpu.CompilerParams(dimension_semantics=sem, vmem_limit_bytes=VMEM_LIMIT)


def _tile(n, cap):
    if n <= cap:
        return n
    best = n
    for t in range(LANES, cap + 1, LANES):
        if n % t == 0:
            best = t
    return best


def _mm(a, b, *, ta=False, tb=False, out_dtype=F32, res=None, scale=1.0, name, tm_cap=512, tn_cap=1408, tk_cap=2816):
    m, k = (a.shape[1], a.shape[0]) if ta else a.shape
    k2, n = (b.shape[1], b.shape[0]) if tb else b.shape
    assert k == k2, (a.shape, b.shape, ta, tb)
    if ta:
        tk_cap = min(tk_cap, 512)
        tm_cap = 1408
    tm, tn, tk = _tile(m, tm_cap), _tile(n, tn_cap), _tile(k, tk_cap)
    if tm * tn * 4 > ACC_BYTES:
        tn = _tile(n, max(LANES, ACC_BYTES // (4 * tm) // LANES * LANES))
    nk = k // tk
    dims = (((0 if ta else 1,), (1 if tb else 0,)), ((), ()))
    has_res = res is not None

    def body(*refs):
        if has_res:
            a_ref, b_ref, r_ref, o_ref, acc_ref = refs
        else:
            a_ref, b_ref, o_ref, acc_ref = refs
        kk = pl.program_id(2)

        @pl.when(kk == 0)
        def _():
            acc_ref[...] = jnp.zeros_like(acc_ref)

        acc_ref[...] += lax.dot_general(a_ref[...].astype(BF16), b_ref[...].astype(BF16), dims,
                                        preferred_element_type=F32)

        @pl.when(kk == nk - 1)
        def _():
            out = acc_ref[...]
            if scale != 1.0:
                out = out * scale
            if has_res:
                out = out + r_ref[...].astype(F32)
            o_ref[...] = out.astype(o_ref.dtype)

    a_spec = pl.BlockSpec((tk, tm), lambda i, j, kk: (kk, i)) if ta else pl.BlockSpec((tm, tk), lambda i, j, kk: (i, kk))
    b_spec = pl.BlockSpec((tn, tk), lambda i, j, kk: (j, kk)) if tb else pl.BlockSpec((tk, tn), lambda i, j, kk: (kk, j))
    o_spec = pl.BlockSpec((tm, tn), lambda i, j, kk: (i, j))
    in_specs = [a_spec, b_spec] + ([o_spec] if has_res else [])
    args = (a, b) + ((res,) if has_res else ())
    return pl.pallas_call(
        body, name=name, grid=(m // tm, n // tn, nk), in_specs=in_specs, out_specs=o_spec,
        out_shape=jax.ShapeDtypeStruct((m, n), out_dtype), scratch_shapes=[pltpu.VMEM((tm, tn), F32)],
        compiler_params=_params(("parallel", "parallel", "arbitrary")),
    )(*args)


def _rowwise(fn, rows, consts, outs, accs=(), *, ts=512, name, deps=()):
    s = rows[0].shape[0]
    ts = min(ts, s)
    assert s % ts == 0
    n_rows, n_consts, n_outs = len(rows), len(consts), len(outs)
    deps = [d for d in deps if d is not None]
    consts = list(consts) + deps

    def body(*refs):
        ins = [r[...] for r in refs[:n_rows + n_consts]]
        res = fn(*ins)
        res = tuple(res) if isinstance(res, (tuple, list)) else (res,)
        out_refs = refs[n_rows + len(consts):]
        for o_ref, val in zip(out_refs[:n_outs], res[:n_outs]):
            o_ref[...] = val.astype(o_ref.dtype)
        if accs:
            first = pl.program_id(0) == 0

            @pl.when(first)
            def _():
                for a_ref, val in zip(out_refs[n_outs:], res[n_outs:]):
                    a_ref[...] = val.astype(F32)

            @pl.when(jnp.logical_not(first))
            def _():
                for a_ref, val in zip(out_refs[n_outs:], res[n_outs:]):
                    a_ref[...] += val.astype(F32)

    in_specs = [pl.BlockSpec((ts, r.shape[1]), lambda i: (i, 0)) for r in rows]
    in_specs += [pl.BlockSpec(c.shape, lambda i: (0, 0)) for c in consts]
    out_specs = [pl.BlockSpec((ts, w), lambda i: (i, 0)) for w, _ in outs]
    out_specs += [pl.BlockSpec(tuple(sh), lambda i: (0, 0)) for sh in accs]
    out_shape = [jax.ShapeDtypeStruct((s, w), dt) for w, dt in outs]
    out_shape += [jax.ShapeDtypeStruct(tuple(sh), F32) for sh in accs]
    res = pl.pallas_call(
        body, name=name, grid=(s // ts,), in_specs=in_specs, out_specs=out_specs, out_shape=out_shape,
        compiler_params=_params(("arbitrary",)),
    )(*rows, *consts)
    return res


def _rowwise_bwd(f, rows, consts, cts, *, row_grads, const_grads, adds=None, ts=512, name, deps=()):
    adds = adds or {}
    n_rows, n_consts, n_cts = len(rows), len(consts), len(cts)
    add_keys = sorted(adds)
    rg = sorted(row_grads)
    cg = sorted(const_grads)

    def fn(*args):
        r = args[:n_rows]
        c = args[n_rows:n_rows + n_consts]
        ct = args[n_rows + n_consts:n_rows + n_consts + n_cts]
        extra = args[n_rows + n_consts + n_cts:]
        outs, vjp = jax.vjp(f, *r, *c)
        outs = tuple(outs) if isinstance(outs, (tuple, list)) else (outs,)
        cot = tuple(g.astype(o.dtype) for g, o in zip(ct, outs))
        grads = vjp(cot if len(cot) > 1 else cot[0])
        res = []
        for i in rg:
            g = grads[i].astype(F32)
            if i in adds:
                g = g + extra[add_keys.index(i)].astype(F32)
            res.append(g)
        for i in cg:
            res.append(grads[n_rows + i])
        return tuple(res)

    rows_all = list(rows) + list(cts) + [adds[i] for i in add_keys]
    def fn2(*args):
        nr = len(rows_all)
        rr, cc = args[:nr], args[nr:]
        return fn(*rr[:n_rows], *cc, *rr[n_rows:])

    outs = [(rows[i].shape[1], row_grads[i]) for i in rg]
    accs = [consts[i].shape for i in cg]
    return _rowwise(fn2, rows_all, list(consts), outs, accs, ts=ts, name=name, deps=deps)


def _rms(x, g):
    xf = x.astype(F32)
    return xf * lax.rsqrt(jnp.mean(xf * xf, axis=-1, keepdims=True) + EPS) * g.astype(F32)


def _sigmoid(x):
    return 1.0 / (1.0 + jnp.exp(-x))


def _f_norm(x, g):
    return _rms(x, g).astype(BF16)


def _f_swiglu(gate, up):
    gate, up = gate.astype(F32), up.astype(F32)
    return (gate * _sigmoid(gate) * up).astype(BF16)


def _f_prep1(proj, gq, gkv):
    return _rms(proj[:, :Q_RANK], gq).astype(BF16), _rms(proj[:, Q_RANK:Q_RANK + KV_RANK], gkv).astype(BF16)


def _f_kr(proj):
    return (proj[:, Q_RANK + KV_RANK + SSM_W:],)


def _f_prep2(qall, kv, kr2, cos, sin, gq, gk):
    kr, krs = kr2[:, :LANES].astype(F32), kr2[:, LANES:].astype(F32)
    k_rot = kr * gk[1:2] * cos + krs * gk[2:3] * sin
    k_ss = jnp.sum(kr * kr, axis=-1, keepdims=True)
    q_scale = QK ** -0.5
    qs, ks, vs = [], [], []
    for h in range(H):
        qn = qall[:, h * LANES:(h + 1) * LANES].astype(F32)
        qr = qall[:, (H + h) * LANES:(H + h + 1) * LANES].astype(F32)
        qrs = qall[:, (2 * H + h) * LANES:(2 * H + h + 1) * LANES].astype(F32)
        rstd = lax.rsqrt((jnp.sum(qn * qn, axis=-1, keepdims=True) + jnp.sum(qr * qr, axis=-1, keepdims=True)) / QK + EPS)
        rstd = rstd * q_scale
        qs += [qn * gq[0:1] * rstd, (qr * gq[1:2] * cos + qrs * gq[2:3] * sin) * rstd]
        kn = kv[:, 2 * h * LANES:(2 * h + 1) * LANES].astype(F32)
        rstd_k = lax.rsqrt((jnp.sum(kn * kn, axis=-1, keepdims=True) + k_ss) / QK + EPS)
        ks += [kn * gk[0:1] * rstd_k, k_rot * rstd_k]
        vs.append(kv[:, (2 * h + 1) * LANES:(2 * h + 2) * LANES])
    return (jnp.concatenate(qs, axis=-1).astype(BF16), jnp.concatenate(ks, axis=-1).astype(BF16),
            jnp.concatenate(vs, axis=-1).astype(BF16))


def _gelu(x):
    return 0.5 * x * (1.0 + jnp.tanh(math.sqrt(2.0 / math.pi) * (x + 0.044715 * (x * x * x))))


def _f_s5_gelu(yc, u, d):
    return _gelu(yc.astype(F32) + d * u.astype(F32))


def _f_outnorm(o_mla, g, z, b_glu, g_om, g_os):
    y_ssm = g * _sigmoid(z + b_glu)
    return jnp.concatenate([_rms(o_mla, g_om), _rms(y_ssm, g_os)], axis=-1).astype(BF16)


def _f_memk(kvm, gk):
    ks = [_rms(kvm[:, h * XH:(h + 1) * XH], gk) for h in range(H)]
    return jnp.concatenate(ks, axis=-1).astype(BF16), kvm[:, H * XH:].astype(BF16)


def _f_disc(lr, li, log_dt, br, bi):
    dt = jnp.exp(log_dt)
    decay = jnp.exp(lr * dt)
    ar = decay * jnp.cos(li * dt)
    ai = decay * jnp.sin(li * dt)
    den = lr * lr + li * li
    nr = ar - 1.0
    coef_r = (nr * lr + ai * li) / den
    coef_i = (ai * lr - nr * li) / den
    return ar, ai, coef_r * br - coef_i * bi, coef_r * bi + coef_i * br


def _causal_mask(i, j, tq, tk):
    qpos = i * tq + lax.broadcasted_iota(jnp.int32, (tq, tk), 0)
    kpos = j * tk + lax.broadcasted_iota(jnp.int32, (tq, tk), 1)
    return qpos >= kpos


def _attn_fwd(q, k, v, *, t=512):
    s = q.shape[0]
    t = min(t, s)
    nb = s // t

    def body(q_ref, k_ref, v_ref, o_ref, lse_ref, m_sc, l_sc, acc_sc):
        i, j = pl.program_id(1), pl.program_id(2)

        @pl.when(j == 0)
        def _():
            m_sc[...] = jnp.full_like(m_sc, -jnp.inf)
            l_sc[...] = jnp.zeros_like(l_sc)
            acc_sc[...] = jnp.zeros_like(acc_sc)

        @pl.when(j <= i)
        def _():
            sc = lax.dot_general(q_ref[...], k_ref[...], (((1,), (1,)), ((), ())), preferred_element_type=F32)
            sc = jnp.where(_causal_mask(i, j, t, t), sc, -jnp.inf)
            m_old = m_sc[...]
            m_new = jnp.maximum(m_old, jnp.max(sc, axis=-1, keepdims=True))
            p = jnp.exp(sc - m_new)
            alpha = jnp.exp(m_old - m_new)
            l_sc[...] = alpha * l_sc[...] + jnp.sum(p, axis=-1, keepdims=True)
            acc_sc[...] = alpha * acc_sc[...] + jnp.dot(p.astype(BF16), v_ref[...], preferred_element_type=F32)
            m_sc[...] = m_new

        @pl.when(j == i)
        def _():
            o_ref[...] = acc_sc[...] / l_sc[...]
            lse_ref[...] = jnp.broadcast_to(m_sc[...] + jnp.log(l_sc[...]), lse_ref.shape)

    kv_map = lambda h, i, j: (jnp.minimum(j, i), h)
    return pl.pallas_call(
        body, name="mla_attn_fwd", grid=(H, nb, nb),
        in_specs=[pl.BlockSpec((t, HQ), lambda h, i, j: (i, h)), pl.BlockSpec((t, HQ), kv_map),
                  pl.BlockSpec((t, VD), kv_map)],
        out_specs=[pl.BlockSpec((t, VD), lambda h, i, j: (i, h)), pl.BlockSpec((t, LANES), lambda h, i, j: (i, h))],
        out_shape=[jax.ShapeDtypeStruct((s, H * VD), F32), jax.ShapeDtypeStruct((s, H * LANES), F32)],
        scratch_shapes=[pltpu.VMEM((t, 1), F32), pltpu.VMEM((t, 1), F32), pltpu.VMEM((t, VD), F32)],
        compiler_params=_params(("parallel", "parallel", "arbitrary")),
    )(q, k, v)


def _attn_probs(q_ref, k_ref, v_ref, do_ref, lse_ref, dl_ref, i, j, t):
    sc = lax.dot_general(q_ref[...], k_ref[...], (((1,), (1,)), ((), ())), preferred_element_type=F32)
    p = jnp.where(_causal_mask(i, j, t, t), jnp.exp(sc - lse_ref[...][:, :1]), 0.0)
    dp = lax.dot_general(do_ref[...], v_ref[...], (((1,), (1,)), ((), ())), preferred_element_type=F32)
    ds = p * (dp - dl_ref[...][:, :1])
    return p, ds


def _attn_bwd(q, k, v, do, lse, delta, *, t=512):
    s = q.shape[0]
    t = min(t, s)
    nb = s // t

    def dq_body(q_ref, k_ref, v_ref, do_ref, lse_ref, dl_ref, dq_ref, acc_sc):
        i, j = pl.program_id(1), pl.program_id(2)

        @pl.when(j == 0)
        def _():
            acc_sc[...] = jnp.zeros_like(acc_sc)

        @pl.when(j <= i)
        def _():
            _, ds = _attn_probs(q_ref, k_ref, v_ref, do_ref, lse_ref, dl_ref, i, j, t)
            acc_sc[...] += jnp.dot(ds.astype(BF16), k_ref[...], preferred_element_type=F32)

        @pl.when(j == i)
        def _():
            dq_ref[...] = acc_sc[...]

    q_map = lambda h, i, j: (i, h)
    kv_map = lambda h, i, j: (jnp.minimum(j, i), h)
    dq = pl.pallas_call(
        dq_body, name="mla_attn_dq", grid=(H, nb, nb),
        in_specs=[pl.BlockSpec((t, HQ), q_map), pl.BlockSpec((t, HQ), kv_map), pl.BlockSpec((t, VD), kv_map),
                  pl.BlockSpec((t, VD), q_map), pl.BlockSpec((t, LANES), q_map), pl.BlockSpec((t, LANES), q_map)],
        out_specs=pl.BlockSpec((t, HQ), q_map),
        out_shape=jax.ShapeDtypeStruct((s, H * HQ), F32),
        scratch_shapes=[pltpu.VMEM((t, HQ), F32)],
        compiler_params=_params(("parallel", "parallel", "arbitrary")),
    )(q, k, v, do, lse, delta)

    def dkv_body(q_ref, k_ref, v_ref, do_ref, lse_ref, dl_ref, dk_ref, dv_ref, dk_sc, dv_sc):
        j, i = pl.program_id(1), pl.program_id(2)

        @pl.when(i == 0)
        def _():
            dk_sc[...] = jnp.zeros_like(dk_sc)
            dv_sc[...] = jnp.zeros_like(dv_sc)

        @pl.when(i >= j)
        def _():
            p, ds = _attn_probs(q_ref, k_ref, v_ref, do_ref, lse_ref, dl_ref, i, j, t)
            dv_sc[...] += lax.dot_general(p.astype(BF16), do_ref[...], (((0,), (0,)), ((), ())), preferred_element_type=F32)
            dk_sc[...] += lax.dot_general(ds.astype(BF16), q_ref[...], (((0,), (0,)), ((), ())), preferred_element_type=F32)

        @pl.when(i == nb - 1)
        def _():
            dk_ref[...] = dk_sc[...]
            dv_ref[...] = dv_sc[...]

    q_map2 = lambda h, j, i: (jnp.maximum(i, j), h)
    kv_map2 = lambda h, j, i: (j, h)
    dk, dv = pl.pallas_call(
        dkv_body, name="mla_attn_dkv", grid=(H, nb, nb),
        in_specs=[pl.BlockSpec((t, HQ), q_map2), pl.BlockSpec((t, HQ), kv_map2), pl.BlockSpec((t, VD), kv_map2),
                  pl.BlockSpec((t, VD), q_map2), pl.BlockSpec((t, LANES), q_map2), pl.BlockSpec((t, LANES), q_map2)],
        out_specs=[pl.BlockSpec((t, HQ), kv_map2), pl.BlockSpec((t, VD), kv_map2)],
        out_shape=[jax.ShapeDtypeStruct((s, H * HQ), F32), jax.ShapeDtypeStruct((s, H * VD), F32)],
        scratch_shapes=[pltpu.VMEM((t, HQ), F32), pltpu.VMEM((t, VD), F32)],
        compiler_params=_params(("parallel", "parallel", "arbitrary")),
    )(q, k, v, do, lse, delta)
    return dq, dk, dv


def _f_delta(do, o):
    prod = do.astype(F32) * o.astype(F32)
    parts = [jnp.broadcast_to(jnp.sum(prod[:, h * VD:(h + 1) * VD], axis=-1, keepdims=True), (do.shape[0], LANES))
             for h in range(H)]
    return jnp.concatenate(parts, axis=-1), do.astype(BF16)


def _xattn_head(qh, kh, gq):
    qn = _rms(qh, gq) * (XH ** -0.5)
    sc = lax.dot_general(qn.astype(BF16), kh, (((1,), (1,)), ((), ())), preferred_element_type=F32)
    sc = sc - jnp.max(sc, axis=-1, keepdims=True)
    e = jnp.exp(sc)
    return qn, e / jnp.sum(e, axis=-1, keepdims=True)


def _xattn_fwd(q, kn, v, gq, *, ts=512):
    def fn(qb, knb, vb, g):
        outs = []
        for h in range(H):
            sl = slice(h * XH, (h + 1) * XH)
            _, p = _xattn_head(qb[:, sl], knb[:, sl], g)
            outs.append(jnp.dot(p.astype(BF16), vb[:, sl], preferred_element_type=F32))
        return (jnp.concatenate(outs, axis=-1),)

    return _rowwise(fn, [q], [kn, v, gq], [(H * XH, BF16)], ts=ts, name="xattn_fwd")[0]


def _xattn_bwd(q, kn, v, gq, do, *, ts=512):
    def fn(qb, dob, knb, vb, g):
        dqs, dks, dvs = [], [], []
        dg = jnp.zeros((1, XH), F32)
        for h in range(H):
            sl = slice(h * XH, (h + 1) * XH)
            qh, kh, vh, doh = qb[:, sl], knb[:, sl], vb[:, sl], dob[:, sl].astype(BF16)
            qn, p = _xattn_head(qh, kh, g)
            dp = lax.dot_general(doh, vh, (((1,), (1,)), ((), ())), preferred_element_type=F32)
            dvs.append(lax.dot_general(p.astype(BF16), doh, (((0,), (0,)), ((), ())), preferred_element_type=F32))
            ds = (p * (dp - jnp.sum(dp * p, axis=-1, keepdims=True))).astype(BF16)
            dqn = jnp.dot(ds, kh, preferred_element_type=F32)
            dks.append(lax.dot_general(ds, qn.astype(BF16), (((0,), (0,)), ((), ())), preferred_element_type=F32))
            _, vjp_n = jax.vjp(lambda a, b: _rms(a, b) * (XH ** -0.5), qh, g)
            dqh, dgh = vjp_n(dqn)
            dqs.append(dqh)
            dg = dg + dgh
        return (jnp.concatenate(dqs, axis=-1), jnp.concatenate(dks, axis=-1), jnp.concatenate(dvs, axis=-1), dg)

    return _rowwise(fn, [q, do], [kn, v, gq], [(H * XH, BF16)], [kn.shape, v.shape, gq.shape], ts=ts, name="xattn_bwd")


def _cmul(ar, ai, xr, xi):
    return ar * xr - ai * xi, ar * xi + ai * xr


def _scan(br, bi, ar, ai, *, reverse, cw=256, name):
    s, n = br.shape
    c = SCAN_CHUNKS
    tt = s // c
    cw = min(cw, n)

    def body(br_ref, bi_ref, ar_ref, ai_ref, xr_ref, xi_ref):
        a_r = jnp.broadcast_to(ar_ref[...], (c, cw))
        a_i = jnp.broadcast_to(ai_ref[...], (c, cw))
        zero = jnp.zeros((c, cw), F32)

        def row(step):
            t = (tt - 1 - step) if reverse else step
            return pl.ds(pl.multiple_of(t * c, c), c)

        def local(step, carry):
            sr, si, qr, qi = carry
            r = row(step)
            nr, ni = _cmul(a_r, a_i, sr, si)
            nr, ni = nr + br_ref[r, :], ni + bi_ref[r, :]
            xr_ref[r, :] = nr
            xi_ref[r, :] = ni
            return (nr, ni) + _cmul(a_r, a_i, qr, qi)

        end_r, end_i, pr, pi = lax.fori_loop(0, tt, local, (zero, zero, jnp.ones((c, cw), F32), zero))

        rows_id = lax.broadcasted_iota(jnp.int32, (c, cw), 0)
        car_r, car_i = zero, zero
        cur_r, cur_i = jnp.zeros((1, cw), F32), jnp.zeros((1, cw), F32)
        order = range(c - 1, -1, -1) if reverse else range(c)
        for kk in order:
            car_r = jnp.where(rows_id == kk, cur_r, car_r)
            car_i = jnp.where(rows_id == kk, cur_i, car_i)
            nr, ni = _cmul(pr[0:1], pi[0:1], cur_r, cur_i)
            cur_r = nr + end_r[kk:kk + 1]
            cur_i = ni + end_i[kk:kk + 1]

        def fix(step, carry):
            qr, qi = _cmul(a_r, a_i, *carry)
            r = row(step)
            dr, di = _cmul(qr, qi, car_r, car_i)
            xr_ref[r, :] += dr
            xi_ref[r, :] += di
            return qr, qi

        lax.fori_loop(0, tt, fix, (jnp.ones((c, cw), F32), zero))

    col = lambda j: (0, j)
    return pl.pallas_call(
        body, name=name, grid=(n // cw,),
        in_specs=[pl.BlockSpec((s, cw), col), pl.BlockSpec((s, cw), col), pl.BlockSpec((1, cw), col), pl.BlockSpec((1, cw), col)],
        out_specs=[pl.BlockSpec((s, cw), col), pl.BlockSpec((s, cw), col)],
        out_shape=[jax.ShapeDtypeStruct((s, n), F32), jax.ShapeDtypeStruct((s, n), F32)],
        compiler_params=_params(("parallel",)),
    )(br, bi, ar, ai)


def _scan_da(lr, li, xr, xi, *, cw=256):
    s, n = lr.shape
    c = SCAN_CHUNKS
    tt = s // c
    cw = min(cw, n)

    def body(lr_ref, li_ref, xr_ref, xi_ref, dar_ref, dai_ref):
        def step(t, carry):
            acc_r, acc_i = carry
            r = pl.ds(pl.multiple_of(t * c, c), c)
            rp = pl.ds(pl.multiple_of((t - 1) * c, c), c)
            l_r, l_i, p_r, p_i = lr_ref[r, :], li_ref[r, :], xr_ref[rp, :], xi_ref[rp, :]
            return acc_r + l_r * p_r + l_i * p_i, acc_i + l_i * p_r - l_r * p_i

        zero = jnp.zeros((c, cw), F32)
        acc_r, acc_i = lax.fori_loop(1, tt, step, (zero, zero))
        last = pl.ds((tt - 1) * c, c)
        rows_id = lax.broadcasted_iota(jnp.int32, (c, cw), 0)
        p_r = jnp.where(rows_id == 0, 0.0, pltpu.roll(xr_ref[last, :], 1, 0))
        p_i = jnp.where(rows_id == 0, 0.0, pltpu.roll(xi_ref[last, :], 1, 0))
        first = pl.ds(0, c)
        l_r, l_i = lr_ref[first, :], li_ref[first, :]
        acc_r = acc_r + l_r * p_r + l_i * p_i
        acc_i = acc_i + l_i * p_r - l_r * p_i
        dar_ref[...] = jnp.sum(acc_r, axis=0, keepdims=True)
        dai_ref[...] = jnp.sum(acc_i, axis=0, keepdims=True)

    col = lambda j: (0, j)
    return pl.pallas_call(
        body, name="s5_scan_da", grid=(n // cw,),
        in_specs=[pl.BlockSpec((s, cw), col)] * 4,
        out_specs=[pl.BlockSpec((1, cw), col)] * 2,
        out_shape=[jax.ShapeDtypeStruct((1, n), F32)] * 2,
        compiler_params=_params(("parallel",)),
    )(lr, li, xr, xi)


def _mesh_place():
    x, y, c = lax.axis_index("x"), lax.axis_index("y"), lax.axis_index("c")
    peers = []
    for k in range(1, N_DEV):
        px, py, pc = x ^ ((k >> 2) & 1), y ^ ((k >> 1) & 1), c ^ (k & 1)
        peers.append(((px, py, pc), 4 * px + 2 * py + pc))
    return 4 * x + 2 * y + c, peers


class _Exchange:
    def __init__(self, arrays, rows, *, gather, name, after=None):
        self.n_arr, self.rows, self.gather, self.name = len(arrays), rows, gather, name
        n_arr = self.n_arr
        if gather:
            assert all(r % BF16_ROWS == 0 for r in rows)
            lands = [lax.empty((N_DEV * r, a.shape[1]), a.dtype) for a, r in zip(arrays, rows)]
        else:
            lands = [lax.empty((N_DEV, a.shape[0] if st is None else n, a.shape[1]), a.dtype)
                     for a, (st, n) in zip(arrays, rows)]
        has_after = after is not None

        def body(*refs):
            ins, zones = refs[:n_arr], refs[n_arr:2 * n_arr]
            sems = refs[2 * n_arr + has_after:4 * n_arr + has_after]
            token, local_sems = refs[-2], refs[-1]
            me, peers = _mesh_place()
            own = [pltpu.make_async_copy(self._src(ins, i, me), self._dst(zones, i, me), local_sems.at[i])
                   for i in range(n_arr)]
            sends = [pltpu.make_async_remote_copy(
                src_ref=self._src(ins, i, pid), dst_ref=self._dst(zones, i, me), send_sem=sems[2 * i],
                recv_sem=sems[2 * i + 1], device_id=pxyz, device_id_type=pl.DeviceIdType.MESH)
                for i in range(n_arr) for pxyz, pid in peers]
            for cp in own + sends:
                cp.start()
            for cp in own:
                cp.wait()
            token[...] = jnp.zeros_like(token)

        hbm = pl.BlockSpec(memory_space=pltpu.HBM)
        sem = pl.BlockSpec(memory_space=pltpu.SEMAPHORE)
        args = [pltpu.with_memory_space_constraint(a, pltpu.HBM) for a in list(arrays) + lands]
        res = pl.pallas_call(
            body, name=name + "_start",
            in_specs=[hbm] * (2 * n_arr) + ([pl.BlockSpec(memory_space=pl.ANY)] if has_after else []),
            out_specs=[sem] * (2 * n_arr) + [hbm] * (2 * n_arr) + [pl.BlockSpec(memory_space=pltpu.VMEM)],
            out_shape=[pltpu.SemaphoreType.DMA(())] * (2 * n_arr) + [pltpu.HBM(a.shape, a.dtype) for a in args]
            + [jax.ShapeDtypeStruct((8, LANES), F32)],
            input_output_aliases={i: 2 * n_arr + i for i in range(2 * n_arr)},
            scratch_shapes=[pltpu.SemaphoreType.DMA((n_arr,))],
            compiler_params=pltpu.CompilerParams(has_side_effects=pltpu.SideEffectType.DATAFLOW_SIDE_EFFECTING),
        )(*args, *([after] if has_after else []))
        self.sems, self.thru, self.token = res[:2 * n_arr], res[2 * n_arr:4 * n_arr], res[-1]

    def _src(self, ins, i, slot):
        if self.gather:
            return ins[i]
        stride, n = self.rows[i]
        return ins[i] if stride is None else ins[i].at[pl.ds(pl.multiple_of(slot * stride, 8), n)]

    def _dst(self, zones, i, slot):
        if self.gather:
            return zones[i].at[pl.ds(pl.multiple_of(slot * self.rows[i], BF16_ROWS), self.rows[i])]
        return zones[i].at[slot]

    def wait(self, after):
        n_arr = self.n_arr

        def body(*refs):
            zones, sems = refs[n_arr:2 * n_arr], refs[2 * n_arr:4 * n_arr]
            myself = (lax.axis_index("x"), lax.axis_index("y"), lax.axis_index("c"))
            for i in range(n_arr):
                seven = zones[i].at[pl.ds(0, (N_DEV - 1) * (self.rows[i] if self.gather else 1))]
                all_seven = pltpu.make_async_remote_copy(
                    src_ref=seven, dst_ref=seven, send_sem=sems[2 * i], recv_sem=sems[2 * i + 1],
                    device_id=myself, device_id_type=pl.DeviceIdType.MESH)
                all_seven.wait_recv()
                all_seven.wait_send()

        hbm = pl.BlockSpec(memory_space=pltpu.HBM)
        sem = pl.BlockSpec(memory_space=pltpu.SEMAPHORE)
        res = pl.pallas_call(
            body, name=self.name + "_wait",
            in_specs=[hbm] * (2 * n_arr) + [sem] * (2 * n_arr) + [pl.BlockSpec(memory_space=pl.ANY)],
            out_specs=[hbm] * (2 * n_arr), out_shape=[pltpu.HBM(a.shape, a.dtype) for a in self.thru],
            input_output_aliases={i: i for i in range(2 * n_arr)},
            compiler_params=pltpu.CompilerParams(has_side_effects=pltpu.SideEffectType.DATAFLOW_SIDE_EFFECTING),
        )(*self.thru, *self.sems, after)
        return res[n_arr:]


def _elementwise_tiles(r, c):
    if r % 128 == 0:
        return 128, c
    return r, (256 if c % 256 == 0 else c)


def _adamw_math(g, w, m, v):
    nm = ADAM_B1 * m + (1.0 - ADAM_B1) * g
    nv = ADAM_B2 * v + (1.0 - ADAM_B2) * (g * g)
    m_hat = nm / (1.0 - ADAM_B1 ** ADAM_STEP)
    v_hat = nv / (1.0 - ADAM_B2 ** ADAM_STEP)
    return -ADAM_LR * (m_hat / (jnp.sqrt(v_hat) + ADAM_EPS) + ADAM_WD * w), nm, nv


def _sum_parts(p_ref):
    g = p_ref[0].astype(F32)
    for i in range(1, N_DEV):
        g = g + p_ref[i].astype(F32)
    return g


def _sum8(parts, *, name):
    _, r, cdim = parts.shape
    tr, tc = _elementwise_tiles(r, cdim)

    def body(p_ref, g_ref):
        g_ref[...] = _sum_parts(p_ref)

    return pl.pallas_call(
        body, name=name, grid=(r // tr, cdim // tc), in_specs=[pl.BlockSpec((N_DEV, tr, tc), lambda i, j: (0, i, j))],
        out_specs=pl.BlockSpec((tr, tc), lambda i, j: (i, j)), out_shape=jax.ShapeDtypeStruct((r, cdim), F32),
        compiler_params=_params(("parallel", "parallel")),
    )(parts)


def _adamw(g, w, m, v, *, parts=None, name):
    r, cdim = w.shape
    tr, tc = _elementwise_tiles(r, cdim)
    summed = parts is not None

    def body(*refs):
        if summed:
            p_ref, w_ref, m_ref, v_ref, g_ref, d_ref, nm_ref, nv_ref = refs
            g = _sum_parts(p_ref)
            g_ref[...] = g
        else:
            gi_ref, w_ref, m_ref, v_ref, d_ref, nm_ref, nv_ref = refs
            g = gi_ref[...]
        d_ref[...], nm_ref[...], nv_ref[...] = _adamw_math(g, w_ref[...], m_ref[...], v_ref[...])

    blk = pl.BlockSpec((tr, tc), lambda i, j: (i, j))
    first = pl.BlockSpec((N_DEV, tr, tc), lambda i, j: (0, i, j)) if summed else blk
    n_out = 4 if summed else 3
    return list(pl.pallas_call(
        body, name=name, grid=(r // tr, cdim // tc), in_specs=[first, blk, blk, blk],
        out_specs=[blk] * n_out, out_shape=[jax.ShapeDtypeStruct((r, cdim), F32)] * n_out,
        compiler_params=_params(("parallel", "parallel")),
    )(parts if summed else g, w, m, v))


SHARD_ROWS_P = {n: (FF_SHARD_P if 'ffn' in n else IN_SHARD_P if n == 'w_in' else None) for n in SHARDED}


def _to_exchange_layout(name, shard):
    t = shard.T if SHARD_AXIS[name] == 1 else shard
    pad = SHARD_ROWS_P[name]
    return t if pad is None else jnp.pad(t, ((0, pad - t.shape[0]), (0, 0)))


def _expand_w_in(wt):
    wt = wt.reshape(N_DEV, IN_SHARD_P, D)[:, :IN_SHARD].reshape(IN_W, D)
    o = Q_RANK + KV_RANK
    kr1, kr2 = wt[o:o + ROPE // 2], wt[o + ROPE // 2:o + ROPE]
    z = jnp.zeros((LANES - ROPE, D), wt.dtype)
    return jnp.concatenate([wt[:o], wt[o + ROPE:], kr1, kr2, z, -kr2, kr1, z], axis=0)


def _expand_w_uq(wt):
    w = wt.reshape(H, QK, Q_RANK)
    z = jnp.zeros((H, LANES - ROPE, Q_RANK), w.dtype)
    q1, q2 = w[:, NOPE:NOPE + ROPE // 2], w[:, NOPE + ROPE // 2:]
    return jnp.concatenate([w[:, :NOPE].reshape(H * NOPE, Q_RANK),
                            jnp.concatenate([q1, q2, z], axis=1).reshape(H * LANES, Q_RANK),
                            jnp.concatenate([-q2, q1, z], axis=1).reshape(H * LANES, Q_RANK)], axis=0)


def _layout_qk_gain(g):
    g = g.reshape(QK)
    g1, g2, z = g[NOPE:NOPE + ROPE // 2], g[NOPE + ROPE // 2:], jnp.zeros((LANES - ROPE,), g.dtype)
    return jnp.stack([g[:NOPE], jnp.concatenate([g1, g2, z]), jnp.concatenate([g2, g1, z])])


def _rep16(a):
    return jnp.repeat(a, SSM_GRP, axis=0)


def _layout_ssm_in(a_re, a_im, log_dt, b_re, b_im):
    b_r = jnp.transpose(b_re, (0, 2, 1)).reshape(SSM_G * SSM_GRP, SSM_P)
    b_i = jnp.transpose(b_im, (0, 2, 1)).reshape(SSM_G * SSM_GRP, SSM_P)
    ldt = jnp.broadcast_to(log_dt.reshape(SSM_G, 1), (SSM_G, SSM_P))
    return _rep16(a_re), _rep16(a_im), _rep16(ldt), b_r, b_i


def _block_diag_b(bb):
    eye = jnp.eye(SSM_G, dtype=bb.dtype)
    return (bb.reshape(SSM_G, SSM_GRP, 1, SSM_P) * eye[:, None, :, None]).reshape(SSM_W, SSM_N)


def _block_diag_c(cc):
    eye = jnp.eye(SSM_G, dtype=cc.dtype)
    return (jnp.transpose(cc, (0, 2, 1))[:, :, None, :] * eye[:, None, :, None]).reshape(SSM_N, SSM_W)


def _time_perm(a, inverse=False):
    s, w = a.shape
    c = SCAN_CHUNKS
    if inverse:
        return jnp.transpose(a.reshape(s // c, c, w), (1, 0, 2)).reshape(s, w)
    return jnp.transpose(a.reshape(c, s // c, w), (1, 0, 2)).reshape(s, w)


class _Weights:
    def __init__(self, groups=(), landed=None):
        self.groups, self.landed = list(groups), dict(landed or {})

    def get(self, name, after):
        if name not in self.landed:
            names, exchange = next(g for g in self.groups if name in g[0])
            self.landed.update(zip(names, exchange.wait(after)))
        return self.landed[name]

    def __getitem__(self, name):
        return self.landed[name]


def _ffn_fwd(x, g, wc, tag, deps=()):
    h = _rowwise(_f_norm, [x], [g], [(D, BF16)], name=tag + "_norm", deps=deps)[0]
    gate = _mm(h, wc.get(tag + '_w_gate', h), tb=True, name=tag + "_gate")
    up = _mm(h, wc.get(tag + '_w_up', h), tb=True, name=tag + "_up")
    act = _rowwise(_f_swiglu, [gate, up], [], [(D_FFP, BF16)], ts=256, name=tag + "_act")[0]
    x_out = _mm(act, wc.get(tag + '_w_down', h), res=x, scale=0.5, name=tag + "_down")
    return x_out, (h, gate, up, act)


def _ffn_bwd(x, g, wc, saved, dx_out, tag, send):
    h, gate, up, act = saved
    w_gt, w_ut, w_d = (wc.get(tag + n, h) for n in ('_w_gate', '_w_up', '_w_down'))
    dact = _mm(dx_out, w_d, tb=True, scale=0.5, out_dtype=BF16, name=tag + "_dact")
    d_d = _mm(act, dx_out, ta=True, scale=0.5, name=tag + "_dwdown")
    token = send({tag + '_w_down': d_d})
    dgate, dup = _rowwise_bwd(_f_swiglu, [gate, up], [], [dact], row_grads={0: BF16, 1: BF16}, const_grads=[], ts=256,
                              name=tag + "_act_bwd", deps=[token])
    d_gt = _mm(dgate, h, ta=True, name=tag + "_dwgate")
    d_ut = _mm(dup, h, ta=True, name=tag + "_dwup")
    token = send({tag + '_w_gate': d_gt, tag + '_w_up': d_ut})
    dh = _mm(dgate, w_gt, name=tag + "_dh_gate")
    dh = _mm(dup, w_ut, res=dh, out_dtype=BF16, name=tag + "_dh_up")
    dx, dg = _rowwise_bwd(_f_norm, [x], [g], [dh], row_grads={0: F32}, const_grads=[0], adds={0: dx_out},
                          name=tag + "_norm_bwd", deps=[token])
    return dx, dg


def _local_step(x, mem, cos, sin, target, wc, ws, send, deps=()):
    gs = {}

    x1, sv1 = _ffn_fwd(x, ws['ffn1_norm'], wc, "ffn1", deps=deps)

    h2 = _rowwise(_f_norm, [x1], [ws['mix_norm']], [(D, BF16)], name="mix_norm")[0]
    w_in_raw, w_uq_raw = wc.get('w_in', h2), wc.get('mla_w_uq', h2)
    w_in_e = _expand_w_in(w_in_raw)
    w_uq_e = _expand_w_uq(w_uq_raw)
    proj = _mm(h2, w_in_e, tb=True, name="w_in")
    c_q, c_kv = _rowwise(_f_prep1, [proj], [ws['q_norm'], ws['kv_norm']], [(Q_RANK, BF16), (KV_RANK, BF16)], name="mla_prep1")
    qall = _mm(c_q, w_uq_e, tb=True, name="w_uq")
    kv = _mm(c_kv, wc['mla_w_ukv'], tb=True, name="w_ukv")
    kr = _rowwise(_f_kr, [proj], [], [(2 * LANES, F32)], name="mla_kr")[0]
    q, k, v = _prep2_fwd(qall, kv, kr, cos, sin, ws['qk_gq'], ws['qk_gk'])
    o_mla, lse = _attn_fwd(q, k, v)

    u = proj[:, Q_RANK + KV_RANK:Q_RANK + KV_RANK + SSM_W]
    u_p = _time_perm(u)
    disc_in = [ws['ssm_lr'], ws['ssm_li'], ws['ssm_ldt'], ws['ssm_br'], ws['ssm_bi']]
    ar16, ai16, bbr, bbi = _rowwise(_f_disc, disc_in, [], [(SSM_P, F32)] * 4, name="s5_disc")
    a_r = ar16[::SSM_GRP].reshape(1, SSM_N)
    a_i = ai16[::SSM_GRP].reshape(1, SSM_N)
    bblk_r, bblk_i = _block_diag_b(bbr).astype(BF16), _block_diag_b(bbi).astype(BF16)
    cblk_r, cblk_i = _block_diag_c(ws['ssm_cr']).astype(BF16), _block_diag_c(-ws['ssm_ci']).astype(BF16)
    bu_r = _mm(u_p, bblk_r, name="s5_bu_r", tn_cap=1024)
    bu_i = _mm(u_p, bblk_i, name="s5_bu_i", tn_cap=1024)
    xr, xi = _scan(bu_r, bu_i, a_r, a_i, reverse=False, name="s5_scan_fwd")
    yc = _mm(xr, cblk_r, name="s5_y_r", tk_cap=2048)
    yc = _mm(xi, cblk_i, res=yc, name="s5_y_i", tk_cap=2048)
    g_p = _rowwise(_f_s5_gelu, [yc, u_p], [ws['ssm_d']], [(SSM_W, F32)], name="s5_gelu")[0]
    z_p = _mm(g_p, wc['ssm_w_glu'], name="s5_glu")
    g_t, z_t = _time_perm(g_p, inverse=True), _time_perm(z_p, inverse=True)
    on_consts = [ws['ssm_b_glu'], ws['out_norm_mla'], ws['out_norm_ssm']]
    ycat = _rowwise(_f_outnorm, [o_mla, g_t, z_t], on_consts, [(D, BF16)], name="out_norm")[0]
    x2 = _mm(ycat, wc['w_o'], res=x1, name="w_o")

    hx = _rowwise(_f_norm, [x2], [ws['xattn_norm']], [(D, BF16)], name="xattn_norm")[0]
    xq = _mm(hx, wc['xattn_w_q'], name="xattn_q")
    mn = _rowwise(_f_norm, [mem], [ws['mem_norm']], [(D, BF16)], name="mem_norm")[0]
    kvm = _mm(mn, wc['xattn_w_kv'], name="xattn_kv")
    xkn, xv = _rowwise(_f_memk, [kvm], [ws['xattn_k_norm']], [(H * XH, BF16), (H * XH, BF16)], name="xattn_knorm")
    xo = _xattn_fwd(xq, xkn, xv, ws['xattn_q_norm'])
    x3 = _mm(xo, wc['xattn_w_o'], tb=True, res=x2, name="xattn_o")

    x4, sv2 = _ffn_fwd(x3, ws['ffn2_norm'], wc, "ffn2")

    def f_loss(yb, tb):
        err = yb - tb
        return err * (1.0 / D), jnp.broadcast_to(jnp.sum(jnp.sum(err * err, axis=1, keepdims=True), axis=0, keepdims=True) * (0.5 / D), (1, LANES))

    dx4, loss = _rowwise(f_loss, [x4, target], [], [(D, F32)], [(1, LANES)], name="loss")

    dx3, gs['ffn2_norm'] = _ffn_bwd(x3, ws['ffn2_norm'], wc, sv2, dx4, "ffn2", send)

    dxo = _mm(dx3, wc['xattn_w_o'], out_dtype=BF16, name="xattn_o_dx")
    send({'xattn_w_o': _mm(dx3, xo, ta=True, name="xattn_o_dw")})
    dxq, dxkn, dxv, gs['xattn_q_norm'] = _xattn_bwd(xq, xkn, xv, ws['xattn_q_norm'], dxo)
    dkvm, gs['xattn_k_norm'] = _rowwise_bwd(_f_memk, [kvm], [ws['xattn_k_norm']], [dxkn, dxv], row_grads={0: BF16},
                                            const_grads=[0], name="xattn_knorm_bwd")
    send({'xattn_w_kv': _mm(mn, dkvm, ta=True, name="xattn_kv_dw")})
    dmn = _mm(dkvm, wc['xattn_w_kv'], tb=True, out_dtype=BF16, name="xattn_kv_dx")
    gs['mem_norm'] = _rowwise_bwd(_f_norm, [mem], [ws['mem_norm']], [dmn], row_grads={}, const_grads=[0], name="mem_norm_bwd")[0]
    send({'xattn_w_q': _mm(hx, dxq, ta=True, name="xattn_q_dw")})
    dhx = _mm(dxq, wc['xattn_w_q'], tb=True, out_dtype=BF16, name="xattn_q_dx")
    dx2, gs['xattn_norm'] = _rowwise_bwd(_f_norm, [x2], [ws['xattn_norm']], [dhx], row_grads={0: F32}, const_grads=[0],
                                         adds={0: dx3}, name="xattn_norm_bwd")

    dycat = _mm(dx2, wc['w_o'], tb=True, out_dtype=BF16, name="w_o_dx")
    send({'w_o': _mm(ycat, dx2, ta=True, name="w_o_dw")})
    do_mla, dg_t, dz_t, gs['ssm_b_glu'], gs['out_norm_mla'], gs['out_norm_ssm'] = _rowwise_bwd(
        _f_outnorm, [o_mla, g_t, z_t], on_consts, [dycat], row_grads={0: F32, 1: F32, 2: BF16}, const_grads=[0, 1, 2],
        name="out_norm_bwd")

    dz_p, dg_p = _time_perm(dz_t), _time_perm(dg_t)
    send({'ssm_w_glu': _mm(g_p, dz_p, ta=True, name="s5_glu_dw")})
    dg_p = _mm(dz_p, wc['ssm_w_glu'], tb=True, res=dg_p, name="s5_glu_dx")
    dyc, du_d, gs['ssm_d'] = _rowwise_bwd(_f_s5_gelu, [yc, u_p], [ws['ssm_d']], [dg_p], row_grads={0: BF16, 1: F32},
                                          const_grads=[0], name="s5_gelu_bwd")
    d_cblk_r = _mm(xr, dyc, ta=True, name="s5_dc_r")
    d_cblk_i = _mm(xi, dyc, ta=True, name="s5_dc_i")
    dxr = _mm(dyc, cblk_r, tb=True, name="s5_dx_r", tn_cap=1024)
    dxi = _mm(dyc, cblk_i, tb=True, name="s5_dx_i", tn_cap=1024)
    lam_r, lam_i = _scan(dxr, dxi, a_r, -a_i, reverse=True, name="s5_scan_bwd")
    d_ar, d_ai = _scan_da(lam_r, lam_i, xr, xi)
    d_bblk_r = _mm(u_p, lam_r, ta=True, name="s5_db_r", tn_cap=1024)
    d_bblk_i = _mm(u_p, lam_i, ta=True, name="s5_db_i", tn_cap=1024)
    du_p = _mm(lam_r, bblk_r, tb=True, res=du_d, name="s5_du_r", tk_cap=2048)
    du_p = _mm(lam_i, bblk_i, tb=True, res=du_p, name="s5_du_i", tk_cap=2048)
    du = _time_perm(du_p, inverse=True)
    gs['ssm_cr'] = jax.linear_transpose(_block_diag_c, ws['ssm_cr'])(d_cblk_r)[0]
    gs['ssm_ci'] = -jax.linear_transpose(_block_diag_c, ws['ssm_ci'])(d_cblk_i)[0]
    d_bbr = jax.linear_transpose(_block_diag_b, bbr)(d_bblk_r)[0]
    d_bbi = jax.linear_transpose(_block_diag_b, bbi)(d_bblk_i)[0]
    d_ar16 = jnp.zeros((SSM_G * SSM_GRP, SSM_P), F32).at[::SSM_GRP].set(d_ar.reshape(SSM_G, SSM_P))
    d_ai16 = jnp.zeros((SSM_G * SSM_GRP, SSM_P), F32).at[::SSM_GRP].set(d_ai.reshape(SSM_G, SSM_P))
    gs['ssm_lr'], gs['ssm_li'], gs['ssm_ldt'], gs['ssm_br'], gs['ssm_bi'] = _rowwise_bwd(
        _f_disc, disc_in, [], [d_ar16, d_ai16, d_bbr, d_bbi], row_grads={i: F32 for i in range(5)}, const_grads=[],
        name="s5_disc_bwd")

    delta, do_b = _rowwise(_f_delta, [do_mla, o_mla], [], [(H * LANES, F32), (H * VD, BF16)], name="mla_delta")
    dq, dk, dv = _attn_bwd(q, k, v, do_b, lse, delta)
    dqall, dkv, dkr, gs['qk_gq'], gs['qk_gk'] = _prep2_bwd(qall, kv, kr, cos, sin, ws['qk_gq'], ws['qk_gk'], dq, dk, dv)
    d_w_uq_e = _mm(dqall, c_q, ta=True, name="w_uq_dw")
    send({'mla_w_uq': jax.linear_transpose(_expand_w_uq, jax.ShapeDtypeStruct(w_uq_raw.shape, F32))(d_w_uq_e)[0]})
    dc_q = _mm(dqall, w_uq_e, out_dtype=BF16, name="w_uq_dx")
    send({'mla_w_ukv': _mm(dkv, c_kv, ta=True, name="w_ukv_dw")})
    dc_kv = _mm(dkv, wc['mla_w_ukv'], out_dtype=BF16, name="w_ukv_dx")

    def f_prep1_bwd(pb, dcq, dckv, dub, dkrb, gq, gkv):
        _, vjp = jax.vjp(_f_prep1, pb[:, :Q_RANK + KV_RANK], gq, gkv)
        dpa, dgq, dgkv = vjp((dcq.astype(BF16), dckv.astype(BF16)))
        return jnp.concatenate([dpa, dub, dkrb], axis=-1), dgq, dgkv

    dproj, gs['q_norm'], gs['kv_norm'] = _rowwise(
        f_prep1_bwd, [proj, dc_q, dc_kv, du, dkr], [ws['q_norm'], ws['kv_norm']], [(IN_WP, BF16)],
        [(1, Q_RANK), (1, KV_RANK)], name="mla_prep1_bwd")
    d_w_in_e = _mm(dproj, h2, ta=True, name="w_in_dw")
    token = send({'w_in': jax.linear_transpose(_expand_w_in, jax.ShapeDtypeStruct(w_in_raw.shape, F32))(d_w_in_e)[0]})
    dh2 = _mm(dproj, w_in_e, out_dtype=BF16, name="w_in_dx")
    dx1, gs['mix_norm'] = _rowwise_bwd(_f_norm, [x1], [ws['mix_norm']], [dh2], row_grads={0: F32}, const_grads=[0],
                                       adds={0: dx2}, name="mix_norm_bwd", deps=[token])

    dx0, gs['ffn1_norm'] = _ffn_bwd(x, ws['ffn1_norm'], wc, sv1, dx1, "ffn1", send)
    return loss, dx0, gs


def _prep2_fwd(qall, kv, kr, cos, sin, gq, gk):
    return _rowwise(_f_prep2, [qall, kv, kr, cos, sin], [gq, gk], [(H * HQ, BF16), (H * HQ, BF16), (H * VD, BF16)],
                    ts=256, name="mla_prep2")


def _prep2_bwd(qall, kv, kr, cos, sin, gq, gk, dq, dk, dv):
    return _rowwise_bwd(_f_prep2, [qall, kv, kr, cos, sin], [gq, gk], [dq, dk, dv], row_grads={0: BF16, 1: BF16, 2: F32},
                        const_grads=[0, 1], ts=256, name="mla_prep2_bwd")


def _rope_tables(pos):
    half = ROPE // 2
    inv = ROPE_THETA ** (-jnp.arange(half, dtype=F32) / half)
    ang = pos.astype(F32)[:, None] * inv[None, :]
    z = jnp.zeros((pos.shape[0], LANES - ROPE), F32)
    cos, sin = jnp.cos(ang), jnp.sin(ang)
    return jnp.concatenate([cos, cos, z], axis=-1), jnp.concatenate([sin, sin, z], axis=-1)


def _small_layout(p):
    lr, li, ldt, br, bi = _layout_ssm_in(p['ssm_a_re'], p['ssm_a_im'], p['ssm_log_dt'], p['ssm_b_re'], p['ssm_b_im'])
    return {
        'ffn1_norm': p['ffn1_norm'].reshape(1, D), 'mix_norm': p['mix_norm'].reshape(1, D),
        'q_norm': p['mla_q_norm'].reshape(1, Q_RANK), 'kv_norm': p['mla_kv_norm'].reshape(1, KV_RANK),
        'qk_gq': _layout_qk_gain(p['mla_qk_norm_q']), 'qk_gk': _layout_qk_gain(p['mla_qk_norm_k']),
        'ssm_lr': lr, 'ssm_li': li, 'ssm_ldt': ldt, 'ssm_br': br, 'ssm_bi': bi,
        'ssm_cr': p['ssm_c_re'], 'ssm_ci': p['ssm_c_im'], 'ssm_d': p['ssm_d'].reshape(1, SSM_W),
        'ssm_b_glu': p['ssm_b_glu'].reshape(1, SSM_W),
        'out_norm_mla': p['out_norm_mla'].reshape(1, SSM_W), 'out_norm_ssm': p['out_norm_ssm'].reshape(1, SSM_W),
        'xattn_norm': p['xattn_norm'].reshape(1, D), 'mem_norm': p['mem_norm'].reshape(1, D),
        'xattn_q_norm': p['xattn_q_norm'].reshape(1, XH), 'xattn_k_norm': p['xattn_k_norm'].reshape(1, XH),
        'ffn2_norm': p['ffn2_norm'].reshape(1, D),
    }


def _pack(arrs, rows):
    flat = jnp.concatenate([a.reshape(-1) for a in arrs])
    return jnp.pad(flat, (0, rows * D - flat.shape[0])).reshape(rows, D)


def _unpack(flat, shapes):
    flat = flat.reshape(-1)
    out, off = [], 0
    for sh in shapes:
        n = int(np.prod(sh))
        out.append(flat[off:off + n].reshape(sh))
        off += n
    return out


def kernel(x, mem, positions, ffn1_norm, ffn1_w_gate, ffn1_w_up, ffn1_w_down, mix_norm, w_in, mla_q_norm, mla_w_uq, mla_kv_norm, mla_w_ukv, mla_qk_norm_q, mla_qk_norm_k, ssm_a_re, ssm_a_im, ssm_log_dt, ssm_b_re, ssm_b_im, ssm_c_re, ssm_c_im, ssm_d, ssm_w_glu, ssm_b_glu, out_norm_mla, out_norm_ssm, w_o, xattn_norm, mem_norm, xattn_w_q, xattn_w_kv, xattn_q_norm, xattn_k_norm, xattn_w_o, ffn2_norm, ffn2_w_gate, ffn2_w_up, ffn2_w_down, loss_target, m_ffn1_norm, m_ffn1_w_gate, m_ffn1_w_up, m_ffn1_w_down, m_mix_norm, m_w_in, m_mla_q_norm, m_mla_w_uq, m_mla_kv_norm, m_mla_w_ukv, m_mla_qk_norm_q, m_mla_qk_norm_k, m_ssm_a_re, m_ssm_a_im, m_ssm_log_dt, m_ssm_b_re, m_ssm_b_im, m_ssm_c_re, m_ssm_c_im, m_ssm_d, m_ssm_w_glu, m_ssm_b_glu, m_out_norm_mla, m_out_norm_ssm, m_w_o, m_xattn_norm, m_mem_norm, m_xattn_w_q, m_xattn_w_kv, m_xattn_q_norm, m_xattn_k_norm, m_xattn_w_o, m_ffn2_norm, m_ffn2_w_gate, m_ffn2_w_up, m_ffn2_w_down, v_ffn1_norm, v_ffn1_w_gate, v_ffn1_w_up, v_ffn1_w_down, v_mix_norm, v_w_in, v_mla_q_norm, v_mla_w_uq, v_mla_kv_norm, v_mla_w_ukv, v_mla_qk_norm_q, v_mla_qk_norm_k, v_ssm_a_re, v_ssm_a_im, v_ssm_log_dt, v_ssm_b_re, v_ssm_b_im, v_ssm_c_re, v_ssm_c_im, v_ssm_d, v_ssm_w_glu, v_ssm_b_glu, v_out_norm_mla, v_out_norm_ssm, v_w_o, v_xattn_norm, v_mem_norm, v_xattn_w_q, v_xattn_w_kv, v_xattn_q_norm, v_xattn_k_norm, v_xattn_w_o, v_ffn2_norm, v_ffn2_w_gate, v_ffn2_w_up, v_ffn2_w_down):
    args = dict(locals())
    w = {n: args[n] for n in WEIGHTS}
    mom = {n: args['m_' + n] for n in WEIGHTS}
    var = {n: args['v_' + n] for n in WEIGHTS}
    return _step(x, mem, positions, loss_target, w, mom, var)


GATHER_GROUPS = [('ffn1', ['ffn1_w_gate', 'ffn1_w_up', 'ffn1_w_down']),
                 ('mix', ['w_in', 'mla_w_uq', 'mla_w_ukv', 'ssm_w_glu', 'w_o', 'xattn_w_q', 'xattn_w_kv', 'xattn_w_o']),
                 ('ffn2', ['ffn2_w_gate', 'ffn2_w_up', 'ffn2_w_down'])]
SCATTER_GROUPS = [('ffn2', ['ffn2_w_down', 'ffn2_w_gate', 'ffn2_w_up']),
                  ('mix', ['xattn_w_o', 'xattn_w_kv', 'xattn_w_q', 'w_o', 'ssm_w_glu', 'mla_w_uq', 'mla_w_ukv', 'w_in']),
                  ('ffn1_down', ['ffn1_w_down']), ('ffn1_gu', ['ffn1_w_gate', 'ffn1_w_up'])]


def _step(x, mem, positions, loss_target, w, mom, var):
    blocks = {n: _to_exchange_layout(n, w[n][0]).astype(BF16) for n in SHARDED}
    gathers, token = [], None
    for tag, names in GATHER_GROUPS:
        ex = _Exchange([blocks[n] for n in names], [blocks[n].shape[0] for n in names], gather=True,
                       name="gather_" + tag, after=token)
        gathers.append((names, ex))
        token = ex.token
    wc = _Weights(gathers)

    rows = {n: (blocks[n].shape[0], blocks[n].shape[0] if SHARD_ROWS_P[n] is None else w[n][0].shape[SHARD_AXIS[n]])
            for n in SHARDED}
    ready, scatters = {}, []

    def send(grads):
        ready.update(grads)
        for tag, names in SCATTER_GROUPS:
            if all(n in ready for n in names) and not any(t == tag for t, _, _ in scatters):
                ex = _Exchange([ready[n] for n in names], [rows[n] for n in names], gather=False, name="scatter_" + tag)
                scatters.append((tag, names, ex))
                return ex.token
        return None

    small = {n: w[n][0] for n in SMALL}
    ws = _small_layout(small)
    cos, sin = _rope_tables(positions[0])
    loss, dx, gs = _local_step(x[0], mem[0], cos, sin, loss_target[0], wc, ws, send, deps=[token])

    g_small = jax.linear_transpose(_small_layout, {n: jax.ShapeDtypeStruct(small[n].shape, F32) for n in SMALL})(gs)[0]
    small_shapes = [small[n].shape for n in SMALL]
    n_small = sum(int(np.prod(sh)) for sh in small_shapes) + 1
    rows_small = -(-n_small // (8 * D)) * 8
    small_pack = _pack([g_small[n] for n in SMALL] + [loss[0, :1]], rows_small)
    small_ex = _Exchange([small_pack], [(None, rows_small)], gather=False, name="scatter_small")

    out = {}
    for _, names, ex in scatters:
        for n, p in zip(names, ex.wait(dx)):
            if SHARD_AXIS[n] == 0:
                out[n] = _adamw(None, w[n][0], mom[n][0], var[n][0], parts=p, name="adamw_" + n)
            else:
                g = _sum8(p, name="sum_" + n).T
                out[n] = [g] + _adamw(g, w[n][0], mom[n][0], var[n][0], name="adamw_" + n)
    state = [_pack([t[n][0] for n in SMALL], rows_small) for t in (w, mom, var)]
    small_out = _adamw(None, *state, parts=small_ex.wait(dx)[0], name="adamw_small")
    loss_total = small_out[0].reshape(-1)[n_small - 1]
    for n, vals in zip(SMALL, zip(*[_unpack(flat, small_shapes) for flat in small_out])):
        out[n] = vals
    outs = [out[n][i][None] for i in range(4) for n in WEIGHTS]
    return (loss_total, dx[None], *outs)
```

```python
import math

import jax
import jax.numpy as jnp
import numpy as np
from jax import lax
from jax.experimental import pallas as pl
from jax.experimental.pallas import tpu as pltpu

F32 = jnp.float32
BF16 = jnp.bfloat16

N_DEV = 8
D = 1024
D_FF = 2752
D_FFP = 2816
MEM_LEN = 256
H = 4
Q_RANK, KV_RANK, NOPE, ROPE, VD = 384, 256, 128, 64, 128
QK = NOPE + ROPE
HQ = 2 * 128
SSM_W, SSM_G, SSM_GRP, SSM_P = 512, 32, 16, 64
SSM_N = SSM_G * SSM_P
IN_W = 1216
IN_WP = 1408
XH = 128
EPS = 1e-6
ROPE_THETA = 10000.0
SCAN_CHUNKS = 8

ADAM_LR, ADAM_B1, ADAM_B2, ADAM_EPS, ADAM_WD, ADAM_STEP = 0.001, 0.9, 0.999, 1e-08, 0.01, 10

VMEM_LIMIT = 56 * 1024 * 1024
ACC_BYTES = 6 * 1024 * 1024
LANES = 128
BF16_ROWS = 16
FF_SHARD = D_FF // N_DEV
FF_SHARD_P = 352
IN_SHARD = IN_W // N_DEV
IN_SHARD_P = 160

WEIGHTS = ['ffn1_norm', 'ffn1_w_gate', 'ffn1_w_up', 'ffn1_w_down', 'mix_norm', 'w_in', 'mla_q_norm', 'mla_w_uq',
           'mla_kv_norm', 'mla_w_ukv', 'mla_qk_norm_q', 'mla_qk_norm_k', 'ssm_a_re', 'ssm_a_im', 'ssm_log_dt',
           'ssm_b_re', 'ssm_b_im', 'ssm_c_re', 'ssm_c_im', 'ssm_d', 'ssm_w_glu', 'ssm_b_glu', 'out_norm_mla',
           'out_norm_ssm', 'w_o', 'xattn_norm', 'mem_norm', 'xattn_w_q', 'xattn_w_kv', 'xattn_q_norm',
           'xattn_k_norm', 'xattn_w_o', 'ffn2_norm', 'ffn2_w_gate', 'ffn2_w_up', 'ffn2_w_down']
SHARD_AXIS = {'ffn1_w_gate': 1, 'ffn1_w_up': 1, 'ffn1_w_down': 0, 'w_in': 1, 'mla_w_uq': 1, 'mla_w_ukv': 1,
              'ssm_w_glu': 0, 'w_o': 0, 'xattn_w_q': 0, 'xattn_w_kv': 0, 'xattn_w_o': 1,
              'ffn2_w_gate': 1, 'ffn2_w_up': 1, 'ffn2_w_down': 0}
SHARDED = [n for n in WEIGHTS if n in SHARD_AXIS]
SMALL = [n for n in WEIGHTS if n not in SHARD_AXIS]


def _params(sem=None):
    return pltpu.CompilerParams(dimension_semantics=sem, vmem_limit_bytes=VMEM_LIMIT)


def _tile(n, cap):
    if n <= cap:
        return n
    best = n
    for t in range(LANES, cap + 1, LANES):
        if n % t == 0:
            best = t
    return best


def _mm(a, b, *, ta=False, tb=False, out_dtype=F32, res=None, scale=1.0, name, tm_cap=512, tn_cap=1408, tk_cap=2816):
    m, k = (a.shape[1], a.shape[0]) if ta else a.shape
    k2, n = (b.shape[1], b.shape[0]) if tb else b.shape
    assert k == k2, (a.shape, b.shape, ta, tb)
    if ta:
        tk_cap = min(tk_cap, 512)
        tm_cap = 1408
    tm, tn, tk = _tile(m, tm_cap), _tile(n, tn_cap), _tile(k, tk_cap)
    if tm * tn * 4 > ACC_BYTES:
        tn = _tile(n, max(LANES, ACC_BYTES // (4 * tm) // LANES * LANES))
    nk = k // tk
    dims = (((0 if ta else 1,), (1 if tb else 0,)), ((), ()))
    has_res = res is not None

    def body(*refs):
        if has_res:
            a_ref, b_ref, r_ref, o_ref, acc_ref = refs
        else:
            a_ref, b_ref, o_ref, acc_ref = refs
        kk = pl.program_id(2)

        @pl.when(kk == 0)
        def _():
            acc_ref[...] = jnp.zeros_like(acc_ref)

        acc_ref[...] += lax.dot_general(a_ref[...].astype(BF16), b_ref[...].astype(BF16), dims,
                                        preferred_element_type=F32)

        @pl.when(kk == nk - 1)
        def _():
            out = acc_ref[...]
            if scale != 1.0:
                out = out * scale
            if has_res:
                out = out + r_ref[...].astype(F32)
            o_ref[...] = out.astype(o_ref.dtype)

    a_spec = pl.BlockSpec((tk, tm), lambda i, j, kk: (kk, i)) if ta else pl.BlockSpec((tm, tk), lambda i, j, kk: (i, kk))
    b_spec = pl.BlockSpec((tn, tk), lambda i, j, kk: (j, kk)) if tb else pl.BlockSpec((tk, tn), lambda i, j, kk: (kk, j))
    o_spec = pl.BlockSpec((tm, tn), lambda i, j, kk: (i, j))
    in_specs = [a_spec, b_spec] + ([o_spec] if has_res else [])
    args = (a, b) + ((res,) if has_res else ())
    return pl.pallas_call(
        body, name=name, grid=(m // tm, n // tn, nk), in_specs=in_specs, out_specs=o_spec,
        out_shape=jax.ShapeDtypeStruct((m, n), out_dtype), scratch_shapes=[pltpu.VMEM((tm, tn), F32)],
        compiler_params=_params(("parallel", "parallel", "arbitrary")),
    )(*args)


def _rowwise(fn, rows, consts, outs, accs=(), *, ts=512, name, deps=()):
    s = rows[0].shape[0]
    ts = min(ts, s)
    assert s % ts == 0
    n_rows, n_consts, n_outs = len(rows), len(consts), len(outs)
    deps = [d for d in deps if d is not None]
    consts = list(consts) + deps

    def body(*refs):
        ins = [r[...] for r in refs[:n_rows + n_consts]]
        res = fn(*ins)
        res = tuple(res) if isinstance(res, (tuple, list)) else (res,)
        out_refs = refs[n_rows + len(consts):]
        for o_ref, val in zip(out_refs[:n_outs], res[:n_outs]):
            o_ref[...] = val.astype(o_ref.dtype)
        if accs:
            first = pl.program_id(0) == 0

            @pl.when(first)
            def _():
                for a_ref, val in zip(out_refs[n_outs:], res[n_outs:]):
                    a_ref[...] = val.astype(F32)

            @pl.when(jnp.logical_not(first))
            def _():
                for a_ref, val in zip(out_refs[n_outs:], res[n_outs:]):
                    a_ref[...] += val.astype(F32)

    in_specs = [pl.BlockSpec((ts, r.shape[1]), lambda i: (i, 0)) for r in rows]
    in_specs += [pl.BlockSpec(c.shape, lambda i: (0, 0)) for c in consts]
    out_specs = [pl.BlockSpec((ts, w), lambda i: (i, 0)) for w, _ in outs]
    out_specs += [pl.BlockSpec(tuple(sh), lambda i: (0, 0)) for sh in accs]
    out_shape = [jax.ShapeDtypeStruct((s, w), dt) for w, dt in outs]
    out_shape += [jax.ShapeDtypeStruct(tuple(sh), F32) for sh in accs]
    res = pl.pallas_call(
        body, name=name, grid=(s // ts,), in_specs=in_specs, out_specs=out_specs, out_shape=out_shape,
        compiler_params=_params(("arbitrary",)),
    )(*rows, *consts)
    return res


def _rowwise_bwd(f, rows, consts, cts, *, row_grads, const_grads, adds=None, ts=512, name, deps=()):
    adds = adds or {}
    n_rows, n_consts, n_cts = len(rows), len(consts), len(cts)
    add_keys = sorted(adds)
    rg = sorted(row_grads)
    cg = sorted(const_grads)

    def fn(*args):
        r = args[:n_rows]
        c = args[n_rows:n_rows + n_consts]
        ct = args[n_rows + n_consts:n_rows + n_consts + n_cts]
        extra = args[n_rows + n_consts + n_cts:]
        outs, vjp = jax.vjp(f, *r, *c)
        outs = tuple(outs) if isinstance(outs, (tuple, list)) else (outs,)
        cot = tuple(g.astype(o.dtype) for g, o in zip(ct, outs))
        grads = vjp(cot if len(cot) > 1 else cot[0])
        res = []
        for i in rg:
            g = grads[i].astype(F32)
            if i in adds:
                g = g + extra[add_keys.index(i)].astype(F32)
            res.append(g)
        for i in cg:
            res.append(grads[n_rows + i])
        return tuple(res)

    rows_all = list(rows) + list(cts) + [adds[i] for i in add_keys]
    def fn2(*args):
        nr = len(rows_all)
        rr, cc = args[:nr], args[nr:]
        return fn(*rr[:n_rows], *cc, *rr[n_rows:])

    outs = [(rows[i].shape[1], row_grads[i]) for i in rg]
    accs = [consts[i].shape for i in cg]
    return _rowwise(fn2, rows_all, list(consts), outs, accs, ts=ts, name=name, deps=deps)


def _rms(x, g):
    xf = x.astype(F32)
    return xf * lax.rsqrt(jnp.mean(xf * xf, axis=-1, keepdims=True) + EPS) * g.astype(F32)


def _sigmoid(x):
    return 1.0 / (1.0 + jnp.exp(-x))


def _f_norm(x, g):
    return _rms(x, g).astype(BF16)


def _f_swiglu(gate, up):
    gate, up = gate.astype(F32), up.astype(F32)
    return (gate * _sigmoid(gate) * up).astype(BF16)


def _f_prep1(proj, gq, gkv):
    return _rms(proj[:, :Q_RANK], gq).astype(BF16), _rms(proj[:, Q_RANK:Q_RANK + KV_RANK], gkv).astype(BF16)


def _f_kr(proj):
    return (proj[:, Q_RANK + KV_RANK + SSM_W:],)


def _f_prep2(qall, kv, kr2, cos, sin, gq, gk):
    kr, krs = kr2[:, :LANES].astype(F32), kr2[:, LANES:].astype(F32)
    k_rot = kr * gk[1:2] * cos + krs * gk[2:3] * sin
    k_ss = jnp.sum(kr * kr, axis=-1, keepdims=True)
    q_scale = QK ** -0.5
    qs, ks, vs = [], [], []
    for h in range(H):
        qn = qall[:, h * LANES:(h + 1) * LANES].astype(F32)
        qr = qall[:, (H + h) * LANES:(H + h + 1) * LANES].astype(F32)
        qrs = qall[:, (2 * H + h) * LANES:(2 * H + h + 1) * LANES].astype(F32)
        rstd = lax.rsqrt((jnp.sum(qn * qn, axis=-1, keepdims=True) + jnp.sum(qr * qr, axis=-1, keepdims=True)) / QK + EPS)
        rstd = rstd * q_scale
        qs += [qn * gq[0:1] * rstd, (qr * gq[1:2] * cos + qrs * gq[2:3] * sin) * rstd]
        kn = kv[:, 2 * h * LANES:(2 * h + 1) * LANES].astype(F32)
        rstd_k = lax.rsqrt((jnp.sum(kn * kn, axis=-1, keepdims=True) + k_ss) / QK + EPS)
        ks += [kn * gk[0:1] * rstd_k, k_rot * rstd_k]
        vs.append(kv[:, (2 * h + 1) * LANES:(2 * h + 2) * LANES])
    return (jnp.concatenate(qs, axis=-1).astype(BF16), jnp.concatenate(ks, axis=-1).astype(BF16),
            jnp.concatenate(vs, axis=-1).astype(BF16))


def _gelu(x):
    return 0.5 * x * (1.0 + jnp.tanh(math.sqrt(2.0 / math.pi) * (x + 0.044715 * (x * x * x))))


def _f_s5_gelu(yc, u, d):
    return _gelu(yc.astype(F32) + d * u.astype(F32))


def _f_outnorm(o_mla, g, z, b_glu, g_om, g_os):
    y_ssm = g * _sigmoid(z + b_glu)
    return jnp.concatenate([_rms(o_mla, g_om), _rms(y_ssm, g_os)], axis=-1).astype(BF16)


def _f_memk(kvm, gk):
    ks = [_rms(kvm[:, h * XH:(h + 1) * XH], gk) for h in range(H)]
    return jnp.concatenate(ks, axis=-1).astype(BF16), kvm[:, H * XH:].astype(BF16)


def _f_disc(lr, li, log_dt, br, bi):
    dt = jnp.exp(log_dt)
    decay = jnp.exp(lr * dt)
    ar = decay * jnp.cos(li * dt)
    ai = decay * jnp.sin(li * dt)
    den = lr * lr + li * li
    nr = ar - 1.0
    coef_r = (nr * lr + ai * li) / den
    coef_i = (ai * lr - nr * li) / den
    return ar, ai, coef_r * br - coef_i * bi, coef_r * bi + coef_i * br


def _causal_mask(i, j, tq, tk):
    qpos = i * tq + lax.broadcasted_iota(jnp.int32, (tq, tk), 0)
    kpos = j * tk + lax.broadcasted_iota(jnp.int32, (tq, tk), 1)
    return qpos >= kpos


def _attn_fwd(q, k, v, *, t=512):
    s = q.shape[0]
    t = min(t, s)
    nb = s // t

    def body(q_ref, k_ref, v_ref, o_ref, lse_ref, m_sc, l_sc, acc_sc):
        i, j = pl.program_id(1), pl.program_id(2)

        @pl.when(j == 0)
        def _():
            m_sc[...] = jnp.full_like(m_sc, -jnp.inf)
            l_sc[...] = jnp.zeros_like(l_sc)
            acc_sc[...] = jnp.zeros_like(acc_sc)

        @pl.when(j <= i)
        def _():
            sc = lax.dot_general(q_ref[...], k_ref[...], (((1,), (1,)), ((), ())), preferred_element_type=F32)
            sc = jnp.where(_causal_mask(i, j, t, t), sc, -jnp.inf)
            m_old = m_sc[...]
            m_new = jnp.maximum(m_old, jnp.max(sc, axis=-1, keepdims=True))
            p = jnp.exp(sc - m_new)
            alpha = jnp.exp(m_old - m_new)
            l_sc[...] = alpha * l_sc[...] + jnp.sum(p, axis=-1, keepdims=True)
            acc_sc[...] = alpha * acc_sc[...] + jnp.dot(p.astype(BF16), v_ref[...], preferred_element_type=F32)
            m_sc[...] = m_new

        @pl.when(j == i)
        def _():
            o_ref[...] = acc_sc[...] / l_sc[...]
            lse_ref[...] = jnp.broadcast_to(m_sc[...] + jnp.log(l_sc[...]), lse_ref.shape)

    kv_map = lambda h, i, j: (jnp.minimum(j, i), h)
    return pl.pallas_call(
        body, name="mla_attn_fwd", grid=(H, nb, nb),
        in_specs=[pl.BlockSpec((t, HQ), lambda h, i, j: (i, h)), pl.BlockSpec((t, HQ), kv_map),
                  pl.BlockSpec((t, VD), kv_map)],
        out_specs=[pl.BlockSpec((t, VD), lambda h, i, j: (i, h)), pl.BlockSpec((t, LANES), lambda h, i, j: (i, h))],
        out_shape=[jax.ShapeDtypeStruct((s, H * VD), F32), jax.ShapeDtypeStruct((s, H * LANES), F32)],
        scratch_shapes=[pltpu.VMEM((t, 1), F32), pltpu.VMEM((t, 1), F32), pltpu.VMEM((t, VD), F32)],
        compiler_params=_params(("parallel", "parallel", "arbitrary")),
    )(q, k, v)


def _attn_probs(q_ref, k_ref, v_ref, do_ref, lse_ref, dl_ref, i, j, t):
    sc = lax.dot_general(q_ref[...], k_ref[...], (((1,), (1,)), ((), ())), preferred_element_type=F32)
    p = jnp.where(_causal_mask(i, j, t, t), jnp.exp(sc - lse_ref[...][:, :1]), 0.0)
    dp = lax.dot_general(do_ref[...], v_ref[...], (((1,), (1,)), ((), ())), preferred_element_type=F32)
    ds = p * (dp - dl_ref[...][:, :1])
    return p, ds


def _attn_bwd(q, k, v, do, lse, delta, *, t=512):
    s = q.shape[0]
    t = min(t, s)
    nb = s // t

    def dq_body(q_ref, k_ref, v_ref, do_ref, lse_ref, dl_ref, dq_ref, acc_sc):
        i, j = pl.program_id(1), pl.program_id(2)

        @pl.when(j == 0)
        def _():
            acc_sc[...] = jnp.zeros_like(acc_sc)

        @pl.when(j <= i)
        def _():
            _, ds = _attn_probs(q_ref, k_ref, v_ref, do_ref, lse_ref, dl_ref, i, j, t)
            acc_sc[...] += jnp.dot(ds.astype(BF16), k_ref[...], preferred_element_type=F32)

        @pl.when(j == i)
        def _():
            dq_ref[...] = acc_sc[...]

    q_map = lambda h, i, j: (i, h)
    kv_map = lambda h, i, j: (jnp.minimum(j, i), h)
    dq = pl.pallas_call(
        dq_body, name="mla_attn_dq", grid=(H, nb, nb),
        in_specs=[pl.BlockSpec((t, HQ), q_map), pl.BlockSpec((t, HQ), kv_map), pl.BlockSpec((t, VD), kv_map),
                  pl.BlockSpec((t, VD), q_map), pl.BlockSpec((t, LANES), q_map), pl.BlockSpec((t, LANES), q_map)],
        out_specs=pl.BlockSpec((t, HQ), q_map),
        out_shape=jax.ShapeDtypeStruct((s, H * HQ), F32),
        scratch_shapes=[pltpu.VMEM((t, HQ), F32)],
        compiler_params=_params(("parallel", "parallel", "arbitrary")),
    )(q, k, v, do, lse, delta)

    def dkv_body(q_ref, k_ref, v_ref, do_ref, lse_ref, dl_ref, dk_ref, dv_ref, dk_sc, dv_sc):
        j, i = pl.program_id(1), pl.program_id(2)

        @pl.when(i == 0)
        def _():
            dk_sc[...] = jnp.zeros_like(dk_sc)
            dv_sc[...] = jnp.zeros_like(dv_sc)

        @pl.when(i >= j)
        def _():
            p, ds = _attn_probs(q_ref, k_ref, v_ref, do_ref, lse_ref, dl_ref, i, j, t)
            dv_sc[...] += lax.dot_general(p.astype(BF16), do_ref[...], (((0,), (0,)), ((), ())), preferred_element_type=F32)
            dk_sc[...] += lax.dot_general(ds.astype(BF16), q_ref[...], (((0,), (0,)), ((), ())), preferred_element_type=F32)

        @pl.when(i == nb - 1)
        def _():
            dk_ref[...] = dk_sc[...]
            dv_ref[...] = dv_sc[...]

    q_map2 = lambda h, j, i: (jnp.maximum(i, j), h)
    kv_map2 = lambda h, j, i: (j, h)
    dk, dv = pl.pallas_call(
        dkv_body, name="mla_attn_dkv", grid=(H, nb, nb),
        in_specs=[pl.BlockSpec((t, HQ), q_map2), pl.BlockSpec((t, HQ), kv_map2), pl.BlockSpec((t, VD), kv_map2),
                  pl.BlockSpec((t, VD), q_map2), pl.BlockSpec((t, LANES), q_map2), pl.BlockSpec((t, LANES), q_map2)],
        out_specs=[pl.BlockSpec((t, HQ), kv_map2), pl.BlockSpec((t, VD), kv_map2)],
        out_shape=[jax.ShapeDtypeStruct((s, H * HQ), F32), jax.ShapeDtypeStruct((s, H * VD), F32)],
        scratch_shapes=[pltpu.VMEM((t, HQ), F32), pltpu.VMEM((t, VD), F32)],
        compiler_params=_params(("parallel", "parallel", "arbitrary")),
    )(q, k, v, do, lse, delta)
    return dq, dk, dv


def _f_delta(do, o):
    prod = do.astype(F32) * o.astype(F32)
    parts = [jnp.broadcast_to(jnp.sum(prod[:, h * VD:(h + 1) * VD], axis=-1, keepdims=True), (do.shape[0], LANES))
             for h in range(H)]
    return jnp.concatenate(parts, axis=-1), do.astype(BF16)


def _xattn_head(qh, kh, gq):
    qn = _rms(qh, gq) * (XH ** -0.5)
    sc = lax.dot_general(qn.astype(BF16), kh, (((1,), (1,)), ((), ())), preferred_element_type=F32)
    sc = sc - jnp.max(sc, axis=-1, keepdims=True)
    e = jnp.exp(sc)
    return qn, e / jnp.sum(e, axis=-1, keepdims=True)


def _xattn_fwd(q, kn, v, gq, *, ts=512):
    def fn(qb, knb, vb, g):
        outs = []
        for h in range(H):
            sl = slice(h * XH, (h + 1) * XH)
            _, p = _xattn_head(qb[:, sl], knb[:, sl], g)
            outs.append(jnp.dot(p.astype(BF16), vb[:, sl], preferred_element_type=F32))
        return (jnp.concatenate(outs, axis=-1),)

    return _rowwise(fn, [q], [kn, v, gq], [(H * XH, BF16)], ts=ts, name="xattn_fwd")[0]


def _xattn_bwd(q, kn, v, gq, do, *, ts=512):
    def fn(qb, dob, knb, vb, g):
        dqs, dks, dvs = [], [], []
        dg = jnp.zeros((1, XH), F32)
        for h in range(H):
            sl = slice(h * XH, (h + 1) * XH)
            qh, kh, vh, doh = qb[:, sl], knb[:, sl], vb[:, sl], dob[:, sl].astype(BF16)
            qn, p = _xattn_head(qh, kh, g)
            dp = lax.dot_general(doh, vh, (((1,), (1,)), ((), ())), preferred_element_type=F32)
            dvs.append(lax.dot_general(p.astype(BF16), doh, (((0,), (0,)), ((), ())), preferred_element_type=F32))
            ds = (p * (dp - jnp.sum(dp * p, axis=-1, keepdims=True))).astype(BF16)
            dqn = jnp.dot(ds, kh, preferred_element_type=F32)
            dks.append(lax.dot_general(ds, qn.astype(BF16), (((0,), (0,)), ((), ())), preferred_element_type=F32))
            _, vjp_n = jax.vjp(lambda a, b: _rms(a, b) * (XH ** -0.5), qh, g)
            dqh, dgh = vjp_n(dqn)
            dqs.append(dqh)
            dg = dg + dgh
        return (jnp.concatenate(dqs, axis=-1), jnp.concatenate(dks, axis=-1), jnp.concatenate(dvs, axis=-1), dg)

    return _rowwise(fn, [q, do], [kn, v, gq], [(H * XH, BF16)], [kn.shape, v.shape, gq.shape], ts=ts, name="xattn_bwd")


def _cmul(ar, ai, xr, xi):
    return ar * xr - ai * xi, ar * xi + ai * xr


def _scan(br, bi, ar, ai, *, reverse, cw=256, name):
    s, n = br.shape
    c = SCAN_CHUNKS
    tt = s // c
    cw = min(cw, n)

    def body(br_ref, bi_ref, ar_ref, ai_ref, xr_ref, xi_ref):
        a_r = jnp.broadcast_to(ar_ref[...], (c, cw))
        a_i = jnp.broadcast_to(ai_ref[...], (c, cw))
        zero = jnp.zeros((c, cw), F32)

        def row(step):
            t = (tt - 1 - step) if reverse else step
            return pl.ds(pl.multiple_of(t * c, c), c)

        def local(step, carry):
            sr, si, qr, qi = carry
            r = row(step)
            nr, ni = _cmul(a_r, a_i, sr, si)
            nr, ni = nr + br_ref[r, :], ni + bi_ref[r, :]
            xr_ref[r, :] = nr
            xi_ref[r, :] = ni
            return (nr, ni) + _cmul(a_r, a_i, qr, qi)

        end_r, end_i, pr, pi = lax.fori_loop(0, tt, local, (zero, zero, jnp.ones((c, cw), F32), zero))

        rows_id = lax.broadcasted_iota(jnp.int32, (c, cw), 0)
        car_r, car_i = zero, zero
        cur_r, cur_i = jnp.zeros((1, cw), F32), jnp.zeros((1, cw), F32)
        order = range(c - 1, -1, -1) if reverse else range(c)
        for kk in order:
            car_r = jnp.where(rows_id == kk, cur_r, car_r)
            car_i = jnp.where(rows_id == kk, cur_i, car_i)
            nr, ni = _cmul(pr[0:1], pi[0:1], cur_r, cur_i)
            cur_r = nr + end_r[kk:kk + 1]
            cur_i = ni + end_i[kk:kk + 1]

        def fix(step, carry):
            qr, qi = _cmul(a_r, a_i, *carry)
            r = row(step)
            dr, di = _cmul(qr, qi, car_r, car_i)
            xr_ref[r, :] += dr
            xi_ref[r, :] += di
            return qr, qi

        lax.fori_loop(0, tt, fix, (jnp.ones((c, cw), F32), zero))

    col = lambda j: (0, j)
    return pl.pallas_call(
        body, name=name, grid=(n // cw,),
        in_specs=[pl.BlockSpec((s, cw), col), pl.BlockSpec((s, cw), col), pl.BlockSpec((1, cw), col), pl.BlockSpec((1, cw), col)],
        out_specs=[pl.BlockSpec((s, cw), col), pl.BlockSpec((s, cw), col)],
        out_shape=[jax.ShapeDtypeStruct((s, n), F32), jax.ShapeDtypeStruct((s, n), F32)],
        compiler_params=_params(("parallel",)),
    )(br, bi, ar, ai)


def _scan_da(lr, li, xr, xi, *, cw=256):
    s, n = lr.shape
    c = SCAN_CHUNKS
    tt = s // c
    cw = min(cw, n)

    def body(lr_ref, li_ref, xr_ref, xi_ref, dar_ref, dai_ref):
        def step(t, carry):
            acc_r, acc_i = carry
            r = pl.ds(pl.multiple_of(t * c, c), c)
            rp = pl.ds(pl.multiple_of((t - 1) * c, c), c)
            l_r, l_i, p_r, p_i = lr_ref[r, :], li_ref[r, :], xr_ref[rp, :], xi_ref[rp, :]
            return acc_r + l_r * p_r + l_i * p_i, acc_i + l_i * p_r - l_r * p_i

        zero = jnp.zeros((c, cw), F32)
        acc_r, acc_i = lax.fori_loop(1, tt, step, (zero, zero))
        last = pl.ds((tt - 1) * c, c)
        rows_id = lax.broadcasted_iota(jnp.int32, (c, cw), 0)
        p_r = jnp.where(rows_id == 0, 0.0, pltpu.roll(xr_ref[last, :], 1, 0))
        p_i = jnp.where(rows_id == 0, 0.0, pltpu.roll(xi_ref[last, :], 1, 0))
        first = pl.ds(0, c)
        l_r, l_i = lr_ref[first, :], li_ref[first, :]
        acc_r = acc_r + l_r * p_r + l_i * p_i
        acc_i = acc_i + l_i * p_r - l_r * p_i
        dar_ref[...] = jnp.sum(acc_r, axis=0, keepdims=True)
        dai_ref[...] = jnp.sum(acc_i, axis=0, keepdims=True)

    col = lambda j: (0, j)
    return pl.pallas_call(
        body, name="s5_scan_da", grid=(n // cw,),
        in_specs=[pl.BlockSpec((s, cw), col)] * 4,
        out_specs=[pl.BlockSpec((1, cw), col)] * 2,
        out_shape=[jax.ShapeDtypeStruct((1, n), F32)] * 2,
        compiler_params=_params(("parallel",)),
    )(lr, li, xr, xi)


def _mesh_place():
    x, y, c = lax.axis_index("x"), lax.axis_index("y"), lax.axis_index("c")
    peers = []
    for k in range(1, N_DEV):
        px, py, pc = x ^ ((k >> 2) & 1), y ^ ((k >> 1) & 1), c ^ (k & 1)
        peers.append(((px, py, pc), 4 * px + 2 * py + pc))
    return 4 * x + 2 * y + c, peers


class _Exchange:
    def __init__(self, arrays, rows, *, gather, name, after=None):
        self.n_arr, self.rows, self.gather, self.name = len(arrays), rows, gather, name
        n_arr = self.n_arr
        if gather:
            assert all(r % BF16_ROWS == 0 for r in rows)
            lands = [lax.empty((N_DEV * r, a.shape[1]), a.dtype) for a, r in zip(arrays, rows)]
        else:
            lands = [lax.empty((N_DEV - 1, a.shape[0] if st is None else n, a.shape[1]), a.dtype)
                     for a, (st, n) in zip(arrays, rows)]
        has_after = after is not None

        def body(*refs):
            ins, zones = refs[:n_arr], refs[n_arr:2 * n_arr]
            sems = refs[2 * n_arr + has_after:4 * n_arr + has_after]
            token = refs[-1]
            me, peers = _mesh_place()
            for i in range(n_arr):
                for k, (pxyz, pid) in enumerate(peers):
                    if gather:
                        src = ins[i]
                        dst = zones[i].at[pl.ds(pl.multiple_of(me * rows[i], BF16_ROWS), rows[i])]
                    else:
                        stride, n = rows[i]
                        src = ins[i] if stride is None else ins[i].at[pl.ds(pl.multiple_of(pid * stride, 8), n)]
                        dst = zones[i].at[k]
                    pltpu.make_async_remote_copy(
                        src_ref=src, dst_ref=dst, send_sem=sems[2 * i], recv_sem=sems[2 * i + 1],
                        device_id=pxyz, device_id_type=pl.DeviceIdType.MESH).start()
            token[...] = jnp.zeros_like(token)

        hbm = pl.BlockSpec(memory_space=pltpu.HBM)
        sem = pl.BlockSpec(memory_space=pltpu.SEMAPHORE)
        args = [pltpu.with_memory_space_constraint(a, pltpu.HBM) for a in list(arrays) + lands]
        res = pl.pallas_call(
            body, name=name + "_start",
            in_specs=[hbm] * (2 * n_arr) + ([pl.BlockSpec(memory_space=pl.ANY)] if has_after else []),
            out_specs=[sem] * (2 * n_arr) + [hbm] * (2 * n_arr) + [pl.BlockSpec(memory_space=pltpu.VMEM)],
            out_shape=[pltpu.SemaphoreType.DMA(())] * (2 * n_arr) + [pltpu.HBM(a.shape, a.dtype) for a in args]
            + [jax.ShapeDtypeStruct((8, LANES), F32)],
            input_output_aliases={i: 2 * n_arr + i for i in range(2 * n_arr)},
            compiler_params=pltpu.CompilerParams(has_side_effects=pltpu.SideEffectType.DATAFLOW_SIDE_EFFECTING),
        )(*args, *([after] if has_after else []))
        self.sems, self.thru, self.token = res[:2 * n_arr], res[2 * n_arr:4 * n_arr], res[-1]

    def wait(self, after):
        n_arr = self.n_arr

        def body(*refs):
            zones, sems = refs[n_arr:2 * n_arr], refs[2 * n_arr:4 * n_arr]
            myself = (lax.axis_index("x"), lax.axis_index("y"), lax.axis_index("c"))
            for i in range(n_arr):
                seven = zones[i].at[pl.ds(0, (N_DEV - 1) * self.rows[i])] if self.gather else zones[i]
                all_seven = pltpu.make_async_remote_copy(
                    src_ref=seven, dst_ref=seven, send_sem=sems[2 * i], recv_sem=sems[2 * i + 1],
                    device_id=myself, device_id_type=pl.DeviceIdType.MESH)
                all_seven.wait_recv()
                all_seven.wait_send()

        hbm = pl.BlockSpec(memory_space=pltpu.HBM)
        sem = pl.BlockSpec(memory_space=pltpu.SEMAPHORE)
        res = pl.pallas_call(
            body, name=self.name + "_wait",
            in_specs=[hbm] * (2 * n_arr) + [sem] * (2 * n_arr) + [pl.BlockSpec(memory_space=pl.ANY)],
            out_specs=[hbm] * (2 * n_arr), out_shape=[pltpu.HBM(a.shape, a.dtype) for a in self.thru],
            input_output_aliases={i: i for i in range(2 * n_arr)},
            compiler_params=pltpu.CompilerParams(has_side_effects=pltpu.SideEffectType.DATAFLOW_SIDE_EFFECTING),
        )(*self.thru, *self.sems, after)
        return res[:n_arr], res[n_arr:]


def _my_slot():
    me = 4 * lax.axis_index("x") + 2 * lax.axis_index("y") + lax.axis_index("c")
    return me.astype(jnp.int32).reshape(1)


def _place_own(gathered, block, me, *, name):
    r, c = block.shape

    def body(me_ref, b_ref, g_ref, o_ref):
        o_ref[...] = b_ref[...]

    return pl.pallas_call(
        body, name=name, out_shape=jax.ShapeDtypeStruct(gathered.shape, gathered.dtype),
        grid_spec=pltpu.PrefetchScalarGridSpec(
            num_scalar_prefetch=1, grid=(1,),
            in_specs=[pl.BlockSpec((r, c), lambda i, me_ref: (0, 0)), pl.BlockSpec(memory_space=pl.ANY)],
            out_specs=pl.BlockSpec((r, c), lambda i, me_ref: (me_ref[0], 0))),
        input_output_aliases={2: 0}, compiler_params=_params(("arbitrary",)),
    )(me, block, gathered)


def _elementwise_tiles(r, c):
    if r % 128 == 0:
        return 128, c
    return r, (256 if c % 256 == 0 else c)


def _adamw_math(g, w, m, v):
    nm = ADAM_B1 * m + (1.0 - ADAM_B1) * g
    nv = ADAM_B2 * v + (1.0 - ADAM_B2) * (g * g)
    m_hat = nm / (1.0 - ADAM_B1 ** ADAM_STEP)
    v_hat = nv / (1.0 - ADAM_B2 ** ADAM_STEP)
    return -ADAM_LR * (m_hat / (jnp.sqrt(v_hat) + ADAM_EPS) + ADAM_WD * w), nm, nv


def _sum_parts(me_ref, own_ref, p_ref):
    n = p_ref.shape[1]
    own = own_ref[0:n, :].astype(F32)
    g = None
    for d in range(N_DEV):
        k = jnp.bitwise_xor(me_ref[0], d)
        term = jnp.where(k == 0, own, p_ref[jnp.maximum(k, 1) - 1].astype(F32))
        g = term if g is None else g + term
    return g


def _sum_adamw(me, sent, stride, parts, w=None, m=None, v=None, *, name):
    _, r, cdim = parts.shape
    tc = 256 if cdim % 256 == 0 else cdim
    update = w is not None
    own_rows = r if stride is None else stride

    def body(me_ref, own_ref, p_ref, *refs):
        g = _sum_parts(me_ref, own_ref, p_ref)
        if update:
            w_ref, m_ref, v_ref, g_ref, d_ref, nm_ref, nv_ref = refs
            d_ref[...], nm_ref[...], nv_ref[...] = _adamw_math(g, w_ref[...], m_ref[...], v_ref[...])
        else:
            g_ref, = refs
        g_ref[...] = g

    blk = pl.BlockSpec((r, tc), lambda j, me_ref: (0, j))
    own_spec = pl.BlockSpec((own_rows, tc), (lambda j, me_ref: (0, j)) if stride is None else (lambda j, me_ref: (me_ref[0], j)))
    n_out = 4 if update else 1
    res = pl.pallas_call(
        body, name=name, out_shape=[jax.ShapeDtypeStruct((r, cdim), F32)] * n_out,
        grid_spec=pltpu.PrefetchScalarGridSpec(
            num_scalar_prefetch=1, grid=(cdim // tc,),
            in_specs=[own_spec, pl.BlockSpec((N_DEV - 1, r, tc), lambda j, me_ref: (0, 0, j))] + ([blk] * 3 if update else []),
            out_specs=[blk] * n_out),
        compiler_params=_params(("parallel",)),
    )(me, sent, parts, *((w, m, v) if update else ()))
    return list(res)


def _adamw(g, w, m, v, *, name):
    r, cdim = w.shape
    tr, tc = _elementwise_tiles(r, cdim)

    def body(g_ref, w_ref, m_ref, v_ref, d_ref, nm_ref, nv_ref):
        d_ref[...], nm_ref[...], nv_ref[...] = _adamw_math(g_ref[...], w_ref[...], m_ref[...], v_ref[...])

    blk = pl.BlockSpec((tr, tc), lambda i, j: (i, j))
    return list(pl.pallas_call(
        body, name=name, grid=(r // tr, cdim // tc), in_specs=[blk] * 4,
        out_specs=[blk] * 3, out_shape=[jax.ShapeDtypeStruct((r, cdim), F32)] * 3,
        compiler_params=_params(("parallel", "parallel")),
    )(g, w, m, v))


SHARD_ROWS_P = {n: (FF_SHARD_P if 'ffn' in n else IN_SHARD_P if n == 'w_in' else None) for n in SHARDED}


def _to_exchange_layout(name, shard):
    t = shard.T if SHARD_AXIS[name] == 1 else shard
    pad = SHARD_ROWS_P[name]
    return t if pad is None else jnp.pad(t, ((0, pad - t.shape[0]), (0, 0)))


def _expand_w_in(wt):
    wt = wt.reshape(N_DEV, IN_SHARD_P, D)[:, :IN_SHARD].reshape(IN_W, D)
    o = Q_RANK + KV_RANK
    kr1, kr2 = wt[o:o + ROPE // 2], wt[o + ROPE // 2:o + ROPE]
    z = jnp.zeros((LANES - ROPE, D), wt.dtype)
    return jnp.concatenate([wt[:o], wt[o + ROPE:], kr1, kr2, z, -kr2, kr1, z], axis=0)


def _expand_w_uq(wt):
    w = wt.reshape(H, QK, Q_RANK)
    z = jnp.zeros((H, LANES - ROPE, Q_RANK), w.dtype)
    q1, q2 = w[:, NOPE:NOPE + ROPE // 2], w[:, NOPE + ROPE // 2:]
    return jnp.concatenate([w[:, :NOPE].reshape(H * NOPE, Q_RANK),
                            jnp.concatenate([q1, q2, z], axis=1).reshape(H * LANES, Q_RANK),
                            jnp.concatenate([-q2, q1, z], axis=1).reshape(H * LANES, Q_RANK)], axis=0)


def _layout_qk_gain(g):
    g = g.reshape(QK)
    g1, g2, z = g[NOPE:NOPE + ROPE // 2], g[NOPE + ROPE // 2:], jnp.zeros((LANES - ROPE,), g.dtype)
    return jnp.stack([g[:NOPE], jnp.concatenate([g1, g2, z]), jnp.concatenate([g2, g1, z])])


def _rep16(a):
    return jnp.repeat(a, SSM_GRP, axis=0)


def _layout_ssm_in(a_re, a_im, log_dt, b_re, b_im):
    b_r = jnp.transpose(b_re, (0, 2, 1)).reshape(SSM_G * SSM_GRP, SSM_P)
    b_i = jnp.transpose(b_im, (0, 2, 1)).reshape(SSM_G * SSM_GRP, SSM_P)
    ldt = jnp.broadcast_to(log_dt.reshape(SSM_G, 1), (SSM_G, SSM_P))
    return _rep16(a_re), _rep16(a_im), _rep16(ldt), b_r, b_i


def _block_diag_b(bb):
    eye = jnp.eye(SSM_G, dtype=bb.dtype)
    return (bb.reshape(SSM_G, SSM_GRP, 1, SSM_P) * eye[:, None, :, None]).reshape(SSM_W, SSM_N)


def _block_diag_c(cc):
    eye = jnp.eye(SSM_G, dtype=cc.dtype)
    return (jnp.transpose(cc, (0, 2, 1))[:, :, None, :] * eye[:, None, :, None]).reshape(SSM_N, SSM_W)


def _time_perm(a, inverse=False):
    s, w = a.shape
    c = SCAN_CHUNKS
    if inverse:
        return jnp.transpose(a.reshape(s // c, c, w), (1, 0, 2)).reshape(s, w)
    return jnp.transpose(a.reshape(c, s // c, w), (1, 0, 2)).reshape(s, w)


class _Weights:
    def __init__(self, groups=(), landed=None, me=None):
        self.groups, self.landed, self.me = list(groups), dict(landed or {}), me

    def get(self, name, after):
        if name not in self.landed:
            names, exchange = next(g for g in self.groups if name in g[0])
            for n, block, gathered in zip(names, *exchange.wait(after)):
                self.landed[n] = _place_own(gathered, block, self.me, name="place_" + n)
        return self.landed[name]

    def __getitem__(self, name):
        return self.landed[name]


def _ffn_fwd(x, g, wc, tag, deps=()):
    h = _rowwise(_f_norm, [x], [g], [(D, BF16)], name=tag + "_norm", deps=deps)[0]
    gate = _mm(h, wc.get(tag + '_w_gate', h), tb=True, name=tag + "_gate")
    up = _mm(h, wc.get(tag + '_w_up', h), tb=True, name=tag + "_up")
    act = _rowwise(_f_swiglu, [gate, up], [], [(D_FFP, BF16)], ts=256, name=tag + "_act")[0]
    x_out = _mm(act, wc.get(tag + '_w_down', h), res=x, scale=0.5, name=tag + "_down")
    return x_out, (h, gate, up, act)


def _ffn_bwd(x, g, wc, saved, dx_out, tag, send):
    h, gate, up, act = saved
    w_gt, w_ut, w_d = (wc.get(tag + n, h) for n in ('_w_gate', '_w_up', '_w_down'))
    dact = _mm(dx_out, w_d, tb=True, scale=0.5, out_dtype=BF16, name=tag + "_dact")
    d_d = _mm(act, dx_out, ta=True, scale=0.5, name=tag + "_dwdown")
    token = send({tag + '_w_down': d_d})
    dgate, dup = _rowwise_bwd(_f_swiglu, [gate, up], [], [dact], row_grads={0: BF16, 1: BF16}, const_grads=[], ts=256,
                              name=tag + "_act_bwd", deps=[token])
    d_gt = _mm(dgate, h, ta=True, name=tag + "_dwgate")
    d_ut = _mm(dup, h, ta=True, name=tag + "_dwup")
    token = send({tag + '_w_gate': d_gt, tag + '_w_up': d_ut})
    dh = _mm(dgate, w_gt, name=tag + "_dh_gate")
    dh = _mm(dup, w_ut, res=dh, out_dtype=BF16, name=tag + "_dh_up")
    dx, dg = _rowwise_bwd(_f_norm, [x], [g], [dh], row_grads={0: F32}, const_grads=[0], adds={0: dx_out},
                          name=tag + "_norm_bwd", deps=[token])
    return dx, dg


def _local_step(x, mem, cos, sin, target, wc, ws, send, deps=()):
    gs = {}

    x1, sv1 = _ffn_fwd(x, ws['ffn1_norm'], wc, "ffn1", deps=deps)

    h2 = _rowwise(_f_norm, [x1], [ws['mix_norm']], [(D, BF16)], name="mix_norm")[0]
    w_in_raw, w_uq_raw = wc.get('w_in', h2), wc.get('mla_w_uq', h2)
    w_in_e = _expand_w_in(w_in_raw)
    w_uq_e = _expand_w_uq(w_uq_raw)
    proj = _mm(h2, w_in_e, tb=True, name="w_in")
    c_q, c_kv = _rowwise(_f_prep1, [proj], [ws['q_norm'], ws['kv_norm']], [(Q_RANK, BF16), (KV_RANK, BF16)], name="mla_prep1")
    qall = _mm(c_q, w_uq_e, tb=True, name="w_uq")
    kv = _mm(c_kv, wc['mla_w_ukv'], tb=True, name="w_ukv")
    kr = _rowwise(_f_kr, [proj], [], [(2 * LANES, F32)], name="mla_kr")[0]
    q, k, v = _prep2_fwd(qall, kv, kr, cos, sin, ws['qk_gq'], ws['qk_gk'])
    o_mla, lse = _attn_fwd(q, k, v)

    u = proj[:, Q_RANK + KV_RANK:Q_RANK + KV_RANK + SSM_W]
    u_p = _time_perm(u)
    disc_in = [ws['ssm_lr'], ws['ssm_li'], ws['ssm_ldt'], ws['ssm_br'], ws['ssm_bi']]
    ar16, ai16, bbr, bbi = _rowwise(_f_disc, disc_in, [], [(SSM_P, F32)] * 4, name="s5_disc")
    a_r = ar16[::SSM_GRP].reshape(1, SSM_N)
    a_i = ai16[::SSM_GRP].reshape(1, SSM_N)
    bblk_r, bblk_i = _block_diag_b(bbr).astype(BF16), _block_diag_b(bbi).astype(BF16)
    cblk_r, cblk_i = _block_diag_c(ws['ssm_cr']).astype(BF16), _block_diag_c(-ws['ssm_ci']).astype(BF16)
    bu_r = _mm(u_p, bblk_r, name="s5_bu_r", tn_cap=1024)
    bu_i = _mm(u_p, bblk_i, name="s5_bu_i", tn_cap=1024)
    xr, xi = _scan(bu_r, bu_i, a_r, a_i, reverse=False, name="s5_scan_fwd")
    yc = _mm(xr, cblk_r, name="s5_y_r", tk_cap=2048)
    yc = _mm(xi, cblk_i, res=yc, name="s5_y_i", tk_cap=2048)
    g_p = _rowwise(_f_s5_gelu, [yc, u_p], [ws['ssm_d']], [(SSM_W, F32)], name="s5_gelu")[0]
    z_p = _mm(g_p, wc['ssm_w_glu'], name="s5_glu")
    g_t, z_t = _time_perm(g_p, inverse=True), _time_perm(z_p, inverse=True)
    on_consts = [ws['ssm_b_glu'], ws['out_norm_mla'], ws['out_norm_ssm']]
    ycat = _rowwise(_f_outnorm, [o_mla, g_t, z_t], on_consts, [(D, BF16)], name="out_norm")[0]
    x2 = _mm(ycat, wc['w_o'], res=x1, name="w_o")

    hx = _rowwise(_f_norm, [x2], [ws['xattn_norm']], [(D, BF16)], name="xattn_norm")[0]
    xq = _mm(hx, wc['xattn_w_q'], name="xattn_q")
    mn = _rowwise(_f_norm, [mem], [ws['mem_norm']], [(D, BF16)], name="mem_norm")[0]
    kvm = _mm(mn, wc['xattn_w_kv'], name="xattn_kv")
    xkn, xv = _rowwise(_f_memk, [kvm], [ws['xattn_k_norm']], [(H * XH, BF16), (H * XH, BF16)], name="xattn_knorm")
    xo = _xattn_fwd(xq, xkn, xv, ws['xattn_q_norm'])
    x3 = _mm(xo, wc['xattn_w_o'], tb=True, res=x2, name="xattn_o")

    x4, sv2 = _ffn_fwd(x3, ws['ffn2_norm'], wc, "ffn2")

    def f_loss(yb, tb):
        err = yb - tb
        return err * (1.0 / D), jnp.broadcast_to(jnp.sum(jnp.sum(err * err, axis=1, keepdims=True), axis=0, keepdims=True) * (0.5 / D), (1, LANES))

    dx4, loss = _rowwise(f_loss, [x4, target], [], [(D, F32)], [(1, LANES)], name="loss")

    dx3, gs['ffn2_norm'] = _ffn_bwd(x3, ws['ffn2_norm'], wc, sv2, dx4, "ffn2", send)

    dxo = _mm(dx3, wc['xattn_w_o'], out_dtype=BF16, name="xattn_o_dx")
    send({'xattn_w_o': _mm(dx3, xo, ta=True, name="xattn_o_dw")})
    dxq, dxkn, dxv, gs['xattn_q_norm'] = _xattn_bwd(xq, xkn, xv, ws['xattn_q_norm'], dxo)
    dkvm, gs['xattn_k_norm'] = _rowwise_bwd(_f_memk, [kvm], [ws['xattn_k_norm']], [dxkn, dxv], row_grads={0: BF16},
                                            const_grads=[0], name="xattn_knorm_bwd")
    send({'xattn_w_kv': _mm(mn, dkvm, ta=True, name="xattn_kv_dw")})
    dmn = _mm(dkvm, wc['xattn_w_kv'], tb=True, out_dtype=BF16, name="xattn_kv_dx")
    gs['mem_norm'] = _rowwise_bwd(_f_norm, [mem], [ws['mem_norm']], [dmn], row_grads={}, const_grads=[0], name="mem_norm_bwd")[0]
    send({'xattn_w_q': _mm(hx, dxq, ta=True, name="xattn_q_dw")})
    dhx = _mm(dxq, wc['xattn_w_q'], tb=True, out_dtype=BF16, name="xattn_q_dx")
    dx2, gs['xattn_norm'] = _rowwise_bwd(_f_norm, [x2], [ws['xattn_norm']], [dhx], row_grads={0: F32}, const_grads=[0],
                                         adds={0: dx3}, name="xattn_norm_bwd")

    dycat = _mm(dx2, wc['w_o'], tb=True, out_dtype=BF16, name="w_o_dx")
    send({'w_o': _mm(ycat, dx2, ta=True, name="w_o_dw")})
    do_mla, dg_t, dz_t, gs['ssm_b_glu'], gs['out_norm_mla'], gs['out_norm_ssm'] = _rowwise_bwd(
        _f_outnorm, [o_mla, g_t, z_t], on_consts, [dycat], row_grads={0: F32, 1: F32, 2: BF16}, const_grads=[0, 1, 2],
        name="out_norm_bwd")

    dz_p, dg_p = _time_perm(dz_t), _time_perm(dg_t)
    send({'ssm_w_glu': _mm(g_p, dz_p, ta=True, name="s5_glu_dw")})
    dg_p = _mm(dz_p, wc['ssm_w_glu'], tb=True, res=dg_p, name="s5_glu_dx")
    dyc, du_d, gs['ssm_d'] = _rowwise_bwd(_f_s5_gelu, [yc, u_p], [ws['ssm_d']], [dg_p], row_grads={0: BF16, 1: F32},
                                          const_grads=[0], name="s5_gelu_bwd")
    d_cblk_r = _mm(xr, dyc, ta=True, name="s5_dc_r")
    d_cblk_i = _mm(xi, dyc, ta=True, name="s5_dc_i")
    dxr = _mm(dyc, cblk_r, tb=True, name="s5_dx_r", tn_cap=1024)
    dxi = _mm(dyc, cblk_i, tb=True, name="s5_dx_i", tn_cap=1024)
    lam_r, lam_i = _scan(dxr, dxi, a_r, -a_i, reverse=True, name="s5_scan_bwd")
    d_ar, d_ai = _scan_da(lam_r, lam_i, xr, xi)
    d_bblk_r = _mm(u_p, lam_r, ta=True, name="s5_db_r", tn_cap=1024)
    d_bblk_i = _mm(u_p, lam_i, ta=True, name="s5_db_i", tn_cap=1024)
    du_p = _mm(lam_r, bblk_r, tb=True, res=du_d, name="s5_du_r", tk_cap=2048)
    du_p = _mm(lam_i, bblk_i, tb=True, res=du_p, name="s5_du_i", tk_cap=2048)
    du = _time_perm(du_p, inverse=True)
    gs['ssm_cr'] = jax.linear_transpose(_block_diag_c, ws['ssm_cr'])(d_cblk_r)[0]
    gs['ssm_ci'] = -jax.linear_transpose(_block_diag_c, ws['ssm_ci'])(d_cblk_i)[0]
    d_bbr = jax.linear_transpose(_block_diag_b, bbr)(d_bblk_r)[0]
    d_bbi = jax.linear_transpose(_block_diag_b, bbi)(d_bblk_i)[0]
    d_ar16 = jnp.zeros((SSM_G * SSM_GRP, SSM_P), F32).at[::SSM_GRP].set(d_ar.reshape(SSM_G, SSM_P))
    d_ai16 = jnp.zeros((SSM_G * SSM_GRP, SSM_P), F32).at[::SSM_GRP].set(d_ai.reshape(SSM_G, SSM_P))
    gs['ssm_lr'], gs['ssm_li'], gs['ssm_ldt'], gs['ssm_br'], gs['ssm_bi'] = _rowwise_bwd(
        _f_disc, disc_in, [], [d_ar16, d_ai16, d_bbr, d_bbi], row_grads={i: F32 for i in range(5)}, const_grads=[],
        name="s5_disc_bwd")

    delta, do_b = _rowwise(_f_delta, [do_mla, o_mla], [], [(H * LANES, F32), (H * VD, BF16)], name="mla_delta")
    dq, dk, dv = _attn_bwd(q, k, v, do_b, lse, delta)
    dqall, dkv, dkr, gs['qk_gq'], gs['qk_gk'] = _prep2_bwd(qall, kv, kr, cos, sin, ws['qk_gq'], ws['qk_gk'], dq, dk, dv)
    d_w_uq_e = _mm(dqall, c_q, ta=True, name="w_uq_dw")
    send({'mla_w_uq': jax.linear_transpose(_expand_w_uq, jax.ShapeDtypeStruct(w_uq_raw.shape, F32))(d_w_uq_e)[0]})
    dc_q = _mm(dqall, w_uq_e, out_dtype=BF16, name="w_uq_dx")
    send({'mla_w_ukv': _mm(dkv, c_kv, ta=True, name="w_ukv_dw")})
    dc_kv = _mm(dkv, wc['mla_w_ukv'], out_dtype=BF16, name="w_ukv_dx")

    def f_prep1_bwd(pb, dcq, dckv, dub, dkrb, gq, gkv):
        _, vjp = jax.vjp(_f_prep1, pb[:, :Q_RANK + KV_RANK], gq, gkv)
        dpa, dgq, dgkv = vjp((dcq.astype(BF16), dckv.astype(BF16)))
        return jnp.concatenate([dpa, dub, dkrb], axis=-1), dgq, dgkv

    dproj, gs['q_norm'], gs['kv_norm'] = _rowwise(
        f_prep1_bwd, [proj, dc_q, dc_kv, du, dkr], [ws['q_norm'], ws['kv_norm']], [(IN_WP, BF16)],
        [(1, Q_RANK), (1, KV_RANK)], name="mla_prep1_bwd")
    d_w_in_e = _mm(dproj, h2, ta=True, name="w_in_dw")
    token = send({'w_in': jax.linear_transpose(_expand_w_in, jax.ShapeDtypeStruct(w_in_raw.shape, F32))(d_w_in_e)[0]})
    dh2 = _mm(dproj, w_in_e, out_dtype=BF16, name="w_in_dx")
    dx1, gs['mix_norm'] = _rowwise_bwd(_f_norm, [x1], [ws['mix_norm']], [dh2], row_grads={0: F32}, const_grads=[0],
                                       adds={0: dx2}, name="mix_norm_bwd", deps=[token])

    dx0, gs['ffn1_norm'] = _ffn_bwd(x, ws['ffn1_norm'], wc, sv1, dx1, "ffn1", send)
    return loss, dx0, gs


def _prep2_fwd(qall, kv, kr, cos, sin, gq, gk):
    return _rowwise(_f_prep2, [qall, kv, kr, cos, sin], [gq, gk], [(H * HQ, BF16), (H * HQ, BF16), (H * VD, BF16)],
                    ts=256, name="mla_prep2")


def _prep2_bwd(qall, kv, kr, cos, sin, gq, gk, dq, dk, dv):
    return _rowwise_bwd(_f_prep2, [qall, kv, kr, cos, sin], [gq, gk], [dq, dk, dv], row_grads={0: BF16, 1: BF16, 2: F32},
                        const_grads=[0, 1], ts=256, name="mla_prep2_bwd")


def _rope_tables(pos):
    half = ROPE // 2
    inv = ROPE_THETA ** (-jnp.arange(half, dtype=F32) / half)
    ang = pos.astype(F32)[:, None] * inv[None, :]
    z = jnp.zeros((pos.shape[0], LANES - ROPE), F32)
    cos, sin = jnp.cos(ang), jnp.sin(ang)
    return jnp.concatenate([cos, cos, z], axis=-1), jnp.concatenate([sin, sin, z], axis=-1)


def _small_layout(p):
    lr, li, ldt, br, bi = _layout_ssm_in(p['ssm_a_re'], p['ssm_a_im'], p['ssm_log_dt'], p['ssm_b_re'], p['ssm_b_im'])
    return {
        'ffn1_norm': p['ffn1_norm'].reshape(1, D), 'mix_norm': p['mix_norm'].reshape(1, D),
        'q_norm': p['mla_q_norm'].reshape(1, Q_RANK), 'kv_norm': p['mla_kv_norm'].reshape(1, KV_RANK),
        'qk_gq': _layout_qk_gain(p['mla_qk_norm_q']), 'qk_gk': _layout_qk_gain(p['mla_qk_norm_k']),
        'ssm_lr': lr, 'ssm_li': li, 'ssm_ldt': ldt, 'ssm_br': br, 'ssm_bi': bi,
        'ssm_cr': p['ssm_c_re'], 'ssm_ci': p['ssm_c_im'], 'ssm_d': p['ssm_d'].reshape(1, SSM_W),
        'ssm_b_glu': p['ssm_b_glu'].reshape(1, SSM_W),
        'out_norm_mla': p['out_norm_mla'].reshape(1, SSM_W), 'out_norm_ssm': p['out_norm_ssm'].reshape(1, SSM_W),
        'xattn_norm': p['xattn_norm'].reshape(1, D), 'mem_norm': p['mem_norm'].reshape(1, D),
        'xattn_q_norm': p['xattn_q_norm'].reshape(1, XH), 'xattn_k_norm': p['xattn_k_norm'].reshape(1, XH),
        'ffn2_norm': p['ffn2_norm'].reshape(1, D),
    }


def _pack(arrs, rows):
    flat = jnp.concatenate([a.reshape(-1) for a in arrs])
    return jnp.pad(flat, (0, rows * D - flat.shape[0])).reshape(rows, D)


def _unpack(flat, shapes):
    flat = flat.reshape(-1)
    out, off = [], 0
    for sh in shapes:
        n = int(np.prod(sh))
        out.append(flat[off:off + n].reshape(sh))
        off += n
    return out


def kernel(x, mem, positions, ffn1_norm, ffn1_w_gate, ffn1_w_up, ffn1_w_down, mix_norm, w_in, mla_q_norm, mla_w_uq, mla_kv_norm, mla_w_ukv, mla_qk_norm_q, mla_qk_norm_k, ssm_a_re, ssm_a_im, ssm_log_dt, ssm_b_re, ssm_b_im, ssm_c_re, ssm_c_im, ssm_d, ssm_w_glu, ssm_b_glu, out_norm_mla, out_norm_ssm, w_o, xattn_norm, mem_norm, xattn_w_q, xattn_w_kv, xattn_q_norm, xattn_k_norm, xattn_w_o, ffn2_norm, ffn2_w_gate, ffn2_w_up, ffn2_w_down, loss_target, m_ffn1_norm, m_ffn1_w_gate, m_ffn1_w_up, m_ffn1_w_down, m_mix_norm, m_w_in, m_mla_q_norm, m_mla_w_uq, m_mla_kv_norm, m_mla_w_ukv, m_mla_qk_norm_q, m_mla_qk_norm_k, m_ssm_a_re, m_ssm_a_im, m_ssm_log_dt, m_ssm_b_re, m_ssm_b_im, m_ssm_c_re, m_ssm_c_im, m_ssm_d, m_ssm_w_glu, m_ssm_b_glu, m_out_norm_mla, m_out_norm_ssm, m_w_o, m_xattn_norm, m_mem_norm, m_xattn_w_q, m_xattn_w_kv, m_xattn_q_norm, m_xattn_k_norm, m_xattn_w_o, m_ffn2_norm, m_ffn2_w_gate, m_ffn2_w_up, m_ffn2_w_down, v_ffn1_norm, v_ffn1_w_gate, v_ffn1_w_up, v_ffn1_w_down, v_mix_norm, v_w_in, v_mla_q_norm, v_mla_w_uq, v_mla_kv_norm, v_mla_w_ukv, v_mla_qk_norm_q, v_mla_qk_norm_k, v_ssm_a_re, v_ssm_a_im, v_ssm_log_dt, v_ssm_b_re, v_ssm_b_im, v_ssm_c_re, v_ssm_c_im, v_ssm_d, v_ssm_w_glu, v_ssm_b_glu, v_out_norm_mla, v_out_norm_ssm, v_w_o, v_xattn_norm, v_mem_norm, v_xattn_w_q, v_xattn_w_kv, v_xattn_q_norm, v_xattn_k_norm, v_xattn_w_o, v_ffn2_norm, v_ffn2_w_gate, v_ffn2_w_up, v_ffn2_w_down):
    args = dict(locals())
    w = {n: args[n] for n in WEIGHTS}
    mom = {n: args['m_' + n] for n in WEIGHTS}
    var = {n: args['v_' + n] for n in WEIGHTS}
    return _step(x, mem, positions, loss_target, w, mom, var)


GATHER_GROUPS = [('ffn1', ['ffn1_w_gate', 'ffn1_w_up', 'ffn1_w_down']),
                 ('mix', ['w_in', 'mla_w_uq', 'mla_w_ukv', 'ssm_w_glu', 'w_o', 'xattn_w_q', 'xattn_w_kv', 'xattn_w_o']),
                 ('ffn2', ['ffn2_w_gate', 'ffn2_w_up', 'ffn2_w_down'])]
SCATTER_GROUPS = [('ffn2', ['ffn2_w_down', 'ffn2_w_gate', 'ffn2_w_up']),
                  ('mix', ['xattn_w_o', 'xattn_w_kv', 'xattn_w_q', 'w_o', 'ssm_w_glu', 'mla_w_uq', 'mla_w_ukv', 'w_in']),
                  ('ffn1_down', ['ffn1_w_down']), ('ffn1_gu', ['ffn1_w_gate', 'ffn1_w_up'])]


def _step(x, mem, positions, loss_target, w, mom, var):
    blocks = {n: _to_exchange_layout(n, w[n][0]).astype(BF16) for n in SHARDED}
    gathers, token = [], None
    for tag, names in GATHER_GROUPS:
        ex = _Exchange([blocks[n] for n in names], [blocks[n].shape[0] for n in names], gather=True,
                       name="gather_" + tag, after=token)
        gathers.append((names, ex))
        token = ex.token
    me = _my_slot()
    wc = _Weights(gathers, me=me)

    rows = {n: (blocks[n].shape[0], blocks[n].shape[0] if SHARD_ROWS_P[n] is None else w[n][0].shape[SHARD_AXIS[n]])
            for n in SHARDED}
    ready, scatters = {}, []

    def send(grads):
        ready.update(grads)
        for tag, names in SCATTER_GROUPS:
            if all(n in ready for n in names) and not any(t == tag for t, _, _ in scatters):
                ex = _Exchange([ready[n] for n in names], [rows[n] for n in names], gather=False, name="scatter_" + tag)
                scatters.append((tag, names, ex))
                return ex.token
        return None

    small = {n: w[n][0] for n in SMALL}
    ws = _small_layout(small)
    cos, sin = _rope_tables(positions[0])
    loss, dx, gs = _local_step(x[0], mem[0], cos, sin, loss_target[0], wc, ws, send, deps=[token])

    g_small = jax.linear_transpose(_small_layout, {n: jax.ShapeDtypeStruct(small[n].shape, F32) for n in SMALL})(gs)[0]
    small_shapes = [small[n].shape for n in SMALL]
    n_small = sum(int(np.prod(sh)) for sh in small_shapes) + 1
    rows_small = -(-n_small // (8 * D)) * 8
    small_pack = _pack([g_small[n] for n in SMALL] + [loss[0, :1]], rows_small)
    small_ex = _Exchange([small_pack], [(None, rows_small)], gather=False, name="scatter_small")

    out = {}
    for _, names, ex in scatters:
        for n, sent, p in zip(names, *ex.wait(dx)):
            if SHARD_AXIS[n] == 0:
                out[n] = _sum_adamw(me, sent, rows[n][0], p, w[n][0], mom[n][0], var[n][0], name="adamw_" + n)
            else:
                g = _sum_adamw(me, sent, rows[n][0], p, name="sum_" + n)[0].T
                out[n] = [g] + _adamw(g, w[n][0], mom[n][0], var[n][0], name="adamw_" + n)
    state = [_pack([t[n][0] for n in SMALL], rows_small) for t in (w, mom, var)]
    sent, p = small_ex.wait(dx)
    small_out = _sum_adamw(me, sent[0], None, p[0], *state, name="adamw_small")
    loss_total = small_out[0].reshape(-1)[n_small - 1]
    for n, vals in zip(SMALL, zip(*[_unpack(flat, small_shapes) for flat in small_out])):
        out[n] = vals
    outs = [out[n][i][None] for i in range(4) for n in WEIGHTS]
    return (loss_total, dx[None], *outs)
```

```python
import math

import jax
import jax.numpy as jnp
import numpy as np
from jax import lax
from jax.experimental import pallas as pl
from jax.experimental.pallas import tpu as pltpu

F32 = jnp.float32
BF16 = jnp.bfloat16

N_DEV = 8
D = 1024
D_FF = 2752
D_FFP = 2816
MEM_LEN = 256
H = 4
Q_RANK, KV_RANK, NOPE, ROPE, VD = 384, 256, 128, 64, 128
QK = NOPE + ROPE
HQ = 2 * 128
SSM_W, SSM_G, SSM_GRP, SSM_P = 512, 32, 16, 64
SSM_N = SSM_G * SSM_P
SSM_PACK = 8
IN_W = 1216
IN_WP = 1408
XH = 128
EPS = 1e-6
ROPE_THETA = 10000.0
SCAN_CHUNKS = 8

ADAM_LR, ADAM_B1, ADAM_B2, ADAM_EPS, ADAM_WD, ADAM_STEP = 0.001, 0.9, 0.999, 1e-08, 0.01, 10

VMEM_LIMIT = 56 * 1024 * 1024
ACC_BYTES = 6 * 1024 * 1024
LANES = 128
BF16_ROWS = 16
GRAD_DTYPE = BF16
FF_SHARD = D_FF // N_DEV
FF_SHARD_P = 352
IN_SHARD = IN_W // N_DEV
IN_SHARD_P = 160

WEIGHTS = ['ffn1_norm', 'ffn1_w_gate', 'ffn1_w_up', 'ffn1_w_down', 'mix_norm', 'w_in', 'mla_q_norm', 'mla_w_uq',
           'mla_kv_norm', 'mla_w_ukv', 'mla_qk_norm_q', 'mla_qk_norm_k', 'ssm_a_re', 'ssm_a_im', 'ssm_log_dt',
           'ssm_b_re', 'ssm_b_im', 'ssm_c_re', 'ssm_c_im', 'ssm_d', 'ssm_w_glu', 'ssm_b_glu', 'out_norm_mla',
           'out_norm_ssm', 'w_o', 'xattn_norm', 'mem_norm', 'xattn_w_q', 'xattn_w_kv', 'xattn_q_norm',
           'xattn_k_norm', 'xattn_w_o', 'ffn2_norm', 'ffn2_w_gate', 'ffn2_w_up', 'ffn2_w_down']
SHARD_AXIS = {'ffn1_w_gate': 1, 'ffn1_w_up': 1, 'ffn1_w_down': 0, 'w_in': 1, 'mla_w_uq': 1, 'mla_w_ukv': 1,
              'ssm_w_glu': 0, 'w_o': 0, 'xattn_w_q': 0, 'xattn_w_kv': 0, 'xattn_w_o': 1,
              'ffn2_w_gate': 1, 'ffn2_w_up': 1, 'ffn2_w_down': 0}
SHARDED = [n for n in WEIGHTS if n in SHARD_AXIS]
SMALL = [n for n in WEIGHTS if n not in SHARD_AXIS]


def _params(sem=None):
    return pltpu.CompilerParams(dimension_semantics=sem, vmem_limit_bytes=VMEM_LIMIT)


def _tile(n, cap):
    if n <= cap:
        return n
    best = n
    for t in range(LANES, cap + 1, LANES):
        if n % t == 0:
            best = t
    return best


def _mm(a, b, *, ta=False, tb=False, out_dtype=F32, res=None, scale=1.0, name, tm_cap=512, tn_cap=1408, tk_cap=2816):
    m, k = (a.shape[1], a.shape[0]) if ta else a.shape
    k2, n = (b.shape[1], b.shape[0]) if tb else b.shape
    assert k == k2, (a.shape, b.shape, ta, tb)
    if ta:
        tk_cap = min(tk_cap, 512)
        tm_cap = 1408
    tm, tn, tk = _tile(m, tm_cap), _tile(n, tn_cap), _tile(k, tk_cap)
    if tm * tn * 4 > ACC_BYTES:
        tn = _tile(n, max(LANES, ACC_BYTES // (4 * tm) // LANES * LANES))
    nk = k // tk
    dims = (((0 if ta else 1,), (1 if tb else 0,)), ((), ()))
    has_res = res is not None

    def body(*refs):
        if has_res:
            a_ref, b_ref, r_ref, o_ref, acc_ref = refs
        else:
            a_ref, b_ref, o_ref, acc_ref = refs
        kk = pl.program_id(2)

        @pl.when(kk == 0)
        def _():
            acc_ref[...] = jnp.zeros_like(acc_ref)

        acc_ref[...] += lax.dot_general(a_ref[...].astype(BF16), b_ref[...].astype(BF16), dims,
                                        preferred_element_type=F32)

        @pl.when(kk == nk - 1)
        def _():
            out = acc_ref[...]
            if scale != 1.0:
                out = out * scale
            if has_res:
                out = out + r_ref[...].astype(F32)
            o_ref[...] = out.astype(o_ref.dtype)

    a_spec = pl.BlockSpec((tk, tm), lambda i, j, kk: (kk, i)) if ta else pl.BlockSpec((tm, tk), lambda i, j, kk: (i, kk))
    b_spec = pl.BlockSpec((tn, tk), lambda i, j, kk: (j, kk)) if tb else pl.BlockSpec((tk, tn), lambda i, j, kk: (kk, j))
    o_spec = pl.BlockSpec((tm, tn), lambda i, j, kk: (i, j))
    in_specs = [a_spec, b_spec] + ([o_spec] if has_res else [])
    args = (a, b) + ((res,) if has_res else ())
    return pl.pallas_call(
        body, name=name, grid=(m // tm, n // tn, nk), in_specs=in_specs, out_specs=o_spec,
        out_shape=jax.ShapeDtypeStruct((m, n), out_dtype), scratch_shapes=[pltpu.VMEM((tm, tn), F32)],
        compiler_params=_params(("parallel", "parallel", "arbitrary")),
    )(*args)


def _mm_grouped(a, b, *, tb=False, res=None, out_dtype=F32, name, tm=512):
    s = a.shape[0]
    g = b.shape[0]
    nb, ka = (b.shape[1], b.shape[2]) if tb else (b.shape[2], b.shape[1])
    assert a.shape[1] == g * ka
    tm = min(tm, s)
    dims = (((1,), (1 if tb else 0,)), ((), ()))
    has_res = res is not None

    def body(*refs):
        if has_res:
            a_ref, b_ref, r_ref, o_ref = refs
        else:
            a_ref, b_ref, o_ref = refs
        out = lax.dot_general(a_ref[...].astype(BF16), b_ref[...].astype(BF16), dims, preferred_element_type=F32)
        if has_res:
            out = out + r_ref[...].astype(F32)
        o_ref[...] = out.astype(o_ref.dtype)

    o_spec = pl.BlockSpec((tm, nb), lambda i, j: (i, j))
    in_specs = [pl.BlockSpec((tm, ka), lambda i, j: (i, j)), pl.BlockSpec((None,) + b.shape[1:], lambda i, j: (j, 0, 0))]
    return pl.pallas_call(
        body, name=name, grid=(s // tm, g), in_specs=in_specs + ([o_spec] if has_res else []), out_specs=o_spec,
        out_shape=jax.ShapeDtypeStruct((s, g * nb), out_dtype), compiler_params=_params(("parallel", "parallel")),
    )(a, b, *((res,) if has_res else ()))


def _mm_grouped_tn(a, b, *, ka, kb, name, tk=512):
    s = a.shape[0]
    g = a.shape[1] // ka
    assert b.shape[1] == g * kb
    tk = min(tk, s)
    nk = s // tk

    def body(a_ref, b_ref, o_ref):
        part = lax.dot_general(a_ref[...].astype(BF16), b_ref[...].astype(BF16), (((0,), (0,)), ((), ())),
                               preferred_element_type=F32)

        @pl.when(pl.program_id(1) == 0)
        def _():
            o_ref[...] = part

        @pl.when(pl.program_id(1) > 0)
        def _():
            o_ref[...] += part

    return pl.pallas_call(
        body, name=name, grid=(g, nk),
        in_specs=[pl.BlockSpec((tk, ka), lambda j, kk: (kk, j)), pl.BlockSpec((tk, kb), lambda j, kk: (kk, j))],
        out_specs=pl.BlockSpec((None, ka, kb), lambda j, kk: (j, 0, 0)),
        out_shape=jax.ShapeDtypeStruct((g, ka, kb), F32), compiler_params=_params(("parallel", "arbitrary")),
    )(a, b)


def _rowwise(fn, rows, consts, outs, accs=(), *, ts=512, name, deps=()):
    s = rows[0].shape[0]
    ts = min(ts, s)
    assert s % ts == 0
    n_rows, n_consts, n_outs = len(rows), len(consts), len(outs)
    deps = [d for d in deps if d is not None]
    consts = list(consts) + deps

    def body(*refs):
        ins = [r[...] for r in refs[:n_rows + n_consts]]
        res = fn(*ins)
        res = tuple(res) if isinstance(res, (tuple, list)) else (res,)
        out_refs = refs[n_rows + len(consts):]
        for o_ref, val in zip(out_refs[:n_outs], res[:n_outs]):
            o_ref[...] = val.astype(o_ref.dtype)
        if accs:
            first = pl.program_id(0) == 0

            @pl.when(first)
            def _():
                for a_ref, val in zip(out_refs[n_outs:], res[n_outs:]):
                    a_ref[...] = val.astype(F32)

            @pl.when(jnp.logical_not(first))
            def _():
                for a_ref, val in zip(out_refs[n_outs:], res[n_outs:]):
                    a_ref[...] += val.astype(F32)

    in_specs = [pl.BlockSpec((ts, r.shape[1]), lambda i: (i, 0)) for r in rows]
    in_specs += [pl.BlockSpec(c.shape, lambda i: (0, 0)) for c in consts]
    out_specs = [pl.BlockSpec((ts, w), lambda i: (i, 0)) for w, _ in outs]
    out_specs += [pl.BlockSpec(tuple(sh), lambda i: (0, 0)) for sh in accs]
    out_shape = [jax.ShapeDtypeStruct((s, w), dt) for w, dt in outs]
    out_shape += [jax.ShapeDtypeStruct(tuple(sh), F32) for sh in accs]
    res = pl.pallas_call(
        body, name=name, grid=(s // ts,), in_specs=in_specs, out_specs=out_specs, out_shape=out_shape,
        compiler_params=_params(("arbitrary",)),
    )(*rows, *consts)
    return res


def _rowwise_bwd(f, rows, consts, cts, *, row_grads, const_grads, adds=None, ts=512, name, deps=()):
    adds = adds or {}
    n_rows, n_consts, n_cts = len(rows), len(consts), len(cts)
    add_keys = sorted(adds)
    rg = sorted(row_grads)
    cg = sorted(const_grads)

    def fn(*args):
        r = args[:n_rows]
        c = args[n_rows:n_rows + n_consts]
        ct = args[n_rows + n_consts:n_rows + n_consts + n_cts]
        extra = args[n_rows + n_consts + n_cts:]
        outs, vjp = jax.vjp(f, *r, *c)
        outs = tuple(outs) if isinstance(outs, (tuple, list)) else (outs,)
        cot = tuple(g.astype(o.dtype) for g, o in zip(ct, outs))
        grads = vjp(cot if len(cot) > 1 else cot[0])
        res = []
        for i in rg:
            g = grads[i].astype(F32)
            if i in adds:
                g = g + extra[add_keys.index(i)].astype(F32)
            res.append(g)
        for i in cg:
            res.append(grads[n_rows + i])
        return tuple(res)

    rows_all = list(rows) + list(cts) + [adds[i] for i in add_keys]
    def fn2(*args):
        nr = len(rows_all)
        rr, cc = args[:nr], args[nr:]
        return fn(*rr[:n_rows], *cc, *rr[n_rows:])

    outs = [(rows[i].shape[1], row_grads[i]) for i in rg]
    accs = [consts[i].shape for i in cg]
    return _rowwise(fn2, rows_all, list(consts), outs, accs, ts=ts, name=name, deps=deps)


def _rms(x, g):
    xf = x.astype(F32)
    return xf * lax.rsqrt(jnp.mean(xf * xf, axis=-1, keepdims=True) + EPS) * g.astype(F32)


def _sigmoid(x):
    return 1.0 / (1.0 + jnp.exp(-x))


def _f_norm(x, g):
    return _rms(x, g).astype(BF16)


def _f_swiglu(gate, up):
    gate, up = gate.astype(F32), up.astype(F32)
    return (gate * _sigmoid(gate) * up).astype(BF16)


def _f_prep1(proj, gq, gkv):
    return _rms(proj[:, :Q_RANK], gq).astype(BF16), _rms(proj[:, Q_RANK:Q_RANK + KV_RANK], gkv).astype(BF16)


def _f_kr(proj):
    return (proj[:, Q_RANK + KV_RANK + SSM_W:],)


def _f_prep2(qall, kv, kr2, cos, sin, gq, gk):
    kr, krs = kr2[:, :LANES].astype(F32), kr2[:, LANES:].astype(F32)
    k_rot = kr * gk[1:2] * cos + krs * gk[2:3] * sin
    k_ss = jnp.sum(kr * kr, axis=-1, keepdims=True)
    q_scale = QK ** -0.5
    qs, ks, vs = [], [], []
    for h in range(H):
        qn = qall[:, h * LANES:(h + 1) * LANES].astype(F32)
        qr = qall[:, (H + h) * LANES:(H + h + 1) * LANES].astype(F32)
        qrs = qall[:, (2 * H + h) * LANES:(2 * H + h + 1) * LANES].astype(F32)
        rstd = lax.rsqrt((jnp.sum(qn * qn, axis=-1, keepdims=True) + jnp.sum(qr * qr, axis=-1, keepdims=True)) / QK + EPS)
        rstd = rstd * q_scale
        qs += [qn * gq[0:1] * rstd, (qr * gq[1:2] * cos + qrs * gq[2:3] * sin) * rstd]
        kn = kv[:, 2 * h * LANES:(2 * h + 1) * LANES].astype(F32)
        rstd_k = lax.rsqrt((jnp.sum(kn * kn, axis=-1, keepdims=True) + k_ss) / QK + EPS)
        ks += [kn * gk[0:1] * rstd_k, k_rot * rstd_k]
        vs.append(kv[:, (2 * h + 1) * LANES:(2 * h + 2) * LANES])
    return (jnp.concatenate(qs, axis=-1).astype(BF16), jnp.concatenate(ks, axis=-1).astype(BF16),
            jnp.concatenate(vs, axis=-1).astype(BF16))


def _gelu(x):
    return 0.5 * x * (1.0 + jnp.tanh(math.sqrt(2.0 / math.pi) * (x + 0.044715 * (x * x * x))))


def _f_s5_gelu(yc, u, d):
    return _gelu(yc.astype(F32) + d * u.astype(F32))


def _f_outnorm(o_mla, g, z, b_glu, g_om, g_os):
    y_ssm = g * _sigmoid(z + b_glu)
    return jnp.concatenate([_rms(o_mla, g_om), _rms(y_ssm, g_os)], axis=-1).astype(BF16)


def _f_memk(kvm, gk):
    ks = [_rms(kvm[:, h * XH:(h + 1) * XH], gk) for h in range(H)]
    return jnp.concatenate(ks, axis=-1).astype(BF16), kvm[:, H * XH:].astype(BF16)


def _f_disc(lr, li, log_dt, br, bi):
    dt = jnp.exp(log_dt)
    decay = jnp.exp(lr * dt)
    ar = decay * jnp.cos(li * dt)
    ai = decay * jnp.sin(li * dt)
    den = lr * lr + li * li
    nr = ar - 1.0
    coef_r = (nr * lr + ai * li) / den
    coef_i = (ai * lr - nr * li) / den
    return ar, ai, coef_r * br - coef_i * bi, coef_r * bi + coef_i * br


def _causal_mask(i, j, tq, tk):
    qpos = i * tq + lax.broadcasted_iota(jnp.int32, (tq, tk), 0)
    kpos = j * tk + lax.broadcasted_iota(jnp.int32, (tq, tk), 1)
    return qpos >= kpos


def _attn_fwd(q, k, v, *, t=512):
    s = q.shape[0]
    t = min(t, s)
    nb = s // t

    def body(q_ref, k_ref, v_ref, o_ref, lse_ref, m_sc, l_sc, acc_sc):
        i, j = pl.program_id(1), pl.program_id(2)

        @pl.when(j == 0)
        def _():
            m_sc[...] = jnp.full_like(m_sc, -jnp.inf)
            l_sc[...] = jnp.zeros_like(l_sc)
            acc_sc[...] = jnp.zeros_like(acc_sc)

        @pl.when(j <= i)
        def _():
            sc = lax.dot_general(q_ref[...], k_ref[...], (((1,), (1,)), ((), ())), preferred_element_type=F32)
            sc = jnp.where(_causal_mask(i, j, t, t), sc, -jnp.inf)
            m_old = m_sc[...]
            m_new = jnp.maximum(m_old, jnp.max(sc, axis=-1, keepdims=True))
            p = jnp.exp(sc - m_new)
            alpha = jnp.exp(m_old - m_new)
            l_sc[...] = alpha * l_sc[...] + jnp.sum(p, axis=-1, keepdims=True)
            acc_sc[...] = alpha * acc_sc[...] + jnp.dot(p.astype(BF16), v_ref[...], preferred_element_type=F32)
            m_sc[...] = m_new

        @pl.when(j == i)
        def _():
            o_ref[...] = acc_sc[...] / l_sc[...]
            lse_ref[...] = jnp.broadcast_to(m_sc[...] + jnp.log(l_sc[...]), lse_ref.shape)

    kv_map = lambda h, i, j: (jnp.minimum(j, i), h)
    return pl.pallas_call(
        body, name="mla_attn_fwd", grid=(H, nb, nb),
        in_specs=[pl.BlockSpec((t, HQ), lambda h, i, j: (i, h)), pl.BlockSpec((t, HQ), kv_map),
                  pl.BlockSpec((t, VD), kv_map)],
        out_specs=[pl.BlockSpec((t, VD), lambda h, i, j: (i, h)), pl.BlockSpec((t, LANES), lambda h, i, j: (i, h))],
        out_shape=[jax.ShapeDtypeStruct((s, H * VD), F32), jax.ShapeDtypeStruct((s, H * LANES), F32)],
        scratch_shapes=[pltpu.VMEM((t, 1), F32), pltpu.VMEM((t, 1), F32), pltpu.VMEM((t, VD), F32)],
        compiler_params=_params(("parallel", "parallel", "arbitrary")),
    )(q, k, v)


def _attn_probs(q_ref, k_ref, v_ref, do_ref, lse_ref, dl_ref, i, j, t):
    sc = lax.dot_general(q_ref[...], k_ref[...], (((1,), (1,)), ((), ())), preferred_element_type=F32)
    p = jnp.where(_causal_mask(i, j, t, t), jnp.exp(sc - lse_ref[...][:, :1]), 0.0)
    dp = lax.dot_general(do_ref[...], v_ref[...], (((1,), (1,)), ((), ())), preferred_element_type=F32)
    ds = p * (dp - dl_ref[...][:, :1])
    return p, ds


def _attn_bwd(q, k, v, do, lse, delta, *, t=512):
    s = q.shape[0]
    t = min(t, s)
    nb = s // t

    def dq_body(q_ref, k_ref, v_ref, do_ref, lse_ref, dl_ref, dq_ref, acc_sc):
        i, j = pl.program_id(1), pl.program_id(2)

        @pl.when(j == 0)
        def _():
            acc_sc[...] = jnp.zeros_like(acc_sc)

        @pl.when(j <= i)
        def _():
            _, ds = _attn_probs(q_ref, k_ref, v_ref, do_ref, lse_ref, dl_ref, i, j, t)
            acc_sc[...] += jnp.dot(ds.astype(BF16), k_ref[...], preferred_element_type=F32)

        @pl.when(j == i)
        def _():
            dq_ref[...] = acc_sc[...]

    q_map = lambda h, i, j: (i, h)
    kv_map = lambda h, i, j: (jnp.minimum(j, i), h)
    dq = pl.pallas_call(
        dq_body, name="mla_attn_dq", grid=(H, nb, nb),
        in_specs=[pl.BlockSpec((t, HQ), q_map), pl.BlockSpec((t, HQ), kv_map), pl.BlockSpec((t, VD), kv_map),
                  pl.BlockSpec((t, VD), q_map), pl.BlockSpec((t, LANES), q_map), pl.BlockSpec((t, LANES), q_map)],
        out_specs=pl.BlockSpec((t, HQ), q_map),
        out_shape=jax.ShapeDtypeStruct((s, H * HQ), F32),
        scratch_shapes=[pltpu.VMEM((t, HQ), F32)],
        compiler_params=_params(("parallel", "parallel", "arbitrary")),
    )(q, k, v, do, lse, delta)

    def dkv_body(q_ref, k_ref, v_ref, do_ref, lse_ref, dl_ref, dk_ref, dv_ref, dk_sc, dv_sc):
        j, i = pl.program_id(1), pl.program_id(2)

        @pl.when(i == 0)
        def _():
            dk_sc[...] = jnp.zeros_like(dk_sc)
            dv_sc[...] = jnp.zeros_like(dv_sc)

        @pl.when(i >= j)
        def _():
            p, ds = _attn_probs(q_ref, k_ref, v_ref, do_ref, lse_ref, dl_ref, i, j, t)
            dv_sc[...] += lax.dot_general(p.astype(BF16), do_ref[...], (((0,), (0,)), ((), ())), preferred_element_type=F32)
            dk_sc[...] += lax.dot_general(ds.astype(BF16), q_ref[...], (((0,), (0,)), ((), ())), preferred_element_type=F32)

        @pl.when(i == nb - 1)
        def _():
            dk_ref[...] = dk_sc[...]
            dv_ref[...] = dv_sc[...]

    q_map2 = lambda h, j, i: (jnp.maximum(i, j), h)
    kv_map2 = lambda h, j, i: (j, h)
    dk, dv = pl.pallas_call(
        dkv_body, name="mla_attn_dkv", grid=(H, nb, nb),
        in_specs=[pl.BlockSpec((t, HQ), q_map2), pl.BlockSpec((t, HQ), kv_map2), pl.BlockSpec((t, VD), kv_map2),
                  pl.BlockSpec((t, VD), q_map2), pl.BlockSpec((t, LANES), q_map2), pl.BlockSpec((t, LANES), q_map2)],
        out_specs=[pl.BlockSpec((t, HQ), kv_map2), pl.BlockSpec((t, VD), kv_map2)],
        out_shape=[jax.ShapeDtypeStruct((s, H * HQ), F32), jax.ShapeDtypeStruct((s, H * VD), F32)],
        scratch_shapes=[pltpu.VMEM((t, HQ), F32), pltpu.VMEM((t, VD), F32)],
        compiler_params=_params(("parallel", "parallel", "arbitrary")),
    )(q, k, v, do, lse, delta)
    return dq, dk, dv


def _f_delta(do, o):
    prod = do.astype(F32) * o.astype(F32)
    parts = [jnp.broadcast_to(jnp.sum(prod[:, h * VD:(h + 1) * VD], axis=-1, keepdims=True), (do.shape[0], LANES))
             for h in range(H)]
    return jnp.concatenate(parts, axis=-1), do.astype(BF16)


def _xattn_head(qh, kh, gq):
    qn = _rms(qh, gq) * (XH ** -0.5)
    sc = lax.dot_general(qn.astype(BF16), kh, (((1,), (1,)), ((), ())), preferred_element_type=F32)
    sc = sc - jnp.max(sc, axis=-1, keepdims=True)
    e = jnp.exp(sc)
    return qn, e / jnp.sum(e, axis=-1, keepdims=True)


def _xattn_fwd(q, kn, v, gq, *, ts=512):
    def fn(qb, knb, vb, g):
        outs = []
        for h in range(H):
            sl = slice(h * XH, (h + 1) * XH)
            _, p = _xattn_head(qb[:, sl], knb[:, sl], g)
            outs.append(jnp.dot(p.astype(BF16), vb[:, sl], preferred_element_type=F32))
        return (jnp.concatenate(outs, axis=-1),)

    return _rowwise(fn, [q], [kn, v, gq], [(H * XH, BF16)], ts=ts, name="xattn_fwd")[0]


def _xattn_bwd(q, kn, v, gq, do, *, ts=512):
    def fn(qb, dob, knb, vb, g):
        dqs, dks, dvs = [], [], []
        dg = jnp.zeros((1, XH), F32)
        for h in range(H):
            sl = slice(h * XH, (h + 1) * XH)
            qh, kh, vh, doh = qb[:, sl], knb[:, sl], vb[:, sl], dob[:, sl].astype(BF16)
            qn, p = _xattn_head(qh, kh, g)
            dp = lax.dot_general(doh, vh, (((1,), (1,)), ((), ())), preferred_element_type=F32)
            dvs.append(lax.dot_general(p.astype(BF16), doh, (((0,), (0,)), ((), ())), preferred_element_type=F32))
            ds = (p * (dp - jnp.sum(dp * p, axis=-1, keepdims=True))).astype(BF16)
            dqn = jnp.dot(ds, kh, preferred_element_type=F32)
            dks.append(lax.dot_general(ds, qn.astype(BF16), (((0,), (0,)), ((), ())), preferred_element_type=F32))
            _, vjp_n = jax.vjp(lambda a, b: _rms(a, b) * (XH ** -0.5), qh, g)
            dqh, dgh = vjp_n(dqn)
            dqs.append(dqh)
            dg = dg + dgh
        return (jnp.concatenate(dqs, axis=-1), jnp.concatenate(dks, axis=-1), jnp.concatenate(dvs, axis=-1), dg)

    return _rowwise(fn, [q, do], [kn, v, gq], [(H * XH, BF16)], [kn.shape, v.shape, gq.shape], ts=ts, name="xattn_bwd")


def _cmul(ar, ai, xr, xi):
    return ar * xr - ai * xi, ar * xi + ai * xr


def _scan(br, bi, ar, ai, *, reverse, cw=256, name):
    s, n = br.shape
    c = SCAN_CHUNKS
    tt = s // c
    cw = min(cw, n)

    def body(br_ref, bi_ref, ar_ref, ai_ref, xr_ref, xi_ref):
        a_r = jnp.broadcast_to(ar_ref[...], (c, cw))
        a_i = jnp.broadcast_to(ai_ref[...], (c, cw))
        zero = jnp.zeros((c, cw), F32)

        def row(step):
            t = (tt - 1 - step) if reverse else step
            return pl.ds(pl.multiple_of(t * c, c), c)

        def local(step, carry):
            sr, si, qr, qi = carry
            r = row(step)
            nr, ni = _cmul(a_r, a_i, sr, si)
            nr, ni = nr + br_ref[r, :], ni + bi_ref[r, :]
            xr_ref[r, :] = nr
            xi_ref[r, :] = ni
            return (nr, ni) + _cmul(a_r, a_i, qr, qi)

        end_r, end_i, pr, pi = lax.fori_loop(0, tt, local, (zero, zero, jnp.ones((c, cw), F32), zero))

        rows_id = lax.broadcasted_iota(jnp.int32, (c, cw), 0)
        car_r, car_i = zero, zero
        cur_r, cur_i = jnp.zeros((1, cw), F32), jnp.zeros((1, cw), F32)
        order = range(c - 1, -1, -1) if reverse else range(c)
        for kk in order:
            car_r = jnp.where(rows_id == kk, cur_r, car_r)
            car_i = jnp.where(rows_id == kk, cur_i, car_i)
            nr, ni = _cmul(pr[0:1], pi[0:1], cur_r, cur_i)
            cur_r = nr + end_r[kk:kk + 1]
            cur_i = ni + end_i[kk:kk + 1]

        def fix(step, carry):
            qr, qi = _cmul(a_r, a_i, *carry)
            r = row(step)
            dr, di = _cmul(qr, qi, car_r, car_i)
            xr_ref[r, :] += dr
            xi_ref[r, :] += di
            return qr, qi

        lax.fori_loop(0, tt, fix, (jnp.ones((c, cw), F32), zero))

    col = lambda j: (0, j)
    return pl.pallas_call(
        body, name=name, grid=(n // cw,),
        in_specs=[pl.BlockSpec((s, cw), col), pl.BlockSpec((s, cw), col), pl.BlockSpec((1, cw), col), pl.BlockSpec((1, cw), col)],
        out_specs=[pl.BlockSpec((s, cw), col), pl.BlockSpec((s, cw), col)],
        out_shape=[jax.ShapeDtypeStruct((s, n), F32), jax.ShapeDtypeStruct((s, n), F32)],
        compiler_params=_params(("parallel",)),
    )(br, bi, ar, ai)


def _scan_da(lr, li, xr, xi, *, cw=256):
    s, n = lr.shape
    c = SCAN_CHUNKS
    tt = s // c
    cw = min(cw, n)

    def body(lr_ref, li_ref, xr_ref, xi_ref, dar_ref, dai_ref):
        def step(t, carry):
            acc_r, acc_i = carry
            r = pl.ds(pl.multiple_of(t * c, c), c)
            rp = pl.ds(pl.multiple_of((t - 1) * c, c), c)
            l_r, l_i, p_r, p_i = lr_ref[r, :], li_ref[r, :], xr_ref[rp, :], xi_ref[rp, :]
            return acc_r + l_r * p_r + l_i * p_i, acc_i + l_i * p_r - l_r * p_i

        zero = jnp.zeros((c, cw), F32)
        acc_r, acc_i = lax.fori_loop(1, tt, step, (zero, zero))
        last = pl.ds((tt - 1) * c, c)
        rows_id = lax.broadcasted_iota(jnp.int32, (c, cw), 0)
        p_r = jnp.where(rows_id == 0, 0.0, pltpu.roll(xr_ref[last, :], 1, 0))
        p_i = jnp.where(rows_id == 0, 0.0, pltpu.roll(xi_ref[last, :], 1, 0))
        first = pl.ds(0, c)
        l_r, l_i = lr_ref[first, :], li_ref[first, :]
        acc_r = acc_r + l_r * p_r + l_i * p_i
        acc_i = acc_i + l_i * p_r - l_r * p_i
        dar_ref[...] = jnp.sum(acc_r, axis=0, keepdims=True)
        dai_ref[...] = jnp.sum(acc_i, axis=0, keepdims=True)

    col = lambda j: (0, j)
    return pl.pallas_call(
        body, name="s5_scan_da", grid=(n // cw,),
        in_specs=[pl.BlockSpec((s, cw), col)] * 4,
        out_specs=[pl.BlockSpec((1, cw), col)] * 2,
        out_shape=[jax.ShapeDtypeStruct((1, n), F32)] * 2,
        compiler_params=_params(("parallel",)),
    )(lr, li, xr, xi)


def _mesh_place():
    x, y, c = lax.axis_index("x"), lax.axis_index("y"), lax.axis_index("c")
    peers = []
    for k in range(1, N_DEV):
        px, py, pc = x ^ ((k >> 2) & 1), y ^ ((k >> 1) & 1), c ^ (k & 1)
        peers.append(((px, py, pc), 4 * px + 2 * py + pc))
    return 4 * x + 2 * y + c, peers


class _Exchange:
    def __init__(self, arrays, rows, *, gather, name, after=None):
        self.n_arr, self.rows, self.gather, self.name = len(arrays), rows, gather, name
        n_arr = self.n_arr
        if gather:
            assert all(r % BF16_ROWS == 0 for r in rows)
            lands = [lax.empty((N_DEV * r, a.shape[1]), a.dtype) for a, r in zip(arrays, rows)]
        else:
            lands = [lax.empty((N_DEV - 1, a.shape[0] if st is None else n, a.shape[1]), a.dtype)
                     for a, (st, n) in zip(arrays, rows)]
        has_after = after is not None

        def body(*refs):
            ins, zones = refs[:n_arr], refs[n_arr:2 * n_arr]
            sems = refs[2 * n_arr + has_after:4 * n_arr + has_after]
            token = refs[-1]
            me, peers = _mesh_place()
            for i in range(n_arr):
                for k, (pxyz, pid) in enumerate(peers):
                    if gather:
                        src = ins[i]
                        dst = zones[i].at[pl.ds(pl.multiple_of(me * rows[i], BF16_ROWS), rows[i])]
                    else:
                        stride, n = rows[i]
                        src = ins[i] if stride is None else ins[i].at[pl.ds(pl.multiple_of(pid * stride, BF16_ROWS), n)]
                        dst = zones[i].at[k]
                    pltpu.make_async_remote_copy(
                        src_ref=src, dst_ref=dst, send_sem=sems[2 * i], recv_sem=sems[2 * i + 1],
                        device_id=pxyz, device_id_type=pl.DeviceIdType.MESH).start()
            token[...] = jnp.zeros_like(token)

        hbm = pl.BlockSpec(memory_space=pltpu.HBM)
        sem = pl.BlockSpec(memory_space=pltpu.SEMAPHORE)
        args = [pltpu.with_memory_space_constraint(a, pltpu.HBM) for a in list(arrays) + lands]
        res = pl.pallas_call(
            body, name=name + "_start",
            in_specs=[hbm] * (2 * n_arr) + ([pl.BlockSpec(memory_space=pl.ANY)] if has_after else []),
            out_specs=[sem] * (2 * n_arr) + [hbm] * (2 * n_arr) + [pl.BlockSpec(memory_space=pltpu.VMEM)],
            out_shape=[pltpu.SemaphoreType.DMA(())] * (2 * n_arr) + [pltpu.HBM(a.shape, a.dtype) for a in args]
            + [jax.ShapeDtypeStruct((8, LANES), F32)],
            input_output_aliases={i: 2 * n_arr + i for i in range(2 * n_arr)},
            compiler_params=pltpu.CompilerParams(has_side_effects=pltpu.SideEffectType.DATAFLOW_SIDE_EFFECTING),
        )(*args, *([after] if has_after else []))
        self.sems, self.thru, self.token = res[:2 * n_arr], res[2 * n_arr:4 * n_arr], res[-1]

    def wait(self, after):
        n_arr = self.n_arr

        def body(*refs):
            zones, sems = refs[n_arr:2 * n_arr], refs[2 * n_arr:4 * n_arr]
            myself = (lax.axis_index("x"), lax.axis_index("y"), lax.axis_index("c"))
            for i in range(n_arr):
                seven = zones[i].at[pl.ds(0, (N_DEV - 1) * self.rows[i])] if self.gather else zones[i]
                all_seven = pltpu.make_async_remote_copy(
                    src_ref=seven, dst_ref=seven, send_sem=sems[2 * i], recv_sem=sems[2 * i + 1],
                    device_id=myself, device_id_type=pl.DeviceIdType.MESH)
                all_seven.wait_recv()
                all_seven.wait_send()

        hbm = pl.BlockSpec(memory_space=pltpu.HBM)
        sem = pl.BlockSpec(memory_space=pltpu.SEMAPHORE)
        res = pl.pallas_call(
            body, name=self.name + "_wait",
            in_specs=[hbm] * (2 * n_arr) + [sem] * (2 * n_arr) + [pl.BlockSpec(memory_space=pl.ANY)],
            out_specs=[hbm] * (2 * n_arr), out_shape=[pltpu.HBM(a.shape, a.dtype) for a in self.thru],
            input_output_aliases={i: i for i in range(2 * n_arr)},
            compiler_params=pltpu.CompilerParams(has_side_effects=pltpu.SideEffectType.DATAFLOW_SIDE_EFFECTING),
        )(*self.thru, *self.sems, after)
        return res[:n_arr], res[n_arr:]


def _my_slot():
    me = 4 * lax.axis_index("x") + 2 * lax.axis_index("y") + lax.axis_index("c")
    return me.astype(jnp.int32).reshape(1)


def _place_own(gathered, block, me, *, name):
    r, c = block.shape

    def body(me_ref, b_ref, g_ref, o_ref):
        o_ref[...] = b_ref[...]

    return pl.pallas_call(
        body, name=name, out_shape=jax.ShapeDtypeStruct(gathered.shape, gathered.dtype),
        grid_spec=pltpu.PrefetchScalarGridSpec(
            num_scalar_prefetch=1, grid=(1,),
            in_specs=[pl.BlockSpec((r, c), lambda i, me_ref: (0, 0)), pl.BlockSpec(memory_space=pl.ANY)],
            out_specs=pl.BlockSpec((r, c), lambda i, me_ref: (me_ref[0], 0))),
        input_output_aliases={2: 0}, compiler_params=_params(("arbitrary",)),
    )(me, block, gathered)


def _elementwise_tiles(r, c):
    if r % 128 == 0:
        return 128, c
    return r, (256 if c % 256 == 0 else c)


def _adamw_math(g, w, m, v):
    nm = ADAM_B1 * m + (1.0 - ADAM_B1) * g
    nv = ADAM_B2 * v + (1.0 - ADAM_B2) * (g * g)
    m_hat = nm / (1.0 - ADAM_B1 ** ADAM_STEP)
    v_hat = nv / (1.0 - ADAM_B2 ** ADAM_STEP)
    return -ADAM_LR * (m_hat / (jnp.sqrt(v_hat) + ADAM_EPS) + ADAM_WD * w), nm, nv


def _sum_parts(me_ref, own_ref, p_ref, r):
    own = own_ref[...].astype(F32)
    g = None
    for d in range(N_DEV):
        k = jnp.bitwise_xor(me_ref[0], d)
        term = jnp.where(k == 0, own, p_ref[jnp.maximum(k, 1) - 1].astype(F32))
        g = term if g is None else g + term
    return g[0:r, :]


def _sum_adamw(me, sent, stride, parts, r, w=None, m=None, v=None, *, name):
    _, own_rows, cdim = parts.shape
    assert stride is None or stride == own_rows
    tc = 256 if cdim % 256 == 0 else cdim
    update = w is not None

    def body(me_ref, own_ref, p_ref, *refs):
        g = _sum_parts(me_ref, own_ref, p_ref, r)
        if update:
            w_ref, m_ref, v_ref, g_ref, d_ref, nm_ref, nv_ref = refs
            d_ref[...], nm_ref[...], nv_ref[...] = _adamw_math(g, w_ref[...], m_ref[...], v_ref[...])
        else:
            g_ref, = refs
        g_ref[...] = g

    blk = pl.BlockSpec((r, tc), lambda j, me_ref: (0, j))
    own_spec = pl.BlockSpec((own_rows, tc), (lambda j, me_ref: (0, j)) if stride is None else (lambda j, me_ref: (me_ref[0], j)))
    n_out = 4 if update else 1
    res = pl.pallas_call(
        body, name=name, out_shape=[jax.ShapeDtypeStruct((r, cdim), F32)] * n_out,
        grid_spec=pltpu.PrefetchScalarGridSpec(
            num_scalar_prefetch=1, grid=(cdim // tc,),
            in_specs=[own_spec, pl.BlockSpec((N_DEV - 1, own_rows, tc), lambda j, me_ref: (0, 0, j))]
            + ([blk] * 3 if update else []),
            out_specs=[blk] * n_out),
        compiler_params=_params(("parallel",)),
    )(me, sent, parts, *((w, m, v) if update else ()))
    return list(res)


def _adamw(g, w, m, v, *, name):
    r, cdim = w.shape
    tr, tc = _elementwise_tiles(r, cdim)

    def body(g_ref, w_ref, m_ref, v_ref, d_ref, nm_ref, nv_ref):
        d_ref[...], nm_ref[...], nv_ref[...] = _adamw_math(g_ref[...], w_ref[...], m_ref[...], v_ref[...])

    blk = pl.BlockSpec((tr, tc), lambda i, j: (i, j))
    return list(pl.pallas_call(
        body, name=name, grid=(r // tr, cdim // tc), in_specs=[blk] * 4,
        out_specs=[blk] * 3, out_shape=[jax.ShapeDtypeStruct((r, cdim), F32)] * 3,
        compiler_params=_params(("parallel", "parallel")),
    )(g, w, m, v))


SHARD_ROWS_P = {n: (FF_SHARD_P if 'ffn' in n else IN_SHARD_P if n == 'w_in' else None) for n in SHARDED}


def _to_exchange_layout(name, shard):
    t = shard.T if SHARD_AXIS[name] == 1 else shard
    pad = SHARD_ROWS_P[name]
    return t if pad is None else jnp.pad(t, ((0, pad - t.shape[0]), (0, 0)))


def _expand_w_in(wt):
    wt = wt.reshape(N_DEV, IN_SHARD_P, D)[:, :IN_SHARD].reshape(IN_W, D)
    o = Q_RANK + KV_RANK
    kr1, kr2 = wt[o:o + ROPE // 2], wt[o + ROPE // 2:o + ROPE]
    z = jnp.zeros((LANES - ROPE, D), wt.dtype)
    return jnp.concatenate([wt[:o], wt[o + ROPE:], kr1, kr2, z, -kr2, kr1, z], axis=0)


def _expand_w_uq(wt):
    w = wt.reshape(H, QK, Q_RANK)
    z = jnp.zeros((H, LANES - ROPE, Q_RANK), w.dtype)
    q1, q2 = w[:, NOPE:NOPE + ROPE // 2], w[:, NOPE + ROPE // 2:]
    return jnp.concatenate([w[:, :NOPE].reshape(H * NOPE, Q_RANK),
                            jnp.concatenate([q1, q2, z], axis=1).reshape(H * LANES, Q_RANK),
                            jnp.concatenate([-q2, q1, z], axis=1).reshape(H * LANES, Q_RANK)], axis=0)


def _layout_qk_gain(g):
    g = g.reshape(QK)
    g1, g2, z = g[NOPE:NOPE + ROPE // 2], g[NOPE + ROPE // 2:], jnp.zeros((LANES - ROPE,), g.dtype)
    return jnp.stack([g[:NOPE], jnp.concatenate([g1, g2, z]), jnp.concatenate([g2, g1, z])])


def _rep16(a):
    return jnp.repeat(a, SSM_GRP, axis=0)


def _layout_ssm_in(a_re, a_im, log_dt, b_re, b_im):
    b_r = jnp.transpose(b_re, (0, 2, 1)).reshape(SSM_G * SSM_GRP, SSM_P)
    b_i = jnp.transpose(b_im, (0, 2, 1)).reshape(SSM_G * SSM_GRP, SSM_P)
    ldt = jnp.broadcast_to(log_dt.reshape(SSM_G, 1), (SSM_G, SSM_P))
    return _rep16(a_re), _rep16(a_im), _rep16(ldt), b_r, b_i


def _block_diag_b(bb):
    eye = jnp.eye(SSM_PACK, dtype=bb.dtype)
    b5 = bb.reshape(SSM_G // SSM_PACK, SSM_PACK, SSM_GRP, 1, SSM_P) * eye[None, :, None, :, None]
    return b5.reshape(SSM_G // SSM_PACK, SSM_PACK * SSM_GRP, SSM_PACK * SSM_P)


def _block_diag_c(cc):
    eye = jnp.eye(SSM_PACK, dtype=cc.dtype)
    c5 = jnp.transpose(cc, (0, 2, 1)).reshape(SSM_G // SSM_PACK, SSM_PACK, SSM_P, 1, SSM_GRP) * eye[None, :, None, :, None]
    return c5.reshape(SSM_G // SSM_PACK, SSM_PACK * SSM_P, SSM_PACK * SSM_GRP)


def _time_perm(a, inverse=False):
    s, w = a.shape
    c = SCAN_CHUNKS
    if inverse:
        return jnp.transpose(a.reshape(s // c, c, w), (1, 0, 2)).reshape(s, w)
    return jnp.transpose(a.reshape(c, s // c, w), (1, 0, 2)).reshape(s, w)


class _Weights:
    def __init__(self, groups=(), landed=None, me=None):
        self.groups, self.landed, self.me = list(groups), dict(landed or {}), me

    def get(self, name, after):
        if name not in self.landed:
            names, exchange = next(g for g in self.groups if name in g[0])
            for n, block, gathered in zip(names, *exchange.wait(after)):
                self.landed[n] = _place_own(gathered, block, self.me, name="place_" + n)
        return self.landed[name]

    def __getitem__(self, name):
        return self.landed[name]


def _ffn_fwd(x, g, wc, tag, deps=()):
    h = _rowwise(_f_norm, [x], [g], [(D, BF16)], name=tag + "_norm", deps=deps)[0]
    gate = _mm(h, wc.get(tag + '_w_gate', h), tb=True, name=tag + "_gate")
    up = _mm(h, wc.get(tag + '_w_up', h), tb=True, name=tag + "_up")
    act = _rowwise(_f_swiglu, [gate, up], [], [(D_FFP, BF16)], ts=256, name=tag + "_act")[0]
    x_out = _mm(act, wc.get(tag + '_w_down', h), res=x, scale=0.5, name=tag + "_down")
    return x_out, (h, gate, up, act)


def _ffn_bwd(x, g, wc, saved, dx_out, tag, send):
    h, gate, up, act = saved
    w_gt, w_ut, w_d = (wc.get(tag + n, h) for n in ('_w_gate', '_w_up', '_w_down'))
    dact = _mm(dx_out, w_d, tb=True, scale=0.5, out_dtype=BF16, name=tag + "_dact")
    d_d = _mm(act, dx_out, ta=True, scale=0.5, out_dtype=GRAD_DTYPE, name=tag + "_dwdown")
    token = send({tag + '_w_down': d_d})
    dgate, dup = _rowwise_bwd(_f_swiglu, [gate, up], [], [dact], row_grads={0: BF16, 1: BF16}, const_grads=[], ts=256,
                              name=tag + "_act_bwd", deps=[token])
    d_gt = _mm(dgate, h, ta=True, out_dtype=GRAD_DTYPE, name=tag + "_dwgate")
    d_ut = _mm(dup, h, ta=True, out_dtype=GRAD_DTYPE, name=tag + "_dwup")
    token = send({tag + '_w_gate': d_gt, tag + '_w_up': d_ut})
    dh = _mm(dgate, w_gt, name=tag + "_dh_gate")
    dh = _mm(dup, w_ut, res=dh, out_dtype=BF16, name=tag + "_dh_up")
    dx, dg = _rowwise_bwd(_f_norm, [x], [g], [dh], row_grads={0: F32}, const_grads=[0], adds={0: dx_out},
                          name=tag + "_norm_bwd", deps=[token])
    return dx, dg


def _local_step(x, mem, cos, sin, target, wc, ws, send, deps=()):
    gs = {}

    x1, sv1 = _ffn_fwd(x, ws['ffn1_norm'], wc, "ffn1", deps=deps)

    h2 = _rowwise(_f_norm, [x1], [ws['mix_norm']], [(D, BF16)], name="mix_norm")[0]
    w_in_raw, w_uq_raw = wc.get('w_in', h2), wc.get('mla_w_uq', h2)
    w_in_e = _expand_w_in(w_in_raw)
    w_uq_e = _expand_w_uq(w_uq_raw)
    proj = _mm(h2, w_in_e, tb=True, name="w_in")
    c_q, c_kv = _rowwise(_f_prep1, [proj], [ws['q_norm'], ws['kv_norm']], [(Q_RANK, BF16), (KV_RANK, BF16)], name="mla_prep1")
    qall = _mm(c_q, w_uq_e, tb=True, name="w_uq")
    kv = _mm(c_kv, wc['mla_w_ukv'], tb=True, name="w_ukv")
    kr = _rowwise(_f_kr, [proj], [], [(2 * LANES, F32)], name="mla_kr")[0]
    q, k, v = _prep2_fwd(qall, kv, kr, cos, sin, ws['qk_gq'], ws['qk_gk'])
    o_mla, lse = _attn_fwd(q, k, v)

    u = proj[:, Q_RANK + KV_RANK:Q_RANK + KV_RANK + SSM_W]
    u_p = _time_perm(u)
    disc_in = [ws['ssm_lr'], ws['ssm_li'], ws['ssm_ldt'], ws['ssm_br'], ws['ssm_bi']]
    ar16, ai16, bbr, bbi = _rowwise(_f_disc, disc_in, [], [(SSM_P, F32)] * 4, name="s5_disc")
    a_r = ar16[::SSM_GRP].reshape(1, SSM_N)
    a_i = ai16[::SSM_GRP].reshape(1, SSM_N)
    bblk_r, bblk_i = _block_diag_b(bbr).astype(BF16), _block_diag_b(bbi).astype(BF16)
    cblk_r, cblk_i = _block_diag_c(ws['ssm_cr']).astype(BF16), _block_diag_c(-ws['ssm_ci']).astype(BF16)
    bu_r = _mm_grouped(u_p, bblk_r, name="s5_bu_r")
    bu_i = _mm_grouped(u_p, bblk_i, name="s5_bu_i")
    xr, xi = _scan(bu_r, bu_i, a_r, a_i, reverse=False, name="s5_scan_fwd")
    yc = _mm_grouped(xr, cblk_r, name="s5_y_r")
    yc = _mm_grouped(xi, cblk_i, res=yc, name="s5_y_i")
    g_p = _rowwise(_f_s5_gelu, [yc, u_p], [ws['ssm_d']], [(SSM_W, F32)], name="s5_gelu")[0]
    z_p = _mm(g_p, wc['ssm_w_glu'], name="s5_glu")
    g_t, z_t = _time_perm(g_p, inverse=True), _time_perm(z_p, inverse=True)
    on_consts = [ws['ssm_b_glu'], ws['out_norm_mla'], ws['out_norm_ssm']]
    ycat = _rowwise(_f_outnorm, [o_mla, g_t, z_t], on_consts, [(D, BF16)], name="out_norm")[0]
    x2 = _mm(ycat, wc['w_o'], res=x1, name="w_o")

    hx = _rowwise(_f_norm, [x2], [ws['xattn_norm']], [(D, BF16)], name="xattn_norm")[0]
    xq = _mm(hx, wc['xattn_w_q'], name="xattn_q")
    mn = _rowwise(_f_norm, [mem], [ws['mem_norm']], [(D, BF16)], name="mem_norm")[0]
    kvm = _mm(mn, wc['xattn_w_kv'], name="xattn_kv")
    xkn, xv = _rowwise(_f_memk, [kvm], [ws['xattn_k_norm']], [(H * XH, BF16), (H * XH, BF16)], name="xattn_knorm")
    xo = _xattn_fwd(xq, xkn, xv, ws['xattn_q_norm'])
    x3 = _mm(xo, wc['xattn_w_o'], tb=True, res=x2, name="xattn_o")

    x4, sv2 = _ffn_fwd(x3, ws['ffn2_norm'], wc, "ffn2")

    def f_loss(yb, tb):
        err = yb - tb
        return err * (1.0 / D), jnp.broadcast_to(jnp.sum(jnp.sum(err * err, axis=1, keepdims=True), axis=0, keepdims=True) * (0.5 / D), (1, LANES))

    dx4, loss = _rowwise(f_loss, [x4, target], [], [(D, F32)], [(1, LANES)], name="loss")

    dx3, gs['ffn2_norm'] = _ffn_bwd(x3, ws['ffn2_norm'], wc, sv2, dx4, "ffn2", send)

    dxo = _mm(dx3, wc['xattn_w_o'], out_dtype=BF16, name="xattn_o_dx")
    send({'xattn_w_o': _mm(dx3, xo, ta=True, out_dtype=GRAD_DTYPE, name="xattn_o_dw")})
    dxq, dxkn, dxv, gs['xattn_q_norm'] = _xattn_bwd(xq, xkn, xv, ws['xattn_q_norm'], dxo)
    dkvm, gs['xattn_k_norm'] = _rowwise_bwd(_f_memk, [kvm], [ws['xattn_k_norm']], [dxkn, dxv], row_grads={0: BF16},
                                            const_grads=[0], name="xattn_knorm_bwd")
    send({'xattn_w_kv': _mm(mn, dkvm, ta=True, out_dtype=GRAD_DTYPE, name="xattn_kv_dw")})
    dmn = _mm(dkvm, wc['xattn_w_kv'], tb=True, out_dtype=BF16, name="xattn_kv_dx")
    gs['mem_norm'] = _rowwise_bwd(_f_norm, [mem], [ws['mem_norm']], [dmn], row_grads={}, const_grads=[0], name="mem_norm_bwd")[0]
    send({'xattn_w_q': _mm(hx, dxq, ta=True, out_dtype=GRAD_DTYPE, name="xattn_q_dw")})
    dhx = _mm(dxq, wc['xattn_w_q'], tb=True, out_dtype=BF16, name="xattn_q_dx")
    dx2, gs['xattn_norm'] = _rowwise_bwd(_f_norm, [x2], [ws['xattn_norm']], [dhx], row_grads={0: F32}, const_grads=[0],
                                         adds={0: dx3}, name="xattn_norm_bwd")

    dycat = _mm(dx2, wc['w_o'], tb=True, out_dtype=BF16, name="w_o_dx")
    send({'w_o': _mm(ycat, dx2, ta=True, out_dtype=GRAD_DTYPE, name="w_o_dw")})
    do_mla, dg_t, dz_t, gs['ssm_b_glu'], gs['out_norm_mla'], gs['out_norm_ssm'] = _rowwise_bwd(
        _f_outnorm, [o_mla, g_t, z_t], on_consts, [dycat], row_grads={0: F32, 1: F32, 2: BF16}, const_grads=[0, 1, 2],
        name="out_norm_bwd")

    dz_p, dg_p = _time_perm(dz_t), _time_perm(dg_t)
    send({'ssm_w_glu': _mm(g_p, dz_p, ta=True, out_dtype=GRAD_DTYPE, name="s5_glu_dw")})
    dg_p = _mm(dz_p, wc['ssm_w_glu'], tb=True, res=dg_p, name="s5_glu_dx")
    dyc, du_d, gs['ssm_d'] = _rowwise_bwd(_f_s5_gelu, [yc, u_p], [ws['ssm_d']], [dg_p], row_grads={0: BF16, 1: F32},
                                          const_grads=[0], name="s5_gelu_bwd")
    n_state, n_chan = SSM_PACK * SSM_P, SSM_PACK * SSM_GRP
    d_cblk_r = _mm_grouped_tn(xr, dyc, ka=n_state, kb=n_chan, name="s5_dc_r")
    d_cblk_i = _mm_grouped_tn(xi, dyc, ka=n_state, kb=n_chan, name="s5_dc_i")
    dxr = _mm_grouped(dyc, cblk_r, tb=True, name="s5_dx_r")
    dxi = _mm_grouped(dyc, cblk_i, tb=True, name="s5_dx_i")
    lam_r, lam_i = _scan(dxr, dxi, a_r, -a_i, reverse=True, name="s5_scan_bwd")
    d_ar, d_ai = _scan_da(lam_r, lam_i, xr, xi)
    d_bblk_r = _mm_grouped_tn(u_p, lam_r, ka=n_chan, kb=n_state, name="s5_db_r")
    d_bblk_i = _mm_grouped_tn(u_p, lam_i, ka=n_chan, kb=n_state, name="s5_db_i")
    du_p = _mm_grouped(lam_r, bblk_r, tb=True, res=du_d, name="s5_du_r")
    du_p = _mm_grouped(lam_i, bblk_i, tb=True, res=du_p, name="s5_du_i")
    du = _time_perm(du_p, inverse=True)
    gs['ssm_cr'] = jax.linear_transpose(_block_diag_c, ws['ssm_cr'])(d_cblk_r)[0]
    gs['ssm_ci'] = -jax.linear_transpose(_block_diag_c, ws['ssm_ci'])(d_cblk_i)[0]
    d_bbr = jax.linear_transpose(_block_diag_b, bbr)(d_bblk_r)[0]
    d_bbi = jax.linear_transpose(_block_diag_b, bbi)(d_bblk_i)[0]
    d_ar16 = jnp.zeros((SSM_G * SSM_GRP, SSM_P), F32).at[::SSM_GRP].set(d_ar.reshape(SSM_G, SSM_P))
    d_ai16 = jnp.zeros((SSM_G * SSM_GRP, SSM_P), F32).at[::SSM_GRP].set(d_ai.reshape(SSM_G, SSM_P))
    gs['ssm_lr'], gs['ssm_li'], gs['ssm_ldt'], gs['ssm_br'], gs['ssm_bi'] = _rowwise_bwd(
        _f_disc, disc_in, [], [d_ar16, d_ai16, d_bbr, d_bbi], row_grads={i: F32 for i in range(5)}, const_grads=[],
        name="s5_disc_bwd")

    delta, do_b = _rowwise(_f_delta, [do_mla, o_mla], [], [(H * LANES, F32), (H * VD, BF16)], name="mla_delta")
    dq, dk, dv = _attn_bwd(q, k, v, do_b, lse, delta)
    dqall, dkv, dkr, gs['qk_gq'], gs['qk_gk'] = _prep2_bwd(qall, kv, kr, cos, sin, ws['qk_gq'], ws['qk_gk'], dq, dk, dv)
    d_w_uq_e = _mm(dqall, c_q, ta=True, name="w_uq_dw")
    send({'mla_w_uq': jax.linear_transpose(_expand_w_uq, jax.ShapeDtypeStruct(w_uq_raw.shape, F32))(d_w_uq_e)[0]})
    dc_q = _mm(dqall, w_uq_e, out_dtype=BF16, name="w_uq_dx")
    send({'mla_w_ukv': _mm(dkv, c_kv, ta=True, out_dtype=GRAD_DTYPE, name="w_ukv_dw")})
    dc_kv = _mm(dkv, wc['mla_w_ukv'], out_dtype=BF16, name="w_ukv_dx")

    def f_prep1_bwd(pb, dcq, dckv, dub, dkrb, gq, gkv):
        _, vjp = jax.vjp(_f_prep1, pb[:, :Q_RANK + KV_RANK], gq, gkv)
        dpa, dgq, dgkv = vjp((dcq.astype(BF16), dckv.astype(BF16)))
        return jnp.concatenate([dpa, dub, dkrb], axis=-1), dgq, dgkv

    dproj, gs['q_norm'], gs['kv_norm'] = _rowwise(
        f_prep1_bwd, [proj, dc_q, dc_kv, du, dkr], [ws['q_norm'], ws['kv_norm']], [(IN_WP, BF16)],
        [(1, Q_RANK), (1, KV_RANK)], name="mla_prep1_bwd")
    d_w_in_e = _mm(dproj, h2, ta=True, name="w_in_dw")
    token = send({'w_in': jax.linear_transpose(_expand_w_in, jax.ShapeDtypeStruct(w_in_raw.shape, F32))(d_w_in_e)[0]})
    dh2 = _mm(dproj, w_in_e, out_dtype=BF16, name="w_in_dx")
    dx1, gs['mix_norm'] = _rowwise_bwd(_f_norm, [x1], [ws['mix_norm']], [dh2], row_grads={0: F32}, const_grads=[0],
                                       adds={0: dx2}, name="mix_norm_bwd", deps=[token])

    dx0, gs['ffn1_norm'] = _ffn_bwd(x, ws['ffn1_norm'], wc, sv1, dx1, "ffn1", send)
    return loss, dx0, gs


def _prep2_fwd(qall, kv, kr, cos, sin, gq, gk):
    return _rowwise(_f_prep2, [qall, kv, kr, cos, sin], [gq, gk], [(H * HQ, BF16), (H * HQ, BF16), (H * VD, BF16)],
                    ts=256, name="mla_prep2")


def _prep2_bwd(qall, kv, kr, cos, sin, gq, gk, dq, dk, dv):
    return _rowwise_bwd(_f_prep2, [qall, kv, kr, cos, sin], [gq, gk], [dq, dk, dv], row_grads={0: BF16, 1: BF16, 2: F32},
                        const_grads=[0, 1], ts=256, name="mla_prep2_bwd")


def _rope_tables(pos):
    half = ROPE // 2
    inv = ROPE_THETA ** (-jnp.arange(half, dtype=F32) / half)
    ang = pos.astype(F32)[:, None] * inv[None, :]
    z = jnp.zeros((pos.shape[0], LANES - ROPE), F32)
    cos, sin = jnp.cos(ang), jnp.sin(ang)
    return jnp.concatenate([cos, cos, z], axis=-1), jnp.concatenate([sin, sin, z], axis=-1)


def _small_layout(p):
    lr, li, ldt, br, bi = _layout_ssm_in(p['ssm_a_re'], p['ssm_a_im'], p['ssm_log_dt'], p['ssm_b_re'], p['ssm_b_im'])
    return {
        'ffn1_norm': p['ffn1_norm'].reshape(1, D), 'mix_norm': p['mix_norm'].reshape(1, D),
        'q_norm': p['mla_q_norm'].reshape(1, Q_RANK), 'kv_norm': p['mla_kv_norm'].reshape(1, KV_RANK),
        'qk_gq': _layout_qk_gain(p['mla_qk_norm_q']), 'qk_gk': _layout_qk_gain(p['mla_qk_norm_k']),
        'ssm_lr': lr, 'ssm_li': li, 'ssm_ldt': ldt, 'ssm_br': br, 'ssm_bi': bi,
        'ssm_cr': p['ssm_c_re'], 'ssm_ci': p['ssm_c_im'], 'ssm_d': p['ssm_d'].reshape(1, SSM_W),
        'ssm_b_glu': p['ssm_b_glu'].reshape(1, SSM_W),
        'out_norm_mla': p['out_norm_mla'].reshape(1, SSM_W), 'out_norm_ssm': p['out_norm_ssm'].reshape(1, SSM_W),
        'xattn_norm': p['xattn_norm'].reshape(1, D), 'mem_norm': p['mem_norm'].reshape(1, D),
        'xattn_q_norm': p['xattn_q_norm'].reshape(1, XH), 'xattn_k_norm': p['xattn_k_norm'].reshape(1, XH),
        'ffn2_norm': p['ffn2_norm'].reshape(1, D),
    }


def _pack(arrs, rows):
    flat = jnp.concatenate([a.reshape(-1) for a in arrs])
    return jnp.pad(flat, (0, rows * D - flat.shape[0])).reshape(rows, D)


def _unpack(flat, shapes):
    flat = flat.reshape(-1)
    out, off = [], 0
    for sh in shapes:
        n = int(np.prod(sh))
        out.append(flat[off:off + n].reshape(sh))
        off += n
    return out


def kernel(x, mem, positions, ffn1_norm, ffn1_w_gate, ffn1_w_up, ffn1_w_down, mix_norm, w_in, mla_q_norm, mla_w_uq, mla_kv_norm, mla_w_ukv, mla_qk_norm_q, mla_qk_norm_k, ssm_a_re, ssm_a_im, ssm_log_dt, ssm_b_re, ssm_b_im, ssm_c_re, ssm_c_im, ssm_d, ssm_w_glu, ssm_b_glu, out_norm_mla, out_norm_ssm, w_o, xattn_norm, mem_norm, xattn_w_q, xattn_w_kv, xattn_q_norm, xattn_k_norm, xattn_w_o, ffn2_norm, ffn2_w_gate, ffn2_w_up, ffn2_w_down, loss_target, m_ffn1_norm, m_ffn1_w_gate, m_ffn1_w_up, m_ffn1_w_down, m_mix_norm, m_w_in, m_mla_q_norm, m_mla_w_uq, m_mla_kv_norm, m_mla_w_ukv, m_mla_qk_norm_q, m_mla_qk_norm_k, m_ssm_a_re, m_ssm_a_im, m_ssm_log_dt, m_ssm_b_re, m_ssm_b_im, m_ssm_c_re, m_ssm_c_im, m_ssm_d, m_ssm_w_glu, m_ssm_b_glu, m_out_norm_mla, m_out_norm_ssm, m_w_o, m_xattn_norm, m_mem_norm, m_xattn_w_q, m_xattn_w_kv, m_xattn_q_norm, m_xattn_k_norm, m_xattn_w_o, m_ffn2_norm, m_ffn2_w_gate, m_ffn2_w_up, m_ffn2_w_down, v_ffn1_norm, v_ffn1_w_gate, v_ffn1_w_up, v_ffn1_w_down, v_mix_norm, v_w_in, v_mla_q_norm, v_mla_w_uq, v_mla_kv_norm, v_mla_w_ukv, v_mla_qk_norm_q, v_mla_qk_norm_k, v_ssm_a_re, v_ssm_a_im, v_ssm_log_dt, v_ssm_b_re, v_ssm_b_im, v_ssm_c_re, v_ssm_c_im, v_ssm_d, v_ssm_w_glu, v_ssm_b_glu, v_out_norm_mla, v_out_norm_ssm, v_w_o, v_xattn_norm, v_mem_norm, v_xattn_w_q, v_xattn_w_kv, v_xattn_q_norm, v_xattn_k_norm, v_xattn_w_o, v_ffn2_norm, v_ffn2_w_gate, v_ffn2_w_up, v_ffn2_w_down):
    args = dict(locals())
    w = {n: args[n] for n in WEIGHTS}
    mom = {n: args['m_' + n] for n in WEIGHTS}
    var = {n: args['v_' + n] for n in WEIGHTS}
    return _step(x, mem, positions, loss_target, w, mom, var)


GATHER_GROUPS = [('ffn1', ['ffn1_w_gate', 'ffn1_w_up', 'ffn1_w_down']),
                 ('mix', ['w_in', 'mla_w_uq', 'mla_w_ukv', 'ssm_w_glu', 'w_o', 'xattn_w_q', 'xattn_w_kv', 'xattn_w_o']),
                 ('ffn2', ['ffn2_w_gate', 'ffn2_w_up', 'ffn2_w_down'])]
SCATTER_GROUPS = [('ffn2', ['ffn2_w_down', 'ffn2_w_gate', 'ffn2_w_up']),
                  ('mix', ['xattn_w_o', 'xattn_w_kv', 'xattn_w_q', 'w_o', 'ssm_w_glu', 'mla_w_uq', 'mla_w_ukv', 'w_in']),
                  ('ffn1_down', ['ffn1_w_down']), ('ffn1_gu', ['ffn1_w_gate', 'ffn1_w_up'])]


def _step(x, mem, positions, loss_target, w, mom, var):
    blocks = {n: _to_exchange_layout(n, w[n][0]).astype(BF16) for n in SHARDED}
    gathers, token = [], None
    for tag, names in GATHER_GROUPS:
        ex = _Exchange([blocks[n] for n in names], [blocks[n].shape[0] for n in names], gather=True,
                       name="gather_" + tag, after=token)
        gathers.append((names, ex))
        token = ex.token
    me = _my_slot()
    wc = _Weights(gathers, me=me)

    rows = {n: (blocks[n].shape[0], blocks[n].shape[0]) for n in SHARDED}
    ready, scatters = {}, []

    def send(grads):
        ready.update({n: g.astype(GRAD_DTYPE) for n, g in grads.items()})
        for tag, names in SCATTER_GROUPS:
            if all(n in ready for n in names) and not any(t == tag for t, _, _ in scatters):
                ex = _Exchange([ready[n] for n in names], [rows[n] for n in names], gather=False, name="scatter_" + tag)
                scatters.append((tag, names, ex))
                return ex.token
        return None

    small = {n: w[n][0] for n in SMALL}
    ws = _small_layout(small)
    cos, sin = _rope_tables(positions[0])
    loss, dx, gs = _local_step(x[0], mem[0], cos, sin, loss_target[0], wc, ws, send, deps=[token])

    g_small = jax.linear_transpose(_small_layout, {n: jax.ShapeDtypeStruct(small[n].shape, F32) for n in SMALL})(gs)[0]
    small_shapes = [small[n].shape for n in SMALL]
    n_small = sum(int(np.prod(sh)) for sh in small_shapes) + 1
    rows_small = -(-n_small // (8 * D)) * 8
    small_pack = _pack([g_small[n] for n in SMALL] + [loss[0, :1]], rows_small)
    small_ex = _Exchange([small_pack], [(None, rows_small)], gather=False, name="scatter_small")

    out = {}
    for _, names, ex in scatters:
        for n, sent, p in zip(names, *ex.wait(dx)):
            r = w[n][0].shape[SHARD_AXIS[n]]
            if SHARD_AXIS[n] == 0:
                out[n] = _sum_adamw(me, sent, rows[n][0], p, r, w[n][0], mom[n][0], var[n][0], name="adamw_" + n)
            else:
                g = _sum_adamw(me, sent, rows[n][0], p, r, name="sum_" + n)[0].T
                out[n] = [g] + _adamw(g, w[n][0], mom[n][0], var[n][0], name="adamw_" + n)
    state = [_pack([t[n][0] for n in SMALL], rows_small) for t in (w, mom, var)]
    sent, p = small_ex.wait(dx)
    small_out = _sum_adamw(me, sent[0], None, p[0], rows_small, *state, name="adamw_small")
    loss_total = small_out[0].reshape(-1)[n_small - 1]
    for n, vals in zip(SMALL, zip(*[_unpack(flat, small_shapes) for flat in small_out])):
        out[n] = vals
    outs = [out[n][i][None] for i in range(4) for n in WEIGHTS]
    return (loss_total, dx[None], *outs)
```

```python
import math

import jax
import jax.numpy as jnp
import numpy as np
from jax import lax
from jax.experimental import pallas as pl
from jax.experimental.pallas import tpu as pltpu

F32 = jnp.float32
BF16 = jnp.bfloat16

N_DEV = 8
D = 1024
D_FF = 2752
D_FFP = 2816
MEM_LEN = 256
H = 4
Q_RANK, KV_RANK, NOPE, ROPE, VD = 384, 256, 128, 64, 128
QK = NOPE + ROPE
HQ = 2 * 128
SSM_W, SSM_G, SSM_GRP, SSM_P = 512, 32, 16, 64
SSM_N = SSM_G * SSM_P
SSM_PACK = 8
IN_W = 1216
IN_WP = 1408
XH = 128
EPS = 1e-6
LN2 = math.log(2.0)
ROPE_THETA = 10000.0
SCAN_CHUNKS = 8

ADAM_LR, ADAM_B1, ADAM_B2, ADAM_EPS, ADAM_WD, ADAM_STEP = 0.001, 0.9, 0.999, 1e-08, 0.01, 10

VMEM_LIMIT = 56 * 1024 * 1024
ACC_BYTES = 6 * 1024 * 1024
LANES = 128
BF16_ROWS = 16
GRAD_DTYPE = BF16
FF_SHARD = D_FF // N_DEV
FF_SHARD_P = 352
IN_SHARD = IN_W // N_DEV
IN_SHARD_P = 160

WEIGHTS = ['ffn1_norm', 'ffn1_w_gate', 'ffn1_w_up', 'ffn1_w_down', 'mix_norm', 'w_in', 'mla_q_norm', 'mla_w_uq',
           'mla_kv_norm', 'mla_w_ukv', 'mla_qk_norm_q', 'mla_qk_norm_k', 'ssm_a_re', 'ssm_a_im', 'ssm_log_dt',
           'ssm_b_re', 'ssm_b_im', 'ssm_c_re', 'ssm_c_im', 'ssm_d', 'ssm_w_glu', 'ssm_b_glu', 'out_norm_mla',
           'out_norm_ssm', 'w_o', 'xattn_norm', 'mem_norm', 'xattn_w_q', 'xattn_w_kv', 'xattn_q_norm',
           'xattn_k_norm', 'xattn_w_o', 'ffn2_norm', 'ffn2_w_gate', 'ffn2_w_up', 'ffn2_w_down']
SHARD_AXIS = {'ffn1_w_gate': 1, 'ffn1_w_up': 1, 'ffn1_w_down': 0, 'w_in': 1, 'mla_w_uq': 1, 'mla_w_ukv': 1,
              'ssm_w_glu': 0, 'w_o': 0, 'xattn_w_q': 0, 'xattn_w_kv': 0, 'xattn_w_o': 1,
              'ffn2_w_gate': 1, 'ffn2_w_up': 1, 'ffn2_w_down': 0}
SHARDED = [n for n in WEIGHTS if n in SHARD_AXIS]
SMALL = [n for n in WEIGHTS if n not in SHARD_AXIS]


def _params(sem=None):
    return pltpu.CompilerParams(dimension_semantics=sem, vmem_limit_bytes=VMEM_LIMIT)


def _tile(n, cap):
    if n <= cap:
        return n
    best = n
    for t in range(LANES, cap + 1, LANES):
        if n % t == 0:
            best = t
    return best


def _mm(a, b, *, ta=False, tb=False, out_dtype=F32, res=None, scale=1.0, name, tm_cap=512, tn_cap=1408, tk_cap=2816):
    m, k = (a.shape[1], a.shape[0]) if ta else a.shape
    k2, n = (b.shape[1], b.shape[0]) if tb else b.shape
    assert k == k2, (a.shape, b.shape, ta, tb)
    if ta:
        tk_cap = min(tk_cap, 512)
        tm_cap = 1408
    tm, tn, tk = _tile(m, tm_cap), _tile(n, tn_cap), _tile(k, tk_cap)
    if tm * tn * 4 > ACC_BYTES:
        tn = _tile(n, max(LANES, ACC_BYTES // (4 * tm) // LANES * LANES))
    nk = k // tk
    dims = (((0 if ta else 1,), (1 if tb else 0,)), ((), ()))
    has_res = res is not None

    def body(*refs):
        if has_res:
            a_ref, b_ref, r_ref, o_ref, acc_ref = refs
        else:
            a_ref, b_ref, o_ref, acc_ref = refs
        kk = pl.program_id(2)

        @pl.when(kk == 0)
        def _():
            acc_ref[...] = jnp.zeros_like(acc_ref)

        acc_ref[...] += lax.dot_general(a_ref[...].astype(BF16), b_ref[...].astype(BF16), dims,
                                        preferred_element_type=F32)

        @pl.when(kk == nk - 1)
        def _():
            out = acc_ref[...]
            if scale != 1.0:
                out = out * scale
            if has_res:
                out = out + r_ref[...].astype(F32)
            o_ref[...] = out.astype(o_ref.dtype)

    a_spec = pl.BlockSpec((tk, tm), lambda i, j, kk: (kk, i)) if ta else pl.BlockSpec((tm, tk), lambda i, j, kk: (i, kk))
    b_spec = pl.BlockSpec((tn, tk), lambda i, j, kk: (j, kk)) if tb else pl.BlockSpec((tk, tn), lambda i, j, kk: (kk, j))
    o_spec = pl.BlockSpec((tm, tn), lambda i, j, kk: (i, j))
    in_specs = [a_spec, b_spec] + ([o_spec] if has_res else [])
    args = (a, b) + ((res,) if has_res else ())
    return pl.pallas_call(
        body, name=name, grid=(m // tm, n // tn, nk), in_specs=in_specs, out_specs=o_spec,
        out_shape=jax.ShapeDtypeStruct((m, n), out_dtype), scratch_shapes=[pltpu.VMEM((tm, tn), F32)],
        compiler_params=_params(("parallel", "parallel", "arbitrary")),
    )(*args)


def _mm_grouped(a, b, *, tb=False, res=None, out_dtype=F32, name, tm=512):
    s = a.shape[0]
    g = b.shape[0]
    nb, ka = (b.shape[1], b.shape[2]) if tb else (b.shape[2], b.shape[1])
    assert a.shape[1] == g * ka
    tm = min(tm, s)
    dims = (((1,), (1 if tb else 0,)), ((), ()))
    has_res = res is not None

    def body(*refs):
        if has_res:
            a_ref, b_ref, r_ref, o_ref = refs
        else:
            a_ref, b_ref, o_ref = refs
        out = lax.dot_general(a_ref[...].astype(BF16), b_ref[...].astype(BF16), dims, preferred_element_type=F32)
        if has_res:
            out = out + r_ref[...].astype(F32)
        o_ref[...] = out.astype(o_ref.dtype)

    o_spec = pl.BlockSpec((tm, nb), lambda i, j: (i, j))
    in_specs = [pl.BlockSpec((tm, ka), lambda i, j: (i, j)), pl.BlockSpec((None,) + b.shape[1:], lambda i, j: (j, 0, 0))]
    return pl.pallas_call(
        body, name=name, grid=(s // tm, g), in_specs=in_specs + ([o_spec] if has_res else []), out_specs=o_spec,
        out_shape=jax.ShapeDtypeStruct((s, g * nb), out_dtype), compiler_params=_params(("parallel", "parallel")),
    )(a, b, *((res,) if has_res else ()))


def _mm_grouped_tn(a, b, *, ka, kb, name, tk=512):
    s = a.shape[0]
    g = a.shape[1] // ka
    assert b.shape[1] == g * kb
    tk = min(tk, s)
    nk = s // tk

    def body(a_ref, b_ref, o_ref):
        part = lax.dot_general(a_ref[...].astype(BF16), b_ref[...].astype(BF16), (((0,), (0,)), ((), ())),
                               preferred_element_type=F32)

        @pl.when(pl.program_id(1) == 0)
        def _():
            o_ref[...] = part

        @pl.when(pl.program_id(1) > 0)
        def _():
            o_ref[...] += part

    return pl.pallas_call(
        body, name=name, grid=(g, nk),
        in_specs=[pl.BlockSpec((tk, ka), lambda j, kk: (kk, j)), pl.BlockSpec((tk, kb), lambda j, kk: (kk, j))],
        out_specs=pl.BlockSpec((None, ka, kb), lambda j, kk: (j, 0, 0)),
        out_shape=jax.ShapeDtypeStruct((g, ka, kb), F32), compiler_params=_params(("parallel", "arbitrary")),
    )(a, b)


def _rowwise(fn, rows, consts, outs, accs=(), *, ts=512, name, deps=()):
    s = rows[0].shape[0]
    ts = min(ts, s)
    assert s % ts == 0
    n_rows, n_consts, n_outs = len(rows), len(consts), len(outs)
    deps = [d for d in deps if d is not None]
    consts = list(consts) + deps

    def body(*refs):
        ins = [r[...] for r in refs[:n_rows + n_consts]]
        res = fn(*ins)
        res = tuple(res) if isinstance(res, (tuple, list)) else (res,)
        out_refs = refs[n_rows + len(consts):]
        for o_ref, val in zip(out_refs[:n_outs], res[:n_outs]):
            o_ref[...] = val.astype(o_ref.dtype)
        if accs:
            first = pl.program_id(0) == 0

            @pl.when(first)
            def _():
                for a_ref, val in zip(out_refs[n_outs:], res[n_outs:]):
                    a_ref[...] = val.astype(F32)

            @pl.when(jnp.logical_not(first))
            def _():
                for a_ref, val in zip(out_refs[n_outs:], res[n_outs:]):
                    a_ref[...] += val.astype(F32)

    in_specs = [pl.BlockSpec((ts, r.shape[1]), lambda i: (i, 0)) for r in rows]
    in_specs += [pl.BlockSpec(c.shape, lambda i: (0, 0)) for c in consts]
    out_specs = [pl.BlockSpec((ts, w), lambda i: (i, 0)) for w, _ in outs]
    out_specs += [pl.BlockSpec(tuple(sh), lambda i: (0, 0)) for sh in accs]
    out_shape = [jax.ShapeDtypeStruct((s, w), dt) for w, dt in outs]
    out_shape += [jax.ShapeDtypeStruct(tuple(sh), F32) for sh in accs]
    res = pl.pallas_call(
        body, name=name, grid=(s // ts,), in_specs=in_specs, out_specs=out_specs, out_shape=out_shape,
        compiler_params=_params(("arbitrary",)),
    )(*rows, *consts)
    return res


def _rowwise_bwd(f, rows, consts, cts, *, row_grads, const_grads, adds=None, ts=512, name, deps=()):
    adds = adds or {}
    n_rows, n_consts, n_cts = len(rows), len(consts), len(cts)
    add_keys = sorted(adds)
    rg = sorted(row_grads)
    cg = sorted(const_grads)

    def fn(*args):
        r = args[:n_rows]
        c = args[n_rows:n_rows + n_consts]
        ct = args[n_rows + n_consts:n_rows + n_consts + n_cts]
        extra = args[n_rows + n_consts + n_cts:]
        outs, vjp = jax.vjp(f, *r, *c)
        outs = tuple(outs) if isinstance(outs, (tuple, list)) else (outs,)
        cot = tuple(g.astype(o.dtype) for g, o in zip(ct, outs))
        grads = vjp(cot if len(cot) > 1 else cot[0])
        res = []
        for i in rg:
            g = grads[i].astype(F32)
            if i in adds:
                g = g + extra[add_keys.index(i)].astype(F32)
            res.append(g)
        for i in cg:
            res.append(grads[n_rows + i])
        return tuple(res)

    rows_all = list(rows) + list(cts) + [adds[i] for i in add_keys]
    def fn2(*args):
        nr = len(rows_all)
        rr, cc = args[:nr], args[nr:]
        return fn(*rr[:n_rows], *cc, *rr[n_rows:])

    outs = [(rows[i].shape[1], row_grads[i]) for i in rg]
    accs = [consts[i].shape for i in cg]
    return _rowwise(fn2, rows_all, list(consts), outs, accs, ts=ts, name=name, deps=deps)


def _rms(x, g):
    xf = x.astype(F32)
    return xf * lax.rsqrt(jnp.mean(xf * xf, axis=-1, keepdims=True) + EPS) * g.astype(F32)


def _sigmoid(x):
    return 1.0 / (1.0 + jnp.exp(-x))


def _f_norm(x, g):
    return _rms(x, g).astype(BF16)


def _f_swiglu(gate, up):
    gate, up = gate.astype(F32), up.astype(F32)
    return (gate * _sigmoid(gate) * up).astype(BF16)


def _f_prep1(proj, gq, gkv):
    return _rms(proj[:, :Q_RANK], gq).astype(BF16), _rms(proj[:, Q_RANK:Q_RANK + KV_RANK], gkv).astype(BF16)


def _f_kr(proj):
    return (proj[:, Q_RANK + KV_RANK + SSM_W:],)


def _f_prep2(qall, kv, kr2, cos, sin, gq, gk):
    kr, krs = kr2[:, :LANES].astype(F32), kr2[:, LANES:].astype(F32)
    k_rot = kr * gk[1:2] * cos + krs * gk[2:3] * sin
    k_ss = jnp.sum(kr * kr, axis=-1, keepdims=True)
    q_scale = QK ** -0.5 / LN2
    qs, ks, vs = [], [], []
    for h in range(H):
        qn = qall[:, h * LANES:(h + 1) * LANES].astype(F32)
        qr = qall[:, (H + h) * LANES:(H + h + 1) * LANES].astype(F32)
        qrs = qall[:, (2 * H + h) * LANES:(2 * H + h + 1) * LANES].astype(F32)
        rstd = lax.rsqrt((jnp.sum(qn * qn, axis=-1, keepdims=True) + jnp.sum(qr * qr, axis=-1, keepdims=True)) / QK + EPS)
        rstd = rstd * q_scale
        qs += [qn * gq[0:1] * rstd, (qr * gq[1:2] * cos + qrs * gq[2:3] * sin) * rstd]
        kn = kv[:, 2 * h * LANES:(2 * h + 1) * LANES].astype(F32)
        rstd_k = lax.rsqrt((jnp.sum(kn * kn, axis=-1, keepdims=True) + k_ss) / QK + EPS)
        ks += [kn * gk[0:1] * rstd_k, k_rot * rstd_k]
        vs.append(kv[:, (2 * h + 1) * LANES:(2 * h + 2) * LANES])
    return (jnp.concatenate(qs, axis=-1).astype(BF16), jnp.concatenate(ks, axis=-1).astype(BF16),
            jnp.concatenate(vs, axis=-1).astype(BF16))


def _gelu(x):
    return 0.5 * x * (1.0 + jnp.tanh(math.sqrt(2.0 / math.pi) * (x + 0.044715 * (x * x * x))))


def _f_s5_gelu(yc, u, d):
    return _gelu(yc.astype(F32) + d * u.astype(F32))


def _f_outnorm(o_mla, g, z, b_glu, g_om, g_os):
    y_ssm = g * _sigmoid(z + b_glu)
    return jnp.concatenate([_rms(o_mla, g_om), _rms(y_ssm, g_os)], axis=-1).astype(BF16)


def _f_memk(kvm, gk):
    ks = [_rms(kvm[:, h * XH:(h + 1) * XH], gk) for h in range(H)]
    return jnp.concatenate(ks, axis=-1).astype(BF16), kvm[:, H * XH:].astype(BF16)


def _f_disc(lr, li, log_dt, br, bi):
    dt = jnp.exp(log_dt)
    decay = jnp.exp(lr * dt)
    ar = decay * jnp.cos(li * dt)
    ai = decay * jnp.sin(li * dt)
    den = lr * lr + li * li
    nr = ar - 1.0
    coef_r = (nr * lr + ai * li) / den
    coef_i = (ai * lr - nr * li) / den
    return ar, ai, coef_r * br - coef_i * bi, coef_r * bi + coef_i * br


def _causal_mask(i, j, tq, tk):
    qpos = i * tq + lax.broadcasted_iota(jnp.int32, (tq, tk), 0)
    kpos = j * tk + lax.broadcasted_iota(jnp.int32, (tq, tk), 1)
    return qpos >= kpos


def _attn_fwd(q, k, v, *, t=512):
    s = q.shape[0]
    t = min(t, s)
    nb = s // t

    def body(q_ref, k_ref, v_ref, o_ref, lse_ref, m_sc, l_sc, acc_sc):
        i, j = pl.program_id(1), pl.program_id(2)

        @pl.when(j == 0)
        def _():
            m_sc[...] = jnp.full_like(m_sc, -jnp.inf)
            l_sc[...] = jnp.zeros_like(l_sc)
            acc_sc[...] = jnp.zeros_like(acc_sc)

        def block(diagonal):
            sc = lax.dot_general(q_ref[...], k_ref[...], (((1,), (1,)), ((), ())), preferred_element_type=F32)
            if diagonal:
                sc = jnp.where(_causal_mask(i, j, t, t), sc, -jnp.inf)
            m_old = m_sc[...]
            m_new = jnp.maximum(m_old, jnp.max(sc, axis=-1, keepdims=True))
            p = jnp.exp2(sc - m_new)
            alpha = jnp.exp2(m_old - m_new)
            l_sc[...] = alpha * l_sc[...] + jnp.sum(p, axis=-1, keepdims=True)
            acc_sc[...] = alpha * acc_sc[...] + jnp.dot(p.astype(BF16), v_ref[...], preferred_element_type=F32)
            m_sc[...] = m_new

        pl.when(j < i)(lambda: block(False))

        @pl.when(j == i)
        def _():
            block(True)
            o_ref[...] = acc_sc[...] / l_sc[...]
            lse_ref[...] = jnp.broadcast_to(m_sc[...] + jnp.log2(l_sc[...]), lse_ref.shape)

    kv_map = lambda h, i, j: (jnp.minimum(j, i), h)
    return pl.pallas_call(
        body, name="mla_attn_fwd", grid=(H, nb, nb),
        in_specs=[pl.BlockSpec((t, HQ), lambda h, i, j: (i, h)), pl.BlockSpec((t, HQ), kv_map),
                  pl.BlockSpec((t, VD), kv_map)],
        out_specs=[pl.BlockSpec((t, VD), lambda h, i, j: (i, h)), pl.BlockSpec((t, LANES), lambda h, i, j: (i, h))],
        out_shape=[jax.ShapeDtypeStruct((s, H * VD), F32), jax.ShapeDtypeStruct((s, H * LANES), F32)],
        scratch_shapes=[pltpu.VMEM((t, 1), F32), pltpu.VMEM((t, 1), F32), pltpu.VMEM((t, VD), F32)],
        compiler_params=_params(("parallel", "parallel", "arbitrary")),
    )(q, k, v)


def _attn_probs(q_ref, k_ref, v_ref, do_ref, lse_ref, dl_ref, i, j, t, diagonal):
    sc = lax.dot_general(q_ref[...], k_ref[...], (((1,), (1,)), ((), ())), preferred_element_type=F32)
    p = jnp.exp2(sc - lse_ref[...][:, :1])
    if diagonal:
        p = jnp.where(_causal_mask(i, j, t, t), p, 0.0)
    dp = lax.dot_general(do_ref[...], v_ref[...], (((1,), (1,)), ((), ())), preferred_element_type=F32)
    ds = p * (dp - dl_ref[...][:, :1])
    return p, ds


def _attn_bwd(q, k, v, do, lse, delta, *, t=512):
    s = q.shape[0]
    t = min(t, s)
    nb = s // t

    def dq_body(q_ref, k_ref, v_ref, do_ref, lse_ref, dl_ref, dq_ref, acc_sc):
        i, j = pl.program_id(1), pl.program_id(2)

        @pl.when(j == 0)
        def _():
            acc_sc[...] = jnp.zeros_like(acc_sc)

        def block(diagonal):
            _, ds = _attn_probs(q_ref, k_ref, v_ref, do_ref, lse_ref, dl_ref, i, j, t, diagonal)
            acc_sc[...] += jnp.dot(ds.astype(BF16), k_ref[...], preferred_element_type=F32)

        pl.when(j < i)(lambda: block(False))

        @pl.when(j == i)
        def _():
            block(True)
            dq_ref[...] = acc_sc[...] * LN2

    q_map = lambda h, i, j: (i, h)
    kv_map = lambda h, i, j: (jnp.minimum(j, i), h)
    dq = pl.pallas_call(
        dq_body, name="mla_attn_dq", grid=(H, nb, nb),
        in_specs=[pl.BlockSpec((t, HQ), q_map), pl.BlockSpec((t, HQ), kv_map), pl.BlockSpec((t, VD), kv_map),
                  pl.BlockSpec((t, VD), q_map), pl.BlockSpec((t, LANES), q_map), pl.BlockSpec((t, LANES), q_map)],
        out_specs=pl.BlockSpec((t, HQ), q_map),
        out_shape=jax.ShapeDtypeStruct((s, H * HQ), F32),
        scratch_shapes=[pltpu.VMEM((t, HQ), F32)],
        compiler_params=_params(("parallel", "parallel", "arbitrary")),
    )(q, k, v, do, lse, delta)

    def dkv_body(q_ref, k_ref, v_ref, do_ref, lse_ref, dl_ref, dk_ref, dv_ref, dk_sc, dv_sc):
        j, i = pl.program_id(1), pl.program_id(2)

        @pl.when(i == 0)
        def _():
            dk_sc[...] = jnp.zeros_like(dk_sc)
            dv_sc[...] = jnp.zeros_like(dv_sc)

        def block(diagonal):
            p, ds = _attn_probs(q_ref, k_ref, v_ref, do_ref, lse_ref, dl_ref, i, j, t, diagonal)
            dv_sc[...] += lax.dot_general(p.astype(BF16), do_ref[...], (((0,), (0,)), ((), ())), preferred_element_type=F32)
            dk_sc[...] += lax.dot_general(ds.astype(BF16), q_ref[...], (((0,), (0,)), ((), ())), preferred_element_type=F32)

        pl.when(i > j)(lambda: block(False))
        pl.when(i == j)(lambda: block(True))

        @pl.when(i == nb - 1)
        def _():
            dk_ref[...] = dk_sc[...] * LN2
            dv_ref[...] = dv_sc[...]

    q_map2 = lambda h, j, i: (jnp.maximum(i, j), h)
    kv_map2 = lambda h, j, i: (j, h)
    dk, dv = pl.pallas_call(
        dkv_body, name="mla_attn_dkv", grid=(H, nb, nb),
        in_specs=[pl.BlockSpec((t, HQ), q_map2), pl.BlockSpec((t, HQ), kv_map2), pl.BlockSpec((t, VD), kv_map2),
                  pl.BlockSpec((t, VD), q_map2), pl.BlockSpec((t, LANES), q_map2), pl.BlockSpec((t, LANES), q_map2)],
        out_specs=[pl.BlockSpec((t, HQ), kv_map2), pl.BlockSpec((t, VD), kv_map2)],
        out_shape=[jax.ShapeDtypeStruct((s, H * HQ), F32), jax.ShapeDtypeStruct((s, H * VD), F32)],
        scratch_shapes=[pltpu.VMEM((t, HQ), F32), pltpu.VMEM((t, VD), F32)],
        compiler_params=_params(("parallel", "parallel", "arbitrary")),
    )(q, k, v, do, lse, delta)
    return dq, dk, dv


def _f_delta(do, o):
    prod = do.astype(F32) * o.astype(F32)
    parts = [jnp.broadcast_to(jnp.sum(prod[:, h * VD:(h + 1) * VD], axis=-1, keepdims=True), (do.shape[0], LANES))
             for h in range(H)]
    return jnp.concatenate(parts, axis=-1), do.astype(BF16)


def _xattn_head(qh, kh, gq):
    qn = _rms(qh, gq) * (XH ** -0.5)
    sc = lax.dot_general(qn.astype(BF16), kh, (((1,), (1,)), ((), ())), preferred_element_type=F32)
    sc = sc - jnp.max(sc, axis=-1, keepdims=True)
    e = jnp.exp(sc)
    return qn, e / jnp.sum(e, axis=-1, keepdims=True)


def _xattn_fwd(q, kn, v, gq, *, ts=512):
    def fn(qb, knb, vb, g):
        outs = []
        for h in range(H):
            sl = slice(h * XH, (h + 1) * XH)
            _, p = _xattn_head(qb[:, sl], knb[:, sl], g)
            outs.append(jnp.dot(p.astype(BF16), vb[:, sl], preferred_element_type=F32))
        return (jnp.concatenate(outs, axis=-1),)

    return _rowwise(fn, [q], [kn, v, gq], [(H * XH, BF16)], ts=ts, name="xattn_fwd")[0]


def _xattn_bwd(q, kn, v, gq, do, *, ts=512):
    def fn(qb, dob, knb, vb, g):
        dqs, dks, dvs = [], [], []
        dg = jnp.zeros((1, XH), F32)
        for h in range(H):
            sl = slice(h * XH, (h + 1) * XH)
            qh, kh, vh, doh = qb[:, sl], knb[:, sl], vb[:, sl], dob[:, sl].astype(BF16)
            qn, p = _xattn_head(qh, kh, g)
            dp = lax.dot_general(doh, vh, (((1,), (1,)), ((), ())), preferred_element_type=F32)
            dvs.append(lax.dot_general(p.astype(BF16), doh, (((0,), (0,)), ((), ())), preferred_element_type=F32))
            ds = (p * (dp - jnp.sum(dp * p, axis=-1, keepdims=True))).astype(BF16)
            dqn = jnp.dot(ds, kh, preferred_element_type=F32)
            dks.append(lax.dot_general(ds, qn.astype(BF16), (((0,), (0,)), ((), ())), preferred_element_type=F32))
            _, vjp_n = jax.vjp(lambda a, b: _rms(a, b) * (XH ** -0.5), qh, g)
            dqh, dgh = vjp_n(dqn)
            dqs.append(dqh)
            dg = dg + dgh
        return (jnp.concatenate(dqs, axis=-1), jnp.concatenate(dks, axis=-1), jnp.concatenate(dvs, axis=-1), dg)

    return _rowwise(fn, [q, do], [kn, v, gq], [(H * XH, BF16)], [kn.shape, v.shape, gq.shape], ts=ts, name="xattn_bwd")


def _cmul(ar, ai, xr, xi):
    return ar * xr - ai * xi, ar * xi + ai * xr


def _scan(br, bi, ar, ai, *, reverse, cw=256, name):
    s, n = br.shape
    c = SCAN_CHUNKS
    tt = s // c
    cw = min(cw, n)

    def body(br_ref, bi_ref, ar_ref, ai_ref, xr_ref, xi_ref):
        a_r = jnp.broadcast_to(ar_ref[...], (c, cw))
        a_i = jnp.broadcast_to(ai_ref[...], (c, cw))
        zero = jnp.zeros((c, cw), F32)

        def row(step):
            t = (tt - 1 - step) if reverse else step
            return pl.ds(pl.multiple_of(t * c, c), c)

        def local(step, carry):
            sr, si, qr, qi = carry
            r = row(step)
            nr, ni = _cmul(a_r, a_i, sr, si)
            nr, ni = nr + br_ref[r, :], ni + bi_ref[r, :]
            xr_ref[r, :] = nr
            xi_ref[r, :] = ni
            return (nr, ni) + _cmul(a_r, a_i, qr, qi)

        end_r, end_i, pr, pi = lax.fori_loop(0, tt, local, (zero, zero, jnp.ones((c, cw), F32), zero))

        rows_id = lax.broadcasted_iota(jnp.int32, (c, cw), 0)
        car_r, car_i = zero, zero
        cur_r, cur_i = jnp.zeros((1, cw), F32), jnp.zeros((1, cw), F32)
        order = range(c - 1, -1, -1) if reverse else range(c)
        for kk in order:
            car_r = jnp.where(rows_id == kk, cur_r, car_r)
            car_i = jnp.where(rows_id == kk, cur_i, car_i)
            nr, ni = _cmul(pr[0:1], pi[0:1], cur_r, cur_i)
            cur_r = nr + end_r[kk:kk + 1]
            cur_i = ni + end_i[kk:kk + 1]

        def fix(step, carry):
            qr, qi = _cmul(a_r, a_i, *carry)
            r = row(step)
            dr, di = _cmul(qr, qi, car_r, car_i)
            xr_ref[r, :] += dr
            xi_ref[r, :] += di
            return qr, qi

        lax.fori_loop(0, tt, fix, (jnp.ones((c, cw), F32), zero))

    col = lambda j: (0, j)
    return pl.pallas_call(
        body, name=name, grid=(n // cw,),
        in_specs=[pl.BlockSpec((s, cw), col), pl.BlockSpec((s, cw), col), pl.BlockSpec((1, cw), col), pl.BlockSpec((1, cw), col)],
        out_specs=[pl.BlockSpec((s, cw), col), pl.BlockSpec((s, cw), col)],
        out_shape=[jax.ShapeDtypeStruct((s, n), F32), jax.ShapeDtypeStruct((s, n), F32)],
        compiler_params=_params(("parallel",)),
    )(br, bi, ar, ai)


def _scan_da(lr, li, xr, xi, *, cw=256):
    s, n = lr.shape
    c = SCAN_CHUNKS
    tt = s // c
    cw = min(cw, n)

    def body(lr_ref, li_ref, xr_ref, xi_ref, dar_ref, dai_ref):
        def step(t, carry):
            acc_r, acc_i = carry
            r = pl.ds(pl.multiple_of(t * c, c), c)
            rp = pl.ds(pl.multiple_of((t - 1) * c, c), c)
            l_r, l_i, p_r, p_i = lr_ref[r, :], li_ref[r, :], xr_ref[rp, :], xi_ref[rp, :]
            return acc_r + l_r * p_r + l_i * p_i, acc_i + l_i * p_r - l_r * p_i

        zero = jnp.zeros((c, cw), F32)
        acc_r, acc_i = lax.fori_loop(1, tt, step, (zero, zero))
        last = pl.ds((tt - 1) * c, c)
        rows_id = lax.broadcasted_iota(jnp.int32, (c, cw), 0)
        p_r = jnp.where(rows_id == 0, 0.0, pltpu.roll(xr_ref[last, :], 1, 0))
        p_i = jnp.where(rows_id == 0, 0.0, pltpu.roll(xi_ref[last, :], 1, 0))
        first = pl.ds(0, c)
        l_r, l_i = lr_ref[first, :], li_ref[first, :]
        acc_r = acc_r + l_r * p_r + l_i * p_i
        acc_i = acc_i + l_i * p_r - l_r * p_i
        dar_ref[...] = jnp.sum(acc_r, axis=0, keepdims=True)
        dai_ref[...] = jnp.sum(acc_i, axis=0, keepdims=True)

    col = lambda j: (0, j)
    return pl.pallas_call(
        body, name="s5_scan_da", grid=(n // cw,),
        in_specs=[pl.BlockSpec((s, cw), col)] * 4,
        out_specs=[pl.BlockSpec((1, cw), col)] * 2,
        out_shape=[jax.ShapeDtypeStruct((1, n), F32)] * 2,
        compiler_params=_params(("parallel",)),
    )(lr, li, xr, xi)


def _mesh_place():
    x, y, c = lax.axis_index("x"), lax.axis_index("y"), lax.axis_index("c")
    peers = []
    for k in range(1, N_DEV):
        px, py, pc = x ^ ((k >> 2) & 1), y ^ ((k >> 1) & 1), c ^ (k & 1)
        peers.append(((px, py, pc), 4 * px + 2 * py + pc))
    return 4 * x + 2 * y + c, peers


class _Exchange:
    def __init__(self, arrays, rows, *, gather, name, after=None):
        self.n_arr, self.rows, self.gather, self.name = len(arrays), rows, gather, name
        n_arr = self.n_arr
        if gather:
            assert all(r % BF16_ROWS == 0 for r in rows)
            lands = [lax.empty((N_DEV * r, a.shape[1]), a.dtype) for a, r in zip(arrays, rows)]
        else:
            lands = [lax.empty((N_DEV - 1, a.shape[0] if st is None else n, a.shape[1]), a.dtype)
                     for a, (st, n) in zip(arrays, rows)]
        has_after = after is not None

        def body(*refs):
            ins, zones = refs[:n_arr], refs[n_arr:2 * n_arr]
            sems = refs[2 * n_arr + has_after:4 * n_arr + has_after]
            token = refs[-1]
            me, peers = _mesh_place()
            for i in range(n_arr):
                for k, (pxyz, pid) in enumerate(peers):
                    if gather:
                        src = ins[i]
                        dst = zones[i].at[pl.ds(pl.multiple_of(me * rows[i], BF16_ROWS), rows[i])]
                    else:
                        stride, n = rows[i]
                        src = ins[i] if stride is None else ins[i].at[pl.ds(pl.multiple_of(pid * stride, BF16_ROWS), n)]
                        dst = zones[i].at[k]
                    pltpu.make_async_remote_copy(
                        src_ref=src, dst_ref=dst, send_sem=sems[2 * i], recv_sem=sems[2 * i + 1],
                        device_id=pxyz, device_id_type=pl.DeviceIdType.MESH).start()
            token[...] = jnp.zeros_like(token)

        hbm = pl.BlockSpec(memory_space=pltpu.HBM)
        sem = pl.BlockSpec(memory_space=pltpu.SEMAPHORE)
        args = [pltpu.with_memory_space_constraint(a, pltpu.HBM) for a in list(arrays) + lands]
        res = pl.pallas_call(
            body, name=name + "_start",
            in_specs=[hbm] * (2 * n_arr) + ([pl.BlockSpec(memory_space=pl.ANY)] if has_after else []),
            out_specs=[sem] * (2 * n_arr) + [hbm] * (2 * n_arr) + [pl.BlockSpec(memory_space=pltpu.VMEM)],
            out_shape=[pltpu.SemaphoreType.DMA(())] * (2 * n_arr) + [pltpu.HBM(a.shape, a.dtype) for a in args]
            + [jax.ShapeDtypeStruct((8, LANES), F32)],
            input_output_aliases={i: 2 * n_arr + i for i in range(2 * n_arr)},
            compiler_params=pltpu.CompilerParams(has_side_effects=pltpu.SideEffectType.DATAFLOW_SIDE_EFFECTING),
        )(*args, *([after] if has_after else []))
        self.sems, self.thru, self.token = res[:2 * n_arr], res[2 * n_arr:4 * n_arr], res[-1]

    def wait(self, after):
        n_arr = self.n_arr

        def body(*refs):
            zones, sems = refs[n_arr:2 * n_arr], refs[2 * n_arr:4 * n_arr]
            myself = (lax.axis_index("x"), lax.axis_index("y"), lax.axis_index("c"))
            for i in range(n_arr):
                seven = zones[i].at[pl.ds(0, (N_DEV - 1) * self.rows[i])] if self.gather else zones[i]
                all_seven = pltpu.make_async_remote_copy(
                    src_ref=seven, dst_ref=seven, send_sem=sems[2 * i], recv_sem=sems[2 * i + 1],
                    device_id=myself, device_id_type=pl.DeviceIdType.MESH)
                all_seven.wait_recv()
                all_seven.wait_send()

        hbm = pl.BlockSpec(memory_space=pltpu.HBM)
        sem = pl.BlockSpec(memory_space=pltpu.SEMAPHORE)
        res = pl.pallas_call(
            body, name=self.name + "_wait",
            in_specs=[hbm] * (2 * n_arr) + [sem] * (2 * n_arr) + [pl.BlockSpec(memory_space=pl.ANY)],
            out_specs=[hbm] * (2 * n_arr), out_shape=[pltpu.HBM(a.shape, a.dtype) for a in self.thru],
            input_output_aliases={i: i for i in range(2 * n_arr)},
            compiler_params=pltpu.CompilerParams(has_side_effects=pltpu.SideEffectType.DATAFLOW_SIDE_EFFECTING),
        )(*self.thru, *self.sems, after)
        return res[:n_arr], res[n_arr:]


def _my_slot():
    me = 4 * lax.axis_index("x") + 2 * lax.axis_index("y") + lax.axis_index("c")
    return me.astype(jnp.int32).reshape(1)


def _place_own(gathered, block, me, *, name):
    r, c = block.shape

    def body(me_ref, b_ref, g_ref, o_ref):
        o_ref[...] = b_ref[...]

    return pl.pallas_call(
        body, name=name, out_shape=jax.ShapeDtypeStruct(gathered.shape, gathered.dtype),
        grid_spec=pltpu.PrefetchScalarGridSpec(
            num_scalar_prefetch=1, grid=(1,),
            in_specs=[pl.BlockSpec((r, c), lambda i, me_ref: (0, 0)), pl.BlockSpec(memory_space=pl.ANY)],
            out_specs=pl.BlockSpec((r, c), lambda i, me_ref: (me_ref[0], 0))),
        input_output_aliases={2: 0}, compiler_params=_params(("arbitrary",)),
    )(me, block, gathered)


def _elementwise_tiles(r, c):
    if r % 128 == 0:
        return 128, c
    return r, (256 if c % 256 == 0 else c)


def _adamw_math(g, w, m, v):
    nm = ADAM_B1 * m + (1.0 - ADAM_B1) * g
    nv = ADAM_B2 * v + (1.0 - ADAM_B2) * (g * g)
    m_hat = nm / (1.0 - ADAM_B1 ** ADAM_STEP)
    v_hat = nv / (1.0 - ADAM_B2 ** ADAM_STEP)
    return -ADAM_LR * (m_hat / (jnp.sqrt(v_hat) + ADAM_EPS) + ADAM_WD * w), nm, nv


def _sum_parts(me_ref, own_ref, p_ref, r):
    own = own_ref[...].astype(F32)
    g = None
    for d in range(N_DEV):
        k = jnp.bitwise_xor(me_ref[0], d)
        term = jnp.where(k == 0, own, p_ref[jnp.maximum(k, 1) - 1].astype(F32))
        g = term if g is None else g + term
    return g[0:r, :]


def _sum_adamw(me, sent, stride, parts, r, w=None, m=None, v=None, *, name):
    _, own_rows, cdim = parts.shape
    assert stride is None or stride == own_rows
    tc = 256 if cdim % 256 == 0 else cdim
    update = w is not None

    def body(me_ref, own_ref, p_ref, *refs):
        g = _sum_parts(me_ref, own_ref, p_ref, r)
        if update:
            w_ref, m_ref, v_ref, g_ref, d_ref, nm_ref, nv_ref = refs
            d_ref[...], nm_ref[...], nv_ref[...] = _adamw_math(g, w_ref[...], m_ref[...], v_ref[...])
        else:
            g_ref, = refs
        g_ref[...] = g

    blk = pl.BlockSpec((r, tc), lambda j, me_ref: (0, j))
    own_spec = pl.BlockSpec((own_rows, tc), (lambda j, me_ref: (0, j)) if stride is None else (lambda j, me_ref: (me_ref[0], j)))
    n_out = 4 if update else 1
    res = pl.pallas_call(
        body, name=name, out_shape=[jax.ShapeDtypeStruct((r, cdim), F32)] * n_out,
        grid_spec=pltpu.PrefetchScalarGridSpec(
            num_scalar_prefetch=1, grid=(cdim // tc,),
            in_specs=[own_spec, pl.BlockSpec((N_DEV - 1, own_rows, tc), lambda j, me_ref: (0, 0, j))]
            + ([blk] * 3 if update else []),
            out_specs=[blk] * n_out),
        compiler_params=_params(("parallel",)),
    )(me, sent, parts, *((w, m, v) if update else ()))
    return list(res)


def _adamw(g, w, m, v, *, name):
    r, cdim = w.shape
    tr, tc = _elementwise_tiles(r, cdim)

    def body(g_ref, w_ref, m_ref, v_ref, d_ref, nm_ref, nv_ref):
        d_ref[...], nm_ref[...], nv_ref[...] = _adamw_math(g_ref[...], w_ref[...], m_ref[...], v_ref[...])

    blk = pl.BlockSpec((tr, tc), lambda i, j: (i, j))
    return list(pl.pallas_call(
        body, name=name, grid=(r // tr, cdim // tc), in_specs=[blk] * 4,
        out_specs=[blk] * 3, out_shape=[jax.ShapeDtypeStruct((r, cdim), F32)] * 3,
        compiler_params=_params(("parallel", "parallel")),
    )(g, w, m, v))


SHARD_ROWS_P = {n: (FF_SHARD_P if 'ffn' in n else IN_SHARD_P if n == 'w_in' else None) for n in SHARDED}


def _to_exchange_layout(name, shard):
    t = shard.T if SHARD_AXIS[name] == 1 else shard
    pad = SHARD_ROWS_P[name]
    return t if pad is None else jnp.pad(t, ((0, pad - t.shape[0]), (0, 0)))


def _expand_w_in(wt):
    wt = wt.reshape(N_DEV, IN_SHARD_P, D)[:, :IN_SHARD].reshape(IN_W, D)
    o = Q_RANK + KV_RANK
    kr1, kr2 = wt[o:o + ROPE // 2], wt[o + ROPE // 2:o + ROPE]
    z = jnp.zeros((LANES - ROPE, D), wt.dtype)
    return jnp.concatenate([wt[:o], wt[o + ROPE:], kr1, kr2, z, -kr2, kr1, z], axis=0)


def _expand_w_uq(wt):
    w = wt.reshape(H, QK, Q_RANK)
    z = jnp.zeros((H, LANES - ROPE, Q_RANK), w.dtype)
    q1, q2 = w[:, NOPE:NOPE + ROPE // 2], w[:, NOPE + ROPE // 2:]
    return jnp.concatenate([w[:, :NOPE].reshape(H * NOPE, Q_RANK),
                            jnp.concatenate([q1, q2, z], axis=1).reshape(H * LANES, Q_RANK),
                            jnp.concatenate([-q2, q1, z], axis=1).reshape(H * LANES, Q_RANK)], axis=0)


def _layout_qk_gain(g):
    g = g.reshape(QK)
    g1, g2, z = g[NOPE:NOPE + ROPE // 2], g[NOPE + ROPE // 2:], jnp.zeros((LANES - ROPE,), g.dtype)
    return jnp.stack([g[:NOPE], jnp.concatenate([g1, g2, z]), jnp.concatenate([g2, g1, z])])


def _rep16(a):
    return jnp.repeat(a, SSM_GRP, axis=0)


def _layout_ssm_in(a_re, a_im, log_dt, b_re, b_im):
    b_r = jnp.transpose(b_re, (0, 2, 1)).reshape(SSM_G * SSM_GRP, SSM_P)
    b_i = jnp.transpose(b_im, (0, 2, 1)).reshape(SSM_G * SSM_GRP, SSM_P)
    ldt = jnp.broadcast_to(log_dt.reshape(SSM_G, 1), (SSM_G, SSM_P))
    return _rep16(a_re), _rep16(a_im), _rep16(ldt), b_r, b_i


def _block_diag_b(bb):
    eye = jnp.eye(SSM_PACK, dtype=bb.dtype)
    b5 = bb.reshape(SSM_G // SSM_PACK, SSM_PACK, SSM_GRP, 1, SSM_P) * eye[None, :, None, :, None]
    return b5.reshape(SSM_G // SSM_PACK, SSM_PACK * SSM_GRP, SSM_PACK * SSM_P)


def _block_diag_c(cc):
    eye = jnp.eye(SSM_PACK, dtype=cc.dtype)
    c5 = jnp.transpose(cc, (0, 2, 1)).reshape(SSM_G // SSM_PACK, SSM_PACK, SSM_P, 1, SSM_GRP) * eye[None, :, None, :, None]
    return c5.reshape(SSM_G // SSM_PACK, SSM_PACK * SSM_P, SSM_PACK * SSM_GRP)


def _time_perm(a, inverse=False):
    s, w = a.shape
    c = SCAN_CHUNKS
    if inverse:
        return jnp.transpose(a.reshape(s // c, c, w), (1, 0, 2)).reshape(s, w)
    return jnp.transpose(a.reshape(c, s // c, w), (1, 0, 2)).reshape(s, w)


class _Weights:
    def __init__(self, groups=(), landed=None, me=None):
        self.groups, self.landed, self.me = list(groups), dict(landed or {}), me

    def get(self, name, after):
        if name not in self.landed:
            names, exchange = next(g for g in self.groups if name in g[0])
            for n, block, gathered in zip(names, *exchange.wait(after)):
                self.landed[n] = _place_own(gathered, block, self.me, name="place_" + n)
        return self.landed[name]

    def __getitem__(self, name):
        return self.landed[name]


def _ffn_gate_up(h, w_gt, w_ut, *, name, tm=512, tn=1408):
    s, k = h.shape
    n = w_gt.shape[0]
    tm, tn = min(tm, s), _tile(n, tn)
    dims = (((1,), (1,)), ((), ()))

    def body(h_ref, wg_ref, wu_ref, g_ref, u_ref, a_ref):
        hb = h_ref[...].astype(BF16)
        gate = lax.dot_general(hb, wg_ref[...], dims, preferred_element_type=F32)
        up = lax.dot_general(hb, wu_ref[...], dims, preferred_element_type=F32)
        g_ref[...] = gate.astype(BF16)
        u_ref[...] = up.astype(BF16)
        a_ref[...] = _f_swiglu(gate, up)

    w_spec = pl.BlockSpec((tn, k), lambda i, j: (j, 0))
    o_spec = pl.BlockSpec((tm, tn), lambda i, j: (i, j))
    return pl.pallas_call(
        body, name=name, grid=(s // tm, n // tn), in_specs=[pl.BlockSpec((tm, k), lambda i, j: (i, 0)), w_spec, w_spec],
        out_specs=[o_spec] * 3, out_shape=[jax.ShapeDtypeStruct((s, n), BF16)] * 3,
        compiler_params=_params(("parallel", "parallel")),
    )(h, w_gt, w_ut)


def _ffn_dgate_dup(dx_out, w_d, gate, up, *, name, tm=512, tn=1408, deps=()):
    s, k = dx_out.shape
    n = w_d.shape[0]
    tm, tn = min(tm, s), _tile(n, tn)
    deps = [d for d in deps if d is not None]

    def body(dx_ref, wd_ref, g_ref, u_ref, *refs):
        dg_ref, du_ref = refs[len(deps):]
        dact = 0.5 * lax.dot_general(dx_ref[...].astype(BF16), wd_ref[...], (((1,), (1,)), ((), ())),
                                     preferred_element_type=F32)
        _, vjp = jax.vjp(_f_swiglu, g_ref[...].astype(F32), u_ref[...].astype(F32))
        dgate, dup = vjp(dact.astype(BF16))
        dg_ref[...] = dgate.astype(BF16)
        du_ref[...] = dup.astype(BF16)

    o_spec = pl.BlockSpec((tm, tn), lambda i, j: (i, j))
    return pl.pallas_call(
        body, name=name, grid=(s // tm, n // tn),
        in_specs=[pl.BlockSpec((tm, k), lambda i, j: (i, 0)), pl.BlockSpec((tn, k), lambda i, j: (j, 0)), o_spec, o_spec]
        + [pl.BlockSpec(d.shape, lambda i, j: (0, 0)) for d in deps],
        out_specs=[o_spec] * 2, out_shape=[jax.ShapeDtypeStruct((s, n), BF16)] * 2,
        compiler_params=_params(("parallel", "parallel")),
    )(dx_out, w_d, gate, up, *deps)


def _ffn_fwd(x, g, wc, tag, deps=()):
    h = _rowwise(_f_norm, [x], [g], [(D, BF16)], name=tag + "_norm", deps=deps)[0]
    gate, up, act = _ffn_gate_up(h, wc.get(tag + '_w_gate', h), wc[tag + '_w_up'], name=tag + "_gate_up")
    x_out = _mm(act, wc[tag + '_w_down'], res=x, scale=0.5, name=tag + "_down")
    return x_out, (h, gate, up, act)


def _ffn_bwd(x, g, wc, saved, dx_out, tag, send):
    h, gate, up, act = saved
    w_gt, w_ut, w_d = (wc.get(tag + n, h) for n in ('_w_gate', '_w_up', '_w_down'))
    d_d = _mm(act, dx_out, ta=True, scale=0.5, out_dtype=GRAD_DTYPE, name=tag + "_dwdown")
    token = send({tag + '_w_down': d_d})
    dgate, dup = _ffn_dgate_dup(dx_out, w_d, gate, up, name=tag + "_dgate_dup", deps=[token])
    d_gt = _mm(dgate, h, ta=True, out_dtype=GRAD_DTYPE, name=tag + "_dwgate")
    d_ut = _mm(dup, h, ta=True, out_dtype=GRAD_DTYPE, name=tag + "_dwup")
    token = send({tag + '_w_gate': d_gt, tag + '_w_up': d_ut})
    dh = _mm(dgate, w_gt, name=tag + "_dh_gate")
    dh = _mm(dup, w_ut, res=dh, out_dtype=BF16, name=tag + "_dh_up")
    dx, dg = _rowwise_bwd(_f_norm, [x], [g], [dh], row_grads={0: F32}, const_grads=[0], adds={0: dx_out},
                          name=tag + "_norm_bwd", deps=[token])
    return dx, dg


def _local_step(x, mem, cos, sin, target, wc, ws, send, deps=()):
    gs = {}

    x1, sv1 = _ffn_fwd(x, ws['ffn1_norm'], wc, "ffn1", deps=deps)

    h2 = _rowwise(_f_norm, [x1], [ws['mix_norm']], [(D, BF16)], name="mix_norm")[0]
    w_in_raw, w_uq_raw = wc.get('w_in', h2), wc.get('mla_w_uq', h2)
    w_in_e = _expand_w_in(w_in_raw)
    w_uq_e = _expand_w_uq(w_uq_raw)
    proj = _mm(h2, w_in_e, tb=True, name="w_in")
    c_q, c_kv = _rowwise(_f_prep1, [proj], [ws['q_norm'], ws['kv_norm']], [(Q_RANK, BF16), (KV_RANK, BF16)], name="mla_prep1")
    qall = _mm(c_q, w_uq_e, tb=True, name="w_uq")
    kv = _mm(c_kv, wc['mla_w_ukv'], tb=True, name="w_ukv")
    kr = _rowwise(_f_kr, [proj], [], [(2 * LANES, F32)], name="mla_kr")[0]
    q, k, v = _prep2_fwd(qall, kv, kr, cos, sin, ws['qk_gq'], ws['qk_gk'])
    o_mla, lse = _attn_fwd(q, k, v)

    u = proj[:, Q_RANK + KV_RANK:Q_RANK + KV_RANK + SSM_W]
    u_p = _time_perm(u)
    disc_in = [ws['ssm_lr'], ws['ssm_li'], ws['ssm_ldt'], ws['ssm_br'], ws['ssm_bi']]
    ar16, ai16, bbr, bbi = _rowwise(_f_disc, disc_in, [], [(SSM_P, F32)] * 4, name="s5_disc")
    a_r = ar16[::SSM_GRP].reshape(1, SSM_N)
    a_i = ai16[::SSM_GRP].reshape(1, SSM_N)
    bblk_r, bblk_i = _block_diag_b(bbr).astype(BF16), _block_diag_b(bbi).astype(BF16)
    cblk_r, cblk_i = _block_diag_c(ws['ssm_cr']).astype(BF16), _block_diag_c(-ws['ssm_ci']).astype(BF16)
    bu_r = _mm_grouped(u_p, bblk_r, name="s5_bu_r")
    bu_i = _mm_grouped(u_p, bblk_i, name="s5_bu_i")
    xr, xi = _scan(bu_r, bu_i, a_r, a_i, reverse=False, name="s5_scan_fwd")
    yc = _mm_grouped(xr, cblk_r, name="s5_y_r")
    yc = _mm_grouped(xi, cblk_i, res=yc, name="s5_y_i")
    g_p = _rowwise(_f_s5_gelu, [yc, u_p], [ws['ssm_d']], [(SSM_W, F32)], name="s5_gelu")[0]
    z_p = _mm(g_p, wc['ssm_w_glu'], name="s5_glu")
    g_t, z_t = _time_perm(g_p, inverse=True), _time_perm(z_p, inverse=True)
    on_consts = [ws['ssm_b_glu'], ws['out_norm_mla'], ws['out_norm_ssm']]
    ycat = _rowwise(_f_outnorm, [o_mla, g_t, z_t], on_consts, [(D, BF16)], name="out_norm")[0]
    x2 = _mm(ycat, wc['w_o'], res=x1, name="w_o")

    hx = _rowwise(_f_norm, [x2], [ws['xattn_norm']], [(D, BF16)], name="xattn_norm")[0]
    xq = _mm(hx, wc['xattn_w_q'], name="xattn_q")
    mn = _rowwise(_f_norm, [mem], [ws['mem_norm']], [(D, BF16)], name="mem_norm")[0]
    kvm = _mm(mn, wc['xattn_w_kv'], name="xattn_kv")
    xkn, xv = _rowwise(_f_memk, [kvm], [ws['xattn_k_norm']], [(H * XH, BF16), (H * XH, BF16)], name="xattn_knorm")
    xo = _xattn_fwd(xq, xkn, xv, ws['xattn_q_norm'])
    x3 = _mm(xo, wc['xattn_w_o'], tb=True, res=x2, name="xattn_o")

    x4, sv2 = _ffn_fwd(x3, ws['ffn2_norm'], wc, "ffn2")

    def f_loss(yb, tb):
        err = yb - tb
        return err * (1.0 / D), jnp.broadcast_to(jnp.sum(jnp.sum(err * err, axis=1, keepdims=True), axis=0, keepdims=True) * (0.5 / D), (1, LANES))

    dx4, loss = _rowwise(f_loss, [x4, target], [], [(D, F32)], [(1, LANES)], name="loss")

    dx3, gs['ffn2_norm'] = _ffn_bwd(x3, ws['ffn2_norm'], wc, sv2, dx4, "ffn2", send)

    dxo = _mm(dx3, wc['xattn_w_o'], out_dtype=BF16, name="xattn_o_dx")
    send({'xattn_w_o': _mm(dx3, xo, ta=True, out_dtype=GRAD_DTYPE, name="xattn_o_dw")})
    dxq, dxkn, dxv, gs['xattn_q_norm'] = _xattn_bwd(xq, xkn, xv, ws['xattn_q_norm'], dxo)
    dkvm, gs['xattn_k_norm'] = _rowwise_bwd(_f_memk, [kvm], [ws['xattn_k_norm']], [dxkn, dxv], row_grads={0: BF16},
                                            const_grads=[0], name="xattn_knorm_bwd")
    send({'xattn_w_kv': _mm(mn, dkvm, ta=True, out_dtype=GRAD_DTYPE, name="xattn_kv_dw")})
    dmn = _mm(dkvm, wc['xattn_w_kv'], tb=True, out_dtype=BF16, name="xattn_kv_dx")
    gs['mem_norm'] = _rowwise_bwd(_f_norm, [mem], [ws['mem_norm']], [dmn], row_grads={}, const_grads=[0], name="mem_norm_bwd")[0]
    send({'xattn_w_q': _mm(hx, dxq, ta=True, out_dtype=GRAD_DTYPE, name="xattn_q_dw")})
    dhx = _mm(dxq, wc['xattn_w_q'], tb=True, out_dtype=BF16, name="xattn_q_dx")
    dx2, gs['xattn_norm'] = _rowwise_bwd(_f_norm, [x2], [ws['xattn_norm']], [dhx], row_grads={0: F32}, const_grads=[0],
                                         adds={0: dx3}, name="xattn_norm_bwd")

    dycat = _mm(dx2, wc['w_o'], tb=True, out_dtype=BF16, name="w_o_dx")
    send({'w_o': _mm(ycat, dx2, ta=True, out_dtype=GRAD_DTYPE, name="w_o_dw")})
    do_mla, dg_t, dz_t, gs['ssm_b_glu'], gs['out_norm_mla'], gs['out_norm_ssm'] = _rowwise_bwd(
        _f_outnorm, [o_mla, g_t, z_t], on_consts, [dycat], row_grads={0: F32, 1: F32, 2: BF16}, const_grads=[0, 1, 2],
        name="out_norm_bwd")

    dz_p, dg_p = _time_perm(dz_t), _time_perm(dg_t)
    send({'ssm_w_glu': _mm(g_p, dz_p, ta=True, out_dtype=GRAD_DTYPE, name="s5_glu_dw")})
    dg_p = _mm(dz_p, wc['ssm_w_glu'], tb=True, res=dg_p, name="s5_glu_dx")
    dyc, du_d, gs['ssm_d'] = _rowwise_bwd(_f_s5_gelu, [yc, u_p], [ws['ssm_d']], [dg_p], row_grads={0: BF16, 1: F32},
                                          const_grads=[0], name="s5_gelu_bwd")
    n_state, n_chan = SSM_PACK * SSM_P, SSM_PACK * SSM_GRP
    d_cblk_r = _mm_grouped_tn(xr, dyc, ka=n_state, kb=n_chan, name="s5_dc_r")
    d_cblk_i = _mm_grouped_tn(xi, dyc, ka=n_state, kb=n_chan, name="s5_dc_i")
    dxr = _mm_grouped(dyc, cblk_r, tb=True, name="s5_dx_r")
    dxi = _mm_grouped(dyc, cblk_i, tb=True, name="s5_dx_i")
    lam_r, lam_i = _scan(dxr, dxi, a_r, -a_i, reverse=True, name="s5_scan_bwd")
    d_ar, d_ai = _scan_da(lam_r, lam_i, xr, xi)
    d_bblk_r = _mm_grouped_tn(u_p, lam_r, ka=n_chan, kb=n_state, name="s5_db_r")
    d_bblk_i = _mm_grouped_tn(u_p, lam_i, ka=n_chan, kb=n_state, name="s5_db_i")
    du_p = _mm_grouped(lam_r, bblk_r, tb=True, res=du_d, name="s5_du_r")
    du_p = _mm_grouped(lam_i, bblk_i, tb=True, res=du_p, name="s5_du_i")
    du = _time_perm(du_p, inverse=True)
    gs['ssm_cr'] = jax.linear_transpose(_block_diag_c, ws['ssm_cr'])(d_cblk_r)[0]
    gs['ssm_ci'] = -jax.linear_transpose(_block_diag_c, ws['ssm_ci'])(d_cblk_i)[0]
    d_bbr = jax.linear_transpose(_block_diag_b, bbr)(d_bblk_r)[0]
    d_bbi = jax.linear_transpose(_block_diag_b, bbi)(d_bblk_i)[0]
    d_ar16 = jnp.zeros((SSM_G * SSM_GRP, SSM_P), F32).at[::SSM_GRP].set(d_ar.reshape(SSM_G, SSM_P))
    d_ai16 = jnp.zeros((SSM_G * SSM_GRP, SSM_P), F32).at[::SSM_GRP].set(d_ai.reshape(SSM_G, SSM_P))
    gs['ssm_lr'], gs['ssm_li'], gs['ssm_ldt'], gs['ssm_br'], gs['ssm_bi'] = _rowwise_bwd(
        _f_disc, disc_in, [], [d_ar16, d_ai16, d_bbr, d_bbi], row_grads={i: F32 for i in range(5)}, const_grads=[],
        name="s5_disc_bwd")

    delta, do_b = _rowwise(_f_delta, [do_mla, o_mla], [], [(H * LANES, F32), (H * VD, BF16)], name="mla_delta")
    dq, dk, dv = _attn_bwd(q, k, v, do_b, lse, delta)
    dqall, dkv, dkr, gs['qk_gq'], gs['qk_gk'] = _prep2_bwd(qall, kv, kr, cos, sin, ws['qk_gq'], ws['qk_gk'], dq, dk, dv)
    d_w_uq_e = _mm(dqall, c_q, ta=True, name="w_uq_dw")
    send({'mla_w_uq': jax.linear_transpose(_expand_w_uq, jax.ShapeDtypeStruct(w_uq_raw.shape, F32))(d_w_uq_e)[0]})
    dc_q = _mm(dqall, w_uq_e, out_dtype=BF16, name="w_uq_dx")
    send({'mla_w_ukv': _mm(dkv, c_kv, ta=True, out_dtype=GRAD_DTYPE, name="w_ukv_dw")})
    dc_kv = _mm(dkv, wc['mla_w_ukv'], out_dtype=BF16, name="w_ukv_dx")

    def f_prep1_bwd(pb, dcq, dckv, dub, dkrb, gq, gkv):
        _, vjp = jax.vjp(_f_prep1, pb[:, :Q_RANK + KV_RANK], gq, gkv)
        dpa, dgq, dgkv = vjp((dcq.astype(BF16), dckv.astype(BF16)))
        return jnp.concatenate([dpa, dub, dkrb], axis=-1), dgq, dgkv

    dproj, gs['q_norm'], gs['kv_norm'] = _rowwise(
        f_prep1_bwd, [proj, dc_q, dc_kv, du, dkr], [ws['q_norm'], ws['kv_norm']], [(IN_WP, BF16)],
        [(1, Q_RANK), (1, KV_RANK)], name="mla_prep1_bwd")
    d_w_in_e = _mm(dproj, h2, ta=True, name="w_in_dw")
    token = send({'w_in': jax.linear_transpose(_expand_w_in, jax.ShapeDtypeStruct(w_in_raw.shape, F32))(d_w_in_e)[0]})
    dh2 = _mm(dproj, w_in_e, out_dtype=BF16, name="w_in_dx")
    dx1, gs['mix_norm'] = _rowwise_bwd(_f_norm, [x1], [ws['mix_norm']], [dh2], row_grads={0: F32}, const_grads=[0],
                                       adds={0: dx2}, name="mix_norm_bwd", deps=[token])

    dx0, gs['ffn1_norm'] = _ffn_bwd(x, ws['ffn1_norm'], wc, sv1, dx1, "ffn1", send)
    return loss, dx0, gs


def _prep2_fwd(qall, kv, kr, cos, sin, gq, gk):
    return _rowwise(_f_prep2, [qall, kv, kr, cos, sin], [gq, gk], [(H * HQ, BF16), (H * HQ, BF16), (H * VD, BF16)],
                    ts=256, name="mla_prep2")


def _prep2_bwd(qall, kv, kr, cos, sin, gq, gk, dq, dk, dv):
    return _rowwise_bwd(_f_prep2, [qall, kv, kr, cos, sin], [gq, gk], [dq, dk, dv], row_grads={0: BF16, 1: BF16, 2: F32},
                        const_grads=[0, 1], ts=256, name="mla_prep2_bwd")


def _rope_tables(pos):
    half = ROPE // 2
    inv = ROPE_THETA ** (-jnp.arange(half, dtype=F32) / half)
    ang = pos.astype(F32)[:, None] * inv[None, :]
    z = jnp.zeros((pos.shape[0], LANES - ROPE), F32)
    cos, sin = jnp.cos(ang), jnp.sin(ang)
    return jnp.concatenate([cos, cos, z], axis=-1), jnp.concatenate([sin, sin, z], axis=-1)


def _small_layout(p):
    lr, li, ldt, br, bi = _layout_ssm_in(p['ssm_a_re'], p['ssm_a_im'], p['ssm_log_dt'], p['ssm_b_re'], p['ssm_b_im'])
    return {
        'ffn1_norm': p['ffn1_norm'].reshape(1, D), 'mix_norm': p['mix_norm'].reshape(1, D),
        'q_norm': p['mla_q_norm'].reshape(1, Q_RANK), 'kv_norm': p['mla_kv_norm'].reshape(1, KV_RANK),
        'qk_gq': _layout_qk_gain(p['mla_qk_norm_q']), 'qk_gk': _layout_qk_gain(p['mla_qk_norm_k']),
        'ssm_lr': lr, 'ssm_li': li, 'ssm_ldt': ldt, 'ssm_br': br, 'ssm_bi': bi,
        'ssm_cr': p['ssm_c_re'], 'ssm_ci': p['ssm_c_im'], 'ssm_d': p['ssm_d'].reshape(1, SSM_W),
        'ssm_b_glu': p['ssm_b_glu'].reshape(1, SSM_W),
        'out_norm_mla': p['out_norm_mla'].reshape(1, SSM_W), 'out_norm_ssm': p['out_norm_ssm'].reshape(1, SSM_W),
        'xattn_norm': p['xattn_norm'].reshape(1, D), 'mem_norm': p['mem_norm'].reshape(1, D),
        'xattn_q_norm': p['xattn_q_norm'].reshape(1, XH), 'xattn_k_norm': p['xattn_k_norm'].reshape(1, XH),
        'ffn2_norm': p['ffn2_norm'].reshape(1, D),
    }


def _pack(arrs, rows):
    flat = jnp.concatenate([a.reshape(-1) for a in arrs])
    return jnp.pad(flat, (0, rows * D - flat.shape[0])).reshape(rows, D)


def _unpack(flat, shapes):
    flat = flat.reshape(-1)
    out, off = [], 0
    for sh in shapes:
        n = int(np.prod(sh))
        out.append(flat[off:off + n].reshape(sh))
        off += n
    return out


def kernel(x, mem, positions, ffn1_norm, ffn1_w_gate, ffn1_w_up, ffn1_w_down, mix_norm, w_in, mla_q_norm, mla_w_uq, mla_kv_norm, mla_w_ukv, mla_qk_norm_q, mla_qk_norm_k, ssm_a_re, ssm_a_im, ssm_log_dt, ssm_b_re, ssm_b_im, ssm_c_re, ssm_c_im, ssm_d, ssm_w_glu, ssm_b_glu, out_norm_mla, out_norm_ssm, w_o, xattn_norm, mem_norm, xattn_w_q, xattn_w_kv, xattn_q_norm, xattn_k_norm, xattn_w_o, ffn2_norm, ffn2_w_gate, ffn2_w_up, ffn2_w_down, loss_target, m_ffn1_norm, m_ffn1_w_gate, m_ffn1_w_up, m_ffn1_w_down, m_mix_norm, m_w_in, m_mla_q_norm, m_mla_w_uq, m_mla_kv_norm, m_mla_w_ukv, m_mla_qk_norm_q, m_mla_qk_norm_k, m_ssm_a_re, m_ssm_a_im, m_ssm_log_dt, m_ssm_b_re, m_ssm_b_im, m_ssm_c_re, m_ssm_c_im, m_ssm_d, m_ssm_w_glu, m_ssm_b_glu, m_out_norm_mla, m_out_norm_ssm, m_w_o, m_xattn_norm, m_mem_norm, m_xattn_w_q, m_xattn_w_kv, m_xattn_q_norm, m_xattn_k_norm, m_xattn_w_o, m_ffn2_norm, m_ffn2_w_gate, m_ffn2_w_up, m_ffn2_w_down, v_ffn1_norm, v_ffn1_w_gate, v_ffn1_w_up, v_ffn1_w_down, v_mix_norm, v_w_in, v_mla_q_norm, v_mla_w_uq, v_mla_kv_norm, v_mla_w_ukv, v_mla_qk_norm_q, v_mla_qk_norm_k, v_ssm_a_re, v_ssm_a_im, v_ssm_log_dt, v_ssm_b_re, v_ssm_b_im, v_ssm_c_re, v_ssm_c_im, v_ssm_d, v_ssm_w_glu, v_ssm_b_glu, v_out_norm_mla, v_out_norm_ssm, v_w_o, v_xattn_norm, v_mem_norm, v_xattn_w_q, v_xattn_w_kv, v_xattn_q_norm, v_xattn_k_norm, v_xattn_w_o, v_ffn2_norm, v_ffn2_w_gate, v_ffn2_w_up, v_ffn2_w_down):
    args = dict(locals())
    w = {n: args[n] for n in WEIGHTS}
    mom = {n: args['m_' + n] for n in WEIGHTS}
    var = {n: args['v_' + n] for n in WEIGHTS}
    return _step(x, mem, positions, loss_target, w, mom, var)


GATHER_GROUPS = [('ffn1', ['ffn1_w_gate', 'ffn1_w_up', 'ffn1_w_down']),
                 ('mix', ['w_in', 'mla_w_uq', 'mla_w_ukv', 'ssm_w_glu', 'w_o', 'xattn_w_q', 'xattn_w_kv', 'xattn_w_o']),
                 ('ffn2', ['ffn2_w_gate', 'ffn2_w_up', 'ffn2_w_down'])]
SCATTER_GROUPS = [('ffn2', ['ffn2_w_down', 'ffn2_w_gate', 'ffn2_w_up']),
                  ('mix', ['xattn_w_o', 'xattn_w_kv', 'xattn_w_q', 'w_o', 'ssm_w_glu', 'mla_w_uq', 'mla_w_ukv', 'w_in']),
                  ('ffn1_down', ['ffn1_w_down']), ('ffn1_gu', ['ffn1_w_gate', 'ffn1_w_up'])]


def _step(x, mem, positions, loss_target, w, mom, var):
    blocks = {n: _to_exchange_layout(n, w[n][0]).astype(BF16) for n in SHARDED}
    gathers, token = [], None
    for tag, names in GATHER_GROUPS:
        ex = _Exchange([blocks[n] for n in names], [blocks[n].shape[0] for n in names], gather=True,
                       name="gather_" + tag, after=token)
        gathers.append((names, ex))
        token = ex.token
    me = _my_slot()
    wc = _Weights(gathers, me=me)

    rows = {n: (blocks[n].shape[0], blocks[n].shape[0]) for n in SHARDED}
    ready, scatters = {}, []

    def send(grads):
        ready.update({n: g.astype(GRAD_DTYPE) for n, g in grads.items()})
        for tag, names in SCATTER_GROUPS:
            if all(n in ready for n in names) and not any(t == tag for t, _, _ in scatters):
                ex = _Exchange([ready[n] for n in names], [rows[n] for n in names], gather=False, name="scatter_" + tag)
                scatters.append((tag, names, ex))
                return ex.token
        return None

    small = {n: w[n][0] for n in SMALL}
    ws = _small_layout(small)
    cos, sin = _rope_tables(positions[0])
    loss, dx, gs = _local_step(x[0], mem[0], cos, sin, loss_target[0], wc, ws, send, deps=[token])

    g_small = jax.linear_transpose(_small_layout, {n: jax.ShapeDtypeStruct(small[n].shape, F32) for n in SMALL})(gs)[0]
    small_shapes = [small[n].shape for n in SMALL]
    n_small = sum(int(np.prod(sh)) for sh in small_shapes) + 1
    rows_small = -(-n_small // (8 * D)) * 8
    small_pack = _pack([g_small[n] for n in SMALL] + [loss[0, :1]], rows_small)
    small_ex = _Exchange([small_pack], [(None, rows_small)], gather=False, name="scatter_small")

    out = {}
    for _, names, ex in scatters:
        for n, sent, p in zip(names, *ex.wait(dx)):
            r = w[n][0].shape[SHARD_AXIS[n]]
            if SHARD_AXIS[n] == 0:
                out[n] = _sum_adamw(me, sent, rows[n][0], p, r, w[n][0], mom[n][0], var[n][0], name="adamw_" + n)
            else:
                g = _sum_adamw(me, sent, rows[n][0], p, r, name="sum_" + n)[0].T
                out[n] = [g] + _adamw(g, w[n][0], mom[n][0], var[n][0], name="adamw_" + n)
    state = [_pack([t[n][0] for n in SMALL], rows_small) for t in (w, mom, var)]
    sent, p = small_ex.wait(dx)
    small_out = _sum_adamw(me, sent[0], None, p[0], rows_small, *state, name="adamw_small")
    loss_total = small_out[0].reshape(-1)[n_small - 1]
    for n, vals in zip(SMALL, zip(*[_unpack(flat, small_shapes) for flat in small_out])):
        out[n] = vals
    outs = [out[n][i][None] for i in range(4) for n in WEIGHTS]
    return (loss_total, dx[None], *outs)
```

```python
import math

import jax
import jax.numpy as jnp
import numpy as np
from jax import lax
from jax.experimental import pallas as pl
from jax.experimental.pallas import tpu as pltpu

F32 = jnp.float32
BF16 = jnp.bfloat16

N_DEV = 8
D = 1024
D_FF = 2752
D_FFP = 2816
MEM_LEN = 256
H = 4
Q_RANK, KV_RANK, NOPE, ROPE, VD = 384, 256, 128, 64, 128
QK = NOPE + ROPE
HQ = 2 * 128
SSM_W, SSM_G, SSM_GRP, SSM_P = 512, 32, 16, 64
SSM_N = SSM_G * SSM_P
SSM_PACK = 8
IN_W = 1216
IN_WP = 1408
XH = 128
EPS = 1e-6
LN2 = math.log(2.0)
ROPE_THETA = 10000.0
SCAN_CHUNKS = 8
SCAN_UNROLL = 8
ADAM_LR, ADAM_B1, ADAM_B2, ADAM_EPS, ADAM_WD, ADAM_STEP = 0.001, 0.9, 0.999, 1e-08, 0.01, 10

VMEM_LIMIT = 56 * 1024 * 1024
ACC_BYTES = 6 * 1024 * 1024
LANES = 128
BF16_ROWS = 16
GRAD_DTYPE = BF16
FF_SHARD = D_FF // N_DEV
FF_SHARD_P = 352
IN_SHARD = IN_W // N_DEV
IN_SHARD_P = 160

WEIGHTS = ['ffn1_norm', 'ffn1_w_gate', 'ffn1_w_up', 'ffn1_w_down', 'mix_norm', 'w_in', 'mla_q_norm', 'mla_w_uq',
           'mla_kv_norm', 'mla_w_ukv', 'mla_qk_norm_q', 'mla_qk_norm_k', 'ssm_a_re', 'ssm_a_im', 'ssm_log_dt',
           'ssm_b_re', 'ssm_b_im', 'ssm_c_re', 'ssm_c_im', 'ssm_d', 'ssm_w_glu', 'ssm_b_glu', 'out_norm_mla',
           'out_norm_ssm', 'w_o', 'xattn_norm', 'mem_norm', 'xattn_w_q', 'xattn_w_kv', 'xattn_q_norm',
           'xattn_k_norm', 'xattn_w_o', 'ffn2_norm', 'ffn2_w_gate', 'ffn2_w_up', 'ffn2_w_down']
SHARD_AXIS = {'ffn1_w_gate': 1, 'ffn1_w_up': 1, 'ffn1_w_down': 0, 'w_in': 1, 'mla_w_uq': 1, 'mla_w_ukv': 1,
              'ssm_w_glu': 0, 'w_o': 0, 'xattn_w_q': 0, 'xattn_w_kv': 0, 'xattn_w_o': 1,
              'ffn2_w_gate': 1, 'ffn2_w_up': 1, 'ffn2_w_down': 0}
SHARDED = [n for n in WEIGHTS if n in SHARD_AXIS]
SMALL = [n for n in WEIGHTS if n not in SHARD_AXIS]


def _params(sem=None):
    return pltpu.CompilerParams(dimension_semantics=sem, vmem_limit_bytes=VMEM_LIMIT)


def _tile(n, cap):
    if n <= cap:
        return n
    best = n
    for t in range(LANES, cap + 1, LANES):
        if n % t == 0:
            best = t
    return best


def _mm(a, b, *, ta=False, tb=False, out_dtype=F32, res=None, scale=1.0, name, tm_cap=512, tn_cap=1408, tk_cap=2816):
    m, k = (a.shape[1], a.shape[0]) if ta else a.shape
    k2, n = (b.shape[1], b.shape[0]) if tb else b.shape
    assert k == k2, (a.shape, b.shape, ta, tb)
    if ta:
        tk_cap = min(tk_cap, 512)
        tm_cap = 1408
    tm, tn, tk = _tile(m, tm_cap), _tile(n, tn_cap), _tile(k, tk_cap)
    if tm * tn * 4 > ACC_BYTES:
        tn = _tile(n, max(LANES, ACC_BYTES // (4 * tm) // LANES * LANES))
    nk = k // tk
    dims = (((0 if ta else 1,), (1 if tb else 0,)), ((), ()))
    has_res = res is not None

    def body(*refs):
        if has_res:
            a_ref, b_ref, r_ref, o_ref, acc_ref = refs
        else:
            a_ref, b_ref, o_ref, acc_ref = refs
        kk = pl.program_id(2)

        @pl.when(kk == 0)
        def _():
            acc_ref[...] = jnp.zeros_like(acc_ref)

        acc_ref[...] += lax.dot_general(a_ref[...].astype(BF16), b_ref[...].astype(BF16), dims,
                                        preferred_element_type=F32)

        @pl.when(kk == nk - 1)
        def _():
            out = acc_ref[...]
            if scale != 1.0:
                out = out * scale
            if has_res:
                out = out + r_ref[...].astype(F32)
            o_ref[...] = out.astype(o_ref.dtype)

    a_spec = pl.BlockSpec((tk, tm), lambda i, j, kk: (kk, i)) if ta else pl.BlockSpec((tm, tk), lambda i, j, kk: (i, kk))
    b_spec = pl.BlockSpec((tn, tk), lambda i, j, kk: (j, kk)) if tb else pl.BlockSpec((tk, tn), lambda i, j, kk: (kk, j))
    o_spec = pl.BlockSpec((tm, tn), lambda i, j, kk: (i, j))
    in_specs = [a_spec, b_spec] + ([o_spec] if has_res else [])
    args = (a, b) + ((res,) if has_res else ())
    return pl.pallas_call(
        body, name=name, grid=(m // tm, n // tn, nk), in_specs=in_specs, out_specs=o_spec,
        out_shape=jax.ShapeDtypeStruct((m, n), out_dtype), scratch_shapes=[pltpu.VMEM((tm, tn), F32)],
        compiler_params=_params(("parallel", "parallel", "arbitrary")),
    )(*args)


def _mm_grouped(a, b, *, tb=False, res=None, out_dtype=F32, name, tm=512):
    s = a.shape[0]
    g = b.shape[0]
    nb, ka = (b.shape[1], b.shape[2]) if tb else (b.shape[2], b.shape[1])
    assert a.shape[1] == g * ka
    tm = min(tm, s)
    dims = (((1,), (1 if tb else 0,)), ((), ()))
    has_res = res is not None

    def body(*refs):
        if has_res:
            a_ref, b_ref, r_ref, o_ref = refs
        else:
            a_ref, b_ref, o_ref = refs
        out = lax.dot_general(a_ref[...].astype(BF16), b_ref[...].astype(BF16), dims, preferred_element_type=F32)
        if has_res:
            out = out + r_ref[...].astype(F32)
        o_ref[...] = out.astype(o_ref.dtype)

    o_spec = pl.BlockSpec((tm, nb), lambda i, j: (i, j))
    in_specs = [pl.BlockSpec((tm, ka), lambda i, j: (i, j)), pl.BlockSpec((None,) + b.shape[1:], lambda i, j: (j, 0, 0))]
    return pl.pallas_call(
        body, name=name, grid=(s // tm, g), in_specs=in_specs + ([o_spec] if has_res else []), out_specs=o_spec,
        out_shape=jax.ShapeDtypeStruct((s, g * nb), out_dtype), compiler_params=_params(("parallel", "parallel")),
    )(a, b, *((res,) if has_res else ()))


def _mm_grouped_tn(a, b, *, ka, kb, name, tk=512):
    s = a.shape[0]
    g = a.shape[1] // ka
    assert b.shape[1] == g * kb
    tk = min(tk, s)
    nk = s // tk

    def body(a_ref, b_ref, o_ref):
        part = lax.dot_general(a_ref[...].astype(BF16), b_ref[...].astype(BF16), (((0,), (0,)), ((), ())),
                               preferred_element_type=F32)

        @pl.when(pl.program_id(1) == 0)
        def _():
            o_ref[...] = part

        @pl.when(pl.program_id(1) > 0)
        def _():
            o_ref[...] += part

    return pl.pallas_call(
        body, name=name, grid=(g, nk),
        in_specs=[pl.BlockSpec((tk, ka), lambda j, kk: (kk, j)), pl.BlockSpec((tk, kb), lambda j, kk: (kk, j))],
        out_specs=pl.BlockSpec((None, ka, kb), lambda j, kk: (j, 0, 0)),
        out_shape=jax.ShapeDtypeStruct((g, ka, kb), F32), compiler_params=_params(("parallel", "arbitrary")),
    )(a, b)


def _rowwise(fn, rows, consts, outs, accs=(), *, ts=512, name, deps=()):
    s = rows[0].shape[0]
    ts = min(ts, s)
    assert s % ts == 0
    n_rows, n_consts, n_outs = len(rows), len(consts), len(outs)
    deps = [d for d in deps if d is not None]
    consts = list(consts) + deps

    def body(*refs):
        ins = [r[...] for r in refs[:n_rows + n_consts]]
        res = fn(*ins)
        res = tuple(res) if isinstance(res, (tuple, list)) else (res,)
        out_refs = refs[n_rows + len(consts):]
        for o_ref, val in zip(out_refs[:n_outs], res[:n_outs]):
            o_ref[...] = val.astype(o_ref.dtype)
        if accs:
            first = pl.program_id(0) == 0

            @pl.when(first)
            def _():
                for a_ref, val in zip(out_refs[n_outs:], res[n_outs:]):
                    a_ref[...] = val.astype(F32)

            @pl.when(jnp.logical_not(first))
            def _():
                for a_ref, val in zip(out_refs[n_outs:], res[n_outs:]):
                    a_ref[...] += val.astype(F32)

    in_specs = [pl.BlockSpec((ts, r.shape[1]), lambda i: (i, 0)) for r in rows]
    in_specs += [pl.BlockSpec(c.shape, lambda i: (0, 0)) for c in consts]
    out_specs = [pl.BlockSpec((ts, w), lambda i: (i, 0)) for w, _ in outs]
    out_specs += [pl.BlockSpec(tuple(sh), lambda i: (0, 0)) for sh in accs]
    out_shape = [jax.ShapeDtypeStruct((s, w), dt) for w, dt in outs]
    out_shape += [jax.ShapeDtypeStruct(tuple(sh), F32) for sh in accs]
    res = pl.pallas_call(
        body, name=name, grid=(s // ts,), in_specs=in_specs, out_specs=out_specs, out_shape=out_shape,
        compiler_params=_params(("arbitrary",)),
    )(*rows, *consts)
    return res


def _rowwise_bwd(f, rows, consts, cts, *, row_grads, const_grads, adds=None, ts=512, name, deps=()):
    adds = adds or {}
    n_rows, n_consts, n_cts = len(rows), len(consts), len(cts)
    add_keys = sorted(adds)
    rg = sorted(row_grads)
    cg = sorted(const_grads)

    def fn(*args):
        r = args[:n_rows]
        c = args[n_rows:n_rows + n_consts]
        ct = args[n_rows + n_consts:n_rows + n_consts + n_cts]
        extra = args[n_rows + n_consts + n_cts:]
        outs, vjp = jax.vjp(f, *r, *c)
        outs = tuple(outs) if isinstance(outs, (tuple, list)) else (outs,)
        cot = tuple(g.astype(o.dtype) for g, o in zip(ct, outs))
        grads = vjp(cot if len(cot) > 1 else cot[0])
        res = []
        for i in rg:
            g = grads[i].astype(F32)
            if i in adds:
                g = g + extra[add_keys.index(i)].astype(F32)
            res.append(g)
        for i in cg:
            res.append(grads[n_rows + i])
        return tuple(res)

    rows_all = list(rows) + list(cts) + [adds[i] for i in add_keys]
    def fn2(*args):
        nr = len(rows_all)
        rr, cc = args[:nr], args[nr:]
        return fn(*rr[:n_rows], *cc, *rr[n_rows:])

    outs = [(rows[i].shape[1], row_grads[i]) for i in rg]
    accs = [consts[i].shape for i in cg]
    return _rowwise(fn2, rows_all, list(consts), outs, accs, ts=ts, name=name, deps=deps)


def _rms(x, g):
    xf = x.astype(F32)
    return xf * lax.rsqrt(jnp.mean(xf * xf, axis=-1, keepdims=True) + EPS) * g.astype(F32)


def _sigmoid(x):
    return 1.0 / (1.0 + jnp.exp(-x))


def _f_norm(x, g):
    return _rms(x, g).astype(BF16)


def _f_swiglu(gate, up):
    gate, up = gate.astype(F32), up.astype(F32)
    return (gate * _sigmoid(gate) * up).astype(BF16)


def _f_prep1(proj, gq, gkv):
    return _rms(proj[:, :Q_RANK], gq).astype(BF16), _rms(proj[:, Q_RANK:Q_RANK + KV_RANK], gkv).astype(BF16)


def _f_kr(proj):
    return (proj[:, Q_RANK + KV_RANK + SSM_W:],)


def _f_prep2(qall, kv, kr2, cos, sin, gq, gk):
    kr, krs = kr2[:, :LANES].astype(F32), kr2[:, LANES:].astype(F32)
    k_rot = kr * gk[1:2] * cos + krs * gk[2:3] * sin
    k_ss = jnp.sum(kr * kr, axis=-1, keepdims=True)
    q_scale = QK ** -0.5 / LN2
    qs, ks, vs = [], [], []
    for h in range(H):
        qn = qall[:, h * LANES:(h + 1) * LANES].astype(F32)
        qr = qall[:, (H + h) * LANES:(H + h + 1) * LANES].astype(F32)
        qrs = qall[:, (2 * H + h) * LANES:(2 * H + h + 1) * LANES].astype(F32)
        rstd = lax.rsqrt((jnp.sum(qn * qn, axis=-1, keepdims=True) + jnp.sum(qr * qr, axis=-1, keepdims=True)) / QK + EPS)
        rstd = rstd * q_scale
        qs += [qn * gq[0:1] * rstd, (qr * gq[1:2] * cos + qrs * gq[2:3] * sin) * rstd]
        kn = kv[:, 2 * h * LANES:(2 * h + 1) * LANES].astype(F32)
        rstd_k = lax.rsqrt((jnp.sum(kn * kn, axis=-1, keepdims=True) + k_ss) / QK + EPS)
        ks += [kn * gk[0:1] * rstd_k, k_rot * rstd_k]
        vs.append(kv[:, (2 * h + 1) * LANES:(2 * h + 2) * LANES])
    return (jnp.concatenate(qs, axis=-1).astype(BF16), jnp.concatenate(ks, axis=-1).astype(BF16),
            jnp.concatenate(vs, axis=-1).astype(BF16))


def _gelu(x):
    return 0.5 * x * (1.0 + jnp.tanh(math.sqrt(2.0 / math.pi) * (x + 0.044715 * (x * x * x))))


def _f_s5_gelu(yc, u, d):
    return _gelu(yc.astype(F32) + d * u.astype(F32))


def _f_outnorm(o_mla, g, z, b_glu, g_om, g_os):
    y_ssm = g * _sigmoid(z + b_glu)
    return jnp.concatenate([_rms(o_mla, g_om), _rms(y_ssm, g_os)], axis=-1).astype(BF16)


def _f_memk(kvm, gk):
    ks = [_rms(kvm[:, h * XH:(h + 1) * XH], gk) for h in range(H)]
    return jnp.concatenate(ks, axis=-1).astype(BF16), kvm[:, H * XH:].astype(BF16)


def _f_disc(lr, li, log_dt, br, bi):
    dt = jnp.exp(log_dt)
    decay = jnp.exp(lr * dt)
    ar = decay * jnp.cos(li * dt)
    ai = decay * jnp.sin(li * dt)
    den = lr * lr + li * li
    nr = ar - 1.0
    coef_r = (nr * lr + ai * li) / den
    coef_i = (ai * lr - nr * li) / den
    return ar, ai, coef_r * br - coef_i * bi, coef_r * bi + coef_i * br


def _causal_mask(i, j, tq, tk):
    qpos = i * tq + lax.broadcasted_iota(jnp.int32, (tq, tk), 0)
    kpos = j * tk + lax.broadcasted_iota(jnp.int32, (tq, tk), 1)
    return qpos >= kpos


def _attn_fwd(q, k, v, *, t=512):
    s = q.shape[0]
    t = min(t, s)
    nb = s // t

    def body(q_ref, k_ref, v_ref, o_ref, lse_ref, m_sc, l_sc, acc_sc):
        i, j = pl.program_id(1), pl.program_id(2)

        @pl.when(j == 0)
        def _():
            m_sc[...] = jnp.full_like(m_sc, -jnp.inf)
            l_sc[...] = jnp.zeros_like(l_sc)
            acc_sc[...] = jnp.zeros_like(acc_sc)

        def block(diagonal):
            sc = lax.dot_general(q_ref[...], k_ref[...], (((1,), (1,)), ((), ())), preferred_element_type=F32)
            if diagonal:
                sc = jnp.where(_causal_mask(i, j, t, t), sc, -jnp.inf)
            m_old = m_sc[...]
            m_new = jnp.maximum(m_old, jnp.max(sc, axis=-1, keepdims=True))
            p = jnp.exp2(sc - m_new)
            alpha = jnp.exp2(m_old - m_new)
            l_sc[...] = alpha * l_sc[...] + jnp.sum(p, axis=-1, keepdims=True)
            acc_sc[...] = alpha * acc_sc[...] + jnp.dot(p.astype(BF16), v_ref[...], preferred_element_type=F32)
            m_sc[...] = m_new

        pl.when(j < i)(lambda: block(False))

        @pl.when(j == i)
        def _():
            block(True)
            o_ref[...] = acc_sc[...] / l_sc[...]
            lse_ref[...] = jnp.broadcast_to(m_sc[...] + jnp.log2(l_sc[...]), lse_ref.shape)

    kv_map = lambda h, i, j: (jnp.minimum(j, i), h)
    return pl.pallas_call(
        body, name="mla_attn_fwd", grid=(H, nb, nb),
        in_specs=[pl.BlockSpec((t, HQ), lambda h, i, j: (i, h)), pl.BlockSpec((t, HQ), kv_map),
                  pl.BlockSpec((t, VD), kv_map)],
        out_specs=[pl.BlockSpec((t, VD), lambda h, i, j: (i, h)), pl.BlockSpec((t, LANES), lambda h, i, j: (i, h))],
        out_shape=[jax.ShapeDtypeStruct((s, H * VD), F32), jax.ShapeDtypeStruct((s, H * LANES), F32)],
        scratch_shapes=[pltpu.VMEM((t, 1), F32), pltpu.VMEM((t, 1), F32), pltpu.VMEM((t, VD), F32)],
        compiler_params=_params(("parallel", "parallel", "arbitrary")),
    )(q, k, v)


def _attn_probs(q_ref, k_ref, v_ref, do_ref, lse_ref, dl_ref, i, j, t, diagonal):
    sc = lax.dot_general(q_ref[...], k_ref[...], (((1,), (1,)), ((), ())), preferred_element_type=F32)
    p = jnp.exp2(sc - lse_ref[...][:, :1])
    if diagonal:
        p = jnp.where(_causal_mask(i, j, t, t), p, 0.0)
    dp = lax.dot_general(do_ref[...], v_ref[...], (((1,), (1,)), ((), ())), preferred_element_type=F32)
    ds = p * (dp - dl_ref[...][:, :1])
    return p, ds


def _attn_bwd(q, k, v, do, lse, delta, *, t=512):
    s = q.shape[0]
    t = min(t, s)
    nb = s // t

    def dq_body(q_ref, k_ref, v_ref, do_ref, lse_ref, dl_ref, dq_ref, acc_sc):
        i, j = pl.program_id(1), pl.program_id(2)

        @pl.when(j == 0)
        def _():
            acc_sc[...] = jnp.zeros_like(acc_sc)

        def block(diagonal):
            _, ds = _attn_probs(q_ref, k_ref, v_ref, do_ref, lse_ref, dl_ref, i, j, t, diagonal)
            acc_sc[...] += jnp.dot(ds.astype(BF16), k_ref[...], preferred_element_type=F32)

        pl.when(j < i)(lambda: block(False))

        @pl.when(j == i)
        def _():
            block(True)
            dq_ref[...] = acc_sc[...] * LN2

    q_map = lambda h, i, j: (i, h)
    kv_map = lambda h, i, j: (jnp.minimum(j, i), h)
    dq = pl.pallas_call(
        dq_body, name="mla_attn_dq", grid=(H, nb, nb),
        in_specs=[pl.BlockSpec((t, HQ), q_map), pl.BlockSpec((t, HQ), kv_map), pl.BlockSpec((t, VD), kv_map),
                  pl.BlockSpec((t, VD), q_map), pl.BlockSpec((t, LANES), q_map), pl.BlockSpec((t, LANES), q_map)],
        out_specs=pl.BlockSpec((t, HQ), q_map),
        out_shape=jax.ShapeDtypeStruct((s, H * HQ), F32),
        scratch_shapes=[pltpu.VMEM((t, HQ), F32)],
        compiler_params=_params(("parallel", "parallel", "arbitrary")),
    )(q, k, v, do, lse, delta)

    def dkv_body(q_ref, k_ref, v_ref, do_ref, lse_ref, dl_ref, dk_ref, dv_ref, dk_sc, dv_sc):
        j, i = pl.program_id(1), pl.program_id(2)

        @pl.when(i == 0)
        def _():
            dk_sc[...] = jnp.zeros_like(dk_sc)
            dv_sc[...] = jnp.zeros_like(dv_sc)

        def block(diagonal):
            p, ds = _attn_probs(q_ref, k_ref, v_ref, do_ref, lse_ref, dl_ref, i, j, t, diagonal)
            dv_sc[...] += lax.dot_general(p.astype(BF16), do_ref[...], (((0,), (0,)), ((), ())), preferred_element_type=F32)
            dk_sc[...] += lax.dot_general(ds.astype(BF16), q_ref[...], (((0,), (0,)), ((), ())), preferred_element_type=F32)

        pl.when(i > j)(lambda: block(False))
        pl.when(i == j)(lambda: block(True))

        @pl.when(i == nb - 1)
        def _():
            dk_ref[...] = dk_sc[...] * LN2
            dv_ref[...] = dv_sc[...]

    q_map2 = lambda h, j, i: (jnp.maximum(i, j), h)
    kv_map2 = lambda h, j, i: (j, h)
    dk, dv = pl.pallas_call(
        dkv_body, name="mla_attn_dkv", grid=(H, nb, nb),
        in_specs=[pl.BlockSpec((t, HQ), q_map2), pl.BlockSpec((t, HQ), kv_map2), pl.BlockSpec((t, VD), kv_map2),
                  pl.BlockSpec((t, VD), q_map2), pl.BlockSpec((t, LANES), q_map2), pl.BlockSpec((t, LANES), q_map2)],
        out_specs=[pl.BlockSpec((t, HQ), kv_map2), pl.BlockSpec((t, VD), kv_map2)],
        out_shape=[jax.ShapeDtypeStruct((s, H * HQ), F32), jax.ShapeDtypeStruct((s, H * VD), F32)],
        scratch_shapes=[pltpu.VMEM((t, HQ), F32), pltpu.VMEM((t, VD), F32)],
        compiler_params=_params(("parallel", "parallel", "arbitrary")),
    )(q, k, v, do, lse, delta)
    return dq, dk, dv


def _f_delta(do, o):
    prod = do.astype(F32) * o.astype(F32)
    parts = [jnp.broadcast_to(jnp.sum(prod[:, h * VD:(h + 1) * VD], axis=-1, keepdims=True), (do.shape[0], LANES))
             for h in range(H)]
    return jnp.concatenate(parts, axis=-1), do.astype(BF16)


def _xattn_head(qh, kh, gq):
    qn = _rms(qh, gq) * (XH ** -0.5)
    sc = lax.dot_general(qn.astype(BF16), kh, (((1,), (1,)), ((), ())), preferred_element_type=F32)
    sc = sc - jnp.max(sc, axis=-1, keepdims=True)
    e = jnp.exp(sc)
    return qn, e / jnp.sum(e, axis=-1, keepdims=True)


def _xattn_fwd(q, kn, v, gq, *, ts=512):
    def fn(qb, knb, vb, g):
        outs = []
        for h in range(H):
            sl = slice(h * XH, (h + 1) * XH)
            _, p = _xattn_head(qb[:, sl], knb[:, sl], g)
            outs.append(jnp.dot(p.astype(BF16), vb[:, sl], preferred_element_type=F32))
        return (jnp.concatenate(outs, axis=-1),)

    return _rowwise(fn, [q], [kn, v, gq], [(H * XH, BF16)], ts=ts, name="xattn_fwd")[0]


def _xattn_bwd(q, kn, v, gq, do, *, ts=512):
    def fn(qb, dob, knb, vb, g):
        dqs, dks, dvs = [], [], []
        dg = jnp.zeros((1, XH), F32)
        for h in range(H):
            sl = slice(h * XH, (h + 1) * XH)
            qh, kh, vh, doh = qb[:, sl], knb[:, sl], vb[:, sl], dob[:, sl].astype(BF16)
            qn, p = _xattn_head(qh, kh, g)
            dp = lax.dot_general(doh, vh, (((1,), (1,)), ((), ())), preferred_element_type=F32)
            dvs.append(lax.dot_general(p.astype(BF16), doh, (((0,), (0,)), ((), ())), preferred_element_type=F32))
            ds = (p * (dp - jnp.sum(dp * p, axis=-1, keepdims=True))).astype(BF16)
            dqn = jnp.dot(ds, kh, preferred_element_type=F32)
            dks.append(lax.dot_general(ds, qn.astype(BF16), (((0,), (0,)), ((), ())), preferred_element_type=F32))
            _, vjp_n = jax.vjp(lambda a, b: _rms(a, b) * (XH ** -0.5), qh, g)
            dqh, dgh = vjp_n(dqn)
            dqs.append(dqh)
            dg = dg + dgh
        return (jnp.concatenate(dqs, axis=-1), jnp.concatenate(dks, axis=-1), jnp.concatenate(dvs, axis=-1), dg)

    return _rowwise(fn, [q, do], [kn, v, gq], [(H * XH, BF16)], [kn.shape, v.shape, gq.shape], ts=ts, name="xattn_bwd")


def _cmul(ar, ai, xr, xi):
    return ar * xr - ai * xi, ar * xi + ai * xr


def _scan_in_place(xr_ref, xi_ref, ar, ai, *, reverse):
    s, cw = xr_ref.shape
    c = SCAN_CHUNKS
    tt = s // c
    a_r = jnp.broadcast_to(ar, (c, cw))
    a_i = jnp.broadcast_to(ai, (c, cw))
    zero = jnp.zeros((c, cw), F32)

    def row(step):
        t = (tt - 1 - step) if reverse else step
        return pl.ds(pl.multiple_of(t * c, c), c)

    def local(step, carry):
        sr, si, qr, qi = carry
        r = row(step)
        nr, ni = _cmul(a_r, a_i, sr, si)
        nr, ni = nr + xr_ref[r, :], ni + xi_ref[r, :]
        xr_ref[r, :] = nr
        xi_ref[r, :] = ni
        return (nr, ni) + _cmul(a_r, a_i, qr, qi)

    end_r, end_i, pr, pi = lax.fori_loop(0, tt, local, (zero, zero, jnp.ones((c, cw), F32), zero), unroll=SCAN_UNROLL)

    rows_id = lax.broadcasted_iota(jnp.int32, (c, cw), 0)
    car_r, car_i = zero, zero
    cur_r, cur_i = jnp.zeros((1, cw), F32), jnp.zeros((1, cw), F32)
    order = range(c - 1, -1, -1) if reverse else range(c)
    for kk in order:
        car_r = jnp.where(rows_id == kk, cur_r, car_r)
        car_i = jnp.where(rows_id == kk, cur_i, car_i)
        nr, ni = _cmul(pr[0:1], pi[0:1], cur_r, cur_i)
        cur_r = nr + end_r[kk:kk + 1]
        cur_i = ni + end_i[kk:kk + 1]

    def fix(step, carry):
        qr, qi = _cmul(a_r, a_i, *carry)
        r = row(step)
        dr, di = _cmul(qr, qi, car_r, car_i)
        xr_ref[r, :] += dr
        xi_ref[r, :] += di
        return qr, qi

    lax.fori_loop(0, tt, fix, (jnp.ones((c, cw), F32), zero), unroll=SCAN_UNROLL)


S5_ROWS = 512


def _s5_scan(v, w_r, w_i, ar, ai, *, reverse, tb, readout=None, name):
    s = v.shape[0]
    g = w_r.shape[0]
    nv, ns = SSM_PACK * SSM_GRP, SSM_PACK * SSM_P
    rows = min(S5_ROWS, s)
    dims = (((1,), (1 if tb else 0,)), ((), ()))
    n_w = 2 if readout is None else 4

    def body(v_ref, ar_ref, ai_ref, *refs):
        w = [r[...] for r in refs[:n_w]]
        xr_ref, xi_ref = refs[n_w:n_w + 2]
        for r0 in range(0, s, rows):
            vb = v_ref[r0:r0 + rows, :].astype(BF16)
            xr_ref[r0:r0 + rows, :] = lax.dot_general(vb, w[0], dims, preferred_element_type=F32)
            xi_ref[r0:r0 + rows, :] = lax.dot_general(vb, w[1], dims, preferred_element_type=F32)
        _scan_in_place(xr_ref, xi_ref, ar_ref[...], ai_ref[...], reverse=reverse)
        if readout is not None:
            y_ref = refs[n_w + 2]
            for r0 in range(0, s, rows):
                y_ref[r0:r0 + rows, :] = (
                    jnp.dot(xr_ref[r0:r0 + rows, :].astype(BF16), w[2], preferred_element_type=F32)
                    + jnp.dot(xi_ref[r0:r0 + rows, :].astype(BF16), w[3], preferred_element_type=F32))

    col = lambda j: (0, j)
    w_spec = lambda a: pl.BlockSpec((None,) + a.shape[1:], lambda j: (j, 0, 0))
    weights = [w_r, w_i] + (list(readout) if readout is not None else [])
    out_specs = [pl.BlockSpec((s, ns), col)] * 2 + ([pl.BlockSpec((s, nv), col)] if readout is not None else [])
    out_shape = [jax.ShapeDtypeStruct((s, g * ns), F32)] * 2 + (
        [jax.ShapeDtypeStruct((s, g * nv), F32)] if readout is not None else [])
    return pl.pallas_call(
        body, name=name, grid=(g,),
        in_specs=[pl.BlockSpec((s, nv), col), pl.BlockSpec((1, ns), col), pl.BlockSpec((1, ns), col)] + [w_spec(a) for a in weights],
        out_specs=out_specs, out_shape=out_shape, compiler_params=_params(("parallel",)),
    )(v, ar, ai, *weights)


def _s5_grads(lam_r, lam_i, xr, xi, u, dyc, du_d, b_r, b_i):
    s = u.shape[0]
    g = b_r.shape[0]
    nv, ns, c = SSM_PACK * SSM_GRP, SSM_PACK * SSM_P, SCAN_CHUNKS
    rows = min(S5_ROWS, s)
    slabs = rows // c
    last_slab = s // c - 1
    nt = (((1,), (1,)), ((), ()))
    tn = (((0,), (0,)), ((), ()))

    def body(lr_ref, li_ref, xr_ref, xi_ref, pr_ref, pi_ref, u_ref, dy_ref, dud_ref, br_ref, bi_ref,
             du_ref, dbr_ref, dbi_ref, dcr_ref, dci_ref, dar_ref, dai_ref):
        first = pl.program_id(1) == 0
        l_r, l_i, x_r, x_i = lr_ref[...], li_ref[...], xr_ref[...], xi_ref[...]
        lrb, lib = l_r.astype(BF16), l_i.astype(BF16)
        du_ref[...] = (dud_ref[...] + lax.dot_general(lrb, br_ref[...], nt, preferred_element_type=F32)
                       + lax.dot_general(lib, bi_ref[...], nt, preferred_element_type=F32))
        ub, dyb = u_ref[...].astype(BF16), dy_ref[...].astype(BF16)
        rows_id = lax.broadcasted_iota(jnp.int32, (c, ns), 0)

        def before(p_ref, x):
            p = p_ref[...]
            p = jnp.where(first, jnp.where(rows_id == 0, 0.0, pltpu.roll(p, 1, 0)), p)
            return jnp.concatenate([p, x[:rows - c]], axis=0)

        xp_r, xp_i = before(pr_ref, x_r), before(pi_ref, x_i)
        parts = (lax.dot_general(ub, lrb, tn, preferred_element_type=F32),
                 lax.dot_general(ub, lib, tn, preferred_element_type=F32),
                 lax.dot_general(x_r.astype(BF16), dyb, tn, preferred_element_type=F32),
                 lax.dot_general(x_i.astype(BF16), dyb, tn, preferred_element_type=F32),
                 jnp.sum(l_r * xp_r + l_i * xp_i, axis=0, keepdims=True),
                 jnp.sum(l_i * xp_r - l_r * xp_i, axis=0, keepdims=True))
        accs = (dbr_ref, dbi_ref, dcr_ref, dci_ref, dar_ref, dai_ref)

        @pl.when(first)
        def _():
            for a_ref, val in zip(accs, parts):
                a_ref[...] = val

        @pl.when(jnp.logical_not(first))
        def _():
            for a_ref, val in zip(accs, parts):
                a_ref[...] += val

    state = pl.BlockSpec((rows, ns), lambda j, k: (k, j))
    chan = pl.BlockSpec((rows, nv), lambda j, k: (k, j))
    slab = pl.BlockSpec((c, ns), lambda j, k: (jnp.where(k == 0, last_slab, k * slabs - 1), j))
    per_b = pl.BlockSpec((None, nv, ns), lambda j, k: (j, 0, 0))
    per_c = pl.BlockSpec((None, ns, nv), lambda j, k: (j, 0, 0))
    per_a = pl.BlockSpec((1, ns), lambda j, k: (0, j))
    return pl.pallas_call(
        body, name="s5_grads", grid=(g, s // rows),
        in_specs=[state, state, state, state, slab, slab, chan, chan, chan, per_b, per_b],
        out_specs=[chan, per_b, per_b, per_c, per_c, per_a, per_a],
        out_shape=[jax.ShapeDtypeStruct((s, g * nv), F32), jax.ShapeDtypeStruct((g, nv, ns), F32),
                   jax.ShapeDtypeStruct((g, nv, ns), F32), jax.ShapeDtypeStruct((g, ns, nv), F32),
                   jax.ShapeDtypeStruct((g, ns, nv), F32), jax.ShapeDtypeStruct((1, g * ns), F32),
                   jax.ShapeDtypeStruct((1, g * ns), F32)],
        compiler_params=_params(("parallel", "arbitrary")),
    )(lam_r, lam_i, xr, xi, xr, xi, u, dyc, du_d, b_r, b_i)


def _mesh_place():
    x, y, c = lax.axis_index("x"), lax.axis_index("y"), lax.axis_index("c")
    peers = []
    for k in range(1, N_DEV):
        px, py, pc = x ^ ((k >> 2) & 1), y ^ ((k >> 1) & 1), c ^ (k & 1)
        peers.append(((px, py, pc), 4 * px + 2 * py + pc))
    return 4 * x + 2 * y + c, peers


class _Exchange:
    def __init__(self, arrays, rows, *, gather, name, after=None):
        self.n_arr, self.rows, self.gather, self.name = len(arrays), rows, gather, name
        n_arr = self.n_arr
        if gather:
            assert all(r % BF16_ROWS == 0 for r in rows)
            lands = [lax.empty((N_DEV * r, a.shape[1]), a.dtype) for a, r in zip(arrays, rows)]
        else:
            lands = [lax.empty((N_DEV - 1, a.shape[0] if st is None else n, a.shape[1]), a.dtype)
                     for a, (st, n) in zip(arrays, rows)]
        has_after = after is not None

        def body(*refs):
            ins, zones = refs[:n_arr], refs[n_arr:2 * n_arr]
            sems = refs[2 * n_arr + has_after:4 * n_arr + has_after]
            token = refs[-1]
            me, peers = _mesh_place()
            for i in range(n_arr):
                for k, (pxyz, pid) in enumerate(peers):
                    if gather:
                        src = ins[i]
                        dst = zones[i].at[pl.ds(pl.multiple_of(me * rows[i], BF16_ROWS), rows[i])]
                    else:
                        stride, n = rows[i]
                        src = ins[i] if stride is None else ins[i].at[pl.ds(pl.multiple_of(pid * stride, BF16_ROWS), n)]
                        dst = zones[i].at[k]
                    pltpu.make_async_remote_copy(
                        src_ref=src, dst_ref=dst, send_sem=sems[2 * i], recv_sem=sems[2 * i + 1],
                        device_id=pxyz, device_id_type=pl.DeviceIdType.MESH).start()
            token[...] = jnp.zeros_like(token)

        hbm = pl.BlockSpec(memory_space=pltpu.HBM)
        sem = pl.BlockSpec(memory_space=pltpu.SEMAPHORE)
        args = [pltpu.with_memory_space_constraint(a, pltpu.HBM) for a in list(arrays) + lands]
        res = pl.pallas_call(
            body, name=name + "_start",
            in_specs=[hbm] * (2 * n_arr) + ([pl.BlockSpec(memory_space=pl.ANY)] if has_after else []),
            out_specs=[sem] * (2 * n_arr) + [hbm] * (2 * n_arr) + [pl.BlockSpec(memory_space=pltpu.VMEM)],
            out_shape=[pltpu.SemaphoreType.DMA(())] * (2 * n_arr) + [pltpu.HBM(a.shape, a.dtype) for a in args]
            + [jax.ShapeDtypeStruct((8, LANES), F32)],
            input_output_aliases={i: 2 * n_arr + i for i in range(2 * n_arr)},
            compiler_params=pltpu.CompilerParams(has_side_effects=pltpu.SideEffectType.DATAFLOW_SIDE_EFFECTING),
        )(*args, *([after] if has_after else []))
        self.sems, self.thru, self.token = res[:2 * n_arr], res[2 * n_arr:4 * n_arr], res[-1]

    def wait(self, after):
        n_arr = self.n_arr

        def body(*refs):
            zones, sems = refs[n_arr:2 * n_arr], refs[2 * n_arr:4 * n_arr]
            myself = (lax.axis_index("x"), lax.axis_index("y"), lax.axis_index("c"))
            for i in range(n_arr):
                seven = zones[i].at[pl.ds(0, (N_DEV - 1) * self.rows[i])] if self.gather else zones[i]
                all_seven = pltpu.make_async_remote_copy(
                    src_ref=seven, dst_ref=seven, send_sem=sems[2 * i], recv_sem=sems[2 * i + 1],
                    device_id=myself, device_id_type=pl.DeviceIdType.MESH)
                all_seven.wait_recv()
                all_seven.wait_send()

        hbm = pl.BlockSpec(memory_space=pltpu.HBM)
        sem = pl.BlockSpec(memory_space=pltpu.SEMAPHORE)
        res = pl.pallas_call(
            body, name=self.name + "_wait",
            in_specs=[hbm] * (2 * n_arr) + [sem] * (2 * n_arr) + [pl.BlockSpec(memory_space=pl.ANY)],
            out_specs=[hbm] * (2 * n_arr), out_shape=[pltpu.HBM(a.shape, a.dtype) for a in self.thru],
            input_output_aliases={i: i for i in range(2 * n_arr)},
            compiler_params=pltpu.CompilerParams(has_side_effects=pltpu.SideEffectType.DATAFLOW_SIDE_EFFECTING),
        )(*self.thru, *self.sems, after)
        return res[:n_arr], res[n_arr:]


def _my_slot():
    me = 4 * lax.axis_index("x") + 2 * lax.axis_index("y") + lax.axis_index("c")
    return me.astype(jnp.int32).reshape(1)


def _place_own(gathered, block, me, *, name):
    r, c = block.shape

    def body(me_ref, b_ref, g_ref, o_ref):
        o_ref[...] = b_ref[...]

    return pl.pallas_call(
        body, name=name, out_shape=jax.ShapeDtypeStruct(gathered.shape, gathered.dtype),
        grid_spec=pltpu.PrefetchScalarGridSpec(
            num_scalar_prefetch=1, grid=(1,),
            in_specs=[pl.BlockSpec((r, c), lambda i, me_ref: (0, 0)), pl.BlockSpec(memory_space=pl.ANY)],
            out_specs=pl.BlockSpec((r, c), lambda i, me_ref: (me_ref[0], 0))),
        input_output_aliases={2: 0}, compiler_params=_params(("arbitrary",)),
    )(me, block, gathered)


def _elementwise_tiles(r, c):
    if r % 128 == 0:
        return 128, c
    return r, (256 if c % 256 == 0 else c)


def _adamw_math(g, w, m, v):
    nm = ADAM_B1 * m + (1.0 - ADAM_B1) * g
    nv = ADAM_B2 * v + (1.0 - ADAM_B2) * (g * g)
    m_hat = nm / (1.0 - ADAM_B1 ** ADAM_STEP)
    v_hat = nv / (1.0 - ADAM_B2 ** ADAM_STEP)
    return -ADAM_LR * (m_hat / (jnp.sqrt(v_hat) + ADAM_EPS) + ADAM_WD * w), nm, nv


def _sum_parts(me_ref, own_ref, p_ref, r):
    own = own_ref[...].astype(F32)
    g = None
    for d in range(N_DEV):
        k = jnp.bitwise_xor(me_ref[0], d)
        term = jnp.where(k == 0, own, p_ref[jnp.maximum(k, 1) - 1].astype(F32))
        g = term if g is None else g + term
    return g[0:r, :]


def _sum_adamw(me, sent, stride, parts, r, w=None, m=None, v=None, *, name):
    _, own_rows, cdim = parts.shape
    assert stride is None or stride == own_rows
    tc = 256 if cdim % 256 == 0 else cdim
    update = w is not None

    def body(me_ref, own_ref, p_ref, *refs):
        g = _sum_parts(me_ref, own_ref, p_ref, r)
        if update:
            w_ref, m_ref, v_ref, g_ref, d_ref, nm_ref, nv_ref = refs
            d_ref[...], nm_ref[...], nv_ref[...] = _adamw_math(g, w_ref[...], m_ref[...], v_ref[...])
        else:
            g_ref, = refs
        g_ref[...] = g

    blk = pl.BlockSpec((r, tc), lambda j, me_ref: (0, j))
    own_spec = pl.BlockSpec((own_rows, tc), (lambda j, me_ref: (0, j)) if stride is None else (lambda j, me_ref: (me_ref[0], j)))
    n_out = 4 if update else 1
    res = pl.pallas_call(
        body, name=name, out_shape=[jax.ShapeDtypeStruct((r, cdim), F32)] * n_out,
        grid_spec=pltpu.PrefetchScalarGridSpec(
            num_scalar_prefetch=1, grid=(cdim // tc,),
            in_specs=[own_spec, pl.BlockSpec((N_DEV - 1, own_rows, tc), lambda j, me_ref: (0, 0, j))]
            + ([blk] * 3 if update else []),
            out_specs=[blk] * n_out),
        compiler_params=_params(("parallel",)),
    )(me, sent, parts, *((w, m, v) if update else ()))
    return list(res)


def _adamw(g, w, m, v, *, name):
    r, cdim = w.shape
    tr, tc = _elementwise_tiles(r, cdim)

    def body(g_ref, w_ref, m_ref, v_ref, d_ref, nm_ref, nv_ref):
        d_ref[...], nm_ref[...], nv_ref[...] = _adamw_math(g_ref[...], w_ref[...], m_ref[...], v_ref[...])

    blk = pl.BlockSpec((tr, tc), lambda i, j: (i, j))
    return list(pl.pallas_call(
        body, name=name, grid=(r // tr, cdim // tc), in_specs=[blk] * 4,
        out_specs=[blk] * 3, out_shape=[jax.ShapeDtypeStruct((r, cdim), F32)] * 3,
        compiler_params=_params(("parallel", "parallel")),
    )(g, w, m, v))


SHARD_ROWS_P = {n: (FF_SHARD_P if 'ffn' in n else IN_SHARD_P if n == 'w_in' else None) for n in SHARDED}


def _to_exchange_layout(name, shard):
    t = shard.T if SHARD_AXIS[name] == 1 else shard
    pad = SHARD_ROWS_P[name]
    return t if pad is None else jnp.pad(t, ((0, pad - t.shape[0]), (0, 0)))


def _expand_w_in(wt):
    wt = wt.reshape(N_DEV, IN_SHARD_P, D)[:, :IN_SHARD].reshape(IN_W, D)
    o = Q_RANK + KV_RANK
    kr1, kr2 = wt[o:o + ROPE // 2], wt[o + ROPE // 2:o + ROPE]
    z = jnp.zeros((LANES - ROPE, D), wt.dtype)
    return jnp.concatenate([wt[:o], wt[o + ROPE:], kr1, kr2, z, -kr2, kr1, z], axis=0)


def _expand_w_uq(wt):
    w = wt.reshape(H, QK, Q_RANK)
    z = jnp.zeros((H, LANES - ROPE, Q_RANK), w.dtype)
    q1, q2 = w[:, NOPE:NOPE + ROPE // 2], w[:, NOPE + ROPE // 2:]
    return jnp.concatenate([w[:, :NOPE].reshape(H * NOPE, Q_RANK),
                            jnp.concatenate([q1, q2, z], axis=1).reshape(H * LANES, Q_RANK),
                            jnp.concatenate([-q2, q1, z], axis=1).reshape(H * LANES, Q_RANK)], axis=0)


def _layout_qk_gain(g):
    g = g.reshape(QK)
    g1, g2, z = g[NOPE:NOPE + ROPE // 2], g[NOPE + ROPE // 2:], jnp.zeros((LANES - ROPE,), g.dtype)
    return jnp.stack([g[:NOPE], jnp.concatenate([g1, g2, z]), jnp.concatenate([g2, g1, z])])


def _rep16(a):
    return jnp.repeat(a, SSM_GRP, axis=0)


def _layout_ssm_in(a_re, a_im, log_dt, b_re, b_im):
    b_r = jnp.transpose(b_re, (0, 2, 1)).reshape(SSM_G * SSM_GRP, SSM_P)
    b_i = jnp.transpose(b_im, (0, 2, 1)).reshape(SSM_G * SSM_GRP, SSM_P)
    ldt = jnp.broadcast_to(log_dt.reshape(SSM_G, 1), (SSM_G, SSM_P))
    return _rep16(a_re), _rep16(a_im), _rep16(ldt), b_r, b_i


def _block_diag_b(bb):
    eye = jnp.eye(SSM_PACK, dtype=bb.dtype)
    b5 = bb.reshape(SSM_G // SSM_PACK, SSM_PACK, SSM_GRP, 1, SSM_P) * eye[None, :, None, :, None]
    return b5.reshape(SSM_G // SSM_PACK, SSM_PACK * SSM_GRP, SSM_PACK * SSM_P)


def _block_diag_c(cc):
    eye = jnp.eye(SSM_PACK, dtype=cc.dtype)
    c5 = jnp.transpose(cc, (0, 2, 1)).reshape(SSM_G // SSM_PACK, SSM_PACK, SSM_P, 1, SSM_GRP) * eye[None, :, None, :, None]
    return c5.reshape(SSM_G // SSM_PACK, SSM_PACK * SSM_P, SSM_PACK * SSM_GRP)


def _time_perm(a, inverse=False):
    s, w = a.shape
    c = SCAN_CHUNKS
    if inverse:
        return jnp.transpose(a.reshape(s // c, c, w), (1, 0, 2)).reshape(s, w)
    return jnp.transpose(a.reshape(c, s // c, w), (1, 0, 2)).reshape(s, w)


class _Weights:
    def __init__(self, groups=(), landed=None, me=None):
        self.groups, self.landed, self.me = list(groups), dict(landed or {}), me

    def get(self, name, after):
        if name not in self.landed:
            names, exchange = next(g for g in self.groups if name in g[0])
            for n, block, gathered in zip(names, *exchange.wait(after)):
                self.landed[n] = _place_own(gathered, block, self.me, name="place_" + n)
        return self.landed[name]

    def __getitem__(self, name):
        return self.landed[name]


def _ffn_gate_up(h, w_gt, w_ut, *, name, tm=512, tn=1408):
    s, k = h.shape
    n = w_gt.shape[0]
    tm, tn = min(tm, s), _tile(n, tn)
    dims = (((1,), (1,)), ((), ()))

    def body(h_ref, wg_ref, wu_ref, g_ref, u_ref, a_ref):
        hb = h_ref[...].astype(BF16)
        gate = lax.dot_general(hb, wg_ref[...], dims, preferred_element_type=F32)
        up = lax.dot_general(hb, wu_ref[...], dims, preferred_element_type=F32)
        g_ref[...] = gate.astype(BF16)
        u_ref[...] = up.astype(BF16)
        a_ref[...] = _f_swiglu(gate, up)

    w_spec = pl.BlockSpec((tn, k), lambda i, j: (j, 0))
    o_spec = pl.BlockSpec((tm, tn), lambda i, j: (i, j))
    return pl.pallas_call(
        body, name=name, grid=(s // tm, n // tn), in_specs=[pl.BlockSpec((tm, k), lambda i, j: (i, 0)), w_spec, w_spec],
        out_specs=[o_spec] * 3, out_shape=[jax.ShapeDtypeStruct((s, n), BF16)] * 3,
        compiler_params=_params(("parallel", "parallel")),
    )(h, w_gt, w_ut)


def _ffn_dgate_dup(dx_out, w_d, gate, up, *, name, tm=512, tn=1408, deps=()):
    s, k = dx_out.shape
    n = w_d.shape[0]
    tm, tn = min(tm, s), _tile(n, tn)
    deps = [d for d in deps if d is not None]

    def body(dx_ref, wd_ref, g_ref, u_ref, *refs):
        dg_ref, du_ref = refs[len(deps):]
        dact = 0.5 * lax.dot_general(dx_ref[...].astype(BF16), wd_ref[...], (((1,), (1,)), ((), ())),
                                     preferred_element_type=F32)
        _, vjp = jax.vjp(_f_swiglu, g_ref[...].astype(F32), u_ref[...].astype(F32))
        dgate, dup = vjp(dact.astype(BF16))
        dg_ref[...] = dgate.astype(BF16)
        du_ref[...] = dup.astype(BF16)

    o_spec = pl.BlockSpec((tm, tn), lambda i, j: (i, j))
    return pl.pallas_call(
        body, name=name, grid=(s // tm, n // tn),
        in_specs=[pl.BlockSpec((tm, k), lambda i, j: (i, 0)), pl.BlockSpec((tn, k), lambda i, j: (j, 0)), o_spec, o_spec]
        + [pl.BlockSpec(d.shape, lambda i, j: (0, 0)) for d in deps],
        out_specs=[o_spec] * 2, out_shape=[jax.ShapeDtypeStruct((s, n), BF16)] * 2,
        compiler_params=_params(("parallel", "parallel")),
    )(dx_out, w_d, gate, up, *deps)


def _ffn_fwd(x, g, wc, tag, deps=()):
    h = _rowwise(_f_norm, [x], [g], [(D, BF16)], name=tag + "_norm", deps=deps)[0]
    gate, up, act = _ffn_gate_up(h, wc.get(tag + '_w_gate', h), wc[tag + '_w_up'], name=tag + "_gate_up")
    x_out = _mm(act, wc.get(tag + '_w_down', act), res=x, scale=0.5, name=tag + "_down")
    return x_out, (h, gate, up, act)


def _ffn_bwd(x, g, wc, saved, dx_out, tag, send):
    h, gate, up, act = saved
    w_gt, w_ut, w_d = (wc.get(tag + n, h) for n in ('_w_gate', '_w_up', '_w_down'))
    d_d = _mm(act, dx_out, ta=True, scale=0.5, out_dtype=GRAD_DTYPE, name=tag + "_dwdown")
    token = send({tag + '_w_down': d_d})
    dgate, dup = _ffn_dgate_dup(dx_out, w_d, gate, up, name=tag + "_dgate_dup", deps=[token])
    d_gt = _mm(dgate, h, ta=True, out_dtype=GRAD_DTYPE, name=tag + "_dwgate")
    d_ut = _mm(dup, h, ta=True, out_dtype=GRAD_DTYPE, name=tag + "_dwup")
    token = send({tag + '_w_gate': d_gt, tag + '_w_up': d_ut})
    dh = _mm(dgate, w_gt, name=tag + "_dh_gate")
    dh = _mm(dup, w_ut, res=dh, out_dtype=BF16, name=tag + "_dh_up")
    dx, dg = _rowwise_bwd(_f_norm, [x], [g], [dh], row_grads={0: F32}, const_grads=[0], adds={0: dx_out},
                          name=tag + "_norm_bwd", deps=[token])
    return dx, dg


def _local_step(x, mem, cos, sin, target, wc, ws, send, deps=()):
    gs = {}

    x1, sv1 = _ffn_fwd(x, ws['ffn1_norm'], wc, "ffn1", deps=deps)

    h2 = _rowwise(_f_norm, [x1], [ws['mix_norm']], [(D, BF16)], name="mix_norm")[0]
    w_in_raw, w_uq_raw = wc.get('w_in', h2), wc.get('mla_w_uq', h2)
    w_in_e = _expand_w_in(w_in_raw)
    w_uq_e = _expand_w_uq(w_uq_raw)
    proj = _mm(h2, w_in_e, tb=True, name="w_in")
    c_q, c_kv = _rowwise(_f_prep1, [proj], [ws['q_norm'], ws['kv_norm']], [(Q_RANK, BF16), (KV_RANK, BF16)], name="mla_prep1")
    qall = _mm(c_q, w_uq_e, tb=True, name="w_uq")
    kv = _mm(c_kv, wc['mla_w_ukv'], tb=True, name="w_ukv")
    kr = _rowwise(_f_kr, [proj], [], [(2 * LANES, F32)], name="mla_kr")[0]
    q, k, v = _prep2_fwd(qall, kv, kr, cos, sin, ws['qk_gq'], ws['qk_gk'])
    o_mla, lse = _attn_fwd(q, k, v)

    u = proj[:, Q_RANK + KV_RANK:Q_RANK + KV_RANK + SSM_W]
    u_p = _time_perm(u)
    disc_in = [ws['ssm_lr'], ws['ssm_li'], ws['ssm_ldt'], ws['ssm_br'], ws['ssm_bi']]
    ar16, ai16, bbr, bbi = _rowwise(_f_disc, disc_in, [], [(SSM_P, F32)] * 4, name="s5_disc")
    a_r = ar16[::SSM_GRP].reshape(1, SSM_N)
    a_i = ai16[::SSM_GRP].reshape(1, SSM_N)
    bblk_r, bblk_i = _block_diag_b(bbr).astype(BF16), _block_diag_b(bbi).astype(BF16)
    cblk_r, cblk_i = _block_diag_c(ws['ssm_cr']).astype(BF16), _block_diag_c(-ws['ssm_ci']).astype(BF16)
    xr, xi, yc = _s5_scan(u_p, bblk_r, bblk_i, a_r, a_i, reverse=False, tb=False, readout=(cblk_r, cblk_i),
                          name="s5_scan_fwd")
    g_p = _rowwise(_f_s5_gelu, [yc, u_p], [ws['ssm_d']], [(SSM_W, F32)], name="s5_gelu")[0]
    z_p = _mm(g_p, wc['ssm_w_glu'], name="s5_glu")
    g_t, z_t = _time_perm(g_p, inverse=True), _time_perm(z_p, inverse=True)
    on_consts = [ws['ssm_b_glu'], ws['out_norm_mla'], ws['out_norm_ssm']]
    ycat = _rowwise(_f_outnorm, [o_mla, g_t, z_t], on_consts, [(D, BF16)], name="out_norm")[0]
    x2 = _mm(ycat, wc['w_o'], res=x1, name="w_o")

    hx = _rowwise(_f_norm, [x2], [ws['xattn_norm']], [(D, BF16)], name="xattn_norm")[0]
    xq = _mm(hx, wc['xattn_w_q'], name="xattn_q")
    mn = _rowwise(_f_norm, [mem], [ws['mem_norm']], [(D, BF16)], name="mem_norm")[0]
    kvm = _mm(mn, wc['xattn_w_kv'], name="xattn_kv")
    xkn, xv = _rowwise(_f_memk, [kvm], [ws['xattn_k_norm']], [(H * XH, BF16), (H * XH, BF16)], name="xattn_knorm")
    xo = _xattn_fwd(xq, xkn, xv, ws['xattn_q_norm'])
    x3 = _mm(xo, wc['xattn_w_o'], tb=True, res=x2, name="xattn_o")

    x4, sv2 = _ffn_fwd(x3, ws['ffn2_norm'], wc, "ffn2")

    def f_loss(yb, tb):
        err = yb - tb
        return err * (1.0 / D), jnp.broadcast_to(jnp.sum(jnp.sum(err * err, axis=1, keepdims=True), axis=0, keepdims=True) * (0.5 / D), (1, LANES))

    dx4, loss = _rowwise(f_loss, [x4, target], [], [(D, F32)], [(1, LANES)], name="loss")

    dx3, gs['ffn2_norm'] = _ffn_bwd(x3, ws['ffn2_norm'], wc, sv2, dx4, "ffn2", send)

    dxo = _mm(dx3, wc['xattn_w_o'], out_dtype=BF16, name="xattn_o_dx")
    send({'xattn_w_o': _mm(dx3, xo, ta=True, out_dtype=GRAD_DTYPE, name="xattn_o_dw")})
    dxq, dxkn, dxv, gs['xattn_q_norm'] = _xattn_bwd(xq, xkn, xv, ws['xattn_q_norm'], dxo)
    dkvm, gs['xattn_k_norm'] = _rowwise_bwd(_f_memk, [kvm], [ws['xattn_k_norm']], [dxkn, dxv], row_grads={0: BF16},
                                            const_grads=[0], name="xattn_knorm_bwd")
    send({'xattn_w_kv': _mm(mn, dkvm, ta=True, out_dtype=GRAD_DTYPE, name="xattn_kv_dw")})
    dmn = _mm(dkvm, wc['xattn_w_kv'], tb=True, out_dtype=BF16, name="xattn_kv_dx")
    gs['mem_norm'] = _rowwise_bwd(_f_norm, [mem], [ws['mem_norm']], [dmn], row_grads={}, const_grads=[0], name="mem_norm_bwd")[0]
    token = send({'xattn_w_q': _mm(hx, dxq, ta=True, out_dtype=GRAD_DTYPE, name="xattn_q_dw")})
    dhx = _mm(dxq, wc['xattn_w_q'], tb=True, out_dtype=BF16, name="xattn_q_dx")
    dx2, gs['xattn_norm'] = _rowwise_bwd(_f_norm, [x2], [ws['xattn_norm']], [dhx], row_grads={0: F32}, const_grads=[0],
                                         adds={0: dx3}, name="xattn_norm_bwd", deps=[token])

    dycat = _mm(dx2, wc['w_o'], tb=True, out_dtype=BF16, name="w_o_dx")
    send({'w_o': _mm(ycat, dx2, ta=True, out_dtype=GRAD_DTYPE, name="w_o_dw")})
    do_mla, dg_t, dz_t, gs['ssm_b_glu'], gs['out_norm_mla'], gs['out_norm_ssm'] = _rowwise_bwd(
        _f_outnorm, [o_mla, g_t, z_t], on_consts, [dycat], row_grads={0: F32, 1: F32, 2: BF16}, const_grads=[0, 1, 2],
        name="out_norm_bwd")

    dz_p, dg_p = _time_perm(dz_t), _time_perm(dg_t)
    send({'ssm_w_glu': _mm(g_p, dz_p, ta=True, out_dtype=GRAD_DTYPE, name="s5_glu_dw")})
    dg_p = _mm(dz_p, wc['ssm_w_glu'], tb=True, res=dg_p, name="s5_glu_dx")
    dyc, du_d, gs['ssm_d'] = _rowwise_bwd(_f_s5_gelu, [yc, u_p], [ws['ssm_d']], [dg_p], row_grads={0: BF16, 1: F32},
                                          const_grads=[0], name="s5_gelu_bwd")
    lam_r, lam_i = _s5_scan(dyc, cblk_r, cblk_i, a_r, -a_i, reverse=True, tb=True, name="s5_scan_bwd")
    du_p, d_bblk_r, d_bblk_i, d_cblk_r, d_cblk_i, d_ar, d_ai = _s5_grads(lam_r, lam_i, xr, xi, u_p, dyc, du_d,
                                                                        bblk_r, bblk_i)
    du = _time_perm(du_p, inverse=True)
    gs['ssm_cr'] = jax.linear_transpose(_block_diag_c, ws['ssm_cr'])(d_cblk_r)[0]
    gs['ssm_ci'] = -jax.linear_transpose(_block_diag_c, ws['ssm_ci'])(d_cblk_i)[0]
    d_bbr = jax.linear_transpose(_block_diag_b, bbr)(d_bblk_r)[0]
    d_bbi = jax.linear_transpose(_block_diag_b, bbi)(d_bblk_i)[0]
    d_ar16 = jnp.zeros((SSM_G * SSM_GRP, SSM_P), F32).at[::SSM_GRP].set(d_ar.reshape(SSM_G, SSM_P))
    d_ai16 = jnp.zeros((SSM_G * SSM_GRP, SSM_P), F32).at[::SSM_GRP].set(d_ai.reshape(SSM_G, SSM_P))
    gs['ssm_lr'], gs['ssm_li'], gs['ssm_ldt'], gs['ssm_br'], gs['ssm_bi'] = _rowwise_bwd(
        _f_disc, disc_in, [], [d_ar16, d_ai16, d_bbr, d_bbi], row_grads={i: F32 for i in range(5)}, const_grads=[],
        name="s5_disc_bwd")

    delta, do_b = _rowwise(_f_delta, [do_mla, o_mla], [], [(H * LANES, F32), (H * VD, BF16)], name="mla_delta")
    dq, dk, dv = _attn_bwd(q, k, v, do_b, lse, delta)
    dqall, dkv, dkr, gs['qk_gq'], gs['qk_gk'] = _prep2_bwd(qall, kv, kr, cos, sin, ws['qk_gq'], ws['qk_gk'], dq, dk, dv)
    d_w_uq_e = _mm(dqall, c_q, ta=True, name="w_uq_dw")
    send({'mla_w_uq': jax.linear_transpose(_expand_w_uq, jax.ShapeDtypeStruct(w_uq_raw.shape, F32))(d_w_uq_e)[0]})
    dc_q = _mm(dqall, w_uq_e, out_dtype=BF16, name="w_uq_dx")
    send({'mla_w_ukv': _mm(dkv, c_kv, ta=True, out_dtype=GRAD_DTYPE, name="w_ukv_dw")})
    dc_kv = _mm(dkv, wc['mla_w_ukv'], out_dtype=BF16, name="w_ukv_dx")

    def f_prep1_bwd(pb, dcq, dckv, dub, dkrb, gq, gkv):
        _, vjp = jax.vjp(_f_prep1, pb[:, :Q_RANK + KV_RANK], gq, gkv)
        dpa, dgq, dgkv = vjp((dcq.astype(BF16), dckv.astype(BF16)))
        return jnp.concatenate([dpa, dub, dkrb], axis=-1), dgq, dgkv

    dproj, gs['q_norm'], gs['kv_norm'] = _rowwise(
        f_prep1_bwd, [proj, dc_q, dc_kv, du, dkr], [ws['q_norm'], ws['kv_norm']], [(IN_WP, BF16)],
        [(1, Q_RANK), (1, KV_RANK)], name="mla_prep1_bwd")
    d_w_in_e = _mm(dproj, h2, ta=True, name="w_in_dw")
    token = send({'w_in': jax.linear_transpose(_expand_w_in, jax.ShapeDtypeStruct(w_in_raw.shape, F32))(d_w_in_e)[0]})
    dh2 = _mm(dproj, w_in_e, out_dtype=BF16, name="w_in_dx")
    dx1, gs['mix_norm'] = _rowwise_bwd(_f_norm, [x1], [ws['mix_norm']], [dh2], row_grads={0: F32}, const_grads=[0],
                                       adds={0: dx2}, name="mix_norm_bwd", deps=[token])

    dx0, gs['ffn1_norm'] = _ffn_bwd(x, ws['ffn1_norm'], wc, sv1, dx1, "ffn1", send)
    return loss, dx0, gs


def _prep2_fwd(qall, kv, kr, cos, sin, gq, gk):
    return _rowwise(_f_prep2, [qall, kv, kr, cos, sin], [gq, gk], [(H * HQ, BF16), (H * HQ, BF16), (H * VD, BF16)],
                    ts=256, name="mla_prep2")


def _prep2_bwd(qall, kv, kr, cos, sin, gq, gk, dq, dk, dv):
    return _rowwise_bwd(_f_prep2, [qall, kv, kr, cos, sin], [gq, gk], [dq, dk, dv], row_grads={0: BF16, 1: BF16, 2: F32},
                        const_grads=[0, 1], ts=256, name="mla_prep2_bwd")


def _rope_tables(pos):
    half = ROPE // 2
    inv = ROPE_THETA ** (-jnp.arange(half, dtype=F32) / half)
    ang = pos.astype(F32)[:, None] * inv[None, :]
    z = jnp.zeros((pos.shape[0], LANES - ROPE), F32)
    cos, sin = jnp.cos(ang), jnp.sin(ang)
    return jnp.concatenate([cos, cos, z], axis=-1), jnp.concatenate([sin, sin, z], axis=-1)


def _small_layout(p):
    lr, li, ldt, br, bi = _layout_ssm_in(p['ssm_a_re'], p['ssm_a_im'], p['ssm_log_dt'], p['ssm_b_re'], p['ssm_b_im'])
    return {
        'ffn1_norm': p['ffn1_norm'].reshape(1, D), 'mix_norm': p['mix_norm'].reshape(1, D),
        'q_norm': p['mla_q_norm'].reshape(1, Q_RANK), 'kv_norm': p['mla_kv_norm'].reshape(1, KV_RANK),
        'qk_gq': _layout_qk_gain(p['mla_qk_norm_q']), 'qk_gk': _layout_qk_gain(p['mla_qk_norm_k']),
        'ssm_lr': lr, 'ssm_li': li, 'ssm_ldt': ldt, 'ssm_br': br, 'ssm_bi': bi,
        'ssm_cr': p['ssm_c_re'], 'ssm_ci': p['ssm_c_im'], 'ssm_d': p['ssm_d'].reshape(1, SSM_W),
        'ssm_b_glu': p['ssm_b_glu'].reshape(1, SSM_W),
        'out_norm_mla': p['out_norm_mla'].reshape(1, SSM_W), 'out_norm_ssm': p['out_norm_ssm'].reshape(1, SSM_W),
        'xattn_norm': p['xattn_norm'].reshape(1, D), 'mem_norm': p['mem_norm'].reshape(1, D),
        'xattn_q_norm': p['xattn_q_norm'].reshape(1, XH), 'xattn_k_norm': p['xattn_k_norm'].reshape(1, XH),
        'ffn2_norm': p['ffn2_norm'].reshape(1, D),
    }


def _pack(arrs, rows):
    flat = jnp.concatenate([a.reshape(-1) for a in arrs])
    return jnp.pad(flat, (0, rows * D - flat.shape[0])).reshape(rows, D)


def _unpack(flat, shapes):
    flat = flat.reshape(-1)
    out, off = [], 0
    for sh in shapes:
        n = int(np.prod(sh))
        out.append(flat[off:off + n].reshape(sh))
        off += n
    return out


def kernel(x, mem, positions, ffn1_norm, ffn1_w_gate, ffn1_w_up, ffn1_w_down, mix_norm, w_in, mla_q_norm, mla_w_uq, mla_kv_norm, mla_w_ukv, mla_qk_norm_q, mla_qk_norm_k, ssm_a_re, ssm_a_im, ssm_log_dt, ssm_b_re, ssm_b_im, ssm_c_re, ssm_c_im, ssm_d, ssm_w_glu, ssm_b_glu, out_norm_mla, out_norm_ssm, w_o, xattn_norm, mem_norm, xattn_w_q, xattn_w_kv, xattn_q_norm, xattn_k_norm, xattn_w_o, ffn2_norm, ffn2_w_gate, ffn2_w_up, ffn2_w_down, loss_target, m_ffn1_norm, m_ffn1_w_gate, m_ffn1_w_up, m_ffn1_w_down, m_mix_norm, m_w_in, m_mla_q_norm, m_mla_w_uq, m_mla_kv_norm, m_mla_w_ukv, m_mla_qk_norm_q, m_mla_qk_norm_k, m_ssm_a_re, m_ssm_a_im, m_ssm_log_dt, m_ssm_b_re, m_ssm_b_im, m_ssm_c_re, m_ssm_c_im, m_ssm_d, m_ssm_w_glu, m_ssm_b_glu, m_out_norm_mla, m_out_norm_ssm, m_w_o, m_xattn_norm, m_mem_norm, m_xattn_w_q, m_xattn_w_kv, m_xattn_q_norm, m_xattn_k_norm, m_xattn_w_o, m_ffn2_norm, m_ffn2_w_gate, m_ffn2_w_up, m_ffn2_w_down, v_ffn1_norm, v_ffn1_w_gate, v_ffn1_w_up, v_ffn1_w_down, v_mix_norm, v_w_in, v_mla_q_norm, v_mla_w_uq, v_mla_kv_norm, v_mla_w_ukv, v_mla_qk_norm_q, v_mla_qk_norm_k, v_ssm_a_re, v_ssm_a_im, v_ssm_log_dt, v_ssm_b_re, v_ssm_b_im, v_ssm_c_re, v_ssm_c_im, v_ssm_d, v_ssm_w_glu, v_ssm_b_glu, v_out_norm_mla, v_out_norm_ssm, v_w_o, v_xattn_norm, v_mem_norm, v_xattn_w_q, v_xattn_w_kv, v_xattn_q_norm, v_xattn_k_norm, v_xattn_w_o, v_ffn2_norm, v_ffn2_w_gate, v_ffn2_w_up, v_ffn2_w_down):
    args = dict(locals())
    w = {n: args[n] for n in WEIGHTS}
    mom = {n: args['m_' + n] for n in WEIGHTS}
    var = {n: args['v_' + n] for n in WEIGHTS}
    return _step(x, mem, positions, loss_target, w, mom, var)


GATHER_GROUPS = [('ffn1_gu', ['ffn1_w_gate', 'ffn1_w_up']), ('ffn1_down', ['ffn1_w_down']),
                 ('mix', ['w_in', 'mla_w_uq', 'mla_w_ukv', 'ssm_w_glu', 'w_o', 'xattn_w_q', 'xattn_w_kv', 'xattn_w_o']),
                 ('ffn2', ['ffn2_w_gate', 'ffn2_w_up', 'ffn2_w_down'])]
SCATTER_GROUPS = [('ffn2_down', ['ffn2_w_down']), ('ffn2_gu', ['ffn2_w_gate', 'ffn2_w_up']),
                  ('xattn', ['xattn_w_o', 'xattn_w_kv', 'xattn_w_q']),
                  ('mix', ['w_o', 'ssm_w_glu', 'mla_w_uq', 'mla_w_ukv', 'w_in']),
                  ('ffn1_down', ['ffn1_w_down']), ('ffn1_gu', ['ffn1_w_gate', 'ffn1_w_up'])]


def _step(x, mem, positions, loss_target, w, mom, var):
    blocks = {n: _to_exchange_layout(n, w[n][0]).astype(BF16) for n in SHARDED}
    gathers, token = [], None
    for tag, names in GATHER_GROUPS:
        ex = _Exchange([blocks[n] for n in names], [blocks[n].shape[0] for n in names], gather=True,
                       name="gather_" + tag, after=token)
        gathers.append((names, ex))
        token = ex.token
    me = _my_slot()
    wc = _Weights(gathers, me=me)

    rows = {n: (blocks[n].shape[0], blocks[n].shape[0]) for n in SHARDED}
    ready, scatters = {}, []

    def send(grads):
        ready.update({n: g.astype(GRAD_DTYPE) for n, g in grads.items()})
        for tag, names in SCATTER_GROUPS:
            if all(n in ready for n in names) and not any(t == tag for t, _, _ in scatters):
                ex = _Exchange([ready[n] for n in names], [rows[n] for n in names], gather=False, name="scatter_" + tag)
                scatters.append((tag, names, ex))
                return ex.token
        return None

    small = {n: w[n][0] for n in SMALL}
    ws = _small_layout(small)
    cos, sin = _rope_tables(positions[0])
    loss, dx, gs = _local_step(x[0], mem[0], cos, sin, loss_target[0], wc, ws, send, deps=[token])

    g_small = jax.linear_transpose(_small_layout, {n: jax.ShapeDtypeStruct(small[n].shape, F32) for n in SMALL})(gs)[0]
    small_shapes = [small[n].shape for n in SMALL]
    n_small = sum(int(np.prod(sh)) for sh in small_shapes) + 1
    rows_small = -(-n_small // (8 * D)) * 8
    small_pack = _pack([g_small[n] for n in SMALL] + [loss[0, :1]], rows_small)
    small_ex = _Exchange([small_pack], [(None, rows_small)], gather=False, name="scatter_small")

    out = {}
    for _, names, ex in scatters:
        for n, sent, p in zip(names, *ex.wait(dx)):
            r = w[n][0].shape[SHARD_AXIS[n]]
            if SHARD_AXIS[n] == 0:
                out[n] = _sum_adamw(me, sent, rows[n][0], p, r, w[n][0], mom[n][0], var[n][0], name="adamw_" + n)
            else:
                g = _sum_adamw(me, sent, rows[n][0], p, r, name="sum_" + n)[0].T
                out[n] = [g] + _adamw(g, w[n][0], mom[n][0], var[n][0], name="adamw_" + n)
    state = [_pack([t[n][0] for n in SMALL], rows_small) for t in (w, mom, var)]
    sent, p = small_ex.wait(dx)
    small_out = _sum_adamw(me, sent[0], None, p[0], rows_small, *state, name="adamw_small")
    loss_total = small_out[0].reshape(-1)[n_small - 1]
    for n, vals in zip(SMALL, zip(*[_unpack(flat, small_shapes) for flat in small_out])):
        out[n] = vals
    outs = [out[n][i][None] for i in range(4) for n in WEIGHTS]
    return (loss_total, dx[None], *outs)
```

```python
import math

import jax
import jax.numpy as jnp
import numpy as np
from jax import lax
from jax.experimental import pallas as pl
from jax.experimental.pallas import tpu as pltpu

F32 = jnp.float32
BF16 = jnp.bfloat16

N_DEV = 8
D = 1024
D_FF = 2752
D_FFP = 2816
MEM_LEN = 256
H = 4
Q_RANK, KV_RANK, NOPE, ROPE, VD = 384, 256, 128, 64, 128
QK = NOPE + ROPE
HQ = 2 * 128
SSM_W, SSM_G, SSM_GRP, SSM_P = 512, 32, 16, 64
SSM_N = SSM_G * SSM_P
SSM_PACK = 8
IN_W = 1216
IN_WP = 1408
XH = 128
EPS = 1e-6
LN2 = math.log(2.0)
ROPE_THETA = 10000.0
SCAN_CHUNKS = 8
SCAN_UNROLL = 8
ADAM_LR, ADAM_B1, ADAM_B2, ADAM_EPS, ADAM_WD, ADAM_STEP = 0.001, 0.9, 0.999, 1e-08, 0.01, 10

VMEM_LIMIT = 56 * 1024 * 1024
ACC_BYTES = 6 * 1024 * 1024
LANES = 128
BF16_ROWS = 16
GRAD_DTYPE = BF16
FF_SHARD = D_FF // N_DEV
FF_SHARD_P = 352
IN_SHARD = IN_W // N_DEV
IN_SHARD_P = 160

WEIGHTS = ['ffn1_norm', 'ffn1_w_gate', 'ffn1_w_up', 'ffn1_w_down', 'mix_norm', 'w_in', 'mla_q_norm', 'mla_w_uq',
           'mla_kv_norm', 'mla_w_ukv', 'mla_qk_norm_q', 'mla_qk_norm_k', 'ssm_a_re', 'ssm_a_im', 'ssm_log_dt',
           'ssm_b_re', 'ssm_b_im', 'ssm_c_re', 'ssm_c_im', 'ssm_d', 'ssm_w_glu', 'ssm_b_glu', 'out_norm_mla',
           'out_norm_ssm', 'w_o', 'xattn_norm', 'mem_norm', 'xattn_w_q', 'xattn_w_kv', 'xattn_q_norm',
           'xattn_k_norm', 'xattn_w_o', 'ffn2_norm', 'ffn2_w_gate', 'ffn2_w_up', 'ffn2_w_down']
SHARD_AXIS = {'ffn1_w_gate': 1, 'ffn1_w_up': 1, 'ffn1_w_down': 0, 'w_in': 1, 'mla_w_uq': 1, 'mla_w_ukv': 1,
              'ssm_w_glu': 0, 'w_o': 0, 'xattn_w_q': 0, 'xattn_w_kv': 0, 'xattn_w_o': 1,
              'ffn2_w_gate': 1, 'ffn2_w_up': 1, 'ffn2_w_down': 0}
SHARDED = [n for n in WEIGHTS if n in SHARD_AXIS]
SMALL = [n for n in WEIGHTS if n not in SHARD_AXIS]


def _params(sem=None):
    return pltpu.CompilerParams(dimension_semantics=sem, vmem_limit_bytes=VMEM_LIMIT)


def _tile(n, cap):
    if n <= cap:
        return n
    best = n
    for t in range(LANES, cap + 1, LANES):
        if n % t == 0:
            best = t
    return best


def _mm(a, b, *, ta=False, tb=False, out_dtype=F32, res=None, scale=1.0, name, tm_cap=512, tn_cap=1408, tk_cap=2816):
    m, k = (a.shape[1], a.shape[0]) if ta else a.shape
    k2, n = (b.shape[1], b.shape[0]) if tb else b.shape
    assert k == k2, (a.shape, b.shape, ta, tb)
    if ta:
        tk_cap = min(tk_cap, 512)
        tm_cap = 1408
    tm, tn, tk = _tile(m, tm_cap), _tile(n, tn_cap), _tile(k, tk_cap)
    if tm * tn * 4 > ACC_BYTES:
        tn = _tile(n, max(LANES, ACC_BYTES // (4 * tm) // LANES * LANES))
    nk = k // tk
    dims = (((0 if ta else 1,), (1 if tb else 0,)), ((), ()))
    has_res = res is not None

    def body(*refs):
        if has_res:
            a_ref, b_ref, r_ref, o_ref, acc_ref = refs
        else:
            a_ref, b_ref, o_ref, acc_ref = refs
        kk = pl.program_id(2)

        @pl.when(kk == 0)
        def _():
            acc_ref[...] = jnp.zeros_like(acc_ref)

        acc_ref[...] += lax.dot_general(a_ref[...].astype(BF16), b_ref[...].astype(BF16), dims,
                                        preferred_element_type=F32)

        @pl.when(kk == nk - 1)
        def _():
            out = acc_ref[...]
            if scale != 1.0:
                out = out * scale
            if has_res:
                out = out + r_ref[...].astype(F32)
            o_ref[...] = out.astype(o_ref.dtype)

    a_spec = pl.BlockSpec((tk, tm), lambda i, j, kk: (kk, i)) if ta else pl.BlockSpec((tm, tk), lambda i, j, kk: (i, kk))
    b_spec = pl.BlockSpec((tn, tk), lambda i, j, kk: (j, kk)) if tb else pl.BlockSpec((tk, tn), lambda i, j, kk: (kk, j))
    o_spec = pl.BlockSpec((tm, tn), lambda i, j, kk: (i, j))
    in_specs = [a_spec, b_spec] + ([o_spec] if has_res else [])
    args = (a, b) + ((res,) if has_res else ())
    return pl.pallas_call(
        body, name=name, grid=(m // tm, n // tn, nk), in_specs=in_specs, out_specs=o_spec,
        out_shape=jax.ShapeDtypeStruct((m, n), out_dtype), scratch_shapes=[pltpu.VMEM((tm, tn), F32)],
        compiler_params=_params(("parallel", "parallel", "arbitrary")),
    )(*args)


def _mm_grouped(a, b, *, tb=False, res=None, out_dtype=F32, name, tm=512):
    s = a.shape[0]
    g = b.shape[0]
    nb, ka = (b.shape[1], b.shape[2]) if tb else (b.shape[2], b.shape[1])
    assert a.shape[1] == g * ka
    tm = min(tm, s)
    dims = (((1,), (1 if tb else 0,)), ((), ()))
    has_res = res is not None

    def body(*refs):
        if has_res:
            a_ref, b_ref, r_ref, o_ref = refs
        else:
            a_ref, b_ref, o_ref = refs
        out = lax.dot_general(a_ref[...].astype(BF16), b_ref[...].astype(BF16), dims, preferred_element_type=F32)
        if has_res:
            out = out + r_ref[...].astype(F32)
        o_ref[...] = out.astype(o_ref.dtype)

    o_spec = pl.BlockSpec((tm, nb), lambda i, j: (i, j))
    in_specs = [pl.BlockSpec((tm, ka), lambda i, j: (i, j)), pl.BlockSpec((None,) + b.shape[1:], lambda i, j: (j, 0, 0))]
    return pl.pallas_call(
        body, name=name, grid=(s // tm, g), in_specs=in_specs + ([o_spec] if has_res else []), out_specs=o_spec,
        out_shape=jax.ShapeDtypeStruct((s, g * nb), out_dtype), compiler_params=_params(("parallel", "parallel")),
    )(a, b, *((res,) if has_res else ()))


def _mm_grouped_tn(a, b, *, ka, kb, name, tk=512):
    s = a.shape[0]
    g = a.shape[1] // ka
    assert b.shape[1] == g * kb
    tk = min(tk, s)
    nk = s // tk

    def body(a_ref, b_ref, o_ref):
        part = lax.dot_general(a_ref[...].astype(BF16), b_ref[...].astype(BF16), (((0,), (0,)), ((), ())),
                               preferred_element_type=F32)

        @pl.when(pl.program_id(1) == 0)
        def _():
            o_ref[...] = part

        @pl.when(pl.program_id(1) > 0)
        def _():
            o_ref[...] += part

    return pl.pallas_call(
        body, name=name, grid=(g, nk),
        in_specs=[pl.BlockSpec((tk, ka), lambda j, kk: (kk, j)), pl.BlockSpec((tk, kb), lambda j, kk: (kk, j))],
        out_specs=pl.BlockSpec((None, ka, kb), lambda j, kk: (j, 0, 0)),
        out_shape=jax.ShapeDtypeStruct((g, ka, kb), F32), compiler_params=_params(("parallel", "arbitrary")),
    )(a, b)


def _rowwise(fn, rows, consts, outs, accs=(), *, ts=512, name, deps=()):
    s = rows[0].shape[0]
    ts = min(ts, s)
    assert s % ts == 0
    n_rows, n_consts, n_outs = len(rows), len(consts), len(outs)
    deps = [d for d in deps if d is not None]
    consts = list(consts) + deps

    def body(*refs):
        ins = [r[...] for r in refs[:n_rows + n_consts]]
        res = fn(*ins)
        res = tuple(res) if isinstance(res, (tuple, list)) else (res,)
        out_refs = refs[n_rows + len(consts):]
        for o_ref, val in zip(out_refs[:n_outs], res[:n_outs]):
            o_ref[...] = val.astype(o_ref.dtype)
        if accs:
            first = pl.program_id(0) == 0

            @pl.when(first)
            def _():
                for a_ref, val in zip(out_refs[n_outs:], res[n_outs:]):
                    a_ref[...] = val.astype(F32)

            @pl.when(jnp.logical_not(first))
            def _():
                for a_ref, val in zip(out_refs[n_outs:], res[n_outs:]):
                    a_ref[...] += val.astype(F32)

    in_specs = [pl.BlockSpec((ts, r.shape[1]), lambda i: (i, 0)) for r in rows]
    in_specs += [pl.BlockSpec(c.shape, lambda i: (0, 0)) for c in consts]
    out_specs = [pl.BlockSpec((ts, w), lambda i: (i, 0)) for w, _ in outs]
    out_specs += [pl.BlockSpec(tuple(sh), lambda i: (0, 0)) for sh in accs]
    out_shape = [jax.ShapeDtypeStruct((s, w), dt) for w, dt in outs]
    out_shape += [jax.ShapeDtypeStruct(tuple(sh), F32) for sh in accs]
    res = pl.pallas_call(
        body, name=name, grid=(s // ts,), in_specs=in_specs, out_specs=out_specs, out_shape=out_shape,
        compiler_params=_params(("arbitrary",)),
    )(*rows, *consts)
    return res


def _rowwise_bwd(f, rows, consts, cts, *, row_grads, const_grads, adds=None, ts=512, name, deps=()):
    adds = adds or {}
    n_rows, n_consts, n_cts = len(rows), len(consts), len(cts)
    add_keys = sorted(adds)
    rg = sorted(row_grads)
    cg = sorted(const_grads)

    def fn(*args):
        r = args[:n_rows]
        c = args[n_rows:n_rows + n_consts]
        ct = args[n_rows + n_consts:n_rows + n_consts + n_cts]
        extra = args[n_rows + n_consts + n_cts:]
        outs, vjp = jax.vjp(f, *r, *c)
        outs = tuple(outs) if isinstance(outs, (tuple, list)) else (outs,)
        cot = tuple(g.astype(o.dtype) for g, o in zip(ct, outs))
        grads = vjp(cot if len(cot) > 1 else cot[0])
        res = []
        for i in rg:
            g = grads[i].astype(F32)
            if i in adds:
                g = g + extra[add_keys.index(i)].astype(F32)
            res.append(g)
        for i in cg:
            res.append(grads[n_rows + i])
        return tuple(res)

    rows_all = list(rows) + list(cts) + [adds[i] for i in add_keys]
    def fn2(*args):
        nr = len(rows_all)
        rr, cc = args[:nr], args[nr:]
        return fn(*rr[:n_rows], *cc, *rr[n_rows:])

    outs = [(rows[i].shape[1], row_grads[i]) for i in rg]
    accs = [consts[i].shape for i in cg]
    return _rowwise(fn2, rows_all, list(consts), outs, accs, ts=ts, name=name, deps=deps)


def _rms(x, g):
    xf = x.astype(F32)
    return xf * lax.rsqrt(jnp.mean(xf * xf, axis=-1, keepdims=True) + EPS) * g.astype(F32)


def _sigmoid(x):
    return 1.0 / (1.0 + jnp.exp(-x))


def _f_norm(x, g):
    return _rms(x, g).astype(BF16)


def _f_swiglu(gate, up):
    gate, up = gate.astype(F32), up.astype(F32)
    return (gate * _sigmoid(gate) * up).astype(BF16)


def _f_prep1(proj, gq, gkv):
    return _rms(proj[:, :Q_RANK], gq).astype(BF16), _rms(proj[:, Q_RANK:Q_RANK + KV_RANK], gkv).astype(BF16)


def _f_kr(proj):
    return (proj[:, Q_RANK + KV_RANK + SSM_W:],)


def _f_prep2(qall, kv, kr2, cos, sin, gq, gk):
    kr, krs = kr2[:, :LANES].astype(F32), kr2[:, LANES:].astype(F32)
    k_rot = kr * gk[1:2] * cos + krs * gk[2:3] * sin
    k_ss = jnp.sum(kr * kr, axis=-1, keepdims=True)
    q_scale = QK ** -0.5 / LN2
    qs, ks, vs = [], [], []
    for h in range(H):
        qn = qall[:, h * LANES:(h + 1) * LANES].astype(F32)
        qr = qall[:, (H + h) * LANES:(H + h + 1) * LANES].astype(F32)
        qrs = qall[:, (2 * H + h) * LANES:(2 * H + h + 1) * LANES].astype(F32)
        rstd = lax.rsqrt((jnp.sum(qn * qn, axis=-1, keepdims=True) + jnp.sum(qr * qr, axis=-1, keepdims=True)) / QK + EPS)
        rstd = rstd * q_scale
        qs += [qn * gq[0:1] * rstd, (qr * gq[1:2] * cos + qrs * gq[2:3] * sin) * rstd]
        kn = kv[:, 2 * h * LANES:(2 * h + 1) * LANES].astype(F32)
        rstd_k = lax.rsqrt((jnp.sum(kn * kn, axis=-1, keepdims=True) + k_ss) / QK + EPS)
        ks += [kn * gk[0:1] * rstd_k, k_rot * rstd_k]
        vs.append(kv[:, (2 * h + 1) * LANES:(2 * h + 2) * LANES])
    return (jnp.concatenate(qs, axis=-1).astype(BF16), jnp.concatenate(ks, axis=-1).astype(BF16),
            jnp.concatenate(vs, axis=-1).astype(BF16))


def _gelu(x):
    return 0.5 * x * (1.0 + jnp.tanh(math.sqrt(2.0 / math.pi) * (x + 0.044715 * (x * x * x))))


def _f_s5_gelu(yc, u, d):
    return _gelu(yc.astype(F32) + d * u.astype(F32))


def _f_outnorm(o_mla, g, z, b_glu, g_om, g_os):
    y_ssm = g * _sigmoid(z + b_glu)
    return jnp.concatenate([_rms(o_mla, g_om), _rms(y_ssm, g_os)], axis=-1).astype(BF16)


def _f_memk(kvm, gk):
    ks = [_rms(kvm[:, h * XH:(h + 1) * XH], gk) for h in range(H)]
    return jnp.concatenate(ks, axis=-1).astype(BF16), kvm[:, H * XH:].astype(BF16)


def _f_disc(lr, li, log_dt, br, bi):
    dt = jnp.exp(log_dt)
    decay = jnp.exp(lr * dt)
    ar = decay * jnp.cos(li * dt)
    ai = decay * jnp.sin(li * dt)
    den = lr * lr + li * li
    nr = ar - 1.0
    coef_r = (nr * lr + ai * li) / den
    coef_i = (ai * lr - nr * li) / den
    return ar, ai, coef_r * br - coef_i * bi, coef_r * bi + coef_i * br


def _causal_mask(i, j, tq, tk):
    qpos = i * tq + lax.broadcasted_iota(jnp.int32, (tq, tk), 0)
    kpos = j * tk + lax.broadcasted_iota(jnp.int32, (tq, tk), 1)
    return qpos >= kpos


def _attn_fwd(q, k, v, *, t=512):
    s = q.shape[0]
    t = min(t, s)
    nb = s // t

    def body(q_ref, k_ref, v_ref, o_ref, lse_ref, m_sc, l_sc, acc_sc):
        i, j = pl.program_id(1), pl.program_id(2)

        @pl.when(j == 0)
        def _():
            m_sc[...] = jnp.full_like(m_sc, -jnp.inf)
            l_sc[...] = jnp.zeros_like(l_sc)
            acc_sc[...] = jnp.zeros_like(acc_sc)

        def block(diagonal):
            sc = lax.dot_general(q_ref[...], k_ref[...], (((1,), (1,)), ((), ())), preferred_element_type=F32)
            if diagonal:
                sc = jnp.where(_causal_mask(i, j, t, t), sc, -jnp.inf)
            m_old = m_sc[...]
            m_new = jnp.maximum(m_old, jnp.max(sc, axis=-1, keepdims=True))
            p = jnp.exp2(sc - m_new)
            alpha = jnp.exp2(m_old - m_new)
            l_sc[...] = alpha * l_sc[...] + jnp.sum(p, axis=-1, keepdims=True)
            acc_sc[...] = alpha * acc_sc[...] + jnp.dot(p.astype(BF16), v_ref[...], preferred_element_type=F32)
            m_sc[...] = m_new

        pl.when(j < i)(lambda: block(False))

        @pl.when(j == i)
        def _():
            block(True)
            o_ref[...] = acc_sc[...] / l_sc[...]
            lse_ref[...] = jnp.broadcast_to(m_sc[...] + jnp.log2(l_sc[...]), lse_ref.shape)

    kv_map = lambda h, i, j: (jnp.minimum(j, i), h)
    return pl.pallas_call(
        body, name="mla_attn_fwd", grid=(H, nb, nb),
        in_specs=[pl.BlockSpec((t, HQ), lambda h, i, j: (i, h)), pl.BlockSpec((t, HQ), kv_map),
                  pl.BlockSpec((t, VD), kv_map)],
        out_specs=[pl.BlockSpec((t, VD), lambda h, i, j: (i, h)), pl.BlockSpec((t, LANES), lambda h, i, j: (i, h))],
        out_shape=[jax.ShapeDtypeStruct((s, H * VD), F32), jax.ShapeDtypeStruct((s, H * LANES), F32)],
        scratch_shapes=[pltpu.VMEM((t, 1), F32), pltpu.VMEM((t, 1), F32), pltpu.VMEM((t, VD), F32)],
        compiler_params=_params(("parallel", "parallel", "arbitrary")),
    )(q, k, v)


def _attn_probs(q_ref, k_ref, v_ref, do_ref, lse_ref, dl_ref, i, j, t, diagonal):
    sc = lax.dot_general(q_ref[...], k_ref[...], (((1,), (1,)), ((), ())), preferred_element_type=F32)
    p = jnp.exp2(sc - jnp.tile(lse_ref[...], (1, t // LANES)))
    if diagonal:
        p = jnp.where(_causal_mask(i, j, t, t), p, 0.0)
    dp = lax.dot_general(do_ref[...], v_ref[...], (((1,), (1,)), ((), ())), preferred_element_type=F32)
    ds = p * (dp - jnp.tile(dl_ref[...], (1, t // LANES)))
    return p, ds


def _attn_bwd(q, k, v, do, lse, delta, *, t=512):
    s = q.shape[0]
    t = min(t, s)
    nb = s // t

    def dq_body(q_ref, k_ref, v_ref, do_ref, lse_ref, dl_ref, dq_ref, acc_sc):
        i, j = pl.program_id(1), pl.program_id(2)

        @pl.when(j == 0)
        def _():
            acc_sc[...] = jnp.zeros_like(acc_sc)

        def block(diagonal):
            _, ds = _attn_probs(q_ref, k_ref, v_ref, do_ref, lse_ref, dl_ref, i, j, t, diagonal)
            acc_sc[...] += jnp.dot(ds.astype(BF16), k_ref[...], preferred_element_type=F32)

        pl.when(j < i)(lambda: block(False))

        @pl.when(j == i)
        def _():
            block(True)
            dq_ref[...] = acc_sc[...] * LN2

    q_map = lambda h, i, j: (i, h)
    kv_map = lambda h, i, j: (jnp.minimum(j, i), h)
    dq = pl.pallas_call(
        dq_body, name="mla_attn_dq", grid=(H, nb, nb),
        in_specs=[pl.BlockSpec((t, HQ), q_map), pl.BlockSpec((t, HQ), kv_map), pl.BlockSpec((t, VD), kv_map),
                  pl.BlockSpec((t, VD), q_map), pl.BlockSpec((t, LANES), q_map), pl.BlockSpec((t, LANES), q_map)],
        out_specs=pl.BlockSpec((t, HQ), q_map),
        out_shape=jax.ShapeDtypeStruct((s, H * HQ), F32),
        scratch_shapes=[pltpu.VMEM((t, HQ), F32)],
        compiler_params=_params(("parallel", "parallel", "arbitrary")),
    )(q, k, v, do, lse, delta)

    def dkv_body(q_ref, k_ref, v_ref, do_ref, lse_ref, dl_ref, dk_ref, dv_ref, dk_sc, dv_sc):
        j, i = pl.program_id(1), pl.program_id(2)

        @pl.when(i == 0)
        def _():
            dk_sc[...] = jnp.zeros_like(dk_sc)
            dv_sc[...] = jnp.zeros_like(dv_sc)

        def block(diagonal):
            p, ds = _attn_probs(q_ref, k_ref, v_ref, do_ref, lse_ref, dl_ref, i, j, t, diagonal)
            dv_sc[...] += lax.dot_general(p.astype(BF16), do_ref[...], (((0,), (0,)), ((), ())), preferred_element_type=F32)
            dk_sc[...] += lax.dot_general(ds.astype(BF16), q_ref[...], (((0,), (0,)), ((), ())), preferred_element_type=F32)

        pl.when(i > j)(lambda: block(False))
        pl.when(i == j)(lambda: block(True))

        @pl.when(i == nb - 1)
        def _():
            dk_ref[...] = dk_sc[...] * LN2
            dv_ref[...] = dv_sc[...]

    q_map2 = lambda h, j, i: (jnp.maximum(i, j), h)
    kv_map2 = lambda h, j, i: (j, h)
    dk, dv = pl.pallas_call(
        dkv_body, name="mla_attn_dkv", grid=(H, nb, nb),
        in_specs=[pl.BlockSpec((t, HQ), q_map2), pl.BlockSpec((t, HQ), kv_map2), pl.BlockSpec((t, VD), kv_map2),
                  pl.BlockSpec((t, VD), q_map2), pl.BlockSpec((t, LANES), q_map2), pl.BlockSpec((t, LANES), q_map2)],
        out_specs=[pl.BlockSpec((t, HQ), kv_map2), pl.BlockSpec((t, VD), kv_map2)],
        out_shape=[jax.ShapeDtypeStruct((s, H * HQ), F32), jax.ShapeDtypeStruct((s, H * VD), F32)],
        scratch_shapes=[pltpu.VMEM((t, HQ), F32), pltpu.VMEM((t, VD), F32)],
        compiler_params=_params(("parallel", "parallel", "arbitrary")),
    )(q, k, v, do, lse, delta)
    return dq, dk, dv


def _f_delta(do, o):
    prod = do.astype(F32) * o.astype(F32)
    parts = [jnp.broadcast_to(jnp.sum(prod[:, h * VD:(h + 1) * VD], axis=-1, keepdims=True), (do.shape[0], LANES))
             for h in range(H)]
    return jnp.concatenate(parts, axis=-1), do.astype(BF16)


def _xattn_head(qh, kh, gq):
    qn = _rms(qh, gq) * (XH ** -0.5)
    sc = lax.dot_general(qn.astype(BF16), kh, (((1,), (1,)), ((), ())), preferred_element_type=F32)
    sc = sc - jnp.max(sc, axis=-1, keepdims=True)
    e = jnp.exp(sc)
    return qn, e / jnp.sum(e, axis=-1, keepdims=True)


def _xattn_fwd(q, kn, v, gq, *, ts=512):
    def fn(qb, knb, vb, g):
        outs = []
        for h in range(H):
            sl = slice(h * XH, (h + 1) * XH)
            _, p = _xattn_head(qb[:, sl], knb[:, sl], g)
            outs.append(jnp.dot(p.astype(BF16), vb[:, sl], preferred_element_type=F32))
        return (jnp.concatenate(outs, axis=-1),)

    return _rowwise(fn, [q], [kn, v, gq], [(H * XH, BF16)], ts=ts, name="xattn_fwd")[0]


def _xattn_bwd(q, kn, v, gq, do, *, ts=512):
    def fn(qb, dob, knb, vb, g):
        dqs, dks, dvs = [], [], []
        dg = jnp.zeros((1, XH), F32)
        for h in range(H):
            sl = slice(h * XH, (h + 1) * XH)
            qh, kh, vh, doh = qb[:, sl], knb[:, sl], vb[:, sl], dob[:, sl].astype(BF16)
            qn, p = _xattn_head(qh, kh, g)
            dp = lax.dot_general(doh, vh, (((1,), (1,)), ((), ())), preferred_element_type=F32)
            dvs.append(lax.dot_general(p.astype(BF16), doh, (((0,), (0,)), ((), ())), preferred_element_type=F32))
            ds = (p * (dp - jnp.sum(dp * p, axis=-1, keepdims=True))).astype(BF16)
            dqn = jnp.dot(ds, kh, preferred_element_type=F32)
            dks.append(lax.dot_general(ds, qn.astype(BF16), (((0,), (0,)), ((), ())), preferred_element_type=F32))
            _, vjp_n = jax.vjp(lambda a, b: _rms(a, b) * (XH ** -0.5), qh, g)
            dqh, dgh = vjp_n(dqn)
            dqs.append(dqh)
            dg = dg + dgh
        return (jnp.concatenate(dqs, axis=-1), jnp.concatenate(dks, axis=-1), jnp.concatenate(dvs, axis=-1), dg)

    return _rowwise(fn, [q, do], [kn, v, gq], [(H * XH, BF16)], [kn.shape, v.shape, gq.shape], ts=ts, name="xattn_bwd")


def _cmul(ar, ai, xr, xi):
    return ar * xr - ai * xi, ar * xi + ai * xr


def _scan_in_place(xr_ref, xi_ref, ar, ai, *, reverse):
    s, cw = xr_ref.shape
    c = SCAN_CHUNKS
    tt = s // c
    a_r = jnp.broadcast_to(ar, (c, cw))
    a_i = jnp.broadcast_to(ai, (c, cw))
    zero = jnp.zeros((c, cw), F32)

    def row(step):
        t = (tt - 1 - step) if reverse else step
        return pl.ds(pl.multiple_of(t * c, c), c)

    def local(step, carry):
        sr, si, qr, qi = carry
        r = row(step)
        nr, ni = _cmul(a_r, a_i, sr, si)
        nr, ni = nr + xr_ref[r, :], ni + xi_ref[r, :]
        xr_ref[r, :] = nr
        xi_ref[r, :] = ni
        return (nr, ni) + _cmul(a_r, a_i, qr, qi)

    end_r, end_i, pr, pi = lax.fori_loop(0, tt, local, (zero, zero, jnp.ones((c, cw), F32), zero), unroll=SCAN_UNROLL)

    rows_id = lax.broadcasted_iota(jnp.int32, (c, cw), 0)
    car_r, car_i = zero, zero
    cur_r, cur_i = jnp.zeros((1, cw), F32), jnp.zeros((1, cw), F32)
    order = range(c - 1, -1, -1) if reverse else range(c)
    for kk in order:
        car_r = jnp.where(rows_id == kk, cur_r, car_r)
        car_i = jnp.where(rows_id == kk, cur_i, car_i)
        nr, ni = _cmul(pr[0:1], pi[0:1], cur_r, cur_i)
        cur_r = nr + end_r[kk:kk + 1]
        cur_i = ni + end_i[kk:kk + 1]

    def fix(step, carry):
        qr, qi = _cmul(a_r, a_i, *carry)
        r = row(step)
        dr, di = _cmul(qr, qi, car_r, car_i)
        xr_ref[r, :] += dr
        xi_ref[r, :] += di
        return qr, qi

    lax.fori_loop(0, tt, fix, (jnp.ones((c, cw), F32), zero), unroll=SCAN_UNROLL)


S5_ROWS = 512


def _s5_scan(v, w_r, w_i, ar, ai, *, reverse, tb, readout=None, name):
    s = v.shape[0]
    g = w_r.shape[0]
    nv, ns = SSM_PACK * SSM_GRP, SSM_PACK * SSM_P
    rows = min(S5_ROWS, s)
    dims = (((1,), (1 if tb else 0,)), ((), ()))
    n_w = 2 if readout is None else 4

    def body(v_ref, ar_ref, ai_ref, *refs):
        w = [r[...] for r in refs[:n_w]]
        xr_ref, xi_ref = refs[n_w:n_w + 2]
        for r0 in range(0, s, rows):
            vb = v_ref[r0:r0 + rows, :].astype(BF16)
            xr_ref[r0:r0 + rows, :] = lax.dot_general(vb, w[0], dims, preferred_element_type=F32)
            xi_ref[r0:r0 + rows, :] = lax.dot_general(vb, w[1], dims, preferred_element_type=F32)
        _scan_in_place(xr_ref, xi_ref, ar_ref[...], ai_ref[...], reverse=reverse)
        if readout is not None:
            y_ref = refs[n_w + 2]
            for r0 in range(0, s, rows):
                y_ref[r0:r0 + rows, :] = (
                    jnp.dot(xr_ref[r0:r0 + rows, :].astype(BF16), w[2], preferred_element_type=F32)
                    + jnp.dot(xi_ref[r0:r0 + rows, :].astype(BF16), w[3], preferred_element_type=F32))

    col = lambda j: (0, j)
    w_spec = lambda a: pl.BlockSpec((None,) + a.shape[1:], lambda j: (j, 0, 0))
    weights = [w_r, w_i] + (list(readout) if readout is not None else [])
    out_specs = [pl.BlockSpec((s, ns), col)] * 2 + ([pl.BlockSpec((s, nv), col)] if readout is not None else [])
    out_shape = [jax.ShapeDtypeStruct((s, g * ns), F32)] * 2 + (
        [jax.ShapeDtypeStruct((s, g * nv), F32)] if readout is not None else [])
    return pl.pallas_call(
        body, name=name, grid=(g,),
        in_specs=[pl.BlockSpec((s, nv), col), pl.BlockSpec((1, ns), col), pl.BlockSpec((1, ns), col)] + [w_spec(a) for a in weights],
        out_specs=out_specs, out_shape=out_shape, compiler_params=_params(("parallel",)),
    )(v, ar, ai, *weights)


def _s5_grads(lam_r, lam_i, xr, xi, u, dyc, du_d, b_r, b_i):
    s = u.shape[0]
    g = b_r.shape[0]
    nv, ns, c = SSM_PACK * SSM_GRP, SSM_PACK * SSM_P, SCAN_CHUNKS
    rows = min(S5_ROWS, s)
    slabs = rows // c
    last_slab = s // c - 1
    nt = (((1,), (1,)), ((), ()))
    tn = (((0,), (0,)), ((), ()))

    def body(lr_ref, li_ref, xr_ref, xi_ref, pr_ref, pi_ref, u_ref, dy_ref, dud_ref, br_ref, bi_ref,
             du_ref, dbr_ref, dbi_ref, dcr_ref, dci_ref, dar_ref, dai_ref):
        first = pl.program_id(1) == 0
        l_r, l_i, x_r, x_i = lr_ref[...], li_ref[...], xr_ref[...], xi_ref[...]
        lrb, lib = l_r.astype(BF16), l_i.astype(BF16)
        du_ref[...] = (dud_ref[...] + lax.dot_general(lrb, br_ref[...], nt, preferred_element_type=F32)
                       + lax.dot_general(lib, bi_ref[...], nt, preferred_element_type=F32))
        ub, dyb = u_ref[...].astype(BF16), dy_ref[...].astype(BF16)
        rows_id = lax.broadcasted_iota(jnp.int32, (c, ns), 0)

        def before(p_ref, x):
            p = p_ref[...]
            p = jnp.where(first, jnp.where(rows_id == 0, 0.0, pltpu.roll(p, 1, 0)), p)
            return jnp.concatenate([p, x[:rows - c]], axis=0)

        xp_r, xp_i = before(pr_ref, x_r), before(pi_ref, x_i)
        parts = (lax.dot_general(ub, lrb, tn, preferred_element_type=F32),
                 lax.dot_general(ub, lib, tn, preferred_element_type=F32),
                 lax.dot_general(x_r.astype(BF16), dyb, tn, preferred_element_type=F32),
                 lax.dot_general(x_i.astype(BF16), dyb, tn, preferred_element_type=F32),
                 jnp.sum(l_r * xp_r + l_i * xp_i, axis=0, keepdims=True),
                 jnp.sum(l_i * xp_r - l_r * xp_i, axis=0, keepdims=True))
        accs = (dbr_ref, dbi_ref, dcr_ref, dci_ref, dar_ref, dai_ref)

        @pl.when(first)
        def _():
            for a_ref, val in zip(accs, parts):
                a_ref[...] = val

        @pl.when(jnp.logical_not(first))
        def _():
            for a_ref, val in zip(accs, parts):
                a_ref[...] += val

    state = pl.BlockSpec((rows, ns), lambda j, k: (k, j))
    chan = pl.BlockSpec((rows, nv), lambda j, k: (k, j))
    slab = pl.BlockSpec((c, ns), lambda j, k: (jnp.where(k == 0, last_slab, k * slabs - 1), j))
    per_b = pl.BlockSpec((None, nv, ns), lambda j, k: (j, 0, 0))
    per_c = pl.BlockSpec((None, ns, nv), lambda j, k: (j, 0, 0))
    per_a = pl.BlockSpec((1, ns), lambda j, k: (0, j))
    return pl.pallas_call(
        body, name="s5_grads", grid=(g, s // rows),
        in_specs=[state, state, state, state, slab, slab, chan, chan, chan, per_b, per_b],
        out_specs=[chan, per_b, per_b, per_c, per_c, per_a, per_a],
        out_shape=[jax.ShapeDtypeStruct((s, g * nv), F32), jax.ShapeDtypeStruct((g, nv, ns), F32),
                   jax.ShapeDtypeStruct((g, nv, ns), F32), jax.ShapeDtypeStruct((g, ns, nv), F32),
                   jax.ShapeDtypeStruct((g, ns, nv), F32), jax.ShapeDtypeStruct((1, g * ns), F32),
                   jax.ShapeDtypeStruct((1, g * ns), F32)],
        compiler_params=_params(("parallel", "arbitrary")),
    )(lam_r, lam_i, xr, xi, xr, xi, u, dyc, du_d, b_r, b_i)


def _mesh_place():
    x, y, c = lax.axis_index("x"), lax.axis_index("y"), lax.axis_index("c")
    peers = []
    for k in range(1, N_DEV):
        px, py, pc = x ^ ((k >> 2) & 1), y ^ ((k >> 1) & 1), c ^ (k & 1)
        peers.append(((px, py, pc), 4 * px + 2 * py + pc))
    return 4 * x + 2 * y + c, peers


class _Exchange:
    def __init__(self, arrays, rows, *, gather, name, after=None):
        self.n_arr, self.rows, self.gather, self.name = len(arrays), rows, gather, name
        n_arr = self.n_arr
        if gather:
            assert all(r % BF16_ROWS == 0 for r in rows)
            lands = [lax.empty((N_DEV * r, a.shape[1]), a.dtype) for a, r in zip(arrays, rows)]
        else:
            lands = [lax.empty((N_DEV - 1, a.shape[0] if st is None else n, a.shape[1]), a.dtype)
                     for a, (st, n) in zip(arrays, rows)]
        has_after = after is not None

        def body(*refs):
            ins, zones = refs[:n_arr], refs[n_arr:2 * n_arr]
            sems = refs[2 * n_arr + has_after:4 * n_arr + has_after]
            token = refs[-1]
            me, peers = _mesh_place()
            for i in range(n_arr):
                for k, (pxyz, pid) in enumerate(peers):
                    if gather:
                        src = ins[i]
                        dst = zones[i].at[pl.ds(pl.multiple_of(me * rows[i], BF16_ROWS), rows[i])]
                    else:
                        stride, n = rows[i]
                        src = ins[i] if stride is None else ins[i].at[pl.ds(pl.multiple_of(pid * stride, BF16_ROWS), n)]
                        dst = zones[i].at[k]
                    pltpu.make_async_remote_copy(
                        src_ref=src, dst_ref=dst, send_sem=sems[2 * i], recv_sem=sems[2 * i + 1],
                        device_id=pxyz, device_id_type=pl.DeviceIdType.MESH).start()
            token[...] = jnp.zeros_like(token)

        hbm = pl.BlockSpec(memory_space=pltpu.HBM)
        sem = pl.BlockSpec(memory_space=pltpu.SEMAPHORE)
        args = [pltpu.with_memory_space_constraint(a, pltpu.HBM) for a in list(arrays) + lands]
        res = pl.pallas_call(
            body, name=name + "_start",
            in_specs=[hbm] * (2 * n_arr) + ([pl.BlockSpec(memory_space=pl.ANY)] if has_after else []),
            out_specs=[sem] * (2 * n_arr) + [hbm] * (2 * n_arr) + [pl.BlockSpec(memory_space=pltpu.VMEM)],
            out_shape=[pltpu.SemaphoreType.DMA(())] * (2 * n_arr) + [pltpu.HBM(a.shape, a.dtype) for a in args]
            + [jax.ShapeDtypeStruct((8, LANES), F32)],
            input_output_aliases={i: 2 * n_arr + i for i in range(2 * n_arr)},
            compiler_params=pltpu.CompilerParams(has_side_effects=pltpu.SideEffectType.DATAFLOW_SIDE_EFFECTING),
        )(*args, *([after] if has_after else []))
        self.sems, self.thru, self.token = res[:2 * n_arr], res[2 * n_arr:4 * n_arr], res[-1]

    def wait(self, after):
        n_arr = self.n_arr

        def body(*refs):
            zones, sems = refs[n_arr:2 * n_arr], refs[2 * n_arr:4 * n_arr]
            myself = (lax.axis_index("x"), lax.axis_index("y"), lax.axis_index("c"))
            for i in range(n_arr):
                seven = zones[i].at[pl.ds(0, (N_DEV - 1) * self.rows[i])] if self.gather else zones[i]
                all_seven = pltpu.make_async_remote_copy(
                    src_ref=seven, dst_ref=seven, send_sem=sems[2 * i], recv_sem=sems[2 * i + 1],
                    device_id=myself, device_id_type=pl.DeviceIdType.MESH)
                all_seven.wait_recv()
                all_seven.wait_send()

        hbm = pl.BlockSpec(memory_space=pltpu.HBM)
        sem = pl.BlockSpec(memory_space=pltpu.SEMAPHORE)
        res = pl.pallas_call(
            body, name=self.name + "_wait",
            in_specs=[hbm] * (2 * n_arr) + [sem] * (2 * n_arr) + [pl.BlockSpec(memory_space=pl.ANY)],
            out_specs=[hbm] * (2 * n_arr), out_shape=[pltpu.HBM(a.shape, a.dtype) for a in self.thru],
            input_output_aliases={i: i for i in range(2 * n_arr)},
            compiler_params=pltpu.CompilerParams(has_side_effects=pltpu.SideEffectType.DATAFLOW_SIDE_EFFECTING),
        )(*self.thru, *self.sems, after)
        return res[:n_arr], res[n_arr:]


def _my_slot():
    me = 4 * lax.axis_index("x") + 2 * lax.axis_index("y") + lax.axis_index("c")
    return me.astype(jnp.int32).reshape(1)


def _place_own(gathered, block, me, *, name):
    r, c = block.shape

    def body(me_ref, b_ref, g_ref, o_ref):
        o_ref[...] = b_ref[...]

    return pl.pallas_call(
        body, name=name, out_shape=jax.ShapeDtypeStruct(gathered.shape, gathered.dtype),
        grid_spec=pltpu.PrefetchScalarGridSpec(
            num_scalar_prefetch=1, grid=(1,),
            in_specs=[pl.BlockSpec((r, c), lambda i, me_ref: (0, 0)), pl.BlockSpec(memory_space=pl.ANY)],
            out_specs=pl.BlockSpec((r, c), lambda i, me_ref: (me_ref[0], 0))),
        input_output_aliases={2: 0}, compiler_params=_params(("arbitrary",)),
    )(me, block, gathered)


def _elementwise_tiles(r, c):
    if r % 128 == 0:
        return 128, c
    return r, (256 if c % 256 == 0 else c)


def _adamw_math(g, w, m, v):
    nm = ADAM_B1 * m + (1.0 - ADAM_B1) * g
    nv = ADAM_B2 * v + (1.0 - ADAM_B2) * (g * g)
    m_hat = nm / (1.0 - ADAM_B1 ** ADAM_STEP)
    v_hat = nv / (1.0 - ADAM_B2 ** ADAM_STEP)
    return -ADAM_LR * (m_hat / (jnp.sqrt(v_hat) + ADAM_EPS) + ADAM_WD * w), nm, nv


def _sum_parts(me_ref, own_ref, p_ref, r):
    own = own_ref[...].astype(F32)
    g = None
    for d in range(N_DEV):
        k = jnp.bitwise_xor(me_ref[0], d)
        term = jnp.where(k == 0, own, p_ref[jnp.maximum(k, 1) - 1].astype(F32))
        g = term if g is None else g + term
    return g[0:r, :]


def _sum_adamw(me, sent, stride, parts, r, w=None, m=None, v=None, *, name):
    _, own_rows, cdim = parts.shape
    assert stride is None or stride == own_rows
    tc = 256 if cdim % 256 == 0 else cdim
    update = w is not None

    def body(me_ref, own_ref, p_ref, *refs):
        g = _sum_parts(me_ref, own_ref, p_ref, r)
        if update:
            w_ref, m_ref, v_ref, g_ref, d_ref, nm_ref, nv_ref = refs
            d_ref[...], nm_ref[...], nv_ref[...] = _adamw_math(g, w_ref[...], m_ref[...], v_ref[...])
        else:
            g_ref, = refs
        g_ref[...] = g

    blk = pl.BlockSpec((r, tc), lambda j, me_ref: (0, j))
    own_spec = pl.BlockSpec((own_rows, tc), (lambda j, me_ref: (0, j)) if stride is None else (lambda j, me_ref: (me_ref[0], j)))
    n_out = 4 if update else 1
    res = pl.pallas_call(
        body, name=name, out_shape=[jax.ShapeDtypeStruct((r, cdim), F32)] * n_out,
        grid_spec=pltpu.PrefetchScalarGridSpec(
            num_scalar_prefetch=1, grid=(cdim // tc,),
            in_specs=[own_spec, pl.BlockSpec((N_DEV - 1, own_rows, tc), lambda j, me_ref: (0, 0, j))]
            + ([blk] * 3 if update else []),
            out_specs=[blk] * n_out),
        compiler_params=_params(("parallel",)),
    )(me, sent, parts, *((w, m, v) if update else ()))
    return list(res)


def _adamw(g, w, m, v, *, name):
    r, cdim = w.shape
    tr, tc = _elementwise_tiles(r, cdim)

    def body(g_ref, w_ref, m_ref, v_ref, d_ref, nm_ref, nv_ref):
        d_ref[...], nm_ref[...], nv_ref[...] = _adamw_math(g_ref[...], w_ref[...], m_ref[...], v_ref[...])

    blk = pl.BlockSpec((tr, tc), lambda i, j: (i, j))
    return list(pl.pallas_call(
        body, name=name, grid=(r // tr, cdim // tc), in_specs=[blk] * 4,
        out_specs=[blk] * 3, out_shape=[jax.ShapeDtypeStruct((r, cdim), F32)] * 3,
        compiler_params=_params(("parallel", "parallel")),
    )(g, w, m, v))


SHARD_ROWS_P = {n: (FF_SHARD_P if 'ffn' in n else IN_SHARD_P if n == 'w_in' else None) for n in SHARDED}


def _to_exchange_layout(name, shard):
    t = shard.T if SHARD_AXIS[name] == 1 else shard
    pad = SHARD_ROWS_P[name]
    return t if pad is None else jnp.pad(t, ((0, pad - t.shape[0]), (0, 0)))


def _expand_w_in(wt):
    wt = wt.reshape(N_DEV, IN_SHARD_P, D)[:, :IN_SHARD].reshape(IN_W, D)
    o = Q_RANK + KV_RANK
    kr1, kr2 = wt[o:o + ROPE // 2], wt[o + ROPE // 2:o + ROPE]
    z = jnp.zeros((LANES - ROPE, D), wt.dtype)
    return jnp.concatenate([wt[:o], wt[o + ROPE:], kr1, kr2, z, -kr2, kr1, z], axis=0)


def _expand_w_uq(wt):
    w = wt.reshape(H, QK, Q_RANK)
    z = jnp.zeros((H, LANES - ROPE, Q_RANK), w.dtype)
    q1, q2 = w[:, NOPE:NOPE + ROPE // 2], w[:, NOPE + ROPE // 2:]
    return jnp.concatenate([w[:, :NOPE].reshape(H * NOPE, Q_RANK),
                            jnp.concatenate([q1, q2, z], axis=1).reshape(H * LANES, Q_RANK),
                            jnp.concatenate([-q2, q1, z], axis=1).reshape(H * LANES, Q_RANK)], axis=0)


def _layout_qk_gain(g):
    g = g.reshape(QK)
    g1, g2, z = g[NOPE:NOPE + ROPE // 2], g[NOPE + ROPE // 2:], jnp.zeros((LANES - ROPE,), g.dtype)
    return jnp.stack([g[:NOPE], jnp.concatenate([g1, g2, z]), jnp.concatenate([g2, g1, z])])


def _rep16(a):
    return jnp.repeat(a, SSM_GRP, axis=0)


def _layout_ssm_in(a_re, a_im, log_dt, b_re, b_im):
    b_r = jnp.transpose(b_re, (0, 2, 1)).reshape(SSM_G * SSM_GRP, SSM_P)
    b_i = jnp.transpose(b_im, (0, 2, 1)).reshape(SSM_G * SSM_GRP, SSM_P)
    ldt = jnp.broadcast_to(log_dt.reshape(SSM_G, 1), (SSM_G, SSM_P))
    return _rep16(a_re), _rep16(a_im), _rep16(ldt), b_r, b_i


def _block_diag_b(bb):
    eye = jnp.eye(SSM_PACK, dtype=bb.dtype)
    b5 = bb.reshape(SSM_G // SSM_PACK, SSM_PACK, SSM_GRP, 1, SSM_P) * eye[None, :, None, :, None]
    return b5.reshape(SSM_G // SSM_PACK, SSM_PACK * SSM_GRP, SSM_PACK * SSM_P)


def _block_diag_c(cc):
    eye = jnp.eye(SSM_PACK, dtype=cc.dtype)
    c5 = jnp.transpose(cc, (0, 2, 1)).reshape(SSM_G // SSM_PACK, SSM_PACK, SSM_P, 1, SSM_GRP) * eye[None, :, None, :, None]
    return c5.reshape(SSM_G // SSM_PACK, SSM_PACK * SSM_P, SSM_PACK * SSM_GRP)


def _time_perm(a, inverse=False):
    s, w = a.shape
    c = SCAN_CHUNKS
    if inverse:
        return jnp.transpose(a.reshape(s // c, c, w), (1, 0, 2)).reshape(s, w)
    return jnp.transpose(a.reshape(c, s // c, w), (1, 0, 2)).reshape(s, w)


class _Weights:
    def __init__(self, groups=(), landed=None, me=None):
        self.groups, self.landed, self.me = list(groups), dict(landed or {}), me

    def get(self, name, after):
        if name not in self.landed:
            names, exchange = next(g for g in self.groups if name in g[0])
            for n, block, gathered in zip(names, *exchange.wait(after)):
                self.landed[n] = _place_own(gathered, block, self.me, name="place_" + n)
        return self.landed[name]

    def __getitem__(self, name):
        return self.landed[name]


def _ffn_gate_up(h, w_gt, w_ut, *, name, tm=512, tn=1408):
    s, k = h.shape
    n = w_gt.shape[0]
    tm, tn = min(tm, s), _tile(n, tn)
    dims = (((1,), (1,)), ((), ()))

    def body(h_ref, wg_ref, wu_ref, g_ref, u_ref, a_ref):
        hb = h_ref[...].astype(BF16)
        gate = lax.dot_general(hb, wg_ref[...], dims, preferred_element_type=F32)
        up = lax.dot_general(hb, wu_ref[...], dims, preferred_element_type=F32)
        g_ref[...] = gate.astype(BF16)
        u_ref[...] = up.astype(BF16)
        a_ref[...] = _f_swiglu(gate, up)

    w_spec = pl.BlockSpec((tn, k), lambda i, j: (j, 0))
    o_spec = pl.BlockSpec((tm, tn), lambda i, j: (i, j))
    return pl.pallas_call(
        body, name=name, grid=(s // tm, n // tn), in_specs=[pl.BlockSpec((tm, k), lambda i, j: (i, 0)), w_spec, w_spec],
        out_specs=[o_spec] * 3, out_shape=[jax.ShapeDtypeStruct((s, n), BF16)] * 3,
        compiler_params=_params(("parallel", "parallel")),
    )(h, w_gt, w_ut)


def _ffn_dgate_dup(dx_out, w_d, gate, up, *, name, tm=512, tn=1408, deps=()):
    s, k = dx_out.shape
    n = w_d.shape[0]
    tm, tn = min(tm, s), _tile(n, tn)
    deps = [d for d in deps if d is not None]

    def body(dx_ref, wd_ref, g_ref, u_ref, *refs):
        dg_ref, du_ref = refs[len(deps):]
        dact = 0.5 * lax.dot_general(dx_ref[...].astype(BF16), wd_ref[...], (((1,), (1,)), ((), ())),
                                     preferred_element_type=F32)
        _, vjp = jax.vjp(_f_swiglu, g_ref[...].astype(F32), u_ref[...].astype(F32))
        dgate, dup = vjp(dact.astype(BF16))
        dg_ref[...] = dgate.astype(BF16)
        du_ref[...] = dup.astype(BF16)

    o_spec = pl.BlockSpec((tm, tn), lambda i, j: (i, j))
    return pl.pallas_call(
        body, name=name, grid=(s // tm, n // tn),
        in_specs=[pl.BlockSpec((tm, k), lambda i, j: (i, 0)), pl.BlockSpec((tn, k), lambda i, j: (j, 0)), o_spec, o_spec]
        + [pl.BlockSpec(d.shape, lambda i, j: (0, 0)) for d in deps],
        out_specs=[o_spec] * 2, out_shape=[jax.ShapeDtypeStruct((s, n), BF16)] * 2,
        compiler_params=_params(("parallel", "parallel")),
    )(dx_out, w_d, gate, up, *deps)


def _ffn_fwd(x, g, wc, tag, deps=()):
    h = _rowwise(_f_norm, [x], [g], [(D, BF16)], name=tag + "_norm", deps=deps)[0]
    gate, up, act = _ffn_gate_up(h, wc.get(tag + '_w_gate', h), wc[tag + '_w_up'], name=tag + "_gate_up")
    x_out = _mm(act, wc.get(tag + '_w_down', act), res=x, scale=0.5, name=tag + "_down")
    return x_out, (h, gate, up, act)


def _ffn_bwd(x, g, wc, saved, dx_out, tag, send):
    h, gate, up, act = saved
    w_gt, w_ut, w_d = (wc.get(tag + n, h) for n in ('_w_gate', '_w_up', '_w_down'))
    d_d = _mm(act, dx_out, ta=True, scale=0.5, out_dtype=GRAD_DTYPE, name=tag + "_dwdown")
    token = send({tag + '_w_down': d_d})
    dgate, dup = _ffn_dgate_dup(dx_out, w_d, gate, up, name=tag + "_dgate_dup", deps=[token])
    d_gt = _mm(dgate, h, ta=True, out_dtype=GRAD_DTYPE, name=tag + "_dwgate")
    d_ut = _mm(dup, h, ta=True, out_dtype=GRAD_DTYPE, name=tag + "_dwup")
    token = send({tag + '_w_gate': d_gt, tag + '_w_up': d_ut})
    dh = _mm(dgate, w_gt, name=tag + "_dh_gate")
    dh = _mm(dup, w_ut, res=dh, out_dtype=BF16, name=tag + "_dh_up")
    dx, dg = _rowwise_bwd(_f_norm, [x], [g], [dh], row_grads={0: F32}, const_grads=[0], adds={0: dx_out},
                          name=tag + "_norm_bwd", deps=[token])
    return dx, dg


def _local_step(x, mem, cos, sin, target, wc, ws, send, deps=()):
    gs = {}

    x1, sv1 = _ffn_fwd(x, ws['ffn1_norm'], wc, "ffn1", deps=deps)

    h2 = _rowwise(_f_norm, [x1], [ws['mix_norm']], [(D, BF16)], name="mix_norm")[0]
    w_in_raw, w_uq_raw = wc.get('w_in', h2), wc.get('mla_w_uq', h2)
    w_in_e = _expand_w_in(w_in_raw)
    w_uq_e = _expand_w_uq(w_uq_raw)
    proj = _mm(h2, w_in_e, tb=True, name="w_in")
    c_q, c_kv = _rowwise(_f_prep1, [proj], [ws['q_norm'], ws['kv_norm']], [(Q_RANK, BF16), (KV_RANK, BF16)], name="mla_prep1")
    qall = _mm(c_q, w_uq_e, tb=True, name="w_uq")
    kv = _mm(c_kv, wc['mla_w_ukv'], tb=True, name="w_ukv")
    kr = _rowwise(_f_kr, [proj], [], [(2 * LANES, F32)], name="mla_kr")[0]
    q, k, v = _prep2_fwd(qall, kv, kr, cos, sin, ws['qk_gq'], ws['qk_gk'])
    o_mla, lse = _attn_fwd(q, k, v)

    u = proj[:, Q_RANK + KV_RANK:Q_RANK + KV_RANK + SSM_W]
    u_p = _time_perm(u)
    disc_in = [ws['ssm_lr'], ws['ssm_li'], ws['ssm_ldt'], ws['ssm_br'], ws['ssm_bi']]
    ar16, ai16, bbr, bbi = _rowwise(_f_disc, disc_in, [], [(SSM_P, F32)] * 4, name="s5_disc")
    a_r = ar16[::SSM_GRP].reshape(1, SSM_N)
    a_i = ai16[::SSM_GRP].reshape(1, SSM_N)
    bblk_r, bblk_i = _block_diag_b(bbr).astype(BF16), _block_diag_b(bbi).astype(BF16)
    cblk_r, cblk_i = _block_diag_c(ws['ssm_cr']).astype(BF16), _block_diag_c(-ws['ssm_ci']).astype(BF16)
    xr, xi, yc = _s5_scan(u_p, bblk_r, bblk_i, a_r, a_i, reverse=False, tb=False, readout=(cblk_r, cblk_i),
                          name="s5_scan_fwd")
    g_p = _rowwise(_f_s5_gelu, [yc, u_p], [ws['ssm_d']], [(SSM_W, F32)], name="s5_gelu")[0]
    z_p = _mm(g_p, wc['ssm_w_glu'], name="s5_glu")
    g_t, z_t = _time_perm(g_p, inverse=True), _time_perm(z_p, inverse=True)
    on_consts = [ws['ssm_b_glu'], ws['out_norm_mla'], ws['out_norm_ssm']]
    ycat = _rowwise(_f_outnorm, [o_mla, g_t, z_t], on_consts, [(D, BF16)], name="out_norm")[0]
    x2 = _mm(ycat, wc['w_o'], res=x1, name="w_o")

    hx = _rowwise(_f_norm, [x2], [ws['xattn_norm']], [(D, BF16)], name="xattn_norm")[0]
    xq = _mm(hx, wc['xattn_w_q'], name="xattn_q")
    mn = _rowwise(_f_norm, [mem], [ws['mem_norm']], [(D, BF16)], name="mem_norm")[0]
    kvm = _mm(mn, wc['xattn_w_kv'], name="xattn_kv")
    xkn, xv = _rowwise(_f_memk, [kvm], [ws['xattn_k_norm']], [(H * XH, BF16), (H * XH, BF16)], name="xattn_knorm")
    xo = _xattn_fwd(xq, xkn, xv, ws['xattn_q_norm'])
    x3 = _mm(xo, wc['xattn_w_o'], tb=True, res=x2, name="xattn_o")

    x4, sv2 = _ffn_fwd(x3, ws['ffn2_norm'], wc, "ffn2")

    def f_loss(yb, tb):
        err = yb - tb
        return err * (1.0 / D), jnp.broadcast_to(jnp.sum(jnp.sum(err * err, axis=1, keepdims=True), axis=0, keepdims=True) * (0.5 / D), (1, LANES))

    dx4, loss = _rowwise(f_loss, [x4, target], [], [(D, F32)], [(1, LANES)], name="loss")

    dx3, gs['ffn2_norm'] = _ffn_bwd(x3, ws['ffn2_norm'], wc, sv2, dx4, "ffn2", send)

    dxo = _mm(dx3, wc['xattn_w_o'], out_dtype=BF16, name="xattn_o_dx")
    send({'xattn_w_o': _mm(dx3, xo, ta=True, out_dtype=GRAD_DTYPE, name="xattn_o_dw")})
    dxq, dxkn, dxv, gs['xattn_q_norm'] = _xattn_bwd(xq, xkn, xv, ws['xattn_q_norm'], dxo)
    dkvm, gs['xattn_k_norm'] = _rowwise_bwd(_f_memk, [kvm], [ws['xattn_k_norm']], [dxkn, dxv], row_grads={0: BF16},
                                            const_grads=[0], name="xattn_knorm_bwd")
    send({'xattn_w_kv': _mm(mn, dkvm, ta=True, out_dtype=GRAD_DTYPE, name="xattn_kv_dw")})
    dmn = _mm(dkvm, wc['xattn_w_kv'], tb=True, out_dtype=BF16, name="xattn_kv_dx")
    gs['mem_norm'] = _rowwise_bwd(_f_norm, [mem], [ws['mem_norm']], [dmn], row_grads={}, const_grads=[0], name="mem_norm_bwd")[0]
    token = send({'xattn_w_q': _mm(hx, dxq, ta=True, out_dtype=GRAD_DTYPE, name="xattn_q_dw")})
    dhx = _mm(dxq, wc['xattn_w_q'], tb=True, out_dtype=BF16, name="xattn_q_dx")
    dx2, gs['xattn_norm'] = _rowwise_bwd(_f_norm, [x2], [ws['xattn_norm']], [dhx], row_grads={0: F32}, const_grads=[0],
                                         adds={0: dx3}, name="xattn_norm_bwd", deps=[token])

    dycat = _mm(dx2, wc['w_o'], tb=True, out_dtype=BF16, name="w_o_dx")
    send({'w_o': _mm(ycat, dx2, ta=True, out_dtype=GRAD_DTYPE, name="w_o_dw")})
    do_mla, dg_t, dz_t, gs['ssm_b_glu'], gs['out_norm_mla'], gs['out_norm_ssm'] = _rowwise_bwd(
        _f_outnorm, [o_mla, g_t, z_t], on_consts, [dycat], row_grads={0: F32, 1: F32, 2: BF16}, const_grads=[0, 1, 2],
        name="out_norm_bwd")

    dz_p, dg_p = _time_perm(dz_t), _time_perm(dg_t)
    send({'ssm_w_glu': _mm(g_p, dz_p, ta=True, out_dtype=GRAD_DTYPE, name="s5_glu_dw")})
    dg_p = _mm(dz_p, wc['ssm_w_glu'], tb=True, res=dg_p, name="s5_glu_dx")
    dyc, du_d, gs['ssm_d'] = _rowwise_bwd(_f_s5_gelu, [yc, u_p], [ws['ssm_d']], [dg_p], row_grads={0: BF16, 1: F32},
                                          const_grads=[0], name="s5_gelu_bwd")
    lam_r, lam_i = _s5_scan(dyc, cblk_r, cblk_i, a_r, -a_i, reverse=True, tb=True, name="s5_scan_bwd")
    du_p, d_bblk_r, d_bblk_i, d_cblk_r, d_cblk_i, d_ar, d_ai = _s5_grads(lam_r, lam_i, xr, xi, u_p, dyc, du_d,
                                                                        bblk_r, bblk_i)
    du = _time_perm(du_p, inverse=True)
    gs['ssm_cr'] = jax.linear_transpose(_block_diag_c, ws['ssm_cr'])(d_cblk_r)[0]
    gs['ssm_ci'] = -jax.linear_transpose(_block_diag_c, ws['ssm_ci'])(d_cblk_i)[0]
    d_bbr = jax.linear_transpose(_block_diag_b, bbr)(d_bblk_r)[0]
    d_bbi = jax.linear_transpose(_block_diag_b, bbi)(d_bblk_i)[0]
    d_ar16 = jnp.zeros((SSM_G * SSM_GRP, SSM_P), F32).at[::SSM_GRP].set(d_ar.reshape(SSM_G, SSM_P))
    d_ai16 = jnp.zeros((SSM_G * SSM_GRP, SSM_P), F32).at[::SSM_GRP].set(d_ai.reshape(SSM_G, SSM_P))
    gs['ssm_lr'], gs['ssm_li'], gs['ssm_ldt'], gs['ssm_br'], gs['ssm_bi'] = _rowwise_bwd(
        _f_disc, disc_in, [], [d_ar16, d_ai16, d_bbr, d_bbi], row_grads={i: F32 for i in range(5)}, const_grads=[],
        name="s5_disc_bwd")

    delta, do_b = _rowwise(_f_delta, [do_mla, o_mla], [], [(H * LANES, F32), (H * VD, BF16)], name="mla_delta")
    dq, dk, dv = _attn_bwd(q, k, v, do_b, lse, delta)
    dqall, dkv, dkr, gs['qk_gq'], gs['qk_gk'] = _prep2_bwd(qall, kv, kr, cos, sin, ws['qk_gq'], ws['qk_gk'], dq, dk, dv)
    d_w_uq_e = _mm(dqall, c_q, ta=True, name="w_uq_dw")
    send({'mla_w_uq': jax.linear_transpose(_expand_w_uq, jax.ShapeDtypeStruct(w_uq_raw.shape, F32))(d_w_uq_e)[0]})
    dc_q = _mm(dqall, w_uq_e, out_dtype=BF16, name="w_uq_dx")
    send({'mla_w_ukv': _mm(dkv, c_kv, ta=True, out_dtype=GRAD_DTYPE, name="w_ukv_dw")})
    dc_kv = _mm(dkv, wc['mla_w_ukv'], out_dtype=BF16, name="w_ukv_dx")

    def f_prep1_bwd(pb, dcq, dckv, dub, dkrb, gq, gkv):
        _, vjp = jax.vjp(_f_prep1, pb[:, :Q_RANK + KV_RANK], gq, gkv)
        dpa, dgq, dgkv = vjp((dcq.astype(BF16), dckv.astype(BF16)))
        return jnp.concatenate([dpa, dub, dkrb], axis=-1), dgq, dgkv

    dproj, gs['q_norm'], gs['kv_norm'] = _rowwise(
        f_prep1_bwd, [proj, dc_q, dc_kv, du, dkr], [ws['q_norm'], ws['kv_norm']], [(IN_WP, BF16)],
        [(1, Q_RANK), (1, KV_RANK)], name="mla_prep1_bwd")
    d_w_in_e = _mm(dproj, h2, ta=True, name="w_in_dw")
    token = send({'w_in': jax.linear_transpose(_expand_w_in, jax.ShapeDtypeStruct(w_in_raw.shape, F32))(d_w_in_e)[0]})
    dh2 = _mm(dproj, w_in_e, out_dtype=BF16, name="w_in_dx")
    dx1, gs['mix_norm'] = _rowwise_bwd(_f_norm, [x1], [ws['mix_norm']], [dh2], row_grads={0: F32}, const_grads=[0],
                                       adds={0: dx2}, name="mix_norm_bwd", deps=[token])

    dx0, gs['ffn1_norm'] = _ffn_bwd(x, ws['ffn1_norm'], wc, sv1, dx1, "ffn1", send)
    return loss, dx0, gs


def _prep2_fwd(qall, kv, kr, cos, sin, gq, gk):
    return _rowwise(_f_prep2, [qall, kv, kr, cos, sin], [gq, gk], [(H * HQ, BF16), (H * HQ, BF16), (H * VD, BF16)],
                    ts=256, name="mla_prep2")


def _prep2_bwd(qall, kv, kr, cos, sin, gq, gk, dq, dk, dv):
    return _rowwise_bwd(_f_prep2, [qall, kv, kr, cos, sin], [gq, gk], [dq, dk, dv], row_grads={0: BF16, 1: BF16, 2: F32},
                        const_grads=[0, 1], ts=256, name="mla_prep2_bwd")


def _rope_tables(pos):
    half = ROPE // 2
    inv = ROPE_THETA ** (-jnp.arange(half, dtype=F32) / half)
    ang = pos.astype(F32)[:, None] * inv[None, :]
    z = jnp.zeros((pos.shape[0], LANES - ROPE), F32)
    cos, sin = jnp.cos(ang), jnp.sin(ang)
    return jnp.concatenate([cos, cos, z], axis=-1), jnp.concatenate([sin, sin, z], axis=-1)


def _small_layout(p):
    lr, li, ldt, br, bi = _layout_ssm_in(p['ssm_a_re'], p['ssm_a_im'], p['ssm_log_dt'], p['ssm_b_re'], p['ssm_b_im'])
    return {
        'ffn1_norm': p['ffn1_norm'].reshape(1, D), 'mix_norm': p['mix_norm'].reshape(1, D),
        'q_norm': p['mla_q_norm'].reshape(1, Q_RANK), 'kv_norm': p['mla_kv_norm'].reshape(1, KV_RANK),
        'qk_gq': _layout_qk_gain(p['mla_qk_norm_q']), 'qk_gk': _layout_qk_gain(p['mla_qk_norm_k']),
        'ssm_lr': lr, 'ssm_li': li, 'ssm_ldt': ldt, 'ssm_br': br, 'ssm_bi': bi,
        'ssm_cr': p['ssm_c_re'], 'ssm_ci': p['ssm_c_im'], 'ssm_d': p['ssm_d'].reshape(1, SSM_W),
        'ssm_b_glu': p['ssm_b_glu'].reshape(1, SSM_W),
        'out_norm_mla': p['out_norm_mla'].reshape(1, SSM_W), 'out_norm_ssm': p['out_norm_ssm'].reshape(1, SSM_W),
        'xattn_norm': p['xattn_norm'].reshape(1, D), 'mem_norm': p['mem_norm'].reshape(1, D),
        'xattn_q_norm': p['xattn_q_norm'].reshape(1, XH), 'xattn_k_norm': p['xattn_k_norm'].reshape(1, XH),
        'ffn2_norm': p['ffn2_norm'].reshape(1, D),
    }


def _pack(arrs, rows):
    flat = jnp.concatenate([a.reshape(-1) for a in arrs])
    return jnp.pad(flat, (0, rows * D - flat.shape[0])).reshape(rows, D)


def _unpack(flat, shapes):
    flat = flat.reshape(-1)
    out, off = [], 0
    for sh in shapes:
        n = int(np.prod(sh))
        out.append(flat[off:off + n].reshape(sh))
        off += n
    return out


def kernel(x, mem, positions, ffn1_norm, ffn1_w_gate, ffn1_w_up, ffn1_w_down, mix_norm, w_in, mla_q_norm, mla_w_uq, mla_kv_norm, mla_w_ukv, mla_qk_norm_q, mla_qk_norm_k, ssm_a_re, ssm_a_im, ssm_log_dt, ssm_b_re, ssm_b_im, ssm_c_re, ssm_c_im, ssm_d, ssm_w_glu, ssm_b_glu, out_norm_mla, out_norm_ssm, w_o, xattn_norm, mem_norm, xattn_w_q, xattn_w_kv, xattn_q_norm, xattn_k_norm, xattn_w_o, ffn2_norm, ffn2_w_gate, ffn2_w_up, ffn2_w_down, loss_target, m_ffn1_norm, m_ffn1_w_gate, m_ffn1_w_up, m_ffn1_w_down, m_mix_norm, m_w_in, m_mla_q_norm, m_mla_w_uq, m_mla_kv_norm, m_mla_w_ukv, m_mla_qk_norm_q, m_mla_qk_norm_k, m_ssm_a_re, m_ssm_a_im, m_ssm_log_dt, m_ssm_b_re, m_ssm_b_im, m_ssm_c_re, m_ssm_c_im, m_ssm_d, m_ssm_w_glu, m_ssm_b_glu, m_out_norm_mla, m_out_norm_ssm, m_w_o, m_xattn_norm, m_mem_norm, m_xattn_w_q, m_xattn_w_kv, m_xattn_q_norm, m_xattn_k_norm, m_xattn_w_o, m_ffn2_norm, m_ffn2_w_gate, m_ffn2_w_up, m_ffn2_w_down, v_ffn1_norm, v_ffn1_w_gate, v_ffn1_w_up, v_ffn1_w_down, v_mix_norm, v_w_in, v_mla_q_norm, v_mla_w_uq, v_mla_kv_norm, v_mla_w_ukv, v_mla_qk_norm_q, v_mla_qk_norm_k, v_ssm_a_re, v_ssm_a_im, v_ssm_log_dt, v_ssm_b_re, v_ssm_b_im, v_ssm_c_re, v_ssm_c_im, v_ssm_d, v_ssm_w_glu, v_ssm_b_glu, v_out_norm_mla, v_out_norm_ssm, v_w_o, v_xattn_norm, v_mem_norm, v_xattn_w_q, v_xattn_w_kv, v_xattn_q_norm, v_xattn_k_norm, v_xattn_w_o, v_ffn2_norm, v_ffn2_w_gate, v_ffn2_w_up, v_ffn2_w_down):
    args = dict(locals())
    w = {n: args[n] for n in WEIGHTS}
    mom = {n: args['m_' + n] for n in WEIGHTS}
    var = {n: args['v_' + n] for n in WEIGHTS}
    return _step(x, mem, positions, loss_target, w, mom, var)


GATHER_GROUPS = [('ffn1_gu', ['ffn1_w_gate', 'ffn1_w_up']), ('ffn1_down', ['ffn1_w_down']),
                 ('mix', ['w_in', 'mla_w_uq', 'mla_w_ukv', 'ssm_w_glu', 'w_o', 'xattn_w_q', 'xattn_w_kv', 'xattn_w_o']),
                 ('ffn2', ['ffn2_w_gate', 'ffn2_w_up', 'ffn2_w_down'])]
SCATTER_GROUPS = [('ffn2_down', ['ffn2_w_down']), ('ffn2_gu', ['ffn2_w_gate', 'ffn2_w_up']),
                  ('xattn', ['xattn_w_o', 'xattn_w_kv', 'xattn_w_q']),
                  ('mix', ['w_o', 'ssm_w_glu', 'mla_w_uq', 'mla_w_ukv', 'w_in']),
                  ('ffn1_down', ['ffn1_w_down']), ('ffn1_gu', ['ffn1_w_gate', 'ffn1_w_up'])]


def _step(x, mem, positions, loss_target, w, mom, var):
    blocks = {n: _to_exchange_layout(n, w[n][0]).astype(BF16) for n in SHARDED}
    gathers, token = [], None
    for tag, names in GATHER_GROUPS:
        ex = _Exchange([blocks[n] for n in names], [blocks[n].shape[0] for n in names], gather=True,
                       name="gather_" + tag, after=token)
        gathers.append((names, ex))
        token = ex.token
    me = _my_slot()
    wc = _Weights(gathers, me=me)

    rows = {n: (blocks[n].shape[0], blocks[n].shape[0]) for n in SHARDED}
    ready, scatters = {}, []

    def send(grads):
        ready.update({n: g.astype(GRAD_DTYPE) for n, g in grads.items()})
        for tag, names in SCATTER_GROUPS:
            if all(n in ready for n in names) and not any(t == tag for t, _, _ in scatters):
                ex = _Exchange([ready[n] for n in names], [rows[n] for n in names], gather=False, name="scatter_" + tag)
                scatters.append((tag, names, ex))
                return ex.token
        return None

    small = {n: w[n][0] for n in SMALL}
    ws = _small_layout(small)
    cos, sin = _rope_tables(positions[0])
    loss, dx, gs = _local_step(x[0], mem[0], cos, sin, loss_target[0], wc, ws, send, deps=[token])

    g_small = jax.linear_transpose(_small_layout, {n: jax.ShapeDtypeStruct(small[n].shape, F32) for n in SMALL})(gs)[0]
    small_shapes = [small[n].shape for n in SMALL]
    n_small = sum(int(np.prod(sh)) for sh in small_shapes) + 1
    rows_small = -(-n_small // (8 * D)) * 8
    small_pack = _pack([g_small[n] for n in SMALL] + [loss[0, :1]], rows_small)
    small_ex = _Exchange([small_pack], [(None, rows_small)], gather=False, name="scatter_small")

    out, after = {}, dx
    for _, names, ex in scatters:
        for n, sent, p in zip(names, *ex.wait(after)):
            r = w[n][0].shape[SHARD_AXIS[n]]
            if SHARD_AXIS[n] == 0:
                out[n] = _sum_adamw(me, sent, rows[n][0], p, r, w[n][0], mom[n][0], var[n][0], name="adamw_" + n)
            else:
                g = _sum_adamw(me, sent, rows[n][0], p, r, name="sum_" + n)[0].T
                out[n] = [g] + _adamw(g, w[n][0], mom[n][0], var[n][0], name="adamw_" + n)
        after = out[names[-1]][1]
    state = [_pack([t[n][0] for n in SMALL], rows_small) for t in (w, mom, var)]
    sent, p = small_ex.wait(after)
    small_out = _sum_adamw(me, sent[0], None, p[0], rows_small, *state, name="adamw_small")
    loss_total = small_out[0].reshape(-1)[n_small - 1]
    for n, vals in zip(SMALL, zip(*[_unpack(flat, small_shapes) for flat in small_out])):
        out[n] = vals
    outs = [out[n][i][None] for i in range(4) for n in WEIGHTS]
    return (loss_total, dx[None], *outs)
```

```python
import math

import jax
import jax.numpy as jnp
import numpy as np
from jax import lax
from jax.experimental import pallas as pl
from jax.experimental.pallas import tpu as pltpu

F32 = jnp.float32
BF16 = jnp.bfloat16

N_DEV = 8
D = 1024
D_FF = 2752
D_FFP = 2816
MEM_LEN = 256
H = 4
Q_RANK, KV_RANK, NOPE, ROPE, VD = 384, 256, 128, 64, 128
QK = NOPE + ROPE
HQ = 2 * 128
SSM_W, SSM_G, SSM_GRP, SSM_P = 512, 32, 16, 64
SSM_N = SSM_G * SSM_P
SSM_PACK = 8
IN_W = 1216
IN_WP = 1408
XH = 128
EPS = 1e-6
LN2 = math.log(2.0)
ROPE_THETA = 10000.0
SCAN_CHUNKS = 8
SCAN_UNROLL = 8
ADAM_LR, ADAM_B1, ADAM_B2, ADAM_EPS, ADAM_WD, ADAM_STEP = 0.001, 0.9, 0.999, 1e-08, 0.01, 10

VMEM_LIMIT = 56 * 1024 * 1024
ACC_BYTES = 6 * 1024 * 1024
LANES = 128
BF16_ROWS = 16
GRAD_DTYPE = BF16
FF_SHARD = D_FF // N_DEV
FF_SHARD_P = 352
IN_SHARD = IN_W // N_DEV
IN_SHARD_P = 160

WEIGHTS = ['ffn1_norm', 'ffn1_w_gate', 'ffn1_w_up', 'ffn1_w_down', 'mix_norm', 'w_in', 'mla_q_norm', 'mla_w_uq',
           'mla_kv_norm', 'mla_w_ukv', 'mla_qk_norm_q', 'mla_qk_norm_k', 'ssm_a_re', 'ssm_a_im', 'ssm_log_dt',
           'ssm_b_re', 'ssm_b_im', 'ssm_c_re', 'ssm_c_im', 'ssm_d', 'ssm_w_glu', 'ssm_b_glu', 'out_norm_mla',
           'out_norm_ssm', 'w_o', 'xattn_norm', 'mem_norm', 'xattn_w_q', 'xattn_w_kv', 'xattn_q_norm',
           'xattn_k_norm', 'xattn_w_o', 'ffn2_norm', 'ffn2_w_gate', 'ffn2_w_up', 'ffn2_w_down']
SHARD_AXIS = {'ffn1_w_gate': 1, 'ffn1_w_up': 1, 'ffn1_w_down': 0, 'w_in': 1, 'mla_w_uq': 1, 'mla_w_ukv': 1,
              'ssm_w_glu': 0, 'w_o': 0, 'xattn_w_q': 0, 'xattn_w_kv': 0, 'xattn_w_o': 1,
              'ffn2_w_gate': 1, 'ffn2_w_up': 1, 'ffn2_w_down': 0}
SHARDED = [n for n in WEIGHTS if n in SHARD_AXIS]
SMALL = [n for n in WEIGHTS if n not in SHARD_AXIS]


def _params(sem=None):
    return pltpu.CompilerParams(dimension_semantics=sem, vmem_limit_bytes=VMEM_LIMIT)


def _tile(n, cap):
    if n <= cap:
        return n
    best = n
    for t in range(LANES, cap + 1, LANES):
        if n % t == 0:
            best = t
    return best


def _mm(a, b, *, ta=False, tb=False, out_dtype=F32, res=None, scale=1.0, name, tm_cap=512, tn_cap=1408, tk_cap=2816):
    m, k = (a.shape[1], a.shape[0]) if ta else a.shape
    k2, n = (b.shape[1], b.shape[0]) if tb else b.shape
    assert k == k2, (a.shape, b.shape, ta, tb)
    if ta:
        tk_cap = min(tk_cap, 512)
        tm_cap = 1408
    tm, tn, tk = _tile(m, tm_cap), _tile(n, tn_cap), _tile(k, tk_cap)
    if tm * tn * 4 > ACC_BYTES:
        tn = _tile(n, max(LANES, ACC_BYTES // (4 * tm) // LANES * LANES))
    nk = k // tk
    dims = (((0 if ta else 1,), (1 if tb else 0,)), ((), ()))
    has_res = res is not None

    def body(*refs):
        if has_res:
            a_ref, b_ref, r_ref, o_ref, acc_ref = refs
        else:
            a_ref, b_ref, o_ref, acc_ref = refs
        kk = pl.program_id(2)

        @pl.when(kk == 0)
        def _():
            acc_ref[...] = jnp.zeros_like(acc_ref)

        acc_ref[...] += lax.dot_general(a_ref[...].astype(BF16), b_ref[...].astype(BF16), dims,
                                        preferred_element_type=F32)

        @pl.when(kk == nk - 1)
        def _():
            out = acc_ref[...]
            if scale != 1.0:
                out = out * scale
            if has_res:
                out = out + r_ref[...].astype(F32)
            o_ref[...] = out.astype(o_ref.dtype)

    a_spec = pl.BlockSpec((tk, tm), lambda i, j, kk: (kk, i)) if ta else pl.BlockSpec((tm, tk), lambda i, j, kk: (i, kk))
    b_spec = pl.BlockSpec((tn, tk), lambda i, j, kk: (j, kk)) if tb else pl.BlockSpec((tk, tn), lambda i, j, kk: (kk, j))
    o_spec = pl.BlockSpec((tm, tn), lambda i, j, kk: (i, j))
    in_specs = [a_spec, b_spec] + ([o_spec] if has_res else [])
    args = (a, b) + ((res,) if has_res else ())
    return pl.pallas_call(
        body, name=name, grid=(m // tm, n // tn, nk), in_specs=in_specs, out_specs=o_spec,
        out_shape=jax.ShapeDtypeStruct((m, n), out_dtype), scratch_shapes=[pltpu.VMEM((tm, tn), F32)],
        compiler_params=_params(("parallel", "parallel", "arbitrary")),
    )(*args)


def _mm_grouped(a, b, *, tb=False, res=None, out_dtype=F32, name, tm=512):
    s = a.shape[0]
    g = b.shape[0]
    nb, ka = (b.shape[1], b.shape[2]) if tb else (b.shape[2], b.shape[1])
    assert a.shape[1] == g * ka
    tm = min(tm, s)
    dims = (((1,), (1 if tb else 0,)), ((), ()))
    has_res = res is not None

    def body(*refs):
        if has_res:
            a_ref, b_ref, r_ref, o_ref = refs
        else:
            a_ref, b_ref, o_ref = refs
        out = lax.dot_general(a_ref[...].astype(BF16), b_ref[...].astype(BF16), dims, preferred_element_type=F32)
        if has_res:
            out = out + r_ref[...].astype(F32)
        o_ref[...] = out.astype(o_ref.dtype)

    o_spec = pl.BlockSpec((tm, nb), lambda i, j: (i, j))
    in_specs = [pl.BlockSpec((tm, ka), lambda i, j: (i, j)), pl.BlockSpec((None,) + b.shape[1:], lambda i, j: (j, 0, 0))]
    return pl.pallas_call(
        body, name=name, grid=(s // tm, g), in_specs=in_specs + ([o_spec] if has_res else []), out_specs=o_spec,
        out_shape=jax.ShapeDtypeStruct((s, g * nb), out_dtype), compiler_params=_params(("parallel", "parallel")),
    )(a, b, *((res,) if has_res else ()))


def _mm_grouped_tn(a, b, *, ka, kb, name, tk=512):
    s = a.shape[0]
    g = a.shape[1] // ka
    assert b.shape[1] == g * kb
    tk = min(tk, s)
    nk = s // tk

    def body(a_ref, b_ref, o_ref):
        part = lax.dot_general(a_ref[...].astype(BF16), b_ref[...].astype(BF16), (((0,), (0,)), ((), ())),
                               preferred_element_type=F32)

        @pl.when(pl.program_id(1) == 0)
        def _():
            o_ref[...] = part

        @pl.when(pl.program_id(1) > 0)
        def _():
            o_ref[...] += part

    return pl.pallas_call(
        body, name=name, grid=(g, nk),
        in_specs=[pl.BlockSpec((tk, ka), lambda j, kk: (kk, j)), pl.BlockSpec((tk, kb), lambda j, kk: (kk, j))],
        out_specs=pl.BlockSpec((None, ka, kb), lambda j, kk: (j, 0, 0)),
        out_shape=jax.ShapeDtypeStruct((g, ka, kb), F32), compiler_params=_params(("parallel", "arbitrary")),
    )(a, b)


def _rowwise(fn, rows, consts, outs, accs=(), *, ts=512, name, deps=()):
    s = rows[0].shape[0]
    ts = min(ts, s)
    assert s % ts == 0
    n_rows, n_consts, n_outs = len(rows), len(consts), len(outs)
    deps = [d for d in deps if d is not None]
    consts = list(consts) + deps

    def body(*refs):
        ins = [r[...] for r in refs[:n_rows + n_consts]]
        res = fn(*ins)
        res = tuple(res) if isinstance(res, (tuple, list)) else (res,)
        out_refs = refs[n_rows + len(consts):]
        for o_ref, val in zip(out_refs[:n_outs], res[:n_outs]):
            o_ref[...] = val.astype(o_ref.dtype)
        if accs:
            first = pl.program_id(0) == 0

            @pl.when(first)
            def _():
                for a_ref, val in zip(out_refs[n_outs:], res[n_outs:]):
                    a_ref[...] = val.astype(F32)

            @pl.when(jnp.logical_not(first))
            def _():
                for a_ref, val in zip(out_refs[n_outs:], res[n_outs:]):
                    a_ref[...] += val.astype(F32)

    in_specs = [pl.BlockSpec((ts, r.shape[1]), lambda i: (i, 0)) for r in rows]
    in_specs += [pl.BlockSpec(c.shape, lambda i: (0, 0)) for c in consts]
    out_specs = [pl.BlockSpec((ts, w), lambda i: (i, 0)) for w, _ in outs]
    out_specs += [pl.BlockSpec(tuple(sh), lambda i: (0, 0)) for sh in accs]
    out_shape = [jax.ShapeDtypeStruct((s, w), dt) for w, dt in outs]
    out_shape += [jax.ShapeDtypeStruct(tuple(sh), F32) for sh in accs]
    res = pl.pallas_call(
        body, name=name, grid=(s // ts,), in_specs=in_specs, out_specs=out_specs, out_shape=out_shape,
        compiler_params=_params(("arbitrary",)),
    )(*rows, *consts)
    return res


def _rowwise_bwd(f, rows, consts, cts, *, row_grads, const_grads, adds=None, ts=512, name, deps=()):
    adds = adds or {}
    n_rows, n_consts, n_cts = len(rows), len(consts), len(cts)
    add_keys = sorted(adds)
    rg = sorted(row_grads)
    cg = sorted(const_grads)

    def fn(*args):
        r = args[:n_rows]
        c = args[n_rows:n_rows + n_consts]
        ct = args[n_rows + n_consts:n_rows + n_consts + n_cts]
        extra = args[n_rows + n_consts + n_cts:]
        outs, vjp = jax.vjp(f, *r, *c)
        outs = tuple(outs) if isinstance(outs, (tuple, list)) else (outs,)
        cot = tuple(g.astype(o.dtype) for g, o in zip(ct, outs))
        grads = vjp(cot if len(cot) > 1 else cot[0])
        res = []
        for i in rg:
            g = grads[i].astype(F32)
            if i in adds:
                g = g + extra[add_keys.index(i)].astype(F32)
            res.append(g)
        for i in cg:
            res.append(grads[n_rows + i])
        return tuple(res)

    rows_all = list(rows) + list(cts) + [adds[i] for i in add_keys]
    def fn2(*args):
        nr = len(rows_all)
        rr, cc = args[:nr], args[nr:]
        return fn(*rr[:n_rows], *cc, *rr[n_rows:])

    outs = [(rows[i].shape[1], row_grads[i]) for i in rg]
    accs = [consts[i].shape for i in cg]
    return _rowwise(fn2, rows_all, list(consts), outs, accs, ts=ts, name=name, deps=deps)


def _rms(x, g):
    xf = x.astype(F32)
    return xf * lax.rsqrt(jnp.mean(xf * xf, axis=-1, keepdims=True) + EPS) * g.astype(F32)


def _sigmoid(x):
    return 1.0 / (1.0 + jnp.exp(-x))


def _f_norm(x, g):
    return _rms(x, g).astype(BF16)


def _f_swiglu(gate, up):
    gate, up = gate.astype(F32), up.astype(F32)
    return (gate * _sigmoid(gate) * up).astype(BF16)


def _f_prep1(proj, gq, gkv):
    return _rms(proj[:, :Q_RANK], gq).astype(BF16), _rms(proj[:, Q_RANK:Q_RANK + KV_RANK], gkv).astype(BF16)


def _f_kr(proj):
    return (proj[:, Q_RANK + KV_RANK + SSM_W:],)


def _f_prep2(qall, kv, kr2, cos, sin, gq, gk):
    kr, krs = kr2[:, :LANES].astype(F32), kr2[:, LANES:].astype(F32)
    k_rot = kr * gk[1:2] * cos + krs * gk[2:3] * sin
    k_ss = jnp.sum(kr * kr, axis=-1, keepdims=True)
    q_scale = QK ** -0.5 / LN2
    qs, ks, vs = [], [], []
    for h in range(H):
        qn = qall[:, h * LANES:(h + 1) * LANES].astype(F32)
        qr = qall[:, (H + h) * LANES:(H + h + 1) * LANES].astype(F32)
        qrs = qall[:, (2 * H + h) * LANES:(2 * H + h + 1) * LANES].astype(F32)
        rstd = lax.rsqrt((jnp.sum(qn * qn, axis=-1, keepdims=True) + jnp.sum(qr * qr, axis=-1, keepdims=True)) / QK + EPS)
        rstd = rstd * q_scale
        qs += [qn * gq[0:1] * rstd, (qr * gq[1:2] * cos + qrs * gq[2:3] * sin) * rstd]
        kn = kv[:, 2 * h * LANES:(2 * h + 1) * LANES].astype(F32)
        rstd_k = lax.rsqrt((jnp.sum(kn * kn, axis=-1, keepdims=True) + k_ss) / QK + EPS)
        ks += [kn * gk[0:1] * rstd_k, k_rot * rstd_k]
        vs.append(kv[:, (2 * h + 1) * LANES:(2 * h + 2) * LANES])
    return (jnp.concatenate(qs, axis=-1).astype(BF16), jnp.concatenate(ks, axis=-1).astype(BF16),
            jnp.concatenate(vs, axis=-1).astype(BF16))


def _gelu(x):
    return 0.5 * x * (1.0 + jnp.tanh(math.sqrt(2.0 / math.pi) * (x + 0.044715 * (x * x * x))))


def _f_s5_gelu(yc, u, d):
    return _gelu(yc.astype(F32) + d * u.astype(F32))


def _f_outnorm(o_mla, g, z, b_glu, g_om, g_os):
    y_ssm = g * _sigmoid(z + b_glu)
    return jnp.concatenate([_rms(o_mla, g_om), _rms(y_ssm, g_os)], axis=-1).astype(BF16)


def _f_memk(kvm, gk):
    ks = [_rms(kvm[:, h * XH:(h + 1) * XH], gk) for h in range(H)]
    return jnp.concatenate(ks, axis=-1).astype(BF16), kvm[:, H * XH:].astype(BF16)


def _f_disc(lr, li, log_dt, br, bi):
    dt = jnp.exp(log_dt)
    decay = jnp.exp(lr * dt)
    ar = decay * jnp.cos(li * dt)
    ai = decay * jnp.sin(li * dt)
    den = lr * lr + li * li
    nr = ar - 1.0
    coef_r = (nr * lr + ai * li) / den
    coef_i = (ai * lr - nr * li) / den
    return ar, ai, coef_r * br - coef_i * bi, coef_r * bi + coef_i * br


def _causal_mask(i, j, tq, tk):
    qpos = i * tq + lax.broadcasted_iota(jnp.int32, (tq, tk), 0)
    kpos = j * tk + lax.broadcasted_iota(jnp.int32, (tq, tk), 1)
    return qpos >= kpos


def _attn_fwd(q, k, v, *, t=512):
    s = q.shape[0]
    t = min(t, s)
    nb = s // t

    def body(q_ref, k_ref, v_ref, o_ref, lse_ref, m_sc, l_sc, acc_sc):
        i, j = pl.program_id(1), pl.program_id(2)

        @pl.when(j == 0)
        def _():
            m_sc[...] = jnp.full_like(m_sc, -jnp.inf)
            l_sc[...] = jnp.zeros_like(l_sc)
            acc_sc[...] = jnp.zeros_like(acc_sc)

        def block(diagonal):
            sc = lax.dot_general(q_ref[...], k_ref[...], (((1,), (1,)), ((), ())), preferred_element_type=F32)
            if diagonal:
                sc = jnp.where(_causal_mask(i, j, t, t), sc, -jnp.inf)
            m_old = m_sc[...]
            m_new = jnp.maximum(m_old, jnp.max(sc, axis=-1, keepdims=True))
            p = jnp.exp2(sc - m_new)
            alpha = jnp.exp2(m_old - m_new)
            l_sc[...] = alpha * l_sc[...] + jnp.sum(p, axis=-1, keepdims=True)
            acc_sc[...] = alpha * acc_sc[...] + jnp.dot(p.astype(BF16), v_ref[...], preferred_element_type=F32)
            m_sc[...] = m_new

        pl.when(j < i)(lambda: block(False))

        @pl.when(j == i)
        def _():
            block(True)
            o_ref[...] = acc_sc[...] / l_sc[...]
            lse_ref[...] = jnp.broadcast_to(m_sc[...] + jnp.log2(l_sc[...]), lse_ref.shape)

    kv_map = lambda h, i, j: (jnp.minimum(j, i), h)
    return pl.pallas_call(
        body, name="mla_attn_fwd", grid=(H, nb, nb),
        in_specs=[pl.BlockSpec((t, HQ), lambda h, i, j: (i, h)), pl.BlockSpec((t, HQ), kv_map),
                  pl.BlockSpec((t, VD), kv_map)],
        out_specs=[pl.BlockSpec((t, VD), lambda h, i, j: (i, h)), pl.BlockSpec((t, LANES), lambda h, i, j: (i, h))],
        out_shape=[jax.ShapeDtypeStruct((s, H * VD), F32), jax.ShapeDtypeStruct((s, H * LANES), F32)],
        scratch_shapes=[pltpu.VMEM((t, 1), F32), pltpu.VMEM((t, 1), F32), pltpu.VMEM((t, VD), F32)],
        compiler_params=_params(("parallel", "parallel", "arbitrary")),
    )(q, k, v)


def _attn_probs(q_ref, k_ref, v_ref, do_ref, lse_ref, dl_ref, i, j, t, diagonal):
    sc = lax.dot_general(q_ref[...], k_ref[...], (((1,), (1,)), ((), ())), preferred_element_type=F32)
    p = jnp.exp2(sc - jnp.tile(lse_ref[...], (1, t // LANES)))
    if diagonal:
        p = jnp.where(_causal_mask(i, j, t, t), p, 0.0)
    dp = lax.dot_general(do_ref[...], v_ref[...], (((1,), (1,)), ((), ())), preferred_element_type=F32)
    ds = p * (dp - jnp.tile(dl_ref[...], (1, t // LANES)))
    return p, ds


def _attn_bwd(q, k, v, do, lse, delta, *, t=512):
    s = q.shape[0]
    t = min(t, s)
    nb = s // t

    def body(q_ref, k_ref, v_ref, do_ref, lse_ref, dl_ref, dq_ref, dk_ref, dv_ref, dk_sc, dv_sc):
        j, i = pl.program_id(1), pl.program_id(2)

        @pl.when(jnp.logical_and(i == 0, j == 0))
        def _():
            dq_ref[...] = jnp.zeros_like(dq_ref)

        @pl.when(i == 0)
        def _():
            dk_sc[...] = jnp.zeros_like(dk_sc)
            dv_sc[...] = jnp.zeros_like(dv_sc)

        def block(diagonal):
            p, ds = _attn_probs(q_ref, k_ref, v_ref, do_ref, lse_ref, dl_ref, i, j, t, diagonal)
            dsb = ds.astype(BF16)
            dv_sc[...] += lax.dot_general(p.astype(BF16), do_ref[...], (((0,), (0,)), ((), ())), preferred_element_type=F32)
            dk_sc[...] += lax.dot_general(dsb, q_ref[...], (((0,), (0,)), ((), ())), preferred_element_type=F32)
            rows = pl.ds(pl.multiple_of(i * t, t), t)
            dq_ref[rows, :] += jnp.dot(dsb, k_ref[...], preferred_element_type=F32)

        pl.when(i > j)(lambda: block(False))
        pl.when(i == j)(lambda: block(True))

        @pl.when(i == nb - 1)
        def _():
            dk_ref[...] = dk_sc[...] * LN2
            dv_ref[...] = dv_sc[...]

        @pl.when(jnp.logical_and(i == nb - 1, j == nb - 1))
        def _():
            dq_ref[...] = dq_ref[...] * LN2

    q_map = lambda h, j, i: (jnp.maximum(i, j), h)
    kv_map = lambda h, j, i: (j, h)
    dq, dk, dv = pl.pallas_call(
        body, name="mla_attn_bwd", grid=(H, nb, nb),
        in_specs=[pl.BlockSpec((t, HQ), q_map), pl.BlockSpec((t, HQ), kv_map), pl.BlockSpec((t, VD), kv_map),
                  pl.BlockSpec((t, VD), q_map), pl.BlockSpec((t, LANES), q_map), pl.BlockSpec((t, LANES), q_map)],
        out_specs=[pl.BlockSpec((s, HQ), lambda h, j, i: (0, h)), pl.BlockSpec((t, HQ), kv_map), pl.BlockSpec((t, VD), kv_map)],
        out_shape=[jax.ShapeDtypeStruct((s, H * HQ), F32), jax.ShapeDtypeStruct((s, H * HQ), F32),
                   jax.ShapeDtypeStruct((s, H * VD), F32)],
        scratch_shapes=[pltpu.VMEM((t, HQ), F32), pltpu.VMEM((t, VD), F32)],
        compiler_params=_params(("parallel", "arbitrary", "arbitrary")),
    )(q, k, v, do, lse, delta)
    return dq, dk, dv


def _f_delta(do, o):
    prod = do.astype(F32) * o.astype(F32)
    parts = [jnp.broadcast_to(jnp.sum(prod[:, h * VD:(h + 1) * VD], axis=-1, keepdims=True), (do.shape[0], LANES))
             for h in range(H)]
    return jnp.concatenate(parts, axis=-1), do.astype(BF16)


def _xattn_head(qh, kh, gq):
    qn = _rms(qh, gq) * (XH ** -0.5)
    sc = lax.dot_general(qn.astype(BF16), kh, (((1,), (1,)), ((), ())), preferred_element_type=F32)
    sc = sc - jnp.max(sc, axis=-1, keepdims=True)
    e = jnp.exp(sc)
    return qn, e / jnp.sum(e, axis=-1, keepdims=True)


def _xattn_fwd(q, kn, v, gq, *, ts=512):
    def fn(qb, knb, vb, g):
        outs = []
        for h in range(H):
            sl = slice(h * XH, (h + 1) * XH)
            _, p = _xattn_head(qb[:, sl], knb[:, sl], g)
            outs.append(jnp.dot(p.astype(BF16), vb[:, sl], preferred_element_type=F32))
        return (jnp.concatenate(outs, axis=-1),)

    return _rowwise(fn, [q], [kn, v, gq], [(H * XH, BF16)], ts=ts, name="xattn_fwd")[0]


def _xattn_bwd(q, kn, v, gq, do, *, ts=512):
    def fn(qb, dob, knb, vb, g):
        dqs, dks, dvs = [], [], []
        dg = jnp.zeros((1, XH), F32)
        for h in range(H):
            sl = slice(h * XH, (h + 1) * XH)
            qh, kh, vh, doh = qb[:, sl], knb[:, sl], vb[:, sl], dob[:, sl].astype(BF16)
            qn, p = _xattn_head(qh, kh, g)
            dp = lax.dot_general(doh, vh, (((1,), (1,)), ((), ())), preferred_element_type=F32)
            dvs.append(lax.dot_general(p.astype(BF16), doh, (((0,), (0,)), ((), ())), preferred_element_type=F32))
            ds = (p * (dp - jnp.sum(dp * p, axis=-1, keepdims=True))).astype(BF16)
            dqn = jnp.dot(ds, kh, preferred_element_type=F32)
            dks.append(lax.dot_general(ds, qn.astype(BF16), (((0,), (0,)), ((), ())), preferred_element_type=F32))
            _, vjp_n = jax.vjp(lambda a, b: _rms(a, b) * (XH ** -0.5), qh, g)
            dqh, dgh = vjp_n(dqn)
            dqs.append(dqh)
            dg = dg + dgh
        return (jnp.concatenate(dqs, axis=-1), jnp.concatenate(dks, axis=-1), jnp.concatenate(dvs, axis=-1), dg)

    return _rowwise(fn, [q, do], [kn, v, gq], [(H * XH, BF16)], [kn.shape, v.shape, gq.shape], ts=ts, name="xattn_bwd")


def _cmul(ar, ai, xr, xi):
    return ar * xr - ai * xi, ar * xi + ai * xr


def _scan_in_place(xr_ref, xi_ref, ar, ai, *, reverse):
    s, cw = xr_ref.shape
    c = SCAN_CHUNKS
    tt = s // c
    a_r = jnp.broadcast_to(ar, (c, cw))
    a_i = jnp.broadcast_to(ai, (c, cw))
    zero = jnp.zeros((c, cw), F32)

    def row(step):
        t = (tt - 1 - step) if reverse else step
        return pl.ds(pl.multiple_of(t * c, c), c)

    def local(step, carry):
        sr, si, qr, qi = carry
        r = row(step)
        nr, ni = _cmul(a_r, a_i, sr, si)
        nr, ni = nr + xr_ref[r, :], ni + xi_ref[r, :]
        xr_ref[r, :] = nr
        xi_ref[r, :] = ni
        return (nr, ni) + _cmul(a_r, a_i, qr, qi)

    end_r, end_i, pr, pi = lax.fori_loop(0, tt, local, (zero, zero, jnp.ones((c, cw), F32), zero), unroll=SCAN_UNROLL)

    rows_id = lax.broadcasted_iota(jnp.int32, (c, cw), 0)
    car_r, car_i = zero, zero
    cur_r, cur_i = jnp.zeros((1, cw), F32), jnp.zeros((1, cw), F32)
    order = range(c - 1, -1, -1) if reverse else range(c)
    for kk in order:
        car_r = jnp.where(rows_id == kk, cur_r, car_r)
        car_i = jnp.where(rows_id == kk, cur_i, car_i)
        nr, ni = _cmul(pr[0:1], pi[0:1], cur_r, cur_i)
        cur_r = nr + end_r[kk:kk + 1]
        cur_i = ni + end_i[kk:kk + 1]

    def fix(step, carry):
        qr, qi = _cmul(a_r, a_i, *carry)
        r = row(step)
        dr, di = _cmul(qr, qi, car_r, car_i)
        xr_ref[r, :] += dr
        xi_ref[r, :] += di
        return qr, qi

    lax.fori_loop(0, tt, fix, (jnp.ones((c, cw), F32), zero), unroll=SCAN_UNROLL)


S5_ROWS = 512


def _s5_scan(v, w_r, w_i, ar, ai, *, reverse, tb, readout=None, name):
    s = v.shape[0]
    g = w_r.shape[0]
    nv, ns = SSM_PACK * SSM_GRP, SSM_PACK * SSM_P
    rows = min(S5_ROWS, s)
    dims = (((1,), (1 if tb else 0,)), ((), ()))
    n_w = 2 if readout is None else 4

    def body(v_ref, ar_ref, ai_ref, *refs):
        w = [r[...] for r in refs[:n_w]]
        xr_ref, xi_ref = refs[n_w:n_w + 2]
        for r0 in range(0, s, rows):
            vb = v_ref[r0:r0 + rows, :].astype(BF16)
            xr_ref[r0:r0 + rows, :] = lax.dot_general(vb, w[0], dims, preferred_element_type=F32)
            xi_ref[r0:r0 + rows, :] = lax.dot_general(vb, w[1], dims, preferred_element_type=F32)
        _scan_in_place(xr_ref, xi_ref, ar_ref[...], ai_ref[...], reverse=reverse)
        if readout is not None:
            y_ref = refs[n_w + 2]
            for r0 in range(0, s, rows):
                y_ref[r0:r0 + rows, :] = (
                    jnp.dot(xr_ref[r0:r0 + rows, :].astype(BF16), w[2], preferred_element_type=F32)
                    + jnp.dot(xi_ref[r0:r0 + rows, :].astype(BF16), w[3], preferred_element_type=F32))

    col = lambda j: (0, j)
    w_spec = lambda a: pl.BlockSpec((None,) + a.shape[1:], lambda j: (j, 0, 0))
    weights = [w_r, w_i] + (list(readout) if readout is not None else [])
    out_specs = [pl.BlockSpec((s, ns), col)] * 2 + ([pl.BlockSpec((s, nv), col)] if readout is not None else [])
    out_shape = [jax.ShapeDtypeStruct((s, g * ns), F32)] * 2 + (
        [jax.ShapeDtypeStruct((s, g * nv), F32)] if readout is not None else [])
    return pl.pallas_call(
        body, name=name, grid=(g,),
        in_specs=[pl.BlockSpec((s, nv), col), pl.BlockSpec((1, ns), col), pl.BlockSpec((1, ns), col)] + [w_spec(a) for a in weights],
        out_specs=out_specs, out_shape=out_shape, compiler_params=_params(("parallel",)),
    )(v, ar, ai, *weights)


def _s5_grads(lam_r, lam_i, xr, xi, u, dyc, du_d, b_r, b_i):
    s = u.shape[0]
    g = b_r.shape[0]
    nv, ns, c = SSM_PACK * SSM_GRP, SSM_PACK * SSM_P, SCAN_CHUNKS
    rows = min(S5_ROWS, s)
    slabs = rows // c
    last_slab = s // c - 1
    nt = (((1,), (1,)), ((), ()))
    tn = (((0,), (0,)), ((), ()))

    def body(lr_ref, li_ref, xr_ref, xi_ref, pr_ref, pi_ref, u_ref, dy_ref, dud_ref, br_ref, bi_ref,
             du_ref, dbr_ref, dbi_ref, dcr_ref, dci_ref, dar_ref, dai_ref):
        first = pl.program_id(1) == 0
        l_r, l_i, x_r, x_i = lr_ref[...], li_ref[...], xr_ref[...], xi_ref[...]
        lrb, lib = l_r.astype(BF16), l_i.astype(BF16)
        du_ref[...] = (dud_ref[...] + lax.dot_general(lrb, br_ref[...], nt, preferred_element_type=F32)
                       + lax.dot_general(lib, bi_ref[...], nt, preferred_element_type=F32))
        ub, dyb = u_ref[...].astype(BF16), dy_ref[...].astype(BF16)
        rows_id = lax.broadcasted_iota(jnp.int32, (c, ns), 0)

        def before(p_ref, x):
            p = p_ref[...]
            p = jnp.where(first, jnp.where(rows_id == 0, 0.0, pltpu.roll(p, 1, 0)), p)
            return jnp.concatenate([p, x[:rows - c]], axis=0)

        xp_r, xp_i = before(pr_ref, x_r), before(pi_ref, x_i)
        parts = (lax.dot_general(ub, lrb, tn, preferred_element_type=F32),
                 lax.dot_general(ub, lib, tn, preferred_element_type=F32),
                 lax.dot_general(x_r.astype(BF16), dyb, tn, preferred_element_type=F32),
                 lax.dot_general(x_i.astype(BF16), dyb, tn, preferred_element_type=F32),
                 jnp.sum(l_r * xp_r + l_i * xp_i, axis=0, keepdims=True),
                 jnp.sum(l_i * xp_r - l_r * xp_i, axis=0, keepdims=True))
        accs = (dbr_ref, dbi_ref, dcr_ref, dci_ref, dar_ref, dai_ref)

        @pl.when(first)
        def _():
            for a_ref, val in zip(accs, parts):
                a_ref[...] = val

        @pl.when(jnp.logical_not(first))
        def _():
            for a_ref, val in zip(accs, parts):
                a_ref[...] += val

    state = pl.BlockSpec((rows, ns), lambda j, k: (k, j))
    chan = pl.BlockSpec((rows, nv), lambda j, k: (k, j))
    slab = pl.BlockSpec((c, ns), lambda j, k: (jnp.where(k == 0, last_slab, k * slabs - 1), j))
    per_b = pl.BlockSpec((None, nv, ns), lambda j, k: (j, 0, 0))
    per_c = pl.BlockSpec((None, ns, nv), lambda j, k: (j, 0, 0))
    per_a = pl.BlockSpec((1, ns), lambda j, k: (0, j))
    return pl.pallas_call(
        body, name="s5_grads", grid=(g, s // rows),
        in_specs=[state, state, state, state, slab, slab, chan, chan, chan, per_b, per_b],
        out_specs=[chan, per_b, per_b, per_c, per_c, per_a, per_a],
        out_shape=[jax.ShapeDtypeStruct((s, g * nv), F32), jax.ShapeDtypeStruct((g, nv, ns), F32),
                   jax.ShapeDtypeStruct((g, nv, ns), F32), jax.ShapeDtypeStruct((g, ns, nv), F32),
                   jax.ShapeDtypeStruct((g, ns, nv), F32), jax.ShapeDtypeStruct((1, g * ns), F32),
                   jax.ShapeDtypeStruct((1, g * ns), F32)],
        compiler_params=_params(("parallel", "arbitrary")),
    )(lam_r, lam_i, xr, xi, xr, xi, u, dyc, du_d, b_r, b_i)


def _mesh_place():
    x, y, c = lax.axis_index("x"), lax.axis_index("y"), lax.axis_index("c")
    peers = []
    for k in range(1, N_DEV):
        px, py, pc = x ^ ((k >> 2) & 1), y ^ ((k >> 1) & 1), c ^ (k & 1)
        peers.append(((px, py, pc), 4 * px + 2 * py + pc))
    return 4 * x + 2 * y + c, peers


class _Exchange:
    def __init__(self, arrays, rows, *, gather, name, after=None):
        self.n_arr, self.rows, self.gather, self.name = len(arrays), rows, gather, name
        n_arr = self.n_arr
        if gather:
            assert all(r % BF16_ROWS == 0 for r in rows)
            lands = [lax.empty((N_DEV * r, a.shape[1]), a.dtype) for a, r in zip(arrays, rows)]
        else:
            lands = [lax.empty((N_DEV - 1, a.shape[0] if st is None else n, a.shape[1]), a.dtype)
                     for a, (st, n) in zip(arrays, rows)]
        has_after = after is not None

        def body(*refs):
            ins, zones = refs[:n_arr], refs[n_arr:2 * n_arr]
            sems = refs[2 * n_arr + has_after:4 * n_arr + has_after]
            token = refs[-1]
            me, peers = _mesh_place()
            for i in range(n_arr):
                for k, (pxyz, pid) in enumerate(peers):
                    if gather:
                        src = ins[i]
                        dst = zones[i].at[pl.ds(pl.multiple_of(me * rows[i], BF16_ROWS), rows[i])]
                    else:
                        stride, n = rows[i]
                        src = ins[i] if stride is None else ins[i].at[pl.ds(pl.multiple_of(pid * stride, BF16_ROWS), n)]
                        dst = zones[i].at[k]
                    pltpu.make_async_remote_copy(
                        src_ref=src, dst_ref=dst, send_sem=sems[2 * i], recv_sem=sems[2 * i + 1],
                        device_id=pxyz, device_id_type=pl.DeviceIdType.MESH).start()
            token[...] = jnp.zeros_like(token)

        hbm = pl.BlockSpec(memory_space=pltpu.HBM)
        sem = pl.BlockSpec(memory_space=pltpu.SEMAPHORE)
        args = [pltpu.with_memory_space_constraint(a, pltpu.HBM) for a in list(arrays) + lands]
        res = pl.pallas_call(
            body, name=name + "_start",
            in_specs=[hbm] * (2 * n_arr) + ([pl.BlockSpec(memory_space=pl.ANY)] if has_after else []),
            out_specs=[sem] * (2 * n_arr) + [hbm] * (2 * n_arr) + [pl.BlockSpec(memory_space=pltpu.VMEM)],
            out_shape=[pltpu.SemaphoreType.DMA(())] * (2 * n_arr) + [pltpu.HBM(a.shape, a.dtype) for a in args]
            + [jax.ShapeDtypeStruct((8, LANES), F32)],
            input_output_aliases={i: 2 * n_arr + i for i in range(2 * n_arr)},
            compiler_params=pltpu.CompilerParams(has_side_effects=pltpu.SideEffectType.DATAFLOW_SIDE_EFFECTING),
        )(*args, *([after] if has_after else []))
        self.sems, self.thru, self.token = res[:2 * n_arr], res[2 * n_arr:4 * n_arr], res[-1]

    def wait(self, after):
        n_arr = self.n_arr

        def body(*refs):
            zones, sems = refs[n_arr:2 * n_arr], refs[2 * n_arr:4 * n_arr]
            myself = (lax.axis_index("x"), lax.axis_index("y"), lax.axis_index("c"))
            for i in range(n_arr):
                seven = zones[i].at[pl.ds(0, (N_DEV - 1) * self.rows[i])] if self.gather else zones[i]
                all_seven = pltpu.make_async_remote_copy(
                    src_ref=seven, dst_ref=seven, send_sem=sems[2 * i], recv_sem=sems[2 * i + 1],
                    device_id=myself, device_id_type=pl.DeviceIdType.MESH)
                all_seven.wait_recv()
                all_seven.wait_send()

        hbm = pl.BlockSpec(memory_space=pltpu.HBM)
        sem = pl.BlockSpec(memory_space=pltpu.SEMAPHORE)
        res = pl.pallas_call(
            body, name=self.name + "_wait",
            in_specs=[hbm] * (2 * n_arr) + [sem] * (2 * n_arr) + [pl.BlockSpec(memory_space=pl.ANY)],
            out_specs=[hbm] * (2 * n_arr), out_shape=[pltpu.HBM(a.shape, a.dtype) for a in self.thru],
            input_output_aliases={i: i for i in range(2 * n_arr)},
            compiler_params=pltpu.CompilerParams(has_side_effects=pltpu.SideEffectType.DATAFLOW_SIDE_EFFECTING),
        )(*self.thru, *self.sems, after)
        return res[:n_arr], res[n_arr:]


def _my_slot():
    me = 4 * lax.axis_index("x") + 2 * lax.axis_index("y") + lax.axis_index("c")
    return me.astype(jnp.int32).reshape(1)


def _place_own(gathered, block, me, *, name):
    r, c = block.shape

    def body(me_ref, b_ref, g_ref, o_ref):
        o_ref[...] = b_ref[...]

    return pl.pallas_call(
        body, name=name, out_shape=jax.ShapeDtypeStruct(gathered.shape, gathered.dtype),
        grid_spec=pltpu.PrefetchScalarGridSpec(
            num_scalar_prefetch=1, grid=(1,),
            in_specs=[pl.BlockSpec((r, c), lambda i, me_ref: (0, 0)), pl.BlockSpec(memory_space=pl.ANY)],
            out_specs=pl.BlockSpec((r, c), lambda i, me_ref: (me_ref[0], 0))),
        input_output_aliases={2: 0}, compiler_params=_params(("arbitrary",)),
    )(me, block, gathered)


def _elementwise_tiles(r, c):
    if r % 128 == 0:
        return 128, c
    return r, (256 if c % 256 == 0 else c)


def _adamw_math(g, w, m, v):
    nm = ADAM_B1 * m + (1.0 - ADAM_B1) * g
    nv = ADAM_B2 * v + (1.0 - ADAM_B2) * (g * g)
    m_hat = nm / (1.0 - ADAM_B1 ** ADAM_STEP)
    v_hat = nv / (1.0 - ADAM_B2 ** ADAM_STEP)
    return -ADAM_LR * (m_hat / (jnp.sqrt(v_hat) + ADAM_EPS) + ADAM_WD * w), nm, nv


def _sum_parts(me_ref, own_ref, p_ref, r):
    own = own_ref[...].astype(F32)
    g = None
    for d in range(N_DEV):
        k = jnp.bitwise_xor(me_ref[0], d)
        term = jnp.where(k == 0, own, p_ref[jnp.maximum(k, 1) - 1].astype(F32))
        g = term if g is None else g + term
    return g[0:r, :]


def _sum_adamw(me, sent, stride, parts, r, w=None, m=None, v=None, *, name):
    _, own_rows, cdim = parts.shape
    assert stride is None or stride == own_rows
    tc = 256 if cdim % 256 == 0 else cdim
    update = w is not None

    def body(me_ref, own_ref, p_ref, *refs):
        g = _sum_parts(me_ref, own_ref, p_ref, r)
        if update:
            w_ref, m_ref, v_ref, g_ref, d_ref, nm_ref, nv_ref = refs
            d_ref[...], nm_ref[...], nv_ref[...] = _adamw_math(g, w_ref[...], m_ref[...], v_ref[...])
        else:
            g_ref, = refs
        g_ref[...] = g

    blk = pl.BlockSpec((r, tc), lambda j, me_ref: (0, j))
    own_spec = pl.BlockSpec((own_rows, tc), (lambda j, me_ref: (0, j)) if stride is None else (lambda j, me_ref: (me_ref[0], j)))
    n_out = 4 if update else 1
    res = pl.pallas_call(
        body, name=name, out_shape=[jax.ShapeDtypeStruct((r, cdim), F32)] * n_out,
        grid_spec=pltpu.PrefetchScalarGridSpec(
            num_scalar_prefetch=1, grid=(cdim // tc,),
            in_specs=[own_spec, pl.BlockSpec((N_DEV - 1, own_rows, tc), lambda j, me_ref: (0, 0, j))]
            + ([blk] * 3 if update else []),
            out_specs=[blk] * n_out),
        compiler_params=_params(("parallel",)),
    )(me, sent, parts, *((w, m, v) if update else ()))
    return list(res)


def _adamw(g, w, m, v, *, name):
    r, cdim = w.shape
    tr, tc = _elementwise_tiles(r, cdim)

    def body(g_ref, w_ref, m_ref, v_ref, d_ref, nm_ref, nv_ref):
        d_ref[...], nm_ref[...], nv_ref[...] = _adamw_math(g_ref[...], w_ref[...], m_ref[...], v_ref[...])

    blk = pl.BlockSpec((tr, tc), lambda i, j: (i, j))
    return list(pl.pallas_call(
        body, name=name, grid=(r // tr, cdim // tc), in_specs=[blk] * 4,
        out_specs=[blk] * 3, out_shape=[jax.ShapeDtypeStruct((r, cdim), F32)] * 3,
        compiler_params=_params(("parallel", "parallel")),
    )(g, w, m, v))


SHARD_ROWS_P = {n: (FF_SHARD_P if 'ffn' in n else IN_SHARD_P if n == 'w_in' else None) for n in SHARDED}


def _to_exchange_layout(name, shard):
    t = shard.T if SHARD_AXIS[name] == 1 else shard
    pad = SHARD_ROWS_P[name]
    return t if pad is None else jnp.pad(t, ((0, pad - t.shape[0]), (0, 0)))


def _expand_w_in(wt):
    wt = wt.reshape(N_DEV, IN_SHARD_P, D)[:, :IN_SHARD].reshape(IN_W, D)
    o = Q_RANK + KV_RANK
    kr1, kr2 = wt[o:o + ROPE // 2], wt[o + ROPE // 2:o + ROPE]
    z = jnp.zeros((LANES - ROPE, D), wt.dtype)
    return jnp.concatenate([wt[:o], wt[o + ROPE:], kr1, kr2, z, -kr2, kr1, z], axis=0)


def _expand_w_uq(wt):
    w = wt.reshape(H, QK, Q_RANK)
    z = jnp.zeros((H, LANES - ROPE, Q_RANK), w.dtype)
    q1, q2 = w[:, NOPE:NOPE + ROPE // 2], w[:, NOPE + ROPE // 2:]
    return jnp.concatenate([w[:, :NOPE].reshape(H * NOPE, Q_RANK),
                            jnp.concatenate([q1, q2, z], axis=1).reshape(H * LANES, Q_RANK),
                            jnp.concatenate([-q2, q1, z], axis=1).reshape(H * LANES, Q_RANK)], axis=0)


def _layout_qk_gain(g):
    g = g.reshape(QK)
    g1, g2, z = g[NOPE:NOPE + ROPE // 2], g[NOPE + ROPE // 2:], jnp.zeros((LANES - ROPE,), g.dtype)
    return jnp.stack([g[:NOPE], jnp.concatenate([g1, g2, z]), jnp.concatenate([g2, g1, z])])


def _rep16(a):
    return jnp.repeat(a, SSM_GRP, axis=0)


def _layout_ssm_in(a_re, a_im, log_dt, b_re, b_im):
    b_r = jnp.transpose(b_re, (0, 2, 1)).reshape(SSM_G * SSM_GRP, SSM_P)
    b_i = jnp.transpose(b_im, (0, 2, 1)).reshape(SSM_G * SSM_GRP, SSM_P)
    ldt = jnp.broadcast_to(log_dt.reshape(SSM_G, 1), (SSM_G, SSM_P))
    return _rep16(a_re), _rep16(a_im), _rep16(ldt), b_r, b_i


def _block_diag_b(bb):
    eye = jnp.eye(SSM_PACK, dtype=bb.dtype)
    b5 = bb.reshape(SSM_G // SSM_PACK, SSM_PACK, SSM_GRP, 1, SSM_P) * eye[None, :, None, :, None]
    return b5.reshape(SSM_G // SSM_PACK, SSM_PACK * SSM_GRP, SSM_PACK * SSM_P)


def _block_diag_c(cc):
    eye = jnp.eye(SSM_PACK, dtype=cc.dtype)
    c5 = jnp.transpose(cc, (0, 2, 1)).reshape(SSM_G // SSM_PACK, SSM_PACK, SSM_P, 1, SSM_GRP) * eye[None, :, None, :, None]
    return c5.reshape(SSM_G // SSM_PACK, SSM_PACK * SSM_P, SSM_PACK * SSM_GRP)


def _time_perm(a, inverse=False):
    s, w = a.shape
    c = SCAN_CHUNKS
    if inverse:
        return jnp.transpose(a.reshape(s // c, c, w), (1, 0, 2)).reshape(s, w)
    return jnp.transpose(a.reshape(c, s // c, w), (1, 0, 2)).reshape(s, w)


class _Weights:
    def __init__(self, groups=(), landed=None, me=None):
        self.groups, self.landed, self.me = list(groups), dict(landed or {}), me

    def get(self, name, after):
        if name not in self.landed:
            names, exchange = next(g for g in self.groups if name in g[0])
            for n, block, gathered in zip(names, *exchange.wait(after)):
                self.landed[n] = _place_own(gathered, block, self.me, name="place_" + n)
        return self.landed[name]

    def __getitem__(self, name):
        return self.landed[name]


def _ffn_gate_up(h, w_gt, w_ut, *, name, tm=512, tn=1408):
    s, k = h.shape
    n = w_gt.shape[0]
    tm, tn = min(tm, s), _tile(n, tn)
    dims = (((1,), (1,)), ((), ()))

    def body(h_ref, wg_ref, wu_ref, g_ref, u_ref, a_ref):
        hb = h_ref[...].astype(BF16)
        gate = lax.dot_general(hb, wg_ref[...], dims, preferred_element_type=F32)
        up = lax.dot_general(hb, wu_ref[...], dims, preferred_element_type=F32)
        g_ref[...] = gate.astype(BF16)
        u_ref[...] = up.astype(BF16)
        a_ref[...] = _f_swiglu(gate, up)

    w_spec = pl.BlockSpec((tn, k), lambda i, j: (j, 0))
    o_spec = pl.BlockSpec((tm, tn), lambda i, j: (i, j))
    return pl.pallas_call(
        body, name=name, grid=(s // tm, n // tn), in_specs=[pl.BlockSpec((tm, k), lambda i, j: (i, 0)), w_spec, w_spec],
        out_specs=[o_spec] * 3, out_shape=[jax.ShapeDtypeStruct((s, n), BF16)] * 3,
        compiler_params=_params(("parallel", "parallel")),
    )(h, w_gt, w_ut)


def _ffn_dgate_dup(dx_out, w_d, gate, up, *, name, tm=512, tn=1408, deps=()):
    s, k = dx_out.shape
    n = w_d.shape[0]
    tm, tn = min(tm, s), _tile(n, tn)
    deps = [d for d in deps if d is not None]

    def body(dx_ref, wd_ref, g_ref, u_ref, *refs):
        dg_ref, du_ref = refs[len(deps):]
        dact = 0.5 * lax.dot_general(dx_ref[...].astype(BF16), wd_ref[...], (((1,), (1,)), ((), ())),
                                     preferred_element_type=F32)
        _, vjp = jax.vjp(_f_swiglu, g_ref[...].astype(F32), u_ref[...].astype(F32))
        dgate, dup = vjp(dact.astype(BF16))
        dg_ref[...] = dgate.astype(BF16)
        du_ref[...] = dup.astype(BF16)

    o_spec = pl.BlockSpec((tm, tn), lambda i, j: (i, j))
    return pl.pallas_call(
        body, name=name, grid=(s // tm, n // tn),
        in_specs=[pl.BlockSpec((tm, k), lambda i, j: (i, 0)), pl.BlockSpec((tn, k), lambda i, j: (j, 0)), o_spec, o_spec]
        + [pl.BlockSpec(d.shape, lambda i, j: (0, 0)) for d in deps],
        out_specs=[o_spec] * 2, out_shape=[jax.ShapeDtypeStruct((s, n), BF16)] * 2,
        compiler_params=_params(("parallel", "parallel")),
    )(dx_out, w_d, gate, up, *deps)


def _ffn_dh(dgate, dup, w_gt, w_ut, *, name, tm=512):
    s, k = dgate.shape
    n = w_gt.shape[1]
    tm = min(tm, s)

    def body(dg_ref, du_ref, wg_ref, wu_ref, o_ref):
        o_ref[...] = (jnp.dot(dg_ref[...], wg_ref[...], preferred_element_type=F32)
                      + jnp.dot(du_ref[...], wu_ref[...], preferred_element_type=F32)).astype(o_ref.dtype)

    a_spec = pl.BlockSpec((tm, k), lambda i: (i, 0))
    w_spec = pl.BlockSpec((k, n), lambda i: (0, 0))
    return pl.pallas_call(
        body, name=name, grid=(s // tm,), in_specs=[a_spec, a_spec, w_spec, w_spec],
        out_specs=pl.BlockSpec((tm, n), lambda i: (i, 0)), out_shape=jax.ShapeDtypeStruct((s, n), BF16),
        compiler_params=_params(("parallel",)),
    )(dgate, dup, w_gt, w_ut)


def _ffn_fwd(x, g, wc, tag, deps=()):
    h = _rowwise(_f_norm, [x], [g], [(D, BF16)], name=tag + "_norm", deps=deps)[0]
    gate, up, act = _ffn_gate_up(h, wc.get(tag + '_w_gate', h), wc[tag + '_w_up'], name=tag + "_gate_up")
    x_out = _mm(act, wc.get(tag + '_w_down', act), res=x, scale=0.5, name=tag + "_down")
    return x_out, (h, gate, up, act)


def _ffn_bwd(x, g, wc, saved, dx_out, tag, send):
    h, gate, up, act = saved
    w_gt, w_ut, w_d = (wc.get(tag + n, h) for n in ('_w_gate', '_w_up', '_w_down'))
    d_d = _mm(act, dx_out, ta=True, scale=0.5, out_dtype=GRAD_DTYPE, name=tag + "_dwdown")
    token = send({tag + '_w_down': d_d})
    dgate, dup = _ffn_dgate_dup(dx_out, w_d, gate, up, name=tag + "_dgate_dup", deps=[token])
    d_gt = _mm(dgate, h, ta=True, out_dtype=GRAD_DTYPE, name=tag + "_dwgate")
    d_ut = _mm(dup, h, ta=True, out_dtype=GRAD_DTYPE, name=tag + "_dwup")
    token = send({tag + '_w_gate': d_gt, tag + '_w_up': d_ut})
    dh = _ffn_dh(dgate, dup, w_gt, w_ut, name=tag + "_dh")
    dx, dg = _rowwise_bwd(_f_norm, [x], [g], [dh], row_grads={0: F32}, const_grads=[0], adds={0: dx_out},
                          name=tag + "_norm_bwd", deps=[token])
    return dx, dg


def _local_step(x, mem, cos, sin, target, wc, ws, send, deps=()):
    gs = {}

    x1, sv1 = _ffn_fwd(x, ws['ffn1_norm'], wc, "ffn1", deps=deps)

    h2 = _rowwise(_f_norm, [x1], [ws['mix_norm']], [(D, BF16)], name="mix_norm")[0]
    w_in_raw, w_uq_raw = wc.get('w_in', h2), wc.get('mla_w_uq', h2)
    w_in_e = _expand_w_in(w_in_raw)
    w_uq_e = _expand_w_uq(w_uq_raw)
    proj = _mm(h2, w_in_e, tb=True, name="w_in")
    c_q, c_kv = _rowwise(_f_prep1, [proj], [ws['q_norm'], ws['kv_norm']], [(Q_RANK, BF16), (KV_RANK, BF16)], name="mla_prep1")
    qall = _mm(c_q, w_uq_e, tb=True, name="w_uq")
    kv = _mm(c_kv, wc['mla_w_ukv'], tb=True, name="w_ukv")
    kr = _rowwise(_f_kr, [proj], [], [(2 * LANES, F32)], name="mla_kr")[0]
    q, k, v = _prep2_fwd(qall, kv, kr, cos, sin, ws['qk_gq'], ws['qk_gk'])
    o_mla, lse = _attn_fwd(q, k, v)

    u = proj[:, Q_RANK + KV_RANK:Q_RANK + KV_RANK + SSM_W]
    u_p = _time_perm(u)
    disc_in = [ws['ssm_lr'], ws['ssm_li'], ws['ssm_ldt'], ws['ssm_br'], ws['ssm_bi']]
    ar16, ai16, bbr, bbi = _rowwise(_f_disc, disc_in, [], [(SSM_P, F32)] * 4, name="s5_disc")
    a_r = ar16[::SSM_GRP].reshape(1, SSM_N)
    a_i = ai16[::SSM_GRP].reshape(1, SSM_N)
    bblk_r, bblk_i = _block_diag_b(bbr).astype(BF16), _block_diag_b(bbi).astype(BF16)
    cblk_r, cblk_i = _block_diag_c(ws['ssm_cr']).astype(BF16), _block_diag_c(-ws['ssm_ci']).astype(BF16)
    xr, xi, yc = _s5_scan(u_p, bblk_r, bblk_i, a_r, a_i, reverse=False, tb=False, readout=(cblk_r, cblk_i),
                          name="s5_scan_fwd")
    g_p = _rowwise(_f_s5_gelu, [yc, u_p], [ws['ssm_d']], [(SSM_W, F32)], name="s5_gelu")[0]
    z_p = _mm(g_p, wc['ssm_w_glu'], name="s5_glu")
    g_t, z_t = _time_perm(g_p, inverse=True), _time_perm(z_p, inverse=True)
    on_consts = [ws['ssm_b_glu'], ws['out_norm_mla'], ws['out_norm_ssm']]
    ycat = _rowwise(_f_outnorm, [o_mla, g_t, z_t], on_consts, [(D, BF16)], name="out_norm")[0]
    x2 = _mm(ycat, wc['w_o'], res=x1, name="w_o")

    hx = _rowwise(_f_norm, [x2], [ws['xattn_norm']], [(D, BF16)], name="xattn_norm")[0]
    xq = _mm(hx, wc['xattn_w_q'], name="xattn_q")
    mn = _rowwise(_f_norm, [mem], [ws['mem_norm']], [(D, BF16)], name="mem_norm")[0]
    kvm = _mm(mn, wc['xattn_w_kv'], name="xattn_kv")
    xkn, xv = _rowwise(_f_memk, [kvm], [ws['xattn_k_norm']], [(H * XH, BF16), (H * XH, BF16)], name="xattn_knorm")
    xo = _xattn_fwd(xq, xkn, xv, ws['xattn_q_norm'])
    x3 = _mm(xo, wc['xattn_w_o'], tb=True, res=x2, name="xattn_o")

    x4, sv2 = _ffn_fwd(x3, ws['ffn2_norm'], wc, "ffn2")

    def f_loss(yb, tb):
        err = yb - tb
        return err * (1.0 / D), jnp.broadcast_to(jnp.sum(jnp.sum(err * err, axis=1, keepdims=True), axis=0, keepdims=True) * (0.5 / D), (1, LANES))

    dx4, loss = _rowwise(f_loss, [x4, target], [], [(D, F32)], [(1, LANES)], name="loss")

    dx3, gs['ffn2_norm'] = _ffn_bwd(x3, ws['ffn2_norm'], wc, sv2, dx4, "ffn2", send)

    dxo = _mm(dx3, wc['xattn_w_o'], out_dtype=BF16, name="xattn_o_dx")
    send({'xattn_w_o': _mm(dx3, xo, ta=True, out_dtype=GRAD_DTYPE, name="xattn_o_dw")})
    dxq, dxkn, dxv, gs['xattn_q_norm'] = _xattn_bwd(xq, xkn, xv, ws['xattn_q_norm'], dxo)
    dkvm, gs['xattn_k_norm'] = _rowwise_bwd(_f_memk, [kvm], [ws['xattn_k_norm']], [dxkn, dxv], row_grads={0: BF16},
                                            const_grads=[0], name="xattn_knorm_bwd")
    send({'xattn_w_kv': _mm(mn, dkvm, ta=True, out_dtype=GRAD_DTYPE, name="xattn_kv_dw")})
    dmn = _mm(dkvm, wc['xattn_w_kv'], tb=True, out_dtype=BF16, name="xattn_kv_dx")
    gs['mem_norm'] = _rowwise_bwd(_f_norm, [mem], [ws['mem_norm']], [dmn], row_grads={}, const_grads=[0], name="mem_norm_bwd")[0]
    token = send({'xattn_w_q': _mm(hx, dxq, ta=True, out_dtype=GRAD_DTYPE, name="xattn_q_dw")})
    dhx = _mm(dxq, wc['xattn_w_q'], tb=True, out_dtype=BF16, name="xattn_q_dx")
    dx2, gs['xattn_norm'] = _rowwise_bwd(_f_norm, [x2], [ws['xattn_norm']], [dhx], row_grads={0: F32}, const_grads=[0],
                                         adds={0: dx3}, name="xattn_norm_bwd", deps=[token])

    dycat = _mm(dx2, wc['w_o'], tb=True, out_dtype=BF16, name="w_o_dx")
    send({'w_o': _mm(ycat, dx2, ta=True, out_dtype=GRAD_DTYPE, name="w_o_dw")})
    do_mla, dg_t, dz_t, gs['ssm_b_glu'], gs['out_norm_mla'], gs['out_norm_ssm'] = _rowwise_bwd(
        _f_outnorm, [o_mla, g_t, z_t], on_consts, [dycat], row_grads={0: F32, 1: F32, 2: BF16}, const_grads=[0, 1, 2],
        name="out_norm_bwd")

    dz_p, dg_p = _time_perm(dz_t), _time_perm(dg_t)
    send({'ssm_w_glu': _mm(g_p, dz_p, ta=True, out_dtype=GRAD_DTYPE, name="s5_glu_dw")})
    dg_p = _mm(dz_p, wc['ssm_w_glu'], tb=True, res=dg_p, name="s5_glu_dx")
    dyc, du_d, gs['ssm_d'] = _rowwise_bwd(_f_s5_gelu, [yc, u_p], [ws['ssm_d']], [dg_p], row_grads={0: BF16, 1: F32},
                                          const_grads=[0], name="s5_gelu_bwd")
    lam_r, lam_i = _s5_scan(dyc, cblk_r, cblk_i, a_r, -a_i, reverse=True, tb=True, name="s5_scan_bwd")
    du_p, d_bblk_r, d_bblk_i, d_cblk_r, d_cblk_i, d_ar, d_ai = _s5_grads(lam_r, lam_i, xr, xi, u_p, dyc, du_d,
                                                                        bblk_r, bblk_i)
    du = _time_perm(du_p, inverse=True)
    gs['ssm_cr'] = jax.linear_transpose(_block_diag_c, ws['ssm_cr'])(d_cblk_r)[0]
    gs['ssm_ci'] = -jax.linear_transpose(_block_diag_c, ws['ssm_ci'])(d_cblk_i)[0]
    d_bbr = jax.linear_transpose(_block_diag_b, bbr)(d_bblk_r)[0]
    d_bbi = jax.linear_transpose(_block_diag_b, bbi)(d_bblk_i)[0]
    d_ar16 = jnp.zeros((SSM_G * SSM_GRP, SSM_P), F32).at[::SSM_GRP].set(d_ar.reshape(SSM_G, SSM_P))
    d_ai16 = jnp.zeros((SSM_G * SSM_GRP, SSM_P), F32).at[::SSM_GRP].set(d_ai.reshape(SSM_G, SSM_P))
    gs['ssm_lr'], gs['ssm_li'], gs['ssm_ldt'], gs['ssm_br'], gs['ssm_bi'] = _rowwise_bwd(
        _f_disc, disc_in, [], [d_ar16, d_ai16, d_bbr, d_bbi], row_grads={i: F32 for i in range(5)}, const_grads=[],
        name="s5_disc_bwd")

    delta, do_b = _rowwise(_f_delta, [do_mla, o_mla], [], [(H * LANES, F32), (H * VD, BF16)], name="mla_delta")
    dq, dk, dv = _attn_bwd(q, k, v, do_b, lse, delta)
    dqall, dkv, dkr, gs['qk_gq'], gs['qk_gk'] = _prep2_bwd(qall, kv, kr, cos, sin, ws['qk_gq'], ws['qk_gk'], dq, dk, dv)
    d_w_uq_e = _mm(dqall, c_q, ta=True, name="w_uq_dw")
    send({'mla_w_uq': jax.linear_transpose(_expand_w_uq, jax.ShapeDtypeStruct(w_uq_raw.shape, F32))(d_w_uq_e)[0]})
    dc_q = _mm(dqall, w_uq_e, out_dtype=BF16, name="w_uq_dx")
    send({'mla_w_ukv': _mm(dkv, c_kv, ta=True, out_dtype=GRAD_DTYPE, name="w_ukv_dw")})
    dc_kv = _mm(dkv, wc['mla_w_ukv'], out_dtype=BF16, name="w_ukv_dx")

    def f_prep1_bwd(pb, dcq, dckv, dub, dkrb, gq, gkv):
        _, vjp = jax.vjp(_f_prep1, pb[:, :Q_RANK + KV_RANK], gq, gkv)
        dpa, dgq, dgkv = vjp((dcq.astype(BF16), dckv.astype(BF16)))
        return jnp.concatenate([dpa, dub, dkrb], axis=-1), dgq, dgkv

    dproj, gs['q_norm'], gs['kv_norm'] = _rowwise(
        f_prep1_bwd, [proj, dc_q, dc_kv, du, dkr], [ws['q_norm'], ws['kv_norm']], [(IN_WP, BF16)],
        [(1, Q_RANK), (1, KV_RANK)], name="mla_prep1_bwd")
    d_w_in_e = _mm(dproj, h2, ta=True, name="w_in_dw")
    token = send({'w_in': jax.linear_transpose(_expand_w_in, jax.ShapeDtypeStruct(w_in_raw.shape, F32))(d_w_in_e)[0]})
    dh2 = _mm(dproj, w_in_e, out_dtype=BF16, name="w_in_dx")
    dx1, gs['mix_norm'] = _rowwise_bwd(_f_norm, [x1], [ws['mix_norm']], [dh2], row_grads={0: F32}, const_grads=[0],
                                       adds={0: dx2}, name="mix_norm_bwd", deps=[token])

    dx0, gs['ffn1_norm'] = _ffn_bwd(x, ws['ffn1_norm'], wc, sv1, dx1, "ffn1", send)
    return loss, dx0, gs


def _prep2_fwd(qall, kv, kr, cos, sin, gq, gk):
    return _rowwise(_f_prep2, [qall, kv, kr, cos, sin], [gq, gk], [(H * HQ, BF16), (H * HQ, BF16), (H * VD, BF16)],
                    ts=256, name="mla_prep2")


def _prep2_bwd(qall, kv, kr, cos, sin, gq, gk, dq, dk, dv):
    return _rowwise_bwd(_f_prep2, [qall, kv, kr, cos, sin], [gq, gk], [dq, dk, dv], row_grads={0: BF16, 1: BF16, 2: F32},
                        const_grads=[0, 1], ts=256, name="mla_prep2_bwd")


def _rope_tables(pos):
    half = ROPE // 2
    inv = ROPE_THETA ** (-jnp.arange(half, dtype=F32) / half)
    ang = pos.astype(F32)[:, None] * inv[None, :]
    z = jnp.zeros((pos.shape[0], LANES - ROPE), F32)
    cos, sin = jnp.cos(ang), jnp.sin(ang)
    return jnp.concatenate([cos, cos, z], axis=-1), jnp.concatenate([sin, sin, z], axis=-1)


def _small_layout(p):
    lr, li, ldt, br, bi = _layout_ssm_in(p['ssm_a_re'], p['ssm_a_im'], p['ssm_log_dt'], p['ssm_b_re'], p['ssm_b_im'])
    return {
        'ffn1_norm': p['ffn1_norm'].reshape(1, D), 'mix_norm': p['mix_norm'].reshape(1, D),
        'q_norm': p['mla_q_norm'].reshape(1, Q_RANK), 'kv_norm': p['mla_kv_norm'].reshape(1, KV_RANK),
        'qk_gq': _layout_qk_gain(p['mla_qk_norm_q']), 'qk_gk': _layout_qk_gain(p['mla_qk_norm_k']),
        'ssm_lr': lr, 'ssm_li': li, 'ssm_ldt': ldt, 'ssm_br': br, 'ssm_bi': bi,
        'ssm_cr': p['ssm_c_re'], 'ssm_ci': p['ssm_c_im'], 'ssm_d': p['ssm_d'].reshape(1, SSM_W),
        'ssm_b_glu': p['ssm_b_glu'].reshape(1, SSM_W),
        'out_norm_mla': p['out_norm_mla'].reshape(1, SSM_W), 'out_norm_ssm': p['out_norm_ssm'].reshape(1, SSM_W),
        'xattn_norm': p['xattn_norm'].reshape(1, D), 'mem_norm': p['mem_norm'].reshape(1, D),
        'xattn_q_norm': p['xattn_q_norm'].reshape(1, XH), 'xattn_k_norm': p['xattn_k_norm'].reshape(1, XH),
        'ffn2_norm': p['ffn2_norm'].reshape(1, D),
    }


def _pack(arrs, rows):
    flat = jnp.concatenate([a.reshape(-1) for a in arrs])
    return jnp.pad(flat, (0, rows * D - flat.shape[0])).reshape(rows, D)


def _unpack(flat, shapes):
    flat = flat.reshape(-1)
    out, off = [], 0
    for sh in shapes:
        n = int(np.prod(sh))
        out.append(flat[off:off + n].reshape(sh))
        off += n
    return out


def kernel(x, mem, positions, ffn1_norm, ffn1_w_gate, ffn1_w_up, ffn1_w_down, mix_norm, w_in, mla_q_norm, mla_w_uq, mla_kv_norm, mla_w_ukv, mla_qk_norm_q, mla_qk_norm_k, ssm_a_re, ssm_a_im, ssm_log_dt, ssm_b_re, ssm_b_im, ssm_c_re, ssm_c_im, ssm_d, ssm_w_glu, ssm_b_glu, out_norm_mla, out_norm_ssm, w_o, xattn_norm, mem_norm, xattn_w_q, xattn_w_kv, xattn_q_norm, xattn_k_norm, xattn_w_o, ffn2_norm, ffn2_w_gate, ffn2_w_up, ffn2_w_down, loss_target, m_ffn1_norm, m_ffn1_w_gate, m_ffn1_w_up, m_ffn1_w_down, m_mix_norm, m_w_in, m_mla_q_norm, m_mla_w_uq, m_mla_kv_norm, m_mla_w_ukv, m_mla_qk_norm_q, m_mla_qk_norm_k, m_ssm_a_re, m_ssm_a_im, m_ssm_log_dt, m_ssm_b_re, m_ssm_b_im, m_ssm_c_re, m_ssm_c_im, m_ssm_d, m_ssm_w_glu, m_ssm_b_glu, m_out_norm_mla, m_out_norm_ssm, m_w_o, m_xattn_norm, m_mem_norm, m_xattn_w_q, m_xattn_w_kv, m_xattn_q_norm, m_xattn_k_norm, m_xattn_w_o, m_ffn2_norm, m_ffn2_w_gate, m_ffn2_w_up, m_ffn2_w_down, v_ffn1_norm, v_ffn1_w_gate, v_ffn1_w_up, v_ffn1_w_down, v_mix_norm, v_w_in, v_mla_q_norm, v_mla_w_uq, v_mla_kv_norm, v_mla_w_ukv, v_mla_qk_norm_q, v_mla_qk_norm_k, v_ssm_a_re, v_ssm_a_im, v_ssm_log_dt, v_ssm_b_re, v_ssm_b_im, v_ssm_c_re, v_ssm_c_im, v_ssm_d, v_ssm_w_glu, v_ssm_b_glu, v_out_norm_mla, v_out_norm_ssm, v_w_o, v_xattn_norm, v_mem_norm, v_xattn_w_q, v_xattn_w_kv, v_xattn_q_norm, v_xattn_k_norm, v_xattn_w_o, v_ffn2_norm, v_ffn2_w_gate, v_ffn2_w_up, v_ffn2_w_down):
    args = dict(locals())
    w = {n: args[n] for n in WEIGHTS}
    mom = {n: args['m_' + n] for n in WEIGHTS}
    var = {n: args['v_' + n] for n in WEIGHTS}
    return _step(x, mem, positions, loss_target, w, mom, var)


GATHER_GROUPS = [('ffn1_gu', ['ffn1_w_gate', 'ffn1_w_up']), ('ffn1_down', ['ffn1_w_down']),
                 ('mix', ['w_in', 'mla_w_uq', 'mla_w_ukv', 'ssm_w_glu', 'w_o', 'xattn_w_q', 'xattn_w_kv', 'xattn_w_o']),
                 ('ffn2', ['ffn2_w_gate', 'ffn2_w_up', 'ffn2_w_down'])]
SCATTER_GROUPS = [('ffn2_down', ['ffn2_w_down']), ('ffn2_gu', ['ffn2_w_gate', 'ffn2_w_up']),
                  ('xattn', ['xattn_w_o', 'xattn_w_kv', 'xattn_w_q']),
                  ('mix', ['w_o', 'ssm_w_glu', 'mla_w_uq', 'mla_w_ukv', 'w_in']),
                  ('ffn1_down', ['ffn1_w_down']), ('ffn1_gu', ['ffn1_w_gate', 'ffn1_w_up'])]


def _step(x, mem, positions, loss_target, w, mom, var):
    blocks = {n: _to_exchange_layout(n, w[n][0]).astype(BF16) for n in SHARDED}
    gathers, token = [], None
    for tag, names in GATHER_GROUPS:
        ex = _Exchange([blocks[n] for n in names], [blocks[n].shape[0] for n in names], gather=True,
                       name="gather_" + tag, after=token)
        gathers.append((names, ex))
        token = ex.token
    me = _my_slot()
    wc = _Weights(gathers, me=me)

    rows = {n: (blocks[n].shape[0], blocks[n].shape[0]) for n in SHARDED}
    ready, scatters = {}, []

    def send(grads):
        ready.update({n: g.astype(GRAD_DTYPE) for n, g in grads.items()})
        for tag, names in SCATTER_GROUPS:
            if all(n in ready for n in names) and not any(t == tag for t, _, _ in scatters):
                ex = _Exchange([ready[n] for n in names], [rows[n] for n in names], gather=False, name="scatter_" + tag)
                scatters.append((tag, names, ex))
                return ex.token
        return None

    small = {n: w[n][0] for n in SMALL}
    ws = _small_layout(small)
    cos, sin = _rope_tables(positions[0])
    loss, dx, gs = _local_step(x[0], mem[0], cos, sin, loss_target[0], wc, ws, send, deps=[token])

    g_small = jax.linear_transpose(_small_layout, {n: jax.ShapeDtypeStruct(small[n].shape, F32) for n in SMALL})(gs)[0]
    small_shapes = [small[n].shape for n in SMALL]
    n_small = sum(int(np.prod(sh)) for sh in small_shapes) + 1
    rows_small = -(-n_small // (8 * D)) * 8
    small_pack = _pack([g_small[n] for n in SMALL] + [loss[0, :1]], rows_small)
    small_ex = _Exchange([small_pack], [(None, rows_small)], gather=False, name="scatter_small")

    out, after = {}, dx
    for _, names, ex in scatters:
        for n, sent, p in zip(names, *ex.wait(after)):
            r = w[n][0].shape[SHARD_AXIS[n]]
            if SHARD_AXIS[n] == 0:
                out[n] = _sum_adamw(me, sent, rows[n][0], p, r, w[n][0], mom[n][0], var[n][0], name="adamw_" + n)
            else:
                g = _sum_adamw(me, sent, rows[n][0], p, r, name="sum_" + n)[0].T
                out[n] = [g] + _adamw(g, w[n][0], mom[n][0], var[n][0], name="adamw_" + n)
        after = out[names[-1]][1]
    state = [_pack([t[n][0] for n in SMALL], rows_small) for t in (w, mom, var)]
    sent, p = small_ex.wait(after)
    small_out = _sum_adamw(me, sent[0], None, p[0], rows_small, *state, name="adamw_small")
    loss_total = small_out[0].reshape(-1)[n_small - 1]
    for n, vals in zip(SMALL, zip(*[_unpack(flat, small_shapes) for flat in small_out])):
        out[n] = vals
    outs = [out[n][i][None] for i in range(4) for n in WEIGHTS]
    return (loss_total, dx[None], *outs)
```

```python
import math

import jax
import jax.numpy as jnp
import numpy as np
from jax import lax
from jax.experimental import pallas as pl
from jax.experimental.pallas import tpu as pltpu

F32 = jnp.float32
BF16 = jnp.bfloat16

N_DEV = 8
D = 1024
D_FF = 2752
D_FFP = 2816
MEM_LEN = 256
H = 4
Q_RANK, KV_RANK, NOPE, ROPE, VD = 384, 256, 128, 64, 128
QK = NOPE + ROPE
HQ = 2 * 128
SSM_W, SSM_G, SSM_GRP, SSM_P = 512, 32, 16, 64
SSM_N = SSM_G * SSM_P
SSM_PACK = 8
IN_W = 1216
IN_WP = 1408
XH = 128
EPS = 1e-6
LN2 = math.log(2.0)
ROPE_THETA = 10000.0
SCAN_CHUNKS = 8
SCAN_UNROLL = 8
ADAM_LR, ADAM_B1, ADAM_B2, ADAM_EPS, ADAM_WD, ADAM_STEP = 0.001, 0.9, 0.999, 1e-08, 0.01, 10

VMEM_LIMIT = 56 * 1024 * 1024
ACC_BYTES = 6 * 1024 * 1024
LANES = 128
BF16_ROWS = 16
GRAD_DTYPE = BF16
FF_SHARD = D_FF // N_DEV
FF_SHARD_P = 352
IN_SHARD = IN_W // N_DEV
IN_SHARD_P = 160

WEIGHTS = ['ffn1_norm', 'ffn1_w_gate', 'ffn1_w_up', 'ffn1_w_down', 'mix_norm', 'w_in', 'mla_q_norm', 'mla_w_uq',
           'mla_kv_norm', 'mla_w_ukv', 'mla_qk_norm_q', 'mla_qk_norm_k', 'ssm_a_re', 'ssm_a_im', 'ssm_log_dt',
           'ssm_b_re', 'ssm_b_im', 'ssm_c_re', 'ssm_c_im', 'ssm_d', 'ssm_w_glu', 'ssm_b_glu', 'out_norm_mla',
           'out_norm_ssm', 'w_o', 'xattn_norm', 'mem_norm', 'xattn_w_q', 'xattn_w_kv', 'xattn_q_norm',
           'xattn_k_norm', 'xattn_w_o', 'ffn2_norm', 'ffn2_w_gate', 'ffn2_w_up', 'ffn2_w_down']
SHARD_AXIS = {'ffn1_w_gate': 1, 'ffn1_w_up': 1, 'ffn1_w_down': 0, 'w_in': 1, 'mla_w_uq': 1, 'mla_w_ukv': 1,
              'ssm_w_glu': 0, 'w_o': 0, 'xattn_w_q': 0, 'xattn_w_kv': 0, 'xattn_w_o': 1,
              'ffn2_w_gate': 1, 'ffn2_w_up': 1, 'ffn2_w_down': 0}
SHARDED = [n for n in WEIGHTS if n in SHARD_AXIS]
SMALL = [n for n in WEIGHTS if n not in SHARD_AXIS]


def _params(sem=None):
    return pltpu.CompilerParams(dimension_semantics=sem, vmem_limit_bytes=VMEM_LIMIT)


def _tile(n, cap):
    if n <= cap:
        return n
    best = n
    for t in range(LANES, cap + 1, LANES):
        if n % t == 0:
            best = t
    return best


def _mm(a, b, *, ta=False, tb=False, out_dtype=F32, res=None, scale=1.0, name, tm_cap=512, tn_cap=1408, tk_cap=2816,
        deps=()):
    m, k = (a.shape[1], a.shape[0]) if ta else a.shape
    k2, n = (b.shape[1], b.shape[0]) if tb else b.shape
    assert k == k2, (a.shape, b.shape, ta, tb)
    if ta:
        tk_cap = min(tk_cap, 512)
        tm_cap = 1408
    tm, tn, tk = _tile(m, tm_cap), _tile(n, tn_cap), _tile(k, tk_cap)
    if tm * tn * 4 > ACC_BYTES:
        tn = _tile(n, max(LANES, ACC_BYTES // (4 * tm) // LANES * LANES))
    nk = k // tk
    dims = (((0 if ta else 1,), (1 if tb else 0,)), ((), ()))
    has_res = res is not None

    deps = [d for d in deps if d is not None]

    def body(*refs):
        a_ref, b_ref = refs[:2]
        r_ref = refs[2] if has_res else None
        o_ref, acc_ref = refs[-2:]
        kk = pl.program_id(2)

        @pl.when(kk == 0)
        def _():
            acc_ref[...] = jnp.zeros_like(acc_ref)

        acc_ref[...] += lax.dot_general(a_ref[...].astype(BF16), b_ref[...].astype(BF16), dims,
                                        preferred_element_type=F32)

        @pl.when(kk == nk - 1)
        def _():
            out = acc_ref[...]
            if scale != 1.0:
                out = out * scale
            if has_res:
                out = out + r_ref[...].astype(F32)
            o_ref[...] = out.astype(o_ref.dtype)

    a_spec = pl.BlockSpec((tk, tm), lambda i, j, kk: (kk, i)) if ta else pl.BlockSpec((tm, tk), lambda i, j, kk: (i, kk))
    b_spec = pl.BlockSpec((tn, tk), lambda i, j, kk: (j, kk)) if tb else pl.BlockSpec((tk, tn), lambda i, j, kk: (kk, j))
    o_spec = pl.BlockSpec((tm, tn), lambda i, j, kk: (i, j))
    in_specs = [a_spec, b_spec] + ([o_spec] if has_res else []) + [pl.BlockSpec(d.shape, lambda i, j, kk: (0, 0)) for d in deps]
    args = (a, b) + ((res,) if has_res else ()) + tuple(deps)
    return pl.pallas_call(
        body, name=name, grid=(m // tm, n // tn, nk), in_specs=in_specs, out_specs=o_spec,
        out_shape=jax.ShapeDtypeStruct((m, n), out_dtype), scratch_shapes=[pltpu.VMEM((tm, tn), F32)],
        compiler_params=_params(("parallel", "parallel", "arbitrary")),
    )(*args)


def _mm_grouped(a, b, *, tb=False, res=None, out_dtype=F32, name, tm=512):
    s = a.shape[0]
    g = b.shape[0]
    nb, ka = (b.shape[1], b.shape[2]) if tb else (b.shape[2], b.shape[1])
    assert a.shape[1] == g * ka
    tm = min(tm, s)
    dims = (((1,), (1 if tb else 0,)), ((), ()))
    has_res = res is not None

    def body(*refs):
        if has_res:
            a_ref, b_ref, r_ref, o_ref = refs
        else:
            a_ref, b_ref, o_ref = refs
        out = lax.dot_general(a_ref[...].astype(BF16), b_ref[...].astype(BF16), dims, preferred_element_type=F32)
        if has_res:
            out = out + r_ref[...].astype(F32)
        o_ref[...] = out.astype(o_ref.dtype)

    o_spec = pl.BlockSpec((tm, nb), lambda i, j: (i, j))
    in_specs = [pl.BlockSpec((tm, ka), lambda i, j: (i, j)), pl.BlockSpec((None,) + b.shape[1:], lambda i, j: (j, 0, 0))]
    return pl.pallas_call(
        body, name=name, grid=(s // tm, g), in_specs=in_specs + ([o_spec] if has_res else []), out_specs=o_spec,
        out_shape=jax.ShapeDtypeStruct((s, g * nb), out_dtype), compiler_params=_params(("parallel", "parallel")),
    )(a, b, *((res,) if has_res else ()))


def _mm_grouped_tn(a, b, *, ka, kb, name, tk=512):
    s = a.shape[0]
    g = a.shape[1] // ka
    assert b.shape[1] == g * kb
    tk = min(tk, s)
    nk = s // tk

    def body(a_ref, b_ref, o_ref):
        part = lax.dot_general(a_ref[...].astype(BF16), b_ref[...].astype(BF16), (((0,), (0,)), ((), ())),
                               preferred_element_type=F32)

        @pl.when(pl.program_id(1) == 0)
        def _():
            o_ref[...] = part

        @pl.when(pl.program_id(1) > 0)
        def _():
            o_ref[...] += part

    return pl.pallas_call(
        body, name=name, grid=(g, nk),
        in_specs=[pl.BlockSpec((tk, ka), lambda j, kk: (kk, j)), pl.BlockSpec((tk, kb), lambda j, kk: (kk, j))],
        out_specs=pl.BlockSpec((None, ka, kb), lambda j, kk: (j, 0, 0)),
        out_shape=jax.ShapeDtypeStruct((g, ka, kb), F32), compiler_params=_params(("parallel", "arbitrary")),
    )(a, b)


def _rowwise(fn, rows, consts, outs, accs=(), *, ts=512, name, deps=()):
    s = rows[0].shape[0]
    ts = min(ts, s)
    assert s % ts == 0
    n_rows, n_consts, n_outs = len(rows), len(consts), len(outs)
    deps = [d for d in deps if d is not None]
    consts = list(consts) + deps

    def body(*refs):
        ins = [r[...] for r in refs[:n_rows + n_consts]]
        res = fn(*ins)
        res = tuple(res) if isinstance(res, (tuple, list)) else (res,)
        out_refs = refs[n_rows + len(consts):]
        for o_ref, val in zip(out_refs[:n_outs], res[:n_outs]):
            o_ref[...] = val.astype(o_ref.dtype)
        if accs:
            first = pl.program_id(0) == 0

            @pl.when(first)
            def _():
                for a_ref, val in zip(out_refs[n_outs:], res[n_outs:]):
                    a_ref[...] = val.astype(F32)

            @pl.when(jnp.logical_not(first))
            def _():
                for a_ref, val in zip(out_refs[n_outs:], res[n_outs:]):
                    a_ref[...] += val.astype(F32)

    in_specs = [pl.BlockSpec((ts, r.shape[1]), lambda i: (i, 0)) for r in rows]
    in_specs += [pl.BlockSpec(c.shape, lambda i: (0, 0)) for c in consts]
    out_specs = [pl.BlockSpec((ts, w), lambda i: (i, 0)) for w, _ in outs]
    out_specs += [pl.BlockSpec(tuple(sh), lambda i: (0, 0)) for sh in accs]
    out_shape = [jax.ShapeDtypeStruct((s, w), dt) for w, dt in outs]
    out_shape += [jax.ShapeDtypeStruct(tuple(sh), F32) for sh in accs]
    res = pl.pallas_call(
        body, name=name, grid=(s // ts,), in_specs=in_specs, out_specs=out_specs, out_shape=out_shape,
        compiler_params=_params(("arbitrary",)),
    )(*rows, *consts)
    return res


def _rowwise_bwd(f, rows, consts, cts, *, row_grads, const_grads, adds=None, ts=512, name, deps=()):
    adds = adds or {}
    n_rows, n_consts, n_cts = len(rows), len(consts), len(cts)
    add_keys = sorted(adds)
    rg = sorted(row_grads)
    cg = sorted(const_grads)

    def fn(*args):
        r = args[:n_rows]
        c = args[n_rows:n_rows + n_consts]
        ct = args[n_rows + n_consts:n_rows + n_consts + n_cts]
        extra = args[n_rows + n_consts + n_cts:]
        outs, vjp = jax.vjp(f, *r, *c)
        outs = tuple(outs) if isinstance(outs, (tuple, list)) else (outs,)
        cot = tuple(g.astype(o.dtype) for g, o in zip(ct, outs))
        grads = vjp(cot if len(cot) > 1 else cot[0])
        res = []
        for i in rg:
            g = grads[i].astype(F32)
            if i in adds:
                g = g + extra[add_keys.index(i)].astype(F32)
            res.append(g)
        for i in cg:
            res.append(grads[n_rows + i])
        return tuple(res)

    rows_all = list(rows) + list(cts) + [adds[i] for i in add_keys]
    def fn2(*args):
        nr = len(rows_all)
        rr, cc = args[:nr], args[nr:]
        return fn(*rr[:n_rows], *cc, *rr[n_rows:])

    outs = [(rows[i].shape[1], row_grads[i]) for i in rg]
    accs = [consts[i].shape for i in cg]
    return _rowwise(fn2, rows_all, list(consts), outs, accs, ts=ts, name=name, deps=deps)


def _rms(x, g):
    xf = x.astype(F32)
    return xf * lax.rsqrt(jnp.mean(xf * xf, axis=-1, keepdims=True) + EPS) * g.astype(F32)


def _sigmoid(x):
    return 1.0 / (1.0 + jnp.exp(-x))


def _f_norm(x, g):
    return _rms(x, g).astype(BF16)


def _f_swiglu(gate, up):
    gate, up = gate.astype(F32), up.astype(F32)
    return (gate * _sigmoid(gate) * up).astype(BF16)


def _f_prep1(proj, gq, gkv):
    return _rms(proj[:, :Q_RANK], gq).astype(BF16), _rms(proj[:, Q_RANK:Q_RANK + KV_RANK], gkv).astype(BF16)


def _f_kr(proj):
    return (proj[:, Q_RANK + KV_RANK + SSM_W:],)


def _f_prep2(qall, kv, kr2, cos, sin, gq, gk):
    kr, krs = kr2[:, :LANES].astype(F32), kr2[:, LANES:].astype(F32)
    k_rot = kr * gk[1:2] * cos + krs * gk[2:3] * sin
    k_ss = jnp.sum(kr * kr, axis=-1, keepdims=True)
    q_scale = QK ** -0.5 / LN2
    qs, ks, vs = [], [], []
    for h in range(H):
        qn = qall[:, h * LANES:(h + 1) * LANES].astype(F32)
        qr = qall[:, (H + h) * LANES:(H + h + 1) * LANES].astype(F32)
        qrs = qall[:, (2 * H + h) * LANES:(2 * H + h + 1) * LANES].astype(F32)
        rstd = lax.rsqrt((jnp.sum(qn * qn, axis=-1, keepdims=True) + jnp.sum(qr * qr, axis=-1, keepdims=True)) / QK + EPS)
        rstd = rstd * q_scale
        qs += [qn * gq[0:1] * rstd, (qr * gq[1:2] * cos + qrs * gq[2:3] * sin) * rstd]
        kn = kv[:, 2 * h * LANES:(2 * h + 1) * LANES].astype(F32)
        rstd_k = lax.rsqrt((jnp.sum(kn * kn, axis=-1, keepdims=True) + k_ss) / QK + EPS)
        ks += [kn * gk[0:1] * rstd_k, k_rot * rstd_k]
        vs.append(kv[:, (2 * h + 1) * LANES:(2 * h + 2) * LANES])
    return (jnp.concatenate(qs, axis=-1).astype(BF16), jnp.concatenate(ks, axis=-1).astype(BF16),
            jnp.concatenate(vs, axis=-1).astype(BF16))


def _gelu(x):
    return 0.5 * x * (1.0 + jnp.tanh(math.sqrt(2.0 / math.pi) * (x + 0.044715 * (x * x * x))))


def _f_s5_gelu(yc, u, d):
    return _gelu(yc.astype(F32) + d * u.astype(F32))


def _f_outnorm(o_mla, g, z, b_glu, g_om, g_os):
    y_ssm = g * _sigmoid(z + b_glu)
    return jnp.concatenate([_rms(o_mla, g_om), _rms(y_ssm, g_os)], axis=-1).astype(BF16)


def _f_memk(kvm, gk):
    ks = [_rms(kvm[:, h * XH:(h + 1) * XH], gk) for h in range(H)]
    return jnp.concatenate(ks, axis=-1).astype(BF16), kvm[:, H * XH:].astype(BF16)


def _f_disc(lr, li, log_dt, br, bi):
    dt = jnp.exp(log_dt)
    decay = jnp.exp(lr * dt)
    ar = decay * jnp.cos(li * dt)
    ai = decay * jnp.sin(li * dt)
    den = lr * lr + li * li
    nr = ar - 1.0
    coef_r = (nr * lr + ai * li) / den
    coef_i = (ai * lr - nr * li) / den
    return ar, ai, coef_r * br - coef_i * bi, coef_r * bi + coef_i * br


def _causal_mask(i, j, tq, tk):
    qpos = i * tq + lax.broadcasted_iota(jnp.int32, (tq, tk), 0)
    kpos = j * tk + lax.broadcasted_iota(jnp.int32, (tq, tk), 1)
    return qpos >= kpos


def _attn_fwd(q, k, v, *, t=512):
    s = q.shape[0]
    t = min(t, s)
    nb = s // t

    def body(q_ref, k_ref, v_ref, o_ref, lse_ref, m_sc, l_sc, acc_sc):
        i, j = pl.program_id(1), pl.program_id(2)

        @pl.when(j == 0)
        def _():
            m_sc[...] = jnp.full_like(m_sc, -jnp.inf)
            l_sc[...] = jnp.zeros_like(l_sc)
            acc_sc[...] = jnp.zeros_like(acc_sc)

        def block(diagonal):
            sc = lax.dot_general(q_ref[...], k_ref[...], (((1,), (1,)), ((), ())), preferred_element_type=F32)
            if diagonal:
                sc = jnp.where(_causal_mask(i, j, t, t), sc, -jnp.inf)
            m_old = m_sc[...]
            m_new = jnp.maximum(m_old, jnp.max(sc, axis=-1, keepdims=True))
            p = jnp.exp2(sc - m_new)
            alpha = jnp.exp2(m_old - m_new)
            l_sc[...] = alpha * l_sc[...] + jnp.sum(p, axis=-1, keepdims=True)
            acc_sc[...] = alpha * acc_sc[...] + jnp.dot(p.astype(BF16), v_ref[...], preferred_element_type=F32)
            m_sc[...] = m_new

        pl.when(j < i)(lambda: block(False))

        @pl.when(j == i)
        def _():
            block(True)
            o_ref[...] = acc_sc[...] / l_sc[...]
            lse_ref[...] = jnp.broadcast_to(m_sc[...] + jnp.log2(l_sc[...]), lse_ref.shape)

    kv_map = lambda h, i, j: (jnp.minimum(j, i), h)
    return pl.pallas_call(
        body, name="mla_attn_fwd", grid=(H, nb, nb),
        in_specs=[pl.BlockSpec((t, HQ), lambda h, i, j: (i, h)), pl.BlockSpec((t, HQ), kv_map),
                  pl.BlockSpec((t, VD), kv_map)],
        out_specs=[pl.BlockSpec((t, VD), lambda h, i, j: (i, h)), pl.BlockSpec((t, LANES), lambda h, i, j: (i, h))],
        out_shape=[jax.ShapeDtypeStruct((s, H * VD), F32), jax.ShapeDtypeStruct((s, H * LANES), F32)],
        scratch_shapes=[pltpu.VMEM((t, 1), F32), pltpu.VMEM((t, 1), F32), pltpu.VMEM((t, VD), F32)],
        compiler_params=_params(("parallel", "parallel", "arbitrary")),
    )(q, k, v)


def _attn_probs(q_ref, k_ref, v_ref, do_ref, lse_ref, dl_ref, i, j, t, diagonal):
    sc = lax.dot_general(q_ref[...], k_ref[...], (((1,), (1,)), ((), ())), preferred_element_type=F32)
    p = jnp.exp2(sc - jnp.tile(lse_ref[...], (1, t // LANES)))
    if diagonal:
        p = jnp.where(_causal_mask(i, j, t, t), p, 0.0)
    dp = lax.dot_general(do_ref[...], v_ref[...], (((1,), (1,)), ((), ())), preferred_element_type=F32)
    ds = p * (dp - jnp.tile(dl_ref[...], (1, t // LANES)))
    return p, ds


def _attn_bwd(q, k, v, do, lse, delta, *, t=512):
    s = q.shape[0]
    t = min(t, s)
    nb = s // t

    def body(q_ref, k_ref, v_ref, do_ref, lse_ref, dl_ref, dq_ref, dk_ref, dv_ref, dk_sc, dv_sc):
        j, i = pl.program_id(1), pl.program_id(2)

        @pl.when(jnp.logical_and(i == 0, j == 0))
        def _():
            dq_ref[...] = jnp.zeros_like(dq_ref)

        @pl.when(i == 0)
        def _():
            dk_sc[...] = jnp.zeros_like(dk_sc)
            dv_sc[...] = jnp.zeros_like(dv_sc)

        def block(diagonal):
            p, ds = _attn_probs(q_ref, k_ref, v_ref, do_ref, lse_ref, dl_ref, i, j, t, diagonal)
            dsb = ds.astype(BF16)
            dv_sc[...] += lax.dot_general(p.astype(BF16), do_ref[...], (((0,), (0,)), ((), ())), preferred_element_type=F32)
            dk_sc[...] += lax.dot_general(dsb, q_ref[...], (((0,), (0,)), ((), ())), preferred_element_type=F32)
            rows = pl.ds(pl.multiple_of(i * t, t), t)
            dq_ref[rows, :] += jnp.dot(dsb, k_ref[...], preferred_element_type=F32)

        pl.when(i > j)(lambda: block(False))
        pl.when(i == j)(lambda: block(True))

        @pl.when(i == nb - 1)
        def _():
            dk_ref[...] = dk_sc[...] * LN2
            dv_ref[...] = dv_sc[...]

        @pl.when(jnp.logical_and(i == nb - 1, j == nb - 1))
        def _():
            dq_ref[...] = dq_ref[...] * LN2

    q_map = lambda h, j, i: (jnp.maximum(i, j), h)
    kv_map = lambda h, j, i: (j, h)
    dq, dk, dv = pl.pallas_call(
        body, name="mla_attn_bwd", grid=(H, nb, nb),
        in_specs=[pl.BlockSpec((t, HQ), q_map), pl.BlockSpec((t, HQ), kv_map), pl.BlockSpec((t, VD), kv_map),
                  pl.BlockSpec((t, VD), q_map), pl.BlockSpec((t, LANES), q_map), pl.BlockSpec((t, LANES), q_map)],
        out_specs=[pl.BlockSpec((s, HQ), lambda h, j, i: (0, h)), pl.BlockSpec((t, HQ), kv_map), pl.BlockSpec((t, VD), kv_map)],
        out_shape=[jax.ShapeDtypeStruct((s, H * HQ), F32), jax.ShapeDtypeStruct((s, H * HQ), F32),
                   jax.ShapeDtypeStruct((s, H * VD), F32)],
        scratch_shapes=[pltpu.VMEM((t, HQ), F32), pltpu.VMEM((t, VD), F32)],
        compiler_params=_params(("parallel", "arbitrary", "arbitrary")),
    )(q, k, v, do, lse, delta)
    return dq, dk, dv


def _f_delta(do, o):
    prod = do.astype(F32) * o.astype(F32)
    parts = [jnp.broadcast_to(jnp.sum(prod[:, h * VD:(h + 1) * VD], axis=-1, keepdims=True), (do.shape[0], LANES))
             for h in range(H)]
    return jnp.concatenate(parts, axis=-1), do.astype(BF16)


def _xattn_head(qh, kh, gq):
    qn = _rms(qh, gq) * (XH ** -0.5)
    sc = lax.dot_general(qn.astype(BF16), kh, (((1,), (1,)), ((), ())), preferred_element_type=F32)
    sc = sc - jnp.max(sc, axis=-1, keepdims=True)
    e = jnp.exp(sc)
    return qn, e / jnp.sum(e, axis=-1, keepdims=True)


def _xattn_fwd(q, kn, v, gq, *, ts=512):
    def fn(qb, knb, vb, g):
        outs = []
        for h in range(H):
            sl = slice(h * XH, (h + 1) * XH)
            _, p = _xattn_head(qb[:, sl], knb[:, sl], g)
            outs.append(jnp.dot(p.astype(BF16), vb[:, sl], preferred_element_type=F32))
        return (jnp.concatenate(outs, axis=-1),)

    return _rowwise(fn, [q], [kn, v, gq], [(H * XH, BF16)], ts=ts, name="xattn_fwd")[0]


def _xattn_bwd(q, kn, v, gq, do, *, ts=512):
    def fn(qb, dob, knb, vb, g):
        dqs, dks, dvs = [], [], []
        dg = jnp.zeros((1, XH), F32)
        for h in range(H):
            sl = slice(h * XH, (h + 1) * XH)
            qh, kh, vh, doh = qb[:, sl], knb[:, sl], vb[:, sl], dob[:, sl].astype(BF16)
            qn, p = _xattn_head(qh, kh, g)
            dp = lax.dot_general(doh, vh, (((1,), (1,)), ((), ())), preferred_element_type=F32)
            dvs.append(lax.dot_general(p.astype(BF16), doh, (((0,), (0,)), ((), ())), preferred_element_type=F32))
            ds = (p * (dp - jnp.sum(dp * p, axis=-1, keepdims=True))).astype(BF16)
            dqn = jnp.dot(ds, kh, preferred_element_type=F32)
            dks.append(lax.dot_general(ds, qn.astype(BF16), (((0,), (0,)), ((), ())), preferred_element_type=F32))
            _, vjp_n = jax.vjp(lambda a, b: _rms(a, b) * (XH ** -0.5), qh, g)
            dqh, dgh = vjp_n(dqn)
            dqs.append(dqh)
            dg = dg + dgh
        return (jnp.concatenate(dqs, axis=-1), jnp.concatenate(dks, axis=-1), jnp.concatenate(dvs, axis=-1), dg)

    return _rowwise(fn, [q, do], [kn, v, gq], [(H * XH, BF16)], [kn.shape, v.shape, gq.shape], ts=ts, name="xattn_bwd")


def _cmul(ar, ai, xr, xi):
    return ar * xr - ai * xi, ar * xi + ai * xr


def _scan_in_place(xr_ref, xi_ref, ar, ai, *, reverse):
    s, cw = xr_ref.shape
    c = SCAN_CHUNKS
    tt = s // c
    a_r = jnp.broadcast_to(ar, (c, cw))
    a_i = jnp.broadcast_to(ai, (c, cw))
    zero = jnp.zeros((c, cw), F32)

    def row(step):
        t = (tt - 1 - step) if reverse else step
        return pl.ds(pl.multiple_of(t * c, c), c)

    def local(step, carry):
        sr, si, qr, qi = carry
        r = row(step)
        nr, ni = _cmul(a_r, a_i, sr, si)
        nr, ni = nr + xr_ref[r, :], ni + xi_ref[r, :]
        xr_ref[r, :] = nr
        xi_ref[r, :] = ni
        return (nr, ni) + _cmul(a_r, a_i, qr, qi)

    end_r, end_i, pr, pi = lax.fori_loop(0, tt, local, (zero, zero, jnp.ones((c, cw), F32), zero), unroll=SCAN_UNROLL)

    rows_id = lax.broadcasted_iota(jnp.int32, (c, cw), 0)
    car_r, car_i = zero, zero
    cur_r, cur_i = jnp.zeros((1, cw), F32), jnp.zeros((1, cw), F32)
    order = range(c - 1, -1, -1) if reverse else range(c)
    for kk in order:
        car_r = jnp.where(rows_id == kk, cur_r, car_r)
        car_i = jnp.where(rows_id == kk, cur_i, car_i)
        nr, ni = _cmul(pr[0:1], pi[0:1], cur_r, cur_i)
        cur_r = nr + end_r[kk:kk + 1]
        cur_i = ni + end_i[kk:kk + 1]

    def fix(step, carry):
        qr, qi = _cmul(a_r, a_i, *carry)
        r = row(step)
        dr, di = _cmul(qr, qi, car_r, car_i)
        xr_ref[r, :] += dr
        xi_ref[r, :] += di
        return qr, qi

    lax.fori_loop(0, tt, fix, (jnp.ones((c, cw), F32), zero), unroll=SCAN_UNROLL)


S5_ROWS = 512


def _s5_scan(v, w_r, w_i, ar, ai, *, reverse, tb, readout=None, name):
    s = v.shape[0]
    g = w_r.shape[0]
    nv, ns = SSM_PACK * SSM_GRP, SSM_PACK * SSM_P
    rows = min(S5_ROWS, s)
    dims = (((1,), (1 if tb else 0,)), ((), ()))
    n_w = 2 if readout is None else 4

    def body(v_ref, ar_ref, ai_ref, *refs):
        w = [r[...] for r in refs[:n_w]]
        xr_ref, xi_ref = refs[n_w:n_w + 2]
        for r0 in range(0, s, rows):
            vb = v_ref[r0:r0 + rows, :].astype(BF16)
            xr_ref[r0:r0 + rows, :] = lax.dot_general(vb, w[0], dims, preferred_element_type=F32)
            xi_ref[r0:r0 + rows, :] = lax.dot_general(vb, w[1], dims, preferred_element_type=F32)
        _scan_in_place(xr_ref, xi_ref, ar_ref[...], ai_ref[...], reverse=reverse)
        if readout is not None:
            y_ref = refs[n_w + 2]
            for r0 in range(0, s, rows):
                y_ref[r0:r0 + rows, :] = (
                    jnp.dot(xr_ref[r0:r0 + rows, :].astype(BF16), w[2], preferred_element_type=F32)
                    + jnp.dot(xi_ref[r0:r0 + rows, :].astype(BF16), w[3], preferred_element_type=F32))

    col = lambda j: (0, j)
    w_spec = lambda a: pl.BlockSpec((None,) + a.shape[1:], lambda j: (j, 0, 0))
    weights = [w_r, w_i] + (list(readout) if readout is not None else [])
    out_specs = [pl.BlockSpec((s, ns), col)] * 2 + ([pl.BlockSpec((s, nv), col)] if readout is not None else [])
    out_shape = [jax.ShapeDtypeStruct((s, g * ns), F32)] * 2 + (
        [jax.ShapeDtypeStruct((s, g * nv), F32)] if readout is not None else [])
    return pl.pallas_call(
        body, name=name, grid=(g,),
        in_specs=[pl.BlockSpec((s, nv), col), pl.BlockSpec((1, ns), col), pl.BlockSpec((1, ns), col)] + [w_spec(a) for a in weights],
        out_specs=out_specs, out_shape=out_shape, compiler_params=_params(("parallel",)),
    )(v, ar, ai, *weights)


def _s5_grads(lam_r, lam_i, xr, xi, u, dyc, du_d, b_r, b_i):
    s = u.shape[0]
    g = b_r.shape[0]
    nv, ns, c = SSM_PACK * SSM_GRP, SSM_PACK * SSM_P, SCAN_CHUNKS
    rows = min(S5_ROWS, s)
    slabs = rows // c
    last_slab = s // c - 1
    nt = (((1,), (1,)), ((), ()))
    tn = (((0,), (0,)), ((), ()))

    def body(lr_ref, li_ref, xr_ref, xi_ref, pr_ref, pi_ref, u_ref, dy_ref, dud_ref, br_ref, bi_ref,
             du_ref, dbr_ref, dbi_ref, dcr_ref, dci_ref, dar_ref, dai_ref):
        first = pl.program_id(1) == 0
        l_r, l_i, x_r, x_i = lr_ref[...], li_ref[...], xr_ref[...], xi_ref[...]
        lrb, lib = l_r.astype(BF16), l_i.astype(BF16)
        du_ref[...] = (dud_ref[...] + lax.dot_general(lrb, br_ref[...], nt, preferred_element_type=F32)
                       + lax.dot_general(lib, bi_ref[...], nt, preferred_element_type=F32))
        ub, dyb = u_ref[...].astype(BF16), dy_ref[...].astype(BF16)
        rows_id = lax.broadcasted_iota(jnp.int32, (c, ns), 0)

        def before(p_ref, x):
            p = p_ref[...]
            p = jnp.where(first, jnp.where(rows_id == 0, 0.0, pltpu.roll(p, 1, 0)), p)
            return jnp.concatenate([p, x[:rows - c]], axis=0)

        xp_r, xp_i = before(pr_ref, x_r), before(pi_ref, x_i)
        parts = (lax.dot_general(ub, lrb, tn, preferred_element_type=F32),
                 lax.dot_general(ub, lib, tn, preferred_element_type=F32),
                 lax.dot_general(x_r.astype(BF16), dyb, tn, preferred_element_type=F32),
                 lax.dot_general(x_i.astype(BF16), dyb, tn, preferred_element_type=F32),
                 jnp.sum(l_r * xp_r + l_i * xp_i, axis=0, keepdims=True),
                 jnp.sum(l_i * xp_r - l_r * xp_i, axis=0, keepdims=True))
        accs = (dbr_ref, dbi_ref, dcr_ref, dci_ref, dar_ref, dai_ref)

        @pl.when(first)
        def _():
            for a_ref, val in zip(accs, parts):
                a_ref[...] = val

        @pl.when(jnp.logical_not(first))
        def _():
            for a_ref, val in zip(accs, parts):
                a_ref[...] += val

    state = pl.BlockSpec((rows, ns), lambda j, k: (k, j))
    chan = pl.BlockSpec((rows, nv), lambda j, k: (k, j))
    slab = pl.BlockSpec((c, ns), lambda j, k: (jnp.where(k == 0, last_slab, k * slabs - 1), j))
    per_b = pl.BlockSpec((None, nv, ns), lambda j, k: (j, 0, 0))
    per_c = pl.BlockSpec((None, ns, nv), lambda j, k: (j, 0, 0))
    per_a = pl.BlockSpec((1, ns), lambda j, k: (0, j))
    return pl.pallas_call(
        body, name="s5_grads", grid=(g, s // rows),
        in_specs=[state, state, state, state, slab, slab, chan, chan, chan, per_b, per_b],
        out_specs=[chan, per_b, per_b, per_c, per_c, per_a, per_a],
        out_shape=[jax.ShapeDtypeStruct((s, g * nv), F32), jax.ShapeDtypeStruct((g, nv, ns), F32),
                   jax.ShapeDtypeStruct((g, nv, ns), F32), jax.ShapeDtypeStruct((g, ns, nv), F32),
                   jax.ShapeDtypeStruct((g, ns, nv), F32), jax.ShapeDtypeStruct((1, g * ns), F32),
                   jax.ShapeDtypeStruct((1, g * ns), F32)],
        compiler_params=_params(("parallel", "arbitrary")),
    )(lam_r, lam_i, xr, xi, xr, xi, u, dyc, du_d, b_r, b_i)


def _mesh_place():
    x, y, c = lax.axis_index("x"), lax.axis_index("y"), lax.axis_index("c")
    peers = []
    for k in range(1, N_DEV):
        px, py, pc = x ^ ((k >> 2) & 1), y ^ ((k >> 1) & 1), c ^ (k & 1)
        peers.append(((px, py, pc), 4 * px + 2 * py + pc))
    return 4 * x + 2 * y + c, peers


class _Exchange:
    def __init__(self, arrays, rows, *, gather, name, after=None):
        self.n_arr, self.rows, self.gather, self.name = len(arrays), rows, gather, name
        n_arr = self.n_arr
        if gather:
            assert all(r % BF16_ROWS == 0 for r in rows)
            lands = [lax.empty((N_DEV * r, a.shape[1]), a.dtype) for a, r in zip(arrays, rows)]
        else:
            lands = [lax.empty((N_DEV - 1, a.shape[0] if st is None else n, a.shape[1]), a.dtype)
                     for a, (st, n) in zip(arrays, rows)]
        has_after = after is not None

        def body(*refs):
            ins, zones = refs[:n_arr], refs[n_arr:2 * n_arr]
            sems = refs[2 * n_arr + has_after:4 * n_arr + has_after]
            token = refs[-1]
            me, peers = _mesh_place()
            for i in range(n_arr):
                for k, (pxyz, pid) in enumerate(peers):
                    if gather:
                        src = ins[i]
                        dst = zones[i].at[pl.ds(pl.multiple_of(me * rows[i], BF16_ROWS), rows[i])]
                    else:
                        stride, n = rows[i]
                        src = ins[i] if stride is None else ins[i].at[pl.ds(pl.multiple_of(pid * stride, BF16_ROWS), n)]
                        dst = zones[i].at[k]
                    pltpu.make_async_remote_copy(
                        src_ref=src, dst_ref=dst, send_sem=sems[2 * i], recv_sem=sems[2 * i + 1],
                        device_id=pxyz, device_id_type=pl.DeviceIdType.MESH).start()
            token[...] = jnp.zeros_like(token)

        hbm = pl.BlockSpec(memory_space=pltpu.HBM)
        sem = pl.BlockSpec(memory_space=pltpu.SEMAPHORE)
        args = [pltpu.with_memory_space_constraint(a, pltpu.HBM) for a in list(arrays) + lands]
        res = pl.pallas_call(
            body, name=name + "_start",
            in_specs=[hbm] * (2 * n_arr) + ([pl.BlockSpec(memory_space=pl.ANY)] if has_after else []),
            out_specs=[sem] * (2 * n_arr) + [hbm] * (2 * n_arr) + [pl.BlockSpec(memory_space=pltpu.VMEM)],
            out_shape=[pltpu.SemaphoreType.DMA(())] * (2 * n_arr) + [pltpu.HBM(a.shape, a.dtype) for a in args]
            + [jax.ShapeDtypeStruct((8, LANES), F32)],
            input_output_aliases={i: 2 * n_arr + i for i in range(2 * n_arr)},
            compiler_params=pltpu.CompilerParams(has_side_effects=pltpu.SideEffectType.DATAFLOW_SIDE_EFFECTING),
        )(*args, *([after] if has_after else []))
        self.sems, self.thru, self.token = res[:2 * n_arr], res[2 * n_arr:4 * n_arr], res[-1]

    def wait(self, after):
        n_arr = self.n_arr

        def body(*refs):
            zones, sems = refs[n_arr:2 * n_arr], refs[2 * n_arr:4 * n_arr]
            myself = (lax.axis_index("x"), lax.axis_index("y"), lax.axis_index("c"))
            for i in range(n_arr):
                seven = zones[i].at[pl.ds(0, (N_DEV - 1) * self.rows[i])] if self.gather else zones[i]
                all_seven = pltpu.make_async_remote_copy(
                    src_ref=seven, dst_ref=seven, send_sem=sems[2 * i], recv_sem=sems[2 * i + 1],
                    device_id=myself, device_id_type=pl.DeviceIdType.MESH)
                all_seven.wait_recv()
                all_seven.wait_send()

        hbm = pl.BlockSpec(memory_space=pltpu.HBM)
        sem = pl.BlockSpec(memory_space=pltpu.SEMAPHORE)
        res = pl.pallas_call(
            body, name=self.name + "_wait",
            in_specs=[hbm] * (2 * n_arr) + [sem] * (2 * n_arr) + [pl.BlockSpec(memory_space=pl.ANY)],
            out_specs=[hbm] * (2 * n_arr), out_shape=[pltpu.HBM(a.shape, a.dtype) for a in self.thru],
            input_output_aliases={i: i for i in range(2 * n_arr)},
            compiler_params=pltpu.CompilerParams(has_side_effects=pltpu.SideEffectType.DATAFLOW_SIDE_EFFECTING),
        )(*self.thru, *self.sems, after)
        return res[:n_arr], res[n_arr:]


def _my_slot():
    me = 4 * lax.axis_index("x") + 2 * lax.axis_index("y") + lax.axis_index("c")
    return me.astype(jnp.int32).reshape(1)


def _place_own(gathered, block, me, *, name):
    r, c = block.shape

    def body(me_ref, b_ref, g_ref, o_ref):
        o_ref[...] = b_ref[...]

    return pl.pallas_call(
        body, name=name, out_shape=jax.ShapeDtypeStruct(gathered.shape, gathered.dtype),
        grid_spec=pltpu.PrefetchScalarGridSpec(
            num_scalar_prefetch=1, grid=(1,),
            in_specs=[pl.BlockSpec((r, c), lambda i, me_ref: (0, 0)), pl.BlockSpec(memory_space=pl.ANY)],
            out_specs=pl.BlockSpec((r, c), lambda i, me_ref: (me_ref[0], 0))),
        input_output_aliases={2: 0}, compiler_params=_params(("arbitrary",)),
    )(me, block, gathered)


def _elementwise_tiles(r, c):
    if r % 128 == 0:
        return 128, c
    return r, (256 if c % 256 == 0 else c)


def _adamw_math(g, w, m, v):
    nm = ADAM_B1 * m + (1.0 - ADAM_B1) * g
    nv = ADAM_B2 * v + (1.0 - ADAM_B2) * (g * g)
    m_hat = nm / (1.0 - ADAM_B1 ** ADAM_STEP)
    v_hat = nv / (1.0 - ADAM_B2 ** ADAM_STEP)
    return -ADAM_LR * (m_hat / (jnp.sqrt(v_hat) + ADAM_EPS) + ADAM_WD * w), nm, nv


def _sum_parts(me_ref, own_ref, p_ref, r):
    own = own_ref[...].astype(F32)
    g = None
    for d in range(N_DEV):
        k = jnp.bitwise_xor(me_ref[0], d)
        term = jnp.where(k == 0, own, p_ref[jnp.maximum(k, 1) - 1].astype(F32))
        g = term if g is None else g + term
    return g[0:r, :]


def _sum_adamw(me, sent, stride, parts, r, w=None, m=None, v=None, *, name):
    _, own_rows, cdim = parts.shape
    assert stride is None or stride == own_rows
    tc = 256 if cdim % 256 == 0 else cdim
    update = w is not None

    def body(me_ref, own_ref, p_ref, *refs):
        g = _sum_parts(me_ref, own_ref, p_ref, r)
        if update:
            w_ref, m_ref, v_ref, g_ref, d_ref, nm_ref, nv_ref = refs
            d_ref[...], nm_ref[...], nv_ref[...] = _adamw_math(g, w_ref[...], m_ref[...], v_ref[...])
        else:
            g_ref, = refs
        g_ref[...] = g

    blk = pl.BlockSpec((r, tc), lambda j, me_ref: (0, j))
    own_spec = pl.BlockSpec((own_rows, tc), (lambda j, me_ref: (0, j)) if stride is None else (lambda j, me_ref: (me_ref[0], j)))
    n_out = 4 if update else 1
    res = pl.pallas_call(
        body, name=name, out_shape=[jax.ShapeDtypeStruct((r, cdim), F32)] * n_out,
        grid_spec=pltpu.PrefetchScalarGridSpec(
            num_scalar_prefetch=1, grid=(cdim // tc,),
            in_specs=[own_spec, pl.BlockSpec((N_DEV - 1, own_rows, tc), lambda j, me_ref: (0, 0, j))]
            + ([blk] * 3 if update else []),
            out_specs=[blk] * n_out),
        compiler_params=_params(("parallel",)),
    )(me, sent, parts, *((w, m, v) if update else ()))
    return list(res)


def _adamw(g, w, m, v, *, name):
    r, cdim = w.shape
    tr, tc = _elementwise_tiles(r, cdim)

    def body(g_ref, w_ref, m_ref, v_ref, d_ref, nm_ref, nv_ref):
        d_ref[...], nm_ref[...], nv_ref[...] = _adamw_math(g_ref[...], w_ref[...], m_ref[...], v_ref[...])

    blk = pl.BlockSpec((tr, tc), lambda i, j: (i, j))
    return list(pl.pallas_call(
        body, name=name, grid=(r // tr, cdim // tc), in_specs=[blk] * 4,
        out_specs=[blk] * 3, out_shape=[jax.ShapeDtypeStruct((r, cdim), F32)] * 3,
        compiler_params=_params(("parallel", "parallel")),
    )(g, w, m, v))


SHARD_ROWS_P = {n: (FF_SHARD_P if 'ffn' in n else IN_SHARD_P if n == 'w_in' else None) for n in SHARDED}


def _to_exchange_layout(name, shard):
    t = shard.T if SHARD_AXIS[name] == 1 else shard
    pad = SHARD_ROWS_P[name]
    return t if pad is None else jnp.pad(t, ((0, pad - t.shape[0]), (0, 0)))


def _expand_w_in(wt):
    wt = wt.reshape(N_DEV, IN_SHARD_P, D)[:, :IN_SHARD].reshape(IN_W, D)
    o = Q_RANK + KV_RANK
    kr1, kr2 = wt[o:o + ROPE // 2], wt[o + ROPE // 2:o + ROPE]
    z = jnp.zeros((LANES - ROPE, D), wt.dtype)
    return jnp.concatenate([wt[:o], wt[o + ROPE:], kr1, kr2, z, -kr2, kr1, z], axis=0)


def _expand_w_uq(wt):
    w = wt.reshape(H, QK, Q_RANK)
    z = jnp.zeros((H, LANES - ROPE, Q_RANK), w.dtype)
    q1, q2 = w[:, NOPE:NOPE + ROPE // 2], w[:, NOPE + ROPE // 2:]
    return jnp.concatenate([w[:, :NOPE].reshape(H * NOPE, Q_RANK),
                            jnp.concatenate([q1, q2, z], axis=1).reshape(H * LANES, Q_RANK),
                            jnp.concatenate([-q2, q1, z], axis=1).reshape(H * LANES, Q_RANK)], axis=0)


def _layout_qk_gain(g):
    g = g.reshape(QK)
    g1, g2, z = g[NOPE:NOPE + ROPE // 2], g[NOPE + ROPE // 2:], jnp.zeros((LANES - ROPE,), g.dtype)
    return jnp.stack([g[:NOPE], jnp.concatenate([g1, g2, z]), jnp.concatenate([g2, g1, z])])


def _rep16(a):
    return jnp.repeat(a, SSM_GRP, axis=0)


def _layout_ssm_in(a_re, a_im, log_dt, b_re, b_im):
    b_r = jnp.transpose(b_re, (0, 2, 1)).reshape(SSM_G * SSM_GRP, SSM_P)
    b_i = jnp.transpose(b_im, (0, 2, 1)).reshape(SSM_G * SSM_GRP, SSM_P)
    ldt = jnp.broadcast_to(log_dt.reshape(SSM_G, 1), (SSM_G, SSM_P))
    return _rep16(a_re), _rep16(a_im), _rep16(ldt), b_r, b_i


def _block_diag_b(bb):
    eye = jnp.eye(SSM_PACK, dtype=bb.dtype)
    b5 = bb.reshape(SSM_G // SSM_PACK, SSM_PACK, SSM_GRP, 1, SSM_P) * eye[None, :, None, :, None]
    return b5.reshape(SSM_G // SSM_PACK, SSM_PACK * SSM_GRP, SSM_PACK * SSM_P)


def _block_diag_c(cc):
    eye = jnp.eye(SSM_PACK, dtype=cc.dtype)
    c5 = jnp.transpose(cc, (0, 2, 1)).reshape(SSM_G // SSM_PACK, SSM_PACK, SSM_P, 1, SSM_GRP) * eye[None, :, None, :, None]
    return c5.reshape(SSM_G // SSM_PACK, SSM_PACK * SSM_P, SSM_PACK * SSM_GRP)


def _time_perm(a, inverse=False):
    s, w = a.shape
    c = SCAN_CHUNKS
    if inverse:
        return jnp.transpose(a.reshape(s // c, c, w), (1, 0, 2)).reshape(s, w)
    return jnp.transpose(a.reshape(c, s // c, w), (1, 0, 2)).reshape(s, w)


class _Weights:
    def __init__(self, groups=(), landed=None, me=None):
        self.groups, self.landed, self.me = list(groups), dict(landed or {}), me

    def get(self, name, after):
        if name not in self.landed:
            names, exchange = next(g for g in self.groups if name in g[0])
            for n, block, gathered in zip(names, *exchange.wait(after)):
                self.landed[n] = _place_own(gathered, block, self.me, name="place_" + n)
        return self.landed[name]

    def __getitem__(self, name):
        return self.landed[name]


def _ffn_gate_up(h, w_gt, w_ut, *, name, tm=512, tn=1408):
    s, k = h.shape
    n = w_gt.shape[0]
    tm, tn = min(tm, s), _tile(n, tn)
    dims = (((1,), (1,)), ((), ()))

    def body(h_ref, wg_ref, wu_ref, g_ref, u_ref, a_ref):
        hb = h_ref[...].astype(BF16)
        gate = lax.dot_general(hb, wg_ref[...], dims, preferred_element_type=F32)
        up = lax.dot_general(hb, wu_ref[...], dims, preferred_element_type=F32)
        g_ref[...] = gate.astype(BF16)
        u_ref[...] = up.astype(BF16)
        a_ref[...] = _f_swiglu(gate, up)

    w_spec = pl.BlockSpec((tn, k), lambda j, i: (j, 0))
    o_spec = pl.BlockSpec((tm, tn), lambda j, i: (i, j))
    return pl.pallas_call(
        body, name=name, grid=(n // tn, s // tm), in_specs=[pl.BlockSpec((tm, k), lambda j, i: (i, 0)), w_spec, w_spec],
        out_specs=[o_spec] * 3, out_shape=[jax.ShapeDtypeStruct((s, n), BF16)] * 3,
        compiler_params=_params(("parallel", "parallel")),
    )(h, w_gt, w_ut)


def _ffn_dgate_dup(dx_out, w_d, gate, up, *, name, tm=512, tn=1408, deps=()):
    s, k = dx_out.shape
    n = w_d.shape[0]
    tm, tn = min(tm, s), _tile(n, tn)
    deps = [d for d in deps if d is not None]

    def body(dx_ref, wd_ref, g_ref, u_ref, *refs):
        dg_ref, du_ref = refs[len(deps):]
        dact = 0.5 * lax.dot_general(dx_ref[...].astype(BF16), wd_ref[...], (((1,), (1,)), ((), ())),
                                     preferred_element_type=F32)
        _, vjp = jax.vjp(_f_swiglu, g_ref[...].astype(F32), u_ref[...].astype(F32))
        dgate, dup = vjp(dact.astype(BF16))
        dg_ref[...] = dgate.astype(BF16)
        du_ref[...] = dup.astype(BF16)

    o_spec = pl.BlockSpec((tm, tn), lambda j, i: (i, j))
    return pl.pallas_call(
        body, name=name, grid=(n // tn, s // tm),
        in_specs=[pl.BlockSpec((tm, k), lambda j, i: (i, 0)), pl.BlockSpec((tn, k), lambda j, i: (j, 0)), o_spec, o_spec]
        + [pl.BlockSpec(d.shape, lambda j, i: (0, 0)) for d in deps],
        out_specs=[o_spec] * 2, out_shape=[jax.ShapeDtypeStruct((s, n), BF16)] * 2,
        compiler_params=_params(("parallel", "parallel")),
    )(dx_out, w_d, gate, up, *deps)


def _ffn_dh(dgate, dup, w_gt, w_ut, *, name, tm=512):
    s, k = dgate.shape
    n = w_gt.shape[1]
    tm = min(tm, s)

    def body(dg_ref, du_ref, wg_ref, wu_ref, o_ref):
        o_ref[...] = (jnp.dot(dg_ref[...], wg_ref[...], preferred_element_type=F32)
                      + jnp.dot(du_ref[...], wu_ref[...], preferred_element_type=F32)).astype(o_ref.dtype)

    a_spec = pl.BlockSpec((tm, k), lambda i: (i, 0))
    w_spec = pl.BlockSpec((k, n), lambda i: (0, 0))
    return pl.pallas_call(
        body, name=name, grid=(s // tm,), in_specs=[a_spec, a_spec, w_spec, w_spec],
        out_specs=pl.BlockSpec((tm, n), lambda i: (i, 0)), out_shape=jax.ShapeDtypeStruct((s, n), BF16),
        compiler_params=_params(("parallel",)),
    )(dgate, dup, w_gt, w_ut)


def _ffn_fwd(x, g, wc, tag, deps=()):
    h = _rowwise(_f_norm, [x], [g], [(D, BF16)], name=tag + "_norm", deps=deps)[0]
    gate, up, act = _ffn_gate_up(h, wc.get(tag + '_w_gate', h), wc[tag + '_w_up'], name=tag + "_gate_up")
    x_out = _mm(act, wc.get(tag + '_w_down', act), res=x, scale=0.5, name=tag + "_down")
    return x_out, (h, gate, up, act)


def _ffn_bwd(x, g, wc, saved, dx_out, tag, send, deps=()):
    h, gate, up, act = saved
    w_gt, w_ut, w_d = (wc.get(tag + n, h) for n in ('_w_gate', '_w_up', '_w_down'))
    d_d = _mm(act, dx_out, ta=True, scale=0.5, out_dtype=GRAD_DTYPE, name=tag + "_dwdown", deps=deps)
    token = send({tag + '_w_down': d_d})
    dgate, dup = _ffn_dgate_dup(dx_out, w_d, gate, up, name=tag + "_dgate_dup", deps=[token])
    d_gt = _mm(dgate, h, ta=True, out_dtype=GRAD_DTYPE, name=tag + "_dwgate")
    token = send({tag + '_w_gate': d_gt})
    d_ut = _mm(dup, h, ta=True, out_dtype=GRAD_DTYPE, name=tag + "_dwup", deps=[token])
    token = send({tag + '_w_up': d_ut})
    dh = _ffn_dh(dgate, dup, w_gt, w_ut, name=tag + "_dh")
    dx, dg = _rowwise_bwd(_f_norm, [x], [g], [dh], row_grads={0: F32}, const_grads=[0], adds={0: dx_out},
                          name=tag + "_norm_bwd", deps=[token])
    return dx, dg


def _local_step(x, mem, cos, sin, target, wc, ws, send, deps=(), send_small=None):
    gs = {}

    x1, sv1 = _ffn_fwd(x, ws['ffn1_norm'], wc, "ffn1", deps=deps)

    h2 = _rowwise(_f_norm, [x1], [ws['mix_norm']], [(D, BF16)], name="mix_norm")[0]
    w_in_raw, w_uq_raw = wc.get('w_in', h2), wc.get('mla_w_uq', h2)
    w_in_e = _expand_w_in(w_in_raw)
    w_uq_e = _expand_w_uq(w_uq_raw)
    proj = _mm(h2, w_in_e, tb=True, name="w_in")
    c_q, c_kv = _rowwise(_f_prep1, [proj], [ws['q_norm'], ws['kv_norm']], [(Q_RANK, BF16), (KV_RANK, BF16)], name="mla_prep1")
    qall = _mm(c_q, w_uq_e, tb=True, name="w_uq")
    kv = _mm(c_kv, wc['mla_w_ukv'], tb=True, name="w_ukv")
    kr = _rowwise(_f_kr, [proj], [], [(2 * LANES, F32)], name="mla_kr")[0]
    q, k, v = _prep2_fwd(qall, kv, kr, cos, sin, ws['qk_gq'], ws['qk_gk'])
    o_mla, lse = _attn_fwd(q, k, v)

    u = proj[:, Q_RANK + KV_RANK:Q_RANK + KV_RANK + SSM_W]
    u_p = _time_perm(u)
    disc_in = [ws['ssm_lr'], ws['ssm_li'], ws['ssm_ldt'], ws['ssm_br'], ws['ssm_bi']]
    ar16, ai16, bbr, bbi = _rowwise(_f_disc, disc_in, [], [(SSM_P, F32)] * 4, name="s5_disc")
    a_r = ar16[::SSM_GRP].reshape(1, SSM_N)
    a_i = ai16[::SSM_GRP].reshape(1, SSM_N)
    bblk_r, bblk_i = _block_diag_b(bbr).astype(BF16), _block_diag_b(bbi).astype(BF16)
    cblk_r, cblk_i = _block_diag_c(ws['ssm_cr']).astype(BF16), _block_diag_c(-ws['ssm_ci']).astype(BF16)
    xr, xi, yc = _s5_scan(u_p, bblk_r, bblk_i, a_r, a_i, reverse=False, tb=False, readout=(cblk_r, cblk_i),
                          name="s5_scan_fwd")
    g_p = _rowwise(_f_s5_gelu, [yc, u_p], [ws['ssm_d']], [(SSM_W, F32)], name="s5_gelu")[0]
    z_p = _mm(g_p, wc['ssm_w_glu'], name="s5_glu")
    g_t, z_t = _time_perm(g_p, inverse=True), _time_perm(z_p, inverse=True)
    on_consts = [ws['ssm_b_glu'], ws['out_norm_mla'], ws['out_norm_ssm']]
    ycat = _rowwise(_f_outnorm, [o_mla, g_t, z_t], on_consts, [(D, BF16)], name="out_norm")[0]
    x2 = _mm(ycat, wc['w_o'], res=x1, name="w_o")

    hx = _rowwise(_f_norm, [x2], [ws['xattn_norm']], [(D, BF16)], name="xattn_norm")[0]
    xq = _mm(hx, wc['xattn_w_q'], name="xattn_q")
    mn = _rowwise(_f_norm, [mem], [ws['mem_norm']], [(D, BF16)], name="mem_norm")[0]
    kvm = _mm(mn, wc['xattn_w_kv'], name="xattn_kv")
    xkn, xv = _rowwise(_f_memk, [kvm], [ws['xattn_k_norm']], [(H * XH, BF16), (H * XH, BF16)], name="xattn_knorm")
    xo = _xattn_fwd(xq, xkn, xv, ws['xattn_q_norm'])
    x3 = _mm(xo, wc['xattn_w_o'], tb=True, res=x2, name="xattn_o")

    x4, sv2 = _ffn_fwd(x3, ws['ffn2_norm'], wc, "ffn2")

    def f_loss(yb, tb):
        err = yb - tb
        return err * (1.0 / D), jnp.broadcast_to(jnp.sum(jnp.sum(err * err, axis=1, keepdims=True), axis=0, keepdims=True) * (0.5 / D), (1, LANES))

    dx4, loss = _rowwise(f_loss, [x4, target], [], [(D, F32)], [(1, LANES)], name="loss")

    dx3, gs['ffn2_norm'] = _ffn_bwd(x3, ws['ffn2_norm'], wc, sv2, dx4, "ffn2", send)

    dxo = _mm(dx3, wc['xattn_w_o'], out_dtype=BF16, name="xattn_o_dx")
    send({'xattn_w_o': _mm(dx3, xo, ta=True, out_dtype=GRAD_DTYPE, name="xattn_o_dw")})
    dxq, dxkn, dxv, gs['xattn_q_norm'] = _xattn_bwd(xq, xkn, xv, ws['xattn_q_norm'], dxo)
    dkvm, gs['xattn_k_norm'] = _rowwise_bwd(_f_memk, [kvm], [ws['xattn_k_norm']], [dxkn, dxv], row_grads={0: BF16},
                                            const_grads=[0], name="xattn_knorm_bwd")
    send({'xattn_w_kv': _mm(mn, dkvm, ta=True, out_dtype=GRAD_DTYPE, name="xattn_kv_dw")})
    dmn = _mm(dkvm, wc['xattn_w_kv'], tb=True, out_dtype=BF16, name="xattn_kv_dx")
    gs['mem_norm'] = _rowwise_bwd(_f_norm, [mem], [ws['mem_norm']], [dmn], row_grads={}, const_grads=[0], name="mem_norm_bwd")[0]
    token = send({'xattn_w_q': _mm(hx, dxq, ta=True, out_dtype=GRAD_DTYPE, name="xattn_q_dw")})
    dhx = _mm(dxq, wc['xattn_w_q'], tb=True, out_dtype=BF16, name="xattn_q_dx")
    dx2, gs['xattn_norm'] = _rowwise_bwd(_f_norm, [x2], [ws['xattn_norm']], [dhx], row_grads={0: F32}, const_grads=[0],
                                         adds={0: dx3}, name="xattn_norm_bwd", deps=[token])

    dycat = _mm(dx2, wc['w_o'], tb=True, out_dtype=BF16, name="w_o_dx")
    send({'w_o': _mm(ycat, dx2, ta=True, out_dtype=GRAD_DTYPE, name="w_o_dw")})
    do_mla, dg_t, dz_t, gs['ssm_b_glu'], gs['out_norm_mla'], gs['out_norm_ssm'] = _rowwise_bwd(
        _f_outnorm, [o_mla, g_t, z_t], on_consts, [dycat], row_grads={0: F32, 1: F32, 2: BF16}, const_grads=[0, 1, 2],
        name="out_norm_bwd")

    dz_p, dg_p = _time_perm(dz_t), _time_perm(dg_t)
    send({'ssm_w_glu': _mm(g_p, dz_p, ta=True, out_dtype=GRAD_DTYPE, name="s5_glu_dw")})
    dg_p = _mm(dz_p, wc['ssm_w_glu'], tb=True, res=dg_p, name="s5_glu_dx")
    dyc, du_d, gs['ssm_d'] = _rowwise_bwd(_f_s5_gelu, [yc, u_p], [ws['ssm_d']], [dg_p], row_grads={0: BF16, 1: F32},
                                          const_grads=[0], name="s5_gelu_bwd")
    lam_r, lam_i = _s5_scan(dyc, cblk_r, cblk_i, a_r, -a_i, reverse=True, tb=True, name="s5_scan_bwd")
    du_p, d_bblk_r, d_bblk_i, d_cblk_r, d_cblk_i, d_ar, d_ai = _s5_grads(lam_r, lam_i, xr, xi, u_p, dyc, du_d,
                                                                        bblk_r, bblk_i)
    du = _time_perm(du_p, inverse=True)
    gs['ssm_cr'] = jax.linear_transpose(_block_diag_c, ws['ssm_cr'])(d_cblk_r)[0]
    gs['ssm_ci'] = -jax.linear_transpose(_block_diag_c, ws['ssm_ci'])(d_cblk_i)[0]
    d_bbr = jax.linear_transpose(_block_diag_b, bbr)(d_bblk_r)[0]
    d_bbi = jax.linear_transpose(_block_diag_b, bbi)(d_bblk_i)[0]
    d_ar16 = jnp.zeros((SSM_G * SSM_GRP, SSM_P), F32).at[::SSM_GRP].set(d_ar.reshape(SSM_G, SSM_P))
    d_ai16 = jnp.zeros((SSM_G * SSM_GRP, SSM_P), F32).at[::SSM_GRP].set(d_ai.reshape(SSM_G, SSM_P))
    gs['ssm_lr'], gs['ssm_li'], gs['ssm_ldt'], gs['ssm_br'], gs['ssm_bi'] = _rowwise_bwd(
        _f_disc, disc_in, [], [d_ar16, d_ai16, d_bbr, d_bbi], row_grads={i: F32 for i in range(5)}, const_grads=[],
        name="s5_disc_bwd")

    delta, do_b = _rowwise(_f_delta, [do_mla, o_mla], [], [(H * LANES, F32), (H * VD, BF16)], name="mla_delta")
    dq, dk, dv = _attn_bwd(q, k, v, do_b, lse, delta)
    dqall, dkv, dkr, gs['qk_gq'], gs['qk_gk'] = _prep2_bwd(qall, kv, kr, cos, sin, ws['qk_gq'], ws['qk_gk'], dq, dk, dv)
    d_w_uq_e = _mm(dqall, c_q, ta=True, name="w_uq_dw")
    send({'mla_w_uq': jax.linear_transpose(_expand_w_uq, jax.ShapeDtypeStruct(w_uq_raw.shape, F32))(d_w_uq_e)[0]})
    dc_q = _mm(dqall, w_uq_e, out_dtype=BF16, name="w_uq_dx")
    send({'mla_w_ukv': _mm(dkv, c_kv, ta=True, out_dtype=GRAD_DTYPE, name="w_ukv_dw")})
    dc_kv = _mm(dkv, wc['mla_w_ukv'], out_dtype=BF16, name="w_ukv_dx")

    def f_prep1_bwd(pb, dcq, dckv, dub, dkrb, gq, gkv):
        _, vjp = jax.vjp(_f_prep1, pb[:, :Q_RANK + KV_RANK], gq, gkv)
        dpa, dgq, dgkv = vjp((dcq.astype(BF16), dckv.astype(BF16)))
        return jnp.concatenate([dpa, dub, dkrb], axis=-1), dgq, dgkv

    dproj, gs['q_norm'], gs['kv_norm'] = _rowwise(
        f_prep1_bwd, [proj, dc_q, dc_kv, du, dkr], [ws['q_norm'], ws['kv_norm']], [(IN_WP, BF16)],
        [(1, Q_RANK), (1, KV_RANK)], name="mla_prep1_bwd")
    d_w_in_e = _mm(dproj, h2, ta=True, name="w_in_dw")
    token = send({'w_in': jax.linear_transpose(_expand_w_in, jax.ShapeDtypeStruct(w_in_raw.shape, F32))(d_w_in_e)[0]})
    dh2 = _mm(dproj, w_in_e, out_dtype=BF16, name="w_in_dx")
    dx1, gs['mix_norm'] = _rowwise_bwd(_f_norm, [x1], [ws['mix_norm']], [dh2], row_grads={0: F32}, const_grads=[0],
                                       adds={0: dx2}, name="mix_norm_bwd", deps=[token])

    token = send_small(gs, loss) if send_small is not None else None
    dx0, gs['ffn1_norm'] = _ffn_bwd(x, ws['ffn1_norm'], wc, sv1, dx1, "ffn1", send, deps=[token])
    return loss, dx0, gs


def _prep2_fwd(qall, kv, kr, cos, sin, gq, gk):
    return _rowwise(_f_prep2, [qall, kv, kr, cos, sin], [gq, gk], [(H * HQ, BF16), (H * HQ, BF16), (H * VD, BF16)],
                    ts=256, name="mla_prep2")


def _prep2_bwd(qall, kv, kr, cos, sin, gq, gk, dq, dk, dv):
    return _rowwise_bwd(_f_prep2, [qall, kv, kr, cos, sin], [gq, gk], [dq, dk, dv], row_grads={0: BF16, 1: BF16, 2: F32},
                        const_grads=[0, 1], ts=256, name="mla_prep2_bwd")


def _rope_tables(pos):
    half = ROPE // 2
    inv = ROPE_THETA ** (-jnp.arange(half, dtype=F32) / half)
    ang = pos.astype(F32)[:, None] * inv[None, :]
    z = jnp.zeros((pos.shape[0], LANES - ROPE), F32)
    cos, sin = jnp.cos(ang), jnp.sin(ang)
    return jnp.concatenate([cos, cos, z], axis=-1), jnp.concatenate([sin, sin, z], axis=-1)


def _small_layout(p):
    lr, li, ldt, br, bi = _layout_ssm_in(p['ssm_a_re'], p['ssm_a_im'], p['ssm_log_dt'], p['ssm_b_re'], p['ssm_b_im'])
    return {
        'ffn1_norm': p['ffn1_norm'].reshape(1, D), 'mix_norm': p['mix_norm'].reshape(1, D),
        'q_norm': p['mla_q_norm'].reshape(1, Q_RANK), 'kv_norm': p['mla_kv_norm'].reshape(1, KV_RANK),
        'qk_gq': _layout_qk_gain(p['mla_qk_norm_q']), 'qk_gk': _layout_qk_gain(p['mla_qk_norm_k']),
        'ssm_lr': lr, 'ssm_li': li, 'ssm_ldt': ldt, 'ssm_br': br, 'ssm_bi': bi,
        'ssm_cr': p['ssm_c_re'], 'ssm_ci': p['ssm_c_im'], 'ssm_d': p['ssm_d'].reshape(1, SSM_W),
        'ssm_b_glu': p['ssm_b_glu'].reshape(1, SSM_W),
        'out_norm_mla': p['out_norm_mla'].reshape(1, SSM_W), 'out_norm_ssm': p['out_norm_ssm'].reshape(1, SSM_W),
        'xattn_norm': p['xattn_norm'].reshape(1, D), 'mem_norm': p['mem_norm'].reshape(1, D),
        'xattn_q_norm': p['xattn_q_norm'].reshape(1, XH), 'xattn_k_norm': p['xattn_k_norm'].reshape(1, XH),
        'ffn2_norm': p['ffn2_norm'].reshape(1, D),
    }


def _pack(arrs, rows):
    flat = jnp.concatenate([a.reshape(-1) for a in arrs])
    return jnp.pad(flat, (0, rows * D - flat.shape[0])).reshape(rows, D)


def _unpack(flat, shapes):
    flat = flat.reshape(-1)
    out, off = [], 0
    for sh in shapes:
        n = int(np.prod(sh))
        out.append(flat[off:off + n].reshape(sh))
        off += n
    return out


def kernel(x, mem, positions, ffn1_norm, ffn1_w_gate, ffn1_w_up, ffn1_w_down, mix_norm, w_in, mla_q_norm, mla_w_uq, mla_kv_norm, mla_w_ukv, mla_qk_norm_q, mla_qk_norm_k, ssm_a_re, ssm_a_im, ssm_log_dt, ssm_b_re, ssm_b_im, ssm_c_re, ssm_c_im, ssm_d, ssm_w_glu, ssm_b_glu, out_norm_mla, out_norm_ssm, w_o, xattn_norm, mem_norm, xattn_w_q, xattn_w_kv, xattn_q_norm, xattn_k_norm, xattn_w_o, ffn2_norm, ffn2_w_gate, ffn2_w_up, ffn2_w_down, loss_target, m_ffn1_norm, m_ffn1_w_gate, m_ffn1_w_up, m_ffn1_w_down, m_mix_norm, m_w_in, m_mla_q_norm, m_mla_w_uq, m_mla_kv_norm, m_mla_w_ukv, m_mla_qk_norm_q, m_mla_qk_norm_k, m_ssm_a_re, m_ssm_a_im, m_ssm_log_dt, m_ssm_b_re, m_ssm_b_im, m_ssm_c_re, m_ssm_c_im, m_ssm_d, m_ssm_w_glu, m_ssm_b_glu, m_out_norm_mla, m_out_norm_ssm, m_w_o, m_xattn_norm, m_mem_norm, m_xattn_w_q, m_xattn_w_kv, m_xattn_q_norm, m_xattn_k_norm, m_xattn_w_o, m_ffn2_norm, m_ffn2_w_gate, m_ffn2_w_up, m_ffn2_w_down, v_ffn1_norm, v_ffn1_w_gate, v_ffn1_w_up, v_ffn1_w_down, v_mix_norm, v_w_in, v_mla_q_norm, v_mla_w_uq, v_mla_kv_norm, v_mla_w_ukv, v_mla_qk_norm_q, v_mla_qk_norm_k, v_ssm_a_re, v_ssm_a_im, v_ssm_log_dt, v_ssm_b_re, v_ssm_b_im, v_ssm_c_re, v_ssm_c_im, v_ssm_d, v_ssm_w_glu, v_ssm_b_glu, v_out_norm_mla, v_out_norm_ssm, v_w_o, v_xattn_norm, v_mem_norm, v_xattn_w_q, v_xattn_w_kv, v_xattn_q_norm, v_xattn_k_norm, v_xattn_w_o, v_ffn2_norm, v_ffn2_w_gate, v_ffn2_w_up, v_ffn2_w_down):
    args = dict(locals())
    w = {n: args[n] for n in WEIGHTS}
    mom = {n: args['m_' + n] for n in WEIGHTS}
    var = {n: args['v_' + n] for n in WEIGHTS}
    return _step(x, mem, positions, loss_target, w, mom, var)


GATHER_GROUPS = [('ffn1_gu', ['ffn1_w_gate', 'ffn1_w_up']), ('ffn1_down', ['ffn1_w_down']),
                 ('mix', ['w_in', 'mla_w_uq', 'mla_w_ukv', 'ssm_w_glu', 'w_o', 'xattn_w_q', 'xattn_w_kv', 'xattn_w_o']),
                 ('ffn2', ['ffn2_w_gate', 'ffn2_w_up', 'ffn2_w_down'])]
SCATTER_GROUPS = [('ffn2_down', ['ffn2_w_down']), ('ffn2_gate', ['ffn2_w_gate']), ('ffn2_up', ['ffn2_w_up']),
                  ('xattn', ['xattn_w_o', 'xattn_w_kv', 'xattn_w_q']),
                  ('mix', ['w_o', 'ssm_w_glu', 'mla_w_uq', 'mla_w_ukv', 'w_in']),
                  ('ffn1_down', ['ffn1_w_down']), ('ffn1_gate', ['ffn1_w_gate']), ('ffn1_up', ['ffn1_w_up'])]


def _step(x, mem, positions, loss_target, w, mom, var):
    blocks = {n: _to_exchange_layout(n, w[n][0]).astype(BF16) for n in SHARDED}
    gathers, token = [], None
    for tag, names in GATHER_GROUPS:
        ex = _Exchange([blocks[n] for n in names], [blocks[n].shape[0] for n in names], gather=True,
                       name="gather_" + tag, after=token)
        gathers.append((names, ex))
        token = ex.token
    me = _my_slot()
    wc = _Weights(gathers, me=me)

    rows = {n: (blocks[n].shape[0], blocks[n].shape[0]) for n in SHARDED}
    ready, scatters = {}, []

    def send(grads):
        ready.update({n: g.astype(GRAD_DTYPE) for n, g in grads.items()})
        for tag, names in SCATTER_GROUPS:
            if all(n in ready for n in names) and not any(t == tag for t, _, _ in scatters):
                ex = _Exchange([ready[n] for n in names], [rows[n] for n in names], gather=False, name="scatter_" + tag)
                scatters.append((tag, names, ex))
                return ex.token
        return None

    small = {n: w[n][0] for n in SMALL}
    small_shapes = [small[n].shape for n in SMALL]
    n_small = sum(int(np.prod(sh)) for sh in small_shapes) + 1
    rows_small = -(-n_small // (8 * D)) * 8
    small_sent = []

    def send_small(gs, loss):
        known = dict(gs, ffn1_norm=jnp.zeros((1, D), F32))
        g_small = jax.linear_transpose(_small_layout, {n: jax.ShapeDtypeStruct(small[n].shape, F32) for n in SMALL})(known)[0]
        pack = _pack([g_small[n] for n in SMALL] + [loss[0, :1]], rows_small)
        small_sent.append(_Exchange([pack], [(None, rows_small)], gather=False, name="scatter_small"))
        return small_sent[0].token

    ws = _small_layout(small)
    cos, sin = _rope_tables(positions[0])
    loss, dx, gs = _local_step(x[0], mem[0], cos, sin, loss_target[0], wc, ws, send, deps=[token], send_small=send_small)
    pad8 = lambda a: jnp.pad(a.reshape(1, D), ((0, 7), (0, 0)))
    last_ex = _Exchange([pad8(gs['ffn1_norm'])], [(None, 8)], gather=False, name="scatter_last")

    out, after = {}, dx
    for _, names, ex in scatters:
        for n, sent, p in zip(names, *ex.wait(after)):
            r = w[n][0].shape[SHARD_AXIS[n]]
            if SHARD_AXIS[n] == 0:
                out[n] = _sum_adamw(me, sent, rows[n][0], p, r, w[n][0], mom[n][0], var[n][0], name="adamw_" + n)
            else:
                g = _sum_adamw(me, sent, rows[n][0], p, r, name="sum_" + n)[0].T
                out[n] = [g] + _adamw(g, w[n][0], mom[n][0], var[n][0], name="adamw_" + n)
        after = out[names[-1]][1]
    state = [_pack([t[n][0] for n in SMALL], rows_small) for t in (w, mom, var)]
    sent, p = small_sent[0].wait(after)
    small_out = _sum_adamw(me, sent[0], None, p[0], rows_small, *state, name="adamw_small")
    loss_total = small_out[0].reshape(-1)[n_small - 1]
    for n, vals in zip(SMALL, zip(*[_unpack(flat, small_shapes) for flat in small_out])):
        out[n] = vals
    sent, p = last_ex.wait(small_out[1])
    last_out = _sum_adamw(me, sent[0], None, p[0], 8, *[pad8(t['ffn1_norm'][0]) for t in (w, mom, var)], name="adamw_last")
    out['ffn1_norm'] = [o[0] for o in last_out]
    outs = [out[n][i][None] for i in range(4) for n in WEIGHTS]
    return (loss_total, dx[None], *outs)
```

```python
import math

import jax
import jax.numpy as jnp
import numpy as np
from jax import lax
from jax.experimental import pallas as pl
from jax.experimental.pallas import tpu as pltpu

F32 = jnp.float32
BF16 = jnp.bfloat16

N_DEV = 8
D = 1024
D_FF = 2752
D_FFP = 2816
MEM_LEN = 256
H = 4
Q_RANK, KV_RANK, NOPE, ROPE, VD = 384, 256, 128, 64, 128
QK = NOPE + ROPE
HQ = 2 * 128
SSM_W, SSM_G, SSM_GRP, SSM_P = 512, 32, 16, 64
SSM_N = SSM_G * SSM_P
SSM_PACK = 8
IN_W = 1216
IN_WP = 1408
XH = 128
EPS = 1e-6
LN2 = math.log(2.0)
ROPE_THETA = 10000.0
SCAN_CHUNKS = 8
SCAN_UNROLL = 8
ADAM_LR, ADAM_B1, ADAM_B2, ADAM_EPS, ADAM_WD, ADAM_STEP = 0.001, 0.9, 0.999, 1e-08, 0.01, 10

VMEM_LIMIT = 56 * 1024 * 1024
ACC_BYTES = 6 * 1024 * 1024
LANES = 128
BF16_ROWS = 16
GRAD_DTYPE = BF16
FF_SHARD = D_FF // N_DEV
FF_SHARD_P = 352
IN_SHARD = IN_W // N_DEV
IN_SHARD_P = 160

WEIGHTS = ['ffn1_norm', 'ffn1_w_gate', 'ffn1_w_up', 'ffn1_w_down', 'mix_norm', 'w_in', 'mla_q_norm', 'mla_w_uq',
           'mla_kv_norm', 'mla_w_ukv', 'mla_qk_norm_q', 'mla_qk_norm_k', 'ssm_a_re', 'ssm_a_im', 'ssm_log_dt',
           'ssm_b_re', 'ssm_b_im', 'ssm_c_re', 'ssm_c_im', 'ssm_d', 'ssm_w_glu', 'ssm_b_glu', 'out_norm_mla',
           'out_norm_ssm', 'w_o', 'xattn_norm', 'mem_norm', 'xattn_w_q', 'xattn_w_kv', 'xattn_q_norm',
           'xattn_k_norm', 'xattn_w_o', 'ffn2_norm', 'ffn2_w_gate', 'ffn2_w_up', 'ffn2_w_down']
SHARD_AXIS = {'ffn1_w_gate': 1, 'ffn1_w_up': 1, 'ffn1_w_down': 0, 'w_in': 1, 'mla_w_uq': 1, 'mla_w_ukv': 1,
              'ssm_w_glu': 0, 'w_o': 0, 'xattn_w_q': 0, 'xattn_w_kv': 0, 'xattn_w_o': 1,
              'ffn2_w_gate': 1, 'ffn2_w_up': 1, 'ffn2_w_down': 0}
SHARDED = [n for n in WEIGHTS if n in SHARD_AXIS]
SMALL = [n for n in WEIGHTS if n not in SHARD_AXIS]


def _params(sem=None):
    return pltpu.CompilerParams(dimension_semantics=sem, vmem_limit_bytes=VMEM_LIMIT)


def _tile(n, cap):
    if n <= cap:
        return n
    best = n
    for t in range(LANES, cap + 1, LANES):
        if n % t == 0:
            best = t
    return best


def _mm(a, b, *, ta=False, tb=False, out_dtype=F32, res=None, scale=1.0, name, tm_cap=512, tn_cap=1408, tk_cap=2816,
        deps=()):
    m, k = (a.shape[1], a.shape[0]) if ta else a.shape
    k2, n = (b.shape[1], b.shape[0]) if tb else b.shape
    assert k == k2, (a.shape, b.shape, ta, tb)
    if ta:
        tk_cap = min(tk_cap, 512)
        tm_cap = 1408
    tm, tn, tk = _tile(m, tm_cap), _tile(n, tn_cap), _tile(k, tk_cap)
    if tm * tn * 4 > ACC_BYTES:
        tn = _tile(n, max(LANES, ACC_BYTES // (4 * tm) // LANES * LANES))
    nk = k // tk
    dims = (((0 if ta else 1,), (1 if tb else 0,)), ((), ()))
    has_res = res is not None

    deps = [d for d in deps if d is not None]

    def body(*refs):
        a_ref, b_ref = refs[:2]
        r_ref = refs[2] if has_res else None
        o_ref, acc_ref = refs[-2:]
        kk = pl.program_id(2)

        @pl.when(kk == 0)
        def _():
            acc_ref[...] = jnp.zeros_like(acc_ref)

        acc_ref[...] += lax.dot_general(a_ref[...].astype(BF16), b_ref[...].astype(BF16), dims,
                                        preferred_element_type=F32)

        @pl.when(kk == nk - 1)
        def _():
            out = acc_ref[...]
            if scale != 1.0:
                out = out * scale
            if has_res:
                out = out + r_ref[...].astype(F32)
            o_ref[...] = out.astype(o_ref.dtype)

    a_spec = pl.BlockSpec((tk, tm), lambda i, j, kk: (kk, i)) if ta else pl.BlockSpec((tm, tk), lambda i, j, kk: (i, kk))
    b_spec = pl.BlockSpec((tn, tk), lambda i, j, kk: (j, kk)) if tb else pl.BlockSpec((tk, tn), lambda i, j, kk: (kk, j))
    o_spec = pl.BlockSpec((tm, tn), lambda i, j, kk: (i, j))
    in_specs = [a_spec, b_spec] + ([o_spec] if has_res else []) + [pl.BlockSpec(d.shape, lambda i, j, kk: (0, 0)) for d in deps]
    args = (a, b) + ((res,) if has_res else ()) + tuple(deps)
    return pl.pallas_call(
        body, name=name, grid=(m // tm, n // tn, nk), in_specs=in_specs, out_specs=o_spec,
        out_shape=jax.ShapeDtypeStruct((m, n), out_dtype), scratch_shapes=[pltpu.VMEM((tm, tn), F32)],
        compiler_params=_params(("parallel", "parallel", "arbitrary")),
    )(*args)


def _mm_grouped(a, b, *, tb=False, res=None, out_dtype=F32, name, tm=512):
    s = a.shape[0]
    g = b.shape[0]
    nb, ka = (b.shape[1], b.shape[2]) if tb else (b.shape[2], b.shape[1])
    assert a.shape[1] == g * ka
    tm = min(tm, s)
    dims = (((1,), (1 if tb else 0,)), ((), ()))
    has_res = res is not None

    def body(*refs):
        if has_res:
            a_ref, b_ref, r_ref, o_ref = refs
        else:
            a_ref, b_ref, o_ref = refs
        out = lax.dot_general(a_ref[...].astype(BF16), b_ref[...].astype(BF16), dims, preferred_element_type=F32)
        if has_res:
            out = out + r_ref[...].astype(F32)
        o_ref[...] = out.astype(o_ref.dtype)

    o_spec = pl.BlockSpec((tm, nb), lambda i, j: (i, j))
    in_specs = [pl.BlockSpec((tm, ka), lambda i, j: (i, j)), pl.BlockSpec((None,) + b.shape[1:], lambda i, j: (j, 0, 0))]
    return pl.pallas_call(
        body, name=name, grid=(s // tm, g), in_specs=in_specs + ([o_spec] if has_res else []), out_specs=o_spec,
        out_shape=jax.ShapeDtypeStruct((s, g * nb), out_dtype), compiler_params=_params(("parallel", "parallel")),
    )(a, b, *((res,) if has_res else ()))


def _mm_grouped_tn(a, b, *, ka, kb, name, tk=512):
    s = a.shape[0]
    g = a.shape[1] // ka
    assert b.shape[1] == g * kb
    tk = min(tk, s)
    nk = s // tk

    def body(a_ref, b_ref, o_ref):
        part = lax.dot_general(a_ref[...].astype(BF16), b_ref[...].astype(BF16), (((0,), (0,)), ((), ())),
                               preferred_element_type=F32)

        @pl.when(pl.program_id(1) == 0)
        def _():
            o_ref[...] = part

        @pl.when(pl.program_id(1) > 0)
        def _():
            o_ref[...] += part

    return pl.pallas_call(
        body, name=name, grid=(g, nk),
        in_specs=[pl.BlockSpec((tk, ka), lambda j, kk: (kk, j)), pl.BlockSpec((tk, kb), lambda j, kk: (kk, j))],
        out_specs=pl.BlockSpec((None, ka, kb), lambda j, kk: (j, 0, 0)),
        out_shape=jax.ShapeDtypeStruct((g, ka, kb), F32), compiler_params=_params(("parallel", "arbitrary")),
    )(a, b)


def _rowwise(fn, rows, consts, outs, accs=(), *, ts=512, name, deps=()):
    s = rows[0].shape[0]
    ts = min(ts, s)
    assert s % ts == 0
    n_rows, n_consts, n_outs = len(rows), len(consts), len(outs)
    deps = [d for d in deps if d is not None]
    consts = list(consts) + deps

    def body(*refs):
        ins = [r[...] for r in refs[:n_rows + n_consts]]
        res = fn(*ins)
        res = tuple(res) if isinstance(res, (tuple, list)) else (res,)
        out_refs = refs[n_rows + len(consts):]
        for o_ref, val in zip(out_refs[:n_outs], res[:n_outs]):
            o_ref[...] = val.astype(o_ref.dtype)
        if accs:
            first = pl.program_id(0) == 0

            @pl.when(first)
            def _():
                for a_ref, val in zip(out_refs[n_outs:], res[n_outs:]):
                    a_ref[...] = val.astype(F32)

            @pl.when(jnp.logical_not(first))
            def _():
                for a_ref, val in zip(out_refs[n_outs:], res[n_outs:]):
                    a_ref[...] += val.astype(F32)

    in_specs = [pl.BlockSpec((ts, r.shape[1]), lambda i: (i, 0)) for r in rows]
    in_specs += [pl.BlockSpec(c.shape, lambda i: (0, 0)) for c in consts]
    out_specs = [pl.BlockSpec((ts, w), lambda i: (i, 0)) for w, _ in outs]
    out_specs += [pl.BlockSpec(tuple(sh), lambda i: (0, 0)) for sh in accs]
    out_shape = [jax.ShapeDtypeStruct((s, w), dt) for w, dt in outs]
    out_shape += [jax.ShapeDtypeStruct(tuple(sh), F32) for sh in accs]
    res = pl.pallas_call(
        body, name=name, grid=(s // ts,), in_specs=in_specs, out_specs=out_specs, out_shape=out_shape,
        compiler_params=_params(("arbitrary",)),
    )(*rows, *consts)
    return res


def _rowwise_bwd(f, rows, consts, cts, *, row_grads, const_grads, adds=None, ts=512, name, deps=()):
    adds = adds or {}
    n_rows, n_consts, n_cts = len(rows), len(consts), len(cts)
    add_keys = sorted(adds)
    rg = sorted(row_grads)
    cg = sorted(const_grads)

    def fn(*args):
        r = args[:n_rows]
        c = args[n_rows:n_rows + n_consts]
        ct = args[n_rows + n_consts:n_rows + n_consts + n_cts]
        extra = args[n_rows + n_consts + n_cts:]
        outs, vjp = jax.vjp(f, *r, *c)
        outs = tuple(outs) if isinstance(outs, (tuple, list)) else (outs,)
        cot = tuple(g.astype(o.dtype) for g, o in zip(ct, outs))
        grads = vjp(cot if len(cot) > 1 else cot[0])
        res = []
        for i in rg:
            g = grads[i].astype(F32)
            if i in adds:
                g = g + extra[add_keys.index(i)].astype(F32)
            res.append(g)
        for i in cg:
            res.append(grads[n_rows + i])
        return tuple(res)

    rows_all = list(rows) + list(cts) + [adds[i] for i in add_keys]
    def fn2(*args):
        nr = len(rows_all)
        rr, cc = args[:nr], args[nr:]
        return fn(*rr[:n_rows], *cc, *rr[n_rows:])

    outs = [(rows[i].shape[1], row_grads[i]) for i in rg]
    accs = [consts[i].shape for i in cg]
    return _rowwise(fn2, rows_all, list(consts), outs, accs, ts=ts, name=name, deps=deps)


def _rms(x, g):
    xf = x.astype(F32)
    return xf * lax.rsqrt(jnp.mean(xf * xf, axis=-1, keepdims=True) + EPS) * g.astype(F32)


def _sigmoid(x):
    return 1.0 / (1.0 + jnp.exp(-x))


def _f_norm(x, g):
    return _rms(x, g).astype(BF16)


def _f_swiglu(gate, up):
    gate, up = gate.astype(F32), up.astype(F32)
    return (gate * _sigmoid(gate) * up).astype(BF16)


def _f_prep1(proj, gq, gkv):
    return _rms(proj[:, :Q_RANK], gq).astype(BF16), _rms(proj[:, Q_RANK:Q_RANK + KV_RANK], gkv).astype(BF16)


def _f_kr(proj):
    return (proj[:, Q_RANK + KV_RANK + SSM_W:],)


def _f_prep2(qall, kv, kr2, cos, sin, gq, gk):
    kr, krs = kr2[:, :LANES].astype(F32), kr2[:, LANES:].astype(F32)
    k_rot = kr * gk[1:2] * cos + krs * gk[2:3] * sin
    k_ss = jnp.sum(kr * kr, axis=-1, keepdims=True)
    q_scale = QK ** -0.5 / LN2
    qs, ks, vs = [], [], []
    for h in range(H):
        qn = qall[:, h * LANES:(h + 1) * LANES].astype(F32)
        qr = qall[:, (H + h) * LANES:(H + h + 1) * LANES].astype(F32)
        qrs = qall[:, (2 * H + h) * LANES:(2 * H + h + 1) * LANES].astype(F32)
        rstd = lax.rsqrt((jnp.sum(qn * qn, axis=-1, keepdims=True) + jnp.sum(qr * qr, axis=-1, keepdims=True)) / QK + EPS)
        rstd = rstd * q_scale
        qs += [qn * gq[0:1] * rstd, (qr * gq[1:2] * cos + qrs * gq[2:3] * sin) * rstd]
        kn = kv[:, 2 * h * LANES:(2 * h + 1) * LANES].astype(F32)
        rstd_k = lax.rsqrt((jnp.sum(kn * kn, axis=-1, keepdims=True) + k_ss) / QK + EPS)
        ks += [kn * gk[0:1] * rstd_k, k_rot * rstd_k]
        vs.append(kv[:, (2 * h + 1) * LANES:(2 * h + 2) * LANES])
    return (jnp.concatenate(qs, axis=-1).astype(BF16), jnp.concatenate(ks, axis=-1).astype(BF16),
            jnp.concatenate(vs, axis=-1).astype(BF16))


def _gelu(x):
    return 0.5 * x * (1.0 + jnp.tanh(math.sqrt(2.0 / math.pi) * (x + 0.044715 * (x * x * x))))


def _f_s5_gelu(yc, u, d):
    return _gelu(yc.astype(F32) + d * u.astype(F32))


def _f_outnorm(o_mla, g, z, b_glu, g_om, g_os):
    y_ssm = g * _sigmoid(z + b_glu)
    return jnp.concatenate([_rms(o_mla, g_om), _rms(y_ssm, g_os)], axis=-1).astype(BF16)


def _f_memk(kvm, gk):
    ks = [_rms(kvm[:, h * XH:(h + 1) * XH], gk) for h in range(H)]
    return jnp.concatenate(ks, axis=-1).astype(BF16), kvm[:, H * XH:].astype(BF16)


def _f_disc(lr, li, log_dt, br, bi):
    dt = jnp.exp(log_dt)
    decay = jnp.exp(lr * dt)
    ar = decay * jnp.cos(li * dt)
    ai = decay * jnp.sin(li * dt)
    den = lr * lr + li * li
    nr = ar - 1.0
    coef_r = (nr * lr + ai * li) / den
    coef_i = (ai * lr - nr * li) / den
    return ar, ai, coef_r * br - coef_i * bi, coef_r * bi + coef_i * br


def _causal_mask(i, j, tq, tk):
    qpos = i * tq + lax.broadcasted_iota(jnp.int32, (tq, tk), 0)
    kpos = j * tk + lax.broadcasted_iota(jnp.int32, (tq, tk), 1)
    return qpos >= kpos


def _attn_fwd(q, k, v, *, t=512):
    s = q.shape[0]
    t = min(t, s)
    nb = s // t

    def body(q_ref, k_ref, v_ref, o_ref, lse_ref, m_sc, l_sc, acc_sc):
        i, j = pl.program_id(1), pl.program_id(2)

        @pl.when(j == 0)
        def _():
            m_sc[...] = jnp.full_like(m_sc, -jnp.inf)
            l_sc[...] = jnp.zeros_like(l_sc)
            acc_sc[...] = jnp.zeros_like(acc_sc)

        def block(diagonal):
            sc = lax.dot_general(q_ref[...], k_ref[...], (((1,), (1,)), ((), ())), preferred_element_type=F32)
            if diagonal:
                sc = jnp.where(_causal_mask(i, j, t, t), sc, -jnp.inf)
            m_old = m_sc[...]
            m_new = jnp.maximum(m_old, jnp.max(sc, axis=-1, keepdims=True))
            p = jnp.exp2(sc - m_new)
            alpha = jnp.exp2(m_old - m_new)
            l_sc[...] = alpha * l_sc[...] + jnp.sum(p, axis=-1, keepdims=True)
            acc_sc[...] = alpha * acc_sc[...] + jnp.dot(p.astype(BF16), v_ref[...], preferred_element_type=F32)
            m_sc[...] = m_new

        pl.when(j < i)(lambda: block(False))

        @pl.when(j == i)
        def _():
            block(True)
            o_ref[...] = acc_sc[...] / l_sc[...]
            lse_ref[...] = jnp.broadcast_to(m_sc[...] + jnp.log2(l_sc[...]), lse_ref.shape)

    kv_map = lambda h, i, j: (jnp.minimum(j, i), h)
    return pl.pallas_call(
        body, name="mla_attn_fwd", grid=(H, nb, nb),
        in_specs=[pl.BlockSpec((t, HQ), lambda h, i, j: (i, h)), pl.BlockSpec((t, HQ), kv_map),
                  pl.BlockSpec((t, VD), kv_map)],
        out_specs=[pl.BlockSpec((t, VD), lambda h, i, j: (i, h)), pl.BlockSpec((t, LANES), lambda h, i, j: (i, h))],
        out_shape=[jax.ShapeDtypeStruct((s, H * VD), F32), jax.ShapeDtypeStruct((s, H * LANES), F32)],
        scratch_shapes=[pltpu.VMEM((t, 1), F32), pltpu.VMEM((t, 1), F32), pltpu.VMEM((t, VD), F32)],
        compiler_params=_params(("parallel", "parallel", "arbitrary")),
    )(q, k, v)


def _attn_probs(q_ref, k_ref, v_ref, do_ref, lse_ref, dl_ref, i, j, t, diagonal):
    sc = lax.dot_general(q_ref[...], k_ref[...], (((1,), (1,)), ((), ())), preferred_element_type=F32)
    p = jnp.exp2(sc - jnp.tile(lse_ref[...], (1, t // LANES)))
    if diagonal:
        p = jnp.where(_causal_mask(i, j, t, t), p, 0.0)
    dp = lax.dot_general(do_ref[...], v_ref[...], (((1,), (1,)), ((), ())), preferred_element_type=F32)
    ds = p * (dp - jnp.tile(dl_ref[...], (1, t // LANES)))
    return p, ds


def _attn_bwd(q, k, v, do, lse, delta, *, t=512):
    s = q.shape[0]
    t = min(t, s)
    nb = s // t

    def body(q_ref, k_ref, v_ref, do_ref, lse_ref, dl_ref, dq_ref, dk_ref, dv_ref, dk_sc, dv_sc):
        j, i = pl.program_id(1), pl.program_id(2)

        @pl.when(jnp.logical_and(i == 0, j == 0))
        def _():
            dq_ref[...] = jnp.zeros_like(dq_ref)

        @pl.when(i == 0)
        def _():
            dk_sc[...] = jnp.zeros_like(dk_sc)
            dv_sc[...] = jnp.zeros_like(dv_sc)

        def block(diagonal):
            p, ds = _attn_probs(q_ref, k_ref, v_ref, do_ref, lse_ref, dl_ref, i, j, t, diagonal)
            dsb = ds.astype(BF16)
            dv_sc[...] += lax.dot_general(p.astype(BF16), do_ref[...], (((0,), (0,)), ((), ())), preferred_element_type=F32)
            dk_sc[...] += lax.dot_general(dsb, q_ref[...], (((0,), (0,)), ((), ())), preferred_element_type=F32)
            rows = pl.ds(pl.multiple_of(i * t, t), t)
            dq_ref[rows, :] += jnp.dot(dsb, k_ref[...], preferred_element_type=F32)

        pl.when(i > j)(lambda: block(False))
        pl.when(i == j)(lambda: block(True))

        @pl.when(i == nb - 1)
        def _():
            dk_ref[...] = dk_sc[...] * LN2
            dv_ref[...] = dv_sc[...]

        @pl.when(jnp.logical_and(i == nb - 1, j == nb - 1))
        def _():
            dq_ref[...] = dq_ref[...] * LN2

    q_map = lambda h, j, i: (jnp.maximum(i, j), h)
    kv_map = lambda h, j, i: (j, h)
    dq, dk, dv = pl.pallas_call(
        body, name="mla_attn_bwd", grid=(H, nb, nb),
        in_specs=[pl.BlockSpec((t, HQ), q_map), pl.BlockSpec((t, HQ), kv_map), pl.BlockSpec((t, VD), kv_map),
                  pl.BlockSpec((t, VD), q_map), pl.BlockSpec((t, LANES), q_map), pl.BlockSpec((t, LANES), q_map)],
        out_specs=[pl.BlockSpec((s, HQ), lambda h, j, i: (0, h)), pl.BlockSpec((t, HQ), kv_map), pl.BlockSpec((t, VD), kv_map)],
        out_shape=[jax.ShapeDtypeStruct((s, H * HQ), F32), jax.ShapeDtypeStruct((s, H * HQ), F32),
                   jax.ShapeDtypeStruct((s, H * VD), F32)],
        scratch_shapes=[pltpu.VMEM((t, HQ), F32), pltpu.VMEM((t, VD), F32)],
        compiler_params=_params(("parallel", "arbitrary", "arbitrary")),
    )(q, k, v, do, lse, delta)
    return dq, dk, dv


def _f_delta(do, o):
    prod = do.astype(F32) * o.astype(F32)
    parts = [jnp.broadcast_to(jnp.sum(prod[:, h * VD:(h + 1) * VD], axis=-1, keepdims=True), (do.shape[0], LANES))
             for h in range(H)]
    return jnp.concatenate(parts, axis=-1), do.astype(BF16)


def _xattn_head(qh, kh, gq):
    qn = _rms(qh, gq) * (XH ** -0.5)
    sc = lax.dot_general(qn.astype(BF16), kh, (((1,), (1,)), ((), ())), preferred_element_type=F32)
    sc = sc - jnp.max(sc, axis=-1, keepdims=True)
    e = jnp.exp(sc)
    return qn, e / jnp.sum(e, axis=-1, keepdims=True)


def _xattn_fwd(q, kn, v, gq, *, ts=512):
    def fn(qb, knb, vb, g):
        outs = []
        for h in range(H):
            sl = slice(h * XH, (h + 1) * XH)
            _, p = _xattn_head(qb[:, sl], knb[:, sl], g)
            outs.append(jnp.dot(p.astype(BF16), vb[:, sl], preferred_element_type=F32))
        return (jnp.concatenate(outs, axis=-1),)

    return _rowwise(fn, [q], [kn, v, gq], [(H * XH, BF16)], ts=ts, name="xattn_fwd")[0]


def _xattn_bwd(q, kn, v, gq, do, *, ts=512):
    def fn(qb, dob, knb, vb, g):
        dqs, dks, dvs = [], [], []
        dg = jnp.zeros((1, XH), F32)
        for h in range(H):
            sl = slice(h * XH, (h + 1) * XH)
            qh, kh, vh, doh = qb[:, sl], knb[:, sl], vb[:, sl], dob[:, sl].astype(BF16)
            qn, p = _xattn_head(qh, kh, g)
            dp = lax.dot_general(doh, vh, (((1,), (1,)), ((), ())), preferred_element_type=F32)
            dvs.append(lax.dot_general(p.astype(BF16), doh, (((0,), (0,)), ((), ())), preferred_element_type=F32))
            ds = (p * (dp - jnp.sum(dp * p, axis=-1, keepdims=True))).astype(BF16)
            dqn = jnp.dot(ds, kh, preferred_element_type=F32)
            dks.append(lax.dot_general(ds, qn.astype(BF16), (((0,), (0,)), ((), ())), preferred_element_type=F32))
            _, vjp_n = jax.vjp(lambda a, b: _rms(a, b) * (XH ** -0.5), qh, g)
            dqh, dgh = vjp_n(dqn)
            dqs.append(dqh)
            dg = dg + dgh
        return (jnp.concatenate(dqs, axis=-1), jnp.concatenate(dks, axis=-1), jnp.concatenate(dvs, axis=-1), dg)

    return _rowwise(fn, [q, do], [kn, v, gq], [(H * XH, BF16)], [kn.shape, v.shape, gq.shape], ts=ts, name="xattn_bwd")


def _cmul(ar, ai, xr, xi):
    return ar * xr - ai * xi, ar * xi + ai * xr


def _scan_in_place(xr_ref, xi_ref, ar, ai, *, reverse):
    s, cw = xr_ref.shape
    c = SCAN_CHUNKS
    tt = s // c
    a_r = jnp.broadcast_to(ar, (c, cw))
    a_i = jnp.broadcast_to(ai, (c, cw))
    zero = jnp.zeros((c, cw), F32)

    def row(step):
        t = (tt - 1 - step) if reverse else step
        return pl.ds(pl.multiple_of(t * c, c), c)

    def local(step, carry):
        sr, si, qr, qi = carry
        r = row(step)
        nr, ni = _cmul(a_r, a_i, sr, si)
        nr, ni = nr + xr_ref[r, :], ni + xi_ref[r, :]
        xr_ref[r, :] = nr
        xi_ref[r, :] = ni
        return (nr, ni) + _cmul(a_r, a_i, qr, qi)

    end_r, end_i, pr, pi = lax.fori_loop(0, tt, local, (zero, zero, jnp.ones((c, cw), F32), zero), unroll=SCAN_UNROLL)

    rows_id = lax.broadcasted_iota(jnp.int32, (c, cw), 0)
    car_r, car_i = zero, zero
    cur_r, cur_i = jnp.zeros((1, cw), F32), jnp.zeros((1, cw), F32)
    order = range(c - 1, -1, -1) if reverse else range(c)
    for kk in order:
        car_r = jnp.where(rows_id == kk, cur_r, car_r)
        car_i = jnp.where(rows_id == kk, cur_i, car_i)
        nr, ni = _cmul(pr[0:1], pi[0:1], cur_r, cur_i)
        cur_r = nr + end_r[kk:kk + 1]
        cur_i = ni + end_i[kk:kk + 1]

    def fix(step, carry):
        qr, qi = _cmul(a_r, a_i, *carry)
        r = row(step)
        dr, di = _cmul(qr, qi, car_r, car_i)
        xr_ref[r, :] += dr
        xi_ref[r, :] += di
        return qr, qi

    lax.fori_loop(0, tt, fix, (jnp.ones((c, cw), F32), zero), unroll=SCAN_UNROLL)


S5_ROWS = 512


def _s5_scan(v, w_r, w_i, ar, ai, *, reverse, tb, readout=None, name):
    s = v.shape[0]
    g = w_r.shape[0]
    nv, ns = SSM_PACK * SSM_GRP, SSM_PACK * SSM_P
    rows = min(S5_ROWS, s)
    dims = (((1,), (1 if tb else 0,)), ((), ()))
    n_w = 2 if readout is None else 4

    def body(v_ref, ar_ref, ai_ref, *refs):
        w = [r[...] for r in refs[:n_w]]
        xr_ref, xi_ref = refs[n_w:n_w + 2]
        for r0 in range(0, s, rows):
            vb = v_ref[r0:r0 + rows, :].astype(BF16)
            xr_ref[r0:r0 + rows, :] = lax.dot_general(vb, w[0], dims, preferred_element_type=F32)
            xi_ref[r0:r0 + rows, :] = lax.dot_general(vb, w[1], dims, preferred_element_type=F32)
        _scan_in_place(xr_ref, xi_ref, ar_ref[...], ai_ref[...], reverse=reverse)
        if readout is not None:
            y_ref = refs[n_w + 2]
            for r0 in range(0, s, rows):
                y_ref[r0:r0 + rows, :] = (
                    jnp.dot(xr_ref[r0:r0 + rows, :].astype(BF16), w[2], preferred_element_type=F32)
                    + jnp.dot(xi_ref[r0:r0 + rows, :].astype(BF16), w[3], preferred_element_type=F32))

    col = lambda j: (0, j)
    w_spec = lambda a: pl.BlockSpec((None,) + a.shape[1:], lambda j: (j, 0, 0))
    weights = [w_r, w_i] + (list(readout) if readout is not None else [])
    out_specs = [pl.BlockSpec((s, ns), col)] * 2 + ([pl.BlockSpec((s, nv), col)] if readout is not None else [])
    out_shape = [jax.ShapeDtypeStruct((s, g * ns), F32)] * 2 + (
        [jax.ShapeDtypeStruct((s, g * nv), F32)] if readout is not None else [])
    return pl.pallas_call(
        body, name=name, grid=(g,),
        in_specs=[pl.BlockSpec((s, nv), col), pl.BlockSpec((1, ns), col), pl.BlockSpec((1, ns), col)] + [w_spec(a) for a in weights],
        out_specs=out_specs, out_shape=out_shape, compiler_params=_params(("parallel",)),
    )(v, ar, ai, *weights)


def _s5_grads(lam_r, lam_i, xr, xi, u, dyc, du_d, b_r, b_i):
    s = u.shape[0]
    g = b_r.shape[0]
    nv, ns, c = SSM_PACK * SSM_GRP, SSM_PACK * SSM_P, SCAN_CHUNKS
    rows = min(S5_ROWS, s)
    slabs = rows // c
    last_slab = s // c - 1
    nt = (((1,), (1,)), ((), ()))
    tn = (((0,), (0,)), ((), ()))

    def body(lr_ref, li_ref, xr_ref, xi_ref, pr_ref, pi_ref, u_ref, dy_ref, dud_ref, br_ref, bi_ref,
             du_ref, dbr_ref, dbi_ref, dcr_ref, dci_ref, dar_ref, dai_ref):
        first = pl.program_id(1) == 0
        l_r, l_i, x_r, x_i = lr_ref[...], li_ref[...], xr_ref[...], xi_ref[...]
        lrb, lib = l_r.astype(BF16), l_i.astype(BF16)
        du_ref[...] = (dud_ref[...] + lax.dot_general(lrb, br_ref[...], nt, preferred_element_type=F32)
                       + lax.dot_general(lib, bi_ref[...], nt, preferred_element_type=F32))
        ub, dyb = u_ref[...].astype(BF16), dy_ref[...].astype(BF16)
        rows_id = lax.broadcasted_iota(jnp.int32, (c, ns), 0)

        def before(p_ref, x):
            p = p_ref[...]
            p = jnp.where(first, jnp.where(rows_id == 0, 0.0, pltpu.roll(p, 1, 0)), p)
            return jnp.concatenate([p, x[:rows - c]], axis=0)

        xp_r, xp_i = before(pr_ref, x_r), before(pi_ref, x_i)
        parts = (lax.dot_general(ub, lrb, tn, preferred_element_type=F32),
                 lax.dot_general(ub, lib, tn, preferred_element_type=F32),
                 lax.dot_general(x_r.astype(BF16), dyb, tn, preferred_element_type=F32),
                 lax.dot_general(x_i.astype(BF16), dyb, tn, preferred_element_type=F32),
                 jnp.sum(l_r * xp_r + l_i * xp_i, axis=0, keepdims=True),
                 jnp.sum(l_i * xp_r - l_r * xp_i, axis=0, keepdims=True))
        accs = (dbr_ref, dbi_ref, dcr_ref, dci_ref, dar_ref, dai_ref)

        @pl.when(first)
        def _():
            for a_ref, val in zip(accs, parts):
                a_ref[...] = val

        @pl.when(jnp.logical_not(first))
        def _():
            for a_ref, val in zip(accs, parts):
                a_ref[...] += val

    state = pl.BlockSpec((rows, ns), lambda j, k: (k, j))
    chan = pl.BlockSpec((rows, nv), lambda j, k: (k, j))
    slab = pl.BlockSpec((c, ns), lambda j, k: (jnp.where(k == 0, last_slab, k * slabs - 1), j))
    per_b = pl.BlockSpec((None, nv, ns), lambda j, k: (j, 0, 0))
    per_c = pl.BlockSpec((None, ns, nv), lambda j, k: (j, 0, 0))
    per_a = pl.BlockSpec((1, ns), lambda j, k: (0, j))
    return pl.pallas_call(
        body, name="s5_grads", grid=(g, s // rows),
        in_specs=[state, state, state, state, slab, slab, chan, chan, chan, per_b, per_b],
        out_specs=[chan, per_b, per_b, per_c, per_c, per_a, per_a],
        out_shape=[jax.ShapeDtypeStruct((s, g * nv), F32), jax.ShapeDtypeStruct((g, nv, ns), F32),
                   jax.ShapeDtypeStruct((g, nv, ns), F32), jax.ShapeDtypeStruct((g, ns, nv), F32),
                   jax.ShapeDtypeStruct((g, ns, nv), F32), jax.ShapeDtypeStruct((1, g * ns), F32),
                   jax.ShapeDtypeStruct((1, g * ns), F32)],
        compiler_params=_params(("parallel", "arbitrary")),
    )(lam_r, lam_i, xr, xi, xr, xi, u, dyc, du_d, b_r, b_i)


def _mesh_place():
    x, y, c = lax.axis_index("x"), lax.axis_index("y"), lax.axis_index("c")
    peers = []
    for k in range(1, N_DEV):
        px, py, pc = x ^ ((k >> 2) & 1), y ^ ((k >> 1) & 1), c ^ (k & 1)
        peers.append(((px, py, pc), 4 * px + 2 * py + pc))
    return 4 * x + 2 * y + c, peers


class _Exchange:
    def __init__(self, arrays, rows, *, gather, name, after=None):
        self.n_arr, self.rows, self.gather, self.name = len(arrays), rows, gather, name
        n_arr = self.n_arr
        if gather:
            assert all(r % BF16_ROWS == 0 for r in rows)
            lands = [lax.empty((N_DEV * r, a.shape[1]), a.dtype) for a, r in zip(arrays, rows)]
        else:
            lands = [lax.empty((N_DEV - 1,) + (tuple(a.shape) if st is None else (n, a.shape[1])), a.dtype)
                     for a, (st, n) in zip(arrays, rows)]
        has_after = after is not None

        def body(*refs):
            ins, zones = refs[:n_arr], refs[n_arr:2 * n_arr]
            sems = refs[2 * n_arr + has_after:4 * n_arr + has_after]
            token = refs[-1]
            me, peers = _mesh_place()
            for i in range(n_arr):
                for k, (pxyz, pid) in enumerate(peers):
                    if gather:
                        src = ins[i]
                        dst = zones[i].at[pl.ds(pl.multiple_of(me * rows[i], BF16_ROWS), rows[i])]
                    else:
                        stride, n = rows[i]
                        src = ins[i] if stride is None else ins[i].at[pl.ds(pl.multiple_of(pid * stride, BF16_ROWS), n)]
                        dst = zones[i].at[k]
                    pltpu.make_async_remote_copy(
                        src_ref=src, dst_ref=dst, send_sem=sems[2 * i], recv_sem=sems[2 * i + 1],
                        device_id=pxyz, device_id_type=pl.DeviceIdType.MESH).start()
            token[...] = jnp.zeros_like(token)

        hbm = pl.BlockSpec(memory_space=pltpu.HBM)
        sem = pl.BlockSpec(memory_space=pltpu.SEMAPHORE)
        args = [pltpu.with_memory_space_constraint(a, pltpu.HBM) for a in list(arrays) + lands]
        res = pl.pallas_call(
            body, name=name + "_start",
            in_specs=[hbm] * (2 * n_arr) + ([pl.BlockSpec(memory_space=pl.ANY)] if has_after else []),
            out_specs=[sem] * (2 * n_arr) + [hbm] * (2 * n_arr) + [pl.BlockSpec(memory_space=pltpu.VMEM)],
            out_shape=[pltpu.SemaphoreType.DMA(())] * (2 * n_arr) + [pltpu.HBM(a.shape, a.dtype) for a in args]
            + [jax.ShapeDtypeStruct((8, LANES), F32)],
            input_output_aliases={i: 2 * n_arr + i for i in range(2 * n_arr)},
            compiler_params=pltpu.CompilerParams(has_side_effects=pltpu.SideEffectType.DATAFLOW_SIDE_EFFECTING),
        )(*args, *([after] if has_after else []))
        self.sems, self.thru, self.token = res[:2 * n_arr], res[2 * n_arr:4 * n_arr], res[-1]

    def wait(self, after):
        n_arr = self.n_arr

        def body(*refs):
            zones, sems = refs[n_arr:2 * n_arr], refs[2 * n_arr:4 * n_arr]
            myself = (lax.axis_index("x"), lax.axis_index("y"), lax.axis_index("c"))
            for i in range(n_arr):
                seven = zones[i].at[pl.ds(0, (N_DEV - 1) * self.rows[i])] if self.gather else zones[i]
                all_seven = pltpu.make_async_remote_copy(
                    src_ref=seven, dst_ref=seven, send_sem=sems[2 * i], recv_sem=sems[2 * i + 1],
                    device_id=myself, device_id_type=pl.DeviceIdType.MESH)
                all_seven.wait_recv()
                all_seven.wait_send()

        hbm = pl.BlockSpec(memory_space=pltpu.HBM)
        sem = pl.BlockSpec(memory_space=pltpu.SEMAPHORE)
        res = pl.pallas_call(
            body, name=self.name + "_wait",
            in_specs=[hbm] * (2 * n_arr) + [sem] * (2 * n_arr) + [pl.BlockSpec(memory_space=pl.ANY)],
            out_specs=[hbm] * (2 * n_arr), out_shape=[pltpu.HBM(a.shape, a.dtype) for a in self.thru],
            input_output_aliases={i: i for i in range(2 * n_arr)},
            compiler_params=pltpu.CompilerParams(has_side_effects=pltpu.SideEffectType.DATAFLOW_SIDE_EFFECTING),
        )(*self.thru, *self.sems, after)
        return res[:n_arr], res[n_arr:]


def _my_slot():
    me = 4 * lax.axis_index("x") + 2 * lax.axis_index("y") + lax.axis_index("c")
    return me.astype(jnp.int32).reshape(1)


def _place_own(gathered, block, me, *, name):
    r, c = block.shape

    def body(me_ref, b_ref, g_ref, o_ref):
        o_ref[...] = b_ref[...]

    return pl.pallas_call(
        body, name=name, out_shape=jax.ShapeDtypeStruct(gathered.shape, gathered.dtype),
        grid_spec=pltpu.PrefetchScalarGridSpec(
            num_scalar_prefetch=1, grid=(1,),
            in_specs=[pl.BlockSpec((r, c), lambda i, me_ref: (0, 0)), pl.BlockSpec(memory_space=pl.ANY)],
            out_specs=pl.BlockSpec((r, c), lambda i, me_ref: (me_ref[0], 0))),
        input_output_aliases={2: 0}, compiler_params=_params(("arbitrary",)),
    )(me, block, gathered)


def _elementwise_tiles(r, c):
    if r % 128 == 0:
        return 128, c
    return r, (256 if c % 256 == 0 else c)


def _adamw_math(g, w, m, v):
    nm = ADAM_B1 * m + (1.0 - ADAM_B1) * g
    nv = ADAM_B2 * v + (1.0 - ADAM_B2) * (g * g)
    m_hat = nm / (1.0 - ADAM_B1 ** ADAM_STEP)
    v_hat = nv / (1.0 - ADAM_B2 ** ADAM_STEP)
    return -ADAM_LR * (m_hat / (jnp.sqrt(v_hat) + ADAM_EPS) + ADAM_WD * w), nm, nv


def _sum_parts(me_ref, own_ref, p_ref, r):
    own = own_ref[...].astype(F32)
    g = None
    for d in range(N_DEV):
        k = jnp.bitwise_xor(me_ref[0], d)
        term = jnp.where(k == 0, own, p_ref[jnp.maximum(k, 1) - 1].astype(F32))
        g = term if g is None else g + term
    return g[0:r, :]


def _sum_adamw(me, sent, stride, parts, r, w=None, m=None, v=None, *, name):
    _, own_rows, cdim = parts.shape
    assert stride is None or stride == own_rows
    tc = 256 if cdim % 256 == 0 else cdim
    update = w is not None

    def body(me_ref, own_ref, p_ref, *refs):
        g = _sum_parts(me_ref, own_ref, p_ref, r)
        if update:
            w_ref, m_ref, v_ref, g_ref, d_ref, nm_ref, nv_ref = refs
            d_ref[...], nm_ref[...], nv_ref[...] = _adamw_math(g, w_ref[...], m_ref[...], v_ref[...])
        else:
            g_ref, = refs
        g_ref[...] = g

    blk = pl.BlockSpec((r, tc), lambda j, me_ref: (0, j))
    own_spec = pl.BlockSpec((own_rows, tc), (lambda j, me_ref: (0, j)) if stride is None else (lambda j, me_ref: (me_ref[0], j)))
    n_out = 4 if update else 1
    res = pl.pallas_call(
        body, name=name, out_shape=[jax.ShapeDtypeStruct((r, cdim), F32)] * n_out,
        grid_spec=pltpu.PrefetchScalarGridSpec(
            num_scalar_prefetch=1, grid=(cdim // tc,),
            in_specs=[own_spec, pl.BlockSpec((N_DEV - 1, own_rows, tc), lambda j, me_ref: (0, 0, j))]
            + ([blk] * 3 if update else []),
            out_specs=[blk] * n_out),
        compiler_params=_params(("parallel",)),
    )(me, sent, parts, *((w, m, v) if update else ()))
    return list(res)


def _sum_adamw_small(me, sents, parts, ws, ms, vs, *, name):
    n = len(sents)

    def body(me_ref, *refs):
        ins, outs = refs[:5 * n], refs[5 * n:]
        for i in range(n):
            own_ref, p_ref, w_ref, m_ref, v_ref = ins[5 * i:5 * i + 5]
            own, g = own_ref[...], None
            for d in range(N_DEV):
                k = jnp.bitwise_xor(me_ref[0], d)
                term = jnp.where(k == 0, own, p_ref[jnp.maximum(k, 1) - 1])
                g = term if g is None else g + term
            outs[4 * i][...] = g
            outs[4 * i + 1][...], outs[4 * i + 2][...], outs[4 * i + 3][...] = _adamw_math(g, w_ref[...], m_ref[...], v_ref[...])

    vmem = pl.BlockSpec(memory_space=pltpu.VMEM)
    args = [a for group in zip(sents, parts, ws, ms, vs) for a in group]
    res = pl.pallas_call(
        body, name=name, in_specs=[pl.BlockSpec(memory_space=pltpu.SMEM)] + [vmem] * (5 * n), out_specs=[vmem] * (4 * n),
        out_shape=[jax.ShapeDtypeStruct(s.shape, F32) for s in sents for _ in range(4)],
        compiler_params=_params(),
    )(me, *args)
    return [list(res[4 * i:4 * i + 4]) for i in range(n)]


def _adamw(g, w, m, v, *, name):
    r, cdim = w.shape
    tr, tc = _elementwise_tiles(r, cdim)

    def body(g_ref, w_ref, m_ref, v_ref, d_ref, nm_ref, nv_ref):
        d_ref[...], nm_ref[...], nv_ref[...] = _adamw_math(g_ref[...], w_ref[...], m_ref[...], v_ref[...])

    blk = pl.BlockSpec((tr, tc), lambda i, j: (i, j))
    return list(pl.pallas_call(
        body, name=name, grid=(r // tr, cdim // tc), in_specs=[blk] * 4,
        out_specs=[blk] * 3, out_shape=[jax.ShapeDtypeStruct((r, cdim), F32)] * 3,
        compiler_params=_params(("parallel", "parallel")),
    )(g, w, m, v))


SHARD_ROWS_P = {n: (FF_SHARD_P if 'ffn' in n else IN_SHARD_P if n == 'w_in' else None) for n in SHARDED}


def _to_exchange_layout(name, shard):
    t = shard.T if SHARD_AXIS[name] == 1 else shard
    pad = SHARD_ROWS_P[name]
    return t if pad is None else jnp.pad(t, ((0, pad - t.shape[0]), (0, 0)))


def _expand_w_in(wt):
    wt = wt.reshape(N_DEV, IN_SHARD_P, D)[:, :IN_SHARD].reshape(IN_W, D)
    o = Q_RANK + KV_RANK
    kr1, kr2 = wt[o:o + ROPE // 2], wt[o + ROPE // 2:o + ROPE]
    z = jnp.zeros((LANES - ROPE, D), wt.dtype)
    return jnp.concatenate([wt[:o], wt[o + ROPE:], kr1, kr2, z, -kr2, kr1, z], axis=0)


def _expand_w_uq(wt):
    w = wt.reshape(H, QK, Q_RANK)
    z = jnp.zeros((H, LANES - ROPE, Q_RANK), w.dtype)
    q1, q2 = w[:, NOPE:NOPE + ROPE // 2], w[:, NOPE + ROPE // 2:]
    return jnp.concatenate([w[:, :NOPE].reshape(H * NOPE, Q_RANK),
                            jnp.concatenate([q1, q2, z], axis=1).reshape(H * LANES, Q_RANK),
                            jnp.concatenate([-q2, q1, z], axis=1).reshape(H * LANES, Q_RANK)], axis=0)


def _layout_qk_gain(g):
    g = g.reshape(QK)
    g1, g2, z = g[NOPE:NOPE + ROPE // 2], g[NOPE + ROPE // 2:], jnp.zeros((LANES - ROPE,), g.dtype)
    return jnp.stack([g[:NOPE], jnp.concatenate([g1, g2, z]), jnp.concatenate([g2, g1, z])])


def _rep16(a):
    return jnp.repeat(a, SSM_GRP, axis=0)


def _layout_ssm_in(a_re, a_im, log_dt, b_re, b_im):
    b_r = jnp.transpose(b_re, (0, 2, 1)).reshape(SSM_G * SSM_GRP, SSM_P)
    b_i = jnp.transpose(b_im, (0, 2, 1)).reshape(SSM_G * SSM_GRP, SSM_P)
    ldt = jnp.broadcast_to(log_dt.reshape(SSM_G, 1), (SSM_G, SSM_P))
    return _rep16(a_re), _rep16(a_im), _rep16(ldt), b_r, b_i


def _block_diag_b(bb):
    eye = jnp.eye(SSM_PACK, dtype=bb.dtype)
    b5 = bb.reshape(SSM_G // SSM_PACK, SSM_PACK, SSM_GRP, 1, SSM_P) * eye[None, :, None, :, None]
    return b5.reshape(SSM_G // SSM_PACK, SSM_PACK * SSM_GRP, SSM_PACK * SSM_P)


def _block_diag_c(cc):
    eye = jnp.eye(SSM_PACK, dtype=cc.dtype)
    c5 = jnp.transpose(cc, (0, 2, 1)).reshape(SSM_G // SSM_PACK, SSM_PACK, SSM_P, 1, SSM_GRP) * eye[None, :, None, :, None]
    return c5.reshape(SSM_G // SSM_PACK, SSM_PACK * SSM_P, SSM_PACK * SSM_GRP)


def _time_perm(a, inverse=False):
    s, w = a.shape
    c = SCAN_CHUNKS
    if inverse:
        return jnp.transpose(a.reshape(s // c, c, w), (1, 0, 2)).reshape(s, w)
    return jnp.transpose(a.reshape(c, s // c, w), (1, 0, 2)).reshape(s, w)


class _Weights:
    def __init__(self, groups=(), landed=None, me=None):
        self.groups, self.landed, self.me = list(groups), dict(landed or {}), me

    def get(self, name, after):
        if name not in self.landed:
            names, exchange = next(g for g in self.groups if name in g[0])
            for n, block, gathered in zip(names, *exchange.wait(after)):
                self.landed[n] = _place_own(gathered, block, self.me, name="place_" + n)
        return self.landed[name]

    def __getitem__(self, name):
        return self.landed[name]


def _ffn_gate_up(h, w_gt, w_ut, *, name, tm=512, tn=1408):
    s, k = h.shape
    n = w_gt.shape[0]
    tm, tn = min(tm, s), _tile(n, tn)
    dims = (((1,), (1,)), ((), ()))

    def body(h_ref, wg_ref, wu_ref, g_ref, u_ref, a_ref):
        hb = h_ref[...].astype(BF16)
        gate = lax.dot_general(hb, wg_ref[...], dims, preferred_element_type=F32)
        up = lax.dot_general(hb, wu_ref[...], dims, preferred_element_type=F32)
        g_ref[...] = gate.astype(BF16)
        u_ref[...] = up.astype(BF16)
        a_ref[...] = _f_swiglu(gate, up)

    w_spec = pl.BlockSpec((tn, k), lambda j, i: (j, 0))
    o_spec = pl.BlockSpec((tm, tn), lambda j, i: (i, j))
    return pl.pallas_call(
        body, name=name, grid=(n // tn, s // tm), in_specs=[pl.BlockSpec((tm, k), lambda j, i: (i, 0)), w_spec, w_spec],
        out_specs=[o_spec] * 3, out_shape=[jax.ShapeDtypeStruct((s, n), BF16)] * 3,
        compiler_params=_params(("parallel", "parallel")),
    )(h, w_gt, w_ut)


def _ffn_dgate_dup(dx_out, w_d, gate, up, *, name, tm=512, tn=1408, deps=()):
    s, k = dx_out.shape
    n = w_d.shape[0]
    tm, tn = min(tm, s), _tile(n, tn)
    deps = [d for d in deps if d is not None]

    def body(dx_ref, wd_ref, g_ref, u_ref, *refs):
        dg_ref, du_ref = refs[len(deps):]
        dact = 0.5 * lax.dot_general(dx_ref[...].astype(BF16), wd_ref[...], (((1,), (1,)), ((), ())),
                                     preferred_element_type=F32)
        _, vjp = jax.vjp(_f_swiglu, g_ref[...].astype(F32), u_ref[...].astype(F32))
        dgate, dup = vjp(dact.astype(BF16))
        dg_ref[...] = dgate.astype(BF16)
        du_ref[...] = dup.astype(BF16)

    o_spec = pl.BlockSpec((tm, tn), lambda j, i: (i, j))
    return pl.pallas_call(
        body, name=name, grid=(n // tn, s // tm),
        in_specs=[pl.BlockSpec((tm, k), lambda j, i: (i, 0)), pl.BlockSpec((tn, k), lambda j, i: (j, 0)), o_spec, o_spec]
        + [pl.BlockSpec(d.shape, lambda j, i: (0, 0)) for d in deps],
        out_specs=[o_spec] * 2, out_shape=[jax.ShapeDtypeStruct((s, n), BF16)] * 2,
        compiler_params=_params(("parallel", "parallel")),
    )(dx_out, w_d, gate, up, *deps)


def _ffn_dh(dgate, dup, w_gt, w_ut, *, name, tm=512):
    s, k = dgate.shape
    n = w_gt.shape[1]
    tm = min(tm, s)

    def body(dg_ref, du_ref, wg_ref, wu_ref, o_ref):
        o_ref[...] = (jnp.dot(dg_ref[...], wg_ref[...], preferred_element_type=F32)
                      + jnp.dot(du_ref[...], wu_ref[...], preferred_element_type=F32)).astype(o_ref.dtype)

    a_spec = pl.BlockSpec((tm, k), lambda i: (i, 0))
    w_spec = pl.BlockSpec((k, n), lambda i: (0, 0))
    return pl.pallas_call(
        body, name=name, grid=(s // tm,), in_specs=[a_spec, a_spec, w_spec, w_spec],
        out_specs=pl.BlockSpec((tm, n), lambda i: (i, 0)), out_shape=jax.ShapeDtypeStruct((s, n), BF16),
        compiler_params=_params(("parallel",)),
    )(dgate, dup, w_gt, w_ut)


def _ffn_fwd(x, g, wc, tag, deps=()):
    h = _rowwise(_f_norm, [x], [g], [(D, BF16)], name=tag + "_norm", deps=deps)[0]
    gate, up, act = _ffn_gate_up(h, wc.get(tag + '_w_gate', h), wc[tag + '_w_up'], name=tag + "_gate_up")
    x_out = _mm(act, wc.get(tag + '_w_down', act), res=x, scale=0.5, name=tag + "_down")
    return x_out, (h, gate, up, act)


def _ffn_bwd(x, g, wc, saved, dx_out, tag, send, deps=()):
    h, gate, up, act = saved
    w_gt, w_ut, w_d = (wc.get(tag + n, h) for n in ('_w_gate', '_w_up', '_w_down'))
    d_d = _mm(act, dx_out, ta=True, scale=0.5, out_dtype=GRAD_DTYPE, name=tag + "_dwdown", deps=deps)
    token = send({tag + '_w_down': d_d})
    dgate, dup = _ffn_dgate_dup(dx_out, w_d, gate, up, name=tag + "_dgate_dup", deps=[token])
    d_gt = _mm(dgate, h, ta=True, out_dtype=GRAD_DTYPE, name=tag + "_dwgate")
    token = send({tag + '_w_gate': d_gt})
    d_ut = _mm(dup, h, ta=True, out_dtype=GRAD_DTYPE, name=tag + "_dwup", deps=[token])
    token = send({tag + '_w_up': d_ut})
    dh = _ffn_dh(dgate, dup, w_gt, w_ut, name=tag + "_dh")
    dx, dg = _rowwise_bwd(_f_norm, [x], [g], [dh], row_grads={0: F32}, const_grads=[0], adds={0: dx_out},
                          name=tag + "_norm_bwd", deps=[token])
    return dx, dg


def _local_step(x, mem, cos, sin, target, wc, ws, send, deps=(), send_small=None):
    gs = {}

    x1, sv1 = _ffn_fwd(x, ws['ffn1_norm'], wc, "ffn1", deps=deps)

    h2 = _rowwise(_f_norm, [x1], [ws['mix_norm']], [(D, BF16)], name="mix_norm")[0]
    w_in_raw, w_uq_raw = wc.get('w_in', h2), wc.get('mla_w_uq', h2)
    w_in_e = _expand_w_in(w_in_raw)
    w_uq_e = _expand_w_uq(w_uq_raw)
    proj = _mm(h2, w_in_e, tb=True, name="w_in")
    c_q, c_kv = _rowwise(_f_prep1, [proj], [ws['q_norm'], ws['kv_norm']], [(Q_RANK, BF16), (KV_RANK, BF16)], name="mla_prep1")
    qall = _mm(c_q, w_uq_e, tb=True, name="w_uq")
    kv = _mm(c_kv, wc['mla_w_ukv'], tb=True, name="w_ukv")
    kr = _rowwise(_f_kr, [proj], [], [(2 * LANES, F32)], name="mla_kr")[0]
    q, k, v = _prep2_fwd(qall, kv, kr, cos, sin, ws['qk_gq'], ws['qk_gk'])
    o_mla, lse = _attn_fwd(q, k, v)

    u = proj[:, Q_RANK + KV_RANK:Q_RANK + KV_RANK + SSM_W]
    u_p = _time_perm(u)
    disc_in = [ws['ssm_lr'], ws['ssm_li'], ws['ssm_ldt'], ws['ssm_br'], ws['ssm_bi']]
    ar16, ai16, bbr, bbi = _rowwise(_f_disc, disc_in, [], [(SSM_P, F32)] * 4, name="s5_disc")
    a_r = ar16[::SSM_GRP].reshape(1, SSM_N)
    a_i = ai16[::SSM_GRP].reshape(1, SSM_N)
    bblk_r, bblk_i = _block_diag_b(bbr).astype(BF16), _block_diag_b(bbi).astype(BF16)
    cblk_r, cblk_i = _block_diag_c(ws['ssm_cr']).astype(BF16), _block_diag_c(-ws['ssm_ci']).astype(BF16)
    xr, xi, yc = _s5_scan(u_p, bblk_r, bblk_i, a_r, a_i, reverse=False, tb=False, readout=(cblk_r, cblk_i),
                          name="s5_scan_fwd")
    g_p = _rowwise(_f_s5_gelu, [yc, u_p], [ws['ssm_d']], [(SSM_W, F32)], name="s5_gelu")[0]
    z_p = _mm(g_p, wc['ssm_w_glu'], name="s5_glu")
    g_t, z_t = _time_perm(g_p, inverse=True), _time_perm(z_p, inverse=True)
    on_consts = [ws['ssm_b_glu'], ws['out_norm_mla'], ws['out_norm_ssm']]
    ycat = _rowwise(_f_outnorm, [o_mla, g_t, z_t], on_consts, [(D, BF16)], name="out_norm")[0]
    x2 = _mm(ycat, wc['w_o'], res=x1, name="w_o")

    hx = _rowwise(_f_norm, [x2], [ws['xattn_norm']], [(D, BF16)], name="xattn_norm")[0]
    xq = _mm(hx, wc['xattn_w_q'], name="xattn_q")
    mn = _rowwise(_f_norm, [mem], [ws['mem_norm']], [(D, BF16)], name="mem_norm")[0]
    kvm = _mm(mn, wc['xattn_w_kv'], name="xattn_kv")
    xkn, xv = _rowwise(_f_memk, [kvm], [ws['xattn_k_norm']], [(H * XH, BF16), (H * XH, BF16)], name="xattn_knorm")
    xo = _xattn_fwd(xq, xkn, xv, ws['xattn_q_norm'])
    x3 = _mm(xo, wc['xattn_w_o'], tb=True, res=x2, name="xattn_o")

    x4, sv2 = _ffn_fwd(x3, ws['ffn2_norm'], wc, "ffn2")

    def f_loss(yb, tb):
        err = yb - tb
        return err * (1.0 / D), jnp.broadcast_to(jnp.sum(jnp.sum(err * err, axis=1, keepdims=True), axis=0, keepdims=True) * (0.5 / D), (1, LANES))

    dx4, loss = _rowwise(f_loss, [x4, target], [], [(D, F32)], [(1, LANES)], name="loss")

    dx3, gs['ffn2_norm'] = _ffn_bwd(x3, ws['ffn2_norm'], wc, sv2, dx4, "ffn2", send)

    dxo = _mm(dx3, wc['xattn_w_o'], out_dtype=BF16, name="xattn_o_dx")
    send({'xattn_w_o': _mm(dx3, xo, ta=True, out_dtype=GRAD_DTYPE, name="xattn_o_dw")})
    dxq, dxkn, dxv, gs['xattn_q_norm'] = _xattn_bwd(xq, xkn, xv, ws['xattn_q_norm'], dxo)
    dkvm, gs['xattn_k_norm'] = _rowwise_bwd(_f_memk, [kvm], [ws['xattn_k_norm']], [dxkn, dxv], row_grads={0: BF16},
                                            const_grads=[0], name="xattn_knorm_bwd")
    send({'xattn_w_kv': _mm(mn, dkvm, ta=True, out_dtype=GRAD_DTYPE, name="xattn_kv_dw")})
    dmn = _mm(dkvm, wc['xattn_w_kv'], tb=True, out_dtype=BF16, name="xattn_kv_dx")
    gs['mem_norm'] = _rowwise_bwd(_f_norm, [mem], [ws['mem_norm']], [dmn], row_grads={}, const_grads=[0], name="mem_norm_bwd")[0]
    token = send({'xattn_w_q': _mm(hx, dxq, ta=True, out_dtype=GRAD_DTYPE, name="xattn_q_dw")})
    dhx = _mm(dxq, wc['xattn_w_q'], tb=True, out_dtype=BF16, name="xattn_q_dx")
    dx2, gs['xattn_norm'] = _rowwise_bwd(_f_norm, [x2], [ws['xattn_norm']], [dhx], row_grads={0: F32}, const_grads=[0],
                                         adds={0: dx3}, name="xattn_norm_bwd", deps=[token])

    dycat = _mm(dx2, wc['w_o'], tb=True, out_dtype=BF16, name="w_o_dx")
    send({'w_o': _mm(ycat, dx2, ta=True, out_dtype=GRAD_DTYPE, name="w_o_dw")})
    do_mla, dg_t, dz_t, gs['ssm_b_glu'], gs['out_norm_mla'], gs['out_norm_ssm'] = _rowwise_bwd(
        _f_outnorm, [o_mla, g_t, z_t], on_consts, [dycat], row_grads={0: F32, 1: F32, 2: BF16}, const_grads=[0, 1, 2],
        name="out_norm_bwd")

    dz_p, dg_p = _time_perm(dz_t), _time_perm(dg_t)
    send({'ssm_w_glu': _mm(g_p, dz_p, ta=True, out_dtype=GRAD_DTYPE, name="s5_glu_dw")})
    dg_p = _mm(dz_p, wc['ssm_w_glu'], tb=True, res=dg_p, name="s5_glu_dx")
    dyc, du_d, gs['ssm_d'] = _rowwise_bwd(_f_s5_gelu, [yc, u_p], [ws['ssm_d']], [dg_p], row_grads={0: BF16, 1: F32},
                                          const_grads=[0], name="s5_gelu_bwd")
    lam_r, lam_i = _s5_scan(dyc, cblk_r, cblk_i, a_r, -a_i, reverse=True, tb=True, name="s5_scan_bwd")
    du_p, d_bblk_r, d_bblk_i, d_cblk_r, d_cblk_i, d_ar, d_ai = _s5_grads(lam_r, lam_i, xr, xi, u_p, dyc, du_d,
                                                                        bblk_r, bblk_i)
    du = _time_perm(du_p, inverse=True)
    gs['ssm_cr'] = jax.linear_transpose(_block_diag_c, ws['ssm_cr'])(d_cblk_r)[0]
    gs['ssm_ci'] = -jax.linear_transpose(_block_diag_c, ws['ssm_ci'])(d_cblk_i)[0]
    d_bbr = jax.linear_transpose(_block_diag_b, bbr)(d_bblk_r)[0]
    d_bbi = jax.linear_transpose(_block_diag_b, bbi)(d_bblk_i)[0]
    d_ar16 = jnp.zeros((SSM_G * SSM_GRP, SSM_P), F32).at[::SSM_GRP].set(d_ar.reshape(SSM_G, SSM_P))
    d_ai16 = jnp.zeros((SSM_G * SSM_GRP, SSM_P), F32).at[::SSM_GRP].set(d_ai.reshape(SSM_G, SSM_P))
    gs['ssm_lr'], gs['ssm_li'], gs['ssm_ldt'], gs['ssm_br'], gs['ssm_bi'] = _rowwise_bwd(
        _f_disc, disc_in, [], [d_ar16, d_ai16, d_bbr, d_bbi], row_grads={i: F32 for i in range(5)}, const_grads=[],
        name="s5_disc_bwd")

    delta, do_b = _rowwise(_f_delta, [do_mla, o_mla], [], [(H * LANES, F32), (H * VD, BF16)], name="mla_delta")
    dq, dk, dv = _attn_bwd(q, k, v, do_b, lse, delta)
    dqall, dkv, dkr, gs['qk_gq'], gs['qk_gk'] = _prep2_bwd(qall, kv, kr, cos, sin, ws['qk_gq'], ws['qk_gk'], dq, dk, dv)
    d_w_uq_e = _mm(dqall, c_q, ta=True, name="w_uq_dw")
    send({'mla_w_uq': jax.linear_transpose(_expand_w_uq, jax.ShapeDtypeStruct(w_uq_raw.shape, F32))(d_w_uq_e)[0]})
    dc_q = _mm(dqall, w_uq_e, out_dtype=BF16, name="w_uq_dx")
    send({'mla_w_ukv': _mm(dkv, c_kv, ta=True, out_dtype=GRAD_DTYPE, name="w_ukv_dw")})
    dc_kv = _mm(dkv, wc['mla_w_ukv'], out_dtype=BF16, name="w_ukv_dx")

    def f_prep1_bwd(pb, dcq, dckv, dub, dkrb, gq, gkv):
        _, vjp = jax.vjp(_f_prep1, pb[:, :Q_RANK + KV_RANK], gq, gkv)
        dpa, dgq, dgkv = vjp((dcq.astype(BF16), dckv.astype(BF16)))
        return jnp.concatenate([dpa, dub, dkrb], axis=-1), dgq, dgkv

    dproj, gs['q_norm'], gs['kv_norm'] = _rowwise(
        f_prep1_bwd, [proj, dc_q, dc_kv, du, dkr], [ws['q_norm'], ws['kv_norm']], [(IN_WP, BF16)],
        [(1, Q_RANK), (1, KV_RANK)], name="mla_prep1_bwd")
    d_w_in_e = _mm(dproj, h2, ta=True, name="w_in_dw")
    token = send({'w_in': jax.linear_transpose(_expand_w_in, jax.ShapeDtypeStruct(w_in_raw.shape, F32))(d_w_in_e)[0]})
    dh2 = _mm(dproj, w_in_e, out_dtype=BF16, name="w_in_dx")
    dx1, gs['mix_norm'] = _rowwise_bwd(_f_norm, [x1], [ws['mix_norm']], [dh2], row_grads={0: F32}, const_grads=[0],
                                       adds={0: dx2}, name="mix_norm_bwd", deps=[token])

    token = send_small(gs, loss) if send_small is not None else None
    dx0, gs['ffn1_norm'] = _ffn_bwd(x, ws['ffn1_norm'], wc, sv1, dx1, "ffn1", send, deps=[token])
    return loss, dx0, gs


def _prep2_fwd(qall, kv, kr, cos, sin, gq, gk):
    return _rowwise(_f_prep2, [qall, kv, kr, cos, sin], [gq, gk], [(H * HQ, BF16), (H * HQ, BF16), (H * VD, BF16)],
                    ts=256, name="mla_prep2")


def _prep2_bwd(qall, kv, kr, cos, sin, gq, gk, dq, dk, dv):
    return _rowwise_bwd(_f_prep2, [qall, kv, kr, cos, sin], [gq, gk], [dq, dk, dv], row_grads={0: BF16, 1: BF16, 2: F32},
                        const_grads=[0, 1], ts=256, name="mla_prep2_bwd")


def _rope_tables(pos):
    half = ROPE // 2
    inv = ROPE_THETA ** (-jnp.arange(half, dtype=F32) / half)
    ang = pos.astype(F32)[:, None] * inv[None, :]
    z = jnp.zeros((pos.shape[0], LANES - ROPE), F32)
    cos, sin = jnp.cos(ang), jnp.sin(ang)
    return jnp.concatenate([cos, cos, z], axis=-1), jnp.concatenate([sin, sin, z], axis=-1)


def _small_layout(p):
    lr, li, ldt, br, bi = _layout_ssm_in(p['ssm_a_re'], p['ssm_a_im'], p['ssm_log_dt'], p['ssm_b_re'], p['ssm_b_im'])
    return {
        'ffn1_norm': p['ffn1_norm'].reshape(1, D), 'mix_norm': p['mix_norm'].reshape(1, D),
        'q_norm': p['mla_q_norm'].reshape(1, Q_RANK), 'kv_norm': p['mla_kv_norm'].reshape(1, KV_RANK),
        'qk_gq': _layout_qk_gain(p['mla_qk_norm_q']), 'qk_gk': _layout_qk_gain(p['mla_qk_norm_k']),
        'ssm_lr': lr, 'ssm_li': li, 'ssm_ldt': ldt, 'ssm_br': br, 'ssm_bi': bi,
        'ssm_cr': p['ssm_c_re'], 'ssm_ci': p['ssm_c_im'], 'ssm_d': p['ssm_d'].reshape(1, SSM_W),
        'ssm_b_glu': p['ssm_b_glu'].reshape(1, SSM_W),
        'out_norm_mla': p['out_norm_mla'].reshape(1, SSM_W), 'out_norm_ssm': p['out_norm_ssm'].reshape(1, SSM_W),
        'xattn_norm': p['xattn_norm'].reshape(1, D), 'mem_norm': p['mem_norm'].reshape(1, D),
        'xattn_q_norm': p['xattn_q_norm'].reshape(1, XH), 'xattn_k_norm': p['xattn_k_norm'].reshape(1, XH),
        'ffn2_norm': p['ffn2_norm'].reshape(1, D),
    }


def kernel(x, mem, positions, ffn1_norm, ffn1_w_gate, ffn1_w_up, ffn1_w_down, mix_norm, w_in, mla_q_norm, mla_w_uq, mla_kv_norm, mla_w_ukv, mla_qk_norm_q, mla_qk_norm_k, ssm_a_re, ssm_a_im, ssm_log_dt, ssm_b_re, ssm_b_im, ssm_c_re, ssm_c_im, ssm_d, ssm_w_glu, ssm_b_glu, out_norm_mla, out_norm_ssm, w_o, xattn_norm, mem_norm, xattn_w_q, xattn_w_kv, xattn_q_norm, xattn_k_norm, xattn_w_o, ffn2_norm, ffn2_w_gate, ffn2_w_up, ffn2_w_down, loss_target, m_ffn1_norm, m_ffn1_w_gate, m_ffn1_w_up, m_ffn1_w_down, m_mix_norm, m_w_in, m_mla_q_norm, m_mla_w_uq, m_mla_kv_norm, m_mla_w_ukv, m_mla_qk_norm_q, m_mla_qk_norm_k, m_ssm_a_re, m_ssm_a_im, m_ssm_log_dt, m_ssm_b_re, m_ssm_b_im, m_ssm_c_re, m_ssm_c_im, m_ssm_d, m_ssm_w_glu, m_ssm_b_glu, m_out_norm_mla, m_out_norm_ssm, m_w_o, m_xattn_norm, m_mem_norm, m_xattn_w_q, m_xattn_w_kv, m_xattn_q_norm, m_xattn_k_norm, m_xattn_w_o, m_ffn2_norm, m_ffn2_w_gate, m_ffn2_w_up, m_ffn2_w_down, v_ffn1_norm, v_ffn1_w_gate, v_ffn1_w_up, v_ffn1_w_down, v_mix_norm, v_w_in, v_mla_q_norm, v_mla_w_uq, v_mla_kv_norm, v_mla_w_ukv, v_mla_qk_norm_q, v_mla_qk_norm_k, v_ssm_a_re, v_ssm_a_im, v_ssm_log_dt, v_ssm_b_re, v_ssm_b_im, v_ssm_c_re, v_ssm_c_im, v_ssm_d, v_ssm_w_glu, v_ssm_b_glu, v_out_norm_mla, v_out_norm_ssm, v_w_o, v_xattn_norm, v_mem_norm, v_xattn_w_q, v_xattn_w_kv, v_xattn_q_norm, v_xattn_k_norm, v_xattn_w_o, v_ffn2_norm, v_ffn2_w_gate, v_ffn2_w_up, v_ffn2_w_down):
    args = dict(locals())
    w = {n: args[n] for n in WEIGHTS}
    mom = {n: args['m_' + n] for n in WEIGHTS}
    var = {n: args['v_' + n] for n in WEIGHTS}
    return _step(x, mem, positions, loss_target, w, mom, var)


GATHER_GROUPS = [('ffn1_gu', ['ffn1_w_gate', 'ffn1_w_up']), ('ffn1_down', ['ffn1_w_down']),
                 ('mix', ['w_in', 'mla_w_uq', 'mla_w_ukv', 'ssm_w_glu', 'w_o', 'xattn_w_q', 'xattn_w_kv', 'xattn_w_o']),
                 ('ffn2', ['ffn2_w_gate', 'ffn2_w_up', 'ffn2_w_down'])]
SCATTER_GROUPS = [('ffn2_down', ['ffn2_w_down']), ('ffn2_gate', ['ffn2_w_gate']), ('ffn2_up', ['ffn2_w_up']),
                  ('xattn', ['xattn_w_o', 'xattn_w_kv', 'xattn_w_q']),
                  ('mix', ['w_o', 'ssm_w_glu', 'mla_w_uq', 'mla_w_ukv', 'w_in']),
                  ('ffn1_down', ['ffn1_w_down']), ('ffn1_gate', ['ffn1_w_gate']), ('ffn1_up', ['ffn1_w_up'])]


def _step(x, mem, positions, loss_target, w, mom, var):
    blocks = {n: _to_exchange_layout(n, w[n][0]).astype(BF16) for n in SHARDED}
    gathers, token = [], None
    for tag, names in GATHER_GROUPS:
        ex = _Exchange([blocks[n] for n in names], [blocks[n].shape[0] for n in names], gather=True,
                       name="gather_" + tag, after=token)
        gathers.append((names, ex))
        token = ex.token
    me = _my_slot()
    wc = _Weights(gathers, me=me)

    rows = {n: (blocks[n].shape[0], blocks[n].shape[0]) for n in SHARDED}
    ready, scatters = {}, []

    def send(grads):
        ready.update({n: g.astype(GRAD_DTYPE) for n, g in grads.items()})
        for tag, names in SCATTER_GROUPS:
            if all(n in ready for n in names) and not any(t == tag for t, _, _ in scatters):
                ex = _Exchange([ready[n] for n in names], [rows[n] for n in names], gather=False, name="scatter_" + tag)
                scatters.append((tag, names, ex))
                return ex.token
        return None

    small = {n: w[n][0] for n in SMALL}
    early = [n for n in SMALL if n != 'ffn1_norm']
    small_sent = []

    def rows2d(a):
        return a.reshape(1, -1) if a.ndim == 1 else a

    def send_small(gs, loss):
        known = dict(gs, ffn1_norm=jnp.zeros((1, D), F32))
        g_small = jax.linear_transpose(_small_layout, {n: jax.ShapeDtypeStruct(small[n].shape, F32) for n in SMALL})(known)[0]
        arrays = [rows2d(g_small[n]) for n in early] + [loss]
        small_sent.append(_Exchange(arrays, [(None, None)] * len(arrays), gather=False, name="scatter_small"))
        return small_sent[0].token

    ws = _small_layout(small)
    cos, sin = _rope_tables(positions[0])
    loss, dx, gs = _local_step(x[0], mem[0], cos, sin, loss_target[0], wc, ws, send, deps=[token], send_small=send_small)
    last_ex = _Exchange([gs['ffn1_norm']], [(None, None)], gather=False, name="scatter_last")

    out, after = {}, dx
    for _, names, ex in scatters:
        for n, sent, p in zip(names, *ex.wait(after)):
            r = w[n][0].shape[SHARD_AXIS[n]]
            if SHARD_AXIS[n] == 0:
                out[n] = _sum_adamw(me, sent, rows[n][0], p, r, w[n][0], mom[n][0], var[n][0], name="adamw_" + n)
            else:
                g = _sum_adamw(me, sent, rows[n][0], p, r, name="sum_" + n)[0].T
                out[n] = [g] + _adamw(g, w[n][0], mom[n][0], var[n][0], name="adamw_" + n)
        after = out[names[-1]][1]
    sent, p = small_sent[0].wait(after)
    state = {n: [rows2d(t[n][0]) for t in (w, mom, var)] for n in SMALL}
    state['loss'] = [loss, loss, loss]
    big = [n for n in early if small[n].ndim == 3]
    calls = [[n] for n in big] + [[n for n in early if n not in big] + ['loss']]
    where = {n: i for i, n in enumerate(early + ['loss'])}
    for names in calls:
        res = _sum_adamw_small(me, [sent[where[n]] for n in names], [p[where[n]] for n in names],
                               *[[state[n][j] for n in names] for j in range(3)], name="adamw_small_" + names[0])
        out.update(zip(names, res))
    sent, p = last_ex.wait(out['loss'][0])
    out['ffn1_norm'] = _sum_adamw_small(me, sent, p, *[[state['ffn1_norm'][j]] for j in range(3)], name="adamw_last")[0]
    loss_total = out['loss'][0][0, 0]
    outs = [out[n][i].reshape(w[n].shape) for i in range(4) for n in WEIGHTS]
    return (loss_total, dx[None], *outs)
```

```python
import math

import jax
import jax.numpy as jnp
import numpy as np
from jax import lax
from jax.experimental import pallas as pl
from jax.experimental.pallas import tpu as pltpu

F32 = jnp.float32
BF16 = jnp.bfloat16

N_DEV = 8
D = 1024
D_FF = 2752
D_FFP = 2816
MEM_LEN = 256
H = 4
Q_RANK, KV_RANK, NOPE, ROPE, VD = 384, 256, 128, 64, 128
QK = NOPE + ROPE
HQ = 2 * 128
SSM_W, SSM_G, SSM_GRP, SSM_P = 512, 32, 16, 64
SSM_N = SSM_G * SSM_P
SSM_PACK = 8
IN_W = 1216
IN_WP = 1408
XH = 128
EPS = 1e-6
LN2 = math.log(2.0)
ROPE_THETA = 10000.0
SCAN_CHUNKS = 8
SCAN_UNROLL = 8
ADAM_LR, ADAM_B1, ADAM_B2, ADAM_EPS, ADAM_WD, ADAM_STEP = 0.001, 0.9, 0.999, 1e-08, 0.01, 10

VMEM_LIMIT = 56 * 1024 * 1024
ACC_BYTES = 6 * 1024 * 1024
LANES = 128
BF16_ROWS = 16
GRAD_DTYPE = BF16
FF_SHARD = D_FF // N_DEV
FF_SHARD_P = 352
IN_SHARD = IN_W // N_DEV
IN_SHARD_P = 160

WEIGHTS = ['ffn1_norm', 'ffn1_w_gate', 'ffn1_w_up', 'ffn1_w_down', 'mix_norm', 'w_in', 'mla_q_norm', 'mla_w_uq',
           'mla_kv_norm', 'mla_w_ukv', 'mla_qk_norm_q', 'mla_qk_norm_k', 'ssm_a_re', 'ssm_a_im', 'ssm_log_dt',
           'ssm_b_re', 'ssm_b_im', 'ssm_c_re', 'ssm_c_im', 'ssm_d', 'ssm_w_glu', 'ssm_b_glu', 'out_norm_mla',
           'out_norm_ssm', 'w_o', 'xattn_norm', 'mem_norm', 'xattn_w_q', 'xattn_w_kv', 'xattn_q_norm',
           'xattn_k_norm', 'xattn_w_o', 'ffn2_norm', 'ffn2_w_gate', 'ffn2_w_up', 'ffn2_w_down']
SHARD_AXIS = {'ffn1_w_gate': 1, 'ffn1_w_up': 1, 'ffn1_w_down': 0, 'w_in': 1, 'mla_w_uq': 1, 'mla_w_ukv': 1,
              'ssm_w_glu': 0, 'w_o': 0, 'xattn_w_q': 0, 'xattn_w_kv': 0, 'xattn_w_o': 1,
              'ffn2_w_gate': 1, 'ffn2_w_up': 1, 'ffn2_w_down': 0}
SHARDED = [n for n in WEIGHTS if n in SHARD_AXIS]
SMALL = [n for n in WEIGHTS if n not in SHARD_AXIS]


def _params(sem=None):
    return pltpu.CompilerParams(dimension_semantics=sem, vmem_limit_bytes=VMEM_LIMIT)


def _tile(n, cap):
    if n <= cap:
        return n
    best = n
    for t in range(LANES, cap + 1, LANES):
        if n % t == 0:
            best = t
    return best


def _mm(a, b, *, ta=False, tb=False, out_dtype=F32, res=None, scale=1.0, name, tm_cap=512, tn_cap=1408, tk_cap=2816,
        deps=()):
    m, k = (a.shape[1], a.shape[0]) if ta else a.shape
    k2, n = (b.shape[1], b.shape[0]) if tb else b.shape
    assert k == k2, (a.shape, b.shape, ta, tb)
    if ta:
        tk_cap = min(tk_cap, 512)
        tm_cap = 1408
    tm, tn, tk = _tile(m, tm_cap), _tile(n, tn_cap), _tile(k, tk_cap)
    if tm * tn * 4 > ACC_BYTES:
        tn = _tile(n, max(LANES, ACC_BYTES // (4 * tm) // LANES * LANES))
    nk = k // tk
    dims = (((0 if ta else 1,), (1 if tb else 0,)), ((), ()))
    has_res = res is not None

    deps = [d for d in deps if d is not None]

    def body(*refs):
        a_ref, b_ref = refs[:2]
        r_ref = refs[2] if has_res else None
        o_ref, acc_ref = refs[-2:]
        kk = pl.program_id(2)

        @pl.when(kk == 0)
        def _():
            acc_ref[...] = jnp.zeros_like(acc_ref)

        acc_ref[...] += lax.dot_general(a_ref[...].astype(BF16), b_ref[...].astype(BF16), dims,
                                        preferred_element_type=F32)

        @pl.when(kk == nk - 1)
        def _():
            out = acc_ref[...]
            if scale != 1.0:
                out = out * scale
            if has_res:
                out = out + r_ref[...].astype(F32)
            o_ref[...] = out.astype(o_ref.dtype)

    a_spec = pl.BlockSpec((tk, tm), lambda i, j, kk: (kk, i)) if ta else pl.BlockSpec((tm, tk), lambda i, j, kk: (i, kk))
    b_spec = pl.BlockSpec((tn, tk), lambda i, j, kk: (j, kk)) if tb else pl.BlockSpec((tk, tn), lambda i, j, kk: (kk, j))
    o_spec = pl.BlockSpec((tm, tn), lambda i, j, kk: (i, j))
    in_specs = [a_spec, b_spec] + ([o_spec] if has_res else []) + [pl.BlockSpec(d.shape, lambda i, j, kk: (0, 0)) for d in deps]
    args = (a, b) + ((res,) if has_res else ()) + tuple(deps)
    return pl.pallas_call(
        body, name=name, grid=(m // tm, n // tn, nk), in_specs=in_specs, out_specs=o_spec,
        out_shape=jax.ShapeDtypeStruct((m, n), out_dtype), scratch_shapes=[pltpu.VMEM((tm, tn), F32)],
        compiler_params=_params(("parallel", "parallel", "arbitrary")),
    )(*args)


def _mm_grouped(a, b, *, tb=False, res=None, out_dtype=F32, name, tm=512):
    s = a.shape[0]
    g = b.shape[0]
    nb, ka = (b.shape[1], b.shape[2]) if tb else (b.shape[2], b.shape[1])
    assert a.shape[1] == g * ka
    tm = min(tm, s)
    dims = (((1,), (1 if tb else 0,)), ((), ()))
    has_res = res is not None

    def body(*refs):
        if has_res:
            a_ref, b_ref, r_ref, o_ref = refs
        else:
            a_ref, b_ref, o_ref = refs
        out = lax.dot_general(a_ref[...].astype(BF16), b_ref[...].astype(BF16), dims, preferred_element_type=F32)
        if has_res:
            out = out + r_ref[...].astype(F32)
        o_ref[...] = out.astype(o_ref.dtype)

    o_spec = pl.BlockSpec((tm, nb), lambda i, j: (i, j))
    in_specs = [pl.BlockSpec((tm, ka), lambda i, j: (i, j)), pl.BlockSpec((None,) + b.shape[1:], lambda i, j: (j, 0, 0))]
    return pl.pallas_call(
        body, name=name, grid=(s // tm, g), in_specs=in_specs + ([o_spec] if has_res else []), out_specs=o_spec,
        out_shape=jax.ShapeDtypeStruct((s, g * nb), out_dtype), compiler_params=_params(("parallel", "parallel")),
    )(a, b, *((res,) if has_res else ()))


def _mm_grouped_tn(a, b, *, ka, kb, name, tk=512):
    s = a.shape[0]
    g = a.shape[1] // ka
    assert b.shape[1] == g * kb
    tk = min(tk, s)
    nk = s // tk

    def body(a_ref, b_ref, o_ref):
        part = lax.dot_general(a_ref[...].astype(BF16), b_ref[...].astype(BF16), (((0,), (0,)), ((), ())),
                               preferred_element_type=F32)

        @pl.when(pl.program_id(1) == 0)
        def _():
            o_ref[...] = part

        @pl.when(pl.program_id(1) > 0)
        def _():
            o_ref[...] += part

    return pl.pallas_call(
        body, name=name, grid=(g, nk),
        in_specs=[pl.BlockSpec((tk, ka), lambda j, kk: (kk, j)), pl.BlockSpec((tk, kb), lambda j, kk: (kk, j))],
        out_specs=pl.BlockSpec((None, ka, kb), lambda j, kk: (j, 0, 0)),
        out_shape=jax.ShapeDtypeStruct((g, ka, kb), F32), compiler_params=_params(("parallel", "arbitrary")),
    )(a, b)


def _rowwise(fn, rows, consts, outs, accs=(), *, ts=512, name, deps=()):
    s = rows[0].shape[0]
    ts = min(ts, s)
    assert s % ts == 0
    n_rows, n_consts, n_outs = len(rows), len(consts), len(outs)
    deps = [d for d in deps if d is not None]
    consts = list(consts) + deps

    def body(*refs):
        ins = [r[...] for r in refs[:n_rows + n_consts]]
        res = fn(*ins)
        res = tuple(res) if isinstance(res, (tuple, list)) else (res,)
        out_refs = refs[n_rows + len(consts):]
        for o_ref, val in zip(out_refs[:n_outs], res[:n_outs]):
            o_ref[...] = val.astype(o_ref.dtype)
        if accs:
            first = pl.program_id(0) == 0

            @pl.when(first)
            def _():
                for a_ref, val in zip(out_refs[n_outs:], res[n_outs:]):
                    a_ref[...] = val.astype(F32)

            @pl.when(jnp.logical_not(first))
            def _():
                for a_ref, val in zip(out_refs[n_outs:], res[n_outs:]):
                    a_ref[...] += val.astype(F32)

    in_specs = [pl.BlockSpec((ts, r.shape[1]), lambda i: (i, 0)) for r in rows]
    in_specs += [pl.BlockSpec(c.shape, lambda i: (0, 0)) for c in consts]
    out_specs = [pl.BlockSpec((ts, w), lambda i: (i, 0)) for w, _ in outs]
    out_specs += [pl.BlockSpec(tuple(sh), lambda i: (0, 0)) for sh in accs]
    out_shape = [jax.ShapeDtypeStruct((s, w), dt) for w, dt in outs]
    out_shape += [jax.ShapeDtypeStruct(tuple(sh), F32) for sh in accs]
    res = pl.pallas_call(
        body, name=name, grid=(s // ts,), in_specs=in_specs, out_specs=out_specs, out_shape=out_shape,
        compiler_params=_params(("arbitrary",)),
    )(*rows, *consts)
    return res


def _rowwise_bwd(f, rows, consts, cts, *, row_grads, const_grads, adds=None, ts=512, name, deps=()):
    adds = adds or {}
    n_rows, n_consts, n_cts = len(rows), len(consts), len(cts)
    add_keys = sorted(adds)
    rg = sorted(row_grads)
    cg = sorted(const_grads)

    def fn(*args):
        r = args[:n_rows]
        c = args[n_rows:n_rows + n_consts]
        ct = args[n_rows + n_consts:n_rows + n_consts + n_cts]
        extra = args[n_rows + n_consts + n_cts:]
        outs, vjp = jax.vjp(f, *r, *c)
        outs = tuple(outs) if isinstance(outs, (tuple, list)) else (outs,)
        cot = tuple(g.astype(o.dtype) for g, o in zip(ct, outs))
        grads = vjp(cot if len(cot) > 1 else cot[0])
        res = []
        for i in rg:
            g = grads[i].astype(F32)
            if i in adds:
                g = g + extra[add_keys.index(i)].astype(F32)
            res.append(g)
        for i in cg:
            res.append(grads[n_rows + i])
        return tuple(res)

    rows_all = list(rows) + list(cts) + [adds[i] for i in add_keys]
    def fn2(*args):
        nr = len(rows_all)
        rr, cc = args[:nr], args[nr:]
        return fn(*rr[:n_rows], *cc, *rr[n_rows:])

    outs = [(rows[i].shape[1], row_grads[i]) for i in rg]
    accs = [consts[i].shape for i in cg]
    return _rowwise(fn2, rows_all, list(consts), outs, accs, ts=ts, name=name, deps=deps)


def _rms(x, g):
    xf = x.astype(F32)
    return xf * lax.rsqrt(jnp.mean(xf * xf, axis=-1, keepdims=True) + EPS) * g.astype(F32)


def _sigmoid(x):
    return 1.0 / (1.0 + jnp.exp(-x))


def _f_norm(x, g):
    return _rms(x, g).astype(BF16)


def _f_swiglu(gate, up):
    gate, up = gate.astype(F32), up.astype(F32)
    return (gate * _sigmoid(gate) * up).astype(BF16)


def _f_prep1(proj, gq, gkv):
    return _rms(proj[:, :Q_RANK], gq).astype(BF16), _rms(proj[:, Q_RANK:Q_RANK + KV_RANK], gkv).astype(BF16)


def _f_kr(proj):
    return (proj[:, Q_RANK + KV_RANK + SSM_W:],)


def _f_prep2(qall, kv, kr2, cos, sin, gq, gk):
    kr, krs = kr2[:, :LANES].astype(F32), kr2[:, LANES:].astype(F32)
    k_rot = kr * gk[1:2] * cos + krs * gk[2:3] * sin
    k_ss = jnp.sum(kr * kr, axis=-1, keepdims=True)
    q_scale = QK ** -0.5 / LN2
    qs, ks, vs = [], [], []
    for h in range(H):
        qn = qall[:, h * LANES:(h + 1) * LANES].astype(F32)
        qr = qall[:, (H + h) * LANES:(H + h + 1) * LANES].astype(F32)
        qrs = qall[:, (2 * H + h) * LANES:(2 * H + h + 1) * LANES].astype(F32)
        rstd = lax.rsqrt((jnp.sum(qn * qn, axis=-1, keepdims=True) + jnp.sum(qr * qr, axis=-1, keepdims=True)) / QK + EPS)
        rstd = rstd * q_scale
        qs += [qn * gq[0:1] * rstd, (qr * gq[1:2] * cos + qrs * gq[2:3] * sin) * rstd]
        kn = kv[:, 2 * h * LANES:(2 * h + 1) * LANES].astype(F32)
        rstd_k = lax.rsqrt((jnp.sum(kn * kn, axis=-1, keepdims=True) + k_ss) / QK + EPS)
        ks += [kn * gk[0:1] * rstd_k, k_rot * rstd_k]
        vs.append(kv[:, (2 * h + 1) * LANES:(2 * h + 2) * LANES])
    return (jnp.concatenate(qs, axis=-1).astype(BF16), jnp.concatenate(ks, axis=-1).astype(BF16),
            jnp.concatenate(vs, axis=-1).astype(BF16))


def _gelu(x):
    return 0.5 * x * (1.0 + jnp.tanh(math.sqrt(2.0 / math.pi) * (x + 0.044715 * (x * x * x))))


def _f_s5_gelu(yc, u, d):
    return _gelu(yc.astype(F32) + d * u.astype(F32))


def _f_outnorm(o_mla, g, z, b_glu, g_om, g_os):
    y_ssm = g * _sigmoid(z + b_glu)
    return jnp.concatenate([_rms(o_mla, g_om), _rms(y_ssm, g_os)], axis=-1).astype(BF16)


def _f_memk(kvm, gk):
    ks = [_rms(kvm[:, h * XH:(h + 1) * XH], gk) for h in range(H)]
    return jnp.concatenate(ks, axis=-1).astype(BF16), kvm[:, H * XH:].astype(BF16)


def _f_disc(lr, li, log_dt, br, bi):
    dt = jnp.exp(log_dt)
    decay = jnp.exp(lr * dt)
    ar = decay * jnp.cos(li * dt)
    ai = decay * jnp.sin(li * dt)
    den = lr * lr + li * li
    nr = ar - 1.0
    coef_r = (nr * lr + ai * li) / den
    coef_i = (ai * lr - nr * li) / den
    return ar, ai, coef_r * br - coef_i * bi, coef_r * bi + coef_i * br


def _causal_mask(i, j, tq, tk):
    qpos = i * tq + lax.broadcasted_iota(jnp.int32, (tq, tk), 0)
    kpos = j * tk + lax.broadcasted_iota(jnp.int32, (tq, tk), 1)
    return qpos >= kpos


def _attn_fwd(q, k, v, *, t=512):
    s = q.shape[0]
    t = min(t, s)
    nb = s // t

    def body(q_ref, k_ref, v_ref, o_ref, lse_ref, m_sc, l_sc, acc_sc):
        i, j = pl.program_id(1), pl.program_id(2)

        @pl.when(j == 0)
        def _():
            m_sc[...] = jnp.full_like(m_sc, -jnp.inf)
            l_sc[...] = jnp.zeros_like(l_sc)
            acc_sc[...] = jnp.zeros_like(acc_sc)

        def block(diagonal):
            sc = lax.dot_general(q_ref[...], k_ref[...], (((1,), (1,)), ((), ())), preferred_element_type=F32)
            if diagonal:
                sc = jnp.where(_causal_mask(i, j, t, t), sc, -jnp.inf)
            m_old = m_sc[...]
            m_new = jnp.maximum(m_old, jnp.max(sc, axis=-1, keepdims=True))
            p = jnp.exp2(sc - m_new)
            alpha = jnp.exp2(m_old - m_new)
            l_sc[...] = alpha * l_sc[...] + jnp.sum(p, axis=-1, keepdims=True)
            acc_sc[...] = alpha * acc_sc[...] + jnp.dot(p.astype(BF16), v_ref[...], preferred_element_type=F32)
            m_sc[...] = m_new

        pl.when(j < i)(lambda: block(False))

        @pl.when(j == i)
        def _():
            block(True)
            o_ref[...] = acc_sc[...] / l_sc[...]
            lse_ref[...] = jnp.broadcast_to(m_sc[...] + jnp.log2(l_sc[...]), lse_ref.shape)

    kv_map = lambda h, i, j: (jnp.minimum(j, i), h)
    return pl.pallas_call(
        body, name="mla_attn_fwd", grid=(H, nb, nb),
        in_specs=[pl.BlockSpec((t, HQ), lambda h, i, j: (i, h)), pl.BlockSpec((t, HQ), kv_map),
                  pl.BlockSpec((t, VD), kv_map)],
        out_specs=[pl.BlockSpec((t, VD), lambda h, i, j: (i, h)), pl.BlockSpec((t, LANES), lambda h, i, j: (i, h))],
        out_shape=[jax.ShapeDtypeStruct((s, H * VD), F32), jax.ShapeDtypeStruct((s, H * LANES), F32)],
        scratch_shapes=[pltpu.VMEM((t, 1), F32), pltpu.VMEM((t, 1), F32), pltpu.VMEM((t, VD), F32)],
        compiler_params=_params(("parallel", "parallel", "arbitrary")),
    )(q, k, v)


def _attn_probs(q_ref, k_ref, v_ref, do_ref, lse_ref, dl_ref, i, j, t, diagonal):
    sc = lax.dot_general(q_ref[...], k_ref[...], (((1,), (1,)), ((), ())), preferred_element_type=F32)
    p = jnp.exp2(sc - jnp.tile(lse_ref[...], (1, t // LANES)))
    if diagonal:
        p = jnp.where(_causal_mask(i, j, t, t), p, 0.0)
    dp = lax.dot_general(do_ref[...], v_ref[...], (((1,), (1,)), ((), ())), preferred_element_type=F32)
    ds = p * (dp - jnp.tile(dl_ref[...], (1, t // LANES)))
    return p, ds


def _attn_bwd(q, k, v, do, lse, delta, *, t=512):
    s = q.shape[0]
    t = min(t, s)
    nb = s // t

    def body(q_ref, k_ref, v_ref, do_ref, lse_ref, dl_ref, dq_ref, dk_ref, dv_ref, dk_sc, dv_sc):
        j, i = pl.program_id(1), pl.program_id(2)

        @pl.when(jnp.logical_and(i == 0, j == 0))
        def _():
            dq_ref[...] = jnp.zeros_like(dq_ref)

        @pl.when(i == 0)
        def _():
            dk_sc[...] = jnp.zeros_like(dk_sc)
            dv_sc[...] = jnp.zeros_like(dv_sc)

        def block(diagonal):
            p, ds = _attn_probs(q_ref, k_ref, v_ref, do_ref, lse_ref, dl_ref, i, j, t, diagonal)
            dsb = ds.astype(BF16)
            dv_sc[...] += lax.dot_general(p.astype(BF16), do_ref[...], (((0,), (0,)), ((), ())), preferred_element_type=F32)
            dk_sc[...] += lax.dot_general(dsb, q_ref[...], (((0,), (0,)), ((), ())), preferred_element_type=F32)
            rows = pl.ds(pl.multiple_of(i * t, t), t)
            dq_ref[rows, :] += jnp.dot(dsb, k_ref[...], preferred_element_type=F32)

        pl.when(i > j)(lambda: block(False))
        pl.when(i == j)(lambda: block(True))

        @pl.when(i == nb - 1)
        def _():
            dk_ref[...] = dk_sc[...] * LN2
            dv_ref[...] = dv_sc[...]

        @pl.when(jnp.logical_and(i == nb - 1, j == nb - 1))
        def _():
            dq_ref[...] = dq_ref[...] * LN2

    q_map = lambda h, j, i: (jnp.maximum(i, j), h)
    kv_map = lambda h, j, i: (j, h)
    dq, dk, dv = pl.pallas_call(
        body, name="mla_attn_bwd", grid=(H, nb, nb),
        in_specs=[pl.BlockSpec((t, HQ), q_map), pl.BlockSpec((t, HQ), kv_map), pl.BlockSpec((t, VD), kv_map),
                  pl.BlockSpec((t, VD), q_map), pl.BlockSpec((t, LANES), q_map), pl.BlockSpec((t, LANES), q_map)],
        out_specs=[pl.BlockSpec((s, HQ), lambda h, j, i: (0, h)), pl.BlockSpec((t, HQ), kv_map), pl.BlockSpec((t, VD), kv_map)],
        out_shape=[jax.ShapeDtypeStruct((s, H * HQ), F32), jax.ShapeDtypeStruct((s, H * HQ), F32),
                   jax.ShapeDtypeStruct((s, H * VD), F32)],
        scratch_shapes=[pltpu.VMEM((t, HQ), F32), pltpu.VMEM((t, VD), F32)],
        compiler_params=_params(("parallel", "arbitrary", "arbitrary")),
    )(q, k, v, do, lse, delta)
    return dq, dk, dv


def _f_delta(do, o):
    prod = do.astype(F32) * o.astype(F32)
    parts = [jnp.broadcast_to(jnp.sum(prod[:, h * VD:(h + 1) * VD], axis=-1, keepdims=True), (do.shape[0], LANES))
             for h in range(H)]
    return jnp.concatenate(parts, axis=-1), do.astype(BF16)


def _xattn_head(qh, kh, gq):
    qn = _rms(qh, gq) * (XH ** -0.5)
    sc = lax.dot_general(qn.astype(BF16), kh, (((1,), (1,)), ((), ())), preferred_element_type=F32)
    sc = sc - jnp.max(sc, axis=-1, keepdims=True)
    e = jnp.exp(sc)
    return qn, e / jnp.sum(e, axis=-1, keepdims=True)


def _xattn_fwd(q, kn, v, gq, *, ts=512):
    def fn(qb, knb, vb, g):
        outs = []
        for h in range(H):
            sl = slice(h * XH, (h + 1) * XH)
            _, p = _xattn_head(qb[:, sl], knb[:, sl], g)
            outs.append(jnp.dot(p.astype(BF16), vb[:, sl], preferred_element_type=F32))
        return (jnp.concatenate(outs, axis=-1),)

    return _rowwise(fn, [q], [kn, v, gq], [(H * XH, BF16)], ts=ts, name="xattn_fwd")[0]


def _xattn_bwd(q, kn, v, gq, do, *, ts=512):
    def fn(qb, dob, knb, vb, g):
        dqs, dks, dvs = [], [], []
        dg = jnp.zeros((1, XH), F32)
        for h in range(H):
            sl = slice(h * XH, (h + 1) * XH)
            qh, kh, vh, doh = qb[:, sl], knb[:, sl], vb[:, sl], dob[:, sl].astype(BF16)
            qn, p = _xattn_head(qh, kh, g)
            dp = lax.dot_general(doh, vh, (((1,), (1,)), ((), ())), preferred_element_type=F32)
            dvs.append(lax.dot_general(p.astype(BF16), doh, (((0,), (0,)), ((), ())), preferred_element_type=F32))
            ds = (p * (dp - jnp.sum(dp * p, axis=-1, keepdims=True))).astype(BF16)
            dqn = jnp.dot(ds, kh, preferred_element_type=F32)
            dks.append(lax.dot_general(ds, qn.astype(BF16), (((0,), (0,)), ((), ())), preferred_element_type=F32))
            _, vjp_n = jax.vjp(lambda a, b: _rms(a, b) * (XH ** -0.5), qh, g)
            dqh, dgh = vjp_n(dqn)
            dqs.append(dqh)
            dg = dg + dgh
        return (jnp.concatenate(dqs, axis=-1), jnp.concatenate(dks, axis=-1), jnp.concatenate(dvs, axis=-1), dg)

    return _rowwise(fn, [q, do], [kn, v, gq], [(H * XH, BF16)], [kn.shape, v.shape, gq.shape], ts=ts, name="xattn_bwd")


def _cmul(ar, ai, xr, xi):
    return ar * xr - ai * xi, ar * xi + ai * xr


def _scan_in_place(xr_ref, xi_ref, ar, ai, *, reverse):
    s, cw = xr_ref.shape
    c = SCAN_CHUNKS
    tt = s // c
    a_r = jnp.broadcast_to(ar, (c, cw))
    a_i = jnp.broadcast_to(ai, (c, cw))
    zero = jnp.zeros((c, cw), F32)

    def row(step):
        t = (tt - 1 - step) if reverse else step
        return pl.ds(pl.multiple_of(t * c, c), c)

    def local(step, carry):
        sr, si, qr, qi = carry
        r = row(step)
        nr, ni = _cmul(a_r, a_i, sr, si)
        nr, ni = nr + xr_ref[r, :], ni + xi_ref[r, :]
        xr_ref[r, :] = nr
        xi_ref[r, :] = ni
        return (nr, ni) + _cmul(a_r, a_i, qr, qi)

    end_r, end_i, pr, pi = lax.fori_loop(0, tt, local, (zero, zero, jnp.ones((c, cw), F32), zero), unroll=SCAN_UNROLL)

    rows_id = lax.broadcasted_iota(jnp.int32, (c, cw), 0)
    car_r, car_i = zero, zero
    cur_r, cur_i = jnp.zeros((1, cw), F32), jnp.zeros((1, cw), F32)
    order = range(c - 1, -1, -1) if reverse else range(c)
    for kk in order:
        car_r = jnp.where(rows_id == kk, cur_r, car_r)
        car_i = jnp.where(rows_id == kk, cur_i, car_i)
        nr, ni = _cmul(pr[0:1], pi[0:1], cur_r, cur_i)
        cur_r = nr + end_r[kk:kk + 1]
        cur_i = ni + end_i[kk:kk + 1]

    def fix(step, carry):
        qr, qi = _cmul(a_r, a_i, *carry)
        r = row(step)
        dr, di = _cmul(qr, qi, car_r, car_i)
        xr_ref[r, :] += dr
        xi_ref[r, :] += di
        return qr, qi

    lax.fori_loop(0, tt, fix, (jnp.ones((c, cw), F32), zero), unroll=SCAN_UNROLL)


S5_ROWS = 512


def _s5_scan(v, w_r, w_i, ar, ai, *, reverse, tb, readout=None, name):
    s = v.shape[0]
    g = w_r.shape[0]
    nv, ns = SSM_PACK * SSM_GRP, SSM_PACK * SSM_P
    rows = min(S5_ROWS, s)
    dims = (((1,), (1 if tb else 0,)), ((), ()))
    n_w = 2 if readout is None else 4

    def body(v_ref, ar_ref, ai_ref, *refs):
        w = [r[...] for r in refs[:n_w]]
        xr_ref, xi_ref = refs[n_w:n_w + 2]
        for r0 in range(0, s, rows):
            vb = v_ref[r0:r0 + rows, :].astype(BF16)
            xr_ref[r0:r0 + rows, :] = lax.dot_general(vb, w[0], dims, preferred_element_type=F32)
            xi_ref[r0:r0 + rows, :] = lax.dot_general(vb, w[1], dims, preferred_element_type=F32)
        _scan_in_place(xr_ref, xi_ref, ar_ref[...], ai_ref[...], reverse=reverse)
        if readout is not None:
            y_ref = refs[n_w + 2]
            for r0 in range(0, s, rows):
                y_ref[r0:r0 + rows, :] = (
                    jnp.dot(xr_ref[r0:r0 + rows, :].astype(BF16), w[2], preferred_element_type=F32)
                    + jnp.dot(xi_ref[r0:r0 + rows, :].astype(BF16), w[3], preferred_element_type=F32))

    col = lambda j: (0, j)
    w_spec = lambda a: pl.BlockSpec((None,) + a.shape[1:], lambda j: (j, 0, 0))
    weights = [w_r, w_i] + (list(readout) if readout is not None else [])
    out_specs = [pl.BlockSpec((s, ns), col)] * 2 + ([pl.BlockSpec((s, nv), col)] if readout is not None else [])
    out_shape = [jax.ShapeDtypeStruct((s, g * ns), F32)] * 2 + (
        [jax.ShapeDtypeStruct((s, g * nv), F32)] if readout is not None else [])
    return pl.pallas_call(
        body, name=name, grid=(g,),
        in_specs=[pl.BlockSpec((s, nv), col), pl.BlockSpec((1, ns), col), pl.BlockSpec((1, ns), col)] + [w_spec(a) for a in weights],
        out_specs=out_specs, out_shape=out_shape, compiler_params=_params(("parallel",)),
    )(v, ar, ai, *weights)


def _s5_grads(lam_r, lam_i, xr, xi, u, dyc, du_d, b_r, b_i):
    s = u.shape[0]
    g = b_r.shape[0]
    nv, ns, c = SSM_PACK * SSM_GRP, SSM_PACK * SSM_P, SCAN_CHUNKS
    rows = min(S5_ROWS, s)
    slabs = rows // c
    last_slab = s // c - 1
    nt = (((1,), (1,)), ((), ()))
    tn = (((0,), (0,)), ((), ()))

    def body(lr_ref, li_ref, xr_ref, xi_ref, pr_ref, pi_ref, u_ref, dy_ref, dud_ref, br_ref, bi_ref,
             du_ref, dbr_ref, dbi_ref, dcr_ref, dci_ref, dar_ref, dai_ref):
        first = pl.program_id(1) == 0
        l_r, l_i, x_r, x_i = lr_ref[...], li_ref[...], xr_ref[...], xi_ref[...]
        lrb, lib = l_r.astype(BF16), l_i.astype(BF16)
        du_ref[...] = (dud_ref[...] + lax.dot_general(lrb, br_ref[...], nt, preferred_element_type=F32)
                       + lax.dot_general(lib, bi_ref[...], nt, preferred_element_type=F32))
        ub, dyb = u_ref[...].astype(BF16), dy_ref[...].astype(BF16)
        rows_id = lax.broadcasted_iota(jnp.int32, (c, ns), 0)

        def before(p_ref, x):
            p = p_ref[...]
            p = jnp.where(first, jnp.where(rows_id == 0, 0.0, pltpu.roll(p, 1, 0)), p)
            return jnp.concatenate([p, x[:rows - c]], axis=0)

        xp_r, xp_i = before(pr_ref, x_r), before(pi_ref, x_i)
        parts = (lax.dot_general(ub, lrb, tn, preferred_element_type=F32),
                 lax.dot_general(ub, lib, tn, preferred_element_type=F32),
                 lax.dot_general(x_r.astype(BF16), dyb, tn, preferred_element_type=F32),
                 lax.dot_general(x_i.astype(BF16), dyb, tn, preferred_element_type=F32),
                 jnp.sum(l_r * xp_r + l_i * xp_i, axis=0, keepdims=True),
                 jnp.sum(l_i * xp_r - l_r * xp_i, axis=0, keepdims=True))
        accs = (dbr_ref, dbi_ref, dcr_ref, dci_ref, dar_ref, dai_ref)

        @pl.when(first)
        def _():
            for a_ref, val in zip(accs, parts):
                a_ref[...] = val

        @pl.when(jnp.logical_not(first))
        def _():
            for a_ref, val in zip(accs, parts):
                a_ref[...] += val

    state = pl.BlockSpec((rows, ns), lambda j, k: (k, j))
    chan = pl.BlockSpec((rows, nv), lambda j, k: (k, j))
    slab = pl.BlockSpec((c, ns), lambda j, k: (jnp.where(k == 0, last_slab, k * slabs - 1), j))
    per_b = pl.BlockSpec((None, nv, ns), lambda j, k: (j, 0, 0))
    per_c = pl.BlockSpec((None, ns, nv), lambda j, k: (j, 0, 0))
    per_a = pl.BlockSpec((1, ns), lambda j, k: (0, j))
    return pl.pallas_call(
        body, name="s5_grads", grid=(g, s // rows),
        in_specs=[state, state, state, state, slab, slab, chan, chan, chan, per_b, per_b],
        out_specs=[chan, per_b, per_b, per_c, per_c, per_a, per_a],
        out_shape=[jax.ShapeDtypeStruct((s, g * nv), F32), jax.ShapeDtypeStruct((g, nv, ns), F32),
                   jax.ShapeDtypeStruct((g, nv, ns), F32), jax.ShapeDtypeStruct((g, ns, nv), F32),
                   jax.ShapeDtypeStruct((g, ns, nv), F32), jax.ShapeDtypeStruct((1, g * ns), F32),
                   jax.ShapeDtypeStruct((1, g * ns), F32)],
        compiler_params=_params(("parallel", "arbitrary")),
    )(lam_r, lam_i, xr, xi, xr, xi, u, dyc, du_d, b_r, b_i)


def _mesh_place():
    x, y, c = lax.axis_index("x"), lax.axis_index("y"), lax.axis_index("c")
    peers = []
    for k in range(1, N_DEV):
        px, py, pc = x ^ ((k >> 2) & 1), y ^ ((k >> 1) & 1), c ^ (k & 1)
        peers.append(((px, py, pc), 4 * px + 2 * py + pc))
    return 4 * x + 2 * y + c, peers


class _Exchange:
    def __init__(self, arrays, rows, *, gather, name, after=None):
        self.n_arr, self.rows, self.gather, self.name = len(arrays), rows, gather, name
        n_arr = self.n_arr
        if gather:
            assert all(r % BF16_ROWS == 0 for r in rows)
            lands = [lax.empty((N_DEV * r, a.shape[1]), a.dtype) for a, r in zip(arrays, rows)]
        else:
            lands = [lax.empty((N_DEV - 1,) + (tuple(a.shape) if st is None else (n, a.shape[1])), a.dtype)
                     for a, (st, n) in zip(arrays, rows)]
        has_after = after is not None

        def body(*refs):
            ins, zones = refs[:n_arr], refs[n_arr:2 * n_arr]
            sems = refs[2 * n_arr + has_after:4 * n_arr + has_after]
            token = refs[-1]
            me, peers = _mesh_place()
            for i in range(n_arr):
                for k, (pxyz, pid) in enumerate(peers):
                    if gather:
                        src = ins[i]
                        dst = zones[i].at[pl.ds(pl.multiple_of(me * rows[i], BF16_ROWS), rows[i])]
                    else:
                        stride, n = rows[i]
                        src = ins[i] if stride is None else ins[i].at[pl.ds(pl.multiple_of(pid * stride, BF16_ROWS), n)]
                        dst = zones[i].at[k]
                    pltpu.make_async_remote_copy(
                        src_ref=src, dst_ref=dst, send_sem=sems[2 * i], recv_sem=sems[2 * i + 1],
                        device_id=pxyz, device_id_type=pl.DeviceIdType.MESH).start()
            token[...] = jnp.zeros_like(token)

        hbm = pl.BlockSpec(memory_space=pltpu.HBM)
        sem = pl.BlockSpec(memory_space=pltpu.SEMAPHORE)
        args = [pltpu.with_memory_space_constraint(a, pltpu.HBM) for a in list(arrays) + lands]
        res = pl.pallas_call(
            body, name=name + "_start",
            in_specs=[hbm] * (2 * n_arr) + ([pl.BlockSpec(memory_space=pl.ANY)] if has_after else []),
            out_specs=[sem] * (2 * n_arr) + [hbm] * (2 * n_arr) + [pl.BlockSpec(memory_space=pltpu.VMEM)],
            out_shape=[pltpu.SemaphoreType.DMA(())] * (2 * n_arr) + [pltpu.HBM(a.shape, a.dtype) for a in args]
            + [jax.ShapeDtypeStruct((8, LANES), F32)],
            input_output_aliases={i: 2 * n_arr + i for i in range(2 * n_arr)},
            compiler_params=pltpu.CompilerParams(has_side_effects=pltpu.SideEffectType.DATAFLOW_SIDE_EFFECTING),
        )(*args, *([after] if has_after else []))
        self.sems, self.thru, self.token = res[:2 * n_arr], res[2 * n_arr:4 * n_arr], res[-1]

    def wait(self, after):
        n_arr = self.n_arr

        def body(*refs):
            zones, sems = refs[n_arr:2 * n_arr], refs[2 * n_arr:4 * n_arr]
            myself = (lax.axis_index("x"), lax.axis_index("y"), lax.axis_index("c"))
            for i in range(n_arr):
                seven = zones[i].at[pl.ds(0, (N_DEV - 1) * self.rows[i])] if self.gather else zones[i]
                all_seven = pltpu.make_async_remote_copy(
                    src_ref=seven, dst_ref=seven, send_sem=sems[2 * i], recv_sem=sems[2 * i + 1],
                    device_id=myself, device_id_type=pl.DeviceIdType.MESH)
                all_seven.wait_recv()
                all_seven.wait_send()

        hbm = pl.BlockSpec(memory_space=pltpu.HBM)
        sem = pl.BlockSpec(memory_space=pltpu.SEMAPHORE)
        res = pl.pallas_call(
            body, name=self.name + "_wait",
            in_specs=[hbm] * (2 * n_arr) + [sem] * (2 * n_arr) + [pl.BlockSpec(memory_space=pl.ANY)],
            out_specs=[hbm] * (2 * n_arr), out_shape=[pltpu.HBM(a.shape, a.dtype) for a in self.thru],
            input_output_aliases={i: i for i in range(2 * n_arr)},
            compiler_params=pltpu.CompilerParams(has_side_effects=pltpu.SideEffectType.DATAFLOW_SIDE_EFFECTING),
        )(*self.thru, *self.sems, after)
        return res[:n_arr], res[n_arr:]


def _my_slot():
    me = 4 * lax.axis_index("x") + 2 * lax.axis_index("y") + lax.axis_index("c")
    return me.astype(jnp.int32).reshape(1)


def _place_own(gathered, block, me, *, name):
    r, c = block.shape

    def body(me_ref, b_ref, g_ref, o_ref):
        o_ref[...] = b_ref[...]

    return pl.pallas_call(
        body, name=name, out_shape=jax.ShapeDtypeStruct(gathered.shape, gathered.dtype),
        grid_spec=pltpu.PrefetchScalarGridSpec(
            num_scalar_prefetch=1, grid=(1,),
            in_specs=[pl.BlockSpec((r, c), lambda i, me_ref: (0, 0)), pl.BlockSpec(memory_space=pl.ANY)],
            out_specs=pl.BlockSpec((r, c), lambda i, me_ref: (me_ref[0], 0))),
        input_output_aliases={2: 0}, compiler_params=_params(("arbitrary",)),
    )(me, block, gathered)


def _elementwise_tiles(r, c):
    if r % 128 == 0:
        return 128, c
    return r, (256 if c % 256 == 0 else c)


def _adamw_math(g, w, m, v):
    nm = ADAM_B1 * m + (1.0 - ADAM_B1) * g
    nv = ADAM_B2 * v + (1.0 - ADAM_B2) * (g * g)
    m_hat = nm / (1.0 - ADAM_B1 ** ADAM_STEP)
    v_hat = nv / (1.0 - ADAM_B2 ** ADAM_STEP)
    return -ADAM_LR * (m_hat / (jnp.sqrt(v_hat) + ADAM_EPS) + ADAM_WD * w), nm, nv


def _sum_parts(me_ref, own_ref, p_ref, r):
    own = own_ref[...].astype(F32)
    g = None
    for d in range(N_DEV):
        k = jnp.bitwise_xor(me_ref[0], d)
        term = jnp.where(k == 0, own, p_ref[jnp.maximum(k, 1) - 1].astype(F32))
        g = term if g is None else g + term
    return g[0:r, :]


def _sum_adamw(me, sent, stride, parts, r, w=None, m=None, v=None, *, name):
    _, own_rows, cdim = parts.shape
    assert stride is None or stride == own_rows
    tc = 256 if cdim % 256 == 0 else cdim
    update = w is not None

    def body(me_ref, own_ref, p_ref, *refs):
        g = _sum_parts(me_ref, own_ref, p_ref, r)
        if update:
            w_ref, m_ref, v_ref, g_ref, d_ref, nm_ref, nv_ref = refs
            d_ref[...], nm_ref[...], nv_ref[...] = _adamw_math(g, w_ref[...], m_ref[...], v_ref[...])
        else:
            g_ref, = refs
        g_ref[...] = g

    blk = pl.BlockSpec((r, tc), lambda j, me_ref: (0, j))
    own_spec = pl.BlockSpec((own_rows, tc), (lambda j, me_ref: (0, j)) if stride is None else (lambda j, me_ref: (me_ref[0], j)))
    n_out = 4 if update else 1
    res = pl.pallas_call(
        body, name=name, out_shape=[jax.ShapeDtypeStruct((r, cdim), F32)] * n_out,
        grid_spec=pltpu.PrefetchScalarGridSpec(
            num_scalar_prefetch=1, grid=(cdim // tc,),
            in_specs=[own_spec, pl.BlockSpec((N_DEV - 1, own_rows, tc), lambda j, me_ref: (0, 0, j))]
            + ([blk] * 3 if update else []),
            out_specs=[blk] * n_out),
        compiler_params=_params(("parallel",)),
    )(me, sent, parts, *((w, m, v) if update else ()))
    return list(res)


def _sum_adamw_small(me, sents, parts, ws, ms, vs, *, name):
    n = len(sents)

    def body(me_ref, *refs):
        ins, outs = refs[:5 * n], refs[5 * n:]
        for i in range(n):
            own_ref, p_ref, w_ref, m_ref, v_ref = ins[5 * i:5 * i + 5]
            own, g = own_ref[...], None
            for d in range(N_DEV):
                k = jnp.bitwise_xor(me_ref[0], d)
                term = jnp.where(k == 0, own, p_ref[jnp.maximum(k, 1) - 1])
                g = term if g is None else g + term
            outs[4 * i][...] = g
            outs[4 * i + 1][...], outs[4 * i + 2][...], outs[4 * i + 3][...] = _adamw_math(g, w_ref[...], m_ref[...], v_ref[...])

    vmem = pl.BlockSpec(memory_space=pltpu.VMEM)
    args = [a for group in zip(sents, parts, ws, ms, vs) for a in group]
    res = pl.pallas_call(
        body, name=name, in_specs=[pl.BlockSpec(memory_space=pltpu.SMEM)] + [vmem] * (5 * n), out_specs=[vmem] * (4 * n),
        out_shape=[jax.ShapeDtypeStruct(s.shape, F32) for s in sents for _ in range(4)],
        compiler_params=_params(),
    )(me, *args)
    return [list(res[4 * i:4 * i + 4]) for i in range(n)]


def _sum_adamw_rows(me, sent, parts, ws, ms, vs, *, name):
    n = len(ws)

    def body(me_ref, own_ref, p_ref, *refs):
        ins, outs, g_sc = refs[:3 * n], refs[3 * n:7 * n], refs[-1]
        own, g = own_ref[...], None
        for d in range(N_DEV):
            k = jnp.bitwise_xor(me_ref[0], d)
            term = jnp.where(k == 0, own, p_ref[jnp.maximum(k, 1) - 1])
            g = term if g is None else g + term
        g_sc[...] = g
        for i in range(n):
            width = ins[3 * i].shape[1]
            gi = g_sc[i:i + 1, 0:width]
            outs[4 * i][...] = gi
            outs[4 * i + 1][...], outs[4 * i + 2][...], outs[4 * i + 3][...] = _adamw_math(
                gi, ins[3 * i][...], ins[3 * i + 1][...], ins[3 * i + 2][...])

    vmem = pl.BlockSpec(memory_space=pltpu.VMEM)
    args = [a for group in zip(ws, ms, vs) for a in group]
    res = pl.pallas_call(
        body, name=name, in_specs=[pl.BlockSpec(memory_space=pltpu.SMEM)] + [vmem] * (2 + 3 * n), out_specs=[vmem] * (4 * n),
        out_shape=[jax.ShapeDtypeStruct(a.shape, F32) for a in ws for _ in range(4)],
        scratch_shapes=[pltpu.VMEM(sent.shape, F32)], compiler_params=_params(),
    )(me, sent, parts, *args)
    return [list(res[4 * i:4 * i + 4]) for i in range(n)]


def _adamw(g, w, m, v, *, name):
    r, cdim = w.shape
    tr, tc = _elementwise_tiles(r, cdim)

    def body(g_ref, w_ref, m_ref, v_ref, d_ref, nm_ref, nv_ref):
        d_ref[...], nm_ref[...], nv_ref[...] = _adamw_math(g_ref[...], w_ref[...], m_ref[...], v_ref[...])

    blk = pl.BlockSpec((tr, tc), lambda i, j: (i, j))
    return list(pl.pallas_call(
        body, name=name, grid=(r // tr, cdim // tc), in_specs=[blk] * 4,
        out_specs=[blk] * 3, out_shape=[jax.ShapeDtypeStruct((r, cdim), F32)] * 3,
        compiler_params=_params(("parallel", "parallel")),
    )(g, w, m, v))


SHARD_ROWS_P = {n: (FF_SHARD_P if 'ffn' in n else IN_SHARD_P if n == 'w_in' else None) for n in SHARDED}


def _to_exchange_layout(name, shard):
    t = shard.T if SHARD_AXIS[name] == 1 else shard
    pad = SHARD_ROWS_P[name]
    return t if pad is None else jnp.pad(t, ((0, pad - t.shape[0]), (0, 0)))


def _expand_w_in(wt):
    wt = wt.reshape(N_DEV, IN_SHARD_P, D)[:, :IN_SHARD].reshape(IN_W, D)
    o = Q_RANK + KV_RANK
    kr1, kr2 = wt[o:o + ROPE // 2], wt[o + ROPE // 2:o + ROPE]
    z = jnp.zeros((LANES - ROPE, D), wt.dtype)
    return jnp.concatenate([wt[:o], wt[o + ROPE:], kr1, kr2, z, -kr2, kr1, z], axis=0)


def _expand_w_uq(wt):
    w = wt.reshape(H, QK, Q_RANK)
    z = jnp.zeros((H, LANES - ROPE, Q_RANK), w.dtype)
    q1, q2 = w[:, NOPE:NOPE + ROPE // 2], w[:, NOPE + ROPE // 2:]
    return jnp.concatenate([w[:, :NOPE].reshape(H * NOPE, Q_RANK),
                            jnp.concatenate([q1, q2, z], axis=1).reshape(H * LANES, Q_RANK),
                            jnp.concatenate([-q2, q1, z], axis=1).reshape(H * LANES, Q_RANK)], axis=0)


def _layout_qk_gain(g):
    g = g.reshape(QK)
    g1, g2, z = g[NOPE:NOPE + ROPE // 2], g[NOPE + ROPE // 2:], jnp.zeros((LANES - ROPE,), g.dtype)
    return jnp.stack([g[:NOPE], jnp.concatenate([g1, g2, z]), jnp.concatenate([g2, g1, z])])


def _rep16(a):
    return jnp.repeat(a, SSM_GRP, axis=0)


def _layout_ssm_in(a_re, a_im, log_dt, b_re, b_im):
    b_r = jnp.transpose(b_re, (0, 2, 1)).reshape(SSM_G * SSM_GRP, SSM_P)
    b_i = jnp.transpose(b_im, (0, 2, 1)).reshape(SSM_G * SSM_GRP, SSM_P)
    ldt = jnp.broadcast_to(log_dt.reshape(SSM_G, 1), (SSM_G, SSM_P))
    return _rep16(a_re), _rep16(a_im), _rep16(ldt), b_r, b_i


def _block_diag_b(bb):
    eye = jnp.eye(SSM_PACK, dtype=bb.dtype)
    b5 = bb.reshape(SSM_G // SSM_PACK, SSM_PACK, SSM_GRP, 1, SSM_P) * eye[None, :, None, :, None]
    return b5.reshape(SSM_G // SSM_PACK, SSM_PACK * SSM_GRP, SSM_PACK * SSM_P)


def _block_diag_c(cc):
    eye = jnp.eye(SSM_PACK, dtype=cc.dtype)
    c5 = jnp.transpose(cc, (0, 2, 1)).reshape(SSM_G // SSM_PACK, SSM_PACK, SSM_P, 1, SSM_GRP) * eye[None, :, None, :, None]
    return c5.reshape(SSM_G // SSM_PACK, SSM_PACK * SSM_P, SSM_PACK * SSM_GRP)


def _time_perm(a, inverse=False):
    s, w = a.shape
    c = SCAN_CHUNKS
    if inverse:
        return jnp.transpose(a.reshape(s // c, c, w), (1, 0, 2)).reshape(s, w)
    return jnp.transpose(a.reshape(c, s // c, w), (1, 0, 2)).reshape(s, w)


class _Weights:
    def __init__(self, groups=(), landed=None, me=None):
        self.groups, self.landed, self.me = list(groups), dict(landed or {}), me

    def get(self, name, after):
        if name not in self.landed:
            names, exchange = next(g for g in self.groups if name in g[0])
            for n, block, gathered in zip(names, *exchange.wait(after)):
                self.landed[n] = _place_own(gathered, block, self.me, name="place_" + n)
        return self.landed[name]

    def __getitem__(self, name):
        return self.landed[name]


def _ffn_gate_up(h, w_gt, w_ut, *, name, tm=512, tn=1408):
    s, k = h.shape
    n = w_gt.shape[0]
    tm, tn = min(tm, s), _tile(n, tn)
    dims = (((1,), (1,)), ((), ()))

    def body(h_ref, wg_ref, wu_ref, g_ref, u_ref, a_ref):
        hb = h_ref[...].astype(BF16)
        gate = lax.dot_general(hb, wg_ref[...], dims, preferred_element_type=F32)
        up = lax.dot_general(hb, wu_ref[...], dims, preferred_element_type=F32)
        g_ref[...] = gate.astype(BF16)
        u_ref[...] = up.astype(BF16)
        a_ref[...] = _f_swiglu(gate, up)

    w_spec = pl.BlockSpec((tn, k), lambda j, i: (j, 0))
    o_spec = pl.BlockSpec((tm, tn), lambda j, i: (i, j))
    return pl.pallas_call(
        body, name=name, grid=(n // tn, s // tm), in_specs=[pl.BlockSpec((tm, k), lambda j, i: (i, 0)), w_spec, w_spec],
        out_specs=[o_spec] * 3, out_shape=[jax.ShapeDtypeStruct((s, n), BF16)] * 3,
        compiler_params=_params(("parallel", "parallel")),
    )(h, w_gt, w_ut)


def _ffn_dgate_dup(dx_out, w_d, gate, up, *, name, tm=512, tn=1408, deps=()):
    s, k = dx_out.shape
    n = w_d.shape[0]
    tm, tn = min(tm, s), _tile(n, tn)
    deps = [d for d in deps if d is not None]

    def body(dx_ref, wd_ref, g_ref, u_ref, *refs):
        dg_ref, du_ref = refs[len(deps):]
        dact = 0.5 * lax.dot_general(dx_ref[...].astype(BF16), wd_ref[...], (((1,), (1,)), ((), ())),
                                     preferred_element_type=F32)
        _, vjp = jax.vjp(_f_swiglu, g_ref[...].astype(F32), u_ref[...].astype(F32))
        dgate, dup = vjp(dact.astype(BF16))
        dg_ref[...] = dgate.astype(BF16)
        du_ref[...] = dup.astype(BF16)

    o_spec = pl.BlockSpec((tm, tn), lambda j, i: (i, j))
    return pl.pallas_call(
        body, name=name, grid=(n // tn, s // tm),
        in_specs=[pl.BlockSpec((tm, k), lambda j, i: (i, 0)), pl.BlockSpec((tn, k), lambda j, i: (j, 0)), o_spec, o_spec]
        + [pl.BlockSpec(d.shape, lambda j, i: (0, 0)) for d in deps],
        out_specs=[o_spec] * 2, out_shape=[jax.ShapeDtypeStruct((s, n), BF16)] * 2,
        compiler_params=_params(("parallel", "parallel")),
    )(dx_out, w_d, gate, up, *deps)


def _ffn_dh(dgate, dup, w_gt, w_ut, *, name, tm=512):
    s, k = dgate.shape
    n = w_gt.shape[1]
    tm = min(tm, s)

    def body(dg_ref, du_ref, wg_ref, wu_ref, o_ref):
        o_ref[...] = (jnp.dot(dg_ref[...], wg_ref[...], preferred_element_type=F32)
                      + jnp.dot(du_ref[...], wu_ref[...], preferred_element_type=F32)).astype(o_ref.dtype)

    a_spec = pl.BlockSpec((tm, k), lambda i: (i, 0))
    w_spec = pl.BlockSpec((k, n), lambda i: (0, 0))
    return pl.pallas_call(
        body, name=name, grid=(s // tm,), in_specs=[a_spec, a_spec, w_spec, w_spec],
        out_specs=pl.BlockSpec((tm, n), lambda i: (i, 0)), out_shape=jax.ShapeDtypeStruct((s, n), BF16),
        compiler_params=_params(("parallel",)),
    )(dgate, dup, w_gt, w_ut)


def _ffn_fwd(x, g, wc, tag, deps=()):
    h = _rowwise(_f_norm, [x], [g], [(D, BF16)], name=tag + "_norm", deps=deps)[0]
    gate, up, act = _ffn_gate_up(h, wc.get(tag + '_w_gate', h), wc[tag + '_w_up'], name=tag + "_gate_up")
    x_out = _mm(act, wc.get(tag + '_w_down', act), res=x, scale=0.5, name=tag + "_down")
    return x_out, (h, gate, up, act)


def _ffn_bwd(x, g, wc, saved, dx_out, tag, send, deps=()):
    h, gate, up, act = saved
    w_gt, w_ut, w_d = (wc.get(tag + n, h) for n in ('_w_gate', '_w_up', '_w_down'))
    d_d = _mm(act, dx_out, ta=True, scale=0.5, out_dtype=GRAD_DTYPE, name=tag + "_dwdown", deps=deps)
    token = send({tag + '_w_down': d_d})
    dgate, dup = _ffn_dgate_dup(dx_out, w_d, gate, up, name=tag + "_dgate_dup", deps=[token])
    d_gt = _mm(dgate, h, ta=True, out_dtype=GRAD_DTYPE, name=tag + "_dwgate")
    token = send({tag + '_w_gate': d_gt})
    d_ut = _mm(dup, h, ta=True, out_dtype=GRAD_DTYPE, name=tag + "_dwup", deps=[token])
    token = send({tag + '_w_up': d_ut})
    dh = _ffn_dh(dgate, dup, w_gt, w_ut, name=tag + "_dh")
    dx, dg = _rowwise_bwd(_f_norm, [x], [g], [dh], row_grads={0: F32}, const_grads=[0], adds={0: dx_out},
                          name=tag + "_norm_bwd", deps=[token])
    return dx, dg


def _local_step(x, mem, cos, sin, target, wc, ws, send, deps=(), send_small=None):
    gs = {}

    x1, sv1 = _ffn_fwd(x, ws['ffn1_norm'], wc, "ffn1", deps=deps)

    h2 = _rowwise(_f_norm, [x1], [ws['mix_norm']], [(D, BF16)], name="mix_norm")[0]
    w_in_raw, w_uq_raw = wc.get('w_in', h2), wc.get('mla_w_uq', h2)
    w_in_e = _expand_w_in(w_in_raw)
    w_uq_e = _expand_w_uq(w_uq_raw)
    proj = _mm(h2, w_in_e, tb=True, name="w_in")
    c_q, c_kv = _rowwise(_f_prep1, [proj], [ws['q_norm'], ws['kv_norm']], [(Q_RANK, BF16), (KV_RANK, BF16)], name="mla_prep1")
    qall = _mm(c_q, w_uq_e, tb=True, name="w_uq")
    kv = _mm(c_kv, wc['mla_w_ukv'], tb=True, name="w_ukv")
    kr = _rowwise(_f_kr, [proj], [], [(2 * LANES, F32)], name="mla_kr")[0]
    q, k, v = _prep2_fwd(qall, kv, kr, cos, sin, ws['qk_gq'], ws['qk_gk'])
    o_mla, lse = _attn_fwd(q, k, v)

    u = proj[:, Q_RANK + KV_RANK:Q_RANK + KV_RANK + SSM_W]
    u_p = _time_perm(u)
    disc_in = [ws['ssm_lr'], ws['ssm_li'], ws['ssm_ldt'], ws['ssm_br'], ws['ssm_bi']]
    ar16, ai16, bbr, bbi = _rowwise(_f_disc, disc_in, [], [(SSM_P, F32)] * 4, name="s5_disc")
    a_r = ar16[::SSM_GRP].reshape(1, SSM_N)
    a_i = ai16[::SSM_GRP].reshape(1, SSM_N)
    bblk_r, bblk_i = _block_diag_b(bbr).astype(BF16), _block_diag_b(bbi).astype(BF16)
    cblk_r, cblk_i = _block_diag_c(ws['ssm_cr']).astype(BF16), _block_diag_c(-ws['ssm_ci']).astype(BF16)
    xr, xi, yc = _s5_scan(u_p, bblk_r, bblk_i, a_r, a_i, reverse=False, tb=False, readout=(cblk_r, cblk_i),
                          name="s5_scan_fwd")
    g_p = _rowwise(_f_s5_gelu, [yc, u_p], [ws['ssm_d']], [(SSM_W, F32)], name="s5_gelu")[0]
    z_p = _mm(g_p, wc['ssm_w_glu'], name="s5_glu")
    g_t, z_t = _time_perm(g_p, inverse=True), _time_perm(z_p, inverse=True)
    on_consts = [ws['ssm_b_glu'], ws['out_norm_mla'], ws['out_norm_ssm']]
    ycat = _rowwise(_f_outnorm, [o_mla, g_t, z_t], on_consts, [(D, BF16)], name="out_norm")[0]
    x2 = _mm(ycat, wc['w_o'], res=x1, name="w_o")

    hx = _rowwise(_f_norm, [x2], [ws['xattn_norm']], [(D, BF16)], name="xattn_norm")[0]
    xq = _mm(hx, wc['xattn_w_q'], name="xattn_q")
    mn = _rowwise(_f_norm, [mem], [ws['mem_norm']], [(D, BF16)], name="mem_norm")[0]
    kvm = _mm(mn, wc['xattn_w_kv'], name="xattn_kv")
    xkn, xv = _rowwise(_f_memk, [kvm], [ws['xattn_k_norm']], [(H * XH, BF16), (H * XH, BF16)], name="xattn_knorm")
    xo = _xattn_fwd(xq, xkn, xv, ws['xattn_q_norm'])
    x3 = _mm(xo, wc['xattn_w_o'], tb=True, res=x2, name="xattn_o")

    x4, sv2 = _ffn_fwd(x3, ws['ffn2_norm'], wc, "ffn2")

    def f_loss(yb, tb):
        err = yb - tb
        return err * (1.0 / D), jnp.broadcast_to(jnp.sum(jnp.sum(err * err, axis=1, keepdims=True), axis=0, keepdims=True) * (0.5 / D), (1, LANES))

    dx4, loss = _rowwise(f_loss, [x4, target], [], [(D, F32)], [(1, LANES)], name="loss")

    dx3, gs['ffn2_norm'] = _ffn_bwd(x3, ws['ffn2_norm'], wc, sv2, dx4, "ffn2", send)

    dxo = _mm(dx3, wc['xattn_w_o'], out_dtype=BF16, name="xattn_o_dx")
    send({'xattn_w_o': _mm(dx3, xo, ta=True, out_dtype=GRAD_DTYPE, name="xattn_o_dw")})
    dxq, dxkn, dxv, gs['xattn_q_norm'] = _xattn_bwd(xq, xkn, xv, ws['xattn_q_norm'], dxo)
    dkvm, gs['xattn_k_norm'] = _rowwise_bwd(_f_memk, [kvm], [ws['xattn_k_norm']], [dxkn, dxv], row_grads={0: BF16},
                                            const_grads=[0], name="xattn_knorm_bwd")
    send({'xattn_w_kv': _mm(mn, dkvm, ta=True, out_dtype=GRAD_DTYPE, name="xattn_kv_dw")})
    dmn = _mm(dkvm, wc['xattn_w_kv'], tb=True, out_dtype=BF16, name="xattn_kv_dx")
    gs['mem_norm'] = _rowwise_bwd(_f_norm, [mem], [ws['mem_norm']], [dmn], row_grads={}, const_grads=[0], name="mem_norm_bwd")[0]
    token = send({'xattn_w_q': _mm(hx, dxq, ta=True, out_dtype=GRAD_DTYPE, name="xattn_q_dw")})
    dhx = _mm(dxq, wc['xattn_w_q'], tb=True, out_dtype=BF16, name="xattn_q_dx")
    dx2, gs['xattn_norm'] = _rowwise_bwd(_f_norm, [x2], [ws['xattn_norm']], [dhx], row_grads={0: F32}, const_grads=[0],
                                         adds={0: dx3}, name="xattn_norm_bwd", deps=[token])

    dycat = _mm(dx2, wc['w_o'], tb=True, out_dtype=BF16, name="w_o_dx")
    send({'w_o': _mm(ycat, dx2, ta=True, out_dtype=GRAD_DTYPE, name="w_o_dw")})
    do_mla, dg_t, dz_t, gs['ssm_b_glu'], gs['out_norm_mla'], gs['out_norm_ssm'] = _rowwise_bwd(
        _f_outnorm, [o_mla, g_t, z_t], on_consts, [dycat], row_grads={0: F32, 1: F32, 2: BF16}, const_grads=[0, 1, 2],
        name="out_norm_bwd")

    dz_p, dg_p = _time_perm(dz_t), _time_perm(dg_t)
    send({'ssm_w_glu': _mm(g_p, dz_p, ta=True, out_dtype=GRAD_DTYPE, name="s5_glu_dw")})
    dg_p = _mm(dz_p, wc['ssm_w_glu'], tb=True, res=dg_p, name="s5_glu_dx")
    dyc, du_d, gs['ssm_d'] = _rowwise_bwd(_f_s5_gelu, [yc, u_p], [ws['ssm_d']], [dg_p], row_grads={0: BF16, 1: F32},
                                          const_grads=[0], name="s5_gelu_bwd")
    lam_r, lam_i = _s5_scan(dyc, cblk_r, cblk_i, a_r, -a_i, reverse=True, tb=True, name="s5_scan_bwd")
    du_p, d_bblk_r, d_bblk_i, d_cblk_r, d_cblk_i, d_ar, d_ai = _s5_grads(lam_r, lam_i, xr, xi, u_p, dyc, du_d,
                                                                        bblk_r, bblk_i)
    du = _time_perm(du_p, inverse=True)
    gs['ssm_cr'] = jax.linear_transpose(_block_diag_c, ws['ssm_cr'])(d_cblk_r)[0]
    gs['ssm_ci'] = -jax.linear_transpose(_block_diag_c, ws['ssm_ci'])(d_cblk_i)[0]
    d_bbr = jax.linear_transpose(_block_diag_b, bbr)(d_bblk_r)[0]
    d_bbi = jax.linear_transpose(_block_diag_b, bbi)(d_bblk_i)[0]
    d_ar16 = jnp.zeros((SSM_G * SSM_GRP, SSM_P), F32).at[::SSM_GRP].set(d_ar.reshape(SSM_G, SSM_P))
    d_ai16 = jnp.zeros((SSM_G * SSM_GRP, SSM_P), F32).at[::SSM_GRP].set(d_ai.reshape(SSM_G, SSM_P))
    gs['ssm_lr'], gs['ssm_li'], gs['ssm_ldt'], gs['ssm_br'], gs['ssm_bi'] = _rowwise_bwd(
        _f_disc, disc_in, [], [d_ar16, d_ai16, d_bbr, d_bbi], row_grads={i: F32 for i in range(5)}, const_grads=[],
        name="s5_disc_bwd")

    delta, do_b = _rowwise(_f_delta, [do_mla, o_mla], [], [(H * LANES, F32), (H * VD, BF16)], name="mla_delta")
    dq, dk, dv = _attn_bwd(q, k, v, do_b, lse, delta)
    dqall, dkv, dkr, gs['qk_gq'], gs['qk_gk'] = _prep2_bwd(qall, kv, kr, cos, sin, ws['qk_gq'], ws['qk_gk'], dq, dk, dv)
    d_w_uq_e = _mm(dqall, c_q, ta=True, name="w_uq_dw")
    send({'mla_w_uq': jax.linear_transpose(_expand_w_uq, jax.ShapeDtypeStruct(w_uq_raw.shape, F32))(d_w_uq_e)[0]})
    dc_q = _mm(dqall, w_uq_e, out_dtype=BF16, name="w_uq_dx")
    send({'mla_w_ukv': _mm(dkv, c_kv, ta=True, out_dtype=GRAD_DTYPE, name="w_ukv_dw")})
    dc_kv = _mm(dkv, wc['mla_w_ukv'], out_dtype=BF16, name="w_ukv_dx")

    def f_prep1_bwd(pb, dcq, dckv, dub, dkrb, gq, gkv):
        _, vjp = jax.vjp(_f_prep1, pb[:, :Q_RANK + KV_RANK], gq, gkv)
        dpa, dgq, dgkv = vjp((dcq.astype(BF16), dckv.astype(BF16)))
        return jnp.concatenate([dpa, dub, dkrb], axis=-1), dgq, dgkv

    dproj, gs['q_norm'], gs['kv_norm'] = _rowwise(
        f_prep1_bwd, [proj, dc_q, dc_kv, du, dkr], [ws['q_norm'], ws['kv_norm']], [(IN_WP, BF16)],
        [(1, Q_RANK), (1, KV_RANK)], name="mla_prep1_bwd")
    d_w_in_e = _mm(dproj, h2, ta=True, name="w_in_dw")
    token = send({'w_in': jax.linear_transpose(_expand_w_in, jax.ShapeDtypeStruct(w_in_raw.shape, F32))(d_w_in_e)[0]})
    dh2 = _mm(dproj, w_in_e, out_dtype=BF16, name="w_in_dx")
    dx1, gs['mix_norm'] = _rowwise_bwd(_f_norm, [x1], [ws['mix_norm']], [dh2], row_grads={0: F32}, const_grads=[0],
                                       adds={0: dx2}, name="mix_norm_bwd", deps=[token])

    token = send_small(gs, loss) if send_small is not None else None
    dx0, gs['ffn1_norm'] = _ffn_bwd(x, ws['ffn1_norm'], wc, sv1, dx1, "ffn1", send, deps=[token])
    return loss, dx0, gs


def _prep2_fwd(qall, kv, kr, cos, sin, gq, gk):
    return _rowwise(_f_prep2, [qall, kv, kr, cos, sin], [gq, gk], [(H * HQ, BF16), (H * HQ, BF16), (H * VD, BF16)],
                    ts=256, name="mla_prep2")


def _prep2_bwd(qall, kv, kr, cos, sin, gq, gk, dq, dk, dv):
    return _rowwise_bwd(_f_prep2, [qall, kv, kr, cos, sin], [gq, gk], [dq, dk, dv], row_grads={0: BF16, 1: BF16, 2: F32},
                        const_grads=[0, 1], ts=256, name="mla_prep2_bwd")


def _rope_tables(pos):
    half = ROPE // 2
    inv = ROPE_THETA ** (-jnp.arange(half, dtype=F32) / half)
    ang = pos.astype(F32)[:, None] * inv[None, :]
    z = jnp.zeros((pos.shape[0], LANES - ROPE), F32)
    cos, sin = jnp.cos(ang), jnp.sin(ang)
    return jnp.concatenate([cos, cos, z], axis=-1), jnp.concatenate([sin, sin, z], axis=-1)


def _small_layout(p):
    lr, li, ldt, br, bi = _layout_ssm_in(p['ssm_a_re'], p['ssm_a_im'], p['ssm_log_dt'], p['ssm_b_re'], p['ssm_b_im'])
    return {
        'ffn1_norm': p['ffn1_norm'].reshape(1, D), 'mix_norm': p['mix_norm'].reshape(1, D),
        'q_norm': p['mla_q_norm'].reshape(1, Q_RANK), 'kv_norm': p['mla_kv_norm'].reshape(1, KV_RANK),
        'qk_gq': _layout_qk_gain(p['mla_qk_norm_q']), 'qk_gk': _layout_qk_gain(p['mla_qk_norm_k']),
        'ssm_lr': lr, 'ssm_li': li, 'ssm_ldt': ldt, 'ssm_br': br, 'ssm_bi': bi,
        'ssm_cr': p['ssm_c_re'], 'ssm_ci': p['ssm_c_im'], 'ssm_d': p['ssm_d'].reshape(1, SSM_W),
        'ssm_b_glu': p['ssm_b_glu'].reshape(1, SSM_W),
        'out_norm_mla': p['out_norm_mla'].reshape(1, SSM_W), 'out_norm_ssm': p['out_norm_ssm'].reshape(1, SSM_W),
        'xattn_norm': p['xattn_norm'].reshape(1, D), 'mem_norm': p['mem_norm'].reshape(1, D),
        'xattn_q_norm': p['xattn_q_norm'].reshape(1, XH), 'xattn_k_norm': p['xattn_k_norm'].reshape(1, XH),
        'ffn2_norm': p['ffn2_norm'].reshape(1, D),
    }


def kernel(x, mem, positions, ffn1_norm, ffn1_w_gate, ffn1_w_up, ffn1_w_down, mix_norm, w_in, mla_q_norm, mla_w_uq, mla_kv_norm, mla_w_ukv, mla_qk_norm_q, mla_qk_norm_k, ssm_a_re, ssm_a_im, ssm_log_dt, ssm_b_re, ssm_b_im, ssm_c_re, ssm_c_im, ssm_d, ssm_w_glu, ssm_b_glu, out_norm_mla, out_norm_ssm, w_o, xattn_norm, mem_norm, xattn_w_q, xattn_w_kv, xattn_q_norm, xattn_k_norm, xattn_w_o, ffn2_norm, ffn2_w_gate, ffn2_w_up, ffn2_w_down, loss_target, m_ffn1_norm, m_ffn1_w_gate, m_ffn1_w_up, m_ffn1_w_down, m_mix_norm, m_w_in, m_mla_q_norm, m_mla_w_uq, m_mla_kv_norm, m_mla_w_ukv, m_mla_qk_norm_q, m_mla_qk_norm_k, m_ssm_a_re, m_ssm_a_im, m_ssm_log_dt, m_ssm_b_re, m_ssm_b_im, m_ssm_c_re, m_ssm_c_im, m_ssm_d, m_ssm_w_glu, m_ssm_b_glu, m_out_norm_mla, m_out_norm_ssm, m_w_o, m_xattn_norm, m_mem_norm, m_xattn_w_q, m_xattn_w_kv, m_xattn_q_norm, m_xattn_k_norm, m_xattn_w_o, m_ffn2_norm, m_ffn2_w_gate, m_ffn2_w_up, m_ffn2_w_down, v_ffn1_norm, v_ffn1_w_gate, v_ffn1_w_up, v_ffn1_w_down, v_mix_norm, v_w_in, v_mla_q_norm, v_mla_w_uq, v_mla_kv_norm, v_mla_w_ukv, v_mla_qk_norm_q, v_mla_qk_norm_k, v_ssm_a_re, v_ssm_a_im, v_ssm_log_dt, v_ssm_b_re, v_ssm_b_im, v_ssm_c_re, v_ssm_c_im, v_ssm_d, v_ssm_w_glu, v_ssm_b_glu, v_out_norm_mla, v_out_norm_ssm, v_w_o, v_xattn_norm, v_mem_norm, v_xattn_w_q, v_xattn_w_kv, v_xattn_q_norm, v_xattn_k_norm, v_xattn_w_o, v_ffn2_norm, v_ffn2_w_gate, v_ffn2_w_up, v_ffn2_w_down):
    args = dict(locals())
    w = {n: args[n] for n in WEIGHTS}
    mom = {n: args['m_' + n] for n in WEIGHTS}
    var = {n: args['v_' + n] for n in WEIGHTS}
    return _step(x, mem, positions, loss_target, w, mom, var)


GATHER_GROUPS = [('ffn1_gu', ['ffn1_w_gate', 'ffn1_w_up']), ('ffn1_down', ['ffn1_w_down']),
                 ('mix', ['w_in', 'mla_w_uq', 'mla_w_ukv', 'ssm_w_glu', 'w_o', 'xattn_w_q', 'xattn_w_kv', 'xattn_w_o']),
                 ('ffn2', ['ffn2_w_gate', 'ffn2_w_up', 'ffn2_w_down'])]
SCATTER_GROUPS = [('ffn2_down', ['ffn2_w_down']), ('ffn2_gate', ['ffn2_w_gate']), ('ffn2_up', ['ffn2_w_up']),
                  ('xattn', ['xattn_w_o', 'xattn_w_kv', 'xattn_w_q']),
                  ('mix', ['w_o', 'ssm_w_glu', 'mla_w_uq', 'mla_w_ukv', 'w_in']),
                  ('ffn1_down', ['ffn1_w_down']), ('ffn1_gate', ['ffn1_w_gate']), ('ffn1_up', ['ffn1_w_up'])]


def _step(x, mem, positions, loss_target, w, mom, var):
    blocks = {n: _to_exchange_layout(n, w[n][0]).astype(BF16) for n in SHARDED}
    gathers, token = [], None
    for tag, names in GATHER_GROUPS:
        ex = _Exchange([blocks[n] for n in names], [blocks[n].shape[0] for n in names], gather=True,
                       name="gather_" + tag, after=token)
        gathers.append((names, ex))
        token = ex.token
    me = _my_slot()
    wc = _Weights(gathers, me=me)

    rows = {n: (blocks[n].shape[0], blocks[n].shape[0]) for n in SHARDED}
    ready, scatters = {}, []

    def send(grads):
        ready.update({n: g.astype(GRAD_DTYPE) for n, g in grads.items()})
        for tag, names in SCATTER_GROUPS:
            if all(n in ready for n in names) and not any(t == tag for t, _, _ in scatters):
                ex = _Exchange([ready[n] for n in names], [rows[n] for n in names], gather=False, name="scatter_" + tag)
                scatters.append((tag, names, ex))
                return ex.token
        return None

    small = {n: w[n][0] for n in SMALL}
    vectors = [n for n in SMALL if small[n].ndim == 1 and n != 'ffn1_norm']
    tensors = [n for n in SMALL if small[n].ndim > 1]
    small_sent = []

    def row(a):
        a = a.reshape(1, -1)
        return jnp.pad(a, ((0, 0), (0, D - a.shape[1])))

    def send_small(gs, loss):
        known = dict(gs, ffn1_norm=jnp.zeros((1, D), F32))
        g_small = jax.linear_transpose(_small_layout, {n: jax.ShapeDtypeStruct(small[n].shape, F32) for n in SMALL})(known)[0]
        rows_ = [row(g_small[n]) for n in vectors] + [row(loss)]
        pack = jnp.concatenate(rows_ + [jnp.zeros(((-len(rows_)) % 8, D), F32)], axis=0)
        arrays = [pack] + [g_small[n] for n in tensors]
        small_sent.append(_Exchange(arrays, [(None, None)] * len(arrays), gather=False, name="scatter_small"))
        return small_sent[0].token

    ws = _small_layout(small)
    cos, sin = _rope_tables(positions[0])
    loss, dx, gs = _local_step(x[0], mem[0], cos, sin, loss_target[0], wc, ws, send, deps=[token], send_small=send_small)
    last_ex = _Exchange([gs['ffn1_norm']], [(None, None)], gather=False, name="scatter_last")

    out, after = {}, dx
    for _, names, ex in scatters:
        for n, sent, p in zip(names, *ex.wait(after)):
            r = w[n][0].shape[SHARD_AXIS[n]]
            if SHARD_AXIS[n] == 0:
                out[n] = _sum_adamw(me, sent, rows[n][0], p, r, w[n][0], mom[n][0], var[n][0], name="adamw_" + n)
            else:
                g = _sum_adamw(me, sent, rows[n][0], p, r, name="sum_" + n)[0].T
                out[n] = [g] + _adamw(g, w[n][0], mom[n][0], var[n][0], name="adamw_" + n)
        after = out[names[-1]][1]
    sent, p = small_sent[0].wait(after)
    state = {n: [t[n][0].reshape(1, -1) if small[n].ndim == 1 else t[n][0] for t in (w, mom, var)] for n in SMALL}
    state['loss'] = [loss[:, :1]] * 3
    res = _sum_adamw_rows(me, sent[0], p[0], *[[state[n][j] for n in vectors + ['loss']] for j in range(3)],
                          name="adamw_vectors")
    out.update(zip(vectors + ['loss'], res))
    where = {n: 1 + i for i, n in enumerate(tensors)}
    big = [n for n in tensors if small[n].ndim == 3]
    for names in [[n] for n in big] + [[n for n in tensors if n not in big]]:
        res = _sum_adamw_small(me, [sent[where[n]] for n in names], [p[where[n]] for n in names],
                               *[[state[n][j] for n in names] for j in range(3)], name="adamw_small_" + names[0])
        out.update(zip(names, res))
    sent, p = last_ex.wait(out[vectors[0]][1])
    out['ffn1_norm'] = _sum_adamw_small(me, sent, p, *[[state['ffn1_norm'][j]] for j in range(3)], name="adamw_last")[0]
    loss_total = out['loss'][0][0, 0]
    outs = [out[n][i].reshape(w[n].shape) for i in range(4) for n in WEIGHTS]
    return (loss_total, dx[None], *outs)
```

```python
import math

import jax
import jax.numpy as jnp
import numpy as np
from jax import lax
from jax.experimental import pallas as pl
from jax.experimental.pallas import tpu as pltpu

F32 = jnp.float32
BF16 = jnp.bfloat16

N_DEV = 8
D = 1024
D_FF = 2752
D_FFP = 2816
MEM_LEN = 256
H = 4
Q_RANK, KV_RANK, NOPE, ROPE, VD = 384, 256, 128, 64, 128
QK = NOPE + ROPE
HQ = 2 * 128
SSM_W, SSM_G, SSM_GRP, SSM_P = 512, 32, 16, 64
SSM_N = SSM_G * SSM_P
SSM_PACK = 8
IN_W = 1216
IN_WP = 1408
XH = 128
EPS = 1e-6
LN2 = math.log(2.0)
ROPE_THETA = 10000.0
SCAN_CHUNKS = 8
SCAN_UNROLL = 8
ADAM_LR, ADAM_B1, ADAM_B2, ADAM_EPS, ADAM_WD, ADAM_STEP = 0.001, 0.9, 0.999, 1e-08, 0.01, 10

VMEM_LIMIT = 56 * 1024 * 1024
ACC_BYTES = 6 * 1024 * 1024
LANES = 128
BF16_ROWS = 16
GRAD_DTYPE = BF16
FF_SHARD = D_FF // N_DEV
FF_SHARD_P = 352
IN_SHARD = IN_W // N_DEV
IN_SHARD_P = 160

WEIGHTS = ['ffn1_norm', 'ffn1_w_gate', 'ffn1_w_up', 'ffn1_w_down', 'mix_norm', 'w_in', 'mla_q_norm', 'mla_w_uq',
           'mla_kv_norm', 'mla_w_ukv', 'mla_qk_norm_q', 'mla_qk_norm_k', 'ssm_a_re', 'ssm_a_im', 'ssm_log_dt',
           'ssm_b_re', 'ssm_b_im', 'ssm_c_re', 'ssm_c_im', 'ssm_d', 'ssm_w_glu', 'ssm_b_glu', 'out_norm_mla',
           'out_norm_ssm', 'w_o', 'xattn_norm', 'mem_norm', 'xattn_w_q', 'xattn_w_kv', 'xattn_q_norm',
           'xattn_k_norm', 'xattn_w_o', 'ffn2_norm', 'ffn2_w_gate', 'ffn2_w_up', 'ffn2_w_down']
SHARD_AXIS = {'ffn1_w_gate': 1, 'ffn1_w_up': 1, 'ffn1_w_down': 0, 'w_in': 1, 'mla_w_uq': 1, 'mla_w_ukv': 1,
              'ssm_w_glu': 0, 'w_o': 0, 'xattn_w_q': 0, 'xattn_w_kv': 0, 'xattn_w_o': 1,
              'ffn2_w_gate': 1, 'ffn2_w_up': 1, 'ffn2_w_down': 0}
SHARDED = [n for n in WEIGHTS if n in SHARD_AXIS]
SMALL = [n for n in WEIGHTS if n not in SHARD_AXIS]


def _params(sem=None):
    return pltpu.CompilerParams(dimension_semantics=sem, vmem_limit_bytes=VMEM_LIMIT)


def _tile(n, cap):
    if n <= cap:
        return n
    best = n
    for t in range(LANES, cap + 1, LANES):
        if n % t == 0:
            best = t
    return best


def _mm(a, b, *, ta=False, tb=False, out_dtype=F32, res=None, scale=1.0, name, tm_cap=512, tn_cap=1408, tk_cap=2816,
        deps=()):
    m, k = (a.shape[1], a.shape[0]) if ta else a.shape
    k2, n = (b.shape[1], b.shape[0]) if tb else b.shape
    assert k == k2, (a.shape, b.shape, ta, tb)
    if ta:
        tk_cap = min(tk_cap, 512)
        tm_cap = 1408
    tm, tn, tk = _tile(m, tm_cap), _tile(n, tn_cap), _tile(k, tk_cap)
    if tm * tn * 4 > ACC_BYTES:
        tn = _tile(n, max(LANES, ACC_BYTES // (4 * tm) // LANES * LANES))
    nk = k // tk
    dims = (((0 if ta else 1,), (1 if tb else 0,)), ((), ()))
    has_res = res is not None

    deps = [d for d in deps if d is not None]

    def body(*refs):
        a_ref, b_ref = refs[:2]
        r_ref = refs[2] if has_res else None
        o_ref, acc_ref = refs[-2:]
        kk = pl.program_id(2)

        @pl.when(kk == 0)
        def _():
            acc_ref[...] = jnp.zeros_like(acc_ref)

        acc_ref[...] += lax.dot_general(a_ref[...].astype(BF16), b_ref[...].astype(BF16), dims,
                                        preferred_element_type=F32)

        @pl.when(kk == nk - 1)
        def _():
            out = acc_ref[...]
            if scale != 1.0:
                out = out * scale
            if has_res:
                out = out + r_ref[...].astype(F32)
            o_ref[...] = out.astype(o_ref.dtype)

    a_spec = pl.BlockSpec((tk, tm), lambda i, j, kk: (kk, i)) if ta else pl.BlockSpec((tm, tk), lambda i, j, kk: (i, kk))
    b_spec = pl.BlockSpec((tn, tk), lambda i, j, kk: (j, kk)) if tb else pl.BlockSpec((tk, tn), lambda i, j, kk: (kk, j))
    o_spec = pl.BlockSpec((tm, tn), lambda i, j, kk: (i, j))
    in_specs = [a_spec, b_spec] + ([o_spec] if has_res else []) + [pl.BlockSpec(d.shape, lambda i, j, kk: (0, 0)) for d in deps]
    args = (a, b) + ((res,) if has_res else ()) + tuple(deps)
    return pl.pallas_call(
        body, name=name, grid=(m // tm, n // tn, nk), in_specs=in_specs, out_specs=o_spec,
        out_shape=jax.ShapeDtypeStruct((m, n), out_dtype), scratch_shapes=[pltpu.VMEM((tm, tn), F32)],
        compiler_params=_params(("parallel", "parallel", "arbitrary")),
    )(*args)


def _mm_grouped(a, b, *, tb=False, res=None, out_dtype=F32, name, tm=512):
    s = a.shape[0]
    g = b.shape[0]
    nb, ka = (b.shape[1], b.shape[2]) if tb else (b.shape[2], b.shape[1])
    assert a.shape[1] == g * ka
    tm = min(tm, s)
    dims = (((1,), (1 if tb else 0,)), ((), ()))
    has_res = res is not None

    def body(*refs):
        if has_res:
            a_ref, b_ref, r_ref, o_ref = refs
        else:
            a_ref, b_ref, o_ref = refs
        out = lax.dot_general(a_ref[...].astype(BF16), b_ref[...].astype(BF16), dims, preferred_element_type=F32)
        if has_res:
            out = out + r_ref[...].astype(F32)
        o_ref[...] = out.astype(o_ref.dtype)

    o_spec = pl.BlockSpec((tm, nb), lambda i, j: (i, j))
    in_specs = [pl.BlockSpec((tm, ka), lambda i, j: (i, j)), pl.BlockSpec((None,) + b.shape[1:], lambda i, j: (j, 0, 0))]
    return pl.pallas_call(
        body, name=name, grid=(s // tm, g), in_specs=in_specs + ([o_spec] if has_res else []), out_specs=o_spec,
        out_shape=jax.ShapeDtypeStruct((s, g * nb), out_dtype), compiler_params=_params(("parallel", "parallel")),
    )(a, b, *((res,) if has_res else ()))


def _mm_grouped_tn(a, b, *, ka, kb, name, tk=512):
    s = a.shape[0]
    g = a.shape[1] // ka
    assert b.shape[1] == g * kb
    tk = min(tk, s)
    nk = s // tk

    def body(a_ref, b_ref, o_ref):
        part = lax.dot_general(a_ref[...].astype(BF16), b_ref[...].astype(BF16), (((0,), (0,)), ((), ())),
                               preferred_element_type=F32)

        @pl.when(pl.program_id(1) == 0)
        def _():
            o_ref[...] = part

        @pl.when(pl.program_id(1) > 0)
        def _():
            o_ref[...] += part

    return pl.pallas_call(
        body, name=name, grid=(g, nk),
        in_specs=[pl.BlockSpec((tk, ka), lambda j, kk: (kk, j)), pl.BlockSpec((tk, kb), lambda j, kk: (kk, j))],
        out_specs=pl.BlockSpec((None, ka, kb), lambda j, kk: (j, 0, 0)),
        out_shape=jax.ShapeDtypeStruct((g, ka, kb), F32), compiler_params=_params(("parallel", "arbitrary")),
    )(a, b)


def _rowwise(fn, rows, consts, outs, accs=(), *, ts=512, name, deps=()):
    s = rows[0].shape[0]
    ts = min(ts, s)
    assert s % ts == 0
    n_rows, n_consts, n_outs = len(rows), len(consts), len(outs)
    deps = [d for d in deps if d is not None]
    consts = list(consts) + deps

    def body(*refs):
        ins = [r[...] for r in refs[:n_rows + n_consts]]
        res = fn(*ins)
        res = tuple(res) if isinstance(res, (tuple, list)) else (res,)
        out_refs = refs[n_rows + len(consts):]
        for o_ref, val in zip(out_refs[:n_outs], res[:n_outs]):
            o_ref[...] = val.astype(o_ref.dtype)
        if accs:
            first = pl.program_id(0) == 0

            @pl.when(first)
            def _():
                for a_ref, val in zip(out_refs[n_outs:], res[n_outs:]):
                    a_ref[...] = val.astype(F32)

            @pl.when(jnp.logical_not(first))
            def _():
                for a_ref, val in zip(out_refs[n_outs:], res[n_outs:]):
                    a_ref[...] += val.astype(F32)

    in_specs = [pl.BlockSpec((ts, r.shape[1]), lambda i: (i, 0)) for r in rows]
    in_specs += [pl.BlockSpec(c.shape, lambda i: (0, 0)) for c in consts]
    out_specs = [pl.BlockSpec((ts, w), lambda i: (i, 0)) for w, _ in outs]
    out_specs += [pl.BlockSpec(tuple(sh), lambda i: (0, 0)) for sh in accs]
    out_shape = [jax.ShapeDtypeStruct((s, w), dt) for w, dt in outs]
    out_shape += [jax.ShapeDtypeStruct(tuple(sh), F32) for sh in accs]
    res = pl.pallas_call(
        body, name=name, grid=(s // ts,), in_specs=in_specs, out_specs=out_specs, out_shape=out_shape,
        compiler_params=_params(("arbitrary",)),
    )(*rows, *consts)
    return res


def _rowwise_bwd(f, rows, consts, cts, *, row_grads, const_grads, adds=None, ts=512, name, deps=()):
    adds = adds or {}
    n_rows, n_consts, n_cts = len(rows), len(consts), len(cts)
    add_keys = sorted(adds)
    rg = sorted(row_grads)
    cg = sorted(const_grads)

    def fn(*args):
        r = args[:n_rows]
        c = args[n_rows:n_rows + n_consts]
        ct = args[n_rows + n_consts:n_rows + n_consts + n_cts]
        extra = args[n_rows + n_consts + n_cts:]
        outs, vjp = jax.vjp(f, *r, *c)
        outs = tuple(outs) if isinstance(outs, (tuple, list)) else (outs,)
        cot = tuple(g.astype(o.dtype) for g, o in zip(ct, outs))
        grads = vjp(cot if len(cot) > 1 else cot[0])
        res = []
        for i in rg:
            g = grads[i].astype(F32)
            if i in adds:
                g = g + extra[add_keys.index(i)].astype(F32)
            res.append(g)
        for i in cg:
            res.append(grads[n_rows + i])
        return tuple(res)

    rows_all = list(rows) + list(cts) + [adds[i] for i in add_keys]
    def fn2(*args):
        nr = len(rows_all)
        rr, cc = args[:nr], args[nr:]
        return fn(*rr[:n_rows], *cc, *rr[n_rows:])

    outs = [(rows[i].shape[1], row_grads[i]) for i in rg]
    accs = [consts[i].shape for i in cg]
    return _rowwise(fn2, rows_all, list(consts), outs, accs, ts=ts, name=name, deps=deps)


def _rms(x, g):
    xf = x.astype(F32)
    return xf * lax.rsqrt(jnp.mean(xf * xf, axis=-1, keepdims=True) + EPS) * g.astype(F32)


def _sigmoid(x):
    return 1.0 / (1.0 + jnp.exp(-x))


def _f_norm(x, g):
    return _rms(x, g).astype(BF16)


def _f_swiglu(gate, up):
    gate, up = gate.astype(F32), up.astype(F32)
    return (gate * _sigmoid(gate) * up).astype(BF16)


def _f_prep1(proj, gq, gkv):
    return _rms(proj[:, :Q_RANK], gq).astype(BF16), _rms(proj[:, Q_RANK:Q_RANK + KV_RANK], gkv).astype(BF16)


def _f_kr(proj):
    return (proj[:, Q_RANK + KV_RANK + SSM_W:],)


def _f_prep2(qall, kv, kr2, cos, sin, gq, gk):
    kr, krs = kr2[:, :LANES].astype(F32), kr2[:, LANES:].astype(F32)
    k_rot = kr * gk[1:2] * cos + krs * gk[2:3] * sin
    k_ss = jnp.sum(kr * kr, axis=-1, keepdims=True)
    q_scale = QK ** -0.5 / LN2
    qs, ks, vs = [], [], []
    for h in range(H):
        qn = qall[:, h * LANES:(h + 1) * LANES].astype(F32)
        qr = qall[:, (H + h) * LANES:(H + h + 1) * LANES].astype(F32)
        qrs = qall[:, (2 * H + h) * LANES:(2 * H + h + 1) * LANES].astype(F32)
        rstd = lax.rsqrt((jnp.sum(qn * qn, axis=-1, keepdims=True) + jnp.sum(qr * qr, axis=-1, keepdims=True)) / QK + EPS)
        rstd = rstd * q_scale
        qs += [qn * gq[0:1] * rstd, (qr * gq[1:2] * cos + qrs * gq[2:3] * sin) * rstd]
        kn = kv[:, 2 * h * LANES:(2 * h + 1) * LANES].astype(F32)
        rstd_k = lax.rsqrt((jnp.sum(kn * kn, axis=-1, keepdims=True) + k_ss) / QK + EPS)
        ks += [kn * gk[0:1] * rstd_k, k_rot * rstd_k]
        vs.append(kv[:, (2 * h + 1) * LANES:(2 * h + 2) * LANES])
    return (jnp.concatenate(qs, axis=-1).astype(BF16), jnp.concatenate(ks, axis=-1).astype(BF16),
            jnp.concatenate(vs, axis=-1).astype(BF16))


def _gelu(x):
    return 0.5 * x * (1.0 + jnp.tanh(math.sqrt(2.0 / math.pi) * (x + 0.044715 * (x * x * x))))


def _f_s5_gelu(yc, u, d):
    return _gelu(yc.astype(F32) + d * u.astype(F32))


def _f_outnorm(o_mla, g, z, b_glu, g_om, g_os):
    y_ssm = g * _sigmoid(z + b_glu)
    return jnp.concatenate([_rms(o_mla, g_om), _rms(y_ssm, g_os)], axis=-1).astype(BF16)


def _f_memk(kvm, gk):
    ks = [_rms(kvm[:, h * XH:(h + 1) * XH], gk) for h in range(H)]
    return jnp.concatenate(ks, axis=-1).astype(BF16), kvm[:, H * XH:].astype(BF16)


def _f_disc(lr, li, log_dt, br, bi):
    dt = jnp.exp(log_dt)
    decay = jnp.exp(lr * dt)
    ar = decay * jnp.cos(li * dt)
    ai = decay * jnp.sin(li * dt)
    den = lr * lr + li * li
    nr = ar - 1.0
    coef_r = (nr * lr + ai * li) / den
    coef_i = (ai * lr - nr * li) / den
    return ar, ai, coef_r * br - coef_i * bi, coef_r * bi + coef_i * br


def _causal_mask(i, j, tq, tk):
    qpos = i * tq + lax.broadcasted_iota(jnp.int32, (tq, tk), 0)
    kpos = j * tk + lax.broadcasted_iota(jnp.int32, (tq, tk), 1)
    return qpos >= kpos


def _attn_fwd(q, k, v, *, t=512):
    s = q.shape[0]
    t = min(t, s)
    nb = s // t

    def body(q_ref, k_ref, v_ref, o_ref, lse_ref, m_sc, l_sc, acc_sc):
        i, j = pl.program_id(1), pl.program_id(2)

        @pl.when(j == 0)
        def _():
            m_sc[...] = jnp.full_like(m_sc, -jnp.inf)
            l_sc[...] = jnp.zeros_like(l_sc)
            acc_sc[...] = jnp.zeros_like(acc_sc)

        def block(diagonal):
            sc = lax.dot_general(q_ref[...], k_ref[...], (((1,), (1,)), ((), ())), preferred_element_type=F32)
            if diagonal:
                sc = jnp.where(_causal_mask(i, j, t, t), sc, -jnp.inf)
            m_old = m_sc[...]
            m_new = jnp.maximum(m_old, jnp.max(sc, axis=-1, keepdims=True))
            p = jnp.exp2(sc - m_new)
            alpha = jnp.exp2(m_old - m_new)
            l_sc[...] = alpha * l_sc[...] + jnp.sum(p, axis=-1, keepdims=True)
            acc_sc[...] = alpha * acc_sc[...] + jnp.dot(p.astype(BF16), v_ref[...], preferred_element_type=F32)
            m_sc[...] = m_new

        pl.when(j < i)(lambda: block(False))

        @pl.when(j == i)
        def _():
            block(True)
            o_ref[...] = acc_sc[...] / l_sc[...]
            lse_ref[...] = jnp.broadcast_to(m_sc[...] + jnp.log2(l_sc[...]), lse_ref.shape)

    kv_map = lambda h, i, j: (jnp.minimum(j, i), h)
    return pl.pallas_call(
        body, name="mla_attn_fwd", grid=(H, nb, nb),
        in_specs=[pl.BlockSpec((t, HQ), lambda h, i, j: (i, h)), pl.BlockSpec((t, HQ), kv_map),
                  pl.BlockSpec((t, VD), kv_map)],
        out_specs=[pl.BlockSpec((t, VD), lambda h, i, j: (i, h)), pl.BlockSpec((t, LANES), lambda h, i, j: (i, h))],
        out_shape=[jax.ShapeDtypeStruct((s, H * VD), F32), jax.ShapeDtypeStruct((s, H * LANES), F32)],
        scratch_shapes=[pltpu.VMEM((t, 1), F32), pltpu.VMEM((t, 1), F32), pltpu.VMEM((t, VD), F32)],
        compiler_params=_params(("parallel", "parallel", "arbitrary")),
    )(q, k, v)


def _attn_probs(q_ref, k_ref, v_ref, do_ref, lse_ref, dl_ref, i, j, t, diagonal):
    sc = lax.dot_general(q_ref[...], k_ref[...], (((1,), (1,)), ((), ())), preferred_element_type=F32)
    p = jnp.exp2(sc - jnp.tile(lse_ref[...], (1, t // LANES)))
    if diagonal:
        p = jnp.where(_causal_mask(i, j, t, t), p, 0.0)
    dp = lax.dot_general(do_ref[...], v_ref[...], (((1,), (1,)), ((), ())), preferred_element_type=F32)
    ds = p * (dp - jnp.tile(dl_ref[...], (1, t // LANES)))
    return p, ds


def _attn_bwd(q, k, v, do, lse, delta, *, t=512):
    s = q.shape[0]
    t = min(t, s)
    nb = s // t

    def body(q_ref, k_ref, v_ref, do_ref, lse_ref, dl_ref, dq_ref, dk_ref, dv_ref, dk_sc, dv_sc):
        j, i = pl.program_id(1), pl.program_id(2)

        @pl.when(jnp.logical_and(i == 0, j == 0))
        def _():
            dq_ref[...] = jnp.zeros_like(dq_ref)

        @pl.when(i == 0)
        def _():
            dk_sc[...] = jnp.zeros_like(dk_sc)
            dv_sc[...] = jnp.zeros_like(dv_sc)

        def block(diagonal):
            p, ds = _attn_probs(q_ref, k_ref, v_ref, do_ref, lse_ref, dl_ref, i, j, t, diagonal)
            dsb = ds.astype(BF16)
            dv_sc[...] += lax.dot_general(p.astype(BF16), do_ref[...], (((0,), (0,)), ((), ())), preferred_element_type=F32)
            dk_sc[...] += lax.dot_general(dsb, q_ref[...], (((0,), (0,)), ((), ())), preferred_element_type=F32)
            rows = pl.ds(pl.multiple_of(i * t, t), t)
            dq_ref[rows, :] += jnp.dot(dsb, k_ref[...], preferred_element_type=F32)

        pl.when(i > j)(lambda: block(False))
        pl.when(i == j)(lambda: block(True))

        @pl.when(i == nb - 1)
        def _():
            dk_ref[...] = dk_sc[...] * LN2
            dv_ref[...] = dv_sc[...]

        @pl.when(jnp.logical_and(i == nb - 1, j == nb - 1))
        def _():
            dq_ref[...] = dq_ref[...] * LN2

    q_map = lambda h, j, i: (jnp.maximum(i, j), h)
    kv_map = lambda h, j, i: (j, h)
    dq, dk, dv = pl.pallas_call(
        body, name="mla_attn_bwd", grid=(H, nb, nb),
        in_specs=[pl.BlockSpec((t, HQ), q_map), pl.BlockSpec((t, HQ), kv_map), pl.BlockSpec((t, VD), kv_map),
                  pl.BlockSpec((t, VD), q_map), pl.BlockSpec((t, LANES), q_map), pl.BlockSpec((t, LANES), q_map)],
        out_specs=[pl.BlockSpec((s, HQ), lambda h, j, i: (0, h)), pl.BlockSpec((t, HQ), kv_map), pl.BlockSpec((t, VD), kv_map)],
        out_shape=[jax.ShapeDtypeStruct((s, H * HQ), F32), jax.ShapeDtypeStruct((s, H * HQ), F32),
                   jax.ShapeDtypeStruct((s, H * VD), F32)],
        scratch_shapes=[pltpu.VMEM((t, HQ), F32), pltpu.VMEM((t, VD), F32)],
        compiler_params=_params(("parallel", "arbitrary", "arbitrary")),
    )(q, k, v, do, lse, delta)
    return dq, dk, dv


def _f_delta(do, o):
    prod = do.astype(F32) * o.astype(F32)
    parts = [jnp.broadcast_to(jnp.sum(prod[:, h * VD:(h + 1) * VD], axis=-1, keepdims=True), (do.shape[0], LANES))
             for h in range(H)]
    return jnp.concatenate(parts, axis=-1), do.astype(BF16)


def _xattn_head(qh, kh, gq):
    qn = _rms(qh, gq) * (XH ** -0.5)
    sc = lax.dot_general(qn.astype(BF16), kh, (((1,), (1,)), ((), ())), preferred_element_type=F32)
    sc = sc - jnp.max(sc, axis=-1, keepdims=True)
    e = jnp.exp(sc)
    return qn, e / jnp.sum(e, axis=-1, keepdims=True)


def _xattn_fwd(q, kn, v, gq, *, ts=512):
    def fn(qb, knb, vb, g):
        outs = []
        for h in range(H):
            sl = slice(h * XH, (h + 1) * XH)
            _, p = _xattn_head(qb[:, sl], knb[:, sl], g)
            outs.append(jnp.dot(p.astype(BF16), vb[:, sl], preferred_element_type=F32))
        return (jnp.concatenate(outs, axis=-1),)

    return _rowwise(fn, [q], [kn, v, gq], [(H * XH, BF16)], ts=ts, name="xattn_fwd")[0]


def _xattn_bwd(q, kn, v, gq, do, *, ts=512):
    def fn(qb, dob, knb, vb, g):
        dqs, dks, dvs = [], [], []
        dg = jnp.zeros((1, XH), F32)
        for h in range(H):
            sl = slice(h * XH, (h + 1) * XH)
            qh, kh, vh, doh = qb[:, sl], knb[:, sl], vb[:, sl], dob[:, sl].astype(BF16)
            qn, p = _xattn_head(qh, kh, g)
            dp = lax.dot_general(doh, vh, (((1,), (1,)), ((), ())), preferred_element_type=F32)
            dvs.append(lax.dot_general(p.astype(BF16), doh, (((0,), (0,)), ((), ())), preferred_element_type=F32))
            ds = (p * (dp - jnp.sum(dp * p, axis=-1, keepdims=True))).astype(BF16)
            dqn = jnp.dot(ds, kh, preferred_element_type=F32)
            dks.append(lax.dot_general(ds, qn.astype(BF16), (((0,), (0,)), ((), ())), preferred_element_type=F32))
            _, vjp_n = jax.vjp(lambda a, b: _rms(a, b) * (XH ** -0.5), qh, g)
            dqh, dgh = vjp_n(dqn)
            dqs.append(dqh)
            dg = dg + dgh
        return (jnp.concatenate(dqs, axis=-1), jnp.concatenate(dks, axis=-1), jnp.concatenate(dvs, axis=-1), dg)

    return _rowwise(fn, [q, do], [kn, v, gq], [(H * XH, BF16)], [kn.shape, v.shape, gq.shape], ts=ts, name="xattn_bwd")


def _cmul(ar, ai, xr, xi):
    return ar * xr - ai * xi, ar * xi + ai * xr


def _scan_in_place(xr_ref, xi_ref, ar, ai, *, reverse):
    s, cw = xr_ref.shape
    c = SCAN_CHUNKS
    tt = s // c
    a_r = jnp.broadcast_to(ar, (c, cw))
    a_i = jnp.broadcast_to(ai, (c, cw))
    zero = jnp.zeros((c, cw), F32)

    def row(step):
        t = (tt - 1 - step) if reverse else step
        return pl.ds(pl.multiple_of(t * c, c), c)

    def local(step, carry):
        sr, si, qr, qi = carry
        r = row(step)
        nr, ni = _cmul(a_r, a_i, sr, si)
        nr, ni = nr + xr_ref[r, :], ni + xi_ref[r, :]
        xr_ref[r, :] = nr
        xi_ref[r, :] = ni
        return (nr, ni) + _cmul(a_r, a_i, qr, qi)

    end_r, end_i, pr, pi = lax.fori_loop(0, tt, local, (zero, zero, jnp.ones((c, cw), F32), zero), unroll=SCAN_UNROLL)

    rows_id = lax.broadcasted_iota(jnp.int32, (c, cw), 0)
    car_r, car_i = zero, zero
    cur_r, cur_i = jnp.zeros((1, cw), F32), jnp.zeros((1, cw), F32)
    order = range(c - 1, -1, -1) if reverse else range(c)
    for kk in order:
        car_r = jnp.where(rows_id == kk, cur_r, car_r)
        car_i = jnp.where(rows_id == kk, cur_i, car_i)
        nr, ni = _cmul(pr[0:1], pi[0:1], cur_r, cur_i)
        cur_r = nr + end_r[kk:kk + 1]
        cur_i = ni + end_i[kk:kk + 1]

    def fix(step, carry):
        qr, qi = _cmul(a_r, a_i, *carry)
        r = row(step)
        dr, di = _cmul(qr, qi, car_r, car_i)
        xr_ref[r, :] += dr
        xi_ref[r, :] += di
        return qr, qi

    lax.fori_loop(0, tt, fix, (jnp.ones((c, cw), F32), zero), unroll=SCAN_UNROLL)


S5_ROWS = 512


def _s5_scan(v, w_r, w_i, ar, ai, *, reverse, tb, readout=None, name):
    s = v.shape[0]
    g = w_r.shape[0]
    nv, ns = SSM_PACK * SSM_GRP, SSM_PACK * SSM_P
    rows = min(S5_ROWS, s)
    dims = (((1,), (1 if tb else 0,)), ((), ()))
    n_w = 2 if readout is None else 4

    def body(v_ref, ar_ref, ai_ref, *refs):
        w = [r[...] for r in refs[:n_w]]
        xr_ref, xi_ref = refs[n_w:n_w + 2]
        for r0 in range(0, s, rows):
            vb = v_ref[r0:r0 + rows, :].astype(BF16)
            xr_ref[r0:r0 + rows, :] = lax.dot_general(vb, w[0], dims, preferred_element_type=F32)
            xi_ref[r0:r0 + rows, :] = lax.dot_general(vb, w[1], dims, preferred_element_type=F32)
        _scan_in_place(xr_ref, xi_ref, ar_ref[...], ai_ref[...], reverse=reverse)
        if readout is not None:
            y_ref = refs[n_w + 2]
            for r0 in range(0, s, rows):
                y_ref[r0:r0 + rows, :] = (
                    jnp.dot(xr_ref[r0:r0 + rows, :].astype(BF16), w[2], preferred_element_type=F32)
                    + jnp.dot(xi_ref[r0:r0 + rows, :].astype(BF16), w[3], preferred_element_type=F32))

    col = lambda j: (0, j)
    w_spec = lambda a: pl.BlockSpec((None,) + a.shape[1:], lambda j: (j, 0, 0))
    weights = [w_r, w_i] + (list(readout) if readout is not None else [])
    out_specs = [pl.BlockSpec((s, ns), col)] * 2 + ([pl.BlockSpec((s, nv), col)] if readout is not None else [])
    out_shape = [jax.ShapeDtypeStruct((s, g * ns), F32)] * 2 + (
        [jax.ShapeDtypeStruct((s, g * nv), F32)] if readout is not None else [])
    return pl.pallas_call(
        body, name=name, grid=(g,),
        in_specs=[pl.BlockSpec((s, nv), col), pl.BlockSpec((1, ns), col), pl.BlockSpec((1, ns), col)] + [w_spec(a) for a in weights],
        out_specs=out_specs, out_shape=out_shape, compiler_params=_params(("parallel",)),
    )(v, ar, ai, *weights)


def _s5_grads(lam_r, lam_i, xr, xi, u, dyc, du_d, b_r, b_i):
    s = u.shape[0]
    g = b_r.shape[0]
    nv, ns, c = SSM_PACK * SSM_GRP, SSM_PACK * SSM_P, SCAN_CHUNKS
    rows = min(S5_ROWS, s)
    slabs = rows // c
    last_slab = s // c - 1
    nt = (((1,), (1,)), ((), ()))
    tn = (((0,), (0,)), ((), ()))

    def body(lr_ref, li_ref, xr_ref, xi_ref, pr_ref, pi_ref, u_ref, dy_ref, dud_ref, br_ref, bi_ref,
             du_ref, dbr_ref, dbi_ref, dcr_ref, dci_ref, dar_ref, dai_ref):
        first = pl.program_id(1) == 0
        l_r, l_i, x_r, x_i = lr_ref[...], li_ref[...], xr_ref[...], xi_ref[...]
        lrb, lib = l_r.astype(BF16), l_i.astype(BF16)
        du_ref[...] = (dud_ref[...] + lax.dot_general(lrb, br_ref[...], nt, preferred_element_type=F32)
                       + lax.dot_general(lib, bi_ref[...], nt, preferred_element_type=F32))
        ub, dyb = u_ref[...].astype(BF16), dy_ref[...].astype(BF16)
        rows_id = lax.broadcasted_iota(jnp.int32, (c, ns), 0)

        def before(p_ref, x):
            p = p_ref[...]
            p = jnp.where(first, jnp.where(rows_id == 0, 0.0, pltpu.roll(p, 1, 0)), p)
            return jnp.concatenate([p, x[:rows - c]], axis=0)

        xp_r, xp_i = before(pr_ref, x_r), before(pi_ref, x_i)
        parts = (lax.dot_general(ub, lrb, tn, preferred_element_type=F32),
                 lax.dot_general(ub, lib, tn, preferred_element_type=F32),
                 lax.dot_general(x_r.astype(BF16), dyb, tn, preferred_element_type=F32),
                 lax.dot_general(x_i.astype(BF16), dyb, tn, preferred_element_type=F32),
                 jnp.sum(l_r * xp_r + l_i * xp_i, axis=0, keepdims=True),
                 jnp.sum(l_i * xp_r - l_r * xp_i, axis=0, keepdims=True))
        accs = (dbr_ref, dbi_ref, dcr_ref, dci_ref, dar_ref, dai_ref)

        @pl.when(first)
        def _():
            for a_ref, val in zip(accs, parts):
                a_ref[...] = val

        @pl.when(jnp.logical_not(first))
        def _():
            for a_ref, val in zip(accs, parts):
                a_ref[...] += val

    state = pl.BlockSpec((rows, ns), lambda j, k: (k, j))
    chan = pl.BlockSpec((rows, nv), lambda j, k: (k, j))
    slab = pl.BlockSpec((c, ns), lambda j, k: (jnp.where(k == 0, last_slab, k * slabs - 1), j))
    per_b = pl.BlockSpec((None, nv, ns), lambda j, k: (j, 0, 0))
    per_c = pl.BlockSpec((None, ns, nv), lambda j, k: (j, 0, 0))
    per_a = pl.BlockSpec((1, ns), lambda j, k: (0, j))
    return pl.pallas_call(
        body, name="s5_grads", grid=(g, s // rows),
        in_specs=[state, state, state, state, slab, slab, chan, chan, chan, per_b, per_b],
        out_specs=[chan, per_b, per_b, per_c, per_c, per_a, per_a],
        out_shape=[jax.ShapeDtypeStruct((s, g * nv), F32), jax.ShapeDtypeStruct((g, nv, ns), F32),
                   jax.ShapeDtypeStruct((g, nv, ns), F32), jax.ShapeDtypeStruct((g, ns, nv), F32),
                   jax.ShapeDtypeStruct((g, ns, nv), F32), jax.ShapeDtypeStruct((1, g * ns), F32),
                   jax.ShapeDtypeStruct((1, g * ns), F32)],
        compiler_params=_params(("parallel", "arbitrary")),
    )(lam_r, lam_i, xr, xi, xr, xi, u, dyc, du_d, b_r, b_i)


def _mesh_place():
    x, y, c = lax.axis_index("x"), lax.axis_index("y"), lax.axis_index("c")
    peers = []
    for k in range(1, N_DEV):
        px, py, pc = x ^ ((k >> 2) & 1), y ^ ((k >> 1) & 1), c ^ (k & 1)
        peers.append(((px, py, pc), 4 * px + 2 * py + pc))
    return 4 * x + 2 * y + c, peers


class _Exchange:
    SAME_CORE_MASKS = (2, 4, 6)

    def __init__(self, arrays, rows, *, gather, name, after=None, two_level=False):
        self.n_arr, self.rows, self.gather, self.name = len(arrays), rows, gather, name
        self.two_level, self.in_flight = two_level, (1 + len(self.SAME_CORE_MASKS) if two_level else N_DEV - 1)
        n_arr = self.n_arr
        if gather:
            assert all(r % BF16_ROWS == 0 for r in rows)
            lands = [lax.empty((N_DEV * r, a.shape[1]), a.dtype) for a, r in zip(arrays, rows)]
        else:
            lands = [lax.empty((N_DEV - 1,) + (tuple(a.shape) if st is None else (n, a.shape[1])), a.dtype)
                     for a, (st, n) in zip(arrays, rows)]
        has_after = after is not None

        def body(*refs):
            ins, zones = refs[:n_arr], refs[n_arr:2 * n_arr]
            sems = refs[2 * n_arr + has_after:4 * n_arr + has_after]
            token = refs[-1]
            me, peers = _mesh_place()
            for i in range(n_arr):
                for k, (pxyz, pid) in enumerate(peers):
                    if two_level and k + 1 not in (1,) + self.SAME_CORE_MASKS:
                        continue
                    if gather:
                        src = ins[i]
                        dst = zones[i].at[pl.ds(pl.multiple_of(me * rows[i], BF16_ROWS), rows[i])]
                    else:
                        stride, n = rows[i]
                        src = ins[i] if stride is None else ins[i].at[pl.ds(pl.multiple_of(pid * stride, BF16_ROWS), n)]
                        dst = zones[i].at[k]
                    pltpu.make_async_remote_copy(
                        src_ref=src, dst_ref=dst, send_sem=sems[2 * i], recv_sem=sems[2 * i + 1],
                        device_id=pxyz, device_id_type=pl.DeviceIdType.MESH).start()
            token[...] = jnp.zeros_like(token)

        hbm = pl.BlockSpec(memory_space=pltpu.HBM)
        sem = pl.BlockSpec(memory_space=pltpu.SEMAPHORE)
        args = [pltpu.with_memory_space_constraint(a, pltpu.HBM) for a in list(arrays) + lands]
        res = pl.pallas_call(
            body, name=name + "_start",
            in_specs=[hbm] * (2 * n_arr) + ([pl.BlockSpec(memory_space=pl.ANY)] if has_after else []),
            out_specs=[sem] * (2 * n_arr) + [hbm] * (2 * n_arr) + [pl.BlockSpec(memory_space=pltpu.VMEM)],
            out_shape=[pltpu.SemaphoreType.DMA(())] * (2 * n_arr) + [pltpu.HBM(a.shape, a.dtype) for a in args]
            + [jax.ShapeDtypeStruct((8, LANES), F32)],
            input_output_aliases={i: 2 * n_arr + i for i in range(2 * n_arr)},
            compiler_params=pltpu.CompilerParams(has_side_effects=pltpu.SideEffectType.DATAFLOW_SIDE_EFFECTING),
        )(*args, *([after] if has_after else []))
        self.sems, self.thru, self.token = res[:2 * n_arr], res[2 * n_arr:4 * n_arr], res[-1]

    def _wait_all(self, zones, sems):
        myself = (lax.axis_index("x"), lax.axis_index("y"), lax.axis_index("c"))
        for i in range(self.n_arr):
            many = zones[i].at[pl.ds(0, self.in_flight * self.rows[i])] if self.gather else zones[i]
            all_of_them = pltpu.make_async_remote_copy(
                src_ref=many, dst_ref=many, send_sem=sems[2 * i], recv_sem=sems[2 * i + 1],
                device_id=myself, device_id_type=pl.DeviceIdType.MESH)
            all_of_them.wait_recv()
            all_of_them.wait_send()

    def forward(self, after):
        n_arr = self.n_arr

        def body(*refs):
            zones, sems = refs[n_arr:2 * n_arr], refs[2 * n_arr:4 * n_arr]
            new_sems = refs[4 * n_arr + 1:6 * n_arr + 1]
            self._wait_all(zones, sems)
            _, peers = _mesh_place()
            sibling, _ = peers[0]
            for i in range(n_arr):
                for mask in self.SAME_CORE_MASKS:
                    _, pid = peers[mask - 1]
                    block = zones[i].at[pl.ds(pl.multiple_of(pid * self.rows[i], BF16_ROWS), self.rows[i])]
                    pltpu.make_async_remote_copy(
                        src_ref=block, dst_ref=block, send_sem=new_sems[2 * i], recv_sem=new_sems[2 * i + 1],
                        device_id=sibling, device_id_type=pl.DeviceIdType.MESH).start()

        hbm = pl.BlockSpec(memory_space=pltpu.HBM)
        sem = pl.BlockSpec(memory_space=pltpu.SEMAPHORE)
        res = pl.pallas_call(
            body, name=self.name + "_forward",
            in_specs=[hbm] * (2 * n_arr) + [sem] * (2 * n_arr) + [pl.BlockSpec(memory_space=pl.ANY)],
            out_specs=[sem] * (2 * n_arr) + [hbm] * (2 * n_arr),
            out_shape=[pltpu.SemaphoreType.DMA(())] * (2 * n_arr) + [pltpu.HBM(a.shape, a.dtype) for a in self.thru],
            input_output_aliases={i: 2 * n_arr + i for i in range(2 * n_arr)},
            compiler_params=pltpu.CompilerParams(has_side_effects=pltpu.SideEffectType.DATAFLOW_SIDE_EFFECTING),
        )(*self.thru, *self.sems, after)
        self.sems, self.thru = res[:2 * n_arr], res[2 * n_arr:]
        self.two_level, self.in_flight = False, len(self.SAME_CORE_MASKS)

    def wait(self, after):
        n_arr = self.n_arr
        if self.two_level:
            self.forward(after)

        def body(*refs):
            self._wait_all(refs[n_arr:2 * n_arr], refs[2 * n_arr:4 * n_arr])

        hbm = pl.BlockSpec(memory_space=pltpu.HBM)
        sem = pl.BlockSpec(memory_space=pltpu.SEMAPHORE)
        res = pl.pallas_call(
            body, name=self.name + "_wait",
            in_specs=[hbm] * (2 * n_arr) + [sem] * (2 * n_arr) + [pl.BlockSpec(memory_space=pl.ANY)],
            out_specs=[hbm] * (2 * n_arr), out_shape=[pltpu.HBM(a.shape, a.dtype) for a in self.thru],
            input_output_aliases={i: i for i in range(2 * n_arr)},
            compiler_params=pltpu.CompilerParams(has_side_effects=pltpu.SideEffectType.DATAFLOW_SIDE_EFFECTING),
        )(*self.thru, *self.sems, after)
        return res[:n_arr], res[n_arr:]


def _my_slot():
    me = 4 * lax.axis_index("x") + 2 * lax.axis_index("y") + lax.axis_index("c")
    return me.astype(jnp.int32).reshape(1)


def _place_own(gathered, block, me, *, name):
    r, c = block.shape

    def body(me_ref, b_ref, g_ref, o_ref):
        o_ref[...] = b_ref[...]

    return pl.pallas_call(
        body, name=name, out_shape=jax.ShapeDtypeStruct(gathered.shape, gathered.dtype),
        grid_spec=pltpu.PrefetchScalarGridSpec(
            num_scalar_prefetch=1, grid=(1,),
            in_specs=[pl.BlockSpec((r, c), lambda i, me_ref: (0, 0)), pl.BlockSpec(memory_space=pl.ANY)],
            out_specs=pl.BlockSpec((r, c), lambda i, me_ref: (me_ref[0], 0))),
        input_output_aliases={2: 0}, compiler_params=_params(("arbitrary",)),
    )(me, block, gathered)


def _elementwise_tiles(r, c):
    if r % 128 == 0:
        return 128, c
    return r, (256 if c % 256 == 0 else c)


def _adamw_math(g, w, m, v):
    nm = ADAM_B1 * m + (1.0 - ADAM_B1) * g
    nv = ADAM_B2 * v + (1.0 - ADAM_B2) * (g * g)
    m_hat = nm / (1.0 - ADAM_B1 ** ADAM_STEP)
    v_hat = nv / (1.0 - ADAM_B2 ** ADAM_STEP)
    return -ADAM_LR * (m_hat / (jnp.sqrt(v_hat) + ADAM_EPS) + ADAM_WD * w), nm, nv


def _sum_parts(me_ref, own_ref, p_ref, r):
    own = own_ref[...].astype(F32)
    g = None
    for d in range(N_DEV):
        k = jnp.bitwise_xor(me_ref[0], d)
        term = jnp.where(k == 0, own, p_ref[jnp.maximum(k, 1) - 1].astype(F32))
        g = term if g is None else g + term
    return g[0:r, :]


def _sum_adamw(me, sent, stride, parts, r, w=None, m=None, v=None, *, name):
    _, own_rows, cdim = parts.shape
    assert stride is None or stride == own_rows
    tc = 256 if cdim % 256 == 0 else cdim
    update = w is not None

    def body(me_ref, own_ref, p_ref, *refs):
        g = _sum_parts(me_ref, own_ref, p_ref, r)
        if update:
            w_ref, m_ref, v_ref, g_ref, d_ref, nm_ref, nv_ref = refs
            d_ref[...], nm_ref[...], nv_ref[...] = _adamw_math(g, w_ref[...], m_ref[...], v_ref[...])
        else:
            g_ref, = refs
        g_ref[...] = g

    blk = pl.BlockSpec((r, tc), lambda j, me_ref: (0, j))
    own_spec = pl.BlockSpec((own_rows, tc), (lambda j, me_ref: (0, j)) if stride is None else (lambda j, me_ref: (me_ref[0], j)))
    n_out = 4 if update else 1
    res = pl.pallas_call(
        body, name=name, out_shape=[jax.ShapeDtypeStruct((r, cdim), F32)] * n_out,
        grid_spec=pltpu.PrefetchScalarGridSpec(
            num_scalar_prefetch=1, grid=(cdim // tc,),
            in_specs=[own_spec, pl.BlockSpec((N_DEV - 1, own_rows, tc), lambda j, me_ref: (0, 0, j))]
            + ([blk] * 3 if update else []),
            out_specs=[blk] * n_out),
        compiler_params=_params(("parallel",)),
    )(me, sent, parts, *((w, m, v) if update else ()))
    return list(res)


def _adamw(g, w, m, v, *, name):
    r, cdim = w.shape
    tr, tc = _elementwise_tiles(r, cdim)

    def body(g_ref, w_ref, m_ref, v_ref, d_ref, nm_ref, nv_ref):
        d_ref[...], nm_ref[...], nv_ref[...] = _adamw_math(g_ref[...], w_ref[...], m_ref[...], v_ref[...])

    blk = pl.BlockSpec((tr, tc), lambda i, j: (i, j))
    return list(pl.pallas_call(
        body, name=name, grid=(r // tr, cdim // tc), in_specs=[blk] * 4,
        out_specs=[blk] * 3, out_shape=[jax.ShapeDtypeStruct((r, cdim), F32)] * 3,
        compiler_params=_params(("parallel", "parallel")),
    )(g, w, m, v))


SHARD_ROWS_P = {n: (FF_SHARD_P if 'ffn' in n else IN_SHARD_P if n == 'w_in' else None) for n in SHARDED}


def _to_exchange_layout(name, shard):
    t = shard.T if SHARD_AXIS[name] == 1 else shard
    pad = SHARD_ROWS_P[name]
    return t if pad is None else jnp.pad(t, ((0, pad - t.shape[0]), (0, 0)))


def _expand_w_in(wt):
    wt = wt.reshape(N_DEV, IN_SHARD_P, D)[:, :IN_SHARD].reshape(IN_W, D)
    o = Q_RANK + KV_RANK
    kr1, kr2 = wt[o:o + ROPE // 2], wt[o + ROPE // 2:o + ROPE]
    z = jnp.zeros((LANES - ROPE, D), wt.dtype)
    return jnp.concatenate([wt[:o], wt[o + ROPE:], kr1, kr2, z, -kr2, kr1, z], axis=0)


def _expand_w_uq(wt):
    w = wt.reshape(H, QK, Q_RANK)
    z = jnp.zeros((H, LANES - ROPE, Q_RANK), w.dtype)
    q1, q2 = w[:, NOPE:NOPE + ROPE // 2], w[:, NOPE + ROPE // 2:]
    return jnp.concatenate([w[:, :NOPE].reshape(H * NOPE, Q_RANK),
                            jnp.concatenate([q1, q2, z], axis=1).reshape(H * LANES, Q_RANK),
                            jnp.concatenate([-q2, q1, z], axis=1).reshape(H * LANES, Q_RANK)], axis=0)


def _layout_qk_gain(g):
    g = g.reshape(QK)
    g1, g2, z = g[NOPE:NOPE + ROPE // 2], g[NOPE + ROPE // 2:], jnp.zeros((LANES - ROPE,), g.dtype)
    return jnp.stack([g[:NOPE], jnp.concatenate([g1, g2, z]), jnp.concatenate([g2, g1, z])])


def _rep16(a):
    return jnp.repeat(a, SSM_GRP, axis=0)


def _layout_ssm_in(a_re, a_im, log_dt, b_re, b_im):
    b_r = jnp.transpose(b_re, (0, 2, 1)).reshape(SSM_G * SSM_GRP, SSM_P)
    b_i = jnp.transpose(b_im, (0, 2, 1)).reshape(SSM_G * SSM_GRP, SSM_P)
    ldt = jnp.broadcast_to(log_dt.reshape(SSM_G, 1), (SSM_G, SSM_P))
    return _rep16(a_re), _rep16(a_im), _rep16(ldt), b_r, b_i


def _block_diag_b(bb):
    eye = jnp.eye(SSM_PACK, dtype=bb.dtype)
    b5 = bb.reshape(SSM_G // SSM_PACK, SSM_PACK, SSM_GRP, 1, SSM_P) * eye[None, :, None, :, None]
    return b5.reshape(SSM_G // SSM_PACK, SSM_PACK * SSM_GRP, SSM_PACK * SSM_P)


def _block_diag_c(cc):
    eye = jnp.eye(SSM_PACK, dtype=cc.dtype)
    c5 = jnp.transpose(cc, (0, 2, 1)).reshape(SSM_G // SSM_PACK, SSM_PACK, SSM_P, 1, SSM_GRP) * eye[None, :, None, :, None]
    return c5.reshape(SSM_G // SSM_PACK, SSM_PACK * SSM_P, SSM_PACK * SSM_GRP)


def _time_perm(a, inverse=False):
    s, w = a.shape
    c = SCAN_CHUNKS
    if inverse:
        return jnp.transpose(a.reshape(s // c, c, w), (1, 0, 2)).reshape(s, w)
    return jnp.transpose(a.reshape(c, s // c, w), (1, 0, 2)).reshape(s, w)


class _Weights:
    def __init__(self, groups=(), landed=None, me=None):
        self.groups, self.landed, self.me = list(groups), dict(landed or {}), me

    def get(self, name, after):
        if name not in self.landed:
            names, exchange = next(g for g in self.groups if name in g[0])
            for n, block, gathered in zip(names, *exchange.wait(after)):
                self.landed[n] = _place_own(gathered, block, self.me, name="place_" + n)
        return self.landed[name]

    def __getitem__(self, name):
        return self.landed[name]

    def prefetch(self, name, after):
        for names, exchange in self.groups:
            if name in names and exchange.two_level:
                exchange.forward(after)


def _ffn_gate_up(h, w_gt, w_ut, *, name, tm=512, tn=1408):
    s, k = h.shape
    n = w_gt.shape[0]
    tm, tn = min(tm, s), _tile(n, tn)
    dims = (((1,), (1,)), ((), ()))

    def body(h_ref, wg_ref, wu_ref, g_ref, u_ref, a_ref):
        hb = h_ref[...].astype(BF16)
        gate = lax.dot_general(hb, wg_ref[...], dims, preferred_element_type=F32)
        up = lax.dot_general(hb, wu_ref[...], dims, preferred_element_type=F32)
        g_ref[...] = gate.astype(BF16)
        u_ref[...] = up.astype(BF16)
        a_ref[...] = _f_swiglu(gate, up)

    w_spec = pl.BlockSpec((tn, k), lambda j, i: (j, 0))
    o_spec = pl.BlockSpec((tm, tn), lambda j, i: (i, j))
    return pl.pallas_call(
        body, name=name, grid=(n // tn, s // tm), in_specs=[pl.BlockSpec((tm, k), lambda j, i: (i, 0)), w_spec, w_spec],
        out_specs=[o_spec] * 3, out_shape=[jax.ShapeDtypeStruct((s, n), BF16)] * 3,
        compiler_params=_params(("parallel", "parallel")),
    )(h, w_gt, w_ut)


def _ffn_dgate_dup(dx_out, w_d, gate, up, *, name, tm=512, tn=1408, deps=()):
    s, k = dx_out.shape
    n = w_d.shape[0]
    tm, tn = min(tm, s), _tile(n, tn)
    deps = [d for d in deps if d is not None]

    def body(dx_ref, wd_ref, g_ref, u_ref, *refs):
        dg_ref, du_ref = refs[len(deps):]
        dact = 0.5 * lax.dot_general(dx_ref[...].astype(BF16), wd_ref[...], (((1,), (1,)), ((), ())),
                                     preferred_element_type=F32)
        _, vjp = jax.vjp(_f_swiglu, g_ref[...].astype(F32), u_ref[...].astype(F32))
        dgate, dup = vjp(dact.astype(BF16))
        dg_ref[...] = dgate.astype(BF16)
        du_ref[...] = dup.astype(BF16)

    o_spec = pl.BlockSpec((tm, tn), lambda j, i: (i, j))
    return pl.pallas_call(
        body, name=name, grid=(n // tn, s // tm),
        in_specs=[pl.BlockSpec((tm, k), lambda j, i: (i, 0)), pl.BlockSpec((tn, k), lambda j, i: (j, 0)), o_spec, o_spec]
        + [pl.BlockSpec(d.shape, lambda j, i: (0, 0)) for d in deps],
        out_specs=[o_spec] * 2, out_shape=[jax.ShapeDtypeStruct((s, n), BF16)] * 2,
        compiler_params=_params(("parallel", "parallel")),
    )(dx_out, w_d, gate, up, *deps)


def _ffn_dh(dgate, dup, w_gt, w_ut, *, name, tm=512):
    s, k = dgate.shape
    n = w_gt.shape[1]
    tm = min(tm, s)

    def body(dg_ref, du_ref, wg_ref, wu_ref, o_ref):
        o_ref[...] = (jnp.dot(dg_ref[...], wg_ref[...], preferred_element_type=F32)
                      + jnp.dot(du_ref[...], wu_ref[...], preferred_element_type=F32)).astype(o_ref.dtype)

    a_spec = pl.BlockSpec((tm, k), lambda i: (i, 0))
    w_spec = pl.BlockSpec((k, n), lambda i: (0, 0))
    return pl.pallas_call(
        body, name=name, grid=(s // tm,), in_specs=[a_spec, a_spec, w_spec, w_spec],
        out_specs=pl.BlockSpec((tm, n), lambda i: (i, 0)), out_shape=jax.ShapeDtypeStruct((s, n), BF16),
        compiler_params=_params(("parallel",)),
    )(dgate, dup, w_gt, w_ut)


def _ffn_fwd(x, g, wc, tag, deps=(), prefetch=()):
    h = _rowwise(_f_norm, [x], [g], [(D, BF16)], name=tag + "_norm", deps=deps)[0]
    gate, up, act = _ffn_gate_up(h, wc.get(tag + '_w_gate', h), wc[tag + '_w_up'], name=tag + "_gate_up")
    for later in (tag + '_w_down',) + tuple(prefetch):
        wc.prefetch(later, gate)
    x_out = _mm(act, wc.get(tag + '_w_down', act), res=x, scale=0.5, name=tag + "_down")
    return x_out, (h, gate, up, act)


def _ffn_bwd(x, g, wc, saved, dx_out, tag, send, deps=()):
    h, gate, up, act = saved
    w_gt, w_ut, w_d = (wc.get(tag + n, h) for n in ('_w_gate', '_w_up', '_w_down'))
    d_d = _mm(act, dx_out, ta=True, scale=0.5, out_dtype=GRAD_DTYPE, name=tag + "_dwdown", deps=deps)
    token = send({tag + '_w_down': d_d})
    dgate, dup = _ffn_dgate_dup(dx_out, w_d, gate, up, name=tag + "_dgate_dup", deps=[token])
    d_gt = _mm(dgate, h, ta=True, out_dtype=GRAD_DTYPE, name=tag + "_dwgate")
    token = send({tag + '_w_gate': d_gt})
    d_ut = _mm(dup, h, ta=True, out_dtype=GRAD_DTYPE, name=tag + "_dwup", deps=[token])
    token = send({tag + '_w_up': d_ut})
    dh = _ffn_dh(dgate, dup, w_gt, w_ut, name=tag + "_dh")
    dx, dg = _rowwise_bwd(_f_norm, [x], [g], [dh], row_grads={0: F32}, const_grads=[0], adds={0: dx_out},
                          name=tag + "_norm_bwd", deps=[token])
    return dx, dg


def _local_step(x, mem, cos, sin, target, wc, ws, send, deps=(), send_small=None):
    gs = {}

    x1, sv1 = _ffn_fwd(x, ws['ffn1_norm'], wc, "ffn1", deps=deps, prefetch=('w_in',))

    h2 = _rowwise(_f_norm, [x1], [ws['mix_norm']], [(D, BF16)], name="mix_norm")[0]
    w_in_raw, w_uq_raw = wc.get('w_in', h2), wc.get('mla_w_uq', h2)
    w_in_e = _expand_w_in(w_in_raw)
    w_uq_e = _expand_w_uq(w_uq_raw)
    proj = _mm(h2, w_in_e, tb=True, name="w_in")
    c_q, c_kv = _rowwise(_f_prep1, [proj], [ws['q_norm'], ws['kv_norm']], [(Q_RANK, BF16), (KV_RANK, BF16)], name="mla_prep1")
    qall = _mm(c_q, w_uq_e, tb=True, name="w_uq")
    kv = _mm(c_kv, wc['mla_w_ukv'], tb=True, name="w_ukv")
    kr = _rowwise(_f_kr, [proj], [], [(2 * LANES, F32)], name="mla_kr")[0]
    q, k, v = _prep2_fwd(qall, kv, kr, cos, sin, ws['qk_gq'], ws['qk_gk'])
    o_mla, lse = _attn_fwd(q, k, v)
    wc.prefetch('ffn2_w_gate', lse)

    u = proj[:, Q_RANK + KV_RANK:Q_RANK + KV_RANK + SSM_W]
    u_p = _time_perm(u)
    disc_in = [ws['ssm_lr'], ws['ssm_li'], ws['ssm_ldt'], ws['ssm_br'], ws['ssm_bi']]
    ar16, ai16, bbr, bbi = _rowwise(_f_disc, disc_in, [], [(SSM_P, F32)] * 4, name="s5_disc")
    a_r = ar16[::SSM_GRP].reshape(1, SSM_N)
    a_i = ai16[::SSM_GRP].reshape(1, SSM_N)
    bblk_r, bblk_i = _block_diag_b(bbr).astype(BF16), _block_diag_b(bbi).astype(BF16)
    cblk_r, cblk_i = _block_diag_c(ws['ssm_cr']).astype(BF16), _block_diag_c(-ws['ssm_ci']).astype(BF16)
    xr, xi, yc = _s5_scan(u_p, bblk_r, bblk_i, a_r, a_i, reverse=False, tb=False, readout=(cblk_r, cblk_i),
                          name="s5_scan_fwd")
    g_p = _rowwise(_f_s5_gelu, [yc, u_p], [ws['ssm_d']], [(SSM_W, F32)], name="s5_gelu")[0]
    z_p = _mm(g_p, wc['ssm_w_glu'], name="s5_glu")
    g_t, z_t = _time_perm(g_p, inverse=True), _time_perm(z_p, inverse=True)
    on_consts = [ws['ssm_b_glu'], ws['out_norm_mla'], ws['out_norm_ssm']]
    ycat = _rowwise(_f_outnorm, [o_mla, g_t, z_t], on_consts, [(D, BF16)], name="out_norm")[0]
    x2 = _mm(ycat, wc['w_o'], res=x1, name="w_o")

    hx = _rowwise(_f_norm, [x2], [ws['xattn_norm']], [(D, BF16)], name="xattn_norm")[0]
    xq = _mm(hx, wc['xattn_w_q'], name="xattn_q")
    mn = _rowwise(_f_norm, [mem], [ws['mem_norm']], [(D, BF16)], name="mem_norm")[0]
    kvm = _mm(mn, wc['xattn_w_kv'], name="xattn_kv")
    xkn, xv = _rowwise(_f_memk, [kvm], [ws['xattn_k_norm']], [(H * XH, BF16), (H * XH, BF16)], name="xattn_knorm")
    xo = _xattn_fwd(xq, xkn, xv, ws['xattn_q_norm'])
    x3 = _mm(xo, wc['xattn_w_o'], tb=True, res=x2, name="xattn_o")

    x4, sv2 = _ffn_fwd(x3, ws['ffn2_norm'], wc, "ffn2")

    def f_loss(yb, tb):
        err = yb - tb
        return err * (1.0 / D), jnp.broadcast_to(jnp.sum(jnp.sum(err * err, axis=1, keepdims=True), axis=0, keepdims=True) * (0.5 / D), (1, LANES))

    dx4, loss = _rowwise(f_loss, [x4, target], [], [(D, F32)], [(1, LANES)], name="loss")

    dx3, gs['ffn2_norm'] = _ffn_bwd(x3, ws['ffn2_norm'], wc, sv2, dx4, "ffn2", send)

    dxo = _mm(dx3, wc['xattn_w_o'], out_dtype=BF16, name="xattn_o_dx")
    send({'xattn_w_o': _mm(dx3, xo, ta=True, out_dtype=GRAD_DTYPE, name="xattn_o_dw")})
    dxq, dxkn, dxv, gs['xattn_q_norm'] = _xattn_bwd(xq, xkn, xv, ws['xattn_q_norm'], dxo)
    dkvm, gs['xattn_k_norm'] = _rowwise_bwd(_f_memk, [kvm], [ws['xattn_k_norm']], [dxkn, dxv], row_grads={0: BF16},
                                            const_grads=[0], name="xattn_knorm_bwd")
    send({'xattn_w_kv': _mm(mn, dkvm, ta=True, out_dtype=GRAD_DTYPE, name="xattn_kv_dw")})
    dmn = _mm(dkvm, wc['xattn_w_kv'], tb=True, out_dtype=BF16, name="xattn_kv_dx")
    gs['mem_norm'] = _rowwise_bwd(_f_norm, [mem], [ws['mem_norm']], [dmn], row_grads={}, const_grads=[0], name="mem_norm_bwd")[0]
    token = send({'xattn_w_q': _mm(hx, dxq, ta=True, out_dtype=GRAD_DTYPE, name="xattn_q_dw")})
    dhx = _mm(dxq, wc['xattn_w_q'], tb=True, out_dtype=BF16, name="xattn_q_dx")
    dx2, gs['xattn_norm'] = _rowwise_bwd(_f_norm, [x2], [ws['xattn_norm']], [dhx], row_grads={0: F32}, const_grads=[0],
                                         adds={0: dx3}, name="xattn_norm_bwd", deps=[token])

    dycat = _mm(dx2, wc['w_o'], tb=True, out_dtype=BF16, name="w_o_dx")
    send({'w_o': _mm(ycat, dx2, ta=True, out_dtype=GRAD_DTYPE, name="w_o_dw")})
    do_mla, dg_t, dz_t, gs['ssm_b_glu'], gs['out_norm_mla'], gs['out_norm_ssm'] = _rowwise_bwd(
        _f_outnorm, [o_mla, g_t, z_t], on_consts, [dycat], row_grads={0: F32, 1: F32, 2: BF16}, const_grads=[0, 1, 2],
        name="out_norm_bwd")

    dz_p, dg_p = _time_perm(dz_t), _time_perm(dg_t)
    send({'ssm_w_glu': _mm(g_p, dz_p, ta=True, out_dtype=GRAD_DTYPE, name="s5_glu_dw")})
    dg_p = _mm(dz_p, wc['ssm_w_glu'], tb=True, res=dg_p, name="s5_glu_dx")
    dyc, du_d, gs['ssm_d'] = _rowwise_bwd(_f_s5_gelu, [yc, u_p], [ws['ssm_d']], [dg_p], row_grads={0: BF16, 1: F32},
                                          const_grads=[0], name="s5_gelu_bwd")
    lam_r, lam_i = _s5_scan(dyc, cblk_r, cblk_i, a_r, -a_i, reverse=True, tb=True, name="s5_scan_bwd")
    du_p, d_bblk_r, d_bblk_i, d_cblk_r, d_cblk_i, d_ar, d_ai = _s5_grads(lam_r, lam_i, xr, xi, u_p, dyc, du_d,
                                                                        bblk_r, bblk_i)
    du = _time_perm(du_p, inverse=True)
    gs['ssm_cr'] = jax.linear_transpose(_block_diag_c, ws['ssm_cr'])(d_cblk_r)[0]
    gs['ssm_ci'] = -jax.linear_transpose(_block_diag_c, ws['ssm_ci'])(d_cblk_i)[0]
    d_bbr = jax.linear_transpose(_block_diag_b, bbr)(d_bblk_r)[0]
    d_bbi = jax.linear_transpose(_block_diag_b, bbi)(d_bblk_i)[0]
    d_ar16 = jnp.zeros((SSM_G * SSM_GRP, SSM_P), F32).at[::SSM_GRP].set(d_ar.reshape(SSM_G, SSM_P))
    d_ai16 = jnp.zeros((SSM_G * SSM_GRP, SSM_P), F32).at[::SSM_GRP].set(d_ai.reshape(SSM_G, SSM_P))
    gs['ssm_lr'], gs['ssm_li'], gs['ssm_ldt'], gs['ssm_br'], gs['ssm_bi'] = _rowwise_bwd(
        _f_disc, disc_in, [], [d_ar16, d_ai16, d_bbr, d_bbi], row_grads={i: F32 for i in range(5)}, const_grads=[],
        name="s5_disc_bwd")

    delta, do_b = _rowwise(_f_delta, [do_mla, o_mla], [], [(H * LANES, F32), (H * VD, BF16)], name="mla_delta")
    dq, dk, dv = _attn_bwd(q, k, v, do_b, lse, delta)
    dqall, dkv, dkr, gs['qk_gq'], gs['qk_gk'] = _prep2_bwd(qall, kv, kr, cos, sin, ws['qk_gq'], ws['qk_gk'], dq, dk, dv)
    d_w_uq_e = _mm(dqall, c_q, ta=True, name="w_uq_dw")
    send({'mla_w_uq': jax.linear_transpose(_expand_w_uq, jax.ShapeDtypeStruct(w_uq_raw.shape, F32))(d_w_uq_e)[0]})
    dc_q = _mm(dqall, w_uq_e, out_dtype=BF16, name="w_uq_dx")
    send({'mla_w_ukv': _mm(dkv, c_kv, ta=True, out_dtype=GRAD_DTYPE, name="w_ukv_dw")})
    dc_kv = _mm(dkv, wc['mla_w_ukv'], out_dtype=BF16, name="w_ukv_dx")

    def f_prep1_bwd(pb, dcq, dckv, dub, dkrb, gq, gkv):
        _, vjp = jax.vjp(_f_prep1, pb[:, :Q_RANK + KV_RANK], gq, gkv)
        dpa, dgq, dgkv = vjp((dcq.astype(BF16), dckv.astype(BF16)))
        return jnp.concatenate([dpa, dub, dkrb], axis=-1), dgq, dgkv

    dproj, gs['q_norm'], gs['kv_norm'] = _rowwise(
        f_prep1_bwd, [proj, dc_q, dc_kv, du, dkr], [ws['q_norm'], ws['kv_norm']], [(IN_WP, BF16)],
        [(1, Q_RANK), (1, KV_RANK)], name="mla_prep1_bwd")
    d_w_in_e = _mm(dproj, h2, ta=True, name="w_in_dw")
    token = send({'w_in': jax.linear_transpose(_expand_w_in, jax.ShapeDtypeStruct(w_in_raw.shape, F32))(d_w_in_e)[0]})
    dh2 = _mm(dproj, w_in_e, out_dtype=BF16, name="w_in_dx")
    dx1, gs['mix_norm'] = _rowwise_bwd(_f_norm, [x1], [ws['mix_norm']], [dh2], row_grads={0: F32}, const_grads=[0],
                                       adds={0: dx2}, name="mix_norm_bwd", deps=[token])

    token = send_small(gs, loss) if send_small is not None else None
    dx0, gs['ffn1_norm'] = _ffn_bwd(x, ws['ffn1_norm'], wc, sv1, dx1, "ffn1", send, deps=[token])
    return loss, dx0, gs


def _prep2_fwd(qall, kv, kr, cos, sin, gq, gk):
    return _rowwise(_f_prep2, [qall, kv, kr, cos, sin], [gq, gk], [(H * HQ, BF16), (H * HQ, BF16), (H * VD, BF16)],
                    ts=256, name="mla_prep2")


def _prep2_bwd(qall, kv, kr, cos, sin, gq, gk, dq, dk, dv):
    return _rowwise_bwd(_f_prep2, [qall, kv, kr, cos, sin], [gq, gk], [dq, dk, dv], row_grads={0: BF16, 1: BF16, 2: F32},
                        const_grads=[0, 1], ts=256, name="mla_prep2_bwd")


def _rope_tables(pos):
    half = ROPE // 2
    inv = ROPE_THETA ** (-jnp.arange(half, dtype=F32) / half)
    ang = pos.astype(F32)[:, None] * inv[None, :]
    z = jnp.zeros((pos.shape[0], LANES - ROPE), F32)
    cos, sin = jnp.cos(ang), jnp.sin(ang)
    return jnp.concatenate([cos, cos, z], axis=-1), jnp.concatenate([sin, sin, z], axis=-1)


def _small_layout(p):
    lr, li, ldt, br, bi = _layout_ssm_in(p['ssm_a_re'], p['ssm_a_im'], p['ssm_log_dt'], p['ssm_b_re'], p['ssm_b_im'])
    return {
        'ffn1_norm': p['ffn1_norm'].reshape(1, D), 'mix_norm': p['mix_norm'].reshape(1, D),
        'q_norm': p['mla_q_norm'].reshape(1, Q_RANK), 'kv_norm': p['mla_kv_norm'].reshape(1, KV_RANK),
        'qk_gq': _layout_qk_gain(p['mla_qk_norm_q']), 'qk_gk': _layout_qk_gain(p['mla_qk_norm_k']),
        'ssm_lr': lr, 'ssm_li': li, 'ssm_ldt': ldt, 'ssm_br': br, 'ssm_bi': bi,
        'ssm_cr': p['ssm_c_re'], 'ssm_ci': p['ssm_c_im'], 'ssm_d': p['ssm_d'].reshape(1, SSM_W),
        'ssm_b_glu': p['ssm_b_glu'].reshape(1, SSM_W),
        'out_norm_mla': p['out_norm_mla'].reshape(1, SSM_W), 'out_norm_ssm': p['out_norm_ssm'].reshape(1, SSM_W),
        'xattn_norm': p['xattn_norm'].reshape(1, D), 'mem_norm': p['mem_norm'].reshape(1, D),
        'xattn_q_norm': p['xattn_q_norm'].reshape(1, XH), 'xattn_k_norm': p['xattn_k_norm'].reshape(1, XH),
        'ffn2_norm': p['ffn2_norm'].reshape(1, D),
    }


def _pack(arrs, rows):
    flat = jnp.concatenate([a.reshape(-1) for a in arrs])
    return jnp.pad(flat, (0, rows * D - flat.shape[0])).reshape(rows, D)


def _unpack(flat, shapes):
    flat = flat.reshape(-1)
    out, off = [], 0
    for sh in shapes:
        n = int(np.prod(sh))
        out.append(flat[off:off + n].reshape(sh))
        off += n
    return out


def kernel(x, mem, positions, ffn1_norm, ffn1_w_gate, ffn1_w_up, ffn1_w_down, mix_norm, w_in, mla_q_norm, mla_w_uq, mla_kv_norm, mla_w_ukv, mla_qk_norm_q, mla_qk_norm_k, ssm_a_re, ssm_a_im, ssm_log_dt, ssm_b_re, ssm_b_im, ssm_c_re, ssm_c_im, ssm_d, ssm_w_glu, ssm_b_glu, out_norm_mla, out_norm_ssm, w_o, xattn_norm, mem_norm, xattn_w_q, xattn_w_kv, xattn_q_norm, xattn_k_norm, xattn_w_o, ffn2_norm, ffn2_w_gate, ffn2_w_up, ffn2_w_down, loss_target, m_ffn1_norm, m_ffn1_w_gate, m_ffn1_w_up, m_ffn1_w_down, m_mix_norm, m_w_in, m_mla_q_norm, m_mla_w_uq, m_mla_kv_norm, m_mla_w_ukv, m_mla_qk_norm_q, m_mla_qk_norm_k, m_ssm_a_re, m_ssm_a_im, m_ssm_log_dt, m_ssm_b_re, m_ssm_b_im, m_ssm_c_re, m_ssm_c_im, m_ssm_d, m_ssm_w_glu, m_ssm_b_glu, m_out_norm_mla, m_out_norm_ssm, m_w_o, m_xattn_norm, m_mem_norm, m_xattn_w_q, m_xattn_w_kv, m_xattn_q_norm, m_xattn_k_norm, m_xattn_w_o, m_ffn2_norm, m_ffn2_w_gate, m_ffn2_w_up, m_ffn2_w_down, v_ffn1_norm, v_ffn1_w_gate, v_ffn1_w_up, v_ffn1_w_down, v_mix_norm, v_w_in, v_mla_q_norm, v_mla_w_uq, v_mla_kv_norm, v_mla_w_ukv, v_mla_qk_norm_q, v_mla_qk_norm_k, v_ssm_a_re, v_ssm_a_im, v_ssm_log_dt, v_ssm_b_re, v_ssm_b_im, v_ssm_c_re, v_ssm_c_im, v_ssm_d, v_ssm_w_glu, v_ssm_b_glu, v_out_norm_mla, v_out_norm_ssm, v_w_o, v_xattn_norm, v_mem_norm, v_xattn_w_q, v_xattn_w_kv, v_xattn_q_norm, v_xattn_k_norm, v_xattn_w_o, v_ffn2_norm, v_ffn2_w_gate, v_ffn2_w_up, v_ffn2_w_down):
    args = dict(locals())
    w = {n: args[n] for n in WEIGHTS}
    mom = {n: args['m_' + n] for n in WEIGHTS}
    var = {n: args['v_' + n] for n in WEIGHTS}
    return _step(x, mem, positions, loss_target, w, mom, var)


GATHER_GROUPS = [('ffn1_gu', ['ffn1_w_gate', 'ffn1_w_up']), ('ffn1_down', ['ffn1_w_down']),
                 ('mix', ['w_in', 'mla_w_uq', 'mla_w_ukv', 'ssm_w_glu', 'w_o', 'xattn_w_q', 'xattn_w_kv', 'xattn_w_o']),
                 ('ffn2', ['ffn2_w_gate', 'ffn2_w_up', 'ffn2_w_down'])]
SCATTER_GROUPS = [('ffn2_down', ['ffn2_w_down']), ('ffn2_gate', ['ffn2_w_gate']), ('ffn2_up', ['ffn2_w_up']),
                  ('xattn', ['xattn_w_o', 'xattn_w_kv', 'xattn_w_q']),
                  ('mix', ['w_o', 'ssm_w_glu', 'mla_w_uq', 'mla_w_ukv', 'w_in']),
                  ('ffn1_down', ['ffn1_w_down']), ('ffn1_gate', ['ffn1_w_gate']), ('ffn1_up', ['ffn1_w_up'])]


def _step(x, mem, positions, loss_target, w, mom, var):
    blocks = {n: _to_exchange_layout(n, w[n][0]).astype(BF16) for n in SHARDED}
    gathers, token = [], None
    for tag, names in GATHER_GROUPS:
        ex = _Exchange([blocks[n] for n in names], [blocks[n].shape[0] for n in names], gather=True,
                       name="gather_" + tag, after=token, two_level=True)
        gathers.append((names, ex))
        token = ex.token
    me = _my_slot()
    wc = _Weights(gathers, me=me)

    rows = {n: (blocks[n].shape[0], blocks[n].shape[0]) for n in SHARDED}
    ready, scatters = {}, []

    def send(grads):
        ready.update({n: g.astype(GRAD_DTYPE) for n, g in grads.items()})
        for tag, names in SCATTER_GROUPS:
            if all(n in ready for n in names) and not any(t == tag for t, _, _ in scatters):
                ex = _Exchange([ready[n] for n in names], [rows[n] for n in names], gather=False, name="scatter_" + tag)
                scatters.append((tag, names, ex))
                return ex.token
        return None

    small = {n: w[n][0] for n in SMALL}
    small_shapes = [small[n].shape for n in SMALL]
    n_small = sum(int(np.prod(sh)) for sh in small_shapes) + 1
    rows_small = -(-n_small // (8 * D)) * 8
    small_sent = []

    def send_small(gs, loss):
        known = dict(gs, ffn1_norm=jnp.zeros((1, D), F32))
        g_small = jax.linear_transpose(_small_layout, {n: jax.ShapeDtypeStruct(small[n].shape, F32) for n in SMALL})(known)[0]
        pack = _pack([g_small[n] for n in SMALL] + [loss[0, :1]], rows_small)
        small_sent.append(_Exchange([pack], [(None, rows_small)], gather=False, name="scatter_small"))
        return small_sent[0].token

    ws = _small_layout(small)
    cos, sin = _rope_tables(positions[0])
    loss, dx, gs = _local_step(x[0], mem[0], cos, sin, loss_target[0], wc, ws, send, deps=[token], send_small=send_small)
    pad8 = lambda a: jnp.pad(a.reshape(1, D), ((0, 7), (0, 0)))
    last_ex = _Exchange([pad8(gs['ffn1_norm'])], [(None, 8)], gather=False, name="scatter_last")

    out, after = {}, dx
    for _, names, ex in scatters:
        for n, sent, p in zip(names, *ex.wait(after)):
            r = w[n][0].shape[SHARD_AXIS[n]]
            if SHARD_AXIS[n] == 0:
                out[n] = _sum_adamw(me, sent, rows[n][0], p, r, w[n][0], mom[n][0], var[n][0], name="adamw_" + n)
            else:
                g = _sum_adamw(me, sent, rows[n][0], p, r, name="sum_" + n)[0].T
                out[n] = [g] + _adamw(g, w[n][0], mom[n][0], var[n][0], name="adamw_" + n)
        after = out[names[-1]][1]
    state = [_pack([t[n][0] for n in SMALL], rows_small) for t in (w, mom, var)]
    sent, p = small_sent[0].wait(after)
    small_out = _sum_adamw(me, sent[0], None, p[0], rows_small, *state, name="adamw_small")
    loss_total = small_out[0].reshape(-1)[n_small - 1]
    for n, vals in zip(SMALL, zip(*[_unpack(flat, small_shapes) for flat in small_out])):
        out[n] = vals
    sent, p = last_ex.wait(small_out[1])
    last_out = _sum_adamw(me, sent[0], None, p[0], 8, *[pad8(t['ffn1_norm'][0]) for t in (w, mom, var)], name="adamw_last")
    out['ffn1_norm'] = [o[0] for o in last_out]
    outs = [out[n][i][None] for i in range(4) for n in WEIGHTS]
    return (loss_total, dx[None], *outs)
```

```python
import math

import jax
import jax.numpy as jnp
import numpy as np
from jax import lax
from jax.experimental import pallas as pl
from jax.experimental.pallas import tpu as pltpu

F32 = jnp.float32
BF16 = jnp.bfloat16

N_DEV = 8
D = 1024
D_FF = 2752
D_FFP = 2816
MEM_LEN = 256
H = 4
Q_RANK, KV_RANK, NOPE, ROPE, VD = 384, 256, 128, 64, 128
QK = NOPE + ROPE
HQ = 2 * 128
SSM_W, SSM_G, SSM_GRP, SSM_P = 512, 32, 16, 64
SSM_N = SSM_G * SSM_P
SSM_PACK = 8
IN_W = 1216
IN_WP = 1408
XH = 128
EPS = 1e-6
LN2 = math.log(2.0)
ROPE_THETA = 10000.0
SCAN_CHUNKS = 8
SCAN_UNROLL = 8
ADAM_LR, ADAM_B1, ADAM_B2, ADAM_EPS, ADAM_WD, ADAM_STEP = 0.001, 0.9, 0.999, 1e-08, 0.01, 10

VMEM_LIMIT = 56 * 1024 * 1024
ACC_BYTES = 6 * 1024 * 1024
LANES = 128
BF16_ROWS = 16
GRAD_DTYPE = BF16
FF_SHARD = D_FF // N_DEV
FF_SHARD_P = 352
IN_SHARD = IN_W // N_DEV
IN_SHARD_P = 160

WEIGHTS = ['ffn1_norm', 'ffn1_w_gate', 'ffn1_w_up', 'ffn1_w_down', 'mix_norm', 'w_in', 'mla_q_norm', 'mla_w_uq',
           'mla_kv_norm', 'mla_w_ukv', 'mla_qk_norm_q', 'mla_qk_norm_k', 'ssm_a_re', 'ssm_a_im', 'ssm_log_dt',
           'ssm_b_re', 'ssm_b_im', 'ssm_c_re', 'ssm_c_im', 'ssm_d', 'ssm_w_glu', 'ssm_b_glu', 'out_norm_mla',
           'out_norm_ssm', 'w_o', 'xattn_norm', 'mem_norm', 'xattn_w_q', 'xattn_w_kv', 'xattn_q_norm',
           'xattn_k_norm', 'xattn_w_o', 'ffn2_norm', 'ffn2_w_gate', 'ffn2_w_up', 'ffn2_w_down']
SHARD_AXIS = {'ffn1_w_gate': 1, 'ffn1_w_up': 1, 'ffn1_w_down': 0, 'w_in': 1, 'mla_w_uq': 1, 'mla_w_ukv': 1,
              'ssm_w_glu': 0, 'w_o': 0, 'xattn_w_q': 0, 'xattn_w_kv': 0, 'xattn_w_o': 1,
              'ffn2_w_gate': 1, 'ffn2_w_up': 1, 'ffn2_w_down': 0}
SHARDED = [n for n in WEIGHTS if n in SHARD_AXIS]
SMALL = [n for n in WEIGHTS if n not in SHARD_AXIS]


def _params(sem=None):
    return pltpu.CompilerParams(dimension_semantics=sem, vmem_limit_bytes=VMEM_LIMIT)


def _tile(n, cap):
    if n <= cap:
        return n
    best = n
    for t in range(LANES, cap + 1, LANES):
        if n % t == 0:
            best = t
    return best


def _mm(a, b, *, ta=False, tb=False, out_dtype=F32, res=None, scale=1.0, name, tm_cap=512, tn_cap=1408, tk_cap=2816,
        deps=()):
    m, k = (a.shape[1], a.shape[0]) if ta else a.shape
    k2, n = (b.shape[1], b.shape[0]) if tb else b.shape
    assert k == k2, (a.shape, b.shape, ta, tb)
    if ta:
        tk_cap = min(tk_cap, 512)
        tm_cap = 1408
    tm, tn, tk = _tile(m, tm_cap), _tile(n, tn_cap), _tile(k, tk_cap)
    if tm * tn * 4 > ACC_BYTES:
        tn = _tile(n, max(LANES, ACC_BYTES // (4 * tm) // LANES * LANES))
    nk = k // tk
    dims = (((0 if ta else 1,), (1 if tb else 0,)), ((), ()))
    has_res = res is not None

    deps = [d for d in deps if d is not None]

    def body(*refs):
        a_ref, b_ref = refs[:2]
        r_ref = refs[2] if has_res else None
        o_ref, acc_ref = refs[-2:]
        kk = pl.program_id(2)

        @pl.when(kk == 0)
        def _():
            acc_ref[...] = jnp.zeros_like(acc_ref)

        acc_ref[...] += lax.dot_general(a_ref[...].astype(BF16), b_ref[...].astype(BF16), dims,
                                        preferred_element_type=F32)

        @pl.when(kk == nk - 1)
        def _():
            out = acc_ref[...]
            if scale != 1.0:
                out = out * scale
            if has_res:
                out = out + r_ref[...].astype(F32)
            o_ref[...] = out.astype(o_ref.dtype)

    a_spec = pl.BlockSpec((tk, tm), lambda i, j, kk: (kk, i)) if ta else pl.BlockSpec((tm, tk), lambda i, j, kk: (i, kk))
    b_spec = pl.BlockSpec((tn, tk), lambda i, j, kk: (j, kk)) if tb else pl.BlockSpec((tk, tn), lambda i, j, kk: (kk, j))
    o_spec = pl.BlockSpec((tm, tn), lambda i, j, kk: (i, j))
    in_specs = [a_spec, b_spec] + ([o_spec] if has_res else []) + [pl.BlockSpec(d.shape, lambda i, j, kk: (0, 0)) for d in deps]
    args = (a, b) + ((res,) if has_res else ()) + tuple(deps)
    return pl.pallas_call(
        body, name=name, grid=(m // tm, n // tn, nk), in_specs=in_specs, out_specs=o_spec,
        out_shape=jax.ShapeDtypeStruct((m, n), out_dtype), scratch_shapes=[pltpu.VMEM((tm, tn), F32)],
        compiler_params=_params(("parallel", "parallel", "arbitrary")),
    )(*args)


def _mm_grouped(a, b, *, tb=False, res=None, out_dtype=F32, name, tm=512):
    s = a.shape[0]
    g = b.shape[0]
    nb, ka = (b.shape[1], b.shape[2]) if tb else (b.shape[2], b.shape[1])
    assert a.shape[1] == g * ka
    tm = min(tm, s)
    dims = (((1,), (1 if tb else 0,)), ((), ()))
    has_res = res is not None

    def body(*refs):
        if has_res:
            a_ref, b_ref, r_ref, o_ref = refs
        else:
            a_ref, b_ref, o_ref = refs
        out = lax.dot_general(a_ref[...].astype(BF16), b_ref[...].astype(BF16), dims, preferred_element_type=F32)
        if has_res:
            out = out + r_ref[...].astype(F32)
        o_ref[...] = out.astype(o_ref.dtype)

    o_spec = pl.BlockSpec((tm, nb), lambda i, j: (i, j))
    in_specs = [pl.BlockSpec((tm, ka), lambda i, j: (i, j)), pl.BlockSpec((None,) + b.shape[1:], lambda i, j: (j, 0, 0))]
    return pl.pallas_call(
        body, name=name, grid=(s // tm, g), in_specs=in_specs + ([o_spec] if has_res else []), out_specs=o_spec,
        out_shape=jax.ShapeDtypeStruct((s, g * nb), out_dtype), compiler_params=_params(("parallel", "parallel")),
    )(a, b, *((res,) if has_res else ()))


def _mm_grouped_tn(a, b, *, ka, kb, name, tk=512):
    s = a.shape[0]
    g = a.shape[1] // ka
    assert b.shape[1] == g * kb
    tk = min(tk, s)
    nk = s // tk

    def body(a_ref, b_ref, o_ref):
        part = lax.dot_general(a_ref[...].astype(BF16), b_ref[...].astype(BF16), (((0,), (0,)), ((), ())),
                               preferred_element_type=F32)

        @pl.when(pl.program_id(1) == 0)
        def _():
            o_ref[...] = part

        @pl.when(pl.program_id(1) > 0)
        def _():
            o_ref[...] += part

    return pl.pallas_call(
        body, name=name, grid=(g, nk),
        in_specs=[pl.BlockSpec((tk, ka), lambda j, kk: (kk, j)), pl.BlockSpec((tk, kb), lambda j, kk: (kk, j))],
        out_specs=pl.BlockSpec((None, ka, kb), lambda j, kk: (j, 0, 0)),
        out_shape=jax.ShapeDtypeStruct((g, ka, kb), F32), compiler_params=_params(("parallel", "arbitrary")),
    )(a, b)


def _rowwise(fn, rows, consts, outs, accs=(), *, ts=512, name, deps=()):
    rows = [r if isinstance(r, tuple) else (r, 0, r.shape[1]) for r in rows]
    s = rows[0][0].shape[0]
    ts = min(ts, s)
    assert s % ts == 0
    n_rows, n_consts, n_outs = len(rows), len(consts), len(outs)
    deps = [d for d in deps if d is not None]
    consts = list(consts) + deps

    def body(*refs):
        ins = [r[...] for r in refs[:n_rows + n_consts]]
        res = fn(*ins)
        res = tuple(res) if isinstance(res, (tuple, list)) else (res,)
        out_refs = refs[n_rows + len(consts):]
        for o_ref, val in zip(out_refs[:n_outs], res[:n_outs]):
            o_ref[...] = val.astype(o_ref.dtype)
        if accs:
            first = pl.program_id(0) == 0

            @pl.when(first)
            def _():
                for a_ref, val in zip(out_refs[n_outs:], res[n_outs:]):
                    a_ref[...] = val.astype(F32)

            @pl.when(jnp.logical_not(first))
            def _():
                for a_ref, val in zip(out_refs[n_outs:], res[n_outs:]):
                    a_ref[...] += val.astype(F32)

    in_specs = [pl.BlockSpec((ts, width), lambda i, cb=cb: (i, cb)) for _, cb, width in rows]
    in_specs += [pl.BlockSpec(c.shape, lambda i: (0, 0)) for c in consts]
    out_specs = [pl.BlockSpec((ts, w), lambda i: (i, 0)) for w, _ in outs]
    out_specs += [pl.BlockSpec(tuple(sh), lambda i: (0, 0)) for sh in accs]
    out_shape = [jax.ShapeDtypeStruct((s, w), dt) for w, dt in outs]
    out_shape += [jax.ShapeDtypeStruct(tuple(sh), F32) for sh in accs]
    res = pl.pallas_call(
        body, name=name, grid=(s // ts,), in_specs=in_specs, out_specs=out_specs, out_shape=out_shape,
        compiler_params=_params(("arbitrary",)),
    )(*[a for a, _, _ in rows], *consts)
    return res


def _rowwise_bwd(f, rows, consts, cts, *, row_grads, const_grads, adds=None, ts=512, name, deps=()):
    adds = adds or {}
    n_rows, n_consts, n_cts = len(rows), len(consts), len(cts)
    add_keys = sorted(adds)
    rg = sorted(row_grads)
    cg = sorted(const_grads)

    def fn(*args):
        r = args[:n_rows]
        c = args[n_rows:n_rows + n_consts]
        ct = args[n_rows + n_consts:n_rows + n_consts + n_cts]
        extra = args[n_rows + n_consts + n_cts:]
        outs, vjp = jax.vjp(f, *r, *c)
        outs = tuple(outs) if isinstance(outs, (tuple, list)) else (outs,)
        cot = tuple(g.astype(o.dtype) for g, o in zip(ct, outs))
        grads = vjp(cot if len(cot) > 1 else cot[0])
        res = []
        for i in rg:
            g = grads[i].astype(F32)
            if i in adds:
                g = g + extra[add_keys.index(i)].astype(F32)
            res.append(g)
        for i in cg:
            res.append(grads[n_rows + i])
        return tuple(res)

    rows_all = list(rows) + list(cts) + [adds[i] for i in add_keys]
    def fn2(*args):
        nr = len(rows_all)
        rr, cc = args[:nr], args[nr:]
        return fn(*rr[:n_rows], *cc, *rr[n_rows:])

    outs = [(rows[i][2] if isinstance(rows[i], tuple) else rows[i].shape[1], row_grads[i]) for i in rg]
    accs = [consts[i].shape for i in cg]
    return _rowwise(fn2, rows_all, list(consts), outs, accs, ts=ts, name=name, deps=deps)


def _rms(x, g):
    xf = x.astype(F32)
    return xf * lax.rsqrt(jnp.mean(xf * xf, axis=-1, keepdims=True) + EPS) * g.astype(F32)


def _sigmoid(x):
    return 1.0 / (1.0 + jnp.exp(-x))


def _f_norm(x, g):
    return _rms(x, g).astype(BF16)


def _f_swiglu(gate, up):
    gate, up = gate.astype(F32), up.astype(F32)
    return (gate * _sigmoid(gate) * up).astype(BF16)


def _f_prep1(proj, gq, gkv):
    return _rms(proj[:, :Q_RANK], gq).astype(BF16), _rms(proj[:, Q_RANK:Q_RANK + KV_RANK], gkv).astype(BF16)


KR_BLOCK = (Q_RANK + KV_RANK + SSM_W) // LANES


def _f_prep2(qall, kv, kr, krs, cos, sin, gq, gk):
    kr, krs = kr.astype(F32), krs.astype(F32)
    k_rot = kr * gk[1:2] * cos + krs * gk[2:3] * sin
    k_ss = jnp.sum(kr * kr, axis=-1, keepdims=True)
    q_scale = QK ** -0.5 / LN2
    qs, ks, vs = [], [], []
    for h in range(H):
        qn = qall[:, h * LANES:(h + 1) * LANES].astype(F32)
        qr = qall[:, (H + h) * LANES:(H + h + 1) * LANES].astype(F32)
        qrs = qall[:, (2 * H + h) * LANES:(2 * H + h + 1) * LANES].astype(F32)
        rstd = lax.rsqrt((jnp.sum(qn * qn, axis=-1, keepdims=True) + jnp.sum(qr * qr, axis=-1, keepdims=True)) / QK + EPS)
        rstd = rstd * q_scale
        qs += [qn * gq[0:1] * rstd, (qr * gq[1:2] * cos + qrs * gq[2:3] * sin) * rstd]
        kn = kv[:, 2 * h * LANES:(2 * h + 1) * LANES].astype(F32)
        rstd_k = lax.rsqrt((jnp.sum(kn * kn, axis=-1, keepdims=True) + k_ss) / QK + EPS)
        ks += [kn * gk[0:1] * rstd_k, k_rot * rstd_k]
        vs.append(kv[:, (2 * h + 1) * LANES:(2 * h + 2) * LANES])
    return (jnp.concatenate(qs, axis=-1).astype(BF16), jnp.concatenate(ks, axis=-1).astype(BF16),
            jnp.concatenate(vs, axis=-1).astype(BF16))


def _gelu(x):
    return 0.5 * x * (1.0 + jnp.tanh(math.sqrt(2.0 / math.pi) * (x + 0.044715 * (x * x * x))))


def _f_s5_gelu(yc, u, d):
    return _gelu(yc.astype(F32) + d * u.astype(F32))


def _f_outnorm(o_mla, g, z, b_glu, g_om, g_os):
    y_ssm = g * _sigmoid(z + b_glu)
    return jnp.concatenate([_rms(o_mla, g_om), _rms(y_ssm, g_os)], axis=-1).astype(BF16)


def _f_memk(kvm, gk):
    ks = [_rms(kvm[:, h * XH:(h + 1) * XH], gk) for h in range(H)]
    return jnp.concatenate(ks, axis=-1).astype(BF16), kvm[:, H * XH:].astype(BF16)


def _f_disc(lr, li, log_dt, br, bi):
    dt = jnp.exp(log_dt)
    decay = jnp.exp(lr * dt)
    ar = decay * jnp.cos(li * dt)
    ai = decay * jnp.sin(li * dt)
    den = lr * lr + li * li
    nr = ar - 1.0
    coef_r = (nr * lr + ai * li) / den
    coef_i = (ai * lr - nr * li) / den
    return ar, ai, coef_r * br - coef_i * bi, coef_r * bi + coef_i * br


def _causal_mask(i, j, tq, tk):
    qpos = i * tq + lax.broadcasted_iota(jnp.int32, (tq, tk), 0)
    kpos = j * tk + lax.broadcasted_iota(jnp.int32, (tq, tk), 1)
    return qpos >= kpos


def _attn_fwd(q, k, v, *, t=512):
    s = q.shape[0]
    t = min(t, s)
    nb = s // t

    def body(q_ref, k_ref, v_ref, o_ref, lse_ref, m_sc, l_sc, acc_sc):
        i, j = pl.program_id(1), pl.program_id(2)

        @pl.when(j == 0)
        def _():
            m_sc[...] = jnp.full_like(m_sc, -jnp.inf)
            l_sc[...] = jnp.zeros_like(l_sc)
            acc_sc[...] = jnp.zeros_like(acc_sc)

        def block(diagonal):
            sc = lax.dot_general(q_ref[...], k_ref[...], (((1,), (1,)), ((), ())), preferred_element_type=F32)
            if diagonal:
                sc = jnp.where(_causal_mask(i, j, t, t), sc, -jnp.inf)
            m_old = m_sc[...]
            m_new = jnp.maximum(m_old, jnp.max(sc, axis=-1, keepdims=True))
            p = jnp.exp2(sc - m_new)
            alpha = jnp.exp2(m_old - m_new)
            l_sc[...] = alpha * l_sc[...] + jnp.sum(p, axis=-1, keepdims=True)
            acc_sc[...] = alpha * acc_sc[...] + jnp.dot(p.astype(BF16), v_ref[...], preferred_element_type=F32)
            m_sc[...] = m_new

        pl.when(j < i)(lambda: block(False))

        @pl.when(j == i)
        def _():
            block(True)
            o_ref[...] = acc_sc[...] / l_sc[...]
            lse_ref[...] = jnp.broadcast_to(m_sc[...] + jnp.log2(l_sc[...]), lse_ref.shape)

    kv_map = lambda h, i, j: (jnp.minimum(j, i), h)
    return pl.pallas_call(
        body, name="mla_attn_fwd", grid=(H, nb, nb),
        in_specs=[pl.BlockSpec((t, HQ), lambda h, i, j: (i, h)), pl.BlockSpec((t, HQ), kv_map),
                  pl.BlockSpec((t, VD), kv_map)],
        out_specs=[pl.BlockSpec((t, VD), lambda h, i, j: (i, h)), pl.BlockSpec((t, LANES), lambda h, i, j: (i, h))],
        out_shape=[jax.ShapeDtypeStruct((s, H * VD), F32), jax.ShapeDtypeStruct((s, H * LANES), F32)],
        scratch_shapes=[pltpu.VMEM((t, 1), F32), pltpu.VMEM((t, 1), F32), pltpu.VMEM((t, VD), F32)],
        compiler_params=_params(("parallel", "parallel", "arbitrary")),
    )(q, k, v)


def _attn_probs(q_ref, k_ref, v_ref, do_ref, lse_ref, dl_ref, i, j, t, diagonal):
    sc = lax.dot_general(q_ref[...], k_ref[...], (((1,), (1,)), ((), ())), preferred_element_type=F32)
    p = jnp.exp2(sc - jnp.tile(lse_ref[...], (1, t // LANES)))
    if diagonal:
        p = jnp.where(_causal_mask(i, j, t, t), p, 0.0)
    dp = lax.dot_general(do_ref[...], v_ref[...], (((1,), (1,)), ((), ())), preferred_element_type=F32)
    ds = p * (dp - jnp.tile(dl_ref[...], (1, t // LANES)))
    return p, ds


def _attn_bwd(q, k, v, do, lse, delta, *, t=512):
    s = q.shape[0]
    t = min(t, s)
    nb = s // t

    def body(q_ref, k_ref, v_ref, do_ref, lse_ref, dl_ref, dq_ref, dk_ref, dv_ref, dk_sc, dv_sc):
        j, i = pl.program_id(1), pl.program_id(2)

        @pl.when(jnp.logical_and(i == 0, j == 0))
        def _():
            dq_ref[...] = jnp.zeros_like(dq_ref)

        @pl.when(i == 0)
        def _():
            dk_sc[...] = jnp.zeros_like(dk_sc)
            dv_sc[...] = jnp.zeros_like(dv_sc)

        def block(diagonal):
            p, ds = _attn_probs(q_ref, k_ref, v_ref, do_ref, lse_ref, dl_ref, i, j, t, diagonal)
            dsb = ds.astype(BF16)
            dv_sc[...] += lax.dot_general(p.astype(BF16), do_ref[...], (((0,), (0,)), ((), ())), preferred_element_type=F32)
            dk_sc[...] += lax.dot_general(dsb, q_ref[...], (((0,), (0,)), ((), ())), preferred_element_type=F32)
            rows = pl.ds(pl.multiple_of(i * t, t), t)
            dq_ref[rows, :] += jnp.dot(dsb, k_ref[...], preferred_element_type=F32)

        pl.when(i > j)(lambda: block(False))
        pl.when(i == j)(lambda: block(True))

        @pl.when(i == nb - 1)
        def _():
            dk_ref[...] = (dk_sc[...] * LN2).astype(dk_ref.dtype)
            dv_ref[...] = dv_sc[...].astype(dv_ref.dtype)

        @pl.when(jnp.logical_and(i == nb - 1, j == nb - 1))
        def _():
            dq_ref[...] = dq_ref[...] * LN2

    q_map = lambda h, j, i: (jnp.maximum(i, j), h)
    kv_map = lambda h, j, i: (j, h)
    dq, dk, dv = pl.pallas_call(
        body, name="mla_attn_bwd", grid=(H, nb, nb),
        in_specs=[pl.BlockSpec((t, HQ), q_map), pl.BlockSpec((t, HQ), kv_map), pl.BlockSpec((t, VD), kv_map),
                  pl.BlockSpec((t, VD), q_map), pl.BlockSpec((t, LANES), q_map), pl.BlockSpec((t, LANES), q_map)],
        out_specs=[pl.BlockSpec((s, HQ), lambda h, j, i: (0, h)), pl.BlockSpec((t, HQ), kv_map), pl.BlockSpec((t, VD), kv_map)],
        out_shape=[jax.ShapeDtypeStruct((s, H * HQ), F32), jax.ShapeDtypeStruct((s, H * HQ), BF16),
                   jax.ShapeDtypeStruct((s, H * VD), BF16)],
        scratch_shapes=[pltpu.VMEM((t, HQ), F32), pltpu.VMEM((t, VD), F32)],
        compiler_params=_params(("parallel", "arbitrary", "arbitrary")),
    )(q, k, v, do, lse, delta)
    return dq, dk, dv


def _f_delta(do, o):
    prod = do.astype(F32) * o.astype(F32)
    parts = [jnp.broadcast_to(jnp.sum(prod[:, h * VD:(h + 1) * VD], axis=-1, keepdims=True), (do.shape[0], LANES))
             for h in range(H)]
    return jnp.concatenate(parts, axis=-1), do.astype(BF16)


def _xattn_head(qh, kh, gq):
    qn = _rms(qh, gq) * (XH ** -0.5)
    sc = lax.dot_general(qn.astype(BF16), kh, (((1,), (1,)), ((), ())), preferred_element_type=F32)
    sc = sc - jnp.max(sc, axis=-1, keepdims=True)
    e = jnp.exp(sc)
    return qn, e / jnp.sum(e, axis=-1, keepdims=True)


def _xattn_fwd(q, kn, v, gq, *, ts=512):
    def fn(qb, knb, vb, g):
        outs = []
        for h in range(H):
            sl = slice(h * XH, (h + 1) * XH)
            _, p = _xattn_head(qb[:, sl], knb[:, sl], g)
            outs.append(jnp.dot(p.astype(BF16), vb[:, sl], preferred_element_type=F32))
        return (jnp.concatenate(outs, axis=-1),)

    return _rowwise(fn, [q], [kn, v, gq], [(H * XH, BF16)], ts=ts, name="xattn_fwd")[0]


def _xattn_bwd(q, kn, v, gq, do, *, ts=512):
    def fn(qb, dob, knb, vb, g):
        dqs, dks, dvs = [], [], []
        dg = jnp.zeros((1, XH), F32)
        for h in range(H):
            sl = slice(h * XH, (h + 1) * XH)
            qh, kh, vh, doh = qb[:, sl], knb[:, sl], vb[:, sl], dob[:, sl].astype(BF16)
            qn, p = _xattn_head(qh, kh, g)
            dp = lax.dot_general(doh, vh, (((1,), (1,)), ((), ())), preferred_element_type=F32)
            dvs.append(lax.dot_general(p.astype(BF16), doh, (((0,), (0,)), ((), ())), preferred_element_type=F32))
            ds = (p * (dp - jnp.sum(dp * p, axis=-1, keepdims=True))).astype(BF16)
            dqn = jnp.dot(ds, kh, preferred_element_type=F32)
            dks.append(lax.dot_general(ds, qn.astype(BF16), (((0,), (0,)), ((), ())), preferred_element_type=F32))
            _, vjp_n = jax.vjp(lambda a, b: _rms(a, b) * (XH ** -0.5), qh, g)
            dqh, dgh = vjp_n(dqn)
            dqs.append(dqh)
            dg = dg + dgh
        return (jnp.concatenate(dqs, axis=-1), jnp.concatenate(dks, axis=-1), jnp.concatenate(dvs, axis=-1), dg)

    return _rowwise(fn, [q, do], [kn, v, gq], [(H * XH, BF16)], [kn.shape, v.shape, gq.shape], ts=ts, name="xattn_bwd")


def _cmul(ar, ai, xr, xi):
    return ar * xr - ai * xi, ar * xi + ai * xr


def _scan_in_place(xr_ref, xi_ref, ar, ai, *, reverse):
    s, cw = xr_ref.shape
    c = SCAN_CHUNKS
    tt = s // c
    a_r = jnp.broadcast_to(ar, (c, cw))
    a_i = jnp.broadcast_to(ai, (c, cw))
    zero = jnp.zeros((c, cw), F32)

    def row(step):
        t = (tt - 1 - step) if reverse else step
        return pl.ds(pl.multiple_of(t * c, c), c)

    def local(step, carry):
        sr, si, qr, qi = carry
        r = row(step)
        nr, ni = _cmul(a_r, a_i, sr, si)
        nr, ni = nr + xr_ref[r, :], ni + xi_ref[r, :]
        xr_ref[r, :] = nr
        xi_ref[r, :] = ni
        return (nr, ni) + _cmul(a_r, a_i, qr, qi)

    end_r, end_i, pr, pi = lax.fori_loop(0, tt, local, (zero, zero, jnp.ones((c, cw), F32), zero), unroll=SCAN_UNROLL)

    rows_id = lax.broadcasted_iota(jnp.int32, (c, cw), 0)
    car_r, car_i = zero, zero
    cur_r, cur_i = jnp.zeros((1, cw), F32), jnp.zeros((1, cw), F32)
    order = range(c - 1, -1, -1) if reverse else range(c)
    for kk in order:
        car_r = jnp.where(rows_id == kk, cur_r, car_r)
        car_i = jnp.where(rows_id == kk, cur_i, car_i)
        nr, ni = _cmul(pr[0:1], pi[0:1], cur_r, cur_i)
        cur_r = nr + end_r[kk:kk + 1]
        cur_i = ni + end_i[kk:kk + 1]

    def fix(step, carry):
        qr, qi = _cmul(a_r, a_i, *carry)
        r = row(step)
        dr, di = _cmul(qr, qi, car_r, car_i)
        xr_ref[r, :] += dr
        xi_ref[r, :] += di
        return qr, qi

    lax.fori_loop(0, tt, fix, (jnp.ones((c, cw), F32), zero), unroll=SCAN_UNROLL)


S5_ROWS = 512


def _s5_scan(v, w_r, w_i, ar, ai, *, reverse, tb, readout=None, name):
    s = v.shape[0]
    g = w_r.shape[0]
    nv, ns = SSM_PACK * SSM_GRP, SSM_PACK * SSM_P
    rows = min(S5_ROWS, s)
    dims = (((1,), (1 if tb else 0,)), ((), ()))
    n_w = 2 if readout is None else 4

    def body(v_ref, ar_ref, ai_ref, *refs):
        w = [r[...] for r in refs[:n_w]]
        xr_ref, xi_ref = refs[n_w:n_w + 2]
        for r0 in range(0, s, rows):
            vb = v_ref[r0:r0 + rows, :].astype(BF16)
            xr_ref[r0:r0 + rows, :] = lax.dot_general(vb, w[0], dims, preferred_element_type=F32)
            xi_ref[r0:r0 + rows, :] = lax.dot_general(vb, w[1], dims, preferred_element_type=F32)
        _scan_in_place(xr_ref, xi_ref, ar_ref[...], ai_ref[...], reverse=reverse)
        if readout is not None:
            y_ref = refs[n_w + 2]
            for r0 in range(0, s, rows):
                y_ref[r0:r0 + rows, :] = (
                    jnp.dot(xr_ref[r0:r0 + rows, :].astype(BF16), w[2], preferred_element_type=F32)
                    + jnp.dot(xi_ref[r0:r0 + rows, :].astype(BF16), w[3], preferred_element_type=F32))

    col = lambda j: (0, j)
    w_spec = lambda a: pl.BlockSpec((None,) + a.shape[1:], lambda j: (j, 0, 0))
    weights = [w_r, w_i] + (list(readout) if readout is not None else [])
    out_specs = [pl.BlockSpec((s, ns), col)] * 2 + ([pl.BlockSpec((s, nv), col)] if readout is not None else [])
    out_shape = [jax.ShapeDtypeStruct((s, g * ns), F32)] * 2 + (
        [jax.ShapeDtypeStruct((s, g * nv), F32)] if readout is not None else [])
    return pl.pallas_call(
        body, name=name, grid=(g,),
        in_specs=[pl.BlockSpec((s, nv), col), pl.BlockSpec((1, ns), col), pl.BlockSpec((1, ns), col)] + [w_spec(a) for a in weights],
        out_specs=out_specs, out_shape=out_shape, compiler_params=_params(("parallel",)),
    )(v, ar, ai, *weights)


def _s5_grads(lam_r, lam_i, xr, xi, u, dyc, du_d, b_r, b_i):
    s = u.shape[0]
    g = b_r.shape[0]
    nv, ns, c = SSM_PACK * SSM_GRP, SSM_PACK * SSM_P, SCAN_CHUNKS
    rows = min(S5_ROWS, s)
    slabs = rows // c
    last_slab = s // c - 1
    nt = (((1,), (1,)), ((), ()))
    tn = (((0,), (0,)), ((), ()))

    def body(lr_ref, li_ref, xr_ref, xi_ref, pr_ref, pi_ref, u_ref, dy_ref, dud_ref, br_ref, bi_ref,
             du_ref, dbr_ref, dbi_ref, dcr_ref, dci_ref, dar_ref, dai_ref):
        first = pl.program_id(1) == 0
        l_r, l_i, x_r, x_i = lr_ref[...], li_ref[...], xr_ref[...], xi_ref[...]
        lrb, lib = l_r.astype(BF16), l_i.astype(BF16)
        du_ref[...] = (dud_ref[...] + lax.dot_general(lrb, br_ref[...], nt, preferred_element_type=F32)
                       + lax.dot_general(lib, bi_ref[...], nt, preferred_element_type=F32))
        ub, dyb = u_ref[...].astype(BF16), dy_ref[...].astype(BF16)
        rows_id = lax.broadcasted_iota(jnp.int32, (c, ns), 0)

        def before(p_ref, x):
            p = p_ref[...]
            p = jnp.where(first, jnp.where(rows_id == 0, 0.0, pltpu.roll(p, 1, 0)), p)
            return jnp.concatenate([p, x[:rows - c]], axis=0)

        xp_r, xp_i = before(pr_ref, x_r), before(pi_ref, x_i)
        parts = (lax.dot_general(ub, lrb, tn, preferred_element_type=F32),
                 lax.dot_general(ub, lib, tn, preferred_element_type=F32),
                 lax.dot_general(x_r.astype(BF16), dyb, tn, preferred_element_type=F32),
                 lax.dot_general(x_i.astype(BF16), dyb, tn, preferred_element_type=F32),
                 jnp.sum(l_r * xp_r + l_i * xp_i, axis=0, keepdims=True),
                 jnp.sum(l_i * xp_r - l_r * xp_i, axis=0, keepdims=True))
        accs = (dbr_ref, dbi_ref, dcr_ref, dci_ref, dar_ref, dai_ref)

        @pl.when(first)
        def _():
            for a_ref, val in zip(accs, parts):
                a_ref[...] = val

        @pl.when(jnp.logical_not(first))
        def _():
            for a_ref, val in zip(accs, parts):
                a_ref[...] += val

    state = pl.BlockSpec((rows, ns), lambda j, k: (k, j))
    chan = pl.BlockSpec((rows, nv), lambda j, k: (k, j))
    slab = pl.BlockSpec((c, ns), lambda j, k: (jnp.where(k == 0, last_slab, k * slabs - 1), j))
    per_b = pl.BlockSpec((None, nv, ns), lambda j, k: (j, 0, 0))
    per_c = pl.BlockSpec((None, ns, nv), lambda j, k: (j, 0, 0))
    per_a = pl.BlockSpec((1, ns), lambda j, k: (0, j))
    return pl.pallas_call(
        body, name="s5_grads", grid=(g, s // rows),
        in_specs=[state, state, state, state, slab, slab, chan, chan, chan, per_b, per_b],
        out_specs=[chan, per_b, per_b, per_c, per_c, per_a, per_a],
        out_shape=[jax.ShapeDtypeStruct((s, g * nv), F32), jax.ShapeDtypeStruct((g, nv, ns), F32),
                   jax.ShapeDtypeStruct((g, nv, ns), F32), jax.ShapeDtypeStruct((g, ns, nv), F32),
                   jax.ShapeDtypeStruct((g, ns, nv), F32), jax.ShapeDtypeStruct((1, g * ns), F32),
                   jax.ShapeDtypeStruct((1, g * ns), F32)],
        compiler_params=_params(("parallel", "arbitrary")),
    )(lam_r, lam_i, xr, xi, xr, xi, u, dyc, du_d, b_r, b_i)


def _mesh_place():
    x, y, c = lax.axis_index("x"), lax.axis_index("y"), lax.axis_index("c")
    peers = []
    for k in range(1, N_DEV):
        px, py, pc = x ^ ((k >> 2) & 1), y ^ ((k >> 1) & 1), c ^ (k & 1)
        peers.append(((px, py, pc), 4 * px + 2 * py + pc))
    return 4 * x + 2 * y + c, peers


class _Exchange:
    SAME_CORE_MASKS = (2, 4, 6)

    def __init__(self, arrays, rows, *, gather, name, after=None, two_level=False):
        self.n_arr, self.rows, self.gather, self.name = len(arrays), rows, gather, name
        self.two_level, self.in_flight = two_level, (1 + len(self.SAME_CORE_MASKS) if two_level else N_DEV - 1)
        n_arr = self.n_arr
        if gather:
            assert all(r % BF16_ROWS == 0 for r in rows)
            lands = [lax.empty((N_DEV * r, a.shape[1]), a.dtype) for a, r in zip(arrays, rows)]
        else:
            lands = [lax.empty((N_DEV - 1,) + (tuple(a.shape) if st is None else (n, a.shape[1])), a.dtype)
                     for a, (st, n) in zip(arrays, rows)]
        has_after = after is not None

        def body(*refs):
            ins, zones = refs[:n_arr], refs[n_arr:2 * n_arr]
            sems = refs[2 * n_arr + has_after:4 * n_arr + has_after]
            token = refs[-1]
            me, peers = _mesh_place()
            for i in range(n_arr):
                for k, (pxyz, pid) in enumerate(peers):
                    if two_level and k + 1 not in (1,) + self.SAME_CORE_MASKS:
                        continue
                    if gather:
                        src = ins[i]
                        dst = zones[i].at[pl.ds(pl.multiple_of(me * rows[i], BF16_ROWS), rows[i])]
                    else:
                        stride, n = rows[i]
                        src = ins[i] if stride is None else ins[i].at[pl.ds(pl.multiple_of(pid * stride, BF16_ROWS), n)]
                        dst = zones[i].at[k]
                    pltpu.make_async_remote_copy(
                        src_ref=src, dst_ref=dst, send_sem=sems[2 * i], recv_sem=sems[2 * i + 1],
                        device_id=pxyz, device_id_type=pl.DeviceIdType.MESH).start()
            token[...] = jnp.zeros_like(token)

        hbm = pl.BlockSpec(memory_space=pltpu.HBM)
        sem = pl.BlockSpec(memory_space=pltpu.SEMAPHORE)
        args = [pltpu.with_memory_space_constraint(a, pltpu.HBM) for a in list(arrays) + lands]
        res = pl.pallas_call(
            body, name=name + "_start",
            in_specs=[hbm] * (2 * n_arr) + ([pl.BlockSpec(memory_space=pl.ANY)] if has_after else []),
            out_specs=[sem] * (2 * n_arr) + [hbm] * (2 * n_arr) + [pl.BlockSpec(memory_space=pltpu.VMEM)],
            out_shape=[pltpu.SemaphoreType.DMA(())] * (2 * n_arr) + [pltpu.HBM(a.shape, a.dtype) for a in args]
            + [jax.ShapeDtypeStruct((8, LANES), F32)],
            input_output_aliases={i: 2 * n_arr + i for i in range(2 * n_arr)},
            compiler_params=pltpu.CompilerParams(has_side_effects=pltpu.SideEffectType.DATAFLOW_SIDE_EFFECTING),
        )(*args, *([after] if has_after else []))
        self.sems, self.thru, self.token = res[:2 * n_arr], res[2 * n_arr:4 * n_arr], res[-1]

    def _wait_all(self, zones, sems):
        myself = (lax.axis_index("x"), lax.axis_index("y"), lax.axis_index("c"))
        for i in range(self.n_arr):
            many = zones[i].at[pl.ds(0, self.in_flight * self.rows[i])] if self.gather else zones[i]
            all_of_them = pltpu.make_async_remote_copy(
                src_ref=many, dst_ref=many, send_sem=sems[2 * i], recv_sem=sems[2 * i + 1],
                device_id=myself, device_id_type=pl.DeviceIdType.MESH)
            all_of_them.wait_recv()
            all_of_them.wait_send()

    def forward(self, after):
        n_arr = self.n_arr

        def body(*refs):
            zones, sems = refs[n_arr:2 * n_arr], refs[2 * n_arr:4 * n_arr]
            new_sems = refs[4 * n_arr + 1:6 * n_arr + 1]
            self._wait_all(zones, sems)
            _, peers = _mesh_place()
            sibling, _ = peers[0]
            for i in range(n_arr):
                for mask in self.SAME_CORE_MASKS:
                    _, pid = peers[mask - 1]
                    block = zones[i].at[pl.ds(pl.multiple_of(pid * self.rows[i], BF16_ROWS), self.rows[i])]
                    pltpu.make_async_remote_copy(
                        src_ref=block, dst_ref=block, send_sem=new_sems[2 * i], recv_sem=new_sems[2 * i + 1],
                        device_id=sibling, device_id_type=pl.DeviceIdType.MESH).start()

        hbm = pl.BlockSpec(memory_space=pltpu.HBM)
        sem = pl.BlockSpec(memory_space=pltpu.SEMAPHORE)
        res = pl.pallas_call(
            body, name=self.name + "_forward",
            in_specs=[hbm] * (2 * n_arr) + [sem] * (2 * n_arr) + [pl.BlockSpec(memory_space=pl.ANY)],
            out_specs=[sem] * (2 * n_arr) + [hbm] * (2 * n_arr),
            out_shape=[pltpu.SemaphoreType.DMA(())] * (2 * n_arr) + [pltpu.HBM(a.shape, a.dtype) for a in self.thru],
            input_output_aliases={i: 2 * n_arr + i for i in range(2 * n_arr)},
            compiler_params=pltpu.CompilerParams(has_side_effects=pltpu.SideEffectType.DATAFLOW_SIDE_EFFECTING),
        )(*self.thru, *self.sems, after)
        self.sems, self.thru = res[:2 * n_arr], res[2 * n_arr:]
        self.two_level, self.in_flight = False, len(self.SAME_CORE_MASKS)

    def wait(self, after):
        n_arr = self.n_arr
        if self.two_level:
            self.forward(after)

        def body(*refs):
            self._wait_all(refs[n_arr:2 * n_arr], refs[2 * n_arr:4 * n_arr])

        hbm = pl.BlockSpec(memory_space=pltpu.HBM)
        sem = pl.BlockSpec(memory_space=pltpu.SEMAPHORE)
        res = pl.pallas_call(
            body, name=self.name + "_wait",
            in_specs=[hbm] * (2 * n_arr) + [sem] * (2 * n_arr) + [pl.BlockSpec(memory_space=pl.ANY)],
            out_specs=[hbm] * (2 * n_arr), out_shape=[pltpu.HBM(a.shape, a.dtype) for a in self.thru],
            input_output_aliases={i: i for i in range(2 * n_arr)},
            compiler_params=pltpu.CompilerParams(has_side_effects=pltpu.SideEffectType.DATAFLOW_SIDE_EFFECTING),
        )(*self.thru, *self.sems, after)
        return res[:n_arr], res[n_arr:]


def _my_slot():
    me = 4 * lax.axis_index("x") + 2 * lax.axis_index("y") + lax.axis_index("c")
    return me.astype(jnp.int32).reshape(1)


def _place_own(gathered, block, me, *, name):
    r, c = block.shape

    def body(me_ref, b_ref, g_ref, o_ref):
        o_ref[...] = b_ref[...]

    return pl.pallas_call(
        body, name=name, out_shape=jax.ShapeDtypeStruct(gathered.shape, gathered.dtype),
        grid_spec=pltpu.PrefetchScalarGridSpec(
            num_scalar_prefetch=1, grid=(1,),
            in_specs=[pl.BlockSpec((r, c), lambda i, me_ref: (0, 0)), pl.BlockSpec(memory_space=pl.ANY)],
            out_specs=pl.BlockSpec((r, c), lambda i, me_ref: (me_ref[0], 0))),
        input_output_aliases={2: 0}, compiler_params=_params(("arbitrary",)),
    )(me, block, gathered)


def _elementwise_tiles(r, c):
    if r % 128 == 0:
        return 128, c
    return r, (256 if c % 256 == 0 else c)


def _adamw_math(g, w, m, v):
    nm = ADAM_B1 * m + (1.0 - ADAM_B1) * g
    nv = ADAM_B2 * v + (1.0 - ADAM_B2) * (g * g)
    m_hat = nm / (1.0 - ADAM_B1 ** ADAM_STEP)
    v_hat = nv / (1.0 - ADAM_B2 ** ADAM_STEP)
    return -ADAM_LR * (m_hat / (jnp.sqrt(v_hat) + ADAM_EPS) + ADAM_WD * w), nm, nv


def _sum_parts(me_ref, own_ref, p_ref, r):
    own = own_ref[...].astype(F32)
    g = None
    for d in range(N_DEV):
        k = jnp.bitwise_xor(me_ref[0], d)
        term = jnp.where(k == 0, own, p_ref[jnp.maximum(k, 1) - 1].astype(F32))
        g = term if g is None else g + term
    return g[0:r, :]


def _sum_adamw(me, sent, stride, parts, r, w=None, m=None, v=None, *, name):
    _, own_rows, cdim = parts.shape
    assert stride is None or stride == own_rows
    tc = 256 if cdim % 256 == 0 else cdim
    update = w is not None

    def body(me_ref, own_ref, p_ref, *refs):
        g = _sum_parts(me_ref, own_ref, p_ref, r)
        if update:
            w_ref, m_ref, v_ref, g_ref, d_ref, nm_ref, nv_ref = refs
            d_ref[...], nm_ref[...], nv_ref[...] = _adamw_math(g, w_ref[...], m_ref[...], v_ref[...])
        else:
            g_ref, = refs
        g_ref[...] = g

    blk = pl.BlockSpec((r, tc), lambda j, me_ref: (0, j))
    own_spec = pl.BlockSpec((own_rows, tc), (lambda j, me_ref: (0, j)) if stride is None else (lambda j, me_ref: (me_ref[0], j)))
    n_out = 4 if update else 1
    res = pl.pallas_call(
        body, name=name, out_shape=[jax.ShapeDtypeStruct((r, cdim), F32)] * n_out,
        grid_spec=pltpu.PrefetchScalarGridSpec(
            num_scalar_prefetch=1, grid=(cdim // tc,),
            in_specs=[own_spec, pl.BlockSpec((N_DEV - 1, own_rows, tc), lambda j, me_ref: (0, 0, j))]
            + ([blk] * 3 if update else []),
            out_specs=[blk] * n_out),
        compiler_params=_params(("parallel",)),
    )(me, sent, parts, *((w, m, v) if update else ()))
    return list(res)


def _adamw(g, w, m, v, *, name):
    r, cdim = w.shape
    tr, tc = _elementwise_tiles(r, cdim)

    def body(g_ref, w_ref, m_ref, v_ref, d_ref, nm_ref, nv_ref):
        d_ref[...], nm_ref[...], nv_ref[...] = _adamw_math(g_ref[...], w_ref[...], m_ref[...], v_ref[...])

    blk = pl.BlockSpec((tr, tc), lambda i, j: (i, j))
    return list(pl.pallas_call(
        body, name=name, grid=(r // tr, cdim // tc), in_specs=[blk] * 4,
        out_specs=[blk] * 3, out_shape=[jax.ShapeDtypeStruct((r, cdim), F32)] * 3,
        compiler_params=_params(("parallel", "parallel")),
    )(g, w, m, v))


SHARD_ROWS_P = {n: (FF_SHARD_P if 'ffn' in n else IN_SHARD_P if n == 'w_in' else None) for n in SHARDED}


def _to_exchange_layout(name, shard):
    t = shard.T if SHARD_AXIS[name] == 1 else shard
    pad = SHARD_ROWS_P[name]
    return t if pad is None else jnp.pad(t, ((0, pad - t.shape[0]), (0, 0)))


def _expand_w_in(wt):
    wt = wt.reshape(N_DEV, IN_SHARD_P, D)[:, :IN_SHARD].reshape(IN_W, D)
    o = Q_RANK + KV_RANK
    kr1, kr2 = wt[o:o + ROPE // 2], wt[o + ROPE // 2:o + ROPE]
    z = jnp.zeros((LANES - ROPE, D), wt.dtype)
    return jnp.concatenate([wt[:o], wt[o + ROPE:], kr1, kr2, z, -kr2, kr1, z], axis=0)


def _expand_w_uq(wt):
    w = wt.reshape(H, QK, Q_RANK)
    z = jnp.zeros((H, LANES - ROPE, Q_RANK), w.dtype)
    q1, q2 = w[:, NOPE:NOPE + ROPE // 2], w[:, NOPE + ROPE // 2:]
    return jnp.concatenate([w[:, :NOPE].reshape(H * NOPE, Q_RANK),
                            jnp.concatenate([q1, q2, z], axis=1).reshape(H * LANES, Q_RANK),
                            jnp.concatenate([-q2, q1, z], axis=1).reshape(H * LANES, Q_RANK)], axis=0)


def _layout_qk_gain(g):
    g = g.reshape(QK)
    g1, g2, z = g[NOPE:NOPE + ROPE // 2], g[NOPE + ROPE // 2:], jnp.zeros((LANES - ROPE,), g.dtype)
    return jnp.stack([g[:NOPE], jnp.concatenate([g1, g2, z]), jnp.concatenate([g2, g1, z])])


def _rep16(a):
    return jnp.repeat(a, SSM_GRP, axis=0)


def _layout_ssm_in(a_re, a_im, log_dt, b_re, b_im):
    b_r = jnp.transpose(b_re, (0, 2, 1)).reshape(SSM_G * SSM_GRP, SSM_P)
    b_i = jnp.transpose(b_im, (0, 2, 1)).reshape(SSM_G * SSM_GRP, SSM_P)
    ldt = jnp.broadcast_to(log_dt.reshape(SSM_G, 1), (SSM_G, SSM_P))
    return _rep16(a_re), _rep16(a_im), _rep16(ldt), b_r, b_i


def _block_diag_b(bb):
    eye = jnp.eye(SSM_PACK, dtype=bb.dtype)
    b5 = bb.reshape(SSM_G // SSM_PACK, SSM_PACK, SSM_GRP, 1, SSM_P) * eye[None, :, None, :, None]
    return b5.reshape(SSM_G // SSM_PACK, SSM_PACK * SSM_GRP, SSM_PACK * SSM_P)


def _block_diag_c(cc):
    eye = jnp.eye(SSM_PACK, dtype=cc.dtype)
    c5 = jnp.transpose(cc, (0, 2, 1)).reshape(SSM_G // SSM_PACK, SSM_PACK, SSM_P, 1, SSM_GRP) * eye[None, :, None, :, None]
    return c5.reshape(SSM_G // SSM_PACK, SSM_PACK * SSM_P, SSM_PACK * SSM_GRP)


def _time_perm(a, inverse=False):
    s, w = a.shape
    c = SCAN_CHUNKS
    if inverse:
        return jnp.transpose(a.reshape(s // c, c, w), (1, 0, 2)).reshape(s, w)
    return jnp.transpose(a.reshape(c, s // c, w), (1, 0, 2)).reshape(s, w)


class _Weights:
    def __init__(self, groups=(), landed=None, me=None):
        self.groups, self.landed, self.me = list(groups), dict(landed or {}), me

    def get(self, name, after):
        if name not in self.landed:
            names, exchange = next(g for g in self.groups if name in g[0])
            for n, block, gathered in zip(names, *exchange.wait(after)):
                self.landed[n] = _place_own(gathered, block, self.me, name="place_" + n)
        return self.landed[name]

    def __getitem__(self, name):
        return self.landed[name]

    def prefetch(self, name, after):
        for names, exchange in self.groups:
            if name in names and exchange.two_level:
                exchange.forward(after)


def _ffn_gate_up(h, w_gt, w_ut, *, name, tm=512, tn=1408):
    s, k = h.shape
    n = w_gt.shape[0]
    tm, tn = min(tm, s), _tile(n, tn)
    dims = (((1,), (1,)), ((), ()))

    def body(h_ref, wg_ref, wu_ref, g_ref, u_ref, a_ref):
        hb = h_ref[...].astype(BF16)
        gate = lax.dot_general(hb, wg_ref[...], dims, preferred_element_type=F32)
        up = lax.dot_general(hb, wu_ref[...], dims, preferred_element_type=F32)
        g_ref[...] = gate.astype(BF16)
        u_ref[...] = up.astype(BF16)
        a_ref[...] = _f_swiglu(gate, up)

    w_spec = pl.BlockSpec((tn, k), lambda j, i: (j, 0))
    o_spec = pl.BlockSpec((tm, tn), lambda j, i: (i, j))
    return pl.pallas_call(
        body, name=name, grid=(n // tn, s // tm), in_specs=[pl.BlockSpec((tm, k), lambda j, i: (i, 0)), w_spec, w_spec],
        out_specs=[o_spec] * 3, out_shape=[jax.ShapeDtypeStruct((s, n), BF16)] * 3,
        compiler_params=_params(("parallel", "parallel")),
    )(h, w_gt, w_ut)


def _ffn_dgate_dup(dx_out, w_d, gate, up, *, name, tm=512, tn=1408, deps=()):
    s, k = dx_out.shape
    n = w_d.shape[0]
    tm, tn = min(tm, s), _tile(n, tn)
    deps = [d for d in deps if d is not None]

    def body(dx_ref, wd_ref, g_ref, u_ref, *refs):
        dg_ref, du_ref = refs[len(deps):]
        dact = 0.5 * lax.dot_general(dx_ref[...].astype(BF16), wd_ref[...], (((1,), (1,)), ((), ())),
                                     preferred_element_type=F32)
        _, vjp = jax.vjp(_f_swiglu, g_ref[...].astype(F32), u_ref[...].astype(F32))
        dgate, dup = vjp(dact.astype(BF16))
        dg_ref[...] = dgate.astype(BF16)
        du_ref[...] = dup.astype(BF16)

    o_spec = pl.BlockSpec((tm, tn), lambda j, i: (i, j))
    return pl.pallas_call(
        body, name=name, grid=(n // tn, s // tm),
        in_specs=[pl.BlockSpec((tm, k), lambda j, i: (i, 0)), pl.BlockSpec((tn, k), lambda j, i: (j, 0)), o_spec, o_spec]
        + [pl.BlockSpec(d.shape, lambda j, i: (0, 0)) for d in deps],
        out_specs=[o_spec] * 2, out_shape=[jax.ShapeDtypeStruct((s, n), BF16)] * 2,
        compiler_params=_params(("parallel", "parallel")),
    )(dx_out, w_d, gate, up, *deps)


def _ffn_dh(dgate, dup, w_gt, w_ut, *, name, tm=512):
    s, k = dgate.shape
    n = w_gt.shape[1]
    tm = min(tm, s)

    def body(dg_ref, du_ref, wg_ref, wu_ref, o_ref):
        o_ref[...] = (jnp.dot(dg_ref[...], wg_ref[...], preferred_element_type=F32)
                      + jnp.dot(du_ref[...], wu_ref[...], preferred_element_type=F32)).astype(o_ref.dtype)

    a_spec = pl.BlockSpec((tm, k), lambda i: (i, 0))
    w_spec = pl.BlockSpec((k, n), lambda i: (0, 0))
    return pl.pallas_call(
        body, name=name, grid=(s // tm,), in_specs=[a_spec, a_spec, w_spec, w_spec],
        out_specs=pl.BlockSpec((tm, n), lambda i: (i, 0)), out_shape=jax.ShapeDtypeStruct((s, n), BF16),
        compiler_params=_params(("parallel",)),
    )(dgate, dup, w_gt, w_ut)


def _ffn_fwd(x, g, wc, tag, deps=(), prefetch=()):
    h = _rowwise(_f_norm, [x], [g], [(D, BF16)], name=tag + "_norm", deps=deps)[0]
    gate, up, act = _ffn_gate_up(h, wc.get(tag + '_w_gate', h), wc[tag + '_w_up'], name=tag + "_gate_up")
    for later in (tag + '_w_down',) + tuple(prefetch):
        wc.prefetch(later, gate)
    x_out = _mm(act, wc.get(tag + '_w_down', act), res=x, scale=0.5, name=tag + "_down")
    return x_out, (h, gate, up, act)


def _ffn_bwd(x, g, wc, saved, dx_out, tag, send, deps=()):
    h, gate, up, act = saved
    w_gt, w_ut, w_d = (wc.get(tag + n, h) for n in ('_w_gate', '_w_up', '_w_down'))
    d_d = _mm(act, dx_out, ta=True, scale=0.5, out_dtype=GRAD_DTYPE, name=tag + "_dwdown", deps=deps)
    token = send({tag + '_w_down': d_d})
    dgate, dup = _ffn_dgate_dup(dx_out, w_d, gate, up, name=tag + "_dgate_dup", deps=[token])
    d_gt = _mm(dgate, h, ta=True, out_dtype=GRAD_DTYPE, name=tag + "_dwgate")
    token = send({tag + '_w_gate': d_gt})
    d_ut = _mm(dup, h, ta=True, out_dtype=GRAD_DTYPE, name=tag + "_dwup", deps=[token])
    token = send({tag + '_w_up': d_ut})
    dh = _ffn_dh(dgate, dup, w_gt, w_ut, name=tag + "_dh")
    dx, dg = _rowwise_bwd(_f_norm, [x], [g], [dh], row_grads={0: F32}, const_grads=[0], adds={0: dx_out},
                          name=tag + "_norm_bwd", deps=[token])
    return dx, dg


def _local_step(x, mem, cos, sin, target, wc, ws, send, deps=(), send_small=None):
    gs = {}

    x1, sv1 = _ffn_fwd(x, ws['ffn1_norm'], wc, "ffn1", deps=deps, prefetch=('w_in',))

    h2 = _rowwise(_f_norm, [x1], [ws['mix_norm']], [(D, BF16)], name="mix_norm")[0]
    w_in_raw, w_uq_raw = wc.get('w_in', h2), wc.get('mla_w_uq', h2)
    w_in_e = _expand_w_in(w_in_raw)
    w_uq_e = _expand_w_uq(w_uq_raw)
    proj = _mm(h2, w_in_e, tb=True, name="w_in")
    c_q, c_kv = _rowwise(_f_prep1, [proj], [ws['q_norm'], ws['kv_norm']], [(Q_RANK, BF16), (KV_RANK, BF16)], name="mla_prep1")
    qall = _mm(c_q, w_uq_e, tb=True, out_dtype=BF16, name="w_uq")
    kv = _mm(c_kv, wc['mla_w_ukv'], tb=True, out_dtype=BF16, name="w_ukv")
    q, k, v = _prep2_fwd(qall, kv, proj, cos, sin, ws['qk_gq'], ws['qk_gk'])
    o_mla, lse = _attn_fwd(q, k, v)
    wc.prefetch('ffn2_w_gate', lse)

    u = proj[:, Q_RANK + KV_RANK:Q_RANK + KV_RANK + SSM_W]
    u_p = _time_perm(u)
    disc_in = [ws['ssm_lr'], ws['ssm_li'], ws['ssm_ldt'], ws['ssm_br'], ws['ssm_bi']]
    ar16, ai16, bbr, bbi = _rowwise(_f_disc, disc_in, [], [(SSM_P, F32)] * 4, name="s5_disc")
    a_r = ar16[::SSM_GRP].reshape(1, SSM_N)
    a_i = ai16[::SSM_GRP].reshape(1, SSM_N)
    bblk_r, bblk_i = _block_diag_b(bbr).astype(BF16), _block_diag_b(bbi).astype(BF16)
    cblk_r, cblk_i = _block_diag_c(ws['ssm_cr']).astype(BF16), _block_diag_c(-ws['ssm_ci']).astype(BF16)
    xr, xi, yc = _s5_scan(u_p, bblk_r, bblk_i, a_r, a_i, reverse=False, tb=False, readout=(cblk_r, cblk_i),
                          name="s5_scan_fwd")
    g_p = _rowwise(_f_s5_gelu, [yc, u_p], [ws['ssm_d']], [(SSM_W, F32)], name="s5_gelu")[0]
    z_p = _mm(g_p, wc['ssm_w_glu'], name="s5_glu")
    g_t, z_t = _time_perm(g_p, inverse=True), _time_perm(z_p, inverse=True)
    on_consts = [ws['ssm_b_glu'], ws['out_norm_mla'], ws['out_norm_ssm']]
    ycat = _rowwise(_f_outnorm, [o_mla, g_t, z_t], on_consts, [(D, BF16)], name="out_norm")[0]
    x2 = _mm(ycat, wc['w_o'], res=x1, name="w_o")

    hx = _rowwise(_f_norm, [x2], [ws['xattn_norm']], [(D, BF16)], name="xattn_norm")[0]
    xq = _mm(hx, wc['xattn_w_q'], name="xattn_q")
    mn = _rowwise(_f_norm, [mem], [ws['mem_norm']], [(D, BF16)], name="mem_norm")[0]
    kvm = _mm(mn, wc['xattn_w_kv'], name="xattn_kv")
    xkn, xv = _rowwise(_f_memk, [kvm], [ws['xattn_k_norm']], [(H * XH, BF16), (H * XH, BF16)], name="xattn_knorm")
    xo = _xattn_fwd(xq, xkn, xv, ws['xattn_q_norm'])
    x3 = _mm(xo, wc['xattn_w_o'], tb=True, res=x2, name="xattn_o")

    x4, sv2 = _ffn_fwd(x3, ws['ffn2_norm'], wc, "ffn2")

    def f_loss(yb, tb):
        err = yb - tb
        return err * (1.0 / D), jnp.broadcast_to(jnp.sum(jnp.sum(err * err, axis=1, keepdims=True), axis=0, keepdims=True) * (0.5 / D), (1, LANES))

    dx4, loss = _rowwise(f_loss, [x4, target], [], [(D, F32)], [(1, LANES)], name="loss")

    dx3, gs['ffn2_norm'] = _ffn_bwd(x3, ws['ffn2_norm'], wc, sv2, dx4, "ffn2", send)

    dxo = _mm(dx3, wc['xattn_w_o'], out_dtype=BF16, name="xattn_o_dx")
    send({'xattn_w_o': _mm(dx3, xo, ta=True, out_dtype=GRAD_DTYPE, name="xattn_o_dw")})
    dxq, dxkn, dxv, gs['xattn_q_norm'] = _xattn_bwd(xq, xkn, xv, ws['xattn_q_norm'], dxo)
    dkvm, gs['xattn_k_norm'] = _rowwise_bwd(_f_memk, [kvm], [ws['xattn_k_norm']], [dxkn, dxv], row_grads={0: BF16},
                                            const_grads=[0], name="xattn_knorm_bwd")
    send({'xattn_w_kv': _mm(mn, dkvm, ta=True, out_dtype=GRAD_DTYPE, name="xattn_kv_dw")})
    dmn = _mm(dkvm, wc['xattn_w_kv'], tb=True, out_dtype=BF16, name="xattn_kv_dx")
    gs['mem_norm'] = _rowwise_bwd(_f_norm, [mem], [ws['mem_norm']], [dmn], row_grads={}, const_grads=[0], name="mem_norm_bwd")[0]
    token = send({'xattn_w_q': _mm(hx, dxq, ta=True, out_dtype=GRAD_DTYPE, name="xattn_q_dw")})
    dhx = _mm(dxq, wc['xattn_w_q'], tb=True, out_dtype=BF16, name="xattn_q_dx")
    dx2, gs['xattn_norm'] = _rowwise_bwd(_f_norm, [x2], [ws['xattn_norm']], [dhx], row_grads={0: F32}, const_grads=[0],
                                         adds={0: dx3}, name="xattn_norm_bwd", deps=[token])

    dycat = _mm(dx2, wc['w_o'], tb=True, out_dtype=BF16, name="w_o_dx")
    send({'w_o': _mm(ycat, dx2, ta=True, out_dtype=GRAD_DTYPE, name="w_o_dw")})
    do_mla, dg_t, dz_t, gs['ssm_b_glu'], gs['out_norm_mla'], gs['out_norm_ssm'] = _rowwise_bwd(
        _f_outnorm, [o_mla, g_t, z_t], on_consts, [dycat], row_grads={0: F32, 1: F32, 2: BF16}, const_grads=[0, 1, 2],
        name="out_norm_bwd")

    dz_p, dg_p = _time_perm(dz_t), _time_perm(dg_t)
    send({'ssm_w_glu': _mm(g_p, dz_p, ta=True, out_dtype=GRAD_DTYPE, name="s5_glu_dw")})
    dg_p = _mm(dz_p, wc['ssm_w_glu'], tb=True, res=dg_p, name="s5_glu_dx")
    dyc, du_d, gs['ssm_d'] = _rowwise_bwd(_f_s5_gelu, [yc, u_p], [ws['ssm_d']], [dg_p], row_grads={0: BF16, 1: F32},
                                          const_grads=[0], name="s5_gelu_bwd")
    lam_r, lam_i = _s5_scan(dyc, cblk_r, cblk_i, a_r, -a_i, reverse=True, tb=True, name="s5_scan_bwd")
    du_p, d_bblk_r, d_bblk_i, d_cblk_r, d_cblk_i, d_ar, d_ai = _s5_grads(lam_r, lam_i, xr, xi, u_p, dyc, du_d,
                                                                        bblk_r, bblk_i)
    du = _time_perm(du_p, inverse=True)
    gs['ssm_cr'] = jax.linear_transpose(_block_diag_c, ws['ssm_cr'])(d_cblk_r)[0]
    gs['ssm_ci'] = -jax.linear_transpose(_block_diag_c, ws['ssm_ci'])(d_cblk_i)[0]
    d_bbr = jax.linear_transpose(_block_diag_b, bbr)(d_bblk_r)[0]
    d_bbi = jax.linear_transpose(_block_diag_b, bbi)(d_bblk_i)[0]
    d_ar16 = jnp.zeros((SSM_G * SSM_GRP, SSM_P), F32).at[::SSM_GRP].set(d_ar.reshape(SSM_G, SSM_P))
    d_ai16 = jnp.zeros((SSM_G * SSM_GRP, SSM_P), F32).at[::SSM_GRP].set(d_ai.reshape(SSM_G, SSM_P))
    gs['ssm_lr'], gs['ssm_li'], gs['ssm_ldt'], gs['ssm_br'], gs['ssm_bi'] = _rowwise_bwd(
        _f_disc, disc_in, [], [d_ar16, d_ai16, d_bbr, d_bbi], row_grads={i: F32 for i in range(5)}, const_grads=[],
        name="s5_disc_bwd")

    delta, do_b = _rowwise(_f_delta, [do_mla, o_mla], [], [(H * LANES, F32), (H * VD, BF16)], name="mla_delta")
    dq, dk, dv = _attn_bwd(q, k, v, do_b, lse, delta)
    dqall, dkv, dkr, dkrs, gs['qk_gq'], gs['qk_gk'] = _prep2_bwd(qall, kv, proj, cos, sin, ws['qk_gq'], ws['qk_gk'], dq, dk, dv)
    d_w_uq_e = _mm(dqall, c_q, ta=True, name="w_uq_dw")
    send({'mla_w_uq': jax.linear_transpose(_expand_w_uq, jax.ShapeDtypeStruct(w_uq_raw.shape, F32))(d_w_uq_e)[0]})
    dc_q = _mm(dqall, w_uq_e, out_dtype=BF16, name="w_uq_dx")
    send({'mla_w_ukv': _mm(dkv, c_kv, ta=True, out_dtype=GRAD_DTYPE, name="w_ukv_dw")})
    dc_kv = _mm(dkv, wc['mla_w_ukv'], out_dtype=BF16, name="w_ukv_dx")

    def f_prep1_bwd(pb, dcq, dckv, dub, dkrb, dkrsb, gq, gkv):
        _, vjp = jax.vjp(_f_prep1, pb[:, :Q_RANK + KV_RANK], gq, gkv)
        dpa, dgq, dgkv = vjp((dcq.astype(BF16), dckv.astype(BF16)))
        return jnp.concatenate([dpa, dub, dkrb, dkrsb], axis=-1), dgq, dgkv

    dproj, gs['q_norm'], gs['kv_norm'] = _rowwise(
        f_prep1_bwd, [proj, dc_q, dc_kv, du, dkr, dkrs], [ws['q_norm'], ws['kv_norm']], [(IN_WP, BF16)],
        [(1, Q_RANK), (1, KV_RANK)], name="mla_prep1_bwd")
    d_w_in_e = _mm(dproj, h2, ta=True, name="w_in_dw")
    token = send({'w_in': jax.linear_transpose(_expand_w_in, jax.ShapeDtypeStruct(w_in_raw.shape, F32))(d_w_in_e)[0]})
    dh2 = _mm(dproj, w_in_e, out_dtype=BF16, name="w_in_dx")
    dx1, gs['mix_norm'] = _rowwise_bwd(_f_norm, [x1], [ws['mix_norm']], [dh2], row_grads={0: F32}, const_grads=[0],
                                       adds={0: dx2}, name="mix_norm_bwd", deps=[token])

    token = send_small(gs, loss) if send_small is not None else None
    dx0, gs['ffn1_norm'] = _ffn_bwd(x, ws['ffn1_norm'], wc, sv1, dx1, "ffn1", send, deps=[token])
    return loss, dx0, gs


def _prep2_rows(qall, kv, proj, cos, sin):
    return [qall, kv, (proj, KR_BLOCK, LANES), (proj, KR_BLOCK + 1, LANES), cos, sin]


def _prep2_fwd(qall, kv, proj, cos, sin, gq, gk):
    return _rowwise(_f_prep2, _prep2_rows(qall, kv, proj, cos, sin), [gq, gk],
                    [(H * HQ, BF16), (H * HQ, BF16), (H * VD, BF16)], ts=256, name="mla_prep2")


def _prep2_bwd(qall, kv, proj, cos, sin, gq, gk, dq, dk, dv):
    return _rowwise_bwd(_f_prep2, _prep2_rows(qall, kv, proj, cos, sin), [gq, gk], [dq, dk, dv],
                        row_grads={0: BF16, 1: BF16, 2: F32, 3: F32}, const_grads=[0, 1], ts=256, name="mla_prep2_bwd")


def _rope_tables(pos):
    half = ROPE // 2
    inv = ROPE_THETA ** (-jnp.arange(half, dtype=F32) / half)
    ang = pos.astype(F32)[:, None] * inv[None, :]
    z = jnp.zeros((pos.shape[0], LANES - ROPE), F32)
    cos, sin = jnp.cos(ang), jnp.sin(ang)
    return jnp.concatenate([cos, cos, z], axis=-1), jnp.concatenate([sin, sin, z], axis=-1)


def _small_layout(p):
    lr, li, ldt, br, bi = _layout_ssm_in(p['ssm_a_re'], p['ssm_a_im'], p['ssm_log_dt'], p['ssm_b_re'], p['ssm_b_im'])
    return {
        'ffn1_norm': p['ffn1_norm'].reshape(1, D), 'mix_norm': p['mix_norm'].reshape(1, D),
        'q_norm': p['mla_q_norm'].reshape(1, Q_RANK), 'kv_norm': p['mla_kv_norm'].reshape(1, KV_RANK),
        'qk_gq': _layout_qk_gain(p['mla_qk_norm_q']), 'qk_gk': _layout_qk_gain(p['mla_qk_norm_k']),
        'ssm_lr': lr, 'ssm_li': li, 'ssm_ldt': ldt, 'ssm_br': br, 'ssm_bi': bi,
        'ssm_cr': p['ssm_c_re'], 'ssm_ci': p['ssm_c_im'], 'ssm_d': p['ssm_d'].reshape(1, SSM_W),
        'ssm_b_glu': p['ssm_b_glu'].reshape(1, SSM_W),
        'out_norm_mla': p['out_norm_mla'].reshape(1, SSM_W), 'out_norm_ssm': p['out_norm_ssm'].reshape(1, SSM_W),
        'xattn_norm': p['xattn_norm'].reshape(1, D), 'mem_norm': p['mem_norm'].reshape(1, D),
        'xattn_q_norm': p['xattn_q_norm'].reshape(1, XH), 'xattn_k_norm': p['xattn_k_norm'].reshape(1, XH),
        'ffn2_norm': p['ffn2_norm'].reshape(1, D),
    }


def _pack(arrs, rows):
    flat = jnp.concatenate([a.reshape(-1) for a in arrs])
    return jnp.pad(flat, (0, rows * D - flat.shape[0])).reshape(rows, D)


def _unpack(flat, shapes):
    flat = flat.reshape(-1)
    out, off = [], 0
    for sh in shapes:
        n = int(np.prod(sh))
        out.append(flat[off:off + n].reshape(sh))
        off += n
    return out


def kernel(x, mem, positions, ffn1_norm, ffn1_w_gate, ffn1_w_up, ffn1_w_down, mix_norm, w_in, mla_q_norm, mla_w_uq, mla_kv_norm, mla_w_ukv, mla_qk_norm_q, mla_qk_norm_k, ssm_a_re, ssm_a_im, ssm_log_dt, ssm_b_re, ssm_b_im, ssm_c_re, ssm_c_im, ssm_d, ssm_w_glu, ssm_b_glu, out_norm_mla, out_norm_ssm, w_o, xattn_norm, mem_norm, xattn_w_q, xattn_w_kv, xattn_q_norm, xattn_k_norm, xattn_w_o, ffn2_norm, ffn2_w_gate, ffn2_w_up, ffn2_w_down, loss_target, m_ffn1_norm, m_ffn1_w_gate, m_ffn1_w_up, m_ffn1_w_down, m_mix_norm, m_w_in, m_mla_q_norm, m_mla_w_uq, m_mla_kv_norm, m_mla_w_ukv, m_mla_qk_norm_q, m_mla_qk_norm_k, m_ssm_a_re, m_ssm_a_im, m_ssm_log_dt, m_ssm_b_re, m_ssm_b_im, m_ssm_c_re, m_ssm_c_im, m_ssm_d, m_ssm_w_glu, m_ssm_b_glu, m_out_norm_mla, m_out_norm_ssm, m_w_o, m_xattn_norm, m_mem_norm, m_xattn_w_q, m_xattn_w_kv, m_xattn_q_norm, m_xattn_k_norm, m_xattn_w_o, m_ffn2_norm, m_ffn2_w_gate, m_ffn2_w_up, m_ffn2_w_down, v_ffn1_norm, v_ffn1_w_gate, v_ffn1_w_up, v_ffn1_w_down, v_mix_norm, v_w_in, v_mla_q_norm, v_mla_w_uq, v_mla_kv_norm, v_mla_w_ukv, v_mla_qk_norm_q, v_mla_qk_norm_k, v_ssm_a_re, v_ssm_a_im, v_ssm_log_dt, v_ssm_b_re, v_ssm_b_im, v_ssm_c_re, v_ssm_c_im, v_ssm_d, v_ssm_w_glu, v_ssm_b_glu, v_out_norm_mla, v_out_norm_ssm, v_w_o, v_xattn_norm, v_mem_norm, v_xattn_w_q, v_xattn_w_kv, v_xattn_q_norm, v_xattn_k_norm, v_xattn_w_o, v_ffn2_norm, v_ffn2_w_gate, v_ffn2_w_up, v_ffn2_w_down):
    args = dict(locals())
    w = {n: args[n] for n in WEIGHTS}
    mom = {n: args['m_' + n] for n in WEIGHTS}
    var = {n: args['v_' + n] for n in WEIGHTS}
    return _step(x, mem, positions, loss_target, w, mom, var)


GATHER_GROUPS = [('ffn1_gu', ['ffn1_w_gate', 'ffn1_w_up']), ('ffn1_down', ['ffn1_w_down']),
                 ('mix', ['w_in', 'mla_w_uq', 'mla_w_ukv', 'ssm_w_glu', 'w_o', 'xattn_w_q', 'xattn_w_kv', 'xattn_w_o']),
                 ('ffn2', ['ffn2_w_gate', 'ffn2_w_up', 'ffn2_w_down'])]
SCATTER_GROUPS = [('ffn2_down', ['ffn2_w_down']), ('ffn2_gate', ['ffn2_w_gate']), ('ffn2_up', ['ffn2_w_up']),
                  ('xattn', ['xattn_w_o', 'xattn_w_kv', 'xattn_w_q']),
                  ('mix', ['w_o', 'ssm_w_glu', 'mla_w_uq', 'mla_w_ukv', 'w_in']),
                  ('ffn1_down', ['ffn1_w_down']), ('ffn1_gate', ['ffn1_w_gate']), ('ffn1_up', ['ffn1_w_up'])]


def _step(x, mem, positions, loss_target, w, mom, var):
    blocks = {n: _to_exchange_layout(n, w[n][0]).astype(BF16) for n in SHARDED}
    gathers, token = [], None
    for tag, names in GATHER_GROUPS:
        ex = _Exchange([blocks[n] for n in names], [blocks[n].shape[0] for n in names], gather=True,
                       name="gather_" + tag, after=token, two_level=True)
        gathers.append((names, ex))
        token = ex.token
    me = _my_slot()
    wc = _Weights(gathers, me=me)

    rows = {n: (blocks[n].shape[0], blocks[n].shape[0]) for n in SHARDED}
    ready, scatters = {}, []

    def send(grads):
        ready.update({n: g.astype(GRAD_DTYPE) for n, g in grads.items()})
        for tag, names in SCATTER_GROUPS:
            if all(n in ready for n in names) and not any(t == tag for t, _, _ in scatters):
                ex = _Exchange([ready[n] for n in names], [rows[n] for n in names], gather=False, name="scatter_" + tag)
                scatters.append((tag, names, ex))
                return ex.token
        return None

    small = {n: w[n][0] for n in SMALL}
    small_shapes = [small[n].shape for n in SMALL]
    n_small = sum(int(np.prod(sh)) for sh in small_shapes) + 1
    rows_small = -(-n_small // (8 * D)) * 8
    small_sent = []

    def send_small(gs, loss):
        known = dict(gs, ffn1_norm=jnp.zeros((1, D), F32))
        g_small = jax.linear_transpose(_small_layout, {n: jax.ShapeDtypeStruct(small[n].shape, F32) for n in SMALL})(known)[0]
        pack = _pack([g_small[n] for n in SMALL] + [loss[0, :1]], rows_small)
        small_sent.append(_Exchange([pack], [(None, rows_small)], gather=False, name="scatter_small"))
        return small_sent[0].token

    ws = _small_layout(small)
    cos, sin = _rope_tables(positions[0])
    loss, dx, gs = _local_step(x[0], mem[0], cos, sin, loss_target[0], wc, ws, send, deps=[token], send_small=send_small)
    pad8 = lambda a: jnp.pad(a.reshape(1, D), ((0, 7), (0, 0)))
    last_ex = _Exchange([pad8(gs['ffn1_norm'])], [(None, 8)], gather=False, name="scatter_last")

    out, after = {}, dx
    for _, names, ex in scatters:
        for n, sent, p in zip(names, *ex.wait(after)):
            r = w[n][0].shape[SHARD_AXIS[n]]
            if SHARD_AXIS[n] == 0:
                out[n] = _sum_adamw(me, sent, rows[n][0], p, r, w[n][0], mom[n][0], var[n][0], name="adamw_" + n)
            else:
                g = _sum_adamw(me, sent, rows[n][0], p, r, name="sum_" + n)[0].T
                out[n] = [g] + _adamw(g, w[n][0], mom[n][0], var[n][0], name="adamw_" + n)
        after = out[names[-1]][1]
    state = [_pack([t[n][0] for n in SMALL], rows_small) for t in (w, mom, var)]
    sent, p = small_sent[0].wait(after)
    small_out = _sum_adamw(me, sent[0], None, p[0], rows_small, *state, name="adamw_small")
    loss_total = small_out[0].reshape(-1)[n_small - 1]
    for n, vals in zip(SMALL, zip(*[_unpack(flat, small_shapes) for flat in small_out])):
        out[n] = vals
    sent, p = last_ex.wait(small_out[1])
    last_out = _sum_adamw(me, sent[0], None, p[0], 8, *[pad8(t['ffn1_norm'][0]) for t in (w, mom, var)], name="adamw_last")
    out['ffn1_norm'] = [o[0] for o in last_out]
    outs = [out[n][i][None] for i in range(4) for n in WEIGHTS]
    return (loss_total, dx[None], *outs)
```

```python
import math

import jax
import jax.numpy as jnp
import numpy as np
from jax import lax
from jax.experimental import pallas as pl
from jax.experimental.pallas import tpu as pltpu

F32 = jnp.float32
BF16 = jnp.bfloat16

N_DEV = 8
D = 1024
D_FF = 2752
D_FFP = 2816
MEM_LEN = 256
H = 4
Q_RANK, KV_RANK, NOPE, ROPE, VD = 384, 256, 128, 64, 128
QK = NOPE + ROPE
HQ = 2 * 128
SSM_W, SSM_G, SSM_GRP, SSM_P = 512, 32, 16, 64
SSM_N = SSM_G * SSM_P
SSM_PACK = 8
IN_W = 1216
IN_WP = 1408
XH = 128
EPS = 1e-6
LN2 = math.log(2.0)
ROPE_THETA = 10000.0
SCAN_CHUNKS = 8
SCAN_UNROLL = 8
ADAM_LR, ADAM_B1, ADAM_B2, ADAM_EPS, ADAM_WD, ADAM_STEP = 0.001, 0.9, 0.999, 1e-08, 0.01, 10

VMEM_LIMIT = 56 * 1024 * 1024
ACC_BYTES = 6 * 1024 * 1024
LANES = 128
BF16_ROWS = 16
GRAD_DTYPE = BF16
FF_SHARD = D_FF // N_DEV
FF_SHARD_P = 352
IN_SHARD = IN_W // N_DEV
IN_SHARD_P = 160

WEIGHTS = ['ffn1_norm', 'ffn1_w_gate', 'ffn1_w_up', 'ffn1_w_down', 'mix_norm', 'w_in', 'mla_q_norm', 'mla_w_uq',
           'mla_kv_norm', 'mla_w_ukv', 'mla_qk_norm_q', 'mla_qk_norm_k', 'ssm_a_re', 'ssm_a_im', 'ssm_log_dt',
           'ssm_b_re', 'ssm_b_im', 'ssm_c_re', 'ssm_c_im', 'ssm_d', 'ssm_w_glu', 'ssm_b_glu', 'out_norm_mla',
           'out_norm_ssm', 'w_o', 'xattn_norm', 'mem_norm', 'xattn_w_q', 'xattn_w_kv', 'xattn_q_norm',
           'xattn_k_norm', 'xattn_w_o', 'ffn2_norm', 'ffn2_w_gate', 'ffn2_w_up', 'ffn2_w_down']
SHARD_AXIS = {'ffn1_w_gate': 1, 'ffn1_w_up': 1, 'ffn1_w_down': 0, 'w_in': 1, 'mla_w_uq': 1, 'mla_w_ukv': 1,
              'ssm_w_glu': 0, 'w_o': 0, 'xattn_w_q': 0, 'xattn_w_kv': 0, 'xattn_w_o': 1,
              'ffn2_w_gate': 1, 'ffn2_w_up': 1, 'ffn2_w_down': 0}
SHARDED = [n for n in WEIGHTS if n in SHARD_AXIS]
SMALL = [n for n in WEIGHTS if n not in SHARD_AXIS]


def _params(sem=None):
    return pltpu.CompilerParams(dimension_semantics=sem, vmem_limit_bytes=VMEM_LIMIT)


def _tile(n, cap):
    if n <= cap:
        return n
    best = n
    for t in range(LANES, cap + 1, LANES):
        if n % t == 0:
            best = t
    return best


def _mm(a, b, *, ta=False, tb=False, out_dtype=F32, res=None, scale=1.0, name, tm_cap=512, tn_cap=1408, tk_cap=2816,
        deps=()):
    m, k = (a.shape[1], a.shape[0]) if ta else a.shape
    k2, n = (b.shape[1], b.shape[0]) if tb else b.shape
    assert k == k2, (a.shape, b.shape, ta, tb)
    if ta:
        tk_cap = min(tk_cap, 512)
        tm_cap = 1408
    tm, tn, tk = _tile(m, tm_cap), _tile(n, tn_cap), _tile(k, tk_cap)
    if tm * tn * 4 > ACC_BYTES:
        tn = _tile(n, max(LANES, ACC_BYTES // (4 * tm) // LANES * LANES))
    nk = k // tk
    dims = (((0 if ta else 1,), (1 if tb else 0,)), ((), ()))
    has_res = res is not None

    deps = [d for d in deps if d is not None]

    def body(*refs):
        a_ref, b_ref = refs[:2]
        r_ref = refs[2] if has_res else None
        o_ref, acc_ref = refs[-2:]
        kk = pl.program_id(2)

        @pl.when(kk == 0)
        def _():
            acc_ref[...] = jnp.zeros_like(acc_ref)

        acc_ref[...] += lax.dot_general(a_ref[...].astype(BF16), b_ref[...].astype(BF16), dims,
                                        preferred_element_type=F32)

        @pl.when(kk == nk - 1)
        def _():
            out = acc_ref[...]
            if scale != 1.0:
                out = out * scale
            if has_res:
                out = out + r_ref[...].astype(F32)
            o_ref[...] = out.astype(o_ref.dtype)

    a_spec = pl.BlockSpec((tk, tm), lambda i, j, kk: (kk, i)) if ta else pl.BlockSpec((tm, tk), lambda i, j, kk: (i, kk))
    b_spec = pl.BlockSpec((tn, tk), lambda i, j, kk: (j, kk)) if tb else pl.BlockSpec((tk, tn), lambda i, j, kk: (kk, j))
    o_spec = pl.BlockSpec((tm, tn), lambda i, j, kk: (i, j))
    in_specs = [a_spec, b_spec] + ([o_spec] if has_res else []) + [pl.BlockSpec(d.shape, lambda i, j, kk: (0, 0)) for d in deps]
    args = (a, b) + ((res,) if has_res else ()) + tuple(deps)
    return pl.pallas_call(
        body, name=name, grid=(m // tm, n // tn, nk), in_specs=in_specs, out_specs=o_spec,
        out_shape=jax.ShapeDtypeStruct((m, n), out_dtype), scratch_shapes=[pltpu.VMEM((tm, tn), F32)],
        compiler_params=_params(("parallel", "parallel", "arbitrary")),
    )(*args)


def _mm_grouped(a, b, *, tb=False, res=None, out_dtype=F32, name, tm=512):
    s = a.shape[0]
    g = b.shape[0]
    nb, ka = (b.shape[1], b.shape[2]) if tb else (b.shape[2], b.shape[1])
    assert a.shape[1] == g * ka
    tm = min(tm, s)
    dims = (((1,), (1 if tb else 0,)), ((), ()))
    has_res = res is not None

    def body(*refs):
        if has_res:
            a_ref, b_ref, r_ref, o_ref = refs
        else:
            a_ref, b_ref, o_ref = refs
        out = lax.dot_general(a_ref[...].astype(BF16), b_ref[...].astype(BF16), dims, preferred_element_type=F32)
        if has_res:
            out = out + r_ref[...].astype(F32)
        o_ref[...] = out.astype(o_ref.dtype)

    o_spec = pl.BlockSpec((tm, nb), lambda i, j: (i, j))
    in_specs = [pl.BlockSpec((tm, ka), lambda i, j: (i, j)), pl.BlockSpec((None,) + b.shape[1:], lambda i, j: (j, 0, 0))]
    return pl.pallas_call(
        body, name=name, grid=(s // tm, g), in_specs=in_specs + ([o_spec] if has_res else []), out_specs=o_spec,
        out_shape=jax.ShapeDtypeStruct((s, g * nb), out_dtype), compiler_params=_params(("parallel", "parallel")),
    )(a, b, *((res,) if has_res else ()))


def _mm_grouped_tn(a, b, *, ka, kb, name, tk=512):
    s = a.shape[0]
    g = a.shape[1] // ka
    assert b.shape[1] == g * kb
    tk = min(tk, s)
    nk = s // tk

    def body(a_ref, b_ref, o_ref):
        part = lax.dot_general(a_ref[...].astype(BF16), b_ref[...].astype(BF16), (((0,), (0,)), ((), ())),
                               preferred_element_type=F32)

        @pl.when(pl.program_id(1) == 0)
        def _():
            o_ref[...] = part

        @pl.when(pl.program_id(1) > 0)
        def _():
            o_ref[...] += part

    return pl.pallas_call(
        body, name=name, grid=(g, nk),
        in_specs=[pl.BlockSpec((tk, ka), lambda j, kk: (kk, j)), pl.BlockSpec((tk, kb), lambda j, kk: (kk, j))],
        out_specs=pl.BlockSpec((None, ka, kb), lambda j, kk: (j, 0, 0)),
        out_shape=jax.ShapeDtypeStruct((g, ka, kb), F32), compiler_params=_params(("parallel", "arbitrary")),
    )(a, b)


def _rowwise(fn, rows, consts, outs, accs=(), *, ts=512, name, deps=()):
    rows = [r if isinstance(r, tuple) else (r, 0, r.shape[1]) for r in rows]
    s = rows[0][0].shape[0]
    ts = min(ts, s)
    assert s % ts == 0
    n_rows, n_consts, n_outs = len(rows), len(consts), len(outs)
    deps = [d for d in deps if d is not None]
    consts = list(consts) + deps

    def body(*refs):
        ins = [r[...] for r in refs[:n_rows + n_consts]]
        res = fn(*ins)
        res = tuple(res) if isinstance(res, (tuple, list)) else (res,)
        out_refs = refs[n_rows + len(consts):]
        for o_ref, val in zip(out_refs[:n_outs], res[:n_outs]):
            o_ref[...] = val.astype(o_ref.dtype)
        if accs:
            first = pl.program_id(0) == 0

            @pl.when(first)
            def _():
                for a_ref, val in zip(out_refs[n_outs:], res[n_outs:]):
                    a_ref[...] = val.astype(F32)

            @pl.when(jnp.logical_not(first))
            def _():
                for a_ref, val in zip(out_refs[n_outs:], res[n_outs:]):
                    a_ref[...] += val.astype(F32)

    in_specs = [pl.BlockSpec((ts, width), lambda i, cb=cb: (i, cb)) for _, cb, width in rows]
    in_specs += [pl.BlockSpec(c.shape, lambda i: (0, 0)) for c in consts]
    out_specs = [pl.BlockSpec((ts, w), lambda i: (i, 0)) for w, _ in outs]
    out_specs += [pl.BlockSpec(tuple(sh), lambda i: (0, 0)) for sh in accs]
    out_shape = [jax.ShapeDtypeStruct((s, w), dt) for w, dt in outs]
    out_shape += [jax.ShapeDtypeStruct(tuple(sh), F32) for sh in accs]
    res = pl.pallas_call(
        body, name=name, grid=(s // ts,), in_specs=in_specs, out_specs=out_specs, out_shape=out_shape,
        compiler_params=_params(("arbitrary",)),
    )(*[a for a, _, _ in rows], *consts)
    return res


def _rowwise_bwd(f, rows, consts, cts, *, row_grads, const_grads, adds=None, ts=512, name, deps=()):
    adds = adds or {}
    n_rows, n_consts, n_cts = len(rows), len(consts), len(cts)
    add_keys = sorted(adds)
    rg = sorted(row_grads)
    cg = sorted(const_grads)

    def fn(*args):
        r = args[:n_rows]
        c = args[n_rows:n_rows + n_consts]
        ct = args[n_rows + n_consts:n_rows + n_consts + n_cts]
        extra = args[n_rows + n_consts + n_cts:]
        outs, vjp = jax.vjp(f, *r, *c)
        outs = tuple(outs) if isinstance(outs, (tuple, list)) else (outs,)
        cot = tuple(g.astype(o.dtype) for g, o in zip(ct, outs))
        grads = vjp(cot if len(cot) > 1 else cot[0])
        res = []
        for i in rg:
            g = grads[i].astype(F32)
            if i in adds:
                g = g + extra[add_keys.index(i)].astype(F32)
            res.append(g)
        for i in cg:
            res.append(grads[n_rows + i])
        return tuple(res)

    rows_all = list(rows) + list(cts) + [adds[i] for i in add_keys]
    def fn2(*args):
        nr = len(rows_all)
        rr, cc = args[:nr], args[nr:]
        return fn(*rr[:n_rows], *cc, *rr[n_rows:])

    outs = [(rows[i][2] if isinstance(rows[i], tuple) else rows[i].shape[1], row_grads[i]) for i in rg]
    accs = [consts[i].shape for i in cg]
    return _rowwise(fn2, rows_all, list(consts), outs, accs, ts=ts, name=name, deps=deps)


def _rms(x, g):
    xf = x.astype(F32)
    return xf * lax.rsqrt(jnp.mean(xf * xf, axis=-1, keepdims=True) + EPS) * g.astype(F32)


def _sigmoid(x):
    return 1.0 / (1.0 + jnp.exp(-x))


def _f_norm(x, g):
    return _rms(x, g).astype(BF16)


def _f_swiglu(gate, up):
    gate, up = gate.astype(F32), up.astype(F32)
    return (gate * _sigmoid(gate) * up).astype(BF16)


def _f_prep1(proj, gq, gkv):
    return _rms(proj[:, :Q_RANK], gq).astype(BF16), _rms(proj[:, Q_RANK:Q_RANK + KV_RANK], gkv).astype(BF16)


KR_BLOCK = (Q_RANK + KV_RANK + SSM_W) // LANES


def _f_prep2(qall, kv, kr, krs, cos, sin, gq, gk):
    kr, krs = kr.astype(F32), krs.astype(F32)
    k_rot = kr * gk[1:2] * cos + krs * gk[2:3] * sin
    k_ss = jnp.sum(kr * kr, axis=-1, keepdims=True)
    q_scale = QK ** -0.5 / LN2
    qs, ks, vs = [], [], []
    for h in range(H):
        qn = qall[:, h * LANES:(h + 1) * LANES].astype(F32)
        qr = qall[:, (H + h) * LANES:(H + h + 1) * LANES].astype(F32)
        qrs = qall[:, (2 * H + h) * LANES:(2 * H + h + 1) * LANES].astype(F32)
        rstd = lax.rsqrt((jnp.sum(qn * qn, axis=-1, keepdims=True) + jnp.sum(qr * qr, axis=-1, keepdims=True)) / QK + EPS)
        rstd = rstd * q_scale
        qs += [qn * gq[0:1] * rstd, (qr * gq[1:2] * cos + qrs * gq[2:3] * sin) * rstd]
        kn = kv[:, 2 * h * LANES:(2 * h + 1) * LANES].astype(F32)
        rstd_k = lax.rsqrt((jnp.sum(kn * kn, axis=-1, keepdims=True) + k_ss) / QK + EPS)
        ks += [kn * gk[0:1] * rstd_k, k_rot * rstd_k]
        vs.append(kv[:, (2 * h + 1) * LANES:(2 * h + 2) * LANES])
    return (jnp.concatenate(qs, axis=-1).astype(BF16), jnp.concatenate(ks, axis=-1).astype(BF16),
            jnp.concatenate(vs, axis=-1).astype(BF16))


def _gelu(x):
    return 0.5 * x * (1.0 + jnp.tanh(math.sqrt(2.0 / math.pi) * (x + 0.044715 * (x * x * x))))


def _f_s5_gelu(yc, u, d):
    return _gelu(yc.astype(F32) + d * u.astype(F32))


def _f_outnorm(o_mla, g, z, b_glu, g_om, g_os):
    y_ssm = g * _sigmoid(z + b_glu)
    return jnp.concatenate([_rms(o_mla, g_om), _rms(y_ssm, g_os)], axis=-1).astype(BF16)


def _f_memk(kvm, gk):
    ks = [_rms(kvm[:, h * XH:(h + 1) * XH], gk) for h in range(H)]
    return jnp.concatenate(ks, axis=-1).astype(BF16), kvm[:, H * XH:].astype(BF16)


def _f_disc(lr, li, log_dt, br, bi):
    dt = jnp.exp(log_dt)
    decay = jnp.exp(lr * dt)
    ar = decay * jnp.cos(li * dt)
    ai = decay * jnp.sin(li * dt)
    den = lr * lr + li * li
    nr = ar - 1.0
    coef_r = (nr * lr + ai * li) / den
    coef_i = (ai * lr - nr * li) / den
    return ar, ai, coef_r * br - coef_i * bi, coef_r * bi + coef_i * br


def _causal_mask(i, j, tq, tk):
    qpos = i * tq + lax.broadcasted_iota(jnp.int32, (tq, tk), 0)
    kpos = j * tk + lax.broadcasted_iota(jnp.int32, (tq, tk), 1)
    return qpos >= kpos


def _attn_fwd(q, k, v, *, t=512):
    s = q.shape[0]
    t = min(t, s)
    nb = s // t

    def body(q_ref, k_ref, v_ref, o_ref, lse_ref, m_sc, l_sc, acc_sc):
        i, j = pl.program_id(1), pl.program_id(2)

        @pl.when(j == 0)
        def _():
            m_sc[...] = jnp.full_like(m_sc, -jnp.inf)
            l_sc[...] = jnp.zeros_like(l_sc)
            acc_sc[...] = jnp.zeros_like(acc_sc)

        def block(diagonal):
            sc = lax.dot_general(q_ref[...], k_ref[...], (((1,), (1,)), ((), ())), preferred_element_type=F32)
            if diagonal:
                sc = jnp.where(_causal_mask(i, j, t, t), sc, -jnp.inf)
            m_old = m_sc[...]
            m_new = jnp.maximum(m_old, jnp.max(sc, axis=-1, keepdims=True))
            p = jnp.exp2(sc - m_new)
            alpha = jnp.exp2(m_old - m_new)
            l_sc[...] = alpha * l_sc[...] + jnp.sum(p, axis=-1, keepdims=True)
            acc_sc[...] = alpha * acc_sc[...] + jnp.dot(p.astype(BF16), v_ref[...], preferred_element_type=F32)
            m_sc[...] = m_new

        pl.when(j < i)(lambda: block(False))

        @pl.when(j == i)
        def _():
            block(True)
            o_ref[...] = acc_sc[...] / l_sc[...]
            lse_ref[...] = jnp.broadcast_to(m_sc[...] + jnp.log2(l_sc[...]), lse_ref.shape)

    kv_map = lambda h, i, j: (jnp.minimum(j, i), h)
    return pl.pallas_call(
        body, name="mla_attn_fwd", grid=(H, nb, nb),
        in_specs=[pl.BlockSpec((t, HQ), lambda h, i, j: (i, h)), pl.BlockSpec((t, HQ), kv_map),
                  pl.BlockSpec((t, VD), kv_map)],
        out_specs=[pl.BlockSpec((t, VD), lambda h, i, j: (i, h)), pl.BlockSpec((t, LANES), lambda h, i, j: (i, h))],
        out_shape=[jax.ShapeDtypeStruct((s, H * VD), F32), jax.ShapeDtypeStruct((s, H * LANES), F32)],
        scratch_shapes=[pltpu.VMEM((t, 1), F32), pltpu.VMEM((t, 1), F32), pltpu.VMEM((t, VD), F32)],
        compiler_params=_params(("parallel", "parallel", "arbitrary")),
    )(q, k, v)


def _attn_probs(q_ref, k_ref, v_ref, do_ref, lse_ref, dl_ref, i, j, t, diagonal):
    sc = lax.dot_general(q_ref[...], k_ref[...], (((1,), (1,)), ((), ())), preferred_element_type=F32)
    p = jnp.exp2(sc - jnp.tile(lse_ref[...], (1, t // LANES)))
    if diagonal:
        p = jnp.where(_causal_mask(i, j, t, t), p, 0.0)
    dp = lax.dot_general(do_ref[...], v_ref[...], (((1,), (1,)), ((), ())), preferred_element_type=F32)
    ds = p * (dp - jnp.tile(dl_ref[...], (1, t // LANES)))
    return p, ds


def _attn_bwd(q, k, v, do, lse, delta, *, t=512):
    s = q.shape[0]
    t = min(t, s)
    nb = s // t

    def body(q_ref, k_ref, v_ref, do_ref, lse_ref, dl_ref, dq_ref, dk_ref, dv_ref, dk_sc, dv_sc):
        j, i = pl.program_id(1), pl.program_id(2)

        @pl.when(jnp.logical_and(i == 0, j == 0))
        def _():
            dq_ref[...] = jnp.zeros_like(dq_ref)

        @pl.when(i == 0)
        def _():
            dk_sc[...] = jnp.zeros_like(dk_sc)
            dv_sc[...] = jnp.zeros_like(dv_sc)

        def block(diagonal):
            p, ds = _attn_probs(q_ref, k_ref, v_ref, do_ref, lse_ref, dl_ref, i, j, t, diagonal)
            dsb = ds.astype(BF16)
            dv_sc[...] += lax.dot_general(p.astype(BF16), do_ref[...], (((0,), (0,)), ((), ())), preferred_element_type=F32)
            dk_sc[...] += lax.dot_general(dsb, q_ref[...], (((0,), (0,)), ((), ())), preferred_element_type=F32)
            rows = pl.ds(pl.multiple_of(i * t, t), t)
            dq_ref[rows, :] += jnp.dot(dsb, k_ref[...], preferred_element_type=F32)

        pl.when(i > j)(lambda: block(False))
        pl.when(i == j)(lambda: block(True))

        @pl.when(i == nb - 1)
        def _():
            dk_ref[...] = (dk_sc[...] * LN2).astype(dk_ref.dtype)
            dv_ref[...] = dv_sc[...].astype(dv_ref.dtype)

        @pl.when(jnp.logical_and(i == nb - 1, j == nb - 1))
        def _():
            dq_ref[...] = dq_ref[...] * LN2

    q_map = lambda h, j, i: (jnp.maximum(i, j), h)
    kv_map = lambda h, j, i: (j, h)
    dq, dk, dv = pl.pallas_call(
        body, name="mla_attn_bwd", grid=(H, nb, nb),
        in_specs=[pl.BlockSpec((t, HQ), q_map), pl.BlockSpec((t, HQ), kv_map), pl.BlockSpec((t, VD), kv_map),
                  pl.BlockSpec((t, VD), q_map), pl.BlockSpec((t, LANES), q_map), pl.BlockSpec((t, LANES), q_map)],
        out_specs=[pl.BlockSpec((s, HQ), lambda h, j, i: (0, h)), pl.BlockSpec((t, HQ), kv_map), pl.BlockSpec((t, VD), kv_map)],
        out_shape=[jax.ShapeDtypeStruct((s, H * HQ), F32), jax.ShapeDtypeStruct((s, H * HQ), BF16),
                   jax.ShapeDtypeStruct((s, H * VD), BF16)],
        scratch_shapes=[pltpu.VMEM((t, HQ), F32), pltpu.VMEM((t, VD), F32)],
        compiler_params=_params(("parallel", "arbitrary", "arbitrary")),
    )(q, k, v, do, lse, delta)
    return dq, dk, dv


def _f_delta(do, o):
    prod = do.astype(F32) * o.astype(F32)
    parts = [jnp.broadcast_to(jnp.sum(prod[:, h * VD:(h + 1) * VD], axis=-1, keepdims=True), (do.shape[0], LANES))
             for h in range(H)]
    return jnp.concatenate(parts, axis=-1), do.astype(BF16)


def _xattn_head(qh, kh, gq):
    qn = _rms(qh, gq) * (XH ** -0.5)
    sc = lax.dot_general(qn.astype(BF16), kh, (((1,), (1,)), ((), ())), preferred_element_type=F32)
    sc = sc - jnp.max(sc, axis=-1, keepdims=True)
    e = jnp.exp(sc)
    return qn, e / jnp.sum(e, axis=-1, keepdims=True)


def _xattn_fwd(q, kn, v, gq, *, ts=512):
    def fn(qb, knb, vb, g):
        outs = []
        for h in range(H):
            sl = slice(h * XH, (h + 1) * XH)
            _, p = _xattn_head(qb[:, sl], knb[:, sl], g)
            outs.append(jnp.dot(p.astype(BF16), vb[:, sl], preferred_element_type=F32))
        return (jnp.concatenate(outs, axis=-1),)

    return _rowwise(fn, [q], [kn, v, gq], [(H * XH, BF16)], ts=ts, name="xattn_fwd")[0]


def _xattn_bwd(q, kn, v, gq, do, *, ts=512):
    def fn(qb, dob, knb, vb, g):
        dqs, dks, dvs = [], [], []
        dg = jnp.zeros((1, XH), F32)
        for h in range(H):
            sl = slice(h * XH, (h + 1) * XH)
            qh, kh, vh, doh = qb[:, sl], knb[:, sl], vb[:, sl], dob[:, sl].astype(BF16)
            qn, p = _xattn_head(qh, kh, g)
            dp = lax.dot_general(doh, vh, (((1,), (1,)), ((), ())), preferred_element_type=F32)
            dvs.append(lax.dot_general(p.astype(BF16), doh, (((0,), (0,)), ((), ())), preferred_element_type=F32))
            ds = (p * (dp - jnp.sum(dp * p, axis=-1, keepdims=True))).astype(BF16)
            dqn = jnp.dot(ds, kh, preferred_element_type=F32)
            dks.append(lax.dot_general(ds, qn.astype(BF16), (((0,), (0,)), ((), ())), preferred_element_type=F32))
            _, vjp_n = jax.vjp(lambda a, b: _rms(a, b) * (XH ** -0.5), qh, g)
            dqh, dgh = vjp_n(dqn)
            dqs.append(dqh)
            dg = dg + dgh
        return (jnp.concatenate(dqs, axis=-1), jnp.concatenate(dks, axis=-1), jnp.concatenate(dvs, axis=-1), dg)

    return _rowwise(fn, [q, do], [kn, v, gq], [(H * XH, BF16)], [kn.shape, v.shape, gq.shape], ts=ts, name="xattn_bwd")


def _cmul(ar, ai, xr, xi):
    return ar * xr - ai * xi, ar * xi + ai * xr


def _scan_in_place(xr_ref, xi_ref, ar, ai, *, reverse):
    s, cw = xr_ref.shape
    c = SCAN_CHUNKS
    tt = s // c
    a_r = jnp.broadcast_to(ar, (c, cw))
    a_i = jnp.broadcast_to(ai, (c, cw))
    zero = jnp.zeros((c, cw), F32)

    def row(step):
        t = (tt - 1 - step) if reverse else step
        return pl.ds(pl.multiple_of(t * c, c), c)

    def local(step, carry):
        sr, si, qr, qi = carry
        r = row(step)
        nr, ni = _cmul(a_r, a_i, sr, si)
        nr, ni = nr + xr_ref[r, :], ni + xi_ref[r, :]
        xr_ref[r, :] = nr
        xi_ref[r, :] = ni
        return (nr, ni) + _cmul(a_r, a_i, qr, qi)

    end_r, end_i, pr, pi = lax.fori_loop(0, tt, local, (zero, zero, jnp.ones((c, cw), F32), zero), unroll=SCAN_UNROLL)

    rows_id = lax.broadcasted_iota(jnp.int32, (c, cw), 0)
    car_r, car_i = zero, zero
    cur_r, cur_i = jnp.zeros((1, cw), F32), jnp.zeros((1, cw), F32)
    order = range(c - 1, -1, -1) if reverse else range(c)
    for kk in order:
        car_r = jnp.where(rows_id == kk, cur_r, car_r)
        car_i = jnp.where(rows_id == kk, cur_i, car_i)
        nr, ni = _cmul(pr[0:1], pi[0:1], cur_r, cur_i)
        cur_r = nr + end_r[kk:kk + 1]
        cur_i = ni + end_i[kk:kk + 1]

    def fix(step, carry):
        qr, qi = _cmul(a_r, a_i, *carry)
        r = row(step)
        dr, di = _cmul(qr, qi, car_r, car_i)
        xr_ref[r, :] += dr
        xi_ref[r, :] += di
        return qr, qi

    lax.fori_loop(0, tt, fix, (jnp.ones((c, cw), F32), zero), unroll=SCAN_UNROLL)


S5_ROWS = 512


def _s5_scan(v, w_r, w_i, ar, ai, *, reverse, tb, readout=None, name):
    s = v.shape[0]
    g = w_r.shape[0]
    nv, ns = SSM_PACK * SSM_GRP, SSM_PACK * SSM_P
    rows = min(S5_ROWS, s)
    dims = (((1,), (1 if tb else 0,)), ((), ()))
    n_w = 2 if readout is None else 4

    def body(v_ref, ar_ref, ai_ref, *refs):
        w = [r[...] for r in refs[:n_w]]
        xr_ref, xi_ref = refs[n_w:n_w + 2]
        for r0 in range(0, s, rows):
            vb = v_ref[r0:r0 + rows, :].astype(BF16)
            xr_ref[r0:r0 + rows, :] = lax.dot_general(vb, w[0], dims, preferred_element_type=F32)
            xi_ref[r0:r0 + rows, :] = lax.dot_general(vb, w[1], dims, preferred_element_type=F32)
        _scan_in_place(xr_ref, xi_ref, ar_ref[...], ai_ref[...], reverse=reverse)
        if readout is not None:
            y_ref = refs[n_w + 2]
            for r0 in range(0, s, rows):
                y_ref[r0:r0 + rows, :] = (
                    jnp.dot(xr_ref[r0:r0 + rows, :].astype(BF16), w[2], preferred_element_type=F32)
                    + jnp.dot(xi_ref[r0:r0 + rows, :].astype(BF16), w[3], preferred_element_type=F32))

    col = lambda j: (0, j)
    w_spec = lambda a: pl.BlockSpec((None,) + a.shape[1:], lambda j: (j, 0, 0))
    weights = [w_r, w_i] + (list(readout) if readout is not None else [])
    out_specs = [pl.BlockSpec((s, ns), col)] * 2 + ([pl.BlockSpec((s, nv), col)] if readout is not None else [])
    out_shape = [jax.ShapeDtypeStruct((s, g * ns), F32)] * 2 + (
        [jax.ShapeDtypeStruct((s, g * nv), F32)] if readout is not None else [])
    return pl.pallas_call(
        body, name=name, grid=(g,),
        in_specs=[pl.BlockSpec((s, nv), col), pl.BlockSpec((1, ns), col), pl.BlockSpec((1, ns), col)] + [w_spec(a) for a in weights],
        out_specs=out_specs, out_shape=out_shape, compiler_params=_params(("parallel",)),
    )(v, ar, ai, *weights)


def _s5_grads(lam_r, lam_i, xr, xi, u, dyc, du_d, b_r, b_i):
    s = u.shape[0]
    g = b_r.shape[0]
    nv, ns, c = SSM_PACK * SSM_GRP, SSM_PACK * SSM_P, SCAN_CHUNKS
    rows = min(S5_ROWS, s)
    slabs = rows // c
    last_slab = s // c - 1
    nt = (((1,), (1,)), ((), ()))
    tn = (((0,), (0,)), ((), ()))

    def body(lr_ref, li_ref, xr_ref, xi_ref, pr_ref, pi_ref, u_ref, dy_ref, dud_ref, br_ref, bi_ref,
             du_ref, dbr_ref, dbi_ref, dcr_ref, dci_ref, dar_ref, dai_ref):
        first = pl.program_id(1) == 0
        l_r, l_i, x_r, x_i = lr_ref[...], li_ref[...], xr_ref[...], xi_ref[...]
        lrb, lib = l_r.astype(BF16), l_i.astype(BF16)
        du_ref[...] = (dud_ref[...] + lax.dot_general(lrb, br_ref[...], nt, preferred_element_type=F32)
                       + lax.dot_general(lib, bi_ref[...], nt, preferred_element_type=F32))
        ub, dyb = u_ref[...].astype(BF16), dy_ref[...].astype(BF16)
        rows_id = lax.broadcasted_iota(jnp.int32, (c, ns), 0)

        def before(p_ref, x):
            p = p_ref[...]
            p = jnp.where(first, jnp.where(rows_id == 0, 0.0, pltpu.roll(p, 1, 0)), p)
            return jnp.concatenate([p, x[:rows - c]], axis=0)

        xp_r, xp_i = before(pr_ref, x_r), before(pi_ref, x_i)
        parts = (lax.dot_general(ub, lrb, tn, preferred_element_type=F32),
                 lax.dot_general(ub, lib, tn, preferred_element_type=F32),
                 lax.dot_general(x_r.astype(BF16), dyb, tn, preferred_element_type=F32),
                 lax.dot_general(x_i.astype(BF16), dyb, tn, preferred_element_type=F32),
                 jnp.sum(l_r * xp_r + l_i * xp_i, axis=0, keepdims=True),
                 jnp.sum(l_i * xp_r - l_r * xp_i, axis=0, keepdims=True))
        accs = (dbr_ref, dbi_ref, dcr_ref, dci_ref, dar_ref, dai_ref)

        @pl.when(first)
        def _():
            for a_ref, val in zip(accs, parts):
                a_ref[...] = val

        @pl.when(jnp.logical_not(first))
        def _():
            for a_ref, val in zip(accs, parts):
                a_ref[...] += val

    state = pl.BlockSpec((rows, ns), lambda j, k: (k, j))
    chan = pl.BlockSpec((rows, nv), lambda j, k: (k, j))
    slab = pl.BlockSpec((c, ns), lambda j, k: (jnp.where(k == 0, last_slab, k * slabs - 1), j))
    per_b = pl.BlockSpec((None, nv, ns), lambda j, k: (j, 0, 0))
    per_c = pl.BlockSpec((None, ns, nv), lambda j, k: (j, 0, 0))
    per_a = pl.BlockSpec((1, ns), lambda j, k: (0, j))
    return pl.pallas_call(
        body, name="s5_grads", grid=(g, s // rows),
        in_specs=[state, state, state, state, slab, slab, chan, chan, chan, per_b, per_b],
        out_specs=[chan, per_b, per_b, per_c, per_c, per_a, per_a],
        out_shape=[jax.ShapeDtypeStruct((s, g * nv), F32), jax.ShapeDtypeStruct((g, nv, ns), F32),
                   jax.ShapeDtypeStruct((g, nv, ns), F32), jax.ShapeDtypeStruct((g, ns, nv), F32),
                   jax.ShapeDtypeStruct((g, ns, nv), F32), jax.ShapeDtypeStruct((1, g * ns), F32),
                   jax.ShapeDtypeStruct((1, g * ns), F32)],
        compiler_params=_params(("parallel", "arbitrary")),
    )(lam_r, lam_i, xr, xi, xr, xi, u, dyc, du_d, b_r, b_i)


def _mesh_place():
    x, y, c = lax.axis_index("x"), lax.axis_index("y"), lax.axis_index("c")
    peers = []
    for k in range(1, N_DEV):
        px, py, pc = x ^ ((k >> 2) & 1), y ^ ((k >> 1) & 1), c ^ (k & 1)
        peers.append(((px, py, pc), 4 * px + 2 * py + pc))
    return 4 * x + 2 * y + c, peers


class _Exchange:
    SAME_CORE_MASKS = (2, 4, 6)

    def __init__(self, arrays, rows, *, gather, name, after=None, two_level=False):
        self.n_arr, self.rows, self.gather, self.name = len(arrays), rows, gather, name
        self.two_level, self.in_flight = two_level, (1 + len(self.SAME_CORE_MASKS) if two_level else N_DEV - 1)
        n_arr = self.n_arr
        if gather:
            assert all(r % BF16_ROWS == 0 for r in rows)
            lands = [lax.empty((N_DEV * r, a.shape[1]), a.dtype) for a, r in zip(arrays, rows)]
        else:
            lands = [lax.empty((N_DEV - 1,) + (tuple(a.shape) if st is None else (n, a.shape[1])), a.dtype)
                     for a, (st, n) in zip(arrays, rows)]
        has_after = after is not None

        def body(*refs):
            ins, zones = refs[:n_arr], refs[n_arr:2 * n_arr]
            sems = refs[2 * n_arr + has_after:4 * n_arr + has_after]
            token = refs[-1]
            me, peers = _mesh_place()
            for i in range(n_arr):
                for k, (pxyz, pid) in enumerate(peers):
                    if two_level and k + 1 not in (1,) + self.SAME_CORE_MASKS:
                        continue
                    if gather:
                        src = ins[i]
                        dst = zones[i].at[pl.ds(pl.multiple_of(me * rows[i], BF16_ROWS), rows[i])]
                    else:
                        stride, n = rows[i]
                        src = ins[i] if stride is None else ins[i].at[pl.ds(pl.multiple_of(pid * stride, BF16_ROWS), n)]
                        dst = zones[i].at[k]
                    pltpu.make_async_remote_copy(
                        src_ref=src, dst_ref=dst, send_sem=sems[2 * i], recv_sem=sems[2 * i + 1],
                        device_id=pxyz, device_id_type=pl.DeviceIdType.MESH).start()
            token[...] = jnp.zeros_like(token)

        hbm = pl.BlockSpec(memory_space=pltpu.HBM)
        sem = pl.BlockSpec(memory_space=pltpu.SEMAPHORE)
        args = [pltpu.with_memory_space_constraint(a, pltpu.HBM) for a in list(arrays) + lands]
        res = pl.pallas_call(
            body, name=name + "_start",
            in_specs=[hbm] * (2 * n_arr) + ([pl.BlockSpec(memory_space=pl.ANY)] if has_after else []),
            out_specs=[sem] * (2 * n_arr) + [hbm] * (2 * n_arr) + [pl.BlockSpec(memory_space=pltpu.VMEM)],
            out_shape=[pltpu.SemaphoreType.DMA(())] * (2 * n_arr) + [pltpu.HBM(a.shape, a.dtype) for a in args]
            + [jax.ShapeDtypeStruct((8, LANES), F32)],
            input_output_aliases={i: 2 * n_arr + i for i in range(2 * n_arr)},
            compiler_params=pltpu.CompilerParams(has_side_effects=pltpu.SideEffectType.DATAFLOW_SIDE_EFFECTING),
        )(*args, *([after] if has_after else []))
        self.sems, self.thru, self.token = res[:2 * n_arr], res[2 * n_arr:4 * n_arr], res[-1]

    def _wait_all(self, zones, sems):
        myself = (lax.axis_index("x"), lax.axis_index("y"), lax.axis_index("c"))
        for i in range(self.n_arr):
            many = zones[i].at[pl.ds(0, self.in_flight * self.rows[i])] if self.gather else zones[i]
            all_of_them = pltpu.make_async_remote_copy(
                src_ref=many, dst_ref=many, send_sem=sems[2 * i], recv_sem=sems[2 * i + 1],
                device_id=myself, device_id_type=pl.DeviceIdType.MESH)
            all_of_them.wait_recv()
            all_of_them.wait_send()

    def forward(self, after):
        n_arr = self.n_arr

        def body(*refs):
            zones, sems = refs[n_arr:2 * n_arr], refs[2 * n_arr:4 * n_arr]
            new_sems = refs[4 * n_arr + 1:6 * n_arr + 1]
            self._wait_all(zones, sems)
            _, peers = _mesh_place()
            sibling, _ = peers[0]
            for i in range(n_arr):
                for mask in self.SAME_CORE_MASKS:
                    _, pid = peers[mask - 1]
                    block = zones[i].at[pl.ds(pl.multiple_of(pid * self.rows[i], BF16_ROWS), self.rows[i])]
                    pltpu.make_async_remote_copy(
                        src_ref=block, dst_ref=block, send_sem=new_sems[2 * i], recv_sem=new_sems[2 * i + 1],
                        device_id=sibling, device_id_type=pl.DeviceIdType.MESH).start()

        hbm = pl.BlockSpec(memory_space=pltpu.HBM)
        sem = pl.BlockSpec(memory_space=pltpu.SEMAPHORE)
        res = pl.pallas_call(
            body, name=self.name + "_forward",
            in_specs=[hbm] * (2 * n_arr) + [sem] * (2 * n_arr) + [pl.BlockSpec(memory_space=pl.ANY)],
            out_specs=[sem] * (2 * n_arr) + [hbm] * (2 * n_arr),
            out_shape=[pltpu.SemaphoreType.DMA(())] * (2 * n_arr) + [pltpu.HBM(a.shape, a.dtype) for a in self.thru],
            input_output_aliases={i: 2 * n_arr + i for i in range(2 * n_arr)},
            compiler_params=pltpu.CompilerParams(has_side_effects=pltpu.SideEffectType.DATAFLOW_SIDE_EFFECTING),
        )(*self.thru, *self.sems, after)
        self.sems, self.thru = res[:2 * n_arr], res[2 * n_arr:]
        self.two_level, self.in_flight = False, len(self.SAME_CORE_MASKS)

    def wait(self, after):
        n_arr = self.n_arr
        if self.two_level:
            self.forward(after)

        def body(*refs):
            self._wait_all(refs[n_arr:2 * n_arr], refs[2 * n_arr:4 * n_arr])

        hbm = pl.BlockSpec(memory_space=pltpu.HBM)
        sem = pl.BlockSpec(memory_space=pltpu.SEMAPHORE)
        res = pl.pallas_call(
            body, name=self.name + "_wait",
            in_specs=[hbm] * (2 * n_arr) + [sem] * (2 * n_arr) + [pl.BlockSpec(memory_space=pl.ANY)],
            out_specs=[hbm] * (2 * n_arr), out_shape=[pltpu.HBM(a.shape, a.dtype) for a in self.thru],
            input_output_aliases={i: i for i in range(2 * n_arr)},
            compiler_params=pltpu.CompilerParams(has_side_effects=pltpu.SideEffectType.DATAFLOW_SIDE_EFFECTING),
        )(*self.thru, *self.sems, after)
        return res[:n_arr], res[n_arr:]


def _my_slot():
    me = 4 * lax.axis_index("x") + 2 * lax.axis_index("y") + lax.axis_index("c")
    return me.astype(jnp.int32).reshape(1)


def _place_own(gathered, block, me, *, name):
    r, c = block.shape

    def body(me_ref, b_ref, g_ref, o_ref):
        o_ref[...] = b_ref[...]

    return pl.pallas_call(
        body, name=name, out_shape=jax.ShapeDtypeStruct(gathered.shape, gathered.dtype),
        grid_spec=pltpu.PrefetchScalarGridSpec(
            num_scalar_prefetch=1, grid=(1,),
            in_specs=[pl.BlockSpec((r, c), lambda i, me_ref: (0, 0)), pl.BlockSpec(memory_space=pl.ANY)],
            out_specs=pl.BlockSpec((r, c), lambda i, me_ref: (me_ref[0], 0))),
        input_output_aliases={2: 0}, compiler_params=_params(("arbitrary",)),
    )(me, block, gathered)


def _elementwise_tiles(r, c):
    if r % 128 == 0:
        return 128, c
    return r, (256 if c % 256 == 0 else c)


def _adamw_math(g, w, m, v):
    nm = ADAM_B1 * m + (1.0 - ADAM_B1) * g
    nv = ADAM_B2 * v + (1.0 - ADAM_B2) * (g * g)
    m_hat = nm / (1.0 - ADAM_B1 ** ADAM_STEP)
    v_hat = nv / (1.0 - ADAM_B2 ** ADAM_STEP)
    return -ADAM_LR * (m_hat / (jnp.sqrt(v_hat) + ADAM_EPS) + ADAM_WD * w), nm, nv


def _sum_parts(me_ref, own_ref, p_ref, r):
    own = own_ref[...].astype(F32)
    g = None
    for d in range(N_DEV):
        k = jnp.bitwise_xor(me_ref[0], d)
        term = jnp.where(k == 0, own, p_ref[jnp.maximum(k, 1) - 1].astype(F32))
        g = term if g is None else g + term
    return g[0:r, :]


def _sum_adamw(me, sent, stride, parts, r, w=None, m=None, v=None, *, name):
    _, own_rows, cdim = parts.shape
    assert stride is None or stride == own_rows
    tc = 256 if cdim % 256 == 0 else cdim
    update = w is not None

    def body(me_ref, own_ref, p_ref, *refs):
        g = _sum_parts(me_ref, own_ref, p_ref, r)
        if update:
            w_ref, m_ref, v_ref, g_ref, d_ref, nm_ref, nv_ref = refs
            d_ref[...], nm_ref[...], nv_ref[...] = _adamw_math(g, w_ref[...], m_ref[...], v_ref[...])
        else:
            g_ref, = refs
        g_ref[...] = g

    blk = pl.BlockSpec((r, tc), lambda j, me_ref: (0, j))
    own_spec = pl.BlockSpec((own_rows, tc), (lambda j, me_ref: (0, j)) if stride is None else (lambda j, me_ref: (me_ref[0], j)))
    n_out = 4 if update else 1
    res = pl.pallas_call(
        body, name=name, out_shape=[jax.ShapeDtypeStruct((r, cdim), F32)] * n_out,
        grid_spec=pltpu.PrefetchScalarGridSpec(
            num_scalar_prefetch=1, grid=(cdim // tc,),
            in_specs=[own_spec, pl.BlockSpec((N_DEV - 1, own_rows, tc), lambda j, me_ref: (0, 0, j))]
            + ([blk] * 3 if update else []),
            out_specs=[blk] * n_out),
        compiler_params=_params(("parallel",)),
    )(me, sent, parts, *((w, m, v) if update else ()))
    return list(res)


def _sum_adamw_rows(me, sent, parts, ws, ms, vs, *, name):
    n = len(ws)
    r = -(-n // 8) * 8

    def body(me_ref, own_ref, p_ref, *refs):
        ins, outs, g_sc = refs[:3 * n], refs[3 * n:7 * n], refs[-1]
        own, g = own_ref[0:r, :], None
        for d in range(N_DEV):
            k = jnp.bitwise_xor(me_ref[0], d)
            term = jnp.where(k == 0, own, p_ref[jnp.maximum(k, 1) - 1, 0:r, :])
            g = term if g is None else g + term
        g_sc[...] = g
        for i in range(n):
            width = ins[3 * i].shape[1]
            gi = g_sc[i:i + 1, 0:width]
            outs[4 * i][...] = gi
            outs[4 * i + 1][...], outs[4 * i + 2][...], outs[4 * i + 3][...] = _adamw_math(
                gi, ins[3 * i][...], ins[3 * i + 1][...], ins[3 * i + 2][...])

    vmem = pl.BlockSpec(memory_space=pltpu.VMEM)
    args = [a for group in zip(ws, ms, vs) for a in group]
    res = pl.pallas_call(
        body, name=name, in_specs=[pl.BlockSpec(memory_space=pltpu.SMEM)] + [vmem] * (2 + 3 * n), out_specs=[vmem] * (4 * n),
        out_shape=[jax.ShapeDtypeStruct(a.shape, F32) for a in ws for _ in range(4)],
        scratch_shapes=[pltpu.VMEM((r, sent.shape[1]), F32)], compiler_params=_params(),
    )(me, sent, parts, *args)
    return [list(res[4 * i:4 * i + 4]) for i in range(n)]


def _adamw(g, w, m, v, *, name):
    r, cdim = w.shape
    tr, tc = _elementwise_tiles(r, cdim)

    def body(g_ref, w_ref, m_ref, v_ref, d_ref, nm_ref, nv_ref):
        d_ref[...], nm_ref[...], nv_ref[...] = _adamw_math(g_ref[...], w_ref[...], m_ref[...], v_ref[...])

    blk = pl.BlockSpec((tr, tc), lambda i, j: (i, j))
    return list(pl.pallas_call(
        body, name=name, grid=(r // tr, cdim // tc), in_specs=[blk] * 4,
        out_specs=[blk] * 3, out_shape=[jax.ShapeDtypeStruct((r, cdim), F32)] * 3,
        compiler_params=_params(("parallel", "parallel")),
    )(g, w, m, v))


SHARD_ROWS_P = {n: (FF_SHARD_P if 'ffn' in n else IN_SHARD_P if n == 'w_in' else None) for n in SHARDED}


def _to_exchange_layout(name, shard):
    t = shard.T if SHARD_AXIS[name] == 1 else shard
    pad = SHARD_ROWS_P[name]
    return t if pad is None else jnp.pad(t, ((0, pad - t.shape[0]), (0, 0)))


def _expand_w_in(wt):
    wt = wt.reshape(N_DEV, IN_SHARD_P, D)[:, :IN_SHARD].reshape(IN_W, D)
    o = Q_RANK + KV_RANK
    kr1, kr2 = wt[o:o + ROPE // 2], wt[o + ROPE // 2:o + ROPE]
    z = jnp.zeros((LANES - ROPE, D), wt.dtype)
    return jnp.concatenate([wt[:o], wt[o + ROPE:], kr1, kr2, z, -kr2, kr1, z], axis=0)


def _expand_w_uq(wt):
    w = wt.reshape(H, QK, Q_RANK)
    z = jnp.zeros((H, LANES - ROPE, Q_RANK), w.dtype)
    q1, q2 = w[:, NOPE:NOPE + ROPE // 2], w[:, NOPE + ROPE // 2:]
    return jnp.concatenate([w[:, :NOPE].reshape(H * NOPE, Q_RANK),
                            jnp.concatenate([q1, q2, z], axis=1).reshape(H * LANES, Q_RANK),
                            jnp.concatenate([-q2, q1, z], axis=1).reshape(H * LANES, Q_RANK)], axis=0)


def _layout_qk_gain(g):
    g = g.reshape(QK)
    g1, g2, z = g[NOPE:NOPE + ROPE // 2], g[NOPE + ROPE // 2:], jnp.zeros((LANES - ROPE,), g.dtype)
    return jnp.stack([g[:NOPE], jnp.concatenate([g1, g2, z]), jnp.concatenate([g2, g1, z])])


def _rep16(a):
    return jnp.repeat(a, SSM_GRP, axis=0)


def _layout_ssm_in(a_re, a_im, log_dt, b_re, b_im):
    b_r = jnp.transpose(b_re, (0, 2, 1)).reshape(SSM_G * SSM_GRP, SSM_P)
    b_i = jnp.transpose(b_im, (0, 2, 1)).reshape(SSM_G * SSM_GRP, SSM_P)
    ldt = jnp.broadcast_to(log_dt.reshape(SSM_G, 1), (SSM_G, SSM_P))
    return _rep16(a_re), _rep16(a_im), _rep16(ldt), b_r, b_i


def _block_diag_b(bb):
    eye = jnp.eye(SSM_PACK, dtype=bb.dtype)
    b5 = bb.reshape(SSM_G // SSM_PACK, SSM_PACK, SSM_GRP, 1, SSM_P) * eye[None, :, None, :, None]
    return b5.reshape(SSM_G // SSM_PACK, SSM_PACK * SSM_GRP, SSM_PACK * SSM_P)


def _block_diag_c(cc):
    eye = jnp.eye(SSM_PACK, dtype=cc.dtype)
    c5 = jnp.transpose(cc, (0, 2, 1)).reshape(SSM_G // SSM_PACK, SSM_PACK, SSM_P, 1, SSM_GRP) * eye[None, :, None, :, None]
    return c5.reshape(SSM_G // SSM_PACK, SSM_PACK * SSM_P, SSM_PACK * SSM_GRP)


def _time_perm(a, inverse=False):
    s, w = a.shape
    c = SCAN_CHUNKS
    if inverse:
        return jnp.transpose(a.reshape(s // c, c, w), (1, 0, 2)).reshape(s, w)
    return jnp.transpose(a.reshape(c, s // c, w), (1, 0, 2)).reshape(s, w)


class _Weights:
    def __init__(self, groups=(), landed=None, me=None):
        self.groups, self.landed, self.me = list(groups), dict(landed or {}), me

    def get(self, name, after):
        if name not in self.landed:
            names, exchange = next(g for g in self.groups if name in g[0])
            for n, block, gathered in zip(names, *exchange.wait(after)):
                self.landed[n] = _place_own(gathered, block, self.me, name="place_" + n)
        return self.landed[name]

    def __getitem__(self, name):
        return self.landed[name]

    def prefetch(self, name, after):
        for names, exchange in self.groups:
            if name in names and exchange.two_level:
                exchange.forward(after)


def _ffn_gate_up(h, w_gt, w_ut, *, name, tm=512, tn=1408):
    s, k = h.shape
    n = w_gt.shape[0]
    tm, tn = min(tm, s), _tile(n, tn)
    dims = (((1,), (1,)), ((), ()))

    def body(h_ref, wg_ref, wu_ref, g_ref, u_ref, a_ref):
        hb = h_ref[...].astype(BF16)
        gate = lax.dot_general(hb, wg_ref[...], dims, preferred_element_type=F32)
        up = lax.dot_general(hb, wu_ref[...], dims, preferred_element_type=F32)
        g_ref[...] = gate.astype(BF16)
        u_ref[...] = up.astype(BF16)
        a_ref[...] = _f_swiglu(gate, up)

    w_spec = pl.BlockSpec((tn, k), lambda j, i: (j, 0))
    o_spec = pl.BlockSpec((tm, tn), lambda j, i: (i, j))
    return pl.pallas_call(
        body, name=name, grid=(n // tn, s // tm), in_specs=[pl.BlockSpec((tm, k), lambda j, i: (i, 0)), w_spec, w_spec],
        out_specs=[o_spec] * 3, out_shape=[jax.ShapeDtypeStruct((s, n), BF16)] * 3,
        compiler_params=_params(("parallel", "parallel")),
    )(h, w_gt, w_ut)


def _ffn_dgate_dup(dx_out, w_d, gate, up, *, name, tm=512, tn=1408, deps=()):
    s, k = dx_out.shape
    n = w_d.shape[0]
    tm, tn = min(tm, s), _tile(n, tn)
    deps = [d for d in deps if d is not None]

    def body(dx_ref, wd_ref, g_ref, u_ref, *refs):
        dg_ref, du_ref = refs[len(deps):]
        dact = 0.5 * lax.dot_general(dx_ref[...].astype(BF16), wd_ref[...], (((1,), (1,)), ((), ())),
                                     preferred_element_type=F32)
        _, vjp = jax.vjp(_f_swiglu, g_ref[...].astype(F32), u_ref[...].astype(F32))
        dgate, dup = vjp(dact.astype(BF16))
        dg_ref[...] = dgate.astype(BF16)
        du_ref[...] = dup.astype(BF16)

    o_spec = pl.BlockSpec((tm, tn), lambda j, i: (i, j))
    return pl.pallas_call(
        body, name=name, grid=(n // tn, s // tm),
        in_specs=[pl.BlockSpec((tm, k), lambda j, i: (i, 0)), pl.BlockSpec((tn, k), lambda j, i: (j, 0)), o_spec, o_spec]
        + [pl.BlockSpec(d.shape, lambda j, i: (0, 0)) for d in deps],
        out_specs=[o_spec] * 2, out_shape=[jax.ShapeDtypeStruct((s, n), BF16)] * 2,
        compiler_params=_params(("parallel", "parallel")),
    )(dx_out, w_d, gate, up, *deps)


def _ffn_dh(dgate, dup, w_gt, w_ut, *, name, tm=512):
    s, k = dgate.shape
    n = w_gt.shape[1]
    tm = min(tm, s)

    def body(dg_ref, du_ref, wg_ref, wu_ref, o_ref):
        o_ref[...] = (jnp.dot(dg_ref[...], wg_ref[...], preferred_element_type=F32)
                      + jnp.dot(du_ref[...], wu_ref[...], preferred_element_type=F32)).astype(o_ref.dtype)

    a_spec = pl.BlockSpec((tm, k), lambda i: (i, 0))
    w_spec = pl.BlockSpec((k, n), lambda i: (0, 0))
    return pl.pallas_call(
        body, name=name, grid=(s // tm,), in_specs=[a_spec, a_spec, w_spec, w_spec],
        out_specs=pl.BlockSpec((tm, n), lambda i: (i, 0)), out_shape=jax.ShapeDtypeStruct((s, n), BF16),
        compiler_params=_params(("parallel",)),
    )(dgate, dup, w_gt, w_ut)


def _ffn_fwd(x, g, wc, tag, deps=(), prefetch=()):
    h = _rowwise(_f_norm, [x], [g], [(D, BF16)], name=tag + "_norm", deps=deps)[0]
    gate, up, act = _ffn_gate_up(h, wc.get(tag + '_w_gate', h), wc[tag + '_w_up'], name=tag + "_gate_up")
    for later in (tag + '_w_down',) + tuple(prefetch):
        wc.prefetch(later, gate)
    x_out = _mm(act, wc.get(tag + '_w_down', act), res=x, scale=0.5, name=tag + "_down")
    return x_out, (h, gate, up, act)


def _ffn_bwd(x, g, wc, saved, dx_out, tag, send, deps=()):
    h, gate, up, act = saved
    w_gt, w_ut, w_d = (wc.get(tag + n, h) for n in ('_w_gate', '_w_up', '_w_down'))
    d_d = _mm(act, dx_out, ta=True, scale=0.5, out_dtype=GRAD_DTYPE, name=tag + "_dwdown", deps=deps)
    token = send({tag + '_w_down': d_d})
    dgate, dup = _ffn_dgate_dup(dx_out, w_d, gate, up, name=tag + "_dgate_dup", deps=[token])
    d_gt = _mm(dgate, h, ta=True, out_dtype=GRAD_DTYPE, name=tag + "_dwgate")
    token = send({tag + '_w_gate': d_gt})
    d_ut = _mm(dup, h, ta=True, out_dtype=GRAD_DTYPE, name=tag + "_dwup", deps=[token])
    token = send({tag + '_w_up': d_ut})
    dh = _ffn_dh(dgate, dup, w_gt, w_ut, name=tag + "_dh")
    dx, dg = _rowwise_bwd(_f_norm, [x], [g], [dh], row_grads={0: F32}, const_grads=[0], adds={0: dx_out},
                          name=tag + "_norm_bwd", deps=[token])
    return dx, dg


def _local_step(x, mem, cos, sin, target, wc, ws, send, deps=(), send_small=None):
    gs = {}

    x1, sv1 = _ffn_fwd(x, ws['ffn1_norm'], wc, "ffn1", deps=deps, prefetch=('w_in',))

    h2 = _rowwise(_f_norm, [x1], [ws['mix_norm']], [(D, BF16)], name="mix_norm")[0]
    w_in_raw, w_uq_raw = wc.get('w_in', h2), wc.get('mla_w_uq', h2)
    w_in_e = _expand_w_in(w_in_raw)
    w_uq_e = _expand_w_uq(w_uq_raw)
    proj = _mm(h2, w_in_e, tb=True, name="w_in")
    c_q, c_kv = _rowwise(_f_prep1, [proj], [ws['q_norm'], ws['kv_norm']], [(Q_RANK, BF16), (KV_RANK, BF16)], name="mla_prep1")
    qall = _mm(c_q, w_uq_e, tb=True, out_dtype=BF16, name="w_uq")
    kv = _mm(c_kv, wc['mla_w_ukv'], tb=True, out_dtype=BF16, name="w_ukv")
    q, k, v = _prep2_fwd(qall, kv, proj, cos, sin, ws['qk_gq'], ws['qk_gk'])
    o_mla, lse = _attn_fwd(q, k, v)
    wc.prefetch('ffn2_w_gate', lse)

    u = proj[:, Q_RANK + KV_RANK:Q_RANK + KV_RANK + SSM_W]
    u_p = _time_perm(u)
    disc_in = [ws['ssm_lr'], ws['ssm_li'], ws['ssm_ldt'], ws['ssm_br'], ws['ssm_bi']]
    ar16, ai16, bbr, bbi = _rowwise(_f_disc, disc_in, [], [(SSM_P, F32)] * 4, name="s5_disc")
    a_r = ar16[::SSM_GRP].reshape(1, SSM_N)
    a_i = ai16[::SSM_GRP].reshape(1, SSM_N)
    bblk_r, bblk_i = _block_diag_b(bbr).astype(BF16), _block_diag_b(bbi).astype(BF16)
    cblk_r, cblk_i = _block_diag_c(ws['ssm_cr']).astype(BF16), _block_diag_c(-ws['ssm_ci']).astype(BF16)
    xr, xi, yc = _s5_scan(u_p, bblk_r, bblk_i, a_r, a_i, reverse=False, tb=False, readout=(cblk_r, cblk_i),
                          name="s5_scan_fwd")
    g_p = _rowwise(_f_s5_gelu, [yc, u_p], [ws['ssm_d']], [(SSM_W, F32)], name="s5_gelu")[0]
    z_p = _mm(g_p, wc['ssm_w_glu'], name="s5_glu")
    g_t, z_t = _time_perm(g_p, inverse=True), _time_perm(z_p, inverse=True)
    on_consts = [ws['ssm_b_glu'], ws['out_norm_mla'], ws['out_norm_ssm']]
    ycat = _rowwise(_f_outnorm, [o_mla, g_t, z_t], on_consts, [(D, BF16)], name="out_norm")[0]
    x2 = _mm(ycat, wc['w_o'], res=x1, name="w_o")

    hx = _rowwise(_f_norm, [x2], [ws['xattn_norm']], [(D, BF16)], name="xattn_norm")[0]
    xq = _mm(hx, wc['xattn_w_q'], name="xattn_q")
    mn = _rowwise(_f_norm, [mem], [ws['mem_norm']], [(D, BF16)], name="mem_norm")[0]
    kvm = _mm(mn, wc['xattn_w_kv'], name="xattn_kv")
    xkn, xv = _rowwise(_f_memk, [kvm], [ws['xattn_k_norm']], [(H * XH, BF16), (H * XH, BF16)], name="xattn_knorm")
    xo = _xattn_fwd(xq, xkn, xv, ws['xattn_q_norm'])
    x3 = _mm(xo, wc['xattn_w_o'], tb=True, res=x2, name="xattn_o")

    x4, sv2 = _ffn_fwd(x3, ws['ffn2_norm'], wc, "ffn2")

    def f_loss(yb, tb):
        err = yb - tb
        return err * (1.0 / D), jnp.broadcast_to(jnp.sum(jnp.sum(err * err, axis=1, keepdims=True), axis=0, keepdims=True) * (0.5 / D), (1, LANES))

    dx4, loss = _rowwise(f_loss, [x4, target], [], [(D, F32)], [(1, LANES)], name="loss")

    dx3, gs['ffn2_norm'] = _ffn_bwd(x3, ws['ffn2_norm'], wc, sv2, dx4, "ffn2", send)

    dxo = _mm(dx3, wc['xattn_w_o'], out_dtype=BF16, name="xattn_o_dx")
    send({'xattn_w_o': _mm(dx3, xo, ta=True, out_dtype=GRAD_DTYPE, name="xattn_o_dw")})
    dxq, dxkn, dxv, gs['xattn_q_norm'] = _xattn_bwd(xq, xkn, xv, ws['xattn_q_norm'], dxo)
    dkvm, gs['xattn_k_norm'] = _rowwise_bwd(_f_memk, [kvm], [ws['xattn_k_norm']], [dxkn, dxv], row_grads={0: BF16},
                                            const_grads=[0], name="xattn_knorm_bwd")
    send({'xattn_w_kv': _mm(mn, dkvm, ta=True, out_dtype=GRAD_DTYPE, name="xattn_kv_dw")})
    dmn = _mm(dkvm, wc['xattn_w_kv'], tb=True, out_dtype=BF16, name="xattn_kv_dx")
    gs['mem_norm'] = _rowwise_bwd(_f_norm, [mem], [ws['mem_norm']], [dmn], row_grads={}, const_grads=[0], name="mem_norm_bwd")[0]
    token = send({'xattn_w_q': _mm(hx, dxq, ta=True, out_dtype=GRAD_DTYPE, name="xattn_q_dw")})
    dhx = _mm(dxq, wc['xattn_w_q'], tb=True, out_dtype=BF16, name="xattn_q_dx")
    dx2, gs['xattn_norm'] = _rowwise_bwd(_f_norm, [x2], [ws['xattn_norm']], [dhx], row_grads={0: F32}, const_grads=[0],
                                         adds={0: dx3}, name="xattn_norm_bwd", deps=[token])

    dycat = _mm(dx2, wc['w_o'], tb=True, out_dtype=BF16, name="w_o_dx")
    send({'w_o': _mm(ycat, dx2, ta=True, out_dtype=GRAD_DTYPE, name="w_o_dw")})
    do_mla, dg_t, dz_t, gs['ssm_b_glu'], gs['out_norm_mla'], gs['out_norm_ssm'] = _rowwise_bwd(
        _f_outnorm, [o_mla, g_t, z_t], on_consts, [dycat], row_grads={0: F32, 1: F32, 2: BF16}, const_grads=[0, 1, 2],
        name="out_norm_bwd")

    dz_p, dg_p = _time_perm(dz_t), _time_perm(dg_t)
    send({'ssm_w_glu': _mm(g_p, dz_p, ta=True, out_dtype=GRAD_DTYPE, name="s5_glu_dw")})
    dg_p = _mm(dz_p, wc['ssm_w_glu'], tb=True, res=dg_p, name="s5_glu_dx")
    dyc, du_d, gs['ssm_d'] = _rowwise_bwd(_f_s5_gelu, [yc, u_p], [ws['ssm_d']], [dg_p], row_grads={0: BF16, 1: F32},
                                          const_grads=[0], name="s5_gelu_bwd")
    lam_r, lam_i = _s5_scan(dyc, cblk_r, cblk_i, a_r, -a_i, reverse=True, tb=True, name="s5_scan_bwd")
    du_p, d_bblk_r, d_bblk_i, d_cblk_r, d_cblk_i, d_ar, d_ai = _s5_grads(lam_r, lam_i, xr, xi, u_p, dyc, du_d,
                                                                        bblk_r, bblk_i)
    du = _time_perm(du_p, inverse=True)
    gs['ssm_cr'] = jax.linear_transpose(_block_diag_c, ws['ssm_cr'])(d_cblk_r)[0]
    gs['ssm_ci'] = -jax.linear_transpose(_block_diag_c, ws['ssm_ci'])(d_cblk_i)[0]
    d_bbr = jax.linear_transpose(_block_diag_b, bbr)(d_bblk_r)[0]
    d_bbi = jax.linear_transpose(_block_diag_b, bbi)(d_bblk_i)[0]
    d_ar16 = jnp.zeros((SSM_G * SSM_GRP, SSM_P), F32).at[::SSM_GRP].set(d_ar.reshape(SSM_G, SSM_P))
    d_ai16 = jnp.zeros((SSM_G * SSM_GRP, SSM_P), F32).at[::SSM_GRP].set(d_ai.reshape(SSM_G, SSM_P))
    gs['ssm_lr'], gs['ssm_li'], gs['ssm_ldt'], gs['ssm_br'], gs['ssm_bi'] = _rowwise_bwd(
        _f_disc, disc_in, [], [d_ar16, d_ai16, d_bbr, d_bbi], row_grads={i: F32 for i in range(5)}, const_grads=[],
        name="s5_disc_bwd")

    delta, do_b = _rowwise(_f_delta, [do_mla, o_mla], [], [(H * LANES, F32), (H * VD, BF16)], name="mla_delta")
    dq, dk, dv = _attn_bwd(q, k, v, do_b, lse, delta)
    dqall, dkv, dkr, dkrs, gs['qk_gq'], gs['qk_gk'] = _prep2_bwd(qall, kv, proj, cos, sin, ws['qk_gq'], ws['qk_gk'], dq, dk, dv)
    d_w_uq_e = _mm(dqall, c_q, ta=True, name="w_uq_dw")
    send({'mla_w_uq': jax.linear_transpose(_expand_w_uq, jax.ShapeDtypeStruct(w_uq_raw.shape, F32))(d_w_uq_e)[0]})
    dc_q = _mm(dqall, w_uq_e, out_dtype=BF16, name="w_uq_dx")
    send({'mla_w_ukv': _mm(dkv, c_kv, ta=True, out_dtype=GRAD_DTYPE, name="w_ukv_dw")})
    dc_kv = _mm(dkv, wc['mla_w_ukv'], out_dtype=BF16, name="w_ukv_dx")

    def f_prep1_bwd(pb, dcq, dckv, dub, dkrb, dkrsb, gq, gkv):
        _, vjp = jax.vjp(_f_prep1, pb[:, :Q_RANK + KV_RANK], gq, gkv)
        dpa, dgq, dgkv = vjp((dcq.astype(BF16), dckv.astype(BF16)))
        return jnp.concatenate([dpa, dub, dkrb, dkrsb], axis=-1), dgq, dgkv

    dproj, gs['q_norm'], gs['kv_norm'] = _rowwise(
        f_prep1_bwd, [proj, dc_q, dc_kv, du, dkr, dkrs], [ws['q_norm'], ws['kv_norm']], [(IN_WP, BF16)],
        [(1, Q_RANK), (1, KV_RANK)], name="mla_prep1_bwd")
    d_w_in_e = _mm(dproj, h2, ta=True, name="w_in_dw")
    token = send({'w_in': jax.linear_transpose(_expand_w_in, jax.ShapeDtypeStruct(w_in_raw.shape, F32))(d_w_in_e)[0]})
    dh2 = _mm(dproj, w_in_e, out_dtype=BF16, name="w_in_dx")
    dx1, gs['mix_norm'] = _rowwise_bwd(_f_norm, [x1], [ws['mix_norm']], [dh2], row_grads={0: F32}, const_grads=[0],
                                       adds={0: dx2}, name="mix_norm_bwd", deps=[token])

    token = send_small(gs, loss) if send_small is not None else None
    dx0, gs['ffn1_norm'] = _ffn_bwd(x, ws['ffn1_norm'], wc, sv1, dx1, "ffn1", send, deps=[token])
    return loss, dx0, gs


def _prep2_rows(qall, kv, proj, cos, sin):
    return [qall, kv, (proj, KR_BLOCK, LANES), (proj, KR_BLOCK + 1, LANES), cos, sin]


def _prep2_fwd(qall, kv, proj, cos, sin, gq, gk):
    return _rowwise(_f_prep2, _prep2_rows(qall, kv, proj, cos, sin), [gq, gk],
                    [(H * HQ, BF16), (H * HQ, BF16), (H * VD, BF16)], ts=256, name="mla_prep2")


def _prep2_bwd(qall, kv, proj, cos, sin, gq, gk, dq, dk, dv):
    return _rowwise_bwd(_f_prep2, _prep2_rows(qall, kv, proj, cos, sin), [gq, gk], [dq, dk, dv],
                        row_grads={0: BF16, 1: BF16, 2: F32, 3: F32}, const_grads=[0, 1], ts=256, name="mla_prep2_bwd")


def _rope_tables(pos):
    half = ROPE // 2
    inv = ROPE_THETA ** (-jnp.arange(half, dtype=F32) / half)
    ang = pos.astype(F32)[:, None] * inv[None, :]
    z = jnp.zeros((pos.shape[0], LANES - ROPE), F32)
    cos, sin = jnp.cos(ang), jnp.sin(ang)
    return jnp.concatenate([cos, cos, z], axis=-1), jnp.concatenate([sin, sin, z], axis=-1)


def _small_layout(p):
    lr, li, ldt, br, bi = _layout_ssm_in(p['ssm_a_re'], p['ssm_a_im'], p['ssm_log_dt'], p['ssm_b_re'], p['ssm_b_im'])
    return {
        'ffn1_norm': p['ffn1_norm'].reshape(1, D), 'mix_norm': p['mix_norm'].reshape(1, D),
        'q_norm': p['mla_q_norm'].reshape(1, Q_RANK), 'kv_norm': p['mla_kv_norm'].reshape(1, KV_RANK),
        'qk_gq': _layout_qk_gain(p['mla_qk_norm_q']), 'qk_gk': _layout_qk_gain(p['mla_qk_norm_k']),
        'ssm_lr': lr, 'ssm_li': li, 'ssm_ldt': ldt, 'ssm_br': br, 'ssm_bi': bi,
        'ssm_cr': p['ssm_c_re'], 'ssm_ci': p['ssm_c_im'], 'ssm_d': p['ssm_d'].reshape(1, SSM_W),
        'ssm_b_glu': p['ssm_b_glu'].reshape(1, SSM_W),
        'out_norm_mla': p['out_norm_mla'].reshape(1, SSM_W), 'out_norm_ssm': p['out_norm_ssm'].reshape(1, SSM_W),
        'xattn_norm': p['xattn_norm'].reshape(1, D), 'mem_norm': p['mem_norm'].reshape(1, D),
        'xattn_q_norm': p['xattn_q_norm'].reshape(1, XH), 'xattn_k_norm': p['xattn_k_norm'].reshape(1, XH),
        'ffn2_norm': p['ffn2_norm'].reshape(1, D),
    }


def _pack(arrs, rows):
    flat = jnp.concatenate([a.reshape(-1) for a in arrs])
    return jnp.pad(flat, (0, rows * D - flat.shape[0])).reshape(rows, D)


def _unpack(flat, shapes):
    flat = flat.reshape(-1)
    out, off = [], 0
    for sh in shapes:
        n = int(np.prod(sh))
        out.append(flat[off:off + n].reshape(sh))
        off += n
    return out


def kernel(x, mem, positions, ffn1_norm, ffn1_w_gate, ffn1_w_up, ffn1_w_down, mix_norm, w_in, mla_q_norm, mla_w_uq, mla_kv_norm, mla_w_ukv, mla_qk_norm_q, mla_qk_norm_k, ssm_a_re, ssm_a_im, ssm_log_dt, ssm_b_re, ssm_b_im, ssm_c_re, ssm_c_im, ssm_d, ssm_w_glu, ssm_b_glu, out_norm_mla, out_norm_ssm, w_o, xattn_norm, mem_norm, xattn_w_q, xattn_w_kv, xattn_q_norm, xattn_k_norm, xattn_w_o, ffn2_norm, ffn2_w_gate, ffn2_w_up, ffn2_w_down, loss_target, m_ffn1_norm, m_ffn1_w_gate, m_ffn1_w_up, m_ffn1_w_down, m_mix_norm, m_w_in, m_mla_q_norm, m_mla_w_uq, m_mla_kv_norm, m_mla_w_ukv, m_mla_qk_norm_q, m_mla_qk_norm_k, m_ssm_a_re, m_ssm_a_im, m_ssm_log_dt, m_ssm_b_re, m_ssm_b_im, m_ssm_c_re, m_ssm_c_im, m_ssm_d, m_ssm_w_glu, m_ssm_b_glu, m_out_norm_mla, m_out_norm_ssm, m_w_o, m_xattn_norm, m_mem_norm, m_xattn_w_q, m_xattn_w_kv, m_xattn_q_norm, m_xattn_k_norm, m_xattn_w_o, m_ffn2_norm, m_ffn2_w_gate, m_ffn2_w_up, m_ffn2_w_down, v_ffn1_norm, v_ffn1_w_gate, v_ffn1_w_up, v_ffn1_w_down, v_mix_norm, v_w_in, v_mla_q_norm, v_mla_w_uq, v_mla_kv_norm, v_mla_w_ukv, v_mla_qk_norm_q, v_mla_qk_norm_k, v_ssm_a_re, v_ssm_a_im, v_ssm_log_dt, v_ssm_b_re, v_ssm_b_im, v_ssm_c_re, v_ssm_c_im, v_ssm_d, v_ssm_w_glu, v_ssm_b_glu, v_out_norm_mla, v_out_norm_ssm, v_w_o, v_xattn_norm, v_mem_norm, v_xattn_w_q, v_xattn_w_kv, v_xattn_q_norm, v_xattn_k_norm, v_xattn_w_o, v_ffn2_norm, v_ffn2_w_gate, v_ffn2_w_up, v_ffn2_w_down):
    args = dict(locals())
    w = {n: args[n] for n in WEIGHTS}
    mom = {n: args['m_' + n] for n in WEIGHTS}
    var = {n: args['v_' + n] for n in WEIGHTS}
    return _step(x, mem, positions, loss_target, w, mom, var)


GATHER_GROUPS = [('ffn1_gu', ['ffn1_w_gate', 'ffn1_w_up']), ('ffn1_down', ['ffn1_w_down']),
                 ('mix', ['w_in', 'mla_w_uq', 'mla_w_ukv', 'ssm_w_glu', 'w_o', 'xattn_w_q', 'xattn_w_kv', 'xattn_w_o']),
                 ('ffn2', ['ffn2_w_gate', 'ffn2_w_up', 'ffn2_w_down'])]
SCATTER_GROUPS = [('ffn2_down', ['ffn2_w_down']), ('ffn2_gate', ['ffn2_w_gate']), ('ffn2_up', ['ffn2_w_up']),
                  ('xattn', ['xattn_w_o', 'xattn_w_kv', 'xattn_w_q']),
                  ('mix', ['w_o', 'ssm_w_glu', 'mla_w_uq', 'mla_w_ukv', 'w_in']),
                  ('ffn1_down', ['ffn1_w_down']), ('ffn1_gate', ['ffn1_w_gate']), ('ffn1_up', ['ffn1_w_up'])]


def _step(x, mem, positions, loss_target, w, mom, var):
    blocks = {n: _to_exchange_layout(n, w[n][0]).astype(BF16) for n in SHARDED}
    gathers, token = [], None
    for tag, names in GATHER_GROUPS:
        ex = _Exchange([blocks[n] for n in names], [blocks[n].shape[0] for n in names], gather=True,
                       name="gather_" + tag, after=token, two_level=True)
        gathers.append((names, ex))
        token = ex.token
    me = _my_slot()
    wc = _Weights(gathers, me=me)

    rows = {n: (blocks[n].shape[0], blocks[n].shape[0]) for n in SHARDED}
    ready, scatters = {}, []

    def send(grads):
        ready.update({n: g.astype(GRAD_DTYPE) for n, g in grads.items()})
        for tag, names in SCATTER_GROUPS:
            if all(n in ready for n in names) and not any(t == tag for t, _, _ in scatters):
                ex = _Exchange([ready[n] for n in names], [rows[n] for n in names], gather=False, name="scatter_" + tag)
                scatters.append((tag, names, ex))
                return ex.token
        return None

    small = {n: w[n][0] for n in SMALL}
    vectors = [n for n in SMALL if small[n].ndim == 1 and n != 'ffn1_norm']
    tensors = [n for n in SMALL if small[n].ndim > 1]
    tensor_shapes = [small[n].shape for n in tensors]
    rows_v = -(-(len(vectors) + 1) // 8) * 8
    rows_t = -(-sum(int(np.prod(sh)) for sh in tensor_shapes) // (8 * D)) * 8
    small_sent = []

    def row(a):
        a = a.reshape(1, -1)
        return jnp.pad(a, ((0, 0), (0, D - a.shape[1])))

    def small_pack(vecs, tens, loss_row):
        head = [row(a) for a in vecs] + [loss_row, jnp.zeros((rows_v - len(vecs) - 1, D), F32)]
        return jnp.concatenate(head + [_pack(tens, rows_t)], axis=0)

    def send_small(gs, loss):
        known = dict(gs, ffn1_norm=jnp.zeros((1, D), F32))
        g_small = jax.linear_transpose(_small_layout, {n: jax.ShapeDtypeStruct(small[n].shape, F32) for n in SMALL})(known)[0]
        pack = small_pack([g_small[n] for n in vectors], [g_small[n] for n in tensors], row(loss[0, :1]))
        small_sent.append(_Exchange([pack], [(None, rows_v + rows_t)], gather=False, name="scatter_small"))
        return small_sent[0].token

    ws = _small_layout(small)
    cos, sin = _rope_tables(positions[0])
    loss, dx, gs = _local_step(x[0], mem[0], cos, sin, loss_target[0], wc, ws, send, deps=[token], send_small=send_small)
    pad8 = lambda a: jnp.pad(a.reshape(1, D), ((0, 7), (0, 0)))
    last_ex = _Exchange([pad8(gs['ffn1_norm'])], [(None, 8)], gather=False, name="scatter_last")

    out, after = {}, dx
    for _, names, ex in scatters:
        for n, sent, p in zip(names, *ex.wait(after)):
            r = w[n][0].shape[SHARD_AXIS[n]]
            if SHARD_AXIS[n] == 0:
                out[n] = _sum_adamw(me, sent, rows[n][0], p, r, w[n][0], mom[n][0], var[n][0], name="adamw_" + n)
            else:
                g = _sum_adamw(me, sent, rows[n][0], p, r, name="sum_" + n)[0].T
                out[n] = [g] + _adamw(g, w[n][0], mom[n][0], var[n][0], name="adamw_" + n)
        after = out[names[-1]][1]
    sent, p = small_sent[0].wait(after)
    vec_state = [[t[n][0].reshape(1, -1) for n in vectors] + [jnp.zeros((1, 1), F32)] for t in (w, mom, var)]
    vec_out = _sum_adamw_rows(me, sent[0], p[0], *vec_state, name="adamw_vectors")
    loss_total = vec_out[-1][0][0, 0]
    for n, vals in zip(vectors, vec_out):
        out[n] = [o.reshape(small[n].shape) for o in vals]
    state = [jnp.concatenate([jnp.zeros((rows_v, D), F32), _pack([t[n][0] for n in tensors], rows_t)], axis=0)
             for t in (w, mom, var)]
    small_out = _sum_adamw(me, sent[0], None, p[0], rows_v + rows_t, *state, name="adamw_tensors")
    for n, vals in zip(tensors, zip(*[_unpack(flat[rows_v:], tensor_shapes) for flat in small_out])):
        out[n] = vals
    sent, p = last_ex.wait(small_out[1])
    last_out = _sum_adamw(me, sent[0], None, p[0], 8, *[pad8(t['ffn1_norm'][0]) for t in (w, mom, var)], name="adamw_last")
    out['ffn1_norm'] = [o[0] for o in last_out]
    outs = [out[n][i][None] for i in range(4) for n in WEIGHTS]
    return (loss_total, dx[None], *outs)
```

```python
import math

import jax
import jax.numpy as jnp
import numpy as np
from jax import lax
from jax.experimental import pallas as pl
from jax.experimental.pallas import tpu as pltpu

F32 = jnp.float32
BF16 = jnp.bfloat16

N_DEV = 8
D = 1024
D_FF = 2752
D_FFP = 2816
MEM_LEN = 256
H = 4
Q_RANK, KV_RANK, NOPE, ROPE, VD = 384, 256, 128, 64, 128
QK = NOPE + ROPE
HQ = 2 * 128
SSM_W, SSM_G, SSM_GRP, SSM_P = 512, 32, 16, 64
SSM_N = SSM_G * SSM_P
SSM_PACK = 8
IN_W = 1216
IN_WP = 1408
XH = 128
EPS = 1e-6
LN2 = math.log(2.0)
ROPE_THETA = 10000.0
SCAN_CHUNKS = 8
SCAN_UNROLL = 8
ADAM_LR, ADAM_B1, ADAM_B2, ADAM_EPS, ADAM_WD, ADAM_STEP = 0.001, 0.9, 0.999, 1e-08, 0.01, 10

VMEM_LIMIT = 56 * 1024 * 1024
ACC_BYTES = 6 * 1024 * 1024
LANES = 128
BF16_ROWS = 16
GRAD_DTYPE = BF16
FF_SHARD = D_FF // N_DEV
FF_SHARD_P = 352
IN_SHARD = IN_W // N_DEV
IN_SHARD_P = 160

WEIGHTS = ['ffn1_norm', 'ffn1_w_gate', 'ffn1_w_up', 'ffn1_w_down', 'mix_norm', 'w_in', 'mla_q_norm', 'mla_w_uq',
           'mla_kv_norm', 'mla_w_ukv', 'mla_qk_norm_q', 'mla_qk_norm_k', 'ssm_a_re', 'ssm_a_im', 'ssm_log_dt',
           'ssm_b_re', 'ssm_b_im', 'ssm_c_re', 'ssm_c_im', 'ssm_d', 'ssm_w_glu', 'ssm_b_glu', 'out_norm_mla',
           'out_norm_ssm', 'w_o', 'xattn_norm', 'mem_norm', 'xattn_w_q', 'xattn_w_kv', 'xattn_q_norm',
           'xattn_k_norm', 'xattn_w_o', 'ffn2_norm', 'ffn2_w_gate', 'ffn2_w_up', 'ffn2_w_down']
SHARD_AXIS = {'ffn1_w_gate': 1, 'ffn1_w_up': 1, 'ffn1_w_down': 0, 'w_in': 1, 'mla_w_uq': 1, 'mla_w_ukv': 1,
              'ssm_w_glu': 0, 'w_o': 0, 'xattn_w_q': 0, 'xattn_w_kv': 0, 'xattn_w_o': 1,
              'ffn2_w_gate': 1, 'ffn2_w_up': 1, 'ffn2_w_down': 0}
SHARDED = [n for n in WEIGHTS if n in SHARD_AXIS]
SMALL = [n for n in WEIGHTS if n not in SHARD_AXIS]


def _params(sem=None):
    return pltpu.CompilerParams(dimension_semantics=sem, vmem_limit_bytes=VMEM_LIMIT)


def _tile(n, cap):
    if n <= cap:
        return n
    best = n
    for t in range(LANES, cap + 1, LANES):
        if n % t == 0:
            best = t
    return best


def _mm(a, b, *, ta=False, tb=False, out_dtype=F32, res=None, scale=1.0, name, tm_cap=512, tn_cap=1408, tk_cap=2816,
        deps=()):
    m, k = (a.shape[1], a.shape[0]) if ta else a.shape
    k2, n = (b.shape[1], b.shape[0]) if tb else b.shape
    assert k == k2, (a.shape, b.shape, ta, tb)
    if ta:
        tk_cap = min(tk_cap, 512)
        tm_cap = 1408
    tm, tn, tk = _tile(m, tm_cap), _tile(n, tn_cap), _tile(k, tk_cap)
    if tm * tn * 4 > ACC_BYTES:
        tn = _tile(n, max(LANES, ACC_BYTES // (4 * tm) // LANES * LANES))
    nk = k // tk
    dims = (((0 if ta else 1,), (1 if tb else 0,)), ((), ()))
    has_res = res is not None

    deps = [d for d in deps if d is not None]

    def body(*refs):
        a_ref, b_ref = refs[:2]
        r_ref = refs[2] if has_res else None
        o_ref, acc_ref = refs[-2:]
        kk = pl.program_id(2)

        @pl.when(kk == 0)
        def _():
            acc_ref[...] = jnp.zeros_like(acc_ref)

        acc_ref[...] += lax.dot_general(a_ref[...].astype(BF16), b_ref[...].astype(BF16), dims,
                                        preferred_element_type=F32)

        @pl.when(kk == nk - 1)
        def _():
            out = acc_ref[...]
            if scale != 1.0:
                out = out * scale
            if has_res:
                out = out + r_ref[...].astype(F32)
            o_ref[...] = out.astype(o_ref.dtype)

    a_spec = pl.BlockSpec((tk, tm), lambda i, j, kk: (kk, i)) if ta else pl.BlockSpec((tm, tk), lambda i, j, kk: (i, kk))
    b_spec = pl.BlockSpec((tn, tk), lambda i, j, kk: (j, kk)) if tb else pl.BlockSpec((tk, tn), lambda i, j, kk: (kk, j))
    o_spec = pl.BlockSpec((tm, tn), lambda i, j, kk: (i, j))
    in_specs = [a_spec, b_spec] + ([o_spec] if has_res else []) + [pl.BlockSpec(d.shape, lambda i, j, kk: (0, 0)) for d in deps]
    args = (a, b) + ((res,) if has_res else ()) + tuple(deps)
    return pl.pallas_call(
        body, name=name, grid=(m // tm, n // tn, nk), in_specs=in_specs, out_specs=o_spec,
        out_shape=jax.ShapeDtypeStruct((m, n), out_dtype), scratch_shapes=[pltpu.VMEM((tm, tn), F32)],
        compiler_params=_params(("parallel", "parallel", "arbitrary")),
    )(*args)


def _rowwise(fn, rows, consts, outs, accs=(), *, ts=512, name, deps=()):
    rows = [r if isinstance(r, tuple) else (r, 0, r.shape[1]) for r in rows]
    s = rows[0][0].shape[0]
    ts = min(ts, s)
    assert s % ts == 0
    n_rows, n_consts, n_outs = len(rows), len(consts), len(outs)
    deps = [d for d in deps if d is not None]
    consts = list(consts) + deps

    def body(*refs):
        ins = [r[...] for r in refs[:n_rows + n_consts]]
        res = fn(*ins)
        res = tuple(res) if isinstance(res, (tuple, list)) else (res,)
        out_refs = refs[n_rows + len(consts):]
        for o_ref, val in zip(out_refs[:n_outs], res[:n_outs]):
            o_ref[...] = val.astype(o_ref.dtype)
        if accs:
            first = pl.program_id(0) == 0

            @pl.when(first)
            def _():
                for a_ref, val in zip(out_refs[n_outs:], res[n_outs:]):
                    a_ref[...] = val.astype(F32)

            @pl.when(jnp.logical_not(first))
            def _():
                for a_ref, val in zip(out_refs[n_outs:], res[n_outs:]):
                    a_ref[...] += val.astype(F32)

    in_specs = [pl.BlockSpec((ts, width), lambda i, cb=cb: (i, cb)) for _, cb, width in rows]
    in_specs += [pl.BlockSpec(c.shape, lambda i: (0, 0)) for c in consts]
    out_specs = [pl.BlockSpec((ts, w), lambda i: (i, 0)) for w, _ in outs]
    out_specs += [pl.BlockSpec(tuple(sh), lambda i: (0, 0)) for sh in accs]
    out_shape = [jax.ShapeDtypeStruct((s, w), dt) for w, dt in outs]
    out_shape += [jax.ShapeDtypeStruct(tuple(sh), F32) for sh in accs]
    res = pl.pallas_call(
        body, name=name, grid=(s // ts,), in_specs=in_specs, out_specs=out_specs, out_shape=out_shape,
        compiler_params=_params(("arbitrary",)),
    )(*[a for a, _, _ in rows], *consts)
    return res


def _rowwise_bwd(f, rows, consts, cts, *, row_grads, const_grads, adds=None, ts=512, name, deps=()):
    adds = adds or {}
    n_rows, n_consts, n_cts = len(rows), len(consts), len(cts)
    add_keys = sorted(adds)
    rg = sorted(row_grads)
    cg = sorted(const_grads)

    def fn(*args):
        r = args[:n_rows]
        c = args[n_rows:n_rows + n_consts]
        ct = args[n_rows + n_consts:n_rows + n_consts + n_cts]
        extra = args[n_rows + n_consts + n_cts:]
        outs, vjp = jax.vjp(f, *r, *c)
        outs = tuple(outs) if isinstance(outs, (tuple, list)) else (outs,)
        cot = tuple(g.astype(o.dtype) for g, o in zip(ct, outs))
        grads = vjp(cot if len(cot) > 1 else cot[0])
        res = []
        for i in rg:
            g = grads[i].astype(F32)
            if i in adds:
                g = g + extra[add_keys.index(i)].astype(F32)
            res.append(g)
        for i in cg:
            res.append(grads[n_rows + i])
        return tuple(res)

    rows_all = list(rows) + list(cts) + [adds[i] for i in add_keys]
    def fn2(*args):
        nr = len(rows_all)
        rr, cc = args[:nr], args[nr:]
        return fn(*rr[:n_rows], *cc, *rr[n_rows:])

    outs = [(rows[i][2] if isinstance(rows[i], tuple) else rows[i].shape[1], row_grads[i]) for i in rg]
    accs = [consts[i].shape for i in cg]
    return _rowwise(fn2, rows_all, list(consts), outs, accs, ts=ts, name=name, deps=deps)


def _rms(x, g):
    xf = x.astype(F32)
    return xf * lax.rsqrt(jnp.mean(xf * xf, axis=-1, keepdims=True) + EPS) * g.astype(F32)


def _sigmoid(x):
    return 1.0 / (1.0 + jnp.exp(-x))


@jax.custom_jvp
def _sigmoid_fast(x):
    return pl.reciprocal(1.0 + jnp.exp(-x), approx=True)


@_sigmoid_fast.defjvp
def _sigmoid_fast_jvp(primals, tangents):
    s = _sigmoid_fast(primals[0])
    return s, tangents[0] * s * (1.0 - s)


def _f_norm(x, g):
    return _rms(x, g).astype(BF16)


def _f_swiglu(gate, up):
    gate, up = gate.astype(F32), up.astype(F32)
    return (gate * _sigmoid_fast(gate) * up).astype(BF16)


def _f_prep1(proj, gq, gkv):
    return _rms(proj[:, :Q_RANK], gq).astype(BF16), _rms(proj[:, Q_RANK:Q_RANK + KV_RANK], gkv).astype(BF16)


KR_BLOCK = (Q_RANK + KV_RANK + SSM_W) // LANES


def _f_prep2(qall, kv, kr, krs, cos, sin, gq, gk):
    kr, krs = kr.astype(F32), krs.astype(F32)
    k_rot = kr * gk[1:2] * cos + krs * gk[2:3] * sin
    k_ss = jnp.sum(kr * kr, axis=-1, keepdims=True)
    q_scale = QK ** -0.5 / LN2
    qs, ks, vs = [], [], []
    for h in range(H):
        qn = qall[:, h * LANES:(h + 1) * LANES].astype(F32)
        qr = qall[:, (H + h) * LANES:(H + h + 1) * LANES].astype(F32)
        qrs = qall[:, (2 * H + h) * LANES:(2 * H + h + 1) * LANES].astype(F32)
        rstd = lax.rsqrt((jnp.sum(qn * qn, axis=-1, keepdims=True) + jnp.sum(qr * qr, axis=-1, keepdims=True)) / QK + EPS)
        rstd = rstd * q_scale
        qs += [qn * gq[0:1] * rstd, (qr * gq[1:2] * cos + qrs * gq[2:3] * sin) * rstd]
        kn = kv[:, 2 * h * LANES:(2 * h + 1) * LANES].astype(F32)
        rstd_k = lax.rsqrt((jnp.sum(kn * kn, axis=-1, keepdims=True) + k_ss) / QK + EPS)
        ks += [kn * gk[0:1] * rstd_k, k_rot * rstd_k]
        vs.append(kv[:, (2 * h + 1) * LANES:(2 * h + 2) * LANES])
    return (jnp.concatenate(qs, axis=-1).astype(BF16), jnp.concatenate(ks, axis=-1).astype(BF16),
            jnp.concatenate(vs, axis=-1).astype(BF16))


def _gelu(x):
    return 0.5 * x * (1.0 + jnp.tanh(math.sqrt(2.0 / math.pi) * (x + 0.044715 * (x * x * x))))


def _f_s5_gelu(yc, u, d):
    return _gelu(yc.astype(F32) + d * u.astype(F32))


def _f_outnorm(o_mla, g, z, b_glu, g_om, g_os):
    y_ssm = g * _sigmoid(z + b_glu)
    return jnp.concatenate([_rms(o_mla, g_om), _rms(y_ssm, g_os)], axis=-1).astype(BF16)


def _f_memk(kvm, gk):
    ks = [_rms(kvm[:, h * XH:(h + 1) * XH], gk) for h in range(H)]
    return jnp.concatenate(ks, axis=-1).astype(BF16), kvm[:, H * XH:].astype(BF16)


def _f_disc(lr, li, log_dt, br, bi):
    dt = jnp.exp(log_dt)
    decay = jnp.exp(lr * dt)
    ar = decay * jnp.cos(li * dt)
    ai = decay * jnp.sin(li * dt)
    den = lr * lr + li * li
    nr = ar - 1.0
    coef_r = (nr * lr + ai * li) / den
    coef_i = (ai * lr - nr * li) / den
    return ar, ai, coef_r * br - coef_i * bi, coef_r * bi + coef_i * br


def _causal_mask(i, j, tq, tk):
    qpos = i * tq + lax.broadcasted_iota(jnp.int32, (tq, tk), 0)
    kpos = j * tk + lax.broadcasted_iota(jnp.int32, (tq, tk), 1)
    return qpos >= kpos


def _attn_fwd(q, k, v, *, t=512):
    s = q.shape[0]
    t = min(t, s)
    nb = s // t

    def body(q_ref, k_ref, v_ref, o_ref, lse_ref, m_sc, l_sc, acc_sc):
        i, j = pl.program_id(1), pl.program_id(2)

        @pl.when(j == 0)
        def _():
            m_sc[...] = jnp.full_like(m_sc, -jnp.inf)
            l_sc[...] = jnp.zeros_like(l_sc)
            acc_sc[...] = jnp.zeros_like(acc_sc)

        def block(diagonal):
            sc = lax.dot_general(q_ref[...], k_ref[...], (((1,), (1,)), ((), ())), preferred_element_type=F32)
            if diagonal:
                sc = jnp.where(_causal_mask(i, j, t, t), sc, -jnp.inf)
            m_old = m_sc[...]
            m_new = jnp.maximum(m_old, jnp.max(sc, axis=-1, keepdims=True))
            p = jnp.exp2(sc - m_new)
            alpha = jnp.exp2(m_old - m_new)
            l_sc[...] = alpha * l_sc[...] + jnp.sum(p, axis=-1, keepdims=True)
            acc_sc[...] = alpha * acc_sc[...] + jnp.dot(p.astype(BF16), v_ref[...], preferred_element_type=F32)
            m_sc[...] = m_new

        pl.when(j < i)(lambda: block(False))

        @pl.when(j == i)
        def _():
            block(True)
            o_ref[...] = acc_sc[...] / l_sc[...]
            lse_ref[...] = jnp.broadcast_to(m_sc[...] + jnp.log2(l_sc[...]), lse_ref.shape)

    kv_map = lambda h, i, j: (jnp.minimum(j, i), h)
    return pl.pallas_call(
        body, name="mla_attn_fwd", grid=(H, nb, nb),
        in_specs=[pl.BlockSpec((t, HQ), lambda h, i, j: (i, h)), pl.BlockSpec((t, HQ), kv_map),
                  pl.BlockSpec((t, VD), kv_map)],
        out_specs=[pl.BlockSpec((t, VD), lambda h, i, j: (i, h)), pl.BlockSpec((t, LANES), lambda h, i, j: (i, h))],
        out_shape=[jax.ShapeDtypeStruct((s, H * VD), F32), jax.ShapeDtypeStruct((s, H * LANES), F32)],
        scratch_shapes=[pltpu.VMEM((t, 1), F32), pltpu.VMEM((t, 1), F32), pltpu.VMEM((t, VD), F32)],
        compiler_params=_params(("parallel", "parallel", "arbitrary")),
    )(q, k, v)


def _attn_probs(q_ref, k_ref, v_ref, do_ref, lse_ref, dl_ref, i, j, t, diagonal):
    sc = lax.dot_general(q_ref[...], k_ref[...], (((1,), (1,)), ((), ())), preferred_element_type=F32)
    p = jnp.exp2(sc - jnp.tile(lse_ref[...], (1, t // LANES)))
    if diagonal:
        p = jnp.where(_causal_mask(i, j, t, t), p, 0.0)
    dp = lax.dot_general(do_ref[...], v_ref[...], (((1,), (1,)), ((), ())), preferred_element_type=F32)
    ds = p * (dp - jnp.tile(dl_ref[...], (1, t // LANES)))
    return p, ds


def _attn_bwd(q, k, v, do, lse, delta, *, t=512):
    s = q.shape[0]
    t = min(t, s)
    nb = s // t

    def body(q_ref, k_ref, v_ref, do_ref, lse_ref, dl_ref, dq_ref, dk_ref, dv_ref, dk_sc, dv_sc):
        j, i = pl.program_id(1), pl.program_id(2)

        @pl.when(jnp.logical_and(i == 0, j == 0))
        def _():
            dq_ref[...] = jnp.zeros_like(dq_ref)

        @pl.when(i == 0)
        def _():
            dk_sc[...] = jnp.zeros_like(dk_sc)
            dv_sc[...] = jnp.zeros_like(dv_sc)

        def block(diagonal):
            p, ds = _attn_probs(q_ref, k_ref, v_ref, do_ref, lse_ref, dl_ref, i, j, t, diagonal)
            dsb = ds.astype(BF16)
            dv_sc[...] += lax.dot_general(p.astype(BF16), do_ref[...], (((0,), (0,)), ((), ())), preferred_element_type=F32)
            dk_sc[...] += lax.dot_general(dsb, q_ref[...], (((0,), (0,)), ((), ())), preferred_element_type=F32)
            rows = pl.ds(pl.multiple_of(i * t, t), t)
            dq_ref[rows, :] += jnp.dot(dsb, k_ref[...], preferred_element_type=F32)

        pl.when(i > j)(lambda: block(False))
        pl.when(i == j)(lambda: block(True))

        @pl.when(i == nb - 1)
        def _():
            dk_ref[...] = (dk_sc[...] * LN2).astype(dk_ref.dtype)
            dv_ref[...] = dv_sc[...].astype(dv_ref.dtype)

        @pl.when(jnp.logical_and(i == nb - 1, j == nb - 1))
        def _():
            dq_ref[...] = dq_ref[...] * LN2

    q_map = lambda h, j, i: (jnp.maximum(i, j), h)
    kv_map = lambda h, j, i: (j, h)
    dq, dk, dv = pl.pallas_call(
        body, name="mla_attn_bwd", grid=(H, nb, nb),
        in_specs=[pl.BlockSpec((t, HQ), q_map), pl.BlockSpec((t, HQ), kv_map), pl.BlockSpec((t, VD), kv_map),
                  pl.BlockSpec((t, VD), q_map), pl.BlockSpec((t, LANES), q_map), pl.BlockSpec((t, LANES), q_map)],
        out_specs=[pl.BlockSpec((s, HQ), lambda h, j, i: (0, h)), pl.BlockSpec((t, HQ), kv_map), pl.BlockSpec((t, VD), kv_map)],
        out_shape=[jax.ShapeDtypeStruct((s, H * HQ), F32), jax.ShapeDtypeStruct((s, H * HQ), BF16),
                   jax.ShapeDtypeStruct((s, H * VD), BF16)],
        scratch_shapes=[pltpu.VMEM((t, HQ), F32), pltpu.VMEM((t, VD), F32)],
        compiler_params=_params(("parallel", "arbitrary", "arbitrary")),
    )(q, k, v, do, lse, delta)
    return dq, dk, dv


def _f_delta(do, o):
    prod = do.astype(F32) * o.astype(F32)
    parts = [jnp.broadcast_to(jnp.sum(prod[:, h * VD:(h + 1) * VD], axis=-1, keepdims=True), (do.shape[0], LANES))
             for h in range(H)]
    return jnp.concatenate(parts, axis=-1), do.astype(BF16)


def _xattn_head(qh, kh, gq):
    qn = _rms(qh, gq) * (XH ** -0.5)
    sc = lax.dot_general(qn.astype(BF16), kh, (((1,), (1,)), ((), ())), preferred_element_type=F32)
    sc = sc - jnp.max(sc, axis=-1, keepdims=True)
    e = jnp.exp(sc)
    return qn, e / jnp.sum(e, axis=-1, keepdims=True)


def _xattn_fwd(q, kn, v, gq, *, ts=512):
    def fn(qb, knb, vb, g):
        outs = []
        for h in range(H):
            sl = slice(h * XH, (h + 1) * XH)
            _, p = _xattn_head(qb[:, sl], knb[:, sl], g)
            outs.append(jnp.dot(p.astype(BF16), vb[:, sl], preferred_element_type=F32))
        return (jnp.concatenate(outs, axis=-1),)

    return _rowwise(fn, [q], [kn, v, gq], [(H * XH, BF16)], ts=ts, name="xattn_fwd")[0]


def _xattn_bwd(q, kn, v, gq, do, *, ts=512):
    def fn(qb, dob, knb, vb, g):
        dqs, dks, dvs = [], [], []
        dg = jnp.zeros((1, XH), F32)
        for h in range(H):
            sl = slice(h * XH, (h + 1) * XH)
            qh, kh, vh, doh = qb[:, sl], knb[:, sl], vb[:, sl], dob[:, sl].astype(BF16)
            qn, p = _xattn_head(qh, kh, g)
            dp = lax.dot_general(doh, vh, (((1,), (1,)), ((), ())), preferred_element_type=F32)
            dvs.append(lax.dot_general(p.astype(BF16), doh, (((0,), (0,)), ((), ())), preferred_element_type=F32))
            ds = (p * (dp - jnp.sum(dp * p, axis=-1, keepdims=True))).astype(BF16)
            dqn = jnp.dot(ds, kh, preferred_element_type=F32)
            dks.append(lax.dot_general(ds, qn.astype(BF16), (((0,), (0,)), ((), ())), preferred_element_type=F32))
            _, vjp_n = jax.vjp(lambda a, b: _rms(a, b) * (XH ** -0.5), qh, g)
            dqh, dgh = vjp_n(dqn)
            dqs.append(dqh)
            dg = dg + dgh
        return (jnp.concatenate(dqs, axis=-1), jnp.concatenate(dks, axis=-1), jnp.concatenate(dvs, axis=-1), dg)

    return _rowwise(fn, [q, do], [kn, v, gq], [(H * XH, BF16)], [kn.shape, v.shape, gq.shape], ts=ts, name="xattn_bwd")


def _cmul(ar, ai, xr, xi):
    return ar * xr - ai * xi, ar * xi + ai * xr


def _scan_in_place(xr_ref, xi_ref, ar, ai, *, reverse):
    s, cw = xr_ref.shape
    c = SCAN_CHUNKS
    tt = s // c
    a_r = jnp.broadcast_to(ar, (c, cw))
    a_i = jnp.broadcast_to(ai, (c, cw))
    zero = jnp.zeros((c, cw), F32)

    def row(step):
        t = (tt - 1 - step) if reverse else step
        return pl.ds(pl.multiple_of(t * c, c), c)

    def local(step, carry):
        sr, si, qr, qi = carry
        r = row(step)
        nr, ni = _cmul(a_r, a_i, sr, si)
        nr, ni = nr + xr_ref[r, :], ni + xi_ref[r, :]
        xr_ref[r, :] = nr
        xi_ref[r, :] = ni
        return (nr, ni) + _cmul(a_r, a_i, qr, qi)

    end_r, end_i, pr, pi = lax.fori_loop(0, tt, local, (zero, zero, jnp.ones((c, cw), F32), zero), unroll=SCAN_UNROLL)

    rows_id = lax.broadcasted_iota(jnp.int32, (c, cw), 0)
    car_r, car_i = zero, zero
    cur_r, cur_i = jnp.zeros((1, cw), F32), jnp.zeros((1, cw), F32)
    order = range(c - 1, -1, -1) if reverse else range(c)
    for kk in order:
        car_r = jnp.where(rows_id == kk, cur_r, car_r)
        car_i = jnp.where(rows_id == kk, cur_i, car_i)
        nr, ni = _cmul(pr[0:1], pi[0:1], cur_r, cur_i)
        cur_r = nr + end_r[kk:kk + 1]
        cur_i = ni + end_i[kk:kk + 1]

    def fix(step, carry):
        qr, qi = _cmul(a_r, a_i, *carry)
        r = row(step)
        dr, di = _cmul(qr, qi, car_r, car_i)
        xr_ref[r, :] += dr
        xi_ref[r, :] += di
        return qr, qi

    lax.fori_loop(0, tt, fix, (jnp.ones((c, cw), F32), zero), unroll=SCAN_UNROLL)


S5_ROWS = 512


def _s5_scan(v, w_r, w_i, ar, ai, *, reverse, tb, readout=None, name):
    s = v.shape[0]
    g = w_r.shape[0]
    nv, ns = SSM_PACK * SSM_GRP, SSM_PACK * SSM_P
    rows = min(S5_ROWS, s)
    dims = (((1,), (1 if tb else 0,)), ((), ()))
    n_w = 2 if readout is None else 4

    def body(v_ref, ar_ref, ai_ref, *refs):
        w = [r[...] for r in refs[:n_w]]
        xr_ref, xi_ref = refs[n_w:n_w + 2]
        for r0 in range(0, s, rows):
            vb = v_ref[r0:r0 + rows, :].astype(BF16)
            xr_ref[r0:r0 + rows, :] = lax.dot_general(vb, w[0], dims, preferred_element_type=F32)
            xi_ref[r0:r0 + rows, :] = lax.dot_general(vb, w[1], dims, preferred_element_type=F32)
        _scan_in_place(xr_ref, xi_ref, ar_ref[...], ai_ref[...], reverse=reverse)
        if readout is not None:
            y_ref = refs[n_w + 2]
            for r0 in range(0, s, rows):
                y_ref[r0:r0 + rows, :] = (
                    jnp.dot(xr_ref[r0:r0 + rows, :].astype(BF16), w[2], preferred_element_type=F32)
                    + jnp.dot(xi_ref[r0:r0 + rows, :].astype(BF16), w[3], preferred_element_type=F32))

    col = lambda j: (0, j)
    w_spec = lambda a: pl.BlockSpec((None,) + a.shape[1:], lambda j: (j, 0, 0))
    weights = [w_r, w_i] + (list(readout) if readout is not None else [])
    out_specs = [pl.BlockSpec((s, ns), col)] * 2 + ([pl.BlockSpec((s, nv), col)] if readout is not None else [])
    out_shape = [jax.ShapeDtypeStruct((s, g * ns), F32)] * 2 + (
        [jax.ShapeDtypeStruct((s, g * nv), F32)] if readout is not None else [])
    return pl.pallas_call(
        body, name=name, grid=(g,),
        in_specs=[pl.BlockSpec((s, nv), col), pl.BlockSpec((1, ns), col), pl.BlockSpec((1, ns), col)] + [w_spec(a) for a in weights],
        out_specs=out_specs, out_shape=out_shape, compiler_params=_params(("parallel",)),
    )(v, ar, ai, *weights)


def _s5_grads(lam_r, lam_i, xr, xi, u, dyc, du_d, b_r, b_i):
    s = u.shape[0]
    g = b_r.shape[0]
    nv, ns, c = SSM_PACK * SSM_GRP, SSM_PACK * SSM_P, SCAN_CHUNKS
    rows = min(S5_ROWS, s)
    slabs = rows // c
    last_slab = s // c - 1
    nt = (((1,), (1,)), ((), ()))
    tn = (((0,), (0,)), ((), ()))

    def body(lr_ref, li_ref, xr_ref, xi_ref, pr_ref, pi_ref, u_ref, dy_ref, dud_ref, br_ref, bi_ref,
             du_ref, dbr_ref, dbi_ref, dcr_ref, dci_ref, dar_ref, dai_ref):
        first = pl.program_id(1) == 0
        l_r, l_i, x_r, x_i = lr_ref[...], li_ref[...], xr_ref[...], xi_ref[...]
        lrb, lib = l_r.astype(BF16), l_i.astype(BF16)
        du_ref[...] = (dud_ref[...] + lax.dot_general(lrb, br_ref[...], nt, preferred_element_type=F32)
                       + lax.dot_general(lib, bi_ref[...], nt, preferred_element_type=F32))
        ub, dyb = u_ref[...].astype(BF16), dy_ref[...].astype(BF16)
        rows_id = lax.broadcasted_iota(jnp.int32, (c, ns), 0)

        def before(p_ref, x):
            p = p_ref[...]
            p = jnp.where(first, jnp.where(rows_id == 0, 0.0, pltpu.roll(p, 1, 0)), p)
            return jnp.concatenate([p, x[:rows - c]], axis=0)

        xp_r, xp_i = before(pr_ref, x_r), before(pi_ref, x_i)
        parts = (lax.dot_general(ub, lrb, tn, preferred_element_type=F32),
                 lax.dot_general(ub, lib, tn, preferred_element_type=F32),
                 lax.dot_general(x_r.astype(BF16), dyb, tn, preferred_element_type=F32),
                 lax.dot_general(x_i.astype(BF16), dyb, tn, preferred_element_type=F32),
                 jnp.sum(l_r * xp_r + l_i * xp_i, axis=0, keepdims=True),
                 jnp.sum(l_i * xp_r - l_r * xp_i, axis=0, keepdims=True))
        accs = (dbr_ref, dbi_ref, dcr_ref, dci_ref, dar_ref, dai_ref)

        @pl.when(first)
        def _():
            for a_ref, val in zip(accs, parts):
                a_ref[...] = val

        @pl.when(jnp.logical_not(first))
        def _():
            for a_ref, val in zip(accs, parts):
                a_ref[...] += val

    state = pl.BlockSpec((rows, ns), lambda j, k: (k, j))
    chan = pl.BlockSpec((rows, nv), lambda j, k: (k, j))
    slab = pl.BlockSpec((c, ns), lambda j, k: (jnp.where(k == 0, last_slab, k * slabs - 1), j))
    per_b = pl.BlockSpec((None, nv, ns), lambda j, k: (j, 0, 0))
    per_c = pl.BlockSpec((None, ns, nv), lambda j, k: (j, 0, 0))
    per_a = pl.BlockSpec((1, ns), lambda j, k: (0, j))
    return pl.pallas_call(
        body, name="s5_grads", grid=(g, s // rows),
        in_specs=[state, state, state, state, slab, slab, chan, chan, chan, per_b, per_b],
        out_specs=[chan, per_b, per_b, per_c, per_c, per_a, per_a],
        out_shape=[jax.ShapeDtypeStruct((s, g * nv), F32), jax.ShapeDtypeStruct((g, nv, ns), F32),
                   jax.ShapeDtypeStruct((g, nv, ns), F32), jax.ShapeDtypeStruct((g, ns, nv), F32),
                   jax.ShapeDtypeStruct((g, ns, nv), F32), jax.ShapeDtypeStruct((1, g * ns), F32),
                   jax.ShapeDtypeStruct((1, g * ns), F32)],
        compiler_params=_params(("parallel", "arbitrary")),
    )(lam_r, lam_i, xr, xi, xr, xi, u, dyc, du_d, b_r, b_i)


def _mesh_place():
    x, y, c = lax.axis_index("x"), lax.axis_index("y"), lax.axis_index("c")
    peers = []
    for k in range(1, N_DEV):
        px, py, pc = x ^ ((k >> 2) & 1), y ^ ((k >> 1) & 1), c ^ (k & 1)
        peers.append(((px, py, pc), 4 * px + 2 * py + pc))
    return 4 * x + 2 * y + c, peers


class _Exchange:
    SAME_CORE_MASKS = (2, 4, 6)

    def __init__(self, arrays, rows, *, gather, name, after=None, two_level=False):
        self.n_arr, self.rows, self.gather, self.name = len(arrays), rows, gather, name
        self.two_level, self.in_flight = two_level, (1 + len(self.SAME_CORE_MASKS) if two_level else N_DEV - 1)
        n_arr = self.n_arr
        if gather:
            assert all(r % BF16_ROWS == 0 for r in rows)
            lands = [lax.empty((N_DEV * r, a.shape[1]), a.dtype) for a, r in zip(arrays, rows)]
        else:
            lands = [lax.empty((N_DEV - 1,) + (tuple(a.shape) if st is None else (n, a.shape[1])), a.dtype)
                     for a, (st, n) in zip(arrays, rows)]
        has_after = after is not None

        def body(*refs):
            ins, zones = refs[:n_arr], refs[n_arr:2 * n_arr]
            sems = refs[2 * n_arr + has_after:4 * n_arr + has_after]
            token = refs[-1]
            me, peers = _mesh_place()
            for i in range(n_arr):
                for k, (pxyz, pid) in enumerate(peers):
                    if two_level and k + 1 not in (1,) + self.SAME_CORE_MASKS:
                        continue
                    if gather:
                        src = ins[i]
                        dst = zones[i].at[pl.ds(pl.multiple_of(me * rows[i], BF16_ROWS), rows[i])]
                    else:
                        stride, n = rows[i]
                        src = ins[i] if stride is None else ins[i].at[pl.ds(pl.multiple_of(pid * stride, BF16_ROWS), n)]
                        dst = zones[i].at[k]
                    pltpu.make_async_remote_copy(
                        src_ref=src, dst_ref=dst, send_sem=sems[2 * i], recv_sem=sems[2 * i + 1],
                        device_id=pxyz, device_id_type=pl.DeviceIdType.MESH).start()
            token[...] = jnp.zeros_like(token)

        hbm = pl.BlockSpec(memory_space=pltpu.HBM)
        sem = pl.BlockSpec(memory_space=pltpu.SEMAPHORE)
        args = [pltpu.with_memory_space_constraint(a, pltpu.HBM) for a in list(arrays) + lands]
        res = pl.pallas_call(
            body, name=name + "_start",
            in_specs=[hbm] * (2 * n_arr) + ([pl.BlockSpec(memory_space=pl.ANY)] if has_after else []),
            out_specs=[sem] * (2 * n_arr) + [hbm] * (2 * n_arr) + [pl.BlockSpec(memory_space=pltpu.VMEM)],
            out_shape=[pltpu.SemaphoreType.DMA(())] * (2 * n_arr) + [pltpu.HBM(a.shape, a.dtype) for a in args]
            + [jax.ShapeDtypeStruct((8, LANES), F32)],
            input_output_aliases={i: 2 * n_arr + i for i in range(2 * n_arr)},
            compiler_params=pltpu.CompilerParams(has_side_effects=pltpu.SideEffectType.DATAFLOW_SIDE_EFFECTING),
        )(*args, *([after] if has_after else []))
        self.sems, self.thru, self.token = res[:2 * n_arr], res[2 * n_arr:4 * n_arr], res[-1]

    def _wait_all(self, zones, sems):
        myself = (lax.axis_index("x"), lax.axis_index("y"), lax.axis_index("c"))
        for i in range(self.n_arr):
            many = zones[i].at[pl.ds(0, self.in_flight * self.rows[i])] if self.gather else zones[i]
            all_of_them = pltpu.make_async_remote_copy(
                src_ref=many, dst_ref=many, send_sem=sems[2 * i], recv_sem=sems[2 * i + 1],
                device_id=myself, device_id_type=pl.DeviceIdType.MESH)
            all_of_them.wait_recv()
            all_of_them.wait_send()

    def forward(self, after):
        n_arr = self.n_arr

        def body(*refs):
            zones, sems = refs[n_arr:2 * n_arr], refs[2 * n_arr:4 * n_arr]
            new_sems = refs[4 * n_arr + 1:6 * n_arr + 1]
            self._wait_all(zones, sems)
            _, peers = _mesh_place()
            sibling, _ = peers[0]
            for i in range(n_arr):
                for mask in self.SAME_CORE_MASKS:
                    _, pid = peers[mask - 1]
                    block = zones[i].at[pl.ds(pl.multiple_of(pid * self.rows[i], BF16_ROWS), self.rows[i])]
                    pltpu.make_async_remote_copy(
                        src_ref=block, dst_ref=block, send_sem=new_sems[2 * i], recv_sem=new_sems[2 * i + 1],
                        device_id=sibling, device_id_type=pl.DeviceIdType.MESH).start()

        hbm = pl.BlockSpec(memory_space=pltpu.HBM)
        sem = pl.BlockSpec(memory_space=pltpu.SEMAPHORE)
        res = pl.pallas_call(
            body, name=self.name + "_forward",
            in_specs=[hbm] * (2 * n_arr) + [sem] * (2 * n_arr) + [pl.BlockSpec(memory_space=pl.ANY)],
            out_specs=[sem] * (2 * n_arr) + [hbm] * (2 * n_arr),
            out_shape=[pltpu.SemaphoreType.DMA(())] * (2 * n_arr) + [pltpu.HBM(a.shape, a.dtype) for a in self.thru],
            input_output_aliases={i: 2 * n_arr + i for i in range(2 * n_arr)},
            compiler_params=pltpu.CompilerParams(has_side_effects=pltpu.SideEffectType.DATAFLOW_SIDE_EFFECTING),
        )(*self.thru, *self.sems, after)
        self.sems, self.thru = res[:2 * n_arr], res[2 * n_arr:]
        self.two_level, self.in_flight = False, len(self.SAME_CORE_MASKS)

    def wait(self, after):
        n_arr = self.n_arr
        if self.two_level:
            self.forward(after)

        def body(*refs):
            self._wait_all(refs[n_arr:2 * n_arr], refs[2 * n_arr:4 * n_arr])

        hbm = pl.BlockSpec(memory_space=pltpu.HBM)
        sem = pl.BlockSpec(memory_space=pltpu.SEMAPHORE)
        res = pl.pallas_call(
            body, name=self.name + "_wait",
            in_specs=[hbm] * (2 * n_arr) + [sem] * (2 * n_arr) + [pl.BlockSpec(memory_space=pl.ANY)],
            out_specs=[hbm] * (2 * n_arr), out_shape=[pltpu.HBM(a.shape, a.dtype) for a in self.thru],
            input_output_aliases={i: i for i in range(2 * n_arr)},
            compiler_params=pltpu.CompilerParams(has_side_effects=pltpu.SideEffectType.DATAFLOW_SIDE_EFFECTING),
        )(*self.thru, *self.sems, after)
        return res[:n_arr], res[n_arr:]


def _my_slot():
    me = 4 * lax.axis_index("x") + 2 * lax.axis_index("y") + lax.axis_index("c")
    return me.astype(jnp.int32).reshape(1)


def _place_own(gathered, block, me, *, name):
    r, c = block.shape

    def body(me_ref, b_ref, g_ref, o_ref):
        o_ref[...] = b_ref[...]

    return pl.pallas_call(
        body, name=name, out_shape=jax.ShapeDtypeStruct(gathered.shape, gathered.dtype),
        grid_spec=pltpu.PrefetchScalarGridSpec(
            num_scalar_prefetch=1, grid=(1,),
            in_specs=[pl.BlockSpec((r, c), lambda i, me_ref: (0, 0)), pl.BlockSpec(memory_space=pl.ANY)],
            out_specs=pl.BlockSpec((r, c), lambda i, me_ref: (me_ref[0], 0))),
        input_output_aliases={2: 0}, compiler_params=_params(("arbitrary",)),
    )(me, block, gathered)


def _elementwise_tiles(r, c):
    if r % 128 == 0:
        return 128, c
    return r, (256 if c % 256 == 0 else c)


def _adamw_math(g, w, m, v):
    nm = ADAM_B1 * m + (1.0 - ADAM_B1) * g
    nv = ADAM_B2 * v + (1.0 - ADAM_B2) * (g * g)
    m_hat = nm / (1.0 - ADAM_B1 ** ADAM_STEP)
    v_hat = nv / (1.0 - ADAM_B2 ** ADAM_STEP)
    return -ADAM_LR * (m_hat / (jnp.sqrt(v_hat) + ADAM_EPS) + ADAM_WD * w), nm, nv


def _sum_parts(me_ref, own_ref, p_ref, r):
    own = own_ref[...].astype(F32)
    g = None
    for d in range(N_DEV):
        k = jnp.bitwise_xor(me_ref[0], d)
        term = jnp.where(k == 0, own, p_ref[jnp.maximum(k, 1) - 1].astype(F32))
        g = term if g is None else g + term
    return g[0:r, :]


def _sum_adamw(me, sent, stride, parts, r, w=None, m=None, v=None, *, name):
    _, own_rows, cdim = parts.shape
    assert stride is None or stride == own_rows
    tc = 256 if cdim % 256 == 0 else cdim
    update = w is not None

    def body(me_ref, own_ref, p_ref, *refs):
        g = _sum_parts(me_ref, own_ref, p_ref, r)
        if update:
            w_ref, m_ref, v_ref, g_ref, d_ref, nm_ref, nv_ref = refs
            d_ref[...], nm_ref[...], nv_ref[...] = _adamw_math(g, w_ref[...], m_ref[...], v_ref[...])
        else:
            g_ref, = refs
        g_ref[...] = g

    blk = pl.BlockSpec((r, tc), lambda j, me_ref: (0, j))
    own_spec = pl.BlockSpec((own_rows, tc), (lambda j, me_ref: (0, j)) if stride is None else (lambda j, me_ref: (me_ref[0], j)))
    n_out = 4 if update else 1
    res = pl.pallas_call(
        body, name=name, out_shape=[jax.ShapeDtypeStruct((r, cdim), F32)] * n_out,
        grid_spec=pltpu.PrefetchScalarGridSpec(
            num_scalar_prefetch=1, grid=(cdim // tc,),
            in_specs=[own_spec, pl.BlockSpec((N_DEV - 1, own_rows, tc), lambda j, me_ref: (0, 0, j))]
            + ([blk] * 3 if update else []),
            out_specs=[blk] * n_out),
        compiler_params=_params(("parallel",)),
    )(me, sent, parts, *((w, m, v) if update else ()))
    return list(res)


def _adamw(g, w, m, v, *, name):
    r, cdim = w.shape
    tr, tc = _elementwise_tiles(r, cdim)

    def body(g_ref, w_ref, m_ref, v_ref, d_ref, nm_ref, nv_ref):
        d_ref[...], nm_ref[...], nv_ref[...] = _adamw_math(g_ref[...], w_ref[...], m_ref[...], v_ref[...])

    blk = pl.BlockSpec((tr, tc), lambda i, j: (i, j))
    return list(pl.pallas_call(
        body, name=name, grid=(r // tr, cdim // tc), in_specs=[blk] * 4,
        out_specs=[blk] * 3, out_shape=[jax.ShapeDtypeStruct((r, cdim), F32)] * 3,
        compiler_params=_params(("parallel", "parallel")),
    )(g, w, m, v))


SHARD_ROWS_P = {n: (FF_SHARD_P if 'ffn' in n else IN_SHARD_P if n == 'w_in' else None) for n in SHARDED}


def _to_exchange_layout(name, shard):
    t = shard.T if SHARD_AXIS[name] == 1 else shard
    pad = SHARD_ROWS_P[name]
    return t if pad is None else jnp.pad(t, ((0, pad - t.shape[0]), (0, 0)))


def _expand_w_in(wt):
    wt = wt.reshape(N_DEV, IN_SHARD_P, D)[:, :IN_SHARD].reshape(IN_W, D)
    o = Q_RANK + KV_RANK
    kr1, kr2 = wt[o:o + ROPE // 2], wt[o + ROPE // 2:o + ROPE]
    z = jnp.zeros((LANES - ROPE, D), wt.dtype)
    return jnp.concatenate([wt[:o], wt[o + ROPE:], kr1, kr2, z, -kr2, kr1, z], axis=0)


def _expand_w_uq(wt):
    w = wt.reshape(H, QK, Q_RANK)
    z = jnp.zeros((H, LANES - ROPE, Q_RANK), w.dtype)
    q1, q2 = w[:, NOPE:NOPE + ROPE // 2], w[:, NOPE + ROPE // 2:]
    return jnp.concatenate([w[:, :NOPE].reshape(H * NOPE, Q_RANK),
                            jnp.concatenate([q1, q2, z], axis=1).reshape(H * LANES, Q_RANK),
                            jnp.concatenate([-q2, q1, z], axis=1).reshape(H * LANES, Q_RANK)], axis=0)


def _layout_qk_gain(g):
    g = g.reshape(QK)
    g1, g2, z = g[NOPE:NOPE + ROPE // 2], g[NOPE + ROPE // 2:], jnp.zeros((LANES - ROPE,), g.dtype)
    return jnp.stack([g[:NOPE], jnp.concatenate([g1, g2, z]), jnp.concatenate([g2, g1, z])])


def _rep16(a):
    return jnp.repeat(a, SSM_GRP, axis=0)


def _layout_ssm_in(a_re, a_im, log_dt, b_re, b_im):
    b_r = jnp.transpose(b_re, (0, 2, 1)).reshape(SSM_G * SSM_GRP, SSM_P)
    b_i = jnp.transpose(b_im, (0, 2, 1)).reshape(SSM_G * SSM_GRP, SSM_P)
    ldt = jnp.broadcast_to(log_dt.reshape(SSM_G, 1), (SSM_G, SSM_P))
    return _rep16(a_re), _rep16(a_im), _rep16(ldt), b_r, b_i


def _block_diag_b(bb):
    eye = jnp.eye(SSM_PACK, dtype=bb.dtype)
    b5 = bb.reshape(SSM_G // SSM_PACK, SSM_PACK, SSM_GRP, 1, SSM_P) * eye[None, :, None, :, None]
    return b5.reshape(SSM_G // SSM_PACK, SSM_PACK * SSM_GRP, SSM_PACK * SSM_P)


def _block_diag_c(cc):
    eye = jnp.eye(SSM_PACK, dtype=cc.dtype)
    c5 = jnp.transpose(cc, (0, 2, 1)).reshape(SSM_G // SSM_PACK, SSM_PACK, SSM_P, 1, SSM_GRP) * eye[None, :, None, :, None]
    return c5.reshape(SSM_G // SSM_PACK, SSM_PACK * SSM_P, SSM_PACK * SSM_GRP)


def _time_perm(a, inverse=False):
    s, w = a.shape
    c = SCAN_CHUNKS
    if inverse:
        return jnp.transpose(a.reshape(s // c, c, w), (1, 0, 2)).reshape(s, w)
    return jnp.transpose(a.reshape(c, s // c, w), (1, 0, 2)).reshape(s, w)


class _Weights:
    def __init__(self, groups=(), landed=None, me=None):
        self.groups, self.landed, self.me = list(groups), dict(landed or {}), me

    def get(self, name, after):
        if name not in self.landed:
            names, exchange = next(g for g in self.groups if name in g[0])
            for n, block, gathered in zip(names, *exchange.wait(after)):
                self.landed[n] = _place_own(gathered, block, self.me, name="place_" + n)
        return self.landed[name]

    def __getitem__(self, name):
        return self.landed[name]

    def prefetch(self, name, after):
        for names, exchange in self.groups:
            if name in names and exchange.two_level:
                exchange.forward(after)


def _ffn_gate_up(h, w_gt, w_ut, *, name, tm=512, tn=1408):
    s, k = h.shape
    n = w_gt.shape[0]
    tm, tn = min(tm, s), _tile(n, tn)
    dims = (((1,), (1,)), ((), ()))

    def body(h_ref, wg_ref, wu_ref, g_ref, u_ref, a_ref):
        hb = h_ref[...].astype(BF16)
        gate = lax.dot_general(hb, wg_ref[...], dims, preferred_element_type=F32)
        up = lax.dot_general(hb, wu_ref[...], dims, preferred_element_type=F32)
        g_ref[...] = gate.astype(BF16)
        u_ref[...] = up.astype(BF16)
        a_ref[...] = _f_swiglu(gate, up)

    w_spec = pl.BlockSpec((tn, k), lambda j, i: (j, 0))
    o_spec = pl.BlockSpec((tm, tn), lambda j, i: (i, j))
    return pl.pallas_call(
        body, name=name, grid=(n // tn, s // tm), in_specs=[pl.BlockSpec((tm, k), lambda j, i: (i, 0)), w_spec, w_spec],
        out_specs=[o_spec] * 3, out_shape=[jax.ShapeDtypeStruct((s, n), BF16)] * 3,
        compiler_params=_params(("parallel", "parallel")),
    )(h, w_gt, w_ut)


def _ffn_dgate_dup(dx_out, w_d, gate, up, *, name, tm=512, tn=1408, deps=()):
    s, k = dx_out.shape
    n = w_d.shape[0]
    tm, tn = min(tm, s), _tile(n, tn)
    deps = [d for d in deps if d is not None]

    def body(dx_ref, wd_ref, g_ref, u_ref, *refs):
        dg_ref, du_ref = refs[len(deps):]
        dact = 0.5 * lax.dot_general(dx_ref[...].astype(BF16), wd_ref[...], (((1,), (1,)), ((), ())),
                                     preferred_element_type=F32)
        _, vjp = jax.vjp(_f_swiglu, g_ref[...].astype(F32), u_ref[...].astype(F32))
        dgate, dup = vjp(dact.astype(BF16))
        dg_ref[...] = dgate.astype(BF16)
        du_ref[...] = dup.astype(BF16)

    o_spec = pl.BlockSpec((tm, tn), lambda j, i: (i, j))
    return pl.pallas_call(
        body, name=name, grid=(n // tn, s // tm),
        in_specs=[pl.BlockSpec((tm, k), lambda j, i: (i, 0)), pl.BlockSpec((tn, k), lambda j, i: (j, 0)), o_spec, o_spec]
        + [pl.BlockSpec(d.shape, lambda j, i: (0, 0)) for d in deps],
        out_specs=[o_spec] * 2, out_shape=[jax.ShapeDtypeStruct((s, n), BF16)] * 2,
        compiler_params=_params(("parallel", "parallel")),
    )(dx_out, w_d, gate, up, *deps)


def _ffn_dh(dgate, dup, w_gt, w_ut, *, name, tm=512):
    s, k = dgate.shape
    n = w_gt.shape[1]
    tm = min(tm, s)

    def body(dg_ref, du_ref, wg_ref, wu_ref, o_ref):
        o_ref[...] = (jnp.dot(dg_ref[...], wg_ref[...], preferred_element_type=F32)
                      + jnp.dot(du_ref[...], wu_ref[...], preferred_element_type=F32)).astype(o_ref.dtype)

    a_spec = pl.BlockSpec((tm, k), lambda i: (i, 0))
    w_spec = pl.BlockSpec((k, n), lambda i: (0, 0))
    return pl.pallas_call(
        body, name=name, grid=(s // tm,), in_specs=[a_spec, a_spec, w_spec, w_spec],
        out_specs=pl.BlockSpec((tm, n), lambda i: (i, 0)), out_shape=jax.ShapeDtypeStruct((s, n), BF16),
        compiler_params=_params(("parallel",)),
    )(dgate, dup, w_gt, w_ut)


def _ffn_fwd(x, g, wc, tag, deps=(), prefetch=()):
    h = _rowwise(_f_norm, [x], [g], [(D, BF16)], name=tag + "_norm", deps=deps)[0]
    gate, up, act = _ffn_gate_up(h, wc.get(tag + '_w_gate', h), wc[tag + '_w_up'], name=tag + "_gate_up")
    for later in (tag + '_w_down',) + tuple(prefetch):
        wc.prefetch(later, gate)
    x_out = _mm(act, wc.get(tag + '_w_down', act), res=x, scale=0.5, name=tag + "_down")
    return x_out, (h, gate, up, act)


def _ffn_bwd(x, g, wc, saved, dx_out, tag, send, deps=()):
    h, gate, up, act = saved
    w_gt, w_ut, w_d = (wc.get(tag + n, h) for n in ('_w_gate', '_w_up', '_w_down'))
    d_d = _mm(act, dx_out, ta=True, scale=0.5, out_dtype=GRAD_DTYPE, name=tag + "_dwdown", deps=deps)
    token = send({tag + '_w_down': d_d})
    dgate, dup = _ffn_dgate_dup(dx_out, w_d, gate, up, name=tag + "_dgate_dup", deps=[token])
    d_gt = _mm(dgate, h, ta=True, out_dtype=GRAD_DTYPE, name=tag + "_dwgate")
    token = send({tag + '_w_gate': d_gt})
    d_ut = _mm(dup, h, ta=True, out_dtype=GRAD_DTYPE, name=tag + "_dwup", deps=[token])
    token = send({tag + '_w_up': d_ut})
    dh = _ffn_dh(dgate, dup, w_gt, w_ut, name=tag + "_dh")
    dx, dg = _rowwise_bwd(_f_norm, [x], [g], [dh], row_grads={0: F32}, const_grads=[0], adds={0: dx_out},
                          name=tag + "_norm_bwd", deps=[token])
    return dx, dg


def _local_step(x, mem, cos, sin, target, wc, ws, send, deps=(), send_small=None):
    gs = {}

    x1, sv1 = _ffn_fwd(x, ws['ffn1_norm'], wc, "ffn1", deps=deps, prefetch=('w_in',))

    h2 = _rowwise(_f_norm, [x1], [ws['mix_norm']], [(D, BF16)], name="mix_norm")[0]
    w_in_raw, w_uq_raw = wc.get('w_in', h2), wc.get('mla_w_uq', h2)
    w_in_e = _expand_w_in(w_in_raw)
    w_uq_e = _expand_w_uq(w_uq_raw)
    proj = _mm(h2, w_in_e, tb=True, name="w_in")
    c_q, c_kv = _rowwise(_f_prep1, [proj], [ws['q_norm'], ws['kv_norm']], [(Q_RANK, BF16), (KV_RANK, BF16)], name="mla_prep1")
    qall = _mm(c_q, w_uq_e, tb=True, out_dtype=BF16, name="w_uq")
    kv = _mm(c_kv, wc['mla_w_ukv'], tb=True, out_dtype=BF16, name="w_ukv")
    q, k, v = _prep2_fwd(qall, kv, proj, cos, sin, ws['qk_gq'], ws['qk_gk'])
    o_mla, lse = _attn_fwd(q, k, v)
    wc.prefetch('ffn2_w_gate', lse)

    u = proj[:, Q_RANK + KV_RANK:Q_RANK + KV_RANK + SSM_W]
    u_p = _time_perm(u)
    disc_in = [ws['ssm_lr'], ws['ssm_li'], ws['ssm_ldt'], ws['ssm_br'], ws['ssm_bi']]
    ar16, ai16, bbr, bbi = _rowwise(_f_disc, disc_in, [], [(SSM_P, F32)] * 4, name="s5_disc")
    a_r = ar16[::SSM_GRP].reshape(1, SSM_N)
    a_i = ai16[::SSM_GRP].reshape(1, SSM_N)
    bblk_r, bblk_i = _block_diag_b(bbr).astype(BF16), _block_diag_b(bbi).astype(BF16)
    cblk_r, cblk_i = _block_diag_c(ws['ssm_cr']).astype(BF16), _block_diag_c(-ws['ssm_ci']).astype(BF16)
    xr, xi, yc = _s5_scan(u_p, bblk_r, bblk_i, a_r, a_i, reverse=False, tb=False, readout=(cblk_r, cblk_i),
                          name="s5_scan_fwd")
    g_p = _rowwise(_f_s5_gelu, [yc, u_p], [ws['ssm_d']], [(SSM_W, F32)], name="s5_gelu")[0]
    z_p = _mm(g_p, wc['ssm_w_glu'], name="s5_glu")
    g_t, z_t = _time_perm(g_p, inverse=True), _time_perm(z_p, inverse=True)
    on_consts = [ws['ssm_b_glu'], ws['out_norm_mla'], ws['out_norm_ssm']]
    ycat = _rowwise(_f_outnorm, [o_mla, g_t, z_t], on_consts, [(D, BF16)], name="out_norm")[0]
    x2 = _mm(ycat, wc['w_o'], res=x1, name="w_o")

    hx = _rowwise(_f_norm, [x2], [ws['xattn_norm']], [(D, BF16)], name="xattn_norm")[0]
    xq = _mm(hx, wc['xattn_w_q'], name="xattn_q")
    mn = _rowwise(_f_norm, [mem], [ws['mem_norm']], [(D, BF16)], name="mem_norm")[0]
    kvm = _mm(mn, wc['xattn_w_kv'], name="xattn_kv")
    xkn, xv = _rowwise(_f_memk, [kvm], [ws['xattn_k_norm']], [(H * XH, BF16), (H * XH, BF16)], name="xattn_knorm")
    xo = _xattn_fwd(xq, xkn, xv, ws['xattn_q_norm'])
    x3 = _mm(xo, wc['xattn_w_o'], tb=True, res=x2, name="xattn_o")

    x4, sv2 = _ffn_fwd(x3, ws['ffn2_norm'], wc, "ffn2")

    def f_loss(yb, tb):
        err = yb - tb
        return err * (1.0 / D), jnp.broadcast_to(jnp.sum(jnp.sum(err * err, axis=1, keepdims=True), axis=0, keepdims=True) * (0.5 / D), (1, LANES))

    dx4, loss = _rowwise(f_loss, [x4, target], [], [(D, F32)], [(1, LANES)], name="loss")

    dx3, gs['ffn2_norm'] = _ffn_bwd(x3, ws['ffn2_norm'], wc, sv2, dx4, "ffn2", send)

    dxo = _mm(dx3, wc['xattn_w_o'], out_dtype=BF16, name="xattn_o_dx")
    send({'xattn_w_o': _mm(dx3, xo, ta=True, out_dtype=GRAD_DTYPE, name="xattn_o_dw")})
    dxq, dxkn, dxv, gs['xattn_q_norm'] = _xattn_bwd(xq, xkn, xv, ws['xattn_q_norm'], dxo)
    dkvm, gs['xattn_k_norm'] = _rowwise_bwd(_f_memk, [kvm], [ws['xattn_k_norm']], [dxkn, dxv], row_grads={0: BF16},
                                            const_grads=[0], name="xattn_knorm_bwd")
    send({'xattn_w_kv': _mm(mn, dkvm, ta=True, out_dtype=GRAD_DTYPE, name="xattn_kv_dw")})
    dmn = _mm(dkvm, wc['xattn_w_kv'], tb=True, out_dtype=BF16, name="xattn_kv_dx")
    gs['mem_norm'] = _rowwise_bwd(_f_norm, [mem], [ws['mem_norm']], [dmn], row_grads={}, const_grads=[0], name="mem_norm_bwd")[0]
    token = send({'xattn_w_q': _mm(hx, dxq, ta=True, out_dtype=GRAD_DTYPE, name="xattn_q_dw")})
    dhx = _mm(dxq, wc['xattn_w_q'], tb=True, out_dtype=BF16, name="xattn_q_dx")
    dx2, gs['xattn_norm'] = _rowwise_bwd(_f_norm, [x2], [ws['xattn_norm']], [dhx], row_grads={0: F32}, const_grads=[0],
                                         adds={0: dx3}, name="xattn_norm_bwd", deps=[token])

    dycat = _mm(dx2, wc['w_o'], tb=True, out_dtype=BF16, name="w_o_dx")
    send({'w_o': _mm(ycat, dx2, ta=True, out_dtype=GRAD_DTYPE, name="w_o_dw")})
    do_mla, dg_t, dz_t, gs['ssm_b_glu'], gs['out_norm_mla'], gs['out_norm_ssm'] = _rowwise_bwd(
        _f_outnorm, [o_mla, g_t, z_t], on_consts, [dycat], row_grads={0: F32, 1: F32, 2: BF16}, const_grads=[0, 1, 2],
        name="out_norm_bwd")

    dz_p, dg_p = _time_perm(dz_t), _time_perm(dg_t)
    send({'ssm_w_glu': _mm(g_p, dz_p, ta=True, out_dtype=GRAD_DTYPE, name="s5_glu_dw")})
    dg_p = _mm(dz_p, wc['ssm_w_glu'], tb=True, res=dg_p, name="s5_glu_dx")
    dyc, du_d, gs['ssm_d'] = _rowwise_bwd(_f_s5_gelu, [yc, u_p], [ws['ssm_d']], [dg_p], row_grads={0: BF16, 1: F32},
                                          const_grads=[0], name="s5_gelu_bwd")
    lam_r, lam_i = _s5_scan(dyc, cblk_r, cblk_i, a_r, -a_i, reverse=True, tb=True, name="s5_scan_bwd")
    du_p, d_bblk_r, d_bblk_i, d_cblk_r, d_cblk_i, d_ar, d_ai = _s5_grads(lam_r, lam_i, xr, xi, u_p, dyc, du_d,
                                                                        bblk_r, bblk_i)
    du = _time_perm(du_p, inverse=True)
    gs['ssm_cr'] = jax.linear_transpose(_block_diag_c, ws['ssm_cr'])(d_cblk_r)[0]
    gs['ssm_ci'] = -jax.linear_transpose(_block_diag_c, ws['ssm_ci'])(d_cblk_i)[0]
    d_bbr = jax.linear_transpose(_block_diag_b, bbr)(d_bblk_r)[0]
    d_bbi = jax.linear_transpose(_block_diag_b, bbi)(d_bblk_i)[0]
    d_ar16 = jnp.zeros((SSM_G * SSM_GRP, SSM_P), F32).at[::SSM_GRP].set(d_ar.reshape(SSM_G, SSM_P))
    d_ai16 = jnp.zeros((SSM_G * SSM_GRP, SSM_P), F32).at[::SSM_GRP].set(d_ai.reshape(SSM_G, SSM_P))
    gs['ssm_lr'], gs['ssm_li'], gs['ssm_ldt'], gs['ssm_br'], gs['ssm_bi'] = _rowwise_bwd(
        _f_disc, disc_in, [], [d_ar16, d_ai16, d_bbr, d_bbi], row_grads={i: F32 for i in range(5)}, const_grads=[],
        name="s5_disc_bwd")

    delta, do_b = _rowwise(_f_delta, [do_mla, o_mla], [], [(H * LANES, F32), (H * VD, BF16)], name="mla_delta")
    dq, dk, dv = _attn_bwd(q, k, v, do_b, lse, delta)
    dqall, dkv, dkr, dkrs, gs['qk_gq'], gs['qk_gk'] = _prep2_bwd(qall, kv, proj, cos, sin, ws['qk_gq'], ws['qk_gk'], dq, dk, dv)
    d_w_uq_e = _mm(dqall, c_q, ta=True, name="w_uq_dw")
    send({'mla_w_uq': jax.linear_transpose(_expand_w_uq, jax.ShapeDtypeStruct(w_uq_raw.shape, F32))(d_w_uq_e)[0]})
    dc_q = _mm(dqall, w_uq_e, out_dtype=BF16, name="w_uq_dx")
    send({'mla_w_ukv': _mm(dkv, c_kv, ta=True, out_dtype=GRAD_DTYPE, name="w_ukv_dw")})
    dc_kv = _mm(dkv, wc['mla_w_ukv'], out_dtype=BF16, name="w_ukv_dx")

    def f_prep1_bwd(pb, dcq, dckv, dub, dkrb, dkrsb, gq, gkv):
        _, vjp = jax.vjp(_f_prep1, pb[:, :Q_RANK + KV_RANK], gq, gkv)
        dpa, dgq, dgkv = vjp((dcq.astype(BF16), dckv.astype(BF16)))
        return jnp.concatenate([dpa, dub, dkrb, dkrsb], axis=-1), dgq, dgkv

    dproj, gs['q_norm'], gs['kv_norm'] = _rowwise(
        f_prep1_bwd, [proj, dc_q, dc_kv, du, dkr, dkrs], [ws['q_norm'], ws['kv_norm']], [(IN_WP, BF16)],
        [(1, Q_RANK), (1, KV_RANK)], name="mla_prep1_bwd")
    d_w_in_e = _mm(dproj, h2, ta=True, name="w_in_dw")
    token = send({'w_in': jax.linear_transpose(_expand_w_in, jax.ShapeDtypeStruct(w_in_raw.shape, F32))(d_w_in_e)[0]})
    dh2 = _mm(dproj, w_in_e, out_dtype=BF16, name="w_in_dx")
    dx1, gs['mix_norm'] = _rowwise_bwd(_f_norm, [x1], [ws['mix_norm']], [dh2], row_grads={0: F32}, const_grads=[0],
                                       adds={0: dx2}, name="mix_norm_bwd", deps=[token])

    token = send_small(gs, loss) if send_small is not None else None
    dx0, gs['ffn1_norm'] = _ffn_bwd(x, ws['ffn1_norm'], wc, sv1, dx1, "ffn1", send, deps=[token])
    return loss, dx0, gs


def _prep2_rows(qall, kv, proj, cos, sin):
    return [qall, kv, (proj, KR_BLOCK, LANES), (proj, KR_BLOCK + 1, LANES), cos, sin]


def _prep2_fwd(qall, kv, proj, cos, sin, gq, gk):
    return _rowwise(_f_prep2, _prep2_rows(qall, kv, proj, cos, sin), [gq, gk],
                    [(H * HQ, BF16), (H * HQ, BF16), (H * VD, BF16)], ts=256, name="mla_prep2")


def _prep2_bwd(qall, kv, proj, cos, sin, gq, gk, dq, dk, dv):
    return _rowwise_bwd(_f_prep2, _prep2_rows(qall, kv, proj, cos, sin), [gq, gk], [dq, dk, dv],
                        row_grads={0: BF16, 1: BF16, 2: F32, 3: F32}, const_grads=[0, 1], ts=256, name="mla_prep2_bwd")


def _rope_tables(pos):
    half = ROPE // 2
    inv = ROPE_THETA ** (-jnp.arange(half, dtype=F32) / half)
    ang = pos.astype(F32)[:, None] * inv[None, :]
    z = jnp.zeros((pos.shape[0], LANES - ROPE), F32)
    cos, sin = jnp.cos(ang), jnp.sin(ang)
    return jnp.concatenate([cos, cos, z], axis=-1), jnp.concatenate([sin, sin, z], axis=-1)


def _small_layout(p):
    lr, li, ldt, br, bi = _layout_ssm_in(p['ssm_a_re'], p['ssm_a_im'], p['ssm_log_dt'], p['ssm_b_re'], p['ssm_b_im'])
    return {
        'ffn1_norm': p['ffn1_norm'].reshape(1, D), 'mix_norm': p['mix_norm'].reshape(1, D),
        'q_norm': p['mla_q_norm'].reshape(1, Q_RANK), 'kv_norm': p['mla_kv_norm'].reshape(1, KV_RANK),
        'qk_gq': _layout_qk_gain(p['mla_qk_norm_q']), 'qk_gk': _layout_qk_gain(p['mla_qk_norm_k']),
        'ssm_lr': lr, 'ssm_li': li, 'ssm_ldt': ldt, 'ssm_br': br, 'ssm_bi': bi,
        'ssm_cr': p['ssm_c_re'], 'ssm_ci': p['ssm_c_im'], 'ssm_d': p['ssm_d'].reshape(1, SSM_W),
        'ssm_b_glu': p['ssm_b_glu'].reshape(1, SSM_W),
        'out_norm_mla': p['out_norm_mla'].reshape(1, SSM_W), 'out_norm_ssm': p['out_norm_ssm'].reshape(1, SSM_W),
        'xattn_norm': p['xattn_norm'].reshape(1, D), 'mem_norm': p['mem_norm'].reshape(1, D),
        'xattn_q_norm': p['xattn_q_norm'].reshape(1, XH), 'xattn_k_norm': p['xattn_k_norm'].reshape(1, XH),
        'ffn2_norm': p['ffn2_norm'].reshape(1, D),
    }


def _pack(arrs, rows):
    flat = jnp.concatenate([a.reshape(-1) for a in arrs])
    return jnp.pad(flat, (0, rows * D - flat.shape[0])).reshape(rows, D)


def _unpack(flat, shapes):
    flat = flat.reshape(-1)
    out, off = [], 0
    for sh in shapes:
        n = int(np.prod(sh))
        out.append(flat[off:off + n].reshape(sh))
        off += n
    return out


def kernel(x, mem, positions, ffn1_norm, ffn1_w_gate, ffn1_w_up, ffn1_w_down, mix_norm, w_in, mla_q_norm, mla_w_uq, mla_kv_norm, mla_w_ukv, mla_qk_norm_q, mla_qk_norm_k, ssm_a_re, ssm_a_im, ssm_log_dt, ssm_b_re, ssm_b_im, ssm_c_re, ssm_c_im, ssm_d, ssm_w_glu, ssm_b_glu, out_norm_mla, out_norm_ssm, w_o, xattn_norm, mem_norm, xattn_w_q, xattn_w_kv, xattn_q_norm, xattn_k_norm, xattn_w_o, ffn2_norm, ffn2_w_gate, ffn2_w_up, ffn2_w_down, loss_target, m_ffn1_norm, m_ffn1_w_gate, m_ffn1_w_up, m_ffn1_w_down, m_mix_norm, m_w_in, m_mla_q_norm, m_mla_w_uq, m_mla_kv_norm, m_mla_w_ukv, m_mla_qk_norm_q, m_mla_qk_norm_k, m_ssm_a_re, m_ssm_a_im, m_ssm_log_dt, m_ssm_b_re, m_ssm_b_im, m_ssm_c_re, m_ssm_c_im, m_ssm_d, m_ssm_w_glu, m_ssm_b_glu, m_out_norm_mla, m_out_norm_ssm, m_w_o, m_xattn_norm, m_mem_norm, m_xattn_w_q, m_xattn_w_kv, m_xattn_q_norm, m_xattn_k_norm, m_xattn_w_o, m_ffn2_norm, m_ffn2_w_gate, m_ffn2_w_up, m_ffn2_w_down, v_ffn1_norm, v_ffn1_w_gate, v_ffn1_w_up, v_ffn1_w_down, v_mix_norm, v_w_in, v_mla_q_norm, v_mla_w_uq, v_mla_kv_norm, v_mla_w_ukv, v_mla_qk_norm_q, v_mla_qk_norm_k, v_ssm_a_re, v_ssm_a_im, v_ssm_log_dt, v_ssm_b_re, v_ssm_b_im, v_ssm_c_re, v_ssm_c_im, v_ssm_d, v_ssm_w_glu, v_ssm_b_glu, v_out_norm_mla, v_out_norm_ssm, v_w_o, v_xattn_norm, v_mem_norm, v_xattn_w_q, v_xattn_w_kv, v_xattn_q_norm, v_xattn_k_norm, v_xattn_w_o, v_ffn2_norm, v_ffn2_w_gate, v_ffn2_w_up, v_ffn2_w_down):
    args = dict(locals())
    w = {n: args[n] for n in WEIGHTS}
    mom = {n: args['m_' + n] for n in WEIGHTS}
    var = {n: args['v_' + n] for n in WEIGHTS}
    return _step(x, mem, positions, loss_target, w, mom, var)


GATHER_GROUPS = [('ffn1_gu', ['ffn1_w_gate', 'ffn1_w_up']), ('ffn1_down', ['ffn1_w_down']),
                 ('mix', ['w_in', 'mla_w_uq', 'mla_w_ukv', 'ssm_w_glu', 'w_o', 'xattn_w_q', 'xattn_w_kv', 'xattn_w_o']),
                 ('ffn2', ['ffn2_w_gate', 'ffn2_w_up', 'ffn2_w_down'])]
SCATTER_GROUPS = [('ffn2_down', ['ffn2_w_down']), ('ffn2_gate', ['ffn2_w_gate']), ('ffn2_up', ['ffn2_w_up']),
                  ('xattn', ['xattn_w_o', 'xattn_w_kv', 'xattn_w_q']),
                  ('mix', ['w_o', 'ssm_w_glu', 'mla_w_uq', 'mla_w_ukv', 'w_in']),
                  ('ffn1_down', ['ffn1_w_down']), ('ffn1_gate', ['ffn1_w_gate']), ('ffn1_up', ['ffn1_w_up'])]


def _step(x, mem, positions, loss_target, w, mom, var):
    blocks = {n: _to_exchange_layout(n, w[n][0]).astype(BF16) for n in SHARDED}
    gathers, token = [], None
    for tag, names in GATHER_GROUPS:
        ex = _Exchange([blocks[n] for n in names], [blocks[n].shape[0] for n in names], gather=True,
                       name="gather_" + tag, after=token, two_level=True)
        gathers.append((names, ex))
        token = ex.token
    me = _my_slot()
    wc = _Weights(gathers, me=me)

    rows = {n: (blocks[n].shape[0], blocks[n].shape[0]) for n in SHARDED}
    ready, scatters = {}, []

    def send(grads):
        ready.update({n: g.astype(GRAD_DTYPE) for n, g in grads.items()})
        for tag, names in SCATTER_GROUPS:
            if all(n in ready for n in names) and not any(t == tag for t, _, _ in scatters):
                ex = _Exchange([ready[n] for n in names], [rows[n] for n in names], gather=False, name="scatter_" + tag)
                scatters.append((tag, names, ex))
                return ex.token
        return None

    small = {n: w[n][0] for n in SMALL}
    small_shapes = [small[n].shape for n in SMALL]
    n_small = sum(int(np.prod(sh)) for sh in small_shapes) + 1
    rows_small = -(-n_small // (8 * D)) * 8
    small_sent = []

    def send_small(gs, loss):
        known = dict(gs, ffn1_norm=jnp.zeros((1, D), F32))
        g_small = jax.linear_transpose(_small_layout, {n: jax.ShapeDtypeStruct(small[n].shape, F32) for n in SMALL})(known)[0]
        pack = _pack([g_small[n] for n in SMALL] + [loss[0, :1]], rows_small)
        small_sent.append(_Exchange([pack], [(None, rows_small)], gather=False, name="scatter_small"))
        return small_sent[0].token

    ws = _small_layout(small)
    cos, sin = _rope_tables(positions[0])
    loss, dx, gs = _local_step(x[0], mem[0], cos, sin, loss_target[0], wc, ws, send, deps=[token], send_small=send_small)
    pad8 = lambda a: jnp.pad(a.reshape(1, D), ((0, 7), (0, 0)))
    last_ex = _Exchange([pad8(gs['ffn1_norm'])], [(None, 8)], gather=False, name="scatter_last")

    out, after = {}, dx
    for _, names, ex in scatters:
        for n, sent, p in zip(names, *ex.wait(after)):
            r = w[n][0].shape[SHARD_AXIS[n]]
            if SHARD_AXIS[n] == 0:
                out[n] = _sum_adamw(me, sent, rows[n][0], p, r, w[n][0], mom[n][0], var[n][0], name="adamw_" + n)
            else:
                g = _sum_adamw(me, sent, rows[n][0], p, r, name="sum_" + n)[0].T
                out[n] = [g] + _adamw(g, w[n][0], mom[n][0], var[n][0], name="adamw_" + n)
        after = out[names[-1]][1]
    state = [_pack([t[n][0] for n in SMALL], rows_small) for t in (w, mom, var)]
    sent, p = small_sent[0].wait(after)
    small_out = _sum_adamw(me, sent[0], None, p[0], rows_small, *state, name="adamw_small")
    loss_total = small_out[0].reshape(-1)[n_small - 1]
    for n, vals in zip(SMALL, zip(*[_unpack(flat, small_shapes) for flat in small_out])):
        out[n] = vals
    sent, p = last_ex.wait(small_out[1])
    last_out = _sum_adamw(me, sent[0], None, p[0], 8, *[pad8(t['ffn1_norm'][0]) for t in (w, mom, var)], name="adamw_last")
    out['ffn1_norm'] = [o[0] for o in last_out]
    outs = [out[n][i][None] for i in range(4) for n in WEIGHTS]
    return (loss_total, dx[None], *outs)
```

```python
import math

import jax
import jax.numpy as jnp
import numpy as np
from jax import lax
from jax.experimental import pallas as pl
from jax.experimental.pallas import tpu as pltpu

F32 = jnp.float32
BF16 = jnp.bfloat16

N_DEV = 8
D = 1024
D_FF = 2752
D_FFP = 2816
MEM_LEN = 256
H = 4
Q_RANK, KV_RANK, NOPE, ROPE, VD = 384, 256, 128, 64, 128
QK = NOPE + ROPE
HQ = 2 * 128
SSM_W, SSM_G, SSM_GRP, SSM_P = 512, 32, 16, 64
SSM_N = SSM_G * SSM_P
SSM_PACK = 8
IN_W = 1216
IN_WP = 1408
XH = 128
EPS = 1e-6
LN2 = math.log(2.0)
ROPE_THETA = 10000.0
SCAN_CHUNKS = 8
SCAN_UNROLL = 8
ADAM_LR, ADAM_B1, ADAM_B2, ADAM_EPS, ADAM_WD, ADAM_STEP = 0.001, 0.9, 0.999, 1e-08, 0.01, 10

VMEM_LIMIT = 56 * 1024 * 1024
ACC_BYTES = 6 * 1024 * 1024
LANES = 128
BF16_ROWS = 16
GRAD_DTYPE = BF16
FF_SHARD = D_FF // N_DEV
FF_SHARD_P = 352
IN_SHARD = IN_W // N_DEV
IN_SHARD_P = 160

WEIGHTS = ['ffn1_norm', 'ffn1_w_gate', 'ffn1_w_up', 'ffn1_w_down', 'mix_norm', 'w_in', 'mla_q_norm', 'mla_w_uq',
           'mla_kv_norm', 'mla_w_ukv', 'mla_qk_norm_q', 'mla_qk_norm_k', 'ssm_a_re', 'ssm_a_im', 'ssm_log_dt',
           'ssm_b_re', 'ssm_b_im', 'ssm_c_re', 'ssm_c_im', 'ssm_d', 'ssm_w_glu', 'ssm_b_glu', 'out_norm_mla',
           'out_norm_ssm', 'w_o', 'xattn_norm', 'mem_norm', 'xattn_w_q', 'xattn_w_kv', 'xattn_q_norm',
           'xattn_k_norm', 'xattn_w_o', 'ffn2_norm', 'ffn2_w_gate', 'ffn2_w_up', 'ffn2_w_down']
SHARD_AXIS = {'ffn1_w_gate': 1, 'ffn1_w_up': 1, 'ffn1_w_down': 0, 'w_in': 1, 'mla_w_uq': 1, 'mla_w_ukv': 1,
              'ssm_w_glu': 0, 'w_o': 0, 'xattn_w_q': 0, 'xattn_w_kv': 0, 'xattn_w_o': 1,
              'ffn2_w_gate': 1, 'ffn2_w_up': 1, 'ffn2_w_down': 0}
SHARDED = [n for n in WEIGHTS if n in SHARD_AXIS]
SMALL = [n for n in WEIGHTS if n not in SHARD_AXIS]


def _params(sem=None):
    return pltpu.CompilerParams(dimension_semantics=sem, vmem_limit_bytes=VMEM_LIMIT)


def _tile(n, cap):
    if n <= cap:
        return n
    best = n
    for t in range(LANES, cap + 1, LANES):
        if n % t == 0:
            best = t
    return best


def _mm(a, b, *, ta=False, tb=False, out_dtype=F32, res=None, scale=1.0, name, tm_cap=512, tn_cap=1408, tk_cap=2816,
        deps=()):
    m, k = (a.shape[1], a.shape[0]) if ta else a.shape
    k2, n = (b.shape[1], b.shape[0]) if tb else b.shape
    assert k == k2, (a.shape, b.shape, ta, tb)
    if ta:
        tk_cap = min(tk_cap, 512)
        tm_cap = 1408
    tm, tn, tk = _tile(m, tm_cap), _tile(n, tn_cap), _tile(k, tk_cap)
    if tm * tn * 4 > ACC_BYTES:
        tn = _tile(n, max(LANES, ACC_BYTES // (4 * tm) // LANES * LANES))
    nk = k // tk
    dims = (((0 if ta else 1,), (1 if tb else 0,)), ((), ()))
    has_res = res is not None

    deps = [d for d in deps if d is not None]

    def body(*refs):
        a_ref, b_ref = refs[:2]
        r_ref = refs[2] if has_res else None
        o_ref, acc_ref = refs[-2:]
        kk = pl.program_id(2)

        @pl.when(kk == 0)
        def _():
            acc_ref[...] = jnp.zeros_like(acc_ref)

        acc_ref[...] += lax.dot_general(a_ref[...].astype(BF16), b_ref[...].astype(BF16), dims,
                                        preferred_element_type=F32)

        @pl.when(kk == nk - 1)
        def _():
            out = acc_ref[...]
            if scale != 1.0:
                out = out * scale
            if has_res:
                out = out + r_ref[...].astype(F32)
            o_ref[...] = out.astype(o_ref.dtype)

    a_spec = pl.BlockSpec((tk, tm), lambda i, j, kk: (kk, i)) if ta else pl.BlockSpec((tm, tk), lambda i, j, kk: (i, kk))
    b_spec = pl.BlockSpec((tn, tk), lambda i, j, kk: (j, kk)) if tb else pl.BlockSpec((tk, tn), lambda i, j, kk: (kk, j))
    o_spec = pl.BlockSpec((tm, tn), lambda i, j, kk: (i, j))
    in_specs = [a_spec, b_spec] + ([o_spec] if has_res else []) + [pl.BlockSpec(d.shape, lambda i, j, kk: (0, 0)) for d in deps]
    args = (a, b) + ((res,) if has_res else ()) + tuple(deps)
    return pl.pallas_call(
        body, name=name, grid=(m // tm, n // tn, nk), in_specs=in_specs, out_specs=o_spec,
        out_shape=jax.ShapeDtypeStruct((m, n), out_dtype), scratch_shapes=[pltpu.VMEM((tm, tn), F32)],
        compiler_params=_params(("parallel", "parallel", "arbitrary")),
    )(*args)


def _rowwise(fn, rows, consts, outs, accs=(), *, ts=512, name, deps=()):
    rows = [r if isinstance(r, tuple) else (r, 0, r.shape[1]) for r in rows]
    s = rows[0][0].shape[0]
    ts = min(ts, s)
    assert s % ts == 0
    n_rows, n_consts, n_outs = len(rows), len(consts), len(outs)
    deps = [d for d in deps if d is not None]
    consts = list(consts) + deps

    def body(*refs):
        ins = [r[...] for r in refs[:n_rows + n_consts]]
        res = fn(*ins)
        res = tuple(res) if isinstance(res, (tuple, list)) else (res,)
        out_refs = refs[n_rows + len(consts):]
        for o_ref, val in zip(out_refs[:n_outs], res[:n_outs]):
            o_ref[...] = val.astype(o_ref.dtype)
        if accs:
            first = pl.program_id(0) == 0

            @pl.when(first)
            def _():
                for a_ref, val in zip(out_refs[n_outs:], res[n_outs:]):
                    a_ref[...] = val.astype(F32)

            @pl.when(jnp.logical_not(first))
            def _():
                for a_ref, val in zip(out_refs[n_outs:], res[n_outs:]):
                    a_ref[...] += val.astype(F32)

    in_specs = [pl.BlockSpec((ts, width), lambda i, cb=cb: (i, cb)) for _, cb, width in rows]
    in_specs += [pl.BlockSpec(c.shape, lambda i: (0, 0)) for c in consts]
    out_specs = [pl.BlockSpec((ts, w), lambda i: (i, 0)) for w, _ in outs]
    out_specs += [pl.BlockSpec(tuple(sh), lambda i: (0, 0)) for sh in accs]
    out_shape = [jax.ShapeDtypeStruct((s, w), dt) for w, dt in outs]
    out_shape += [jax.ShapeDtypeStruct(tuple(sh), F32) for sh in accs]
    res = pl.pallas_call(
        body, name=name, grid=(s // ts,), in_specs=in_specs, out_specs=out_specs, out_shape=out_shape,
        compiler_params=_params(("arbitrary",)),
    )(*[a for a, _, _ in rows], *consts)
    return res


def _rowwise_bwd(f, rows, consts, cts, *, row_grads, const_grads, adds=None, ts=512, name, deps=()):
    adds = adds or {}
    n_rows, n_consts, n_cts = len(rows), len(consts), len(cts)
    add_keys = sorted(adds)
    rg = sorted(row_grads)
    cg = sorted(const_grads)

    def fn(*args):
        r = args[:n_rows]
        c = args[n_rows:n_rows + n_consts]
        ct = args[n_rows + n_consts:n_rows + n_consts + n_cts]
        extra = args[n_rows + n_consts + n_cts:]
        outs, vjp = jax.vjp(f, *r, *c)
        outs = tuple(outs) if isinstance(outs, (tuple, list)) else (outs,)
        cot = tuple(g.astype(o.dtype) for g, o in zip(ct, outs))
        grads = vjp(cot if len(cot) > 1 else cot[0])
        res = []
        for i in rg:
            g = grads[i].astype(F32)
            if i in adds:
                g = g + extra[add_keys.index(i)].astype(F32)
            res.append(g)
        for i in cg:
            res.append(grads[n_rows + i])
        return tuple(res)

    rows_all = list(rows) + list(cts) + [adds[i] for i in add_keys]
    def fn2(*args):
        nr = len(rows_all)
        rr, cc = args[:nr], args[nr:]
        return fn(*rr[:n_rows], *cc, *rr[n_rows:])

    outs = [(rows[i][2] if isinstance(rows[i], tuple) else rows[i].shape[1], row_grads[i]) for i in rg]
    accs = [consts[i].shape for i in cg]
    return _rowwise(fn2, rows_all, list(consts), outs, accs, ts=ts, name=name, deps=deps)


def _rms(x, g):
    xf = x.astype(F32)
    return xf * lax.rsqrt(jnp.mean(xf * xf, axis=-1, keepdims=True) + EPS) * g.astype(F32)


def _sigmoid(x):
    return 1.0 / (1.0 + jnp.exp(-x))


def _f_norm(x, g):
    return _rms(x, g).astype(BF16)


def _f_swiglu(gate, up):
    gate, up = gate.astype(F32), up.astype(F32)
    return (gate * _sigmoid(gate) * up).astype(BF16)


def _f_prep1(proj, gq, gkv):
    return _rms(proj[:, :Q_RANK], gq).astype(BF16), _rms(proj[:, Q_RANK:Q_RANK + KV_RANK], gkv).astype(BF16)


KR_BLOCK = (Q_RANK + KV_RANK + SSM_W) // LANES


def _f_prep2(qall, kv, kr, krs, cos, sin, gq, gk):
    kr, krs = kr.astype(F32), krs.astype(F32)
    k_rot = kr * gk[1:2] * cos + krs * gk[2:3] * sin
    k_ss = jnp.sum(kr * kr, axis=-1, keepdims=True)
    q_scale = QK ** -0.5 / LN2
    qs, ks, vs = [], [], []
    for h in range(H):
        qn = qall[:, h * LANES:(h + 1) * LANES].astype(F32)
        qr = qall[:, (H + h) * LANES:(H + h + 1) * LANES].astype(F32)
        qrs = qall[:, (2 * H + h) * LANES:(2 * H + h + 1) * LANES].astype(F32)
        rstd = lax.rsqrt((jnp.sum(qn * qn, axis=-1, keepdims=True) + jnp.sum(qr * qr, axis=-1, keepdims=True)) / QK + EPS)
        rstd = rstd * q_scale
        qs += [qn * gq[0:1] * rstd, (qr * gq[1:2] * cos + qrs * gq[2:3] * sin) * rstd]
        kn = kv[:, 2 * h * LANES:(2 * h + 1) * LANES].astype(F32)
        rstd_k = lax.rsqrt((jnp.sum(kn * kn, axis=-1, keepdims=True) + k_ss) / QK + EPS)
        ks += [kn * gk[0:1] * rstd_k, k_rot * rstd_k]
        vs.append(kv[:, (2 * h + 1) * LANES:(2 * h + 2) * LANES])
    return (jnp.concatenate(qs, axis=-1).astype(BF16), jnp.concatenate(ks, axis=-1).astype(BF16),
            jnp.concatenate(vs, axis=-1).astype(BF16))


def _gelu(x):
    return 0.5 * x * (1.0 + jnp.tanh(math.sqrt(2.0 / math.pi) * (x + 0.044715 * (x * x * x))))


def _f_s5_gelu(yc, u, d):
    return _gelu(yc.astype(F32) + d * u.astype(F32))


def _f_outnorm(o_mla, g, z, b_glu, g_om, g_os):
    y_ssm = g * _sigmoid(z + b_glu)
    return jnp.concatenate([_rms(o_mla, g_om), _rms(y_ssm, g_os)], axis=-1).astype(BF16)


def _f_memk(kvm, gk):
    ks = [_rms(kvm[:, h * XH:(h + 1) * XH], gk) for h in range(H)]
    return jnp.concatenate(ks, axis=-1).astype(BF16), kvm[:, H * XH:].astype(BF16)


def _f_disc(lr, li, log_dt, br, bi):
    dt = jnp.exp(log_dt)
    decay = jnp.exp(lr * dt)
    ar = decay * jnp.cos(li * dt)
    ai = decay * jnp.sin(li * dt)
    den = lr * lr + li * li
    nr = ar - 1.0
    coef_r = (nr * lr + ai * li) / den
    coef_i = (ai * lr - nr * li) / den
    return ar, ai, coef_r * br - coef_i * bi, coef_r * bi + coef_i * br


def _causal_mask(i, j, tq, tk):
    qpos = i * tq + lax.broadcasted_iota(jnp.int32, (tq, tk), 0)
    kpos = j * tk + lax.broadcasted_iota(jnp.int32, (tq, tk), 1)
    return qpos >= kpos


def _attn_fwd(q, k, v, *, t=512):
    s = q.shape[0]
    t = min(t, s)
    nb = s // t

    def body(q_ref, k_ref, v_ref, o_ref, lse_ref, m_sc, l_sc, acc_sc):
        i, j = pl.program_id(1), pl.program_id(2)

        @pl.when(j == 0)
        def _():
            m_sc[...] = jnp.full_like(m_sc, -jnp.inf)
            l_sc[...] = jnp.zeros_like(l_sc)
            acc_sc[...] = jnp.zeros_like(acc_sc)

        def block(diagonal):
            sc = lax.dot_general(q_ref[...], k_ref[...], (((1,), (1,)), ((), ())), preferred_element_type=F32)
            if diagonal:
                sc = jnp.where(_causal_mask(i, j, t, t), sc, -jnp.inf)
            m_old = m_sc[...]
            m_new = jnp.maximum(m_old, jnp.max(sc, axis=-1, keepdims=True))
            p = jnp.exp2(sc - m_new)
            alpha = jnp.exp2(m_old - m_new)
            l_sc[...] = alpha * l_sc[...] + jnp.sum(p, axis=-1, keepdims=True)
            acc_sc[...] = alpha * acc_sc[...] + jnp.dot(p.astype(BF16), v_ref[...], preferred_element_type=F32)
            m_sc[...] = m_new

        pl.when(j < i)(lambda: block(False))

        @pl.when(j == i)
        def _():
            block(True)
            o_ref[...] = acc_sc[...] / l_sc[...]
            lse_ref[...] = jnp.broadcast_to(m_sc[...] + jnp.log2(l_sc[...]), lse_ref.shape)

    kv_map = lambda h, i, j: (jnp.minimum(j, i), h)
    return pl.pallas_call(
        body, name="mla_attn_fwd", grid=(H, nb, nb),
        in_specs=[pl.BlockSpec((t, HQ), lambda h, i, j: (i, h)), pl.BlockSpec((t, HQ), kv_map),
                  pl.BlockSpec((t, VD), kv_map)],
        out_specs=[pl.BlockSpec((t, VD), lambda h, i, j: (i, h)), pl.BlockSpec((t, LANES), lambda h, i, j: (i, h))],
        out_shape=[jax.ShapeDtypeStruct((s, H * VD), F32), jax.ShapeDtypeStruct((s, H * LANES), F32)],
        scratch_shapes=[pltpu.VMEM((t, 1), F32), pltpu.VMEM((t, 1), F32), pltpu.VMEM((t, VD), F32)],
        compiler_params=_params(("parallel", "parallel", "arbitrary")),
    )(q, k, v)


def _attn_probs(q_ref, k_ref, v_ref, do_ref, lse_ref, dl_ref, i, j, t, diagonal):
    sc = lax.dot_general(q_ref[...], k_ref[...], (((1,), (1,)), ((), ())), preferred_element_type=F32)
    p = jnp.exp2(sc - jnp.tile(lse_ref[...], (1, t // LANES)))
    if diagonal:
        p = jnp.where(_causal_mask(i, j, t, t), p, 0.0)
    dp = lax.dot_general(do_ref[...], v_ref[...], (((1,), (1,)), ((), ())), preferred_element_type=F32)
    ds = p * (dp - jnp.tile(dl_ref[...], (1, t // LANES)))
    return p, ds


def _attn_bwd(q, k, v, do, lse, delta, *, t=512):
    s = q.shape[0]
    t = min(t, s)
    nb = s // t

    def body(q_ref, k_ref, v_ref, do_ref, lse_ref, dl_ref, dq_ref, dk_ref, dv_ref, dk_sc, dv_sc):
        j, i = pl.program_id(1), pl.program_id(2)

        @pl.when(jnp.logical_and(i == 0, j == 0))
        def _():
            dq_ref[...] = jnp.zeros_like(dq_ref)

        @pl.when(i == 0)
        def _():
            dk_sc[...] = jnp.zeros_like(dk_sc)
            dv_sc[...] = jnp.zeros_like(dv_sc)

        def block(diagonal):
            p, ds = _attn_probs(q_ref, k_ref, v_ref, do_ref, lse_ref, dl_ref, i, j, t, diagonal)
            dsb = ds.astype(BF16)
            dv_sc[...] += lax.dot_general(p.astype(BF16), do_ref[...], (((0,), (0,)), ((), ())), preferred_element_type=F32)
            dk_sc[...] += lax.dot_general(dsb, q_ref[...], (((0,), (0,)), ((), ())), preferred_element_type=F32)
            rows = pl.ds(pl.multiple_of(i * t, t), t)
            dq_ref[rows, :] += jnp.dot(dsb, k_ref[...], preferred_element_type=F32)

        pl.when(i > j)(lambda: block(False))
        pl.when(i == j)(lambda: block(True))

        @pl.when(i == nb - 1)
        def _():
            dk_ref[...] = (dk_sc[...] * LN2).astype(dk_ref.dtype)
            dv_ref[...] = dv_sc[...].astype(dv_ref.dtype)

        @pl.when(jnp.logical_and(i == nb - 1, j == nb - 1))
        def _():
            dq_ref[...] = dq_ref[...] * LN2

    q_map = lambda h, j, i: (jnp.maximum(i, j), h)
    kv_map = lambda h, j, i: (j, h)
    dq, dk, dv = pl.pallas_call(
        body, name="mla_attn_bwd", grid=(H, nb, nb),
        in_specs=[pl.BlockSpec((t, HQ), q_map), pl.BlockSpec((t, HQ), kv_map), pl.BlockSpec((t, VD), kv_map),
                  pl.BlockSpec((t, VD), q_map), pl.BlockSpec((t, LANES), q_map), pl.BlockSpec((t, LANES), q_map)],
        out_specs=[pl.BlockSpec((s, HQ), lambda h, j, i: (0, h)), pl.BlockSpec((t, HQ), kv_map), pl.BlockSpec((t, VD), kv_map)],
        out_shape=[jax.ShapeDtypeStruct((s, H * HQ), F32), jax.ShapeDtypeStruct((s, H * HQ), BF16),
                   jax.ShapeDtypeStruct((s, H * VD), BF16)],
        scratch_shapes=[pltpu.VMEM((t, HQ), F32), pltpu.VMEM((t, VD), F32)],
        compiler_params=_params(("parallel", "arbitrary", "arbitrary")),
    )(q, k, v, do, lse, delta)
    return dq, dk, dv


def _f_delta(do, o):
    prod = do.astype(F32) * o.astype(F32)
    parts = [jnp.broadcast_to(jnp.sum(prod[:, h * VD:(h + 1) * VD], axis=-1, keepdims=True), (do.shape[0], LANES))
             for h in range(H)]
    return jnp.concatenate(parts, axis=-1), do.astype(BF16)


def _xattn_head(qh, kh, gq):
    qn = _rms(qh, gq) * (XH ** -0.5)
    sc = lax.dot_general(qn.astype(BF16), kh, (((1,), (1,)), ((), ())), preferred_element_type=F32)
    sc = sc - jnp.max(sc, axis=-1, keepdims=True)
    e = jnp.exp(sc)
    return qn, e / jnp.sum(e, axis=-1, keepdims=True)


def _xattn_fwd(q, kn, v, gq, *, ts=512):
    def fn(qb, knb, vb, g):
        outs = []
        for h in range(H):
            sl = slice(h * XH, (h + 1) * XH)
            _, p = _xattn_head(qb[:, sl], knb[:, sl], g)
            outs.append(jnp.dot(p.astype(BF16), vb[:, sl], preferred_element_type=F32))
        return (jnp.concatenate(outs, axis=-1),)

    return _rowwise(fn, [q], [kn, v, gq], [(H * XH, BF16)], ts=ts, name="xattn_fwd")[0]


def _xattn_bwd(q, kn, v, gq, do, *, ts=512):
    def fn(qb, dob, knb, vb, g):
        dqs, dks, dvs = [], [], []
        dg = jnp.zeros((1, XH), F32)
        for h in range(H):
            sl = slice(h * XH, (h + 1) * XH)
            qh, kh, vh, doh = qb[:, sl], knb[:, sl], vb[:, sl], dob[:, sl].astype(BF16)
            qn, p = _xattn_head(qh, kh, g)
            dp = lax.dot_general(doh, vh, (((1,), (1,)), ((), ())), preferred_element_type=F32)
            dvs.append(lax.dot_general(p.astype(BF16), doh, (((0,), (0,)), ((), ())), preferred_element_type=F32))
            ds = (p * (dp - jnp.sum(dp * p, axis=-1, keepdims=True))).astype(BF16)
            dqn = jnp.dot(ds, kh, preferred_element_type=F32)
            dks.append(lax.dot_general(ds, qn.astype(BF16), (((0,), (0,)), ((), ())), preferred_element_type=F32))
            _, vjp_n = jax.vjp(lambda a, b: _rms(a, b) * (XH ** -0.5), qh, g)
            dqh, dgh = vjp_n(dqn)
            dqs.append(dqh)
            dg = dg + dgh
        return (jnp.concatenate(dqs, axis=-1), jnp.concatenate(dks, axis=-1), jnp.concatenate(dvs, axis=-1), dg)

    return _rowwise(fn, [q, do], [kn, v, gq], [(H * XH, BF16)], [kn.shape, v.shape, gq.shape], ts=ts, name="xattn_bwd")


def _cmul(ar, ai, xr, xi):
    return ar * xr - ai * xi, ar * xi + ai * xr


def _scan_in_place(xr_ref, xi_ref, ar, ai, *, reverse):
    s, cw = xr_ref.shape
    c = SCAN_CHUNKS
    tt = s // c
    a_r = jnp.broadcast_to(ar, (c, cw))
    a_i = jnp.broadcast_to(ai, (c, cw))
    zero = jnp.zeros((c, cw), F32)

    def row(step):
        t = (tt - 1 - step) if reverse else step
        return pl.ds(pl.multiple_of(t * c, c), c)

    def local(step, carry):
        sr, si, qr, qi = carry
        r = row(step)
        nr, ni = _cmul(a_r, a_i, sr, si)
        nr, ni = nr + xr_ref[r, :], ni + xi_ref[r, :]
        xr_ref[r, :] = nr
        xi_ref[r, :] = ni
        return (nr, ni) + _cmul(a_r, a_i, qr, qi)

    end_r, end_i, pr, pi = lax.fori_loop(0, tt, local, (zero, zero, jnp.ones((c, cw), F32), zero), unroll=SCAN_UNROLL)

    rows_id = lax.broadcasted_iota(jnp.int32, (c, cw), 0)
    car_r, car_i = zero, zero
    cur_r, cur_i = jnp.zeros((1, cw), F32), jnp.zeros((1, cw), F32)
    order = range(c - 1, -1, -1) if reverse else range(c)
    for kk in order:
        car_r = jnp.where(rows_id == kk, cur_r, car_r)
        car_i = jnp.where(rows_id == kk, cur_i, car_i)
        nr, ni = _cmul(pr[0:1], pi[0:1], cur_r, cur_i)
        cur_r = nr + end_r[kk:kk + 1]
        cur_i = ni + end_i[kk:kk + 1]

    def fix(step, carry):
        qr, qi = _cmul(a_r, a_i, *carry)
        r = row(step)
        dr, di = _cmul(qr, qi, car_r, car_i)
        xr_ref[r, :] += dr
        xi_ref[r, :] += di
        return qr, qi

    lax.fori_loop(0, tt, fix, (jnp.ones((c, cw), F32), zero), unroll=SCAN_UNROLL)


S5_ROWS = 512


def _s5_scan(v, w_r, w_i, ar, ai, *, reverse, tb, readout=None, name):
    s = v.shape[0]
    g = w_r.shape[0]
    nv, ns = SSM_PACK * SSM_GRP, SSM_PACK * SSM_P
    rows = min(S5_ROWS, s)
    dims = (((1,), (1 if tb else 0,)), ((), ()))
    n_w = 2 if readout is None else 4

    def body(v_ref, ar_ref, ai_ref, *refs):
        w = [r[...] for r in refs[:n_w]]
        xr_ref, xi_ref = refs[n_w:n_w + 2]
        for r0 in range(0, s, rows):
            vb = v_ref[r0:r0 + rows, :].astype(BF16)
            xr_ref[r0:r0 + rows, :] = lax.dot_general(vb, w[0], dims, preferred_element_type=F32)
            xi_ref[r0:r0 + rows, :] = lax.dot_general(vb, w[1], dims, preferred_element_type=F32)
        _scan_in_place(xr_ref, xi_ref, ar_ref[...], ai_ref[...], reverse=reverse)
        if readout is not None:
            y_ref = refs[n_w + 2]
            for r0 in range(0, s, rows):
                y_ref[r0:r0 + rows, :] = (
                    jnp.dot(xr_ref[r0:r0 + rows, :].astype(BF16), w[2], preferred_element_type=F32)
                    + jnp.dot(xi_ref[r0:r0 + rows, :].astype(BF16), w[3], preferred_element_type=F32))

    col = lambda j: (0, j)
    w_spec = lambda a: pl.BlockSpec((None,) + a.shape[1:], lambda j: (j, 0, 0))
    weights = [w_r, w_i] + (list(readout) if readout is not None else [])
    out_specs = [pl.BlockSpec((s, ns), col)] * 2 + ([pl.BlockSpec((s, nv), col)] if readout is not None else [])
    out_shape = [jax.ShapeDtypeStruct((s, g * ns), F32)] * 2 + (
        [jax.ShapeDtypeStruct((s, g * nv), F32)] if readout is not None else [])
    return pl.pallas_call(
        body, name=name, grid=(g,),
        in_specs=[pl.BlockSpec((s, nv), col), pl.BlockSpec((1, ns), col), pl.BlockSpec((1, ns), col)] + [w_spec(a) for a in weights],
        out_specs=out_specs, out_shape=out_shape, compiler_params=_params(("parallel",)),
    )(v, ar, ai, *weights)


def _s5_grads(lam_r, lam_i, xr, xi, u, dyc, du_d, b_r, b_i):
    s = u.shape[0]
    g = b_r.shape[0]
    nv, ns, c = SSM_PACK * SSM_GRP, SSM_PACK * SSM_P, SCAN_CHUNKS
    rows = min(S5_ROWS, s)
    slabs = rows // c
    last_slab = s // c - 1
    nt = (((1,), (1,)), ((), ()))
    tn = (((0,), (0,)), ((), ()))

    def body(lr_ref, li_ref, xr_ref, xi_ref, pr_ref, pi_ref, u_ref, dy_ref, dud_ref, br_ref, bi_ref,
             du_ref, dbr_ref, dbi_ref, dcr_ref, dci_ref, dar_ref, dai_ref):
        first = pl.program_id(1) == 0
        l_r, l_i, x_r, x_i = lr_ref[...], li_ref[...], xr_ref[...], xi_ref[...]
        lrb, lib = l_r.astype(BF16), l_i.astype(BF16)
        du_ref[...] = (dud_ref[...] + lax.dot_general(lrb, br_ref[...], nt, preferred_element_type=F32)
                       + lax.dot_general(lib, bi_ref[...], nt, preferred_element_type=F32))
        ub, dyb = u_ref[...].astype(BF16), dy_ref[...].astype(BF16)
        rows_id = lax.broadcasted_iota(jnp.int32, (c, ns), 0)

        def before(p_ref, x):
            p = p_ref[...]
            p = jnp.where(first, jnp.where(rows_id == 0, 0.0, pltpu.roll(p, 1, 0)), p)
            return jnp.concatenate([p, x[:rows - c]], axis=0)

        xp_r, xp_i = before(pr_ref, x_r), before(pi_ref, x_i)
        parts = (lax.dot_general(ub, lrb, tn, preferred_element_type=F32),
                 lax.dot_general(ub, lib, tn, preferred_element_type=F32),
                 lax.dot_general(x_r.astype(BF16), dyb, tn, preferred_element_type=F32),
                 lax.dot_general(x_i.astype(BF16), dyb, tn, preferred_element_type=F32),
                 jnp.sum(l_r * xp_r + l_i * xp_i, axis=0, keepdims=True),
                 jnp.sum(l_i * xp_r - l_r * xp_i, axis=0, keepdims=True))
        accs = (dbr_ref, dbi_ref, dcr_ref, dci_ref, dar_ref, dai_ref)

        @pl.when(first)
        def _():
            for a_ref, val in zip(accs, parts):
                a_ref[...] = val

        @pl.when(jnp.logical_not(first))
        def _():
            for a_ref, val in zip(accs, parts):
                a_ref[...] += val

    state = pl.BlockSpec((rows, ns), lambda j, k: (k, j))
    chan = pl.BlockSpec((rows, nv), lambda j, k: (k, j))
    slab = pl.BlockSpec((c, ns), lambda j, k: (jnp.where(k == 0, last_slab, k * slabs - 1), j))
    per_b = pl.BlockSpec((None, nv, ns), lambda j, k: (j, 0, 0))
    per_c = pl.BlockSpec((None, ns, nv), lambda j, k: (j, 0, 0))
    per_a = pl.BlockSpec((1, ns), lambda j, k: (0, j))
    return pl.pallas_call(
        body, name="s5_grads", grid=(g, s // rows),
        in_specs=[state, state, state, state, slab, slab, chan, chan, chan, per_b, per_b],
        out_specs=[chan, per_b, per_b, per_c, per_c, per_a, per_a],
        out_shape=[jax.ShapeDtypeStruct((s, g * nv), F32), jax.ShapeDtypeStruct((g, nv, ns), F32),
                   jax.ShapeDtypeStruct((g, nv, ns), F32), jax.ShapeDtypeStruct((g, ns, nv), F32),
                   jax.ShapeDtypeStruct((g, ns, nv), F32), jax.ShapeDtypeStruct((1, g * ns), F32),
                   jax.ShapeDtypeStruct((1, g * ns), F32)],
        compiler_params=_params(("parallel", "arbitrary")),
    )(lam_r, lam_i, xr, xi, xr, xi, u, dyc, du_d, b_r, b_i)


def _mesh_place():
    x, y, c = lax.axis_index("x"), lax.axis_index("y"), lax.axis_index("c")
    peers = []
    for k in range(1, N_DEV):
        px, py, pc = x ^ ((k >> 2) & 1), y ^ ((k >> 1) & 1), c ^ (k & 1)
        peers.append(((px, py, pc), 4 * px + 2 * py + pc))
    return 4 * x + 2 * y + c, peers


class _Exchange:
    SAME_CORE_MASKS = (2, 4, 6)

    def __init__(self, arrays, rows, *, gather, name, after=None, two_level=False):
        self.n_arr, self.rows, self.gather, self.name = len(arrays), rows, gather, name
        self.two_level, self.in_flight = two_level, (1 + len(self.SAME_CORE_MASKS) if two_level else N_DEV - 1)
        n_arr = self.n_arr
        if gather:
            assert all(r % BF16_ROWS == 0 for r in rows)
            lands = [lax.empty((N_DEV * r, a.shape[1]), a.dtype) for a, r in zip(arrays, rows)]
        else:
            lands = [lax.empty((N_DEV - 1,) + (tuple(a.shape) if st is None else (n, a.shape[1])), a.dtype)
                     for a, (st, n) in zip(arrays, rows)]
        has_after = after is not None

        def body(*refs):
            ins, zones = refs[:n_arr], refs[n_arr:2 * n_arr]
            sems = refs[2 * n_arr + has_after:4 * n_arr + has_after]
            token = refs[-1]
            me, peers = _mesh_place()
            for i in range(n_arr):
                for k, (pxyz, pid) in enumerate(peers):
                    if two_level and k + 1 not in (1,) + self.SAME_CORE_MASKS:
                        continue
                    if gather:
                        src = ins[i]
                        dst = zones[i].at[pl.ds(pl.multiple_of(me * rows[i], BF16_ROWS), rows[i])]
                    else:
                        stride, n = rows[i]
                        src = ins[i] if stride is None else ins[i].at[pl.ds(pl.multiple_of(pid * stride, BF16_ROWS), n)]
                        dst = zones[i].at[k]
                    pltpu.make_async_remote_copy(
                        src_ref=src, dst_ref=dst, send_sem=sems[2 * i], recv_sem=sems[2 * i + 1],
                        device_id=pxyz, device_id_type=pl.DeviceIdType.MESH).start()
            token[...] = jnp.zeros_like(token)

        hbm = pl.BlockSpec(memory_space=pltpu.HBM)
        sem = pl.BlockSpec(memory_space=pltpu.SEMAPHORE)
        args = [pltpu.with_memory_space_constraint(a, pltpu.HBM) for a in list(arrays) + lands]
        res = pl.pallas_call(
            body, name=name + "_start",
            in_specs=[hbm] * (2 * n_arr) + ([pl.BlockSpec(memory_space=pl.ANY)] if has_after else []),
            out_specs=[sem] * (2 * n_arr) + [hbm] * (2 * n_arr) + [pl.BlockSpec(memory_space=pltpu.VMEM)],
            out_shape=[pltpu.SemaphoreType.DMA(())] * (2 * n_arr) + [pltpu.HBM(a.shape, a.dtype) for a in args]
            + [jax.ShapeDtypeStruct((8, LANES), F32)],
            input_output_aliases={i: 2 * n_arr + i for i in range(2 * n_arr)},
            compiler_params=pltpu.CompilerParams(has_side_effects=pltpu.SideEffectType.DATAFLOW_SIDE_EFFECTING),
        )(*args, *([after] if has_after else []))
        self.sems, self.thru, self.token = res[:2 * n_arr], res[2 * n_arr:4 * n_arr], res[-1]

    def _wait_all(self, zones, sems):
        myself = (lax.axis_index("x"), lax.axis_index("y"), lax.axis_index("c"))
        for i in range(self.n_arr):
            many = zones[i].at[pl.ds(0, self.in_flight * self.rows[i])] if self.gather else zones[i]
            all_of_them = pltpu.make_async_remote_copy(
                src_ref=many, dst_ref=many, send_sem=sems[2 * i], recv_sem=sems[2 * i + 1],
                device_id=myself, device_id_type=pl.DeviceIdType.MESH)
            all_of_them.wait_recv()
            all_of_them.wait_send()

    def forward(self, after):
        n_arr = self.n_arr

        def body(*refs):
            zones, sems = refs[n_arr:2 * n_arr], refs[2 * n_arr:4 * n_arr]
            new_sems = refs[4 * n_arr + 1:6 * n_arr + 1]
            self._wait_all(zones, sems)
            _, peers = _mesh_place()
            sibling, _ = peers[0]
            for i in range(n_arr):
                for mask in self.SAME_CORE_MASKS:
                    _, pid = peers[mask - 1]
                    block = zones[i].at[pl.ds(pl.multiple_of(pid * self.rows[i], BF16_ROWS), self.rows[i])]
                    pltpu.make_async_remote_copy(
                        src_ref=block, dst_ref=block, send_sem=new_sems[2 * i], recv_sem=new_sems[2 * i + 1],
                        device_id=sibling, device_id_type=pl.DeviceIdType.MESH).start()

        hbm = pl.BlockSpec(memory_space=pltpu.HBM)
        sem = pl.BlockSpec(memory_space=pltpu.SEMAPHORE)
        res = pl.pallas_call(
            body, name=self.name + "_forward",
            in_specs=[hbm] * (2 * n_arr) + [sem] * (2 * n_arr) + [pl.BlockSpec(memory_space=pl.ANY)],
            out_specs=[sem] * (2 * n_arr) + [hbm] * (2 * n_arr),
            out_shape=[pltpu.SemaphoreType.DMA(())] * (2 * n_arr) + [pltpu.HBM(a.shape, a.dtype) for a in self.thru],
            input_output_aliases={i: 2 * n_arr + i for i in range(2 * n_arr)},
            compiler_params=pltpu.CompilerParams(has_side_effects=pltpu.SideEffectType.DATAFLOW_SIDE_EFFECTING),
        )(*self.thru, *self.sems, after)
        self.sems, self.thru = res[:2 * n_arr], res[2 * n_arr:]
        self.two_level, self.in_flight = False, len(self.SAME_CORE_MASKS)

    def wait(self, after):
        n_arr = self.n_arr
        if self.two_level:
            self.forward(after)

        def body(*refs):
            self._wait_all(refs[n_arr:2 * n_arr], refs[2 * n_arr:4 * n_arr])

        hbm = pl.BlockSpec(memory_space=pltpu.HBM)
        sem = pl.BlockSpec(memory_space=pltpu.SEMAPHORE)
        res = pl.pallas_call(
            body, name=self.name + "_wait",
            in_specs=[hbm] * (2 * n_arr) + [sem] * (2 * n_arr) + [pl.BlockSpec(memory_space=pl.ANY)],
            out_specs=[hbm] * (2 * n_arr), out_shape=[pltpu.HBM(a.shape, a.dtype) for a in self.thru],
            input_output_aliases={i: i for i in range(2 * n_arr)},
            compiler_params=pltpu.CompilerParams(has_side_effects=pltpu.SideEffectType.DATAFLOW_SIDE_EFFECTING),
        )(*self.thru, *self.sems, after)
        return res[:n_arr], res[n_arr:]


def _my_slot():
    me = 4 * lax.axis_index("x") + 2 * lax.axis_index("y") + lax.axis_index("c")
    return me.astype(jnp.int32).reshape(1)


def _place_own(gathered, blocks, me, *, name):
    n = len(blocks)

    def body(me_ref, *refs):
        for b_ref, o_ref in zip(refs[:n], refs[2 * n:]):
            o_ref[...] = b_ref[...]

    res = pl.pallas_call(
        body, name=name, out_shape=[jax.ShapeDtypeStruct(g.shape, g.dtype) for g in gathered],
        grid_spec=pltpu.PrefetchScalarGridSpec(
            num_scalar_prefetch=1, grid=(1,),
            in_specs=[pl.BlockSpec(b.shape, lambda i, me_ref: (0, 0)) for b in blocks] + [pl.BlockSpec(memory_space=pl.ANY)] * n,
            out_specs=[pl.BlockSpec(b.shape, lambda i, me_ref: (me_ref[0], 0)) for b in blocks]),
        input_output_aliases={1 + n + i: i for i in range(n)}, compiler_params=_params(("arbitrary",)),
    )(me, *blocks, *gathered)
    return list(res)


def _elementwise_tiles(r, c):
    if r % 128 == 0:
        return 128, c
    return r, (256 if c % 256 == 0 else c)


def _adamw_math(g, w, m, v):
    nm = ADAM_B1 * m + (1.0 - ADAM_B1) * g
    nv = ADAM_B2 * v + (1.0 - ADAM_B2) * (g * g)
    m_hat = nm / (1.0 - ADAM_B1 ** ADAM_STEP)
    v_hat = nv / (1.0 - ADAM_B2 ** ADAM_STEP)
    return -ADAM_LR * (m_hat / (jnp.sqrt(v_hat) + ADAM_EPS) + ADAM_WD * w), nm, nv


def _sum_parts(me_ref, own_ref, p_ref, r):
    own = own_ref[...].astype(F32)
    g = None
    for d in range(N_DEV):
        k = jnp.bitwise_xor(me_ref[0], d)
        term = jnp.where(k == 0, own, p_ref[jnp.maximum(k, 1) - 1].astype(F32))
        g = term if g is None else g + term
    return g[0:r, :]


def _sum_adamw(me, sent, stride, parts, r, w=None, m=None, v=None, *, name):
    _, own_rows, cdim = parts.shape
    assert stride is None or stride == own_rows
    tc = 256 if cdim % 256 == 0 else cdim
    update = w is not None

    def body(me_ref, own_ref, p_ref, *refs):
        g = _sum_parts(me_ref, own_ref, p_ref, r)
        if update:
            w_ref, m_ref, v_ref, g_ref, d_ref, nm_ref, nv_ref = refs
            d_ref[...], nm_ref[...], nv_ref[...] = _adamw_math(g, w_ref[...], m_ref[...], v_ref[...])
        else:
            g_ref, = refs
        g_ref[...] = g

    blk = pl.BlockSpec((r, tc), lambda j, me_ref: (0, j))
    own_spec = pl.BlockSpec((own_rows, tc), (lambda j, me_ref: (0, j)) if stride is None else (lambda j, me_ref: (me_ref[0], j)))
    n_out = 4 if update else 1
    res = pl.pallas_call(
        body, name=name, out_shape=[jax.ShapeDtypeStruct((r, cdim), F32)] * n_out,
        grid_spec=pltpu.PrefetchScalarGridSpec(
            num_scalar_prefetch=1, grid=(cdim // tc,),
            in_specs=[own_spec, pl.BlockSpec((N_DEV - 1, own_rows, tc), lambda j, me_ref: (0, 0, j))]
            + ([blk] * 3 if update else []),
            out_specs=[blk] * n_out),
        compiler_params=_params(("parallel",)),
    )(me, sent, parts, *((w, m, v) if update else ()))
    return list(res)


def _adamw(g, w, m, v, *, name):
    r, cdim = w.shape
    tr, tc = _elementwise_tiles(r, cdim)

    def body(g_ref, w_ref, m_ref, v_ref, d_ref, nm_ref, nv_ref):
        d_ref[...], nm_ref[...], nv_ref[...] = _adamw_math(g_ref[...], w_ref[...], m_ref[...], v_ref[...])

    blk = pl.BlockSpec((tr, tc), lambda i, j: (i, j))
    return list(pl.pallas_call(
        body, name=name, grid=(r // tr, cdim // tc), in_specs=[blk] * 4,
        out_specs=[blk] * 3, out_shape=[jax.ShapeDtypeStruct((r, cdim), F32)] * 3,
        compiler_params=_params(("parallel", "parallel")),
    )(g, w, m, v))


SHARD_ROWS_P = {n: (FF_SHARD_P if 'ffn' in n else IN_SHARD_P if n == 'w_in' else None) for n in SHARDED}


def _to_exchange_layout(name, shard):
    t = shard.T if SHARD_AXIS[name] == 1 else shard
    pad = SHARD_ROWS_P[name]
    return t if pad is None else jnp.pad(t, ((0, pad - t.shape[0]), (0, 0)))


def _expand_w_in(wt):
    wt = wt.reshape(N_DEV, IN_SHARD_P, D)[:, :IN_SHARD].reshape(IN_W, D)
    o = Q_RANK + KV_RANK
    kr1, kr2 = wt[o:o + ROPE // 2], wt[o + ROPE // 2:o + ROPE]
    z = jnp.zeros((LANES - ROPE, D), wt.dtype)
    return jnp.concatenate([wt[:o], wt[o + ROPE:], kr1, kr2, z, -kr2, kr1, z], axis=0)


def _expand_w_uq(wt):
    w = wt.reshape(H, QK, Q_RANK)
    z = jnp.zeros((H, LANES - ROPE, Q_RANK), w.dtype)
    q1, q2 = w[:, NOPE:NOPE + ROPE // 2], w[:, NOPE + ROPE // 2:]
    return jnp.concatenate([w[:, :NOPE].reshape(H * NOPE, Q_RANK),
                            jnp.concatenate([q1, q2, z], axis=1).reshape(H * LANES, Q_RANK),
                            jnp.concatenate([-q2, q1, z], axis=1).reshape(H * LANES, Q_RANK)], axis=0)


def _layout_qk_gain(g):
    g = g.reshape(QK)
    g1, g2, z = g[NOPE:NOPE + ROPE // 2], g[NOPE + ROPE // 2:], jnp.zeros((LANES - ROPE,), g.dtype)
    return jnp.stack([g[:NOPE], jnp.concatenate([g1, g2, z]), jnp.concatenate([g2, g1, z])])


def _rep16(a):
    return jnp.repeat(a, SSM_GRP, axis=0)


def _layout_ssm_in(a_re, a_im, log_dt, b_re, b_im):
    b_r = jnp.transpose(b_re, (0, 2, 1)).reshape(SSM_G * SSM_GRP, SSM_P)
    b_i = jnp.transpose(b_im, (0, 2, 1)).reshape(SSM_G * SSM_GRP, SSM_P)
    ldt = jnp.broadcast_to(log_dt.reshape(SSM_G, 1), (SSM_G, SSM_P))
    return _rep16(a_re), _rep16(a_im), _rep16(ldt), b_r, b_i


def _block_diag_b(bb):
    eye = jnp.eye(SSM_PACK, dtype=bb.dtype)
    b5 = bb.reshape(SSM_G // SSM_PACK, SSM_PACK, SSM_GRP, 1, SSM_P) * eye[None, :, None, :, None]
    return b5.reshape(SSM_G // SSM_PACK, SSM_PACK * SSM_GRP, SSM_PACK * SSM_P)


def _block_diag_c(cc):
    eye = jnp.eye(SSM_PACK, dtype=cc.dtype)
    c5 = jnp.transpose(cc, (0, 2, 1)).reshape(SSM_G // SSM_PACK, SSM_PACK, SSM_P, 1, SSM_GRP) * eye[None, :, None, :, None]
    return c5.reshape(SSM_G // SSM_PACK, SSM_PACK * SSM_P, SSM_PACK * SSM_GRP)


def _time_perm(a, inverse=False):
    s, w = a.shape
    c = SCAN_CHUNKS
    if inverse:
        return jnp.transpose(a.reshape(s // c, c, w), (1, 0, 2)).reshape(s, w)
    return jnp.transpose(a.reshape(c, s // c, w), (1, 0, 2)).reshape(s, w)


class _Weights:
    def __init__(self, groups=(), landed=None, me=None):
        self.groups, self.landed, self.me = list(groups), dict(landed or {}), me

    def get(self, name, after):
        if name not in self.landed:
            names, exchange = next(g for g in self.groups if name in g[0])
            blocks, gathered = exchange.wait(after)
            self.landed.update(zip(names, _place_own(gathered, blocks, self.me, name="place_" + names[0])))
        return self.landed[name]

    def __getitem__(self, name):
        return self.landed[name]

    def prefetch(self, name, after):
        for names, exchange in self.groups:
            if name in names and exchange.two_level:
                exchange.forward(after)


def _ffn_gate_up(h, w_gt, w_ut, *, name, tm=512, tn=1408):
    s, k = h.shape
    n = w_gt.shape[0]
    tm, tn = min(tm, s), _tile(n, tn)
    dims = (((1,), (1,)), ((), ()))

    def body(h_ref, wg_ref, wu_ref, g_ref, u_ref, a_ref):
        hb = h_ref[...].astype(BF16)
        gate = lax.dot_general(hb, wg_ref[...], dims, preferred_element_type=F32)
        up = lax.dot_general(hb, wu_ref[...], dims, preferred_element_type=F32)
        g_ref[...] = gate.astype(BF16)
        u_ref[...] = up.astype(BF16)
        a_ref[...] = _f_swiglu(gate, up)

    w_spec = pl.BlockSpec((tn, k), lambda j, i: (j, 0))
    o_spec = pl.BlockSpec((tm, tn), lambda j, i: (i, j))
    return pl.pallas_call(
        body, name=name, grid=(n // tn, s // tm), in_specs=[pl.BlockSpec((tm, k), lambda j, i: (i, 0)), w_spec, w_spec],
        out_specs=[o_spec] * 3, out_shape=[jax.ShapeDtypeStruct((s, n), BF16)] * 3,
        compiler_params=_params(("parallel", "parallel")),
    )(h, w_gt, w_ut)


def _ffn_dgate_dup(dx_out, w_d, gate, up, *, name, tm=512, tn=1408, deps=()):
    s, k = dx_out.shape
    n = w_d.shape[0]
    tm, tn = min(tm, s), _tile(n, tn)
    deps = [d for d in deps if d is not None]

    def body(dx_ref, wd_ref, g_ref, u_ref, *refs):
        dg_ref, du_ref = refs[len(deps):]
        dact = 0.5 * lax.dot_general(dx_ref[...].astype(BF16), wd_ref[...], (((1,), (1,)), ((), ())),
                                     preferred_element_type=F32)
        _, vjp = jax.vjp(_f_swiglu, g_ref[...].astype(F32), u_ref[...].astype(F32))
        dgate, dup = vjp(dact.astype(BF16))
        dg_ref[...] = dgate.astype(BF16)
        du_ref[...] = dup.astype(BF16)

    o_spec = pl.BlockSpec((tm, tn), lambda j, i: (i, j))
    return pl.pallas_call(
        body, name=name, grid=(n // tn, s // tm),
        in_specs=[pl.BlockSpec((tm, k), lambda j, i: (i, 0)), pl.BlockSpec((tn, k), lambda j, i: (j, 0)), o_spec, o_spec]
        + [pl.BlockSpec(d.shape, lambda j, i: (0, 0)) for d in deps],
        out_specs=[o_spec] * 2, out_shape=[jax.ShapeDtypeStruct((s, n), BF16)] * 2,
        compiler_params=_params(("parallel", "parallel")),
    )(dx_out, w_d, gate, up, *deps)


def _ffn_dh(dgate, dup, w_gt, w_ut, *, name, tm=512):
    s, k = dgate.shape
    n = w_gt.shape[1]
    tm = min(tm, s)

    def body(dg_ref, du_ref, wg_ref, wu_ref, o_ref):
        o_ref[...] = (jnp.dot(dg_ref[...], wg_ref[...], preferred_element_type=F32)
                      + jnp.dot(du_ref[...], wu_ref[...], preferred_element_type=F32)).astype(o_ref.dtype)

    a_spec = pl.BlockSpec((tm, k), lambda i: (i, 0))
    w_spec = pl.BlockSpec((k, n), lambda i: (0, 0))
    return pl.pallas_call(
        body, name=name, grid=(s // tm,), in_specs=[a_spec, a_spec, w_spec, w_spec],
        out_specs=pl.BlockSpec((tm, n), lambda i: (i, 0)), out_shape=jax.ShapeDtypeStruct((s, n), BF16),
        compiler_params=_params(("parallel",)),
    )(dgate, dup, w_gt, w_ut)


def _ffn_fwd(x, g, wc, tag, deps=(), prefetch=()):
    h = _rowwise(_f_norm, [x], [g], [(D, BF16)], name=tag + "_norm", deps=deps)[0]
    gate, up, act = _ffn_gate_up(h, wc.get(tag + '_w_gate', h), wc[tag + '_w_up'], name=tag + "_gate_up")
    for later in (tag + '_w_down',) + tuple(prefetch):
        wc.prefetch(later, gate)
    x_out = _mm(act, wc.get(tag + '_w_down', act), res=x, scale=0.5, name=tag + "_down")
    return x_out, (h, gate, up, act)


def _ffn_bwd(x, g, wc, saved, dx_out, tag, send, deps=()):
    h, gate, up, act = saved
    w_gt, w_ut, w_d = (wc.get(tag + n, h) for n in ('_w_gate', '_w_up', '_w_down'))
    d_d = _mm(act, dx_out, ta=True, scale=0.5, out_dtype=GRAD_DTYPE, name=tag + "_dwdown", deps=deps)
    token = send({tag + '_w_down': d_d})
    dgate, dup = _ffn_dgate_dup(dx_out, w_d, gate, up, name=tag + "_dgate_dup", deps=[token])
    d_gt = _mm(dgate, h, ta=True, out_dtype=GRAD_DTYPE, name=tag + "_dwgate")
    token = send({tag + '_w_gate': d_gt})
    d_ut = _mm(dup, h, ta=True, out_dtype=GRAD_DTYPE, name=tag + "_dwup", deps=[token])
    token = send({tag + '_w_up': d_ut})
    dh = _ffn_dh(dgate, dup, w_gt, w_ut, name=tag + "_dh")
    dx, dg = _rowwise_bwd(_f_norm, [x], [g], [dh], row_grads={0: F32}, const_grads=[0], adds={0: dx_out},
                          name=tag + "_norm_bwd", deps=[token])
    return dx, dg


def _local_step(x, mem, cos, sin, target, wc, ws, send, deps=(), send_small=None):
    gs = {}

    x1, sv1 = _ffn_fwd(x, ws['ffn1_norm'], wc, "ffn1", deps=deps, prefetch=('w_in',))

    h2 = _rowwise(_f_norm, [x1], [ws['mix_norm']], [(D, BF16)], name="mix_norm")[0]
    w_in_raw, w_uq_raw = wc.get('w_in', h2), wc.get('mla_w_uq', h2)
    w_in_e = _expand_w_in(w_in_raw)
    w_uq_e = _expand_w_uq(w_uq_raw)
    proj = _mm(h2, w_in_e, tb=True, name="w_in")
    c_q, c_kv = _rowwise(_f_prep1, [proj], [ws['q_norm'], ws['kv_norm']], [(Q_RANK, BF16), (KV_RANK, BF16)], name="mla_prep1")
    qall = _mm(c_q, w_uq_e, tb=True, out_dtype=BF16, name="w_uq")
    kv = _mm(c_kv, wc['mla_w_ukv'], tb=True, out_dtype=BF16, name="w_ukv")
    q, k, v = _prep2_fwd(qall, kv, proj, cos, sin, ws['qk_gq'], ws['qk_gk'])
    o_mla, lse = _attn_fwd(q, k, v)
    wc.prefetch('ffn2_w_gate', lse)

    u = proj[:, Q_RANK + KV_RANK:Q_RANK + KV_RANK + SSM_W]
    u_p = _time_perm(u)
    disc_in = [ws['ssm_lr'], ws['ssm_li'], ws['ssm_ldt'], ws['ssm_br'], ws['ssm_bi']]
    ar16, ai16, bbr, bbi = _rowwise(_f_disc, disc_in, [], [(SSM_P, F32)] * 4, name="s5_disc")
    a_r = ar16[::SSM_GRP].reshape(1, SSM_N)
    a_i = ai16[::SSM_GRP].reshape(1, SSM_N)
    bblk_r, bblk_i = _block_diag_b(bbr).astype(BF16), _block_diag_b(bbi).astype(BF16)
    cblk_r, cblk_i = _block_diag_c(ws['ssm_cr']).astype(BF16), _block_diag_c(-ws['ssm_ci']).astype(BF16)
    xr, xi, yc = _s5_scan(u_p, bblk_r, bblk_i, a_r, a_i, reverse=False, tb=False, readout=(cblk_r, cblk_i),
                          name="s5_scan_fwd")
    g_p = _rowwise(_f_s5_gelu, [yc, u_p], [ws['ssm_d']], [(SSM_W, F32)], name="s5_gelu")[0]
    z_p = _mm(g_p, wc['ssm_w_glu'], name="s5_glu")
    g_t, z_t = _time_perm(g_p, inverse=True), _time_perm(z_p, inverse=True)
    on_consts = [ws['ssm_b_glu'], ws['out_norm_mla'], ws['out_norm_ssm']]
    ycat = _rowwise(_f_outnorm, [o_mla, g_t, z_t], on_consts, [(D, BF16)], name="out_norm")[0]
    x2 = _mm(ycat, wc['w_o'], res=x1, name="w_o")

    hx = _rowwise(_f_norm, [x2], [ws['xattn_norm']], [(D, BF16)], name="xattn_norm")[0]
    xq = _mm(hx, wc['xattn_w_q'], name="xattn_q")
    mn = _rowwise(_f_norm, [mem], [ws['mem_norm']], [(D, BF16)], name="mem_norm")[0]
    kvm = _mm(mn, wc['xattn_w_kv'], name="xattn_kv")
    xkn, xv = _rowwise(_f_memk, [kvm], [ws['xattn_k_norm']], [(H * XH, BF16), (H * XH, BF16)], name="xattn_knorm")
    xo = _xattn_fwd(xq, xkn, xv, ws['xattn_q_norm'])
    x3 = _mm(xo, wc['xattn_w_o'], tb=True, res=x2, name="xattn_o")

    x4, sv2 = _ffn_fwd(x3, ws['ffn2_norm'], wc, "ffn2")

    def f_loss(yb, tb):
        err = yb - tb
        return err * (1.0 / D), jnp.broadcast_to(jnp.sum(jnp.sum(err * err, axis=1, keepdims=True), axis=0, keepdims=True) * (0.5 / D), (1, LANES))

    dx4, loss = _rowwise(f_loss, [x4, target], [], [(D, F32)], [(1, LANES)], name="loss")

    dx3, gs['ffn2_norm'] = _ffn_bwd(x3, ws['ffn2_norm'], wc, sv2, dx4, "ffn2", send)

    dxo = _mm(dx3, wc['xattn_w_o'], out_dtype=BF16, name="xattn_o_dx")
    send({'xattn_w_o': _mm(dx3, xo, ta=True, out_dtype=GRAD_DTYPE, name="xattn_o_dw")})
    dxq, dxkn, dxv, gs['xattn_q_norm'] = _xattn_bwd(xq, xkn, xv, ws['xattn_q_norm'], dxo)
    dkvm, gs['xattn_k_norm'] = _rowwise_bwd(_f_memk, [kvm], [ws['xattn_k_norm']], [dxkn, dxv], row_grads={0: BF16},
                                            const_grads=[0], name="xattn_knorm_bwd")
    send({'xattn_w_kv': _mm(mn, dkvm, ta=True, out_dtype=GRAD_DTYPE, name="xattn_kv_dw")})
    dmn = _mm(dkvm, wc['xattn_w_kv'], tb=True, out_dtype=BF16, name="xattn_kv_dx")
    gs['mem_norm'] = _rowwise_bwd(_f_norm, [mem], [ws['mem_norm']], [dmn], row_grads={}, const_grads=[0], name="mem_norm_bwd")[0]
    token = send({'xattn_w_q': _mm(hx, dxq, ta=True, out_dtype=GRAD_DTYPE, name="xattn_q_dw")})
    dhx = _mm(dxq, wc['xattn_w_q'], tb=True, out_dtype=BF16, name="xattn_q_dx")
    dx2, gs['xattn_norm'] = _rowwise_bwd(_f_norm, [x2], [ws['xattn_norm']], [dhx], row_grads={0: F32}, const_grads=[0],
                                         adds={0: dx3}, name="xattn_norm_bwd", deps=[token])

    dycat = _mm(dx2, wc['w_o'], tb=True, out_dtype=BF16, name="w_o_dx")
    send({'w_o': _mm(ycat, dx2, ta=True, out_dtype=GRAD_DTYPE, name="w_o_dw")})
    do_mla, dg_t, dz_t, gs['ssm_b_glu'], gs['out_norm_mla'], gs['out_norm_ssm'] = _rowwise_bwd(
        _f_outnorm, [o_mla, g_t, z_t], on_consts, [dycat], row_grads={0: F32, 1: F32, 2: BF16}, const_grads=[0, 1, 2],
        name="out_norm_bwd")

    dz_p, dg_p = _time_perm(dz_t), _time_perm(dg_t)
    send({'ssm_w_glu': _mm(g_p, dz_p, ta=True, out_dtype=GRAD_DTYPE, name="s5_glu_dw")})
    dg_p = _mm(dz_p, wc['ssm_w_glu'], tb=True, res=dg_p, name="s5_glu_dx")
    dyc, du_d, gs['ssm_d'] = _rowwise_bwd(_f_s5_gelu, [yc, u_p], [ws['ssm_d']], [dg_p], row_grads={0: BF16, 1: F32},
                                          const_grads=[0], name="s5_gelu_bwd")
    lam_r, lam_i = _s5_scan(dyc, cblk_r, cblk_i, a_r, -a_i, reverse=True, tb=True, name="s5_scan_bwd")
    du_p, d_bblk_r, d_bblk_i, d_cblk_r, d_cblk_i, d_ar, d_ai = _s5_grads(lam_r, lam_i, xr, xi, u_p, dyc, du_d,
                                                                        bblk_r, bblk_i)
    du = _time_perm(du_p, inverse=True)
    gs['ssm_cr'] = jax.linear_transpose(_block_diag_c, ws['ssm_cr'])(d_cblk_r)[0]
    gs['ssm_ci'] = -jax.linear_transpose(_block_diag_c, ws['ssm_ci'])(d_cblk_i)[0]
    d_bbr = jax.linear_transpose(_block_diag_b, bbr)(d_bblk_r)[0]
    d_bbi = jax.linear_transpose(_block_diag_b, bbi)(d_bblk_i)[0]
    d_ar16 = jnp.zeros((SSM_G * SSM_GRP, SSM_P), F32).at[::SSM_GRP].set(d_ar.reshape(SSM_G, SSM_P))
    d_ai16 = jnp.zeros((SSM_G * SSM_GRP, SSM_P), F32).at[::SSM_GRP].set(d_ai.reshape(SSM_G, SSM_P))
    gs['ssm_lr'], gs['ssm_li'], gs['ssm_ldt'], gs['ssm_br'], gs['ssm_bi'] = _rowwise_bwd(
        _f_disc, disc_in, [], [d_ar16, d_ai16, d_bbr, d_bbi], row_grads={i: F32 for i in range(5)}, const_grads=[],
        name="s5_disc_bwd")

    delta, do_b = _rowwise(_f_delta, [do_mla, o_mla], [], [(H * LANES, F32), (H * VD, BF16)], name="mla_delta")
    dq, dk, dv = _attn_bwd(q, k, v, do_b, lse, delta)
    dqall, dkv, dkr, dkrs, gs['qk_gq'], gs['qk_gk'] = _prep2_bwd(qall, kv, proj, cos, sin, ws['qk_gq'], ws['qk_gk'], dq, dk, dv)
    d_w_uq_e = _mm(dqall, c_q, ta=True, name="w_uq_dw")
    send({'mla_w_uq': jax.linear_transpose(_expand_w_uq, jax.ShapeDtypeStruct(w_uq_raw.shape, F32))(d_w_uq_e)[0]})
    dc_q = _mm(dqall, w_uq_e, out_dtype=BF16, name="w_uq_dx")
    send({'mla_w_ukv': _mm(dkv, c_kv, ta=True, out_dtype=GRAD_DTYPE, name="w_ukv_dw")})
    dc_kv = _mm(dkv, wc['mla_w_ukv'], out_dtype=BF16, name="w_ukv_dx")

    def f_prep1_bwd(pb, dcq, dckv, dub, dkrb, dkrsb, gq, gkv):
        _, vjp = jax.vjp(_f_prep1, pb[:, :Q_RANK + KV_RANK], gq, gkv)
        dpa, dgq, dgkv = vjp((dcq.astype(BF16), dckv.astype(BF16)))
        return jnp.concatenate([dpa, dub, dkrb, dkrsb], axis=-1), dgq, dgkv

    dproj, gs['q_norm'], gs['kv_norm'] = _rowwise(
        f_prep1_bwd, [proj, dc_q, dc_kv, du, dkr, dkrs], [ws['q_norm'], ws['kv_norm']], [(IN_WP, BF16)],
        [(1, Q_RANK), (1, KV_RANK)], name="mla_prep1_bwd")
    d_w_in_e = _mm(dproj, h2, ta=True, name="w_in_dw")
    token = send({'w_in': jax.linear_transpose(_expand_w_in, jax.ShapeDtypeStruct(w_in_raw.shape, F32))(d_w_in_e)[0]})
    dh2 = _mm(dproj, w_in_e, out_dtype=BF16, name="w_in_dx")
    dx1, gs['mix_norm'] = _rowwise_bwd(_f_norm, [x1], [ws['mix_norm']], [dh2], row_grads={0: F32}, const_grads=[0],
                                       adds={0: dx2}, name="mix_norm_bwd", deps=[token])

    token = send_small(gs, loss) if send_small is not None else None
    dx0, gs['ffn1_norm'] = _ffn_bwd(x, ws['ffn1_norm'], wc, sv1, dx1, "ffn1", send, deps=[token])
    return loss, dx0, gs


def _prep2_rows(qall, kv, proj, cos, sin):
    return [qall, kv, (proj, KR_BLOCK, LANES), (proj, KR_BLOCK + 1, LANES), cos, sin]


def _prep2_fwd(qall, kv, proj, cos, sin, gq, gk):
    return _rowwise(_f_prep2, _prep2_rows(qall, kv, proj, cos, sin), [gq, gk],
                    [(H * HQ, BF16), (H * HQ, BF16), (H * VD, BF16)], ts=256, name="mla_prep2")


def _prep2_bwd(qall, kv, proj, cos, sin, gq, gk, dq, dk, dv):
    return _rowwise_bwd(_f_prep2, _prep2_rows(qall, kv, proj, cos, sin), [gq, gk], [dq, dk, dv],
                        row_grads={0: BF16, 1: BF16, 2: F32, 3: F32}, const_grads=[0, 1], ts=256, name="mla_prep2_bwd")


def _rope_tables(pos):
    half = ROPE // 2
    inv = ROPE_THETA ** (-jnp.arange(half, dtype=F32) / half)
    ang = pos.astype(F32)[:, None] * inv[None, :]
    z = jnp.zeros((pos.shape[0], LANES - ROPE), F32)
    cos, sin = jnp.cos(ang), jnp.sin(ang)
    return jnp.concatenate([cos, cos, z], axis=-1), jnp.concatenate([sin, sin, z], axis=-1)


def _small_layout(p):
    lr, li, ldt, br, bi = _layout_ssm_in(p['ssm_a_re'], p['ssm_a_im'], p['ssm_log_dt'], p['ssm_b_re'], p['ssm_b_im'])
    return {
        'ffn1_norm': p['ffn1_norm'].reshape(1, D), 'mix_norm': p['mix_norm'].reshape(1, D),
        'q_norm': p['mla_q_norm'].reshape(1, Q_RANK), 'kv_norm': p['mla_kv_norm'].reshape(1, KV_RANK),
        'qk_gq': _layout_qk_gain(p['mla_qk_norm_q']), 'qk_gk': _layout_qk_gain(p['mla_qk_norm_k']),
        'ssm_lr': lr, 'ssm_li': li, 'ssm_ldt': ldt, 'ssm_br': br, 'ssm_bi': bi,
        'ssm_cr': p['ssm_c_re'], 'ssm_ci': p['ssm_c_im'], 'ssm_d': p['ssm_d'].reshape(1, SSM_W),
        'ssm_b_glu': p['ssm_b_glu'].reshape(1, SSM_W),
        'out_norm_mla': p['out_norm_mla'].reshape(1, SSM_W), 'out_norm_ssm': p['out_norm_ssm'].reshape(1, SSM_W),
        'xattn_norm': p['xattn_norm'].reshape(1, D), 'mem_norm': p['mem_norm'].reshape(1, D),
        'xattn_q_norm': p['xattn_q_norm'].reshape(1, XH), 'xattn_k_norm': p['xattn_k_norm'].reshape(1, XH),
        'ffn2_norm': p['ffn2_norm'].reshape(1, D),
    }


def _pack(arrs, rows):
    flat = jnp.concatenate([a.reshape(-1) for a in arrs])
    return jnp.pad(flat, (0, rows * D - flat.shape[0])).reshape(rows, D)


def _unpack(flat, shapes):
    flat = flat.reshape(-1)
    out, off = [], 0
    for sh in shapes:
        n = int(np.prod(sh))
        out.append(flat[off:off + n].reshape(sh))
        off += n
    return out


def kernel(x, mem, positions, ffn1_norm, ffn1_w_gate, ffn1_w_up, ffn1_w_down, mix_norm, w_in, mla_q_norm, mla_w_uq, mla_kv_norm, mla_w_ukv, mla_qk_norm_q, mla_qk_norm_k, ssm_a_re, ssm_a_im, ssm_log_dt, ssm_b_re, ssm_b_im, ssm_c_re, ssm_c_im, ssm_d, ssm_w_glu, ssm_b_glu, out_norm_mla, out_norm_ssm, w_o, xattn_norm, mem_norm, xattn_w_q, xattn_w_kv, xattn_q_norm, xattn_k_norm, xattn_w_o, ffn2_norm, ffn2_w_gate, ffn2_w_up, ffn2_w_down, loss_target, m_ffn1_norm, m_ffn1_w_gate, m_ffn1_w_up, m_ffn1_w_down, m_mix_norm, m_w_in, m_mla_q_norm, m_mla_w_uq, m_mla_kv_norm, m_mla_w_ukv, m_mla_qk_norm_q, m_mla_qk_norm_k, m_ssm_a_re, m_ssm_a_im, m_ssm_log_dt, m_ssm_b_re, m_ssm_b_im, m_ssm_c_re, m_ssm_c_im, m_ssm_d, m_ssm_w_glu, m_ssm_b_glu, m_out_norm_mla, m_out_norm_ssm, m_w_o, m_xattn_norm, m_mem_norm, m_xattn_w_q, m_xattn_w_kv, m_xattn_q_norm, m_xattn_k_norm, m_xattn_w_o, m_ffn2_norm, m_ffn2_w_gate, m_ffn2_w_up, m_ffn2_w_down, v_ffn1_norm, v_ffn1_w_gate, v_ffn1_w_up, v_ffn1_w_down, v_mix_norm, v_w_in, v_mla_q_norm, v_mla_w_uq, v_mla_kv_norm, v_mla_w_ukv, v_mla_qk_norm_q, v_mla_qk_norm_k, v_ssm_a_re, v_ssm_a_im, v_ssm_log_dt, v_ssm_b_re, v_ssm_b_im, v_ssm_c_re, v_ssm_c_im, v_ssm_d, v_ssm_w_glu, v_ssm_b_glu, v_out_norm_mla, v_out_norm_ssm, v_w_o, v_xattn_norm, v_mem_norm, v_xattn_w_q, v_xattn_w_kv, v_xattn_q_norm, v_xattn_k_norm, v_xattn_w_o, v_ffn2_norm, v_ffn2_w_gate, v_ffn2_w_up, v_ffn2_w_down):
    args = dict(locals())
    w = {n: args[n] for n in WEIGHTS}
    mom = {n: args['m_' + n] for n in WEIGHTS}
    var = {n: args['v_' + n] for n in WEIGHTS}
    return _step(x, mem, positions, loss_target, w, mom, var)


GATHER_GROUPS = [('ffn1_gu', ['ffn1_w_gate', 'ffn1_w_up']), ('ffn1_down', ['ffn1_w_down']),
                 ('mix', ['w_in', 'mla_w_uq', 'mla_w_ukv', 'ssm_w_glu', 'w_o', 'xattn_w_q', 'xattn_w_kv', 'xattn_w_o']),
                 ('ffn2', ['ffn2_w_gate', 'ffn2_w_up', 'ffn2_w_down'])]
SCATTER_GROUPS = [('ffn2_down', ['ffn2_w_down']), ('ffn2_gate', ['ffn2_w_gate']), ('ffn2_up', ['ffn2_w_up']),
                  ('xattn', ['xattn_w_o', 'xattn_w_kv', 'xattn_w_q']),
                  ('mix', ['w_o', 'ssm_w_glu', 'mla_w_uq', 'mla_w_ukv', 'w_in']),
                  ('ffn1_down', ['ffn1_w_down']), ('ffn1_gate', ['ffn1_w_gate']), ('ffn1_up', ['ffn1_w_up'])]


def _step(x, mem, positions, loss_target, w, mom, var):
    blocks = {n: _to_exchange_layout(n, w[n][0]).astype(BF16) for n in SHARDED}
    gathers, token = [], None
    for tag, names in GATHER_GROUPS:
        ex = _Exchange([blocks[n] for n in names], [blocks[n].shape[0] for n in names], gather=True,
                       name="gather_" + tag, after=token, two_level=True)
        gathers.append((names, ex))
        token = ex.token
    me = _my_slot()
    wc = _Weights(gathers, me=me)

    rows = {n: (blocks[n].shape[0], blocks[n].shape[0]) for n in SHARDED}
    ready, scatters = {}, []

    def send(grads):
        ready.update({n: g.astype(GRAD_DTYPE) for n, g in grads.items()})
        for tag, names in SCATTER_GROUPS:
            if all(n in ready for n in names) and not any(t == tag for t, _, _ in scatters):
                ex = _Exchange([ready[n] for n in names], [rows[n] for n in names], gather=False, name="scatter_" + tag)
                scatters.append((tag, names, ex))
                return ex.token
        return None

    small = {n: w[n][0] for n in SMALL}
    small_shapes = [small[n].shape for n in SMALL]
    n_small = sum(int(np.prod(sh)) for sh in small_shapes) + 1
    rows_small = -(-n_small // (8 * D)) * 8
    small_sent = []

    def send_small(gs, loss):
        known = dict(gs, ffn1_norm=jnp.zeros((1, D), F32))
        g_small = jax.linear_transpose(_small_layout, {n: jax.ShapeDtypeStruct(small[n].shape, F32) for n in SMALL})(known)[0]
        pack = _pack([g_small[n] for n in SMALL] + [loss[0, :1]], rows_small)
        small_sent.append(_Exchange([pack], [(None, rows_small)], gather=False, name="scatter_small"))
        return small_sent[0].token

    ws = _small_layout(small)
    cos, sin = _rope_tables(positions[0])
    loss, dx, gs = _local_step(x[0], mem[0], cos, sin, loss_target[0], wc, ws, send, deps=[token], send_small=send_small)
    pad8 = lambda a: jnp.pad(a.reshape(1, D), ((0, 7), (0, 0)))
    last_ex = _Exchange([pad8(gs['ffn1_norm'])], [(None, 8)], gather=False, name="scatter_last")

    out, after = {}, dx
    for _, names, ex in scatters:
        for n, sent, p in zip(names, *ex.wait(after)):
            r = w[n][0].shape[SHARD_AXIS[n]]
            if SHARD_AXIS[n] == 0:
                out[n] = _sum_adamw(me, sent, rows[n][0], p, r, w[n][0], mom[n][0], var[n][0], name="adamw_" + n)
            else:
                g = _sum_adamw(me, sent, rows[n][0], p, r, name="sum_" + n)[0].T
                out[n] = [g] + _adamw(g, w[n][0], mom[n][0], var[n][0], name="adamw_" + n)
        after = out[names[-1]][1]
    state = [_pack([t[n][0] for n in SMALL], rows_small) for t in (w, mom, var)]
    sent, p = small_sent[0].wait(after)
    small_out = _sum_adamw(me, sent[0], None, p[0], rows_small, *state, name="adamw_small")
    loss_total = small_out[0].reshape(-1)[n_small - 1]
    for n, vals in zip(SMALL, zip(*[_unpack(flat, small_shapes) for flat in small_out])):
        out[n] = vals
    sent, p = last_ex.wait(small_out[1])
    last_out = _sum_adamw(me, sent[0], None, p[0], 8, *[pad8(t['ffn1_norm'][0]) for t in (w, mom, var)], name="adamw_last")
    out['ffn1_norm'] = [o[0] for o in last_out]
    outs = [out[n][i][None] for i in range(4) for n in WEIGHTS]
    return (loss_total, dx[None], *outs)
```

```python
import math

import jax
import jax.numpy as jnp
import numpy as np
from jax import lax
from jax.experimental import pallas as pl
from jax.experimental.pallas import tpu as pltpu

F32 = jnp.float32
BF16 = jnp.bfloat16

N_DEV = 8
D = 1024
D_FF = 2752
D_FFP = 2816
MEM_LEN = 256
H = 4
Q_RANK, KV_RANK, NOPE, ROPE, VD = 384, 256, 128, 64, 128
QK = NOPE + ROPE
HQ = 2 * 128
SSM_W, SSM_G, SSM_GRP, SSM_P = 512, 32, 16, 64
SSM_N = SSM_G * SSM_P
SSM_PACK = 8
IN_W = 1216
IN_WP = 1408
XH = 128
EPS = 1e-6
LN2 = math.log(2.0)
ROPE_THETA = 10000.0
SCAN_CHUNKS = 8
SCAN_UNROLL = 8
ADAM_LR, ADAM_B1, ADAM_B2, ADAM_EPS, ADAM_WD, ADAM_STEP = 0.001, 0.9, 0.999, 1e-08, 0.01, 10

VMEM_LIMIT = 56 * 1024 * 1024
ACC_BYTES = 6 * 1024 * 1024
LANES = 128
BF16_ROWS = 16
GRAD_DTYPE = BF16
FF_SHARD = D_FF // N_DEV
FF_SHARD_P = 352
IN_SHARD = IN_W // N_DEV
IN_SHARD_P = 160

WEIGHTS = ['ffn1_norm', 'ffn1_w_gate', 'ffn1_w_up', 'ffn1_w_down', 'mix_norm', 'w_in', 'mla_q_norm', 'mla_w_uq',
           'mla_kv_norm', 'mla_w_ukv', 'mla_qk_norm_q', 'mla_qk_norm_k', 'ssm_a_re', 'ssm_a_im', 'ssm_log_dt',
           'ssm_b_re', 'ssm_b_im', 'ssm_c_re', 'ssm_c_im', 'ssm_d', 'ssm_w_glu', 'ssm_b_glu', 'out_norm_mla',
           'out_norm_ssm', 'w_o', 'xattn_norm', 'mem_norm', 'xattn_w_q', 'xattn_w_kv', 'xattn_q_norm',
           'xattn_k_norm', 'xattn_w_o', 'ffn2_norm', 'ffn2_w_gate', 'ffn2_w_up', 'ffn2_w_down']
SHARD_AXIS = {'ffn1_w_gate': 1, 'ffn1_w_up': 1, 'ffn1_w_down': 0, 'w_in': 1, 'mla_w_uq': 1, 'mla_w_ukv': 1,
              'ssm_w_glu': 0, 'w_o': 0, 'xattn_w_q': 0, 'xattn_w_kv': 0, 'xattn_w_o': 1,
              'ffn2_w_gate': 1, 'ffn2_w_up': 1, 'ffn2_w_down': 0}
SHARDED = [n for n in WEIGHTS if n in SHARD_AXIS]
SMALL = [n for n in WEIGHTS if n not in SHARD_AXIS]


def _params(sem=None):
    return pltpu.CompilerParams(dimension_semantics=sem, vmem_limit_bytes=VMEM_LIMIT)


def _tile(n, cap):
    if n <= cap:
        return n
    best = n
    for t in range(LANES, cap + 1, LANES):
        if n % t == 0:
            best = t
    return best


def _mm(a, b, *, ta=False, tb=False, out_dtype=F32, res=None, scale=1.0, name, tm_cap=1024, tn_cap=1408, tk_cap=2816,
        deps=()):
    m, k = (a.shape[1], a.shape[0]) if ta else a.shape
    k2, n = (b.shape[1], b.shape[0]) if tb else b.shape
    assert k == k2, (a.shape, b.shape, ta, tb)
    if ta:
        tk_cap = min(tk_cap, 512)
        tm_cap = 1408
    tm, tn, tk = _tile(m, tm_cap), _tile(n, tn_cap), _tile(k, tk_cap)
    if tm * tn * 4 > ACC_BYTES:
        tn = _tile(n, max(LANES, ACC_BYTES // (4 * tm) // LANES * LANES))
    nk = k // tk
    dims = (((0 if ta else 1,), (1 if tb else 0,)), ((), ()))
    has_res = res is not None

    deps = [d for d in deps if d is not None]

    def body(*refs):
        a_ref, b_ref = refs[:2]
        r_ref = refs[2] if has_res else None
        o_ref, acc_ref = refs[-2:]
        kk = pl.program_id(2)

        @pl.when(kk == 0)
        def _():
            acc_ref[...] = jnp.zeros_like(acc_ref)

        acc_ref[...] += lax.dot_general(a_ref[...].astype(BF16), b_ref[...].astype(BF16), dims,
                                        preferred_element_type=F32)

        @pl.when(kk == nk - 1)
        def _():
            out = acc_ref[...]
            if scale != 1.0:
                out = out * scale
            if has_res:
                out = out + r_ref[...].astype(F32)
            o_ref[...] = out.astype(o_ref.dtype)

    a_spec = pl.BlockSpec((tk, tm), lambda i, j, kk: (kk, i)) if ta else pl.BlockSpec((tm, tk), lambda i, j, kk: (i, kk))
    b_spec = pl.BlockSpec((tn, tk), lambda i, j, kk: (j, kk)) if tb else pl.BlockSpec((tk, tn), lambda i, j, kk: (kk, j))
    o_spec = pl.BlockSpec((tm, tn), lambda i, j, kk: (i, j))
    in_specs = [a_spec, b_spec] + ([o_spec] if has_res else []) + [pl.BlockSpec(d.shape, lambda i, j, kk: (0, 0)) for d in deps]
    args = (a, b) + ((res,) if has_res else ()) + tuple(deps)
    return pl.pallas_call(
        body, name=name, grid=(m // tm, n // tn, nk), in_specs=in_specs, out_specs=o_spec,
        out_shape=jax.ShapeDtypeStruct((m, n), out_dtype), scratch_shapes=[pltpu.VMEM((tm, tn), F32)],
        compiler_params=_params(("parallel", "parallel", "arbitrary")),
    )(*args)


def _rowwise(fn, rows, consts, outs, accs=(), *, ts=512, name, deps=()):
    rows = [r if isinstance(r, tuple) else (r, 0, r.shape[1]) for r in rows]
    s = rows[0][0].shape[0]
    ts = min(ts, s)
    assert s % ts == 0
    n_rows, n_consts, n_outs = len(rows), len(consts), len(outs)
    deps = [d for d in deps if d is not None]
    consts = list(consts) + deps

    def body(*refs):
        ins = [r[...] for r in refs[:n_rows + n_consts]]
        res = fn(*ins)
        res = tuple(res) if isinstance(res, (tuple, list)) else (res,)
        out_refs = refs[n_rows + len(consts):]
        for o_ref, val in zip(out_refs[:n_outs], res[:n_outs]):
            o_ref[...] = val.astype(o_ref.dtype)
        if accs:
            first = pl.program_id(0) == 0

            @pl.when(first)
            def _():
                for a_ref, val in zip(out_refs[n_outs:], res[n_outs:]):
                    a_ref[...] = val.astype(F32)

            @pl.when(jnp.logical_not(first))
            def _():
                for a_ref, val in zip(out_refs[n_outs:], res[n_outs:]):
                    a_ref[...] += val.astype(F32)

    in_specs = [pl.BlockSpec((ts, width), lambda i, cb=cb: (i, cb)) for _, cb, width in rows]
    in_specs += [pl.BlockSpec(c.shape, lambda i: (0, 0)) for c in consts]
    out_specs = [pl.BlockSpec((ts, w), lambda i: (i, 0)) for w, _ in outs]
    out_specs += [pl.BlockSpec(tuple(sh), lambda i: (0, 0)) for sh in accs]
    out_shape = [jax.ShapeDtypeStruct((s, w), dt) for w, dt in outs]
    out_shape += [jax.ShapeDtypeStruct(tuple(sh), F32) for sh in accs]
    res = pl.pallas_call(
        body, name=name, grid=(s // ts,), in_specs=in_specs, out_specs=out_specs, out_shape=out_shape,
        compiler_params=_params(("arbitrary",)),
    )(*[a for a, _, _ in rows], *consts)
    return res


def _rowwise_bwd(f, rows, consts, cts, *, row_grads, const_grads, adds=None, ts=512, name, deps=()):
    adds = adds or {}
    n_rows, n_consts, n_cts = len(rows), len(consts), len(cts)
    add_keys = sorted(adds)
    rg = sorted(row_grads)
    cg = sorted(const_grads)

    def fn(*args):
        r = args[:n_rows]
        c = args[n_rows:n_rows + n_consts]
        ct = args[n_rows + n_consts:n_rows + n_consts + n_cts]
        extra = args[n_rows + n_consts + n_cts:]
        outs, vjp = jax.vjp(f, *r, *c)
        outs = tuple(outs) if isinstance(outs, (tuple, list)) else (outs,)
        cot = tuple(g.astype(o.dtype) for g, o in zip(ct, outs))
        grads = vjp(cot if len(cot) > 1 else cot[0])
        res = []
        for i in rg:
            g = grads[i].astype(F32)
            if i in adds:
                g = g + extra[add_keys.index(i)].astype(F32)
            res.append(g)
        for i in cg:
            res.append(grads[n_rows + i])
        return tuple(res)

    rows_all = list(rows) + list(cts) + [adds[i] for i in add_keys]
    def fn2(*args):
        nr = len(rows_all)
        rr, cc = args[:nr], args[nr:]
        return fn(*rr[:n_rows], *cc, *rr[n_rows:])

    outs = [(rows[i][2] if isinstance(rows[i], tuple) else rows[i].shape[1], row_grads[i]) for i in rg]
    accs = [consts[i].shape for i in cg]
    return _rowwise(fn2, rows_all, list(consts), outs, accs, ts=ts, name=name, deps=deps)


def _rms(x, g):
    xf = x.astype(F32)
    return xf * lax.rsqrt(jnp.mean(xf * xf, axis=-1, keepdims=True) + EPS) * g.astype(F32)


def _sigmoid(x):
    return 1.0 / (1.0 + jnp.exp(-x))


def _f_norm(x, g):
    return _rms(x, g).astype(BF16)


def _f_swiglu(gate, up):
    gate, up = gate.astype(F32), up.astype(F32)
    return (gate * _sigmoid(gate) * up).astype(BF16)


def _f_prep1(proj, gq, gkv):
    return _rms(proj[:, :Q_RANK], gq).astype(BF16), _rms(proj[:, Q_RANK:Q_RANK + KV_RANK], gkv).astype(BF16)


KR_BLOCK = (Q_RANK + KV_RANK + SSM_W) // LANES


def _f_prep2(qall, kv, kr, krs, cos, sin, gq, gk):
    kr, krs = kr.astype(F32), krs.astype(F32)
    k_rot = kr * gk[1:2] * cos + krs * gk[2:3] * sin
    k_ss = jnp.sum(kr * kr, axis=-1, keepdims=True)
    q_scale = QK ** -0.5 / LN2
    qs, ks, vs = [], [], []
    for h in range(H):
        qn = qall[:, h * LANES:(h + 1) * LANES].astype(F32)
        qr = qall[:, (H + h) * LANES:(H + h + 1) * LANES].astype(F32)
        qrs = qall[:, (2 * H + h) * LANES:(2 * H + h + 1) * LANES].astype(F32)
        rstd = lax.rsqrt((jnp.sum(qn * qn, axis=-1, keepdims=True) + jnp.sum(qr * qr, axis=-1, keepdims=True)) / QK + EPS)
        rstd = rstd * q_scale
        qs += [qn * gq[0:1] * rstd, (qr * gq[1:2] * cos + qrs * gq[2:3] * sin) * rstd]
        kn = kv[:, 2 * h * LANES:(2 * h + 1) * LANES].astype(F32)
        rstd_k = lax.rsqrt((jnp.sum(kn * kn, axis=-1, keepdims=True) + k_ss) / QK + EPS)
        ks += [kn * gk[0:1] * rstd_k, k_rot * rstd_k]
        vs.append(kv[:, (2 * h + 1) * LANES:(2 * h + 2) * LANES])
    return (jnp.concatenate(qs, axis=-1).astype(BF16), jnp.concatenate(ks, axis=-1).astype(BF16),
            jnp.concatenate(vs, axis=-1).astype(BF16))


def _gelu(x):
    return 0.5 * x * (1.0 + jnp.tanh(math.sqrt(2.0 / math.pi) * (x + 0.044715 * (x * x * x))))


def _f_s5_gelu(yc, u, d):
    return _gelu(yc.astype(F32) + d * u.astype(F32))


def _f_outnorm(o_mla, g, z, b_glu, g_om, g_os):
    y_ssm = g * _sigmoid(z + b_glu)
    return jnp.concatenate([_rms(o_mla, g_om), _rms(y_ssm, g_os)], axis=-1).astype(BF16)


def _f_memk(kvm, gk):
    ks = [_rms(kvm[:, h * XH:(h + 1) * XH], gk) for h in range(H)]
    return jnp.concatenate(ks, axis=-1).astype(BF16), kvm[:, H * XH:].astype(BF16)


def _f_disc(lr, li, log_dt, br, bi):
    dt = jnp.exp(log_dt)
    decay = jnp.exp(lr * dt)
    ar = decay * jnp.cos(li * dt)
    ai = decay * jnp.sin(li * dt)
    den = lr * lr + li * li
    nr = ar - 1.0
    coef_r = (nr * lr + ai * li) / den
    coef_i = (ai * lr - nr * li) / den
    return ar, ai, coef_r * br - coef_i * bi, coef_r * bi + coef_i * br


def _causal_mask(i, j, tq, tk):
    qpos = i * tq + lax.broadcasted_iota(jnp.int32, (tq, tk), 0)
    kpos = j * tk + lax.broadcasted_iota(jnp.int32, (tq, tk), 1)
    return qpos >= kpos


def _attn_fwd(q, k, v, *, t=512):
    s = q.shape[0]
    t = min(t, s)
    nb = s // t

    def body(q_ref, k_ref, v_ref, o_ref, lse_ref, m_sc, l_sc, acc_sc):
        i, j = pl.program_id(1), pl.program_id(2)

        @pl.when(j == 0)
        def _():
            m_sc[...] = jnp.full_like(m_sc, -jnp.inf)
            l_sc[...] = jnp.zeros_like(l_sc)
            acc_sc[...] = jnp.zeros_like(acc_sc)

        def block(diagonal):
            sc = lax.dot_general(q_ref[...], k_ref[...], (((1,), (1,)), ((), ())), preferred_element_type=F32)
            if diagonal:
                sc = jnp.where(_causal_mask(i, j, t, t), sc, -jnp.inf)
            m_old = m_sc[...]
            m_new = jnp.maximum(m_old, jnp.max(sc, axis=-1, keepdims=True))
            p = jnp.exp2(sc - m_new)
            alpha = jnp.exp2(m_old - m_new)
            l_sc[...] = alpha * l_sc[...] + jnp.sum(p, axis=-1, keepdims=True)
            acc_sc[...] = alpha * acc_sc[...] + jnp.dot(p.astype(BF16), v_ref[...], preferred_element_type=F32)
            m_sc[...] = m_new

        pl.when(j < i)(lambda: block(False))

        @pl.when(j == i)
        def _():
            block(True)
            o_ref[...] = acc_sc[...] / l_sc[...]
            lse_ref[...] = jnp.broadcast_to(m_sc[...] + jnp.log2(l_sc[...]), lse_ref.shape)

    kv_map = lambda h, i, j: (jnp.minimum(j, i), h)
    return pl.pallas_call(
        body, name="mla_attn_fwd", grid=(H, nb, nb),
        in_specs=[pl.BlockSpec((t, HQ), lambda h, i, j: (i, h)), pl.BlockSpec((t, HQ), kv_map),
                  pl.BlockSpec((t, VD), kv_map)],
        out_specs=[pl.BlockSpec((t, VD), lambda h, i, j: (i, h)), pl.BlockSpec((t, LANES), lambda h, i, j: (i, h))],
        out_shape=[jax.ShapeDtypeStruct((s, H * VD), F32), jax.ShapeDtypeStruct((s, H * LANES), F32)],
        scratch_shapes=[pltpu.VMEM((t, 1), F32), pltpu.VMEM((t, 1), F32), pltpu.VMEM((t, VD), F32)],
        compiler_params=_params(("parallel", "parallel", "arbitrary")),
    )(q, k, v)


def _attn_probs(q_ref, k_ref, v_ref, do_ref, lse_ref, dl_ref, i, j, t, diagonal):
    sc = lax.dot_general(q_ref[...], k_ref[...], (((1,), (1,)), ((), ())), preferred_element_type=F32)
    p = jnp.exp2(sc - jnp.tile(lse_ref[...], (1, t // LANES)))
    if diagonal:
        p = jnp.where(_causal_mask(i, j, t, t), p, 0.0)
    dp = lax.dot_general(do_ref[...], v_ref[...], (((1,), (1,)), ((), ())), preferred_element_type=F32)
    ds = p * (dp - jnp.tile(dl_ref[...], (1, t // LANES)))
    return p, ds


def _attn_bwd(q, k, v, do, lse, delta, *, t=512):
    s = q.shape[0]
    t = min(t, s)
    nb = s // t

    def body(q_ref, k_ref, v_ref, do_ref, lse_ref, dl_ref, dq_ref, dk_ref, dv_ref, dk_sc, dv_sc):
        j, i = pl.program_id(1), pl.program_id(2)

        @pl.when(jnp.logical_and(i == 0, j == 0))
        def _():
            dq_ref[...] = jnp.zeros_like(dq_ref)

        @pl.when(i == 0)
        def _():
            dk_sc[...] = jnp.zeros_like(dk_sc)
            dv_sc[...] = jnp.zeros_like(dv_sc)

        def block(diagonal):
            p, ds = _attn_probs(q_ref, k_ref, v_ref, do_ref, lse_ref, dl_ref, i, j, t, diagonal)
            dsb = ds.astype(BF16)
            dv_sc[...] += lax.dot_general(p.astype(BF16), do_ref[...], (((0,), (0,)), ((), ())), preferred_element_type=F32)
            dk_sc[...] += lax.dot_general(dsb, q_ref[...], (((0,), (0,)), ((), ())), preferred_element_type=F32)
            rows = pl.ds(pl.multiple_of(i * t, t), t)
            dq_ref[rows, :] += jnp.dot(dsb, k_ref[...], preferred_element_type=F32)

        pl.when(i > j)(lambda: block(False))
        pl.when(i == j)(lambda: block(True))

        @pl.when(i == nb - 1)
        def _():
            dk_ref[...] = (dk_sc[...] * LN2).astype(dk_ref.dtype)
            dv_ref[...] = dv_sc[...].astype(dv_ref.dtype)

        @pl.when(jnp.logical_and(i == nb - 1, j == nb - 1))
        def _():
            dq_ref[...] = dq_ref[...] * LN2

    q_map = lambda h, j, i: (jnp.maximum(i, j), h)
    kv_map = lambda h, j, i: (j, h)
    dq, dk, dv = pl.pallas_call(
        body, name="mla_attn_bwd", grid=(H, nb, nb),
        in_specs=[pl.BlockSpec((t, HQ), q_map), pl.BlockSpec((t, HQ), kv_map), pl.BlockSpec((t, VD), kv_map),
                  pl.BlockSpec((t, VD), q_map), pl.BlockSpec((t, LANES), q_map), pl.BlockSpec((t, LANES), q_map)],
        out_specs=[pl.BlockSpec((s, HQ), lambda h, j, i: (0, h)), pl.BlockSpec((t, HQ), kv_map), pl.BlockSpec((t, VD), kv_map)],
        out_shape=[jax.ShapeDtypeStruct((s, H * HQ), F32), jax.ShapeDtypeStruct((s, H * HQ), BF16),
                   jax.ShapeDtypeStruct((s, H * VD), BF16)],
        scratch_shapes=[pltpu.VMEM((t, HQ), F32), pltpu.VMEM((t, VD), F32)],
        compiler_params=_params(("parallel", "arbitrary", "arbitrary")),
    )(q, k, v, do, lse, delta)
    return dq, dk, dv


def _f_delta(do, o):
    prod = do.astype(F32) * o.astype(F32)
    parts = [jnp.broadcast_to(jnp.sum(prod[:, h * VD:(h + 1) * VD], axis=-1, keepdims=True), (do.shape[0], LANES))
             for h in range(H)]
    return jnp.concatenate(parts, axis=-1), do.astype(BF16)


def _xattn_head(qh, kh, gq):
    qn = _rms(qh, gq) * (XH ** -0.5)
    sc = lax.dot_general(qn.astype(BF16), kh, (((1,), (1,)), ((), ())), preferred_element_type=F32)
    sc = sc - jnp.max(sc, axis=-1, keepdims=True)
    e = jnp.exp(sc)
    return qn, e / jnp.sum(e, axis=-1, keepdims=True)


def _xattn_fwd(q, kn, v, gq, *, ts=512):
    def fn(qb, knb, vb, g):
        outs = []
        for h in range(H):
            sl = slice(h * XH, (h + 1) * XH)
            _, p = _xattn_head(qb[:, sl], knb[:, sl], g)
            outs.append(jnp.dot(p.astype(BF16), vb[:, sl], preferred_element_type=F32))
        return (jnp.concatenate(outs, axis=-1),)

    return _rowwise(fn, [q], [kn, v, gq], [(H * XH, BF16)], ts=ts, name="xattn_fwd")[0]


def _xattn_bwd(q, kn, v, gq, do, *, ts=512):
    def fn(qb, dob, knb, vb, g):
        dqs, dks, dvs = [], [], []
        dg = jnp.zeros((1, XH), F32)
        for h in range(H):
            sl = slice(h * XH, (h + 1) * XH)
            qh, kh, vh, doh = qb[:, sl], knb[:, sl], vb[:, sl], dob[:, sl].astype(BF16)
            qn, p = _xattn_head(qh, kh, g)
            dp = lax.dot_general(doh, vh, (((1,), (1,)), ((), ())), preferred_element_type=F32)
            dvs.append(lax.dot_general(p.astype(BF16), doh, (((0,), (0,)), ((), ())), preferred_element_type=F32))
            ds = (p * (dp - jnp.sum(dp * p, axis=-1, keepdims=True))).astype(BF16)
            dqn = jnp.dot(ds, kh, preferred_element_type=F32)
            dks.append(lax.dot_general(ds, qn.astype(BF16), (((0,), (0,)), ((), ())), preferred_element_type=F32))
            _, vjp_n = jax.vjp(lambda a, b: _rms(a, b) * (XH ** -0.5), qh, g)
            dqh, dgh = vjp_n(dqn)
            dqs.append(dqh)
            dg = dg + dgh
        return (jnp.concatenate(dqs, axis=-1), jnp.concatenate(dks, axis=-1), jnp.concatenate(dvs, axis=-1), dg)

    return _rowwise(fn, [q, do], [kn, v, gq], [(H * XH, BF16)], [kn.shape, v.shape, gq.shape], ts=ts, name="xattn_bwd")


def _cmul(ar, ai, xr, xi):
    return ar * xr - ai * xi, ar * xi + ai * xr


def _scan_in_place(xr_ref, xi_ref, ar, ai, *, reverse):
    s, cw = xr_ref.shape
    c = SCAN_CHUNKS
    tt = s // c
    a_r = jnp.broadcast_to(ar, (c, cw))
    a_i = jnp.broadcast_to(ai, (c, cw))
    zero = jnp.zeros((c, cw), F32)

    def row(step):
        t = (tt - 1 - step) if reverse else step
        return pl.ds(pl.multiple_of(t * c, c), c)

    def local(step, carry):
        sr, si, qr, qi = carry
        r = row(step)
        nr, ni = _cmul(a_r, a_i, sr, si)
        nr, ni = nr + xr_ref[r, :], ni + xi_ref[r, :]
        xr_ref[r, :] = nr
        xi_ref[r, :] = ni
        return (nr, ni) + _cmul(a_r, a_i, qr, qi)

    end_r, end_i, pr, pi = lax.fori_loop(0, tt, local, (zero, zero, jnp.ones((c, cw), F32), zero), unroll=SCAN_UNROLL)

    rows_id = lax.broadcasted_iota(jnp.int32, (c, cw), 0)
    car_r, car_i = zero, zero
    cur_r, cur_i = jnp.zeros((1, cw), F32), jnp.zeros((1, cw), F32)
    order = range(c - 1, -1, -1) if reverse else range(c)
    for kk in order:
        car_r = jnp.where(rows_id == kk, cur_r, car_r)
        car_i = jnp.where(rows_id == kk, cur_i, car_i)
        nr, ni = _cmul(pr[0:1], pi[0:1], cur_r, cur_i)
        cur_r = nr + end_r[kk:kk + 1]
        cur_i = ni + end_i[kk:kk + 1]

    def fix(step, carry):
        qr, qi = _cmul(a_r, a_i, *carry)
        r = row(step)
        dr, di = _cmul(qr, qi, car_r, car_i)
        xr_ref[r, :] += dr
        xi_ref[r, :] += di
        return qr, qi

    lax.fori_loop(0, tt, fix, (jnp.ones((c, cw), F32), zero), unroll=SCAN_UNROLL)


S5_ROWS = 512


def _s5_scan(v, w_r, w_i, ar, ai, *, reverse, tb, readout=None, name):
    s = v.shape[0]
    g = w_r.shape[0]
    nv, ns = SSM_PACK * SSM_GRP, SSM_PACK * SSM_P
    rows = min(S5_ROWS, s)
    dims = (((1,), (1 if tb else 0,)), ((), ()))
    n_w = 2 if readout is None else 4

    def body(v_ref, ar_ref, ai_ref, *refs):
        w = [r[...] for r in refs[:n_w]]
        xr_ref, xi_ref = refs[n_w:n_w + 2]
        for r0 in range(0, s, rows):
            vb = v_ref[r0:r0 + rows, :].astype(BF16)
            xr_ref[r0:r0 + rows, :] = lax.dot_general(vb, w[0], dims, preferred_element_type=F32)
            xi_ref[r0:r0 + rows, :] = lax.dot_general(vb, w[1], dims, preferred_element_type=F32)
        _scan_in_place(xr_ref, xi_ref, ar_ref[...], ai_ref[...], reverse=reverse)
        if readout is not None:
            y_ref = refs[n_w + 2]
            for r0 in range(0, s, rows):
                y_ref[r0:r0 + rows, :] = (
                    jnp.dot(xr_ref[r0:r0 + rows, :].astype(BF16), w[2], preferred_element_type=F32)
                    + jnp.dot(xi_ref[r0:r0 + rows, :].astype(BF16), w[3], preferred_element_type=F32))

    col = lambda j: (0, j)
    w_spec = lambda a: pl.BlockSpec((None,) + a.shape[1:], lambda j: (j, 0, 0))
    weights = [w_r, w_i] + (list(readout) if readout is not None else [])
    out_specs = [pl.BlockSpec((s, ns), col)] * 2 + ([pl.BlockSpec((s, nv), col)] if readout is not None else [])
    out_shape = [jax.ShapeDtypeStruct((s, g * ns), F32)] * 2 + (
        [jax.ShapeDtypeStruct((s, g * nv), F32)] if readout is not None else [])
    return pl.pallas_call(
        body, name=name, grid=(g,),
        in_specs=[pl.BlockSpec((s, nv), col), pl.BlockSpec((1, ns), col), pl.BlockSpec((1, ns), col)] + [w_spec(a) for a in weights],
        out_specs=out_specs, out_shape=out_shape, compiler_params=_params(("parallel",)),
    )(v, ar, ai, *weights)


def _s5_grads(lam_r, lam_i, xr, xi, u, dyc, du_d, b_r, b_i):
    s = u.shape[0]
    g = b_r.shape[0]
    nv, ns, c = SSM_PACK * SSM_GRP, SSM_PACK * SSM_P, SCAN_CHUNKS
    rows = min(S5_ROWS, s)
    slabs = rows // c
    last_slab = s // c - 1
    nt = (((1,), (1,)), ((), ()))
    tn = (((0,), (0,)), ((), ()))

    def body(lr_ref, li_ref, xr_ref, xi_ref, pr_ref, pi_ref, u_ref, dy_ref, dud_ref, br_ref, bi_ref,
             du_ref, dbr_ref, dbi_ref, dcr_ref, dci_ref, dar_ref, dai_ref):
        first = pl.program_id(1) == 0
        l_r, l_i, x_r, x_i = lr_ref[...], li_ref[...], xr_ref[...], xi_ref[...]
        lrb, lib = l_r.astype(BF16), l_i.astype(BF16)
        du_ref[...] = (dud_ref[...] + lax.dot_general(lrb, br_ref[...], nt, preferred_element_type=F32)
                       + lax.dot_general(lib, bi_ref[...], nt, preferred_element_type=F32))
        ub, dyb = u_ref[...].astype(BF16), dy_ref[...].astype(BF16)
        rows_id = lax.broadcasted_iota(jnp.int32, (c, ns), 0)

        def before(p_ref, x):
            p = p_ref[...]
            p = jnp.where(first, jnp.where(rows_id == 0, 0.0, pltpu.roll(p, 1, 0)), p)
            return jnp.concatenate([p, x[:rows - c]], axis=0)

        xp_r, xp_i = before(pr_ref, x_r), before(pi_ref, x_i)
        parts = (lax.dot_general(ub, lrb, tn, preferred_element_type=F32),
                 lax.dot_general(ub, lib, tn, preferred_element_type=F32),
                 lax.dot_general(x_r.astype(BF16), dyb, tn, preferred_element_type=F32),
                 lax.dot_general(x_i.astype(BF16), dyb, tn, preferred_element_type=F32),
                 jnp.sum(l_r * xp_r + l_i * xp_i, axis=0, keepdims=True),
                 jnp.sum(l_i * xp_r - l_r * xp_i, axis=0, keepdims=True))
        accs = (dbr_ref, dbi_ref, dcr_ref, dci_ref, dar_ref, dai_ref)

        @pl.when(first)
        def _():
            for a_ref, val in zip(accs, parts):
                a_ref[...] = val

        @pl.when(jnp.logical_not(first))
        def _():
            for a_ref, val in zip(accs, parts):
                a_ref[...] += val

    state = pl.BlockSpec((rows, ns), lambda j, k: (k, j))
    chan = pl.BlockSpec((rows, nv), lambda j, k: (k, j))
    slab = pl.BlockSpec((c, ns), lambda j, k: (jnp.where(k == 0, last_slab, k * slabs - 1), j))
    per_b = pl.BlockSpec((None, nv, ns), lambda j, k: (j, 0, 0))
    per_c = pl.BlockSpec((None, ns, nv), lambda j, k: (j, 0, 0))
    per_a = pl.BlockSpec((1, ns), lambda j, k: (0, j))
    return pl.pallas_call(
        body, name="s5_grads", grid=(g, s // rows),
        in_specs=[state, state, state, state, slab, slab, chan, chan, chan, per_b, per_b],
        out_specs=[chan, per_b, per_b, per_c, per_c, per_a, per_a],
        out_shape=[jax.ShapeDtypeStruct((s, g * nv), F32), jax.ShapeDtypeStruct((g, nv, ns), F32),
                   jax.ShapeDtypeStruct((g, nv, ns), F32), jax.ShapeDtypeStruct((g, ns, nv), F32),
                   jax.ShapeDtypeStruct((g, ns, nv), F32), jax.ShapeDtypeStruct((1, g * ns), F32),
                   jax.ShapeDtypeStruct((1, g * ns), F32)],
        compiler_params=_params(("parallel", "arbitrary")),
    )(lam_r, lam_i, xr, xi, xr, xi, u, dyc, du_d, b_r, b_i)


def _mesh_place():
    x, y, c = lax.axis_index("x"), lax.axis_index("y"), lax.axis_index("c")
    peers = []
    for k in range(1, N_DEV):
        px, py, pc = x ^ ((k >> 2) & 1), y ^ ((k >> 1) & 1), c ^ (k & 1)
        peers.append(((px, py, pc), 4 * px + 2 * py + pc))
    return 4 * x + 2 * y + c, peers


class _Exchange:
    SAME_CORE_MASKS = (2, 4, 6)

    def __init__(self, arrays, rows, *, gather, name, after=None, two_level=False):
        self.n_arr, self.rows, self.gather, self.name = len(arrays), rows, gather, name
        self.two_level, self.in_flight = two_level, (1 + len(self.SAME_CORE_MASKS) if two_level else N_DEV - 1)
        n_arr = self.n_arr
        if gather:
            assert all(r % BF16_ROWS == 0 for r in rows)
            lands = [lax.empty((N_DEV * r, a.shape[1]), a.dtype) for a, r in zip(arrays, rows)]
        else:
            lands = [lax.empty((N_DEV - 1,) + (tuple(a.shape) if st is None else (n, a.shape[1])), a.dtype)
                     for a, (st, n) in zip(arrays, rows)]
        has_after = after is not None

        def body(*refs):
            ins, zones = refs[:n_arr], refs[n_arr:2 * n_arr]
            sems = refs[2 * n_arr + has_after:4 * n_arr + has_after]
            token = refs[-1]
            me, peers = _mesh_place()
            for i in range(n_arr):
                for k, (pxyz, pid) in enumerate(peers):
                    if two_level and k + 1 not in (1,) + self.SAME_CORE_MASKS:
                        continue
                    if gather:
                        src = ins[i]
                        dst = zones[i].at[pl.ds(pl.multiple_of(me * rows[i], BF16_ROWS), rows[i])]
                    else:
                        stride, n = rows[i]
                        src = ins[i] if stride is None else ins[i].at[pl.ds(pl.multiple_of(pid * stride, BF16_ROWS), n)]
                        dst = zones[i].at[k]
                    pltpu.make_async_remote_copy(
                        src_ref=src, dst_ref=dst, send_sem=sems[2 * i], recv_sem=sems[2 * i + 1],
                        device_id=pxyz, device_id_type=pl.DeviceIdType.MESH).start()
            token[...] = jnp.zeros_like(token)

        hbm = pl.BlockSpec(memory_space=pltpu.HBM)
        sem = pl.BlockSpec(memory_space=pltpu.SEMAPHORE)
        args = [pltpu.with_memory_space_constraint(a, pltpu.HBM) for a in list(arrays) + lands]
        res = pl.pallas_call(
            body, name=name + "_start",
            in_specs=[hbm] * (2 * n_arr) + ([pl.BlockSpec(memory_space=pl.ANY)] if has_after else []),
            out_specs=[sem] * (2 * n_arr) + [hbm] * (2 * n_arr) + [pl.BlockSpec(memory_space=pltpu.VMEM)],
            out_shape=[pltpu.SemaphoreType.DMA(())] * (2 * n_arr) + [pltpu.HBM(a.shape, a.dtype) for a in args]
            + [jax.ShapeDtypeStruct((8, LANES), F32)],
            input_output_aliases={i: 2 * n_arr + i for i in range(2 * n_arr)},
            compiler_params=pltpu.CompilerParams(has_side_effects=pltpu.SideEffectType.DATAFLOW_SIDE_EFFECTING),
        )(*args, *([after] if has_after else []))
        self.sems, self.thru, self.token = res[:2 * n_arr], res[2 * n_arr:4 * n_arr], res[-1]

    def _wait_all(self, zones, sems):
        myself = (lax.axis_index("x"), lax.axis_index("y"), lax.axis_index("c"))
        for i in range(self.n_arr):
            many = zones[i].at[pl.ds(0, self.in_flight * self.rows[i])] if self.gather else zones[i]
            all_of_them = pltpu.make_async_remote_copy(
                src_ref=many, dst_ref=many, send_sem=sems[2 * i], recv_sem=sems[2 * i + 1],
                device_id=myself, device_id_type=pl.DeviceIdType.MESH)
            all_of_them.wait_recv()
            all_of_them.wait_send()

    def forward(self, after):
        n_arr = self.n_arr

        def body(*refs):
            zones, sems = refs[n_arr:2 * n_arr], refs[2 * n_arr:4 * n_arr]
            new_sems = refs[4 * n_arr + 1:6 * n_arr + 1]
            self._wait_all(zones, sems)
            _, peers = _mesh_place()
            sibling, _ = peers[0]
            for i in range(n_arr):
                for mask in self.SAME_CORE_MASKS:
                    _, pid = peers[mask - 1]
                    block = zones[i].at[pl.ds(pl.multiple_of(pid * self.rows[i], BF16_ROWS), self.rows[i])]
                    pltpu.make_async_remote_copy(
                        src_ref=block, dst_ref=block, send_sem=new_sems[2 * i], recv_sem=new_sems[2 * i + 1],
                        device_id=sibling, device_id_type=pl.DeviceIdType.MESH).start()

        hbm = pl.BlockSpec(memory_space=pltpu.HBM)
        sem = pl.BlockSpec(memory_space=pltpu.SEMAPHORE)
        res = pl.pallas_call(
            body, name=self.name + "_forward",
            in_specs=[hbm] * (2 * n_arr) + [sem] * (2 * n_arr) + [pl.BlockSpec(memory_space=pl.ANY)],
            out_specs=[sem] * (2 * n_arr) + [hbm] * (2 * n_arr),
            out_shape=[pltpu.SemaphoreType.DMA(())] * (2 * n_arr) + [pltpu.HBM(a.shape, a.dtype) for a in self.thru],
            input_output_aliases={i: 2 * n_arr + i for i in range(2 * n_arr)},
            compiler_params=pltpu.CompilerParams(has_side_effects=pltpu.SideEffectType.DATAFLOW_SIDE_EFFECTING),
        )(*self.thru, *self.sems, after)
        self.sems, self.thru = res[:2 * n_arr], res[2 * n_arr:]
        self.two_level, self.in_flight = False, len(self.SAME_CORE_MASKS)

    def wait(self, after):
        n_arr = self.n_arr
        if self.two_level:
            self.forward(after)

        def body(*refs):
            self._wait_all(refs[n_arr:2 * n_arr], refs[2 * n_arr:4 * n_arr])

        hbm = pl.BlockSpec(memory_space=pltpu.HBM)
        sem = pl.BlockSpec(memory_space=pltpu.SEMAPHORE)
        res = pl.pallas_call(
            body, name=self.name + "_wait",
            in_specs=[hbm] * (2 * n_arr) + [sem] * (2 * n_arr) + [pl.BlockSpec(memory_space=pl.ANY)],
            out_specs=[hbm] * (2 * n_arr), out_shape=[pltpu.HBM(a.shape, a.dtype) for a in self.thru],
            input_output_aliases={i: i for i in range(2 * n_arr)},
            compiler_params=pltpu.CompilerParams(has_side_effects=pltpu.SideEffectType.DATAFLOW_SIDE_EFFECTING),
        )(*self.thru, *self.sems, after)
        return res[:n_arr], res[n_arr:]


def _my_slot():
    me = 4 * lax.axis_index("x") + 2 * lax.axis_index("y") + lax.axis_index("c")
    return me.astype(jnp.int32).reshape(1)


def _place_own(gathered, blocks, me, *, name):
    n = len(blocks)

    def body(me_ref, *refs):
        for b_ref, o_ref in zip(refs[:n], refs[2 * n:]):
            o_ref[...] = b_ref[...]

    res = pl.pallas_call(
        body, name=name, out_shape=[jax.ShapeDtypeStruct(g.shape, g.dtype) for g in gathered],
        grid_spec=pltpu.PrefetchScalarGridSpec(
            num_scalar_prefetch=1, grid=(1,),
            in_specs=[pl.BlockSpec(b.shape, lambda i, me_ref: (0, 0)) for b in blocks] + [pl.BlockSpec(memory_space=pl.ANY)] * n,
            out_specs=[pl.BlockSpec(b.shape, lambda i, me_ref: (me_ref[0], 0)) for b in blocks]),
        input_output_aliases={1 + n + i: i for i in range(n)}, compiler_params=_params(("arbitrary",)),
    )(me, *blocks, *gathered)
    return list(res)


def _elementwise_tiles(r, c):
    if r % 128 == 0:
        return 128, c
    return r, (256 if c % 256 == 0 else c)


def _adamw_math(g, w, m, v):
    nm = ADAM_B1 * m + (1.0 - ADAM_B1) * g
    nv = ADAM_B2 * v + (1.0 - ADAM_B2) * (g * g)
    m_hat = nm / (1.0 - ADAM_B1 ** ADAM_STEP)
    v_hat = nv / (1.0 - ADAM_B2 ** ADAM_STEP)
    return -ADAM_LR * (m_hat / (jnp.sqrt(v_hat) + ADAM_EPS) + ADAM_WD * w), nm, nv


def _sum_parts(me_ref, own_ref, p_ref, r):
    own = own_ref[...].astype(F32)
    g = None
    for d in range(N_DEV):
        k = jnp.bitwise_xor(me_ref[0], d)
        term = jnp.where(k == 0, own, p_ref[jnp.maximum(k, 1) - 1].astype(F32))
        g = term if g is None else g + term
    return g[0:r, :]


def _sum_adamw(me, sent, stride, parts, r, w=None, m=None, v=None, *, name):
    _, own_rows, cdim = parts.shape
    assert stride is None or stride == own_rows
    tc = 256 if cdim % 256 == 0 else cdim
    update = w is not None

    def body(me_ref, own_ref, p_ref, *refs):
        g = _sum_parts(me_ref, own_ref, p_ref, r)
        if update:
            w_ref, m_ref, v_ref, g_ref, d_ref, nm_ref, nv_ref = refs
            d_ref[...], nm_ref[...], nv_ref[...] = _adamw_math(g, w_ref[...], m_ref[...], v_ref[...])
        else:
            g_ref, = refs
        g_ref[...] = g

    blk = pl.BlockSpec((r, tc), lambda j, me_ref: (0, j))
    own_spec = pl.BlockSpec((own_rows, tc), (lambda j, me_ref: (0, j)) if stride is None else (lambda j, me_ref: (me_ref[0], j)))
    n_out = 4 if update else 1
    res = pl.pallas_call(
        body, name=name, out_shape=[jax.ShapeDtypeStruct((r, cdim), F32)] * n_out,
        grid_spec=pltpu.PrefetchScalarGridSpec(
            num_scalar_prefetch=1, grid=(cdim // tc,),
            in_specs=[own_spec, pl.BlockSpec((N_DEV - 1, own_rows, tc), lambda j, me_ref: (0, 0, j))]
            + ([blk] * 3 if update else []),
            out_specs=[blk] * n_out),
        compiler_params=_params(("parallel",)),
    )(me, sent, parts, *((w, m, v) if update else ()))
    return list(res)


def _adamw(g, w, m, v, *, name):
    r, cdim = w.shape
    tr, tc = _elementwise_tiles(r, cdim)

    def body(g_ref, w_ref, m_ref, v_ref, d_ref, nm_ref, nv_ref):
        d_ref[...], nm_ref[...], nv_ref[...] = _adamw_math(g_ref[...], w_ref[...], m_ref[...], v_ref[...])

    blk = pl.BlockSpec((tr, tc), lambda i, j: (i, j))
    return list(pl.pallas_call(
        body, name=name, grid=(r // tr, cdim // tc), in_specs=[blk] * 4,
        out_specs=[blk] * 3, out_shape=[jax.ShapeDtypeStruct((r, cdim), F32)] * 3,
        compiler_params=_params(("parallel", "parallel")),
    )(g, w, m, v))


SHARD_ROWS_P = {n: (FF_SHARD_P if 'ffn' in n else IN_SHARD_P if n == 'w_in' else None) for n in SHARDED}


def _to_exchange_layout(name, shard):
    t = shard.T if SHARD_AXIS[name] == 1 else shard
    pad = SHARD_ROWS_P[name]
    return t if pad is None else jnp.pad(t, ((0, pad - t.shape[0]), (0, 0)))


def _expand_w_in(wt):
    wt = wt.reshape(N_DEV, IN_SHARD_P, D)[:, :IN_SHARD].reshape(IN_W, D)
    o = Q_RANK + KV_RANK
    kr1, kr2 = wt[o:o + ROPE // 2], wt[o + ROPE // 2:o + ROPE]
    z = jnp.zeros((LANES - ROPE, D), wt.dtype)
    return jnp.concatenate([wt[:o], wt[o + ROPE:], kr1, kr2, z, -kr2, kr1, z], axis=0)


def _expand_w_uq(wt):
    w = wt.reshape(H, QK, Q_RANK)
    z = jnp.zeros((H, LANES - ROPE, Q_RANK), w.dtype)
    q1, q2 = w[:, NOPE:NOPE + ROPE // 2], w[:, NOPE + ROPE // 2:]
    return jnp.concatenate([w[:, :NOPE].reshape(H * NOPE, Q_RANK),
                            jnp.concatenate([q1, q2, z], axis=1).reshape(H * LANES, Q_RANK),
                            jnp.concatenate([-q2, q1, z], axis=1).reshape(H * LANES, Q_RANK)], axis=0)


def _layout_qk_gain(g):
    g = g.reshape(QK)
    g1, g2, z = g[NOPE:NOPE + ROPE // 2], g[NOPE + ROPE // 2:], jnp.zeros((LANES - ROPE,), g.dtype)
    return jnp.stack([g[:NOPE], jnp.concatenate([g1, g2, z]), jnp.concatenate([g2, g1, z])])


def _rep16(a):
    return jnp.repeat(a, SSM_GRP, axis=0)


def _layout_ssm_in(a_re, a_im, log_dt, b_re, b_im):
    b_r = jnp.transpose(b_re, (0, 2, 1)).reshape(SSM_G * SSM_GRP, SSM_P)
    b_i = jnp.transpose(b_im, (0, 2, 1)).reshape(SSM_G * SSM_GRP, SSM_P)
    ldt = jnp.broadcast_to(log_dt.reshape(SSM_G, 1), (SSM_G, SSM_P))
    return _rep16(a_re), _rep16(a_im), _rep16(ldt), b_r, b_i


def _block_diag_b(bb):
    eye = jnp.eye(SSM_PACK, dtype=bb.dtype)
    b5 = bb.reshape(SSM_G // SSM_PACK, SSM_PACK, SSM_GRP, 1, SSM_P) * eye[None, :, None, :, None]
    return b5.reshape(SSM_G // SSM_PACK, SSM_PACK * SSM_GRP, SSM_PACK * SSM_P)


def _block_diag_c(cc):
    eye = jnp.eye(SSM_PACK, dtype=cc.dtype)
    c5 = jnp.transpose(cc, (0, 2, 1)).reshape(SSM_G // SSM_PACK, SSM_PACK, SSM_P, 1, SSM_GRP) * eye[None, :, None, :, None]
    return c5.reshape(SSM_G // SSM_PACK, SSM_PACK * SSM_P, SSM_PACK * SSM_GRP)


def _time_perm(a, inverse=False):
    s, w = a.shape
    c = SCAN_CHUNKS
    if inverse:
        return jnp.transpose(a.reshape(s // c, c, w), (1, 0, 2)).reshape(s, w)
    return jnp.transpose(a.reshape(c, s // c, w), (1, 0, 2)).reshape(s, w)


class _Weights:
    def __init__(self, groups=(), landed=None, me=None):
        self.groups, self.landed, self.me = list(groups), dict(landed or {}), me

    def get(self, name, after):
        if name not in self.landed:
            names, exchange = next(g for g in self.groups if name in g[0])
            blocks, gathered = exchange.wait(after)
            self.landed.update(zip(names, _place_own(gathered, blocks, self.me, name="place_" + names[0])))
        return self.landed[name]

    def __getitem__(self, name):
        return self.landed[name]

    def prefetch(self, name, after):
        for names, exchange in self.groups:
            if name in names and exchange.two_level:
                exchange.forward(after)


def _ffn_gate_up(h, w_gt, w_ut, *, name, tm=512, tn=1408):
    s, k = h.shape
    n = w_gt.shape[0]
    tm, tn = min(tm, s), _tile(n, tn)
    dims = (((1,), (1,)), ((), ()))

    def body(h_ref, wg_ref, wu_ref, g_ref, u_ref, a_ref):
        hb = h_ref[...].astype(BF16)
        gate = lax.dot_general(hb, wg_ref[...], dims, preferred_element_type=F32)
        up = lax.dot_general(hb, wu_ref[...], dims, preferred_element_type=F32)
        g_ref[...] = gate.astype(BF16)
        u_ref[...] = up.astype(BF16)
        a_ref[...] = _f_swiglu(gate, up)

    w_spec = pl.BlockSpec((tn, k), lambda j, i: (j, 0))
    o_spec = pl.BlockSpec((tm, tn), lambda j, i: (i, j))
    return pl.pallas_call(
        body, name=name, grid=(n // tn, s // tm), in_specs=[pl.BlockSpec((tm, k), lambda j, i: (i, 0)), w_spec, w_spec],
        out_specs=[o_spec] * 3, out_shape=[jax.ShapeDtypeStruct((s, n), BF16)] * 3,
        compiler_params=_params(("parallel", "parallel")),
    )(h, w_gt, w_ut)


def _ffn_dgate_dup(dx_out, w_d, gate, up, *, name, tm=512, tn=1408, deps=()):
    s, k = dx_out.shape
    n = w_d.shape[0]
    tm, tn = min(tm, s), _tile(n, tn)
    deps = [d for d in deps if d is not None]

    def body(dx_ref, wd_ref, g_ref, u_ref, *refs):
        dg_ref, du_ref = refs[len(deps):]
        dact = 0.5 * lax.dot_general(dx_ref[...].astype(BF16), wd_ref[...], (((1,), (1,)), ((), ())),
                                     preferred_element_type=F32)
        _, vjp = jax.vjp(_f_swiglu, g_ref[...].astype(F32), u_ref[...].astype(F32))
        dgate, dup = vjp(dact.astype(BF16))
        dg_ref[...] = dgate.astype(BF16)
        du_ref[...] = dup.astype(BF16)

    o_spec = pl.BlockSpec((tm, tn), lambda j, i: (i, j))
    return pl.pallas_call(
        body, name=name, grid=(n // tn, s // tm),
        in_specs=[pl.BlockSpec((tm, k), lambda j, i: (i, 0)), pl.BlockSpec((tn, k), lambda j, i: (j, 0)), o_spec, o_spec]
        + [pl.BlockSpec(d.shape, lambda j, i: (0, 0)) for d in deps],
        out_specs=[o_spec] * 2, out_shape=[jax.ShapeDtypeStruct((s, n), BF16)] * 2,
        compiler_params=_params(("parallel", "parallel")),
    )(dx_out, w_d, gate, up, *deps)


def _ffn_dh(dgate, dup, w_gt, w_ut, *, name, tm=512):
    s, k = dgate.shape
    n = w_gt.shape[1]
    tm = min(tm, s)

    def body(dg_ref, du_ref, wg_ref, wu_ref, o_ref):
        o_ref[...] = (jnp.dot(dg_ref[...], wg_ref[...], preferred_element_type=F32)
                      + jnp.dot(du_ref[...], wu_ref[...], preferred_element_type=F32)).astype(o_ref.dtype)

    a_spec = pl.BlockSpec((tm, k), lambda i: (i, 0))
    w_spec = pl.BlockSpec((k, n), lambda i: (0, 0))
    return pl.pallas_call(
        body, name=name, grid=(s // tm,), in_specs=[a_spec, a_spec, w_spec, w_spec],
        out_specs=pl.BlockSpec((tm, n), lambda i: (i, 0)), out_shape=jax.ShapeDtypeStruct((s, n), BF16),
        compiler_params=_params(("parallel",)),
    )(dgate, dup, w_gt, w_ut)


def _ffn_fwd(x, g, wc, tag, deps=(), prefetch=()):
    h = _rowwise(_f_norm, [x], [g], [(D, BF16)], name=tag + "_norm", deps=deps)[0]
    gate, up, act = _ffn_gate_up(h, wc.get(tag + '_w_gate', h), wc[tag + '_w_up'], name=tag + "_gate_up")
    for later in (tag + '_w_down',) + tuple(prefetch):
        wc.prefetch(later, gate)
    x_out = _mm(act, wc.get(tag + '_w_down', act), res=x, scale=0.5, name=tag + "_down")
    return x_out, (h, gate, up, act)


def _ffn_bwd(x, g, wc, saved, dx_out, tag, send, deps=()):
    h, gate, up, act = saved
    w_gt, w_ut, w_d = (wc.get(tag + n, h) for n in ('_w_gate', '_w_up', '_w_down'))
    d_d = _mm(act, dx_out, ta=True, scale=0.5, out_dtype=GRAD_DTYPE, name=tag + "_dwdown", deps=deps)
    token = send({tag + '_w_down': d_d})
    dgate, dup = _ffn_dgate_dup(dx_out, w_d, gate, up, name=tag + "_dgate_dup", deps=[token])
    d_gt = _mm(dgate, h, ta=True, out_dtype=GRAD_DTYPE, name=tag + "_dwgate")
    token = send({tag + '_w_gate': d_gt})
    d_ut = _mm(dup, h, ta=True, out_dtype=GRAD_DTYPE, name=tag + "_dwup", deps=[token])
    token = send({tag + '_w_up': d_ut})
    dh = _ffn_dh(dgate, dup, w_gt, w_ut, name=tag + "_dh")
    dx, dg = _rowwise_bwd(_f_norm, [x], [g], [dh], row_grads={0: F32}, const_grads=[0], adds={0: dx_out},
                          name=tag + "_norm_bwd", deps=[token])
    return dx, dg


def _local_step(x, mem, cos, sin, target, wc, ws, send, deps=(), send_small=None):
    gs = {}

    x1, sv1 = _ffn_fwd(x, ws['ffn1_norm'], wc, "ffn1", deps=deps, prefetch=('w_in',))

    h2 = _rowwise(_f_norm, [x1], [ws['mix_norm']], [(D, BF16)], name="mix_norm")[0]
    w_in_raw, w_uq_raw = wc.get('w_in', h2), wc.get('mla_w_uq', h2)
    w_in_e = _expand_w_in(w_in_raw)
    w_uq_e = _expand_w_uq(w_uq_raw)
    proj = _mm(h2, w_in_e, tb=True, name="w_in")
    c_q, c_kv = _rowwise(_f_prep1, [proj], [ws['q_norm'], ws['kv_norm']], [(Q_RANK, BF16), (KV_RANK, BF16)], name="mla_prep1")
    qall = _mm(c_q, w_uq_e, tb=True, out_dtype=BF16, name="w_uq")
    kv = _mm(c_kv, wc['mla_w_ukv'], tb=True, out_dtype=BF16, name="w_ukv")
    q, k, v = _prep2_fwd(qall, kv, proj, cos, sin, ws['qk_gq'], ws['qk_gk'])
    o_mla, lse = _attn_fwd(q, k, v)
    wc.prefetch('ffn2_w_gate', lse)

    u = proj[:, Q_RANK + KV_RANK:Q_RANK + KV_RANK + SSM_W]
    u_p = _time_perm(u)
    disc_in = [ws['ssm_lr'], ws['ssm_li'], ws['ssm_ldt'], ws['ssm_br'], ws['ssm_bi']]
    ar16, ai16, bbr, bbi = _rowwise(_f_disc, disc_in, [], [(SSM_P, F32)] * 4, name="s5_disc")
    a_r = ar16[::SSM_GRP].reshape(1, SSM_N)
    a_i = ai16[::SSM_GRP].reshape(1, SSM_N)
    bblk_r, bblk_i = _block_diag_b(bbr).astype(BF16), _block_diag_b(bbi).astype(BF16)
    cblk_r, cblk_i = _block_diag_c(ws['ssm_cr']).astype(BF16), _block_diag_c(-ws['ssm_ci']).astype(BF16)
    xr, xi, yc = _s5_scan(u_p, bblk_r, bblk_i, a_r, a_i, reverse=False, tb=False, readout=(cblk_r, cblk_i),
                          name="s5_scan_fwd")
    g_p = _rowwise(_f_s5_gelu, [yc, u_p], [ws['ssm_d']], [(SSM_W, F32)], name="s5_gelu")[0]
    z_p = _mm(g_p, wc['ssm_w_glu'], name="s5_glu")
    g_t, z_t = _time_perm(g_p, inverse=True), _time_perm(z_p, inverse=True)
    on_consts = [ws['ssm_b_glu'], ws['out_norm_mla'], ws['out_norm_ssm']]
    ycat = _rowwise(_f_outnorm, [o_mla, g_t, z_t], on_consts, [(D, BF16)], name="out_norm")[0]
    x2 = _mm(ycat, wc['w_o'], res=x1, name="w_o")

    hx = _rowwise(_f_norm, [x2], [ws['xattn_norm']], [(D, BF16)], name="xattn_norm")[0]
    xq = _mm(hx, wc['xattn_w_q'], name="xattn_q")
    mn = _rowwise(_f_norm, [mem], [ws['mem_norm']], [(D, BF16)], name="mem_norm")[0]
    kvm = _mm(mn, wc['xattn_w_kv'], name="xattn_kv")
    xkn, xv = _rowwise(_f_memk, [kvm], [ws['xattn_k_norm']], [(H * XH, BF16), (H * XH, BF16)], name="xattn_knorm")
    xo = _xattn_fwd(xq, xkn, xv, ws['xattn_q_norm'])
    x3 = _mm(xo, wc['xattn_w_o'], tb=True, res=x2, name="xattn_o")

    x4, sv2 = _ffn_fwd(x3, ws['ffn2_norm'], wc, "ffn2")

    def f_loss(yb, tb):
        err = yb - tb
        return err * (1.0 / D), jnp.broadcast_to(jnp.sum(jnp.sum(err * err, axis=1, keepdims=True), axis=0, keepdims=True) * (0.5 / D), (1, LANES))

    dx4, loss = _rowwise(f_loss, [x4, target], [], [(D, F32)], [(1, LANES)], name="loss")

    dx3, gs['ffn2_norm'] = _ffn_bwd(x3, ws['ffn2_norm'], wc, sv2, dx4, "ffn2", send)

    dxo = _mm(dx3, wc['xattn_w_o'], out_dtype=BF16, name="xattn_o_dx")
    send({'xattn_w_o': _mm(dx3, xo, ta=True, out_dtype=GRAD_DTYPE, name="xattn_o_dw")})
    dxq, dxkn, dxv, gs['xattn_q_norm'] = _xattn_bwd(xq, xkn, xv, ws['xattn_q_norm'], dxo)
    dkvm, gs['xattn_k_norm'] = _rowwise_bwd(_f_memk, [kvm], [ws['xattn_k_norm']], [dxkn, dxv], row_grads={0: BF16},
                                            const_grads=[0], name="xattn_knorm_bwd")
    send({'xattn_w_kv': _mm(mn, dkvm, ta=True, out_dtype=GRAD_DTYPE, name="xattn_kv_dw")})
    dmn = _mm(dkvm, wc['xattn_w_kv'], tb=True, out_dtype=BF16, name="xattn_kv_dx")
    gs['mem_norm'] = _rowwise_bwd(_f_norm, [mem], [ws['mem_norm']], [dmn], row_grads={}, const_grads=[0], name="mem_norm_bwd")[0]
    token = send({'xattn_w_q': _mm(hx, dxq, ta=True, out_dtype=GRAD_DTYPE, name="xattn_q_dw")})
    dhx = _mm(dxq, wc['xattn_w_q'], tb=True, out_dtype=BF16, name="xattn_q_dx")
    dx2, gs['xattn_norm'] = _rowwise_bwd(_f_norm, [x2], [ws['xattn_norm']], [dhx], row_grads={0: F32}, const_grads=[0],
                                         adds={0: dx3}, name="xattn_norm_bwd", deps=[token])

    dycat = _mm(dx2, wc['w_o'], tb=True, out_dtype=BF16, name="w_o_dx")
    send({'w_o': _mm(ycat, dx2, ta=True, out_dtype=GRAD_DTYPE, name="w_o_dw")})
    do_mla, dg_t, dz_t, gs['ssm_b_glu'], gs['out_norm_mla'], gs['out_norm_ssm'] = _rowwise_bwd(
        _f_outnorm, [o_mla, g_t, z_t], on_consts, [dycat], row_grads={0: F32, 1: F32, 2: BF16}, const_grads=[0, 1, 2],
        name="out_norm_bwd")

    dz_p, dg_p = _time_perm(dz_t), _time_perm(dg_t)
    send({'ssm_w_glu': _mm(g_p, dz_p, ta=True, out_dtype=GRAD_DTYPE, name="s5_glu_dw")})
    dg_p = _mm(dz_p, wc['ssm_w_glu'], tb=True, res=dg_p, name="s5_glu_dx")
    dyc, du_d, gs['ssm_d'] = _rowwise_bwd(_f_s5_gelu, [yc, u_p], [ws['ssm_d']], [dg_p], row_grads={0: BF16, 1: F32},
                                          const_grads=[0], name="s5_gelu_bwd")
    lam_r, lam_i = _s5_scan(dyc, cblk_r, cblk_i, a_r, -a_i, reverse=True, tb=True, name="s5_scan_bwd")
    du_p, d_bblk_r, d_bblk_i, d_cblk_r, d_cblk_i, d_ar, d_ai = _s5_grads(lam_r, lam_i, xr, xi, u_p, dyc, du_d,
                                                                        bblk_r, bblk_i)
    du = _time_perm(du_p, inverse=True)
    gs['ssm_cr'] = jax.linear_transpose(_block_diag_c, ws['ssm_cr'])(d_cblk_r)[0]
    gs['ssm_ci'] = -jax.linear_transpose(_block_diag_c, ws['ssm_ci'])(d_cblk_i)[0]
    d_bbr = jax.linear_transpose(_block_diag_b, bbr)(d_bblk_r)[0]
    d_bbi = jax.linear_transpose(_block_diag_b, bbi)(d_bblk_i)[0]
    d_ar16 = jnp.zeros((SSM_G * SSM_GRP, SSM_P), F32).at[::SSM_GRP].set(d_ar.reshape(SSM_G, SSM_P))
    d_ai16 = jnp.zeros((SSM_G * SSM_GRP, SSM_P), F32).at[::SSM_GRP].set(d_ai.reshape(SSM_G, SSM_P))
    gs['ssm_lr'], gs['ssm_li'], gs['ssm_ldt'], gs['ssm_br'], gs['ssm_bi'] = _rowwise_bwd(
        _f_disc, disc_in, [], [d_ar16, d_ai16, d_bbr, d_bbi], row_grads={i: F32 for i in range(5)}, const_grads=[],
        name="s5_disc_bwd")

    delta, do_b = _rowwise(_f_delta, [do_mla, o_mla], [], [(H * LANES, F32), (H * VD, BF16)], name="mla_delta")
    dq, dk, dv = _attn_bwd(q, k, v, do_b, lse, delta)
    dqall, dkv, dkr, dkrs, gs['qk_gq'], gs['qk_gk'] = _prep2_bwd(qall, kv, proj, cos, sin, ws['qk_gq'], ws['qk_gk'], dq, dk, dv)
    d_w_uq_e = _mm(dqall, c_q, ta=True, name="w_uq_dw")
    send({'mla_w_uq': jax.linear_transpose(_expand_w_uq, jax.ShapeDtypeStruct(w_uq_raw.shape, F32))(d_w_uq_e)[0]})
    dc_q = _mm(dqall, w_uq_e, out_dtype=BF16, name="w_uq_dx")
    send({'mla_w_ukv': _mm(dkv, c_kv, ta=True, out_dtype=GRAD_DTYPE, name="w_ukv_dw")})
    dc_kv = _mm(dkv, wc['mla_w_ukv'], out_dtype=BF16, name="w_ukv_dx")

    def f_prep1_bwd(pb, dcq, dckv, dub, dkrb, dkrsb, gq, gkv):
        _, vjp = jax.vjp(_f_prep1, pb[:, :Q_RANK + KV_RANK], gq, gkv)
        dpa, dgq, dgkv = vjp((dcq.astype(BF16), dckv.astype(BF16)))
        return jnp.concatenate([dpa, dub, dkrb, dkrsb], axis=-1), dgq, dgkv

    dproj, gs['q_norm'], gs['kv_norm'] = _rowwise(
        f_prep1_bwd, [proj, dc_q, dc_kv, du, dkr, dkrs], [ws['q_norm'], ws['kv_norm']], [(IN_WP, BF16)],
        [(1, Q_RANK), (1, KV_RANK)], name="mla_prep1_bwd")
    d_w_in_e = _mm(dproj, h2, ta=True, name="w_in_dw")
    token = send({'w_in': jax.linear_transpose(_expand_w_in, jax.ShapeDtypeStruct(w_in_raw.shape, F32))(d_w_in_e)[0]})
    dh2 = _mm(dproj, w_in_e, out_dtype=BF16, name="w_in_dx")
    dx1, gs['mix_norm'] = _rowwise_bwd(_f_norm, [x1], [ws['mix_norm']], [dh2], row_grads={0: F32}, const_grads=[0],
                                       adds={0: dx2}, name="mix_norm_bwd", deps=[token])

    token = send_small(gs, loss) if send_small is not None else None
    dx0, gs['ffn1_norm'] = _ffn_bwd(x, ws['ffn1_norm'], wc, sv1, dx1, "ffn1", send, deps=[token])
    return loss, dx0, gs


def _prep2_rows(qall, kv, proj, cos, sin):
    return [qall, kv, (proj, KR_BLOCK, LANES), (proj, KR_BLOCK + 1, LANES), cos, sin]


def _prep2_fwd(qall, kv, proj, cos, sin, gq, gk):
    return _rowwise(_f_prep2, _prep2_rows(qall, kv, proj, cos, sin), [gq, gk],
                    [(H * HQ, BF16), (H * HQ, BF16), (H * VD, BF16)], ts=256, name="mla_prep2")


def _prep2_bwd(qall, kv, proj, cos, sin, gq, gk, dq, dk, dv):
    return _rowwise_bwd(_f_prep2, _prep2_rows(qall, kv, proj, cos, sin), [gq, gk], [dq, dk, dv],
                        row_grads={0: BF16, 1: BF16, 2: F32, 3: F32}, const_grads=[0, 1], ts=256, name="mla_prep2_bwd")


def _rope_tables(pos):
    half = ROPE // 2
    inv = ROPE_THETA ** (-jnp.arange(half, dtype=F32) / half)
    ang = pos.astype(F32)[:, None] * inv[None, :]
    z = jnp.zeros((pos.shape[0], LANES - ROPE), F32)
    cos, sin = jnp.cos(ang), jnp.sin(ang)
    return jnp.concatenate([cos, cos, z], axis=-1), jnp.concatenate([sin, sin, z], axis=-1)


def _small_layout(p):
    lr, li, ldt, br, bi = _layout_ssm_in(p['ssm_a_re'], p['ssm_a_im'], p['ssm_log_dt'], p['ssm_b_re'], p['ssm_b_im'])
    return {
        'ffn1_norm': p['ffn1_norm'].reshape(1, D), 'mix_norm': p['mix_norm'].reshape(1, D),
        'q_norm': p['mla_q_norm'].reshape(1, Q_RANK), 'kv_norm': p['mla_kv_norm'].reshape(1, KV_RANK),
        'qk_gq': _layout_qk_gain(p['mla_qk_norm_q']), 'qk_gk': _layout_qk_gain(p['mla_qk_norm_k']),
        'ssm_lr': lr, 'ssm_li': li, 'ssm_ldt': ldt, 'ssm_br': br, 'ssm_bi': bi,
        'ssm_cr': p['ssm_c_re'], 'ssm_ci': p['ssm_c_im'], 'ssm_d': p['ssm_d'].reshape(1, SSM_W),
        'ssm_b_glu': p['ssm_b_glu'].reshape(1, SSM_W),
        'out_norm_mla': p['out_norm_mla'].reshape(1, SSM_W), 'out_norm_ssm': p['out_norm_ssm'].reshape(1, SSM_W),
        'xattn_norm': p['xattn_norm'].reshape(1, D), 'mem_norm': p['mem_norm'].reshape(1, D),
        'xattn_q_norm': p['xattn_q_norm'].reshape(1, XH), 'xattn_k_norm': p['xattn_k_norm'].reshape(1, XH),
        'ffn2_norm': p['ffn2_norm'].reshape(1, D),
    }


def _pack(arrs, rows):
    flat = jnp.concatenate([a.reshape(-1) for a in arrs])
    return jnp.pad(flat, (0, rows * D - flat.shape[0])).reshape(rows, D)


def _unpack(flat, shapes):
    flat = flat.reshape(-1)
    out, off = [], 0
    for sh in shapes:
        n = int(np.prod(sh))
        out.append(flat[off:off + n].reshape(sh))
        off += n
    return out


def kernel(x, mem, positions, ffn1_norm, ffn1_w_gate, ffn1_w_up, ffn1_w_down, mix_norm, w_in, mla_q_norm, mla_w_uq, mla_kv_norm, mla_w_ukv, mla_qk_norm_q, mla_qk_norm_k, ssm_a_re, ssm_a_im, ssm_log_dt, ssm_b_re, ssm_b_im, ssm_c_re, ssm_c_im, ssm_d, ssm_w_glu, ssm_b_glu, out_norm_mla, out_norm_ssm, w_o, xattn_norm, mem_norm, xattn_w_q, xattn_w_kv, xattn_q_norm, xattn_k_norm, xattn_w_o, ffn2_norm, ffn2_w_gate, ffn2_w_up, ffn2_w_down, loss_target, m_ffn1_norm, m_ffn1_w_gate, m_ffn1_w_up, m_ffn1_w_down, m_mix_norm, m_w_in, m_mla_q_norm, m_mla_w_uq, m_mla_kv_norm, m_mla_w_ukv, m_mla_qk_norm_q, m_mla_qk_norm_k, m_ssm_a_re, m_ssm_a_im, m_ssm_log_dt, m_ssm_b_re, m_ssm_b_im, m_ssm_c_re, m_ssm_c_im, m_ssm_d, m_ssm_w_glu, m_ssm_b_glu, m_out_norm_mla, m_out_norm_ssm, m_w_o, m_xattn_norm, m_mem_norm, m_xattn_w_q, m_xattn_w_kv, m_xattn_q_norm, m_xattn_k_norm, m_xattn_w_o, m_ffn2_norm, m_ffn2_w_gate, m_ffn2_w_up, m_ffn2_w_down, v_ffn1_norm, v_ffn1_w_gate, v_ffn1_w_up, v_ffn1_w_down, v_mix_norm, v_w_in, v_mla_q_norm, v_mla_w_uq, v_mla_kv_norm, v_mla_w_ukv, v_mla_qk_norm_q, v_mla_qk_norm_k, v_ssm_a_re, v_ssm_a_im, v_ssm_log_dt, v_ssm_b_re, v_ssm_b_im, v_ssm_c_re, v_ssm_c_im, v_ssm_d, v_ssm_w_glu, v_ssm_b_glu, v_out_norm_mla, v_out_norm_ssm, v_w_o, v_xattn_norm, v_mem_norm, v_xattn_w_q, v_xattn_w_kv, v_xattn_q_norm, v_xattn_k_norm, v_xattn_w_o, v_ffn2_norm, v_ffn2_w_gate, v_ffn2_w_up, v_ffn2_w_down):
    args = dict(locals())
    w = {n: args[n] for n in WEIGHTS}
    mom = {n: args['m_' + n] for n in WEIGHTS}
    var = {n: args['v_' + n] for n in WEIGHTS}
    return _step(x, mem, positions, loss_target, w, mom, var)


GATHER_GROUPS = [('ffn1_gu', ['ffn1_w_gate', 'ffn1_w_up']), ('ffn1_down', ['ffn1_w_down']),
                 ('mix', ['w_in', 'mla_w_uq', 'mla_w_ukv', 'ssm_w_glu', 'w_o', 'xattn_w_q', 'xattn_w_kv', 'xattn_w_o']),
                 ('ffn2', ['ffn2_w_gate', 'ffn2_w_up', 'ffn2_w_down'])]
SCATTER_GROUPS = [('ffn2_down', ['ffn2_w_down']), ('ffn2_gate', ['ffn2_w_gate']), ('ffn2_up', ['ffn2_w_up']),
                  ('xattn', ['xattn_w_o', 'xattn_w_kv', 'xattn_w_q']),
                  ('mix', ['w_o', 'ssm_w_glu', 'mla_w_uq', 'mla_w_ukv', 'w_in']),
                  ('ffn1_down', ['ffn1_w_down']), ('ffn1_gate', ['ffn1_w_gate']), ('ffn1_up', ['ffn1_w_up'])]


def _step(x, mem, positions, loss_target, w, mom, var):
    blocks = {n: _to_exchange_layout(n, w[n][0]).astype(BF16) for n in SHARDED}
    gathers, token = [], None
    for tag, names in GATHER_GROUPS:
        ex = _Exchange([blocks[n] for n in names], [blocks[n].shape[0] for n in names], gather=True,
                       name="gather_" + tag, after=token, two_level=True)
        gathers.append((names, ex))
        token = ex.token
    me = _my_slot()
    wc = _Weights(gathers, me=me)

    rows = {n: (blocks[n].shape[0], blocks[n].shape[0]) for n in SHARDED}
    ready, scatters = {}, []

    def send(grads):
        ready.update({n: g.astype(GRAD_DTYPE) for n, g in grads.items()})
        for tag, names in SCATTER_GROUPS:
            if all(n in ready for n in names) and not any(t == tag for t, _, _ in scatters):
                ex = _Exchange([ready[n] for n in names], [rows[n] for n in names], gather=False, name="scatter_" + tag)
                scatters.append((tag, names, ex))
                return ex.token
        return None

    small = {n: w[n][0] for n in SMALL}
    small_shapes = [small[n].shape for n in SMALL]
    n_small = sum(int(np.prod(sh)) for sh in small_shapes) + 1
    rows_small = -(-n_small // (8 * D)) * 8
    small_sent = []

    def send_small(gs, loss):
        known = dict(gs, ffn1_norm=jnp.zeros((1, D), F32))
        g_small = jax.linear_transpose(_small_layout, {n: jax.ShapeDtypeStruct(small[n].shape, F32) for n in SMALL})(known)[0]
        pack = _pack([g_small[n] for n in SMALL] + [loss[0, :1]], rows_small)
        small_sent.append(_Exchange([pack], [(None, rows_small)], gather=False, name="scatter_small"))
        return small_sent[0].token

    ws = _small_layout(small)
    cos, sin = _rope_tables(positions[0])
    loss, dx, gs = _local_step(x[0], mem[0], cos, sin, loss_target[0], wc, ws, send, deps=[token], send_small=send_small)
    pad8 = lambda a: jnp.pad(a.reshape(1, D), ((0, 7), (0, 0)))
    last_ex = _Exchange([pad8(gs['ffn1_norm'])], [(None, 8)], gather=False, name="scatter_last")

    out, after = {}, dx
    for _, names, ex in scatters:
        for n, sent, p in zip(names, *ex.wait(after)):
            r = w[n][0].shape[SHARD_AXIS[n]]
            if SHARD_AXIS[n] == 0:
                out[n] = _sum_adamw(me, sent, rows[n][0], p, r, w[n][0], mom[n][0], var[n][0], name="adamw_" + n)
            else:
                g = _sum_adamw(me, sent, rows[n][0], p, r, name="sum_" + n)[0].T
                out[n] = [g] + _adamw(g, w[n][0], mom[n][0], var[n][0], name="adamw_" + n)
        after = out[names[-1]][1]
    state = [_pack([t[n][0] for n in SMALL], rows_small) for t in (w, mom, var)]
    sent, p = small_sent[0].wait(after)
    small_out = _sum_adamw(me, sent[0], None, p[0], rows_small, *state, name="adamw_small")
    loss_total = small_out[0].reshape(-1)[n_small - 1]
    for n, vals in zip(SMALL, zip(*[_unpack(flat, small_shapes) for flat in small_out])):
        out[n] = vals
    sent, p = last_ex.wait(small_out[1])
    last_out = _sum_adamw(me, sent[0], None, p[0], 8, *[pad8(t['ffn1_norm'][0]) for t in (w, mom, var)], name="adamw_last")
    out['ffn1_norm'] = [o[0] for o in last_out]
    outs = [out[n][i][None] for i in range(4) for n in WEIGHTS]
    return (loss_total, dx[None], *outs)
```

```python
import math

import jax
import jax.numpy as jnp
import numpy as np
from jax import lax
from jax.experimental import pallas as pl
from jax.experimental.pallas import tpu as pltpu

F32 = jnp.float32
BF16 = jnp.bfloat16

N_DEV = 8
D = 1024
D_FF = 2752
D_FFP = 2816
MEM_LEN = 256
H = 4
Q_RANK, KV_RANK, NOPE, ROPE, VD = 384, 256, 128, 64, 128
QK = NOPE + ROPE
HQ = 2 * 128
SSM_W, SSM_G, SSM_GRP, SSM_P = 512, 32, 16, 64
SSM_N = SSM_G * SSM_P
SSM_PACK = 8
IN_W = 1216
IN_WP = 1408
XH = 128
EPS = 1e-6
LN2 = math.log(2.0)
ROPE_THETA = 10000.0
SCAN_CHUNKS = 8
SCAN_UNROLL = 8
ADAM_LR, ADAM_B1, ADAM_B2, ADAM_EPS, ADAM_WD, ADAM_STEP = 0.001, 0.9, 0.999, 1e-08, 0.01, 10

VMEM_LIMIT = 56 * 1024 * 1024
ACC_BYTES = 6 * 1024 * 1024
LANES = 128
BF16_ROWS = 16
GRAD_DTYPE = BF16
FF_SHARD = D_FF // N_DEV
FF_SHARD_P = 352
IN_SHARD = IN_W // N_DEV
IN_SHARD_P = 160

WEIGHTS = ['ffn1_norm', 'ffn1_w_gate', 'ffn1_w_up', 'ffn1_w_down', 'mix_norm', 'w_in', 'mla_q_norm', 'mla_w_uq',
           'mla_kv_norm', 'mla_w_ukv', 'mla_qk_norm_q', 'mla_qk_norm_k', 'ssm_a_re', 'ssm_a_im', 'ssm_log_dt',
           'ssm_b_re', 'ssm_b_im', 'ssm_c_re', 'ssm_c_im', 'ssm_d', 'ssm_w_glu', 'ssm_b_glu', 'out_norm_mla',
           'out_norm_ssm', 'w_o', 'xattn_norm', 'mem_norm', 'xattn_w_q', 'xattn_w_kv', 'xattn_q_norm',
           'xattn_k_norm', 'xattn_w_o', 'ffn2_norm', 'ffn2_w_gate', 'ffn2_w_up', 'ffn2_w_down']
SHARD_AXIS = {'ffn1_w_gate': 1, 'ffn1_w_up': 1, 'ffn1_w_down': 0, 'w_in': 1, 'mla_w_uq': 1, 'mla_w_ukv': 1,
              'ssm_w_glu': 0, 'w_o': 0, 'xattn_w_q': 0, 'xattn_w_kv': 0, 'xattn_w_o': 1,
              'ffn2_w_gate': 1, 'ffn2_w_up': 1, 'ffn2_w_down': 0}
SHARDED = [n for n in WEIGHTS if n in SHARD_AXIS]
SMALL = [n for n in WEIGHTS if n not in SHARD_AXIS]


def _params(sem=None):
    return pltpu.CompilerParams(dimension_semantics=sem, vmem_limit_bytes=VMEM_LIMIT)


def _tile(n, cap):
    if n <= cap:
        return n
    best = n
    for t in range(LANES, cap + 1, LANES):
        if n % t == 0:
            best = t
    return best


def _mm(a, b, *, ta=False, tb=False, out_dtype=F32, res=None, scale=1.0, name, tm_cap=1024, tn_cap=1408, tk_cap=2816,
        deps=()):
    m, k = (a.shape[1], a.shape[0]) if ta else a.shape
    k2, n = (b.shape[1], b.shape[0]) if tb else b.shape
    assert k == k2, (a.shape, b.shape, ta, tb)
    if ta:
        tk_cap = min(tk_cap, 1024)
        tm_cap = 1408
    tm, tn, tk = _tile(m, tm_cap), _tile(n, tn_cap), _tile(k, tk_cap)
    if tm * tn * 4 > ACC_BYTES:
        tn = _tile(n, max(LANES, ACC_BYTES // (4 * tm) // LANES * LANES))
    nk = k // tk
    dims = (((0 if ta else 1,), (1 if tb else 0,)), ((), ()))
    has_res = res is not None

    deps = [d for d in deps if d is not None]

    def body(*refs):
        a_ref, b_ref = refs[:2]
        r_ref = refs[2] if has_res else None
        o_ref, acc_ref = refs[-2:]
        kk = pl.program_id(2)

        @pl.when(kk == 0)
        def _():
            acc_ref[...] = jnp.zeros_like(acc_ref)

        acc_ref[...] += lax.dot_general(a_ref[...].astype(BF16), b_ref[...].astype(BF16), dims,
                                        preferred_element_type=F32)

        @pl.when(kk == nk - 1)
        def _():
            out = acc_ref[...]
            if scale != 1.0:
                out = out * scale
            if has_res:
                out = out + r_ref[...].astype(F32)
            o_ref[...] = out.astype(o_ref.dtype)

    a_spec = pl.BlockSpec((tk, tm), lambda i, j, kk: (kk, i)) if ta else pl.BlockSpec((tm, tk), lambda i, j, kk: (i, kk))
    b_spec = pl.BlockSpec((tn, tk), lambda i, j, kk: (j, kk)) if tb else pl.BlockSpec((tk, tn), lambda i, j, kk: (kk, j))
    o_spec = pl.BlockSpec((tm, tn), lambda i, j, kk: (i, j))
    in_specs = [a_spec, b_spec] + ([o_spec] if has_res else []) + [pl.BlockSpec(d.shape, lambda i, j, kk: (0, 0)) for d in deps]
    args = (a, b) + ((res,) if has_res else ()) + tuple(deps)
    return pl.pallas_call(
        body, name=name, grid=(m // tm, n // tn, nk), in_specs=in_specs, out_specs=o_spec,
        out_shape=jax.ShapeDtypeStruct((m, n), out_dtype), scratch_shapes=[pltpu.VMEM((tm, tn), F32)],
        compiler_params=_params(("parallel", "parallel", "arbitrary")),
    )(*args)


def _rowwise(fn, rows, consts, outs, accs=(), *, ts=1024, name, deps=()):
    rows = [r if isinstance(r, tuple) else (r, 0, r.shape[1]) for r in rows]
    s = rows[0][0].shape[0]
    ts = min(ts, s)
    assert s % ts == 0
    n_rows, n_consts, n_outs = len(rows), len(consts), len(outs)
    deps = [d for d in deps if d is not None]
    consts = list(consts) + deps

    def body(*refs):
        ins = [r[...] for r in refs[:n_rows + n_consts]]
        res = fn(*ins)
        res = tuple(res) if isinstance(res, (tuple, list)) else (res,)
        out_refs = refs[n_rows + len(consts):]
        for o_ref, val in zip(out_refs[:n_outs], res[:n_outs]):
            o_ref[...] = val.astype(o_ref.dtype)
        if accs:
            first = pl.program_id(0) == 0

            @pl.when(first)
            def _():
                for a_ref, val in zip(out_refs[n_outs:], res[n_outs:]):
                    a_ref[...] = val.astype(F32)

            @pl.when(jnp.logical_not(first))
            def _():
                for a_ref, val in zip(out_refs[n_outs:], res[n_outs:]):
                    a_ref[...] += val.astype(F32)

    in_specs = [pl.BlockSpec((ts, width), lambda i, cb=cb: (i, cb)) for _, cb, width in rows]
    in_specs += [pl.BlockSpec(c.shape, lambda i: (0, 0)) for c in consts]
    out_specs = [pl.BlockSpec((ts, w), lambda i: (i, 0)) for w, _ in outs]
    out_specs += [pl.BlockSpec(tuple(sh), lambda i: (0, 0)) for sh in accs]
    out_shape = [jax.ShapeDtypeStruct((s, w), dt) for w, dt in outs]
    out_shape += [jax.ShapeDtypeStruct(tuple(sh), F32) for sh in accs]
    res = pl.pallas_call(
        body, name=name, grid=(s // ts,), in_specs=in_specs, out_specs=out_specs, out_shape=out_shape,
        compiler_params=_params(("arbitrary",)),
    )(*[a for a, _, _ in rows], *consts)
    return res


def _rowwise_bwd(f, rows, consts, cts, *, row_grads, const_grads, adds=None, ts=512, name, deps=()):
    adds = adds or {}
    n_rows, n_consts, n_cts = len(rows), len(consts), len(cts)
    add_keys = sorted(adds)
    rg = sorted(row_grads)
    cg = sorted(const_grads)

    def fn(*args):
        r = args[:n_rows]
        c = args[n_rows:n_rows + n_consts]
        ct = args[n_rows + n_consts:n_rows + n_consts + n_cts]
        extra = args[n_rows + n_consts + n_cts:]
        outs, vjp = jax.vjp(f, *r, *c)
        outs = tuple(outs) if isinstance(outs, (tuple, list)) else (outs,)
        cot = tuple(g.astype(o.dtype) for g, o in zip(ct, outs))
        grads = vjp(cot if len(cot) > 1 else cot[0])
        res = []
        for i in rg:
            g = grads[i].astype(F32)
            if i in adds:
                g = g + extra[add_keys.index(i)].astype(F32)
            res.append(g)
        for i in cg:
            res.append(grads[n_rows + i])
        return tuple(res)

    rows_all = list(rows) + list(cts) + [adds[i] for i in add_keys]
    def fn2(*args):
        nr = len(rows_all)
        rr, cc = args[:nr], args[nr:]
        return fn(*rr[:n_rows], *cc, *rr[n_rows:])

    outs = [(rows[i][2] if isinstance(rows[i], tuple) else rows[i].shape[1], row_grads[i]) for i in rg]
    accs = [consts[i].shape for i in cg]
    return _rowwise(fn2, rows_all, list(consts), outs, accs, ts=ts, name=name, deps=deps)


def _rms(x, g):
    xf = x.astype(F32)
    return xf * lax.rsqrt(jnp.mean(xf * xf, axis=-1, keepdims=True) + EPS) * g.astype(F32)


def _sigmoid(x):
    return 1.0 / (1.0 + jnp.exp(-x))


def _f_norm(x, g):
    return _rms(x, g).astype(BF16)


def _f_swiglu(gate, up):
    gate, up = gate.astype(F32), up.astype(F32)
    return (gate * _sigmoid(gate) * up).astype(BF16)


def _f_prep1(proj, gq, gkv):
    return _rms(proj[:, :Q_RANK], gq).astype(BF16), _rms(proj[:, Q_RANK:Q_RANK + KV_RANK], gkv).astype(BF16)


KR_BLOCK = (Q_RANK + KV_RANK + SSM_W) // LANES


def _f_prep2(qall, kv, kr, krs, cos, sin, gq, gk):
    kr, krs = kr.astype(F32), krs.astype(F32)
    k_rot = kr * gk[1:2] * cos + krs * gk[2:3] * sin
    k_ss = jnp.sum(kr * kr, axis=-1, keepdims=True)
    q_scale = QK ** -0.5 / LN2
    qs, ks, vs = [], [], []
    for h in range(H):
        qn = qall[:, h * LANES:(h + 1) * LANES].astype(F32)
        qr = qall[:, (H + h) * LANES:(H + h + 1) * LANES].astype(F32)
        qrs = qall[:, (2 * H + h) * LANES:(2 * H + h + 1) * LANES].astype(F32)
        rstd = lax.rsqrt((jnp.sum(qn * qn, axis=-1, keepdims=True) + jnp.sum(qr * qr, axis=-1, keepdims=True)) / QK + EPS)
        rstd = rstd * q_scale
        qs += [qn * gq[0:1] * rstd, (qr * gq[1:2] * cos + qrs * gq[2:3] * sin) * rstd]
        kn = kv[:, 2 * h * LANES:(2 * h + 1) * LANES].astype(F32)
        rstd_k = lax.rsqrt((jnp.sum(kn * kn, axis=-1, keepdims=True) + k_ss) / QK + EPS)
        ks += [kn * gk[0:1] * rstd_k, k_rot * rstd_k]
        vs.append(kv[:, (2 * h + 1) * LANES:(2 * h + 2) * LANES])
    return (jnp.concatenate(qs, axis=-1).astype(BF16), jnp.concatenate(ks, axis=-1).astype(BF16),
            jnp.concatenate(vs, axis=-1).astype(BF16))


def _gelu(x):
    return 0.5 * x * (1.0 + jnp.tanh(math.sqrt(2.0 / math.pi) * (x + 0.044715 * (x * x * x))))


def _f_s5_gelu(yc, u, d):
    return _gelu(yc.astype(F32) + d * u.astype(F32))


def _f_outnorm(o_mla, g, z, b_glu, g_om, g_os):
    y_ssm = g * _sigmoid(z + b_glu)
    return jnp.concatenate([_rms(o_mla, g_om), _rms(y_ssm, g_os)], axis=-1).astype(BF16)


def _f_memk(kvm, gk):
    ks = [_rms(kvm[:, h * XH:(h + 1) * XH], gk) for h in range(H)]
    return jnp.concatenate(ks, axis=-1).astype(BF16), kvm[:, H * XH:].astype(BF16)


def _f_disc(lr, li, log_dt, br, bi):
    dt = jnp.exp(log_dt)
    decay = jnp.exp(lr * dt)
    ar = decay * jnp.cos(li * dt)
    ai = decay * jnp.sin(li * dt)
    den = lr * lr + li * li
    nr = ar - 1.0
    coef_r = (nr * lr + ai * li) / den
    coef_i = (ai * lr - nr * li) / den
    return ar, ai, coef_r * br - coef_i * bi, coef_r * bi + coef_i * br


def _causal_mask(i, j, tq, tk):
    qpos = i * tq + lax.broadcasted_iota(jnp.int32, (tq, tk), 0)
    kpos = j * tk + lax.broadcasted_iota(jnp.int32, (tq, tk), 1)
    return qpos >= kpos


def _attn_fwd(q, k, v, *, t=512):
    s = q.shape[0]
    t = min(t, s)
    nb = s // t

    def body(q_ref, k_ref, v_ref, o_ref, lse_ref, m_sc, l_sc, acc_sc):
        i, j = pl.program_id(1), pl.program_id(2)

        @pl.when(j == 0)
        def _():
            m_sc[...] = jnp.full_like(m_sc, -jnp.inf)
            l_sc[...] = jnp.zeros_like(l_sc)
            acc_sc[...] = jnp.zeros_like(acc_sc)

        def block(diagonal):
            sc = lax.dot_general(q_ref[...], k_ref[...], (((1,), (1,)), ((), ())), preferred_element_type=F32)
            if diagonal:
                sc = jnp.where(_causal_mask(i, j, t, t), sc, -jnp.inf)
            m_old = m_sc[...]
            m_new = jnp.maximum(m_old, jnp.max(sc, axis=-1, keepdims=True))
            p = jnp.exp2(sc - m_new)
            alpha = jnp.exp2(m_old - m_new)
            l_sc[...] = alpha * l_sc[...] + jnp.sum(p, axis=-1, keepdims=True)
            acc_sc[...] = alpha * acc_sc[...] + jnp.dot(p.astype(BF16), v_ref[...], preferred_element_type=F32)
            m_sc[...] = m_new

        pl.when(j < i)(lambda: block(False))

        @pl.when(j == i)
        def _():
            block(True)
            o_ref[...] = acc_sc[...] / l_sc[...]
            lse_ref[...] = jnp.broadcast_to(m_sc[...] + jnp.log2(l_sc[...]), lse_ref.shape)

    kv_map = lambda h, i, j: (jnp.minimum(j, i), h)
    return pl.pallas_call(
        body, name="mla_attn_fwd", grid=(H, nb, nb),
        in_specs=[pl.BlockSpec((t, HQ), lambda h, i, j: (i, h)), pl.BlockSpec((t, HQ), kv_map),
                  pl.BlockSpec((t, VD), kv_map)],
        out_specs=[pl.BlockSpec((t, VD), lambda h, i, j: (i, h)), pl.BlockSpec((t, LANES), lambda h, i, j: (i, h))],
        out_shape=[jax.ShapeDtypeStruct((s, H * VD), F32), jax.ShapeDtypeStruct((s, H * LANES), F32)],
        scratch_shapes=[pltpu.VMEM((t, 1), F32), pltpu.VMEM((t, 1), F32), pltpu.VMEM((t, VD), F32)],
        compiler_params=_params(("parallel", "parallel", "arbitrary")),
    )(q, k, v)


def _attn_probs(q_ref, k_ref, v_ref, do_ref, lse_ref, dl_ref, i, j, t, diagonal):
    sc = lax.dot_general(q_ref[...], k_ref[...], (((1,), (1,)), ((), ())), preferred_element_type=F32)
    p = jnp.exp2(sc - jnp.tile(lse_ref[...], (1, t // LANES)))
    if diagonal:
        p = jnp.where(_causal_mask(i, j, t, t), p, 0.0)
    dp = lax.dot_general(do_ref[...], v_ref[...], (((1,), (1,)), ((), ())), preferred_element_type=F32)
    ds = p * (dp - jnp.tile(dl_ref[...], (1, t // LANES)))
    return p, ds


def _attn_bwd(q, k, v, do, lse, delta, *, t=512):
    s = q.shape[0]
    t = min(t, s)
    nb = s // t

    def body(q_ref, k_ref, v_ref, do_ref, lse_ref, dl_ref, dq_ref, dk_ref, dv_ref, dk_sc, dv_sc):
        j, i = pl.program_id(1), pl.program_id(2)

        @pl.when(jnp.logical_and(i == 0, j == 0))
        def _():
            dq_ref[...] = jnp.zeros_like(dq_ref)

        @pl.when(i == 0)
        def _():
            dk_sc[...] = jnp.zeros_like(dk_sc)
            dv_sc[...] = jnp.zeros_like(dv_sc)

        def block(diagonal):
            p, ds = _attn_probs(q_ref, k_ref, v_ref, do_ref, lse_ref, dl_ref, i, j, t, diagonal)
            dsb = ds.astype(BF16)
            dv_sc[...] += lax.dot_general(p.astype(BF16), do_ref[...], (((0,), (0,)), ((), ())), preferred_element_type=F32)
            dk_sc[...] += lax.dot_general(dsb, q_ref[...], (((0,), (0,)), ((), ())), preferred_element_type=F32)
            rows = pl.ds(pl.multiple_of(i * t, t), t)
            dq_ref[rows, :] += jnp.dot(dsb, k_ref[...], preferred_element_type=F32)

        pl.when(i > j)(lambda: block(False))
        pl.when(i == j)(lambda: block(True))

        @pl.when(i == nb - 1)
        def _():
            dk_ref[...] = (dk_sc[...] * LN2).astype(dk_ref.dtype)
            dv_ref[...] = dv_sc[...].astype(dv_ref.dtype)

        @pl.when(jnp.logical_and(i == nb - 1, j == nb - 1))
        def _():
            dq_ref[...] = dq_ref[...] * LN2

    q_map = lambda h, j, i: (jnp.maximum(i, j), h)
    kv_map = lambda h, j, i: (j, h)
    dq, dk, dv = pl.pallas_call(
        body, name="mla_attn_bwd", grid=(H, nb, nb),
        in_specs=[pl.BlockSpec((t, HQ), q_map), pl.BlockSpec((t, HQ), kv_map), pl.BlockSpec((t, VD), kv_map),
                  pl.BlockSpec((t, VD), q_map), pl.BlockSpec((t, LANES), q_map), pl.BlockSpec((t, LANES), q_map)],
        out_specs=[pl.BlockSpec((s, HQ), lambda h, j, i: (0, h)), pl.BlockSpec((t, HQ), kv_map), pl.BlockSpec((t, VD), kv_map)],
        out_shape=[jax.ShapeDtypeStruct((s, H * HQ), F32), jax.ShapeDtypeStruct((s, H * HQ), BF16),
                   jax.ShapeDtypeStruct((s, H * VD), BF16)],
        scratch_shapes=[pltpu.VMEM((t, HQ), F32), pltpu.VMEM((t, VD), F32)],
        compiler_params=_params(("parallel", "arbitrary", "arbitrary")),
    )(q, k, v, do, lse, delta)
    return dq, dk, dv


def _f_delta(do, o):
    prod = do.astype(F32) * o.astype(F32)
    parts = [jnp.broadcast_to(jnp.sum(prod[:, h * VD:(h + 1) * VD], axis=-1, keepdims=True), (do.shape[0], LANES))
             for h in range(H)]
    return jnp.concatenate(parts, axis=-1), do.astype(BF16)


def _xattn_head(qh, kh, gq):
    qn = _rms(qh, gq) * (XH ** -0.5)
    sc = lax.dot_general(qn.astype(BF16), kh, (((1,), (1,)), ((), ())), preferred_element_type=F32)
    sc = sc - jnp.max(sc, axis=-1, keepdims=True)
    e = jnp.exp(sc)
    return qn, e / jnp.sum(e, axis=-1, keepdims=True)


def _xattn_fwd(q, kn, v, gq, *, ts=512):
    def fn(qb, knb, vb, g):
        outs = []
        for h in range(H):
            sl = slice(h * XH, (h + 1) * XH)
            _, p = _xattn_head(qb[:, sl], knb[:, sl], g)
            outs.append(jnp.dot(p.astype(BF16), vb[:, sl], preferred_element_type=F32))
        return (jnp.concatenate(outs, axis=-1),)

    return _rowwise(fn, [q], [kn, v, gq], [(H * XH, BF16)], ts=ts, name="xattn_fwd")[0]


def _xattn_bwd(q, kn, v, gq, do, *, ts=512):
    def fn(qb, dob, knb, vb, g):
        dqs, dks, dvs = [], [], []
        dg = jnp.zeros((1, XH), F32)
        for h in range(H):
            sl = slice(h * XH, (h + 1) * XH)
            qh, kh, vh, doh = qb[:, sl], knb[:, sl], vb[:, sl], dob[:, sl].astype(BF16)
            qn, p = _xattn_head(qh, kh, g)
            dp = lax.dot_general(doh, vh, (((1,), (1,)), ((), ())), preferred_element_type=F32)
            dvs.append(lax.dot_general(p.astype(BF16), doh, (((0,), (0,)), ((), ())), preferred_element_type=F32))
            ds = (p * (dp - jnp.sum(dp * p, axis=-1, keepdims=True))).astype(BF16)
            dqn = jnp.dot(ds, kh, preferred_element_type=F32)
            dks.append(lax.dot_general(ds, qn.astype(BF16), (((0,), (0,)), ((), ())), preferred_element_type=F32))
            _, vjp_n = jax.vjp(lambda a, b: _rms(a, b) * (XH ** -0.5), qh, g)
            dqh, dgh = vjp_n(dqn)
            dqs.append(dqh)
            dg = dg + dgh
        return (jnp.concatenate(dqs, axis=-1), jnp.concatenate(dks, axis=-1), jnp.concatenate(dvs, axis=-1), dg)

    return _rowwise(fn, [q, do], [kn, v, gq], [(H * XH, BF16)], [kn.shape, v.shape, gq.shape], ts=ts, name="xattn_bwd")


def _cmul(ar, ai, xr, xi):
    return ar * xr - ai * xi, ar * xi + ai * xr


def _scan_in_place(xr_ref, xi_ref, ar, ai, *, reverse):
    s, cw = xr_ref.shape
    c = SCAN_CHUNKS
    tt = s // c
    a_r = jnp.broadcast_to(ar, (c, cw))
    a_i = jnp.broadcast_to(ai, (c, cw))
    zero = jnp.zeros((c, cw), F32)

    def row(step):
        t = (tt - 1 - step) if reverse else step
        return pl.ds(pl.multiple_of(t * c, c), c)

    def local(step, carry):
        sr, si, qr, qi = carry
        r = row(step)
        nr, ni = _cmul(a_r, a_i, sr, si)
        nr, ni = nr + xr_ref[r, :], ni + xi_ref[r, :]
        xr_ref[r, :] = nr
        xi_ref[r, :] = ni
        return (nr, ni) + _cmul(a_r, a_i, qr, qi)

    end_r, end_i, pr, pi = lax.fori_loop(0, tt, local, (zero, zero, jnp.ones((c, cw), F32), zero), unroll=SCAN_UNROLL)

    rows_id = lax.broadcasted_iota(jnp.int32, (c, cw), 0)
    car_r, car_i = zero, zero
    cur_r, cur_i = jnp.zeros((1, cw), F32), jnp.zeros((1, cw), F32)
    order = range(c - 1, -1, -1) if reverse else range(c)
    for kk in order:
        car_r = jnp.where(rows_id == kk, cur_r, car_r)
        car_i = jnp.where(rows_id == kk, cur_i, car_i)
        nr, ni = _cmul(pr[0:1], pi[0:1], cur_r, cur_i)
        cur_r = nr + end_r[kk:kk + 1]
        cur_i = ni + end_i[kk:kk + 1]

    def fix(step, carry):
        qr, qi = _cmul(a_r, a_i, *carry)
        r = row(step)
        dr, di = _cmul(qr, qi, car_r, car_i)
        xr_ref[r, :] += dr
        xi_ref[r, :] += di
        return qr, qi

    lax.fori_loop(0, tt, fix, (jnp.ones((c, cw), F32), zero), unroll=SCAN_UNROLL)


S5_ROWS = 512


def _s5_scan(v, w_r, w_i, ar, ai, *, reverse, tb, readout=None, name):
    s = v.shape[0]
    g = w_r.shape[0]
    nv, ns = SSM_PACK * SSM_GRP, SSM_PACK * SSM_P
    rows = min(S5_ROWS, s)
    dims = (((1,), (1 if tb else 0,)), ((), ()))
    n_w = 2 if readout is None else 4

    def body(v_ref, ar_ref, ai_ref, *refs):
        w = [r[...] for r in refs[:n_w]]
        xr_ref, xi_ref = refs[n_w:n_w + 2]
        for r0 in range(0, s, rows):
            vb = v_ref[r0:r0 + rows, :].astype(BF16)
            xr_ref[r0:r0 + rows, :] = lax.dot_general(vb, w[0], dims, preferred_element_type=F32)
            xi_ref[r0:r0 + rows, :] = lax.dot_general(vb, w[1], dims, preferred_element_type=F32)
        _scan_in_place(xr_ref, xi_ref, ar_ref[...], ai_ref[...], reverse=reverse)
        if readout is not None:
            y_ref = refs[n_w + 2]
            for r0 in range(0, s, rows):
                y_ref[r0:r0 + rows, :] = (
                    jnp.dot(xr_ref[r0:r0 + rows, :].astype(BF16), w[2], preferred_element_type=F32)
                    + jnp.dot(xi_ref[r0:r0 + rows, :].astype(BF16), w[3], preferred_element_type=F32))

    col = lambda j: (0, j)
    w_spec = lambda a: pl.BlockSpec((None,) + a.shape[1:], lambda j: (j, 0, 0))
    weights = [w_r, w_i] + (list(readout) if readout is not None else [])
    out_specs = [pl.BlockSpec((s, ns), col)] * 2 + ([pl.BlockSpec((s, nv), col)] if readout is not None else [])
    out_shape = [jax.ShapeDtypeStruct((s, g * ns), F32)] * 2 + (
        [jax.ShapeDtypeStruct((s, g * nv), F32)] if readout is not None else [])
    return pl.pallas_call(
        body, name=name, grid=(g,),
        in_specs=[pl.BlockSpec((s, nv), col), pl.BlockSpec((1, ns), col), pl.BlockSpec((1, ns), col)] + [w_spec(a) for a in weights],
        out_specs=out_specs, out_shape=out_shape, compiler_params=_params(("parallel",)),
    )(v, ar, ai, *weights)


def _s5_grads(lam_r, lam_i, xr, xi, u, dyc, du_d, b_r, b_i):
    s = u.shape[0]
    g = b_r.shape[0]
    nv, ns, c = SSM_PACK * SSM_GRP, SSM_PACK * SSM_P, SCAN_CHUNKS
    rows = min(S5_ROWS, s)
    slabs = rows // c
    last_slab = s // c - 1
    nt = (((1,), (1,)), ((), ()))
    tn = (((0,), (0,)), ((), ()))

    def body(lr_ref, li_ref, xr_ref, xi_ref, pr_ref, pi_ref, u_ref, dy_ref, dud_ref, br_ref, bi_ref,
             du_ref, dbr_ref, dbi_ref, dcr_ref, dci_ref, dar_ref, dai_ref):
        first = pl.program_id(1) == 0
        l_r, l_i, x_r, x_i = lr_ref[...], li_ref[...], xr_ref[...], xi_ref[...]
        lrb, lib = l_r.astype(BF16), l_i.astype(BF16)
        du_ref[...] = (dud_ref[...] + lax.dot_general(lrb, br_ref[...], nt, preferred_element_type=F32)
                       + lax.dot_general(lib, bi_ref[...], nt, preferred_element_type=F32))
        ub, dyb = u_ref[...].astype(BF16), dy_ref[...].astype(BF16)
        rows_id = lax.broadcasted_iota(jnp.int32, (c, ns), 0)

        def before(p_ref, x):
            p = p_ref[...]
            p = jnp.where(first, jnp.where(rows_id == 0, 0.0, pltpu.roll(p, 1, 0)), p)
            return jnp.concatenate([p, x[:rows - c]], axis=0)

        xp_r, xp_i = before(pr_ref, x_r), before(pi_ref, x_i)
        parts = (lax.dot_general(ub, lrb, tn, preferred_element_type=F32),
                 lax.dot_general(ub, lib, tn, preferred_element_type=F32),
                 lax.dot_general(x_r.astype(BF16), dyb, tn, preferred_element_type=F32),
                 lax.dot_general(x_i.astype(BF16), dyb, tn, preferred_element_type=F32),
                 jnp.sum(l_r * xp_r + l_i * xp_i, axis=0, keepdims=True),
                 jnp.sum(l_i * xp_r - l_r * xp_i, axis=0, keepdims=True))
        accs = (dbr_ref, dbi_ref, dcr_ref, dci_ref, dar_ref, dai_ref)

        @pl.when(first)
        def _():
            for a_ref, val in zip(accs, parts):
                a_ref[...] = val

        @pl.when(jnp.logical_not(first))
        def _():
            for a_ref, val in zip(accs, parts):
                a_ref[...] += val

    state = pl.BlockSpec((rows, ns), lambda j, k: (k, j))
    chan = pl.BlockSpec((rows, nv), lambda j, k: (k, j))
    slab = pl.BlockSpec((c, ns), lambda j, k: (jnp.where(k == 0, last_slab, k * slabs - 1), j))
    per_b = pl.BlockSpec((None, nv, ns), lambda j, k: (j, 0, 0))
    per_c = pl.BlockSpec((None, ns, nv), lambda j, k: (j, 0, 0))
    per_a = pl.BlockSpec((1, ns), lambda j, k: (0, j))
    return pl.pallas_call(
        body, name="s5_grads", grid=(g, s // rows),
        in_specs=[state, state, state, state, slab, slab, chan, chan, chan, per_b, per_b],
        out_specs=[chan, per_b, per_b, per_c, per_c, per_a, per_a],
        out_shape=[jax.ShapeDtypeStruct((s, g * nv), F32), jax.ShapeDtypeStruct((g, nv, ns), F32),
                   jax.ShapeDtypeStruct((g, nv, ns), F32), jax.ShapeDtypeStruct((g, ns, nv), F32),
                   jax.ShapeDtypeStruct((g, ns, nv), F32), jax.ShapeDtypeStruct((1, g * ns), F32),
                   jax.ShapeDtypeStruct((1, g * ns), F32)],
        compiler_params=_params(("parallel", "arbitrary")),
    )(lam_r, lam_i, xr, xi, xr, xi, u, dyc, du_d, b_r, b_i)


def _mesh_place():
    x, y, c = lax.axis_index("x"), lax.axis_index("y"), lax.axis_index("c")
    peers = []
    for k in range(1, N_DEV):
        px, py, pc = x ^ ((k >> 2) & 1), y ^ ((k >> 1) & 1), c ^ (k & 1)
        peers.append(((px, py, pc), 4 * px + 2 * py + pc))
    return 4 * x + 2 * y + c, peers


class _Exchange:
    SAME_CORE_MASKS = (2, 4, 6)

    def __init__(self, arrays, rows, *, gather, name, after=None, two_level=False):
        self.n_arr, self.rows, self.gather, self.name = len(arrays), rows, gather, name
        self.two_level, self.in_flight = two_level, (1 + len(self.SAME_CORE_MASKS) if two_level else N_DEV - 1)
        n_arr = self.n_arr
        if gather:
            assert all(r % BF16_ROWS == 0 for r in rows)
            lands = [lax.empty((N_DEV * r, a.shape[1]), a.dtype) for a, r in zip(arrays, rows)]
        else:
            lands = [lax.empty((N_DEV - 1,) + (tuple(a.shape) if st is None else (n, a.shape[1])), a.dtype)
                     for a, (st, n) in zip(arrays, rows)]
        has_after = after is not None

        def body(*refs):
            ins, zones = refs[:n_arr], refs[n_arr:2 * n_arr]
            sems = refs[2 * n_arr + has_after:4 * n_arr + has_after]
            token = refs[-1]
            me, peers = _mesh_place()
            for i in range(n_arr):
                for k, (pxyz, pid) in enumerate(peers):
                    if two_level and k + 1 not in (1,) + self.SAME_CORE_MASKS:
                        continue
                    if gather:
                        src = ins[i]
                        dst = zones[i].at[pl.ds(pl.multiple_of(me * rows[i], BF16_ROWS), rows[i])]
                    else:
                        stride, n = rows[i]
                        src = ins[i] if stride is None else ins[i].at[pl.ds(pl.multiple_of(pid * stride, BF16_ROWS), n)]
                        dst = zones[i].at[k]
                    pltpu.make_async_remote_copy(
                        src_ref=src, dst_ref=dst, send_sem=sems[2 * i], recv_sem=sems[2 * i + 1],
                        device_id=pxyz, device_id_type=pl.DeviceIdType.MESH).start()
            token[...] = jnp.zeros_like(token)

        hbm = pl.BlockSpec(memory_space=pltpu.HBM)
        sem = pl.BlockSpec(memory_space=pltpu.SEMAPHORE)
        args = [pltpu.with_memory_space_constraint(a, pltpu.HBM) for a in list(arrays) + lands]
        res = pl.pallas_call(
            body, name=name + "_start",
            in_specs=[hbm] * (2 * n_arr) + ([pl.BlockSpec(memory_space=pl.ANY)] if has_after else []),
            out_specs=[sem] * (2 * n_arr) + [hbm] * (2 * n_arr) + [pl.BlockSpec(memory_space=pltpu.VMEM)],
            out_shape=[pltpu.SemaphoreType.DMA(())] * (2 * n_arr) + [pltpu.HBM(a.shape, a.dtype) for a in args]
            + [jax.ShapeDtypeStruct((8, LANES), F32)],
            input_output_aliases={i: 2 * n_arr + i for i in range(2 * n_arr)},
            compiler_params=pltpu.CompilerParams(has_side_effects=pltpu.SideEffectType.DATAFLOW_SIDE_EFFECTING),
        )(*args, *([after] if has_after else []))
        self.sems, self.thru, self.token = res[:2 * n_arr], res[2 * n_arr:4 * n_arr], res[-1]

    def _wait_all(self, zones, sems):
        myself = (lax.axis_index("x"), lax.axis_index("y"), lax.axis_index("c"))
        for i in range(self.n_arr):
            many = zones[i].at[pl.ds(0, self.in_flight * self.rows[i])] if self.gather else zones[i]
            all_of_them = pltpu.make_async_remote_copy(
                src_ref=many, dst_ref=many, send_sem=sems[2 * i], recv_sem=sems[2 * i + 1],
                device_id=myself, device_id_type=pl.DeviceIdType.MESH)
            all_of_them.wait_recv()
            all_of_them.wait_send()

    def forward(self, after):
        n_arr = self.n_arr

        def body(*refs):
            zones, sems = refs[n_arr:2 * n_arr], refs[2 * n_arr:4 * n_arr]
            new_sems = refs[4 * n_arr + 1:6 * n_arr + 1]
            self._wait_all(zones, sems)
            _, peers = _mesh_place()
            sibling, _ = peers[0]
            for i in range(n_arr):
                for mask in self.SAME_CORE_MASKS:
                    _, pid = peers[mask - 1]
                    block = zones[i].at[pl.ds(pl.multiple_of(pid * self.rows[i], BF16_ROWS), self.rows[i])]
                    pltpu.make_async_remote_copy(
                        src_ref=block, dst_ref=block, send_sem=new_sems[2 * i], recv_sem=new_sems[2 * i + 1],
                        device_id=sibling, device_id_type=pl.DeviceIdType.MESH).start()

        hbm = pl.BlockSpec(memory_space=pltpu.HBM)
        sem = pl.BlockSpec(memory_space=pltpu.SEMAPHORE)
        res = pl.pallas_call(
            body, name=self.name + "_forward",
            in_specs=[hbm] * (2 * n_arr) + [sem] * (2 * n_arr) + [pl.BlockSpec(memory_space=pl.ANY)],
            out_specs=[sem] * (2 * n_arr) + [hbm] * (2 * n_arr),
            out_shape=[pltpu.SemaphoreType.DMA(())] * (2 * n_arr) + [pltpu.HBM(a.shape, a.dtype) for a in self.thru],
            input_output_aliases={i: 2 * n_arr + i for i in range(2 * n_arr)},
            compiler_params=pltpu.CompilerParams(has_side_effects=pltpu.SideEffectType.DATAFLOW_SIDE_EFFECTING),
        )(*self.thru, *self.sems, after)
        self.sems, self.thru = res[:2 * n_arr], res[2 * n_arr:]
        self.two_level, self.in_flight = False, len(self.SAME_CORE_MASKS)

    def wait(self, after):
        n_arr = self.n_arr
        if self.two_level:
            self.forward(after)

        def body(*refs):
            self._wait_all(refs[n_arr:2 * n_arr], refs[2 * n_arr:4 * n_arr])

        hbm = pl.BlockSpec(memory_space=pltpu.HBM)
        sem = pl.BlockSpec(memory_space=pltpu.SEMAPHORE)
        res = pl.pallas_call(
            body, name=self.name + "_wait",
            in_specs=[hbm] * (2 * n_arr) + [sem] * (2 * n_arr) + [pl.BlockSpec(memory_space=pl.ANY)],
            out_specs=[hbm] * (2 * n_arr), out_shape=[pltpu.HBM(a.shape, a.dtype) for a in self.thru],
            input_output_aliases={i: i for i in range(2 * n_arr)},
            compiler_params=pltpu.CompilerParams(has_side_effects=pltpu.SideEffectType.DATAFLOW_SIDE_EFFECTING),
        )(*self.thru, *self.sems, after)
        return res[:n_arr], res[n_arr:]


def _my_slot():
    me = 4 * lax.axis_index("x") + 2 * lax.axis_index("y") + lax.axis_index("c")
    return me.astype(jnp.int32).reshape(1)


def _place_own(gathered, blocks, me, *, name):
    n = len(blocks)

    def body(me_ref, *refs):
        for b_ref, o_ref in zip(refs[:n], refs[2 * n:]):
            o_ref[...] = b_ref[...]

    res = pl.pallas_call(
        body, name=name, out_shape=[jax.ShapeDtypeStruct(g.shape, g.dtype) for g in gathered],
        grid_spec=pltpu.PrefetchScalarGridSpec(
            num_scalar_prefetch=1, grid=(1,),
            in_specs=[pl.BlockSpec(b.shape, lambda i, me_ref: (0, 0)) for b in blocks] + [pl.BlockSpec(memory_space=pl.ANY)] * n,
            out_specs=[pl.BlockSpec(b.shape, lambda i, me_ref: (me_ref[0], 0)) for b in blocks]),
        input_output_aliases={1 + n + i: i for i in range(n)}, compiler_params=_params(("arbitrary",)),
    )(me, *blocks, *gathered)
    return list(res)


def _elementwise_tiles(r, c):
    if r % 128 == 0:
        return 128, c
    return r, (256 if c % 256 == 0 else c)


def _adamw_math(g, w, m, v):
    nm = ADAM_B1 * m + (1.0 - ADAM_B1) * g
    nv = ADAM_B2 * v + (1.0 - ADAM_B2) * (g * g)
    m_hat = nm / (1.0 - ADAM_B1 ** ADAM_STEP)
    v_hat = nv / (1.0 - ADAM_B2 ** ADAM_STEP)
    return -ADAM_LR * (m_hat / (jnp.sqrt(v_hat) + ADAM_EPS) + ADAM_WD * w), nm, nv


def _sum_parts(me_ref, own_ref, p_ref, r):
    own = own_ref[...].astype(F32)
    g = None
    for d in range(N_DEV):
        k = jnp.bitwise_xor(me_ref[0], d)
        term = jnp.where(k == 0, own, p_ref[jnp.maximum(k, 1) - 1].astype(F32))
        g = term if g is None else g + term
    return g[0:r, :]


def _sum_adamw(me, sent, stride, parts, r, w=None, m=None, v=None, *, name):
    _, own_rows, cdim = parts.shape
    assert stride is None or stride == own_rows
    tc = 256 if cdim % 256 == 0 else cdim
    update = w is not None

    def body(me_ref, own_ref, p_ref, *refs):
        g = _sum_parts(me_ref, own_ref, p_ref, r)
        if update:
            w_ref, m_ref, v_ref, g_ref, d_ref, nm_ref, nv_ref = refs
            d_ref[...], nm_ref[...], nv_ref[...] = _adamw_math(g, w_ref[...], m_ref[...], v_ref[...])
        else:
            g_ref, = refs
        g_ref[...] = g

    blk = pl.BlockSpec((r, tc), lambda j, me_ref: (0, j))
    own_spec = pl.BlockSpec((own_rows, tc), (lambda j, me_ref: (0, j)) if stride is None else (lambda j, me_ref: (me_ref[0], j)))
    n_out = 4 if update else 1
    res = pl.pallas_call(
        body, name=name, out_shape=[jax.ShapeDtypeStruct((r, cdim), F32)] * n_out,
        grid_spec=pltpu.PrefetchScalarGridSpec(
            num_scalar_prefetch=1, grid=(cdim // tc,),
            in_specs=[own_spec, pl.BlockSpec((N_DEV - 1, own_rows, tc), lambda j, me_ref: (0, 0, j))]
            + ([blk] * 3 if update else []),
            out_specs=[blk] * n_out),
        compiler_params=_params(("parallel",)),
    )(me, sent, parts, *((w, m, v) if update else ()))
    return list(res)


def _adamw(g, w, m, v, *, name):
    r, cdim = w.shape
    tr, tc = _elementwise_tiles(r, cdim)

    def body(g_ref, w_ref, m_ref, v_ref, d_ref, nm_ref, nv_ref):
        d_ref[...], nm_ref[...], nv_ref[...] = _adamw_math(g_ref[...], w_ref[...], m_ref[...], v_ref[...])

    blk = pl.BlockSpec((tr, tc), lambda i, j: (i, j))
    return list(pl.pallas_call(
        body, name=name, grid=(r // tr, cdim // tc), in_specs=[blk] * 4,
        out_specs=[blk] * 3, out_shape=[jax.ShapeDtypeStruct((r, cdim), F32)] * 3,
        compiler_params=_params(("parallel", "parallel")),
    )(g, w, m, v))


SHARD_ROWS_P = {n: (FF_SHARD_P if 'ffn' in n else IN_SHARD_P if n == 'w_in' else None) for n in SHARDED}


def _to_exchange_layout(name, shard):
    t = shard.T if SHARD_AXIS[name] == 1 else shard
    pad = SHARD_ROWS_P[name]
    return t if pad is None else jnp.pad(t, ((0, pad - t.shape[0]), (0, 0)))


def _expand_w_in(wt):
    wt = wt.reshape(N_DEV, IN_SHARD_P, D)[:, :IN_SHARD].reshape(IN_W, D)
    o = Q_RANK + KV_RANK
    kr1, kr2 = wt[o:o + ROPE // 2], wt[o + ROPE // 2:o + ROPE]
    z = jnp.zeros((LANES - ROPE, D), wt.dtype)
    return jnp.concatenate([wt[:o], wt[o + ROPE:], kr1, kr2, z, -kr2, kr1, z], axis=0)


def _expand_w_uq(wt):
    w = wt.reshape(H, QK, Q_RANK)
    z = jnp.zeros((H, LANES - ROPE, Q_RANK), w.dtype)
    q1, q2 = w[:, NOPE:NOPE + ROPE // 2], w[:, NOPE + ROPE // 2:]
    return jnp.concatenate([w[:, :NOPE].reshape(H * NOPE, Q_RANK),
                            jnp.concatenate([q1, q2, z], axis=1).reshape(H * LANES, Q_RANK),
                            jnp.concatenate([-q2, q1, z], axis=1).reshape(H * LANES, Q_RANK)], axis=0)


def _layout_qk_gain(g):
    g = g.reshape(QK)
    g1, g2, z = g[NOPE:NOPE + ROPE // 2], g[NOPE + ROPE // 2:], jnp.zeros((LANES - ROPE,), g.dtype)
    return jnp.stack([g[:NOPE], jnp.concatenate([g1, g2, z]), jnp.concatenate([g2, g1, z])])


def _rep16(a):
    return jnp.repeat(a, SSM_GRP, axis=0)


def _layout_ssm_in(a_re, a_im, log_dt, b_re, b_im):
    b_r = jnp.transpose(b_re, (0, 2, 1)).reshape(SSM_G * SSM_GRP, SSM_P)
    b_i = jnp.transpose(b_im, (0, 2, 1)).reshape(SSM_G * SSM_GRP, SSM_P)
    ldt = jnp.broadcast_to(log_dt.reshape(SSM_G, 1), (SSM_G, SSM_P))
    return _rep16(a_re), _rep16(a_im), _rep16(ldt), b_r, b_i


def _block_diag_b(bb):
    eye = jnp.eye(SSM_PACK, dtype=bb.dtype)
    b5 = bb.reshape(SSM_G // SSM_PACK, SSM_PACK, SSM_GRP, 1, SSM_P) * eye[None, :, None, :, None]
    return b5.reshape(SSM_G // SSM_PACK, SSM_PACK * SSM_GRP, SSM_PACK * SSM_P)


def _block_diag_c(cc):
    eye = jnp.eye(SSM_PACK, dtype=cc.dtype)
    c5 = jnp.transpose(cc, (0, 2, 1)).reshape(SSM_G // SSM_PACK, SSM_PACK, SSM_P, 1, SSM_GRP) * eye[None, :, None, :, None]
    return c5.reshape(SSM_G // SSM_PACK, SSM_PACK * SSM_P, SSM_PACK * SSM_GRP)


def _time_perm(a, inverse=False):
    s, w = a.shape
    c = SCAN_CHUNKS
    if inverse:
        return jnp.transpose(a.reshape(s // c, c, w), (1, 0, 2)).reshape(s, w)
    return jnp.transpose(a.reshape(c, s // c, w), (1, 0, 2)).reshape(s, w)


class _Weights:
    def __init__(self, groups=(), landed=None, me=None):
        self.groups, self.landed, self.me = list(groups), dict(landed or {}), me

    def get(self, name, after):
        if name not in self.landed:
            names, exchange = next(g for g in self.groups if name in g[0])
            blocks, gathered = exchange.wait(after)
            self.landed.update(zip(names, _place_own(gathered, blocks, self.me, name="place_" + names[0])))
        return self.landed[name]

    def __getitem__(self, name):
        return self.landed[name]

    def prefetch(self, name, after):
        for names, exchange in self.groups:
            if name in names and exchange.two_level:
                exchange.forward(after)


def _ffn_gate_up(h, w_gt, w_ut, *, name, tm=512, tn=1408):
    s, k = h.shape
    n = w_gt.shape[0]
    tm, tn = min(tm, s), _tile(n, tn)
    dims = (((1,), (1,)), ((), ()))

    def body(h_ref, wg_ref, wu_ref, g_ref, u_ref, a_ref):
        hb = h_ref[...].astype(BF16)
        gate = lax.dot_general(hb, wg_ref[...], dims, preferred_element_type=F32)
        up = lax.dot_general(hb, wu_ref[...], dims, preferred_element_type=F32)
        g_ref[...] = gate.astype(BF16)
        u_ref[...] = up.astype(BF16)
        a_ref[...] = _f_swiglu(gate, up)

    w_spec = pl.BlockSpec((tn, k), lambda j, i: (j, 0))
    o_spec = pl.BlockSpec((tm, tn), lambda j, i: (i, j))
    return pl.pallas_call(
        body, name=name, grid=(n // tn, s // tm), in_specs=[pl.BlockSpec((tm, k), lambda j, i: (i, 0)), w_spec, w_spec],
        out_specs=[o_spec] * 3, out_shape=[jax.ShapeDtypeStruct((s, n), BF16)] * 3,
        compiler_params=_params(("parallel", "parallel")),
    )(h, w_gt, w_ut)


def _ffn_dgate_dup(dx_out, w_d, gate, up, *, name, tm=512, tn=1408, deps=()):
    s, k = dx_out.shape
    n = w_d.shape[0]
    tm, tn = min(tm, s), _tile(n, tn)
    deps = [d for d in deps if d is not None]

    def body(dx_ref, wd_ref, g_ref, u_ref, *refs):
        dg_ref, du_ref = refs[len(deps):]
        dact = 0.5 * lax.dot_general(dx_ref[...].astype(BF16), wd_ref[...], (((1,), (1,)), ((), ())),
                                     preferred_element_type=F32)
        _, vjp = jax.vjp(_f_swiglu, g_ref[...].astype(F32), u_ref[...].astype(F32))
        dgate, dup = vjp(dact.astype(BF16))
        dg_ref[...] = dgate.astype(BF16)
        du_ref[...] = dup.astype(BF16)

    o_spec = pl.BlockSpec((tm, tn), lambda j, i: (i, j))
    return pl.pallas_call(
        body, name=name, grid=(n // tn, s // tm),
        in_specs=[pl.BlockSpec((tm, k), lambda j, i: (i, 0)), pl.BlockSpec((tn, k), lambda j, i: (j, 0)), o_spec, o_spec]
        + [pl.BlockSpec(d.shape, lambda j, i: (0, 0)) for d in deps],
        out_specs=[o_spec] * 2, out_shape=[jax.ShapeDtypeStruct((s, n), BF16)] * 2,
        compiler_params=_params(("parallel", "parallel")),
    )(dx_out, w_d, gate, up, *deps)


def _ffn_dh(dgate, dup, w_gt, w_ut, *, name, tm=512):
    s, k = dgate.shape
    n = w_gt.shape[1]
    tm = min(tm, s)

    def body(dg_ref, du_ref, wg_ref, wu_ref, o_ref):
        o_ref[...] = (jnp.dot(dg_ref[...], wg_ref[...], preferred_element_type=F32)
                      + jnp.dot(du_ref[...], wu_ref[...], preferred_element_type=F32)).astype(o_ref.dtype)

    a_spec = pl.BlockSpec((tm, k), lambda i: (i, 0))
    w_spec = pl.BlockSpec((k, n), lambda i: (0, 0))
    return pl.pallas_call(
        body, name=name, grid=(s // tm,), in_specs=[a_spec, a_spec, w_spec, w_spec],
        out_specs=pl.BlockSpec((tm, n), lambda i: (i, 0)), out_shape=jax.ShapeDtypeStruct((s, n), BF16),
        compiler_params=_params(("parallel",)),
    )(dgate, dup, w_gt, w_ut)


def _ffn_fwd(x, g, wc, tag, deps=(), prefetch=()):
    h = _rowwise(_f_norm, [x], [g], [(D, BF16)], name=tag + "_norm", deps=deps)[0]
    gate, up, act = _ffn_gate_up(h, wc.get(tag + '_w_gate', h), wc[tag + '_w_up'], name=tag + "_gate_up")
    for later in (tag + '_w_down',) + tuple(prefetch):
        wc.prefetch(later, gate)
    x_out = _mm(act, wc.get(tag + '_w_down', act), res=x, scale=0.5, name=tag + "_down")
    return x_out, (h, gate, up, act)


def _ffn_bwd(x, g, wc, saved, dx_out, tag, send, deps=()):
    h, gate, up, act = saved
    w_gt, w_ut, w_d = (wc.get(tag + n, h) for n in ('_w_gate', '_w_up', '_w_down'))
    d_d = _mm(act, dx_out, ta=True, scale=0.5, out_dtype=GRAD_DTYPE, name=tag + "_dwdown", deps=deps)
    token = send({tag + '_w_down': d_d})
    dgate, dup = _ffn_dgate_dup(dx_out, w_d, gate, up, name=tag + "_dgate_dup", deps=[token])
    d_gt = _mm(dgate, h, ta=True, out_dtype=GRAD_DTYPE, name=tag + "_dwgate")
    token = send({tag + '_w_gate': d_gt})
    d_ut = _mm(dup, h, ta=True, out_dtype=GRAD_DTYPE, name=tag + "_dwup", deps=[token])
    token = send({tag + '_w_up': d_ut})
    dh = _ffn_dh(dgate, dup, w_gt, w_ut, name=tag + "_dh")
    dx, dg = _rowwise_bwd(_f_norm, [x], [g], [dh], row_grads={0: F32}, const_grads=[0], adds={0: dx_out},
                          name=tag + "_norm_bwd", deps=[token])
    return dx, dg


def _local_step(x, mem, cos, sin, target, wc, ws, send, deps=(), send_small=None):
    gs = {}

    x1, sv1 = _ffn_fwd(x, ws['ffn1_norm'], wc, "ffn1", deps=deps, prefetch=('w_in',))

    h2 = _rowwise(_f_norm, [x1], [ws['mix_norm']], [(D, BF16)], name="mix_norm")[0]
    w_in_raw, w_uq_raw = wc.get('w_in', h2), wc.get('mla_w_uq', h2)
    w_in_e = _expand_w_in(w_in_raw)
    w_uq_e = _expand_w_uq(w_uq_raw)
    proj = _mm(h2, w_in_e, tb=True, name="w_in")
    c_q, c_kv = _rowwise(_f_prep1, [proj], [ws['q_norm'], ws['kv_norm']], [(Q_RANK, BF16), (KV_RANK, BF16)], name="mla_prep1")
    qall = _mm(c_q, w_uq_e, tb=True, out_dtype=BF16, name="w_uq")
    kv = _mm(c_kv, wc['mla_w_ukv'], tb=True, out_dtype=BF16, name="w_ukv")
    q, k, v = _prep2_fwd(qall, kv, proj, cos, sin, ws['qk_gq'], ws['qk_gk'])
    o_mla, lse = _attn_fwd(q, k, v)
    wc.prefetch('ffn2_w_gate', lse)

    u = proj[:, Q_RANK + KV_RANK:Q_RANK + KV_RANK + SSM_W]
    u_p = _time_perm(u)
    disc_in = [ws['ssm_lr'], ws['ssm_li'], ws['ssm_ldt'], ws['ssm_br'], ws['ssm_bi']]
    ar16, ai16, bbr, bbi = _rowwise(_f_disc, disc_in, [], [(SSM_P, F32)] * 4, name="s5_disc")
    a_r = ar16[::SSM_GRP].reshape(1, SSM_N)
    a_i = ai16[::SSM_GRP].reshape(1, SSM_N)
    bblk_r, bblk_i = _block_diag_b(bbr).astype(BF16), _block_diag_b(bbi).astype(BF16)
    cblk_r, cblk_i = _block_diag_c(ws['ssm_cr']).astype(BF16), _block_diag_c(-ws['ssm_ci']).astype(BF16)
    xr, xi, yc = _s5_scan(u_p, bblk_r, bblk_i, a_r, a_i, reverse=False, tb=False, readout=(cblk_r, cblk_i),
                          name="s5_scan_fwd")
    g_p = _rowwise(_f_s5_gelu, [yc, u_p], [ws['ssm_d']], [(SSM_W, F32)], name="s5_gelu")[0]
    z_p = _mm(g_p, wc['ssm_w_glu'], name="s5_glu")
    g_t, z_t = _time_perm(g_p, inverse=True), _time_perm(z_p, inverse=True)
    on_consts = [ws['ssm_b_glu'], ws['out_norm_mla'], ws['out_norm_ssm']]
    ycat = _rowwise(_f_outnorm, [o_mla, g_t, z_t], on_consts, [(D, BF16)], name="out_norm")[0]
    x2 = _mm(ycat, wc['w_o'], res=x1, name="w_o")

    hx = _rowwise(_f_norm, [x2], [ws['xattn_norm']], [(D, BF16)], name="xattn_norm")[0]
    xq = _mm(hx, wc['xattn_w_q'], name="xattn_q")
    mn = _rowwise(_f_norm, [mem], [ws['mem_norm']], [(D, BF16)], name="mem_norm")[0]
    kvm = _mm(mn, wc['xattn_w_kv'], name="xattn_kv")
    xkn, xv = _rowwise(_f_memk, [kvm], [ws['xattn_k_norm']], [(H * XH, BF16), (H * XH, BF16)], name="xattn_knorm")
    xo = _xattn_fwd(xq, xkn, xv, ws['xattn_q_norm'])
    x3 = _mm(xo, wc['xattn_w_o'], tb=True, res=x2, name="xattn_o")

    x4, sv2 = _ffn_fwd(x3, ws['ffn2_norm'], wc, "ffn2")

    def f_loss(yb, tb):
        err = yb - tb
        return err * (1.0 / D), jnp.broadcast_to(jnp.sum(jnp.sum(err * err, axis=1, keepdims=True), axis=0, keepdims=True) * (0.5 / D), (1, LANES))

    dx4, loss = _rowwise(f_loss, [x4, target], [], [(D, F32)], [(1, LANES)], name="loss")

    dx3, gs['ffn2_norm'] = _ffn_bwd(x3, ws['ffn2_norm'], wc, sv2, dx4, "ffn2", send)

    dxo = _mm(dx3, wc['xattn_w_o'], out_dtype=BF16, name="xattn_o_dx")
    send({'xattn_w_o': _mm(dx3, xo, ta=True, out_dtype=GRAD_DTYPE, name="xattn_o_dw")})
    dxq, dxkn, dxv, gs['xattn_q_norm'] = _xattn_bwd(xq, xkn, xv, ws['xattn_q_norm'], dxo)
    dkvm, gs['xattn_k_norm'] = _rowwise_bwd(_f_memk, [kvm], [ws['xattn_k_norm']], [dxkn, dxv], row_grads={0: BF16},
                                            const_grads=[0], name="xattn_knorm_bwd")
    send({'xattn_w_kv': _mm(mn, dkvm, ta=True, out_dtype=GRAD_DTYPE, name="xattn_kv_dw")})
    dmn = _mm(dkvm, wc['xattn_w_kv'], tb=True, out_dtype=BF16, name="xattn_kv_dx")
    gs['mem_norm'] = _rowwise_bwd(_f_norm, [mem], [ws['mem_norm']], [dmn], row_grads={}, const_grads=[0], name="mem_norm_bwd")[0]
    token = send({'xattn_w_q': _mm(hx, dxq, ta=True, out_dtype=GRAD_DTYPE, name="xattn_q_dw")})
    dhx = _mm(dxq, wc['xattn_w_q'], tb=True, out_dtype=BF16, name="xattn_q_dx")
    dx2, gs['xattn_norm'] = _rowwise_bwd(_f_norm, [x2], [ws['xattn_norm']], [dhx], row_grads={0: F32}, const_grads=[0],
                                         adds={0: dx3}, name="xattn_norm_bwd", deps=[token])

    dycat = _mm(dx2, wc['w_o'], tb=True, out_dtype=BF16, name="w_o_dx")
    send({'w_o': _mm(ycat, dx2, ta=True, out_dtype=GRAD_DTYPE, name="w_o_dw")})
    do_mla, dg_t, dz_t, gs['ssm_b_glu'], gs['out_norm_mla'], gs['out_norm_ssm'] = _rowwise_bwd(
        _f_outnorm, [o_mla, g_t, z_t], on_consts, [dycat], row_grads={0: F32, 1: F32, 2: BF16}, const_grads=[0, 1, 2],
        name="out_norm_bwd")

    dz_p, dg_p = _time_perm(dz_t), _time_perm(dg_t)
    send({'ssm_w_glu': _mm(g_p, dz_p, ta=True, out_dtype=GRAD_DTYPE, name="s5_glu_dw")})
    dg_p = _mm(dz_p, wc['ssm_w_glu'], tb=True, res=dg_p, name="s5_glu_dx")
    dyc, du_d, gs['ssm_d'] = _rowwise_bwd(_f_s5_gelu, [yc, u_p], [ws['ssm_d']], [dg_p], row_grads={0: BF16, 1: F32},
                                          const_grads=[0], name="s5_gelu_bwd")
    lam_r, lam_i = _s5_scan(dyc, cblk_r, cblk_i, a_r, -a_i, reverse=True, tb=True, name="s5_scan_bwd")
    du_p, d_bblk_r, d_bblk_i, d_cblk_r, d_cblk_i, d_ar, d_ai = _s5_grads(lam_r, lam_i, xr, xi, u_p, dyc, du_d,
                                                                        bblk_r, bblk_i)
    du = _time_perm(du_p, inverse=True)
    gs['ssm_cr'] = jax.linear_transpose(_block_diag_c, ws['ssm_cr'])(d_cblk_r)[0]
    gs['ssm_ci'] = -jax.linear_transpose(_block_diag_c, ws['ssm_ci'])(d_cblk_i)[0]
    d_bbr = jax.linear_transpose(_block_diag_b, bbr)(d_bblk_r)[0]
    d_bbi = jax.linear_transpose(_block_diag_b, bbi)(d_bblk_i)[0]
    d_ar16 = jnp.zeros((SSM_G * SSM_GRP, SSM_P), F32).at[::SSM_GRP].set(d_ar.reshape(SSM_G, SSM_P))
    d_ai16 = jnp.zeros((SSM_G * SSM_GRP, SSM_P), F32).at[::SSM_GRP].set(d_ai.reshape(SSM_G, SSM_P))
    gs['ssm_lr'], gs['ssm_li'], gs['ssm_ldt'], gs['ssm_br'], gs['ssm_bi'] = _rowwise_bwd(
        _f_disc, disc_in, [], [d_ar16, d_ai16, d_bbr, d_bbi], row_grads={i: F32 for i in range(5)}, const_grads=[],
        name="s5_disc_bwd")

    delta, do_b = _rowwise(_f_delta, [do_mla, o_mla], [], [(H * LANES, F32), (H * VD, BF16)], name="mla_delta")
    dq, dk, dv = _attn_bwd(q, k, v, do_b, lse, delta)
    dqall, dkv, dkr, dkrs, gs['qk_gq'], gs['qk_gk'] = _prep2_bwd(qall, kv, proj, cos, sin, ws['qk_gq'], ws['qk_gk'], dq, dk, dv)
    d_w_uq_e = _mm(dqall, c_q, ta=True, name="w_uq_dw")
    send({'mla_w_uq': jax.linear_transpose(_expand_w_uq, jax.ShapeDtypeStruct(w_uq_raw.shape, F32))(d_w_uq_e)[0]})
    dc_q = _mm(dqall, w_uq_e, out_dtype=BF16, name="w_uq_dx")
    send({'mla_w_ukv': _mm(dkv, c_kv, ta=True, out_dtype=GRAD_DTYPE, name="w_ukv_dw")})
    dc_kv = _mm(dkv, wc['mla_w_ukv'], out_dtype=BF16, name="w_ukv_dx")

    def f_prep1_bwd(pb, dcq, dckv, dub, dkrb, dkrsb, gq, gkv):
        _, vjp = jax.vjp(_f_prep1, pb[:, :Q_RANK + KV_RANK], gq, gkv)
        dpa, dgq, dgkv = vjp((dcq.astype(BF16), dckv.astype(BF16)))
        return jnp.concatenate([dpa, dub, dkrb, dkrsb], axis=-1), dgq, dgkv

    dproj, gs['q_norm'], gs['kv_norm'] = _rowwise(
        f_prep1_bwd, [proj, dc_q, dc_kv, du, dkr, dkrs], [ws['q_norm'], ws['kv_norm']], [(IN_WP, BF16)],
        [(1, Q_RANK), (1, KV_RANK)], name="mla_prep1_bwd")
    d_w_in_e = _mm(dproj, h2, ta=True, name="w_in_dw")
    token = send({'w_in': jax.linear_transpose(_expand_w_in, jax.ShapeDtypeStruct(w_in_raw.shape, F32))(d_w_in_e)[0]})
    dh2 = _mm(dproj, w_in_e, out_dtype=BF16, name="w_in_dx")
    dx1, gs['mix_norm'] = _rowwise_bwd(_f_norm, [x1], [ws['mix_norm']], [dh2], row_grads={0: F32}, const_grads=[0],
                                       adds={0: dx2}, name="mix_norm_bwd", deps=[token])

    token = send_small(gs, loss) if send_small is not None else None
    dx0, gs['ffn1_norm'] = _ffn_bwd(x, ws['ffn1_norm'], wc, sv1, dx1, "ffn1", send, deps=[token])
    return loss, dx0, gs


def _prep2_rows(qall, kv, proj, cos, sin):
    return [qall, kv, (proj, KR_BLOCK, LANES), (proj, KR_BLOCK + 1, LANES), cos, sin]


def _prep2_fwd(qall, kv, proj, cos, sin, gq, gk):
    return _rowwise(_f_prep2, _prep2_rows(qall, kv, proj, cos, sin), [gq, gk],
                    [(H * HQ, BF16), (H * HQ, BF16), (H * VD, BF16)], ts=256, name="mla_prep2")


def _prep2_bwd(qall, kv, proj, cos, sin, gq, gk, dq, dk, dv):
    return _rowwise_bwd(_f_prep2, _prep2_rows(qall, kv, proj, cos, sin), [gq, gk], [dq, dk, dv],
                        row_grads={0: BF16, 1: BF16, 2: F32, 3: F32}, const_grads=[0, 1], ts=256, name="mla_prep2_bwd")


def _rope_tables(pos):
    half = ROPE // 2
    inv = ROPE_THETA ** (-jnp.arange(half, dtype=F32) / half)
    ang = pos.astype(F32)[:, None] * inv[None, :]
    z = jnp.zeros((pos.shape[0], LANES - ROPE), F32)
    cos, sin = jnp.cos(ang), jnp.sin(ang)
    return jnp.concatenate([cos, cos, z], axis=-1), jnp.concatenate([sin, sin, z], axis=-1)


def _small_layout(p):
    lr, li, ldt, br, bi = _layout_ssm_in(p['ssm_a_re'], p['ssm_a_im'], p['ssm_log_dt'], p['ssm_b_re'], p['ssm_b_im'])
    return {
        'ffn1_norm': p['ffn1_norm'].reshape(1, D), 'mix_norm': p['mix_norm'].reshape(1, D),
        'q_norm': p['mla_q_norm'].reshape(1, Q_RANK), 'kv_norm': p['mla_kv_norm'].reshape(1, KV_RANK),
        'qk_gq': _layout_qk_gain(p['mla_qk_norm_q']), 'qk_gk': _layout_qk_gain(p['mla_qk_norm_k']),
        'ssm_lr': lr, 'ssm_li': li, 'ssm_ldt': ldt, 'ssm_br': br, 'ssm_bi': bi,
        'ssm_cr': p['ssm_c_re'], 'ssm_ci': p['ssm_c_im'], 'ssm_d': p['ssm_d'].reshape(1, SSM_W),
        'ssm_b_glu': p['ssm_b_glu'].reshape(1, SSM_W),
        'out_norm_mla': p['out_norm_mla'].reshape(1, SSM_W), 'out_norm_ssm': p['out_norm_ssm'].reshape(1, SSM_W),
        'xattn_norm': p['xattn_norm'].reshape(1, D), 'mem_norm': p['mem_norm'].reshape(1, D),
        'xattn_q_norm': p['xattn_q_norm'].reshape(1, XH), 'xattn_k_norm': p['xattn_k_norm'].reshape(1, XH),
        'ffn2_norm': p['ffn2_norm'].reshape(1, D),
    }


def _pack(arrs, rows):
    flat = jnp.concatenate([a.reshape(-1) for a in arrs])
    return jnp.pad(flat, (0, rows * D - flat.shape[0])).reshape(rows, D)


def _unpack(flat, shapes):
    flat = flat.reshape(-1)
    out, off = [], 0
    for sh in shapes:
        n = int(np.prod(sh))
        out.append(flat[off:off + n].reshape(sh))
        off += n
    return out


def kernel(x, mem, positions, ffn1_norm, ffn1_w_gate, ffn1_w_up, ffn1_w_down, mix_norm, w_in, mla_q_norm, mla_w_uq, mla_kv_norm, mla_w_ukv, mla_qk_norm_q, mla_qk_norm_k, ssm_a_re, ssm_a_im, ssm_log_dt, ssm_b_re, ssm_b_im, ssm_c_re, ssm_c_im, ssm_d, ssm_w_glu, ssm_b_glu, out_norm_mla, out_norm_ssm, w_o, xattn_norm, mem_norm, xattn_w_q, xattn_w_kv, xattn_q_norm, xattn_k_norm, xattn_w_o, ffn2_norm, ffn2_w_gate, ffn2_w_up, ffn2_w_down, loss_target, m_ffn1_norm, m_ffn1_w_gate, m_ffn1_w_up, m_ffn1_w_down, m_mix_norm, m_w_in, m_mla_q_norm, m_mla_w_uq, m_mla_kv_norm, m_mla_w_ukv, m_mla_qk_norm_q, m_mla_qk_norm_k, m_ssm_a_re, m_ssm_a_im, m_ssm_log_dt, m_ssm_b_re, m_ssm_b_im, m_ssm_c_re, m_ssm_c_im, m_ssm_d, m_ssm_w_glu, m_ssm_b_glu, m_out_norm_mla, m_out_norm_ssm, m_w_o, m_xattn_norm, m_mem_norm, m_xattn_w_q, m_xattn_w_kv, m_xattn_q_norm, m_xattn_k_norm, m_xattn_w_o, m_ffn2_norm, m_ffn2_w_gate, m_ffn2_w_up, m_ffn2_w_down, v_ffn1_norm, v_ffn1_w_gate, v_ffn1_w_up, v_ffn1_w_down, v_mix_norm, v_w_in, v_mla_q_norm, v_mla_w_uq, v_mla_kv_norm, v_mla_w_ukv, v_mla_qk_norm_q, v_mla_qk_norm_k, v_ssm_a_re, v_ssm_a_im, v_ssm_log_dt, v_ssm_b_re, v_ssm_b_im, v_ssm_c_re, v_ssm_c_im, v_ssm_d, v_ssm_w_glu, v_ssm_b_glu, v_out_norm_mla, v_out_norm_ssm, v_w_o, v_xattn_norm, v_mem_norm, v_xattn_w_q, v_xattn_w_kv, v_xattn_q_norm, v_xattn_k_norm, v_xattn_w_o, v_ffn2_norm, v_ffn2_w_gate, v_ffn2_w_up, v_ffn2_w_down):
    args = dict(locals())
    w = {n: args[n] for n in WEIGHTS}
    mom = {n: args['m_' + n] for n in WEIGHTS}
    var = {n: args['v_' + n] for n in WEIGHTS}
    return _step(x, mem, positions, loss_target, w, mom, var)


GATHER_GROUPS = [('ffn1_gu', ['ffn1_w_gate', 'ffn1_w_up']), ('ffn1_down', ['ffn1_w_down']),
                 ('mix', ['w_in', 'mla_w_uq', 'mla_w_ukv', 'ssm_w_glu', 'w_o', 'xattn_w_q', 'xattn_w_kv', 'xattn_w_o']),
                 ('ffn2', ['ffn2_w_gate', 'ffn2_w_up', 'ffn2_w_down'])]
SCATTER_GROUPS = [('ffn2_down', ['ffn2_w_down']), ('ffn2_gate', ['ffn2_w_gate']), ('ffn2_up', ['ffn2_w_up']),
                  ('xattn', ['xattn_w_o', 'xattn_w_kv', 'xattn_w_q']),
                  ('mix', ['w_o', 'ssm_w_glu', 'mla_w_uq', 'mla_w_ukv', 'w_in']),
                  ('ffn1_down', ['ffn1_w_down']), ('ffn1_gate', ['ffn1_w_gate']), ('ffn1_up', ['ffn1_w_up'])]


def _step(x, mem, positions, loss_target, w, mom, var):
    blocks = {n: _to_exchange_layout(n, w[n][0]).astype(BF16) for n in SHARDED}
    gathers, token = [], None
    for tag, names in GATHER_GROUPS:
        ex = _Exchange([blocks[n] for n in names], [blocks[n].shape[0] for n in names], gather=True,
                       name="gather_" + tag, after=token, two_level=True)
        gathers.append((names, ex))
        token = ex.token
    me = _my_slot()
    wc = _Weights(gathers, me=me)

    rows = {n: (blocks[n].shape[0], blocks[n].shape[0]) for n in SHARDED}
    ready, scatters = {}, []

    def send(grads):
        ready.update({n: g.astype(GRAD_DTYPE) for n, g in grads.items()})
        for tag, names in SCATTER_GROUPS:
            if all(n in ready for n in names) and not any(t == tag for t, _, _ in scatters):
                ex = _Exchange([ready[n] for n in names], [rows[n] for n in names], gather=False, name="scatter_" + tag)
                scatters.append((tag, names, ex))
                return ex.token
        return None

    small = {n: w[n][0] for n in SMALL}
    small_shapes = [small[n].shape for n in SMALL]
    n_small = sum(int(np.prod(sh)) for sh in small_shapes) + 1
    rows_small = -(-n_small // (8 * D)) * 8
    small_sent = []

    def send_small(gs, loss):
        known = dict(gs, ffn1_norm=jnp.zeros((1, D), F32))
        g_small = jax.linear_transpose(_small_layout, {n: jax.ShapeDtypeStruct(small[n].shape, F32) for n in SMALL})(known)[0]
        pack = _pack([g_small[n] for n in SMALL] + [loss[0, :1]], rows_small)
        small_sent.append(_Exchange([pack], [(None, rows_small)], gather=False, name="scatter_small"))
        return small_sent[0].token

    ws = _small_layout(small)
    cos, sin = _rope_tables(positions[0])
    loss, dx, gs = _local_step(x[0], mem[0], cos, sin, loss_target[0], wc, ws, send, deps=[token], send_small=send_small)
    pad8 = lambda a: jnp.pad(a.reshape(1, D), ((0, 7), (0, 0)))
    last_ex = _Exchange([pad8(gs['ffn1_norm'])], [(None, 8)], gather=False, name="scatter_last")

    out, after = {}, dx
    for _, names, ex in scatters:
        for n, sent, p in zip(names, *ex.wait(after)):
            r = w[n][0].shape[SHARD_AXIS[n]]
            if SHARD_AXIS[n] == 0:
                out[n] = _sum_adamw(me, sent, rows[n][0], p, r, w[n][0], mom[n][0], var[n][0], name="adamw_" + n)
            else:
                g = _sum_adamw(me, sent, rows[n][0], p, r, name="sum_" + n)[0].T
                out[n] = [g] + _adamw(g, w[n][0], mom[n][0], var[n][0], name="adamw_" + n)
        after = out[names[-1]][1]
    state = [_pack([t[n][0] for n in SMALL], rows_small) for t in (w, mom, var)]
    sent, p = small_sent[0].wait(after)
    small_out = _sum_adamw(me, sent[0], None, p[0], rows_small, *state, name="adamw_small")
    loss_total = small_out[0].reshape(-1)[n_small - 1]
    for n, vals in zip(SMALL, zip(*[_unpack(flat, small_shapes) for flat in small_out])):
        out[n] = vals
    sent, p = last_ex.wait(small_out[1])
    last_out = _sum_adamw(me, sent[0], None, p[0], 8, *[pad8(t['ffn1_norm'][0]) for t in (w, mom, var)], name="adamw_last")
    out['ffn1_norm'] = [o[0] for o in last_out]
    outs = [out[n][i][None] for i in range(4) for n in WEIGHTS]
    return (loss_total, dx[None], *outs)
```

```python
import math

import jax
import jax.numpy as jnp
import numpy as np
from jax import lax
from jax.experimental import pallas as pl
from jax.experimental.pallas import tpu as pltpu

F32 = jnp.float32
BF16 = jnp.bfloat16

N_DEV = 8
D = 1024
D_FF = 2752
D_FFP = 2816
MEM_LEN = 256
H = 4
Q_RANK, KV_RANK, NOPE, ROPE, VD = 384, 256, 128, 64, 128
QK = NOPE + ROPE
HQ = 2 * 128
SSM_W, SSM_G, SSM_GRP, SSM_P = 512, 32, 16, 64
SSM_N = SSM_G * SSM_P
SSM_PACK = 8
IN_W = 1216
IN_WP = 1408
XH = 128
EPS = 1e-6
LN2 = math.log(2.0)
ROPE_THETA = 10000.0
SCAN_CHUNKS = 8
SCAN_UNROLL = 8
ADAM_LR, ADAM_B1, ADAM_B2, ADAM_EPS, ADAM_WD, ADAM_STEP = 0.001, 0.9, 0.999, 1e-08, 0.01, 10

VMEM_LIMIT = 56 * 1024 * 1024
ACC_BYTES = 6 * 1024 * 1024
LANES = 128
BF16_ROWS = 16
GRAD_DTYPE = BF16
FF_SHARD = D_FF // N_DEV
FF_SHARD_P = 352
IN_SHARD = IN_W // N_DEV
IN_SHARD_P = 160

WEIGHTS = ['ffn1_norm', 'ffn1_w_gate', 'ffn1_w_up', 'ffn1_w_down', 'mix_norm', 'w_in', 'mla_q_norm', 'mla_w_uq',
           'mla_kv_norm', 'mla_w_ukv', 'mla_qk_norm_q', 'mla_qk_norm_k', 'ssm_a_re', 'ssm_a_im', 'ssm_log_dt',
           'ssm_b_re', 'ssm_b_im', 'ssm_c_re', 'ssm_c_im', 'ssm_d', 'ssm_w_glu', 'ssm_b_glu', 'out_norm_mla',
           'out_norm_ssm', 'w_o', 'xattn_norm', 'mem_norm', 'xattn_w_q', 'xattn_w_kv', 'xattn_q_norm',
           'xattn_k_norm', 'xattn_w_o', 'ffn2_norm', 'ffn2_w_gate', 'ffn2_w_up', 'ffn2_w_down']
SHARD_AXIS = {'ffn1_w_gate': 1, 'ffn1_w_up': 1, 'ffn1_w_down': 0, 'w_in': 1, 'mla_w_uq': 1, 'mla_w_ukv': 1,
              'ssm_w_glu': 0, 'w_o': 0, 'xattn_w_q': 0, 'xattn_w_kv': 0, 'xattn_w_o': 1,
              'ffn2_w_gate': 1, 'ffn2_w_up': 1, 'ffn2_w_down': 0}
SHARDED = [n for n in WEIGHTS if n in SHARD_AXIS]
SMALL = [n for n in WEIGHTS if n not in SHARD_AXIS]


def _params(sem=None):
    return pltpu.CompilerParams(dimension_semantics=sem, vmem_limit_bytes=VMEM_LIMIT)


def _tile(n, cap):
    if n <= cap:
        return n
    best = n
    for t in range(LANES, cap + 1, LANES):
        if n % t == 0:
            best = t
    return best


def _mm(a, b, *, ta=False, tb=False, out_dtype=F32, res=None, scale=1.0, name, tm_cap=1024, tn_cap=1408, tk_cap=2816,
        deps=()):
    m, k = (a.shape[1], a.shape[0]) if ta else a.shape
    k2, n = (b.shape[1], b.shape[0]) if tb else b.shape
    assert k == k2, (a.shape, b.shape, ta, tb)
    if ta:
        tk_cap = min(tk_cap, 1024)
        tm_cap = 1408
    tm, tn, tk = _tile(m, tm_cap), _tile(n, tn_cap), _tile(k, tk_cap)
    if tm * tn * 4 > ACC_BYTES:
        tn = _tile(n, max(LANES, ACC_BYTES // (4 * tm) // LANES * LANES))
    nk = k // tk
    dims = (((0 if ta else 1,), (1 if tb else 0,)), ((), ()))
    has_res = res is not None

    deps = [d for d in deps if d is not None]

    def body(*refs):
        a_ref, b_ref = refs[:2]
        r_ref = refs[2] if has_res else None
        o_ref, acc_ref = refs[-2:]
        kk = pl.program_id(2)

        @pl.when(kk == 0)
        def _():
            acc_ref[...] = jnp.zeros_like(acc_ref)

        acc_ref[...] += lax.dot_general(a_ref[...].astype(BF16), b_ref[...].astype(BF16), dims,
                                        preferred_element_type=F32)

        @pl.when(kk == nk - 1)
        def _():
            out = acc_ref[...]
            if scale != 1.0:
                out = out * scale
            if has_res:
                out = out + r_ref[...].astype(F32)
            o_ref[...] = out.astype(o_ref.dtype)

    a_spec = pl.BlockSpec((tk, tm), lambda i, j, kk: (kk, i)) if ta else pl.BlockSpec((tm, tk), lambda i, j, kk: (i, kk))
    b_spec = pl.BlockSpec((tn, tk), lambda i, j, kk: (j, kk)) if tb else pl.BlockSpec((tk, tn), lambda i, j, kk: (kk, j))
    o_spec = pl.BlockSpec((tm, tn), lambda i, j, kk: (i, j))
    in_specs = [a_spec, b_spec] + ([o_spec] if has_res else []) + [pl.BlockSpec(d.shape, lambda i, j, kk: (0, 0)) for d in deps]
    args = (a, b) + ((res,) if has_res else ()) + tuple(deps)
    return pl.pallas_call(
        body, name=name, grid=(m // tm, n // tn, nk), in_specs=in_specs, out_specs=o_spec,
        out_shape=jax.ShapeDtypeStruct((m, n), out_dtype), scratch_shapes=[pltpu.VMEM((tm, tn), F32)],
        compiler_params=_params(("parallel", "parallel", "arbitrary")),
    )(*args)


def _rowwise(fn, rows, consts, outs, accs=(), *, ts=1024, name, deps=()):
    rows = [r if isinstance(r, tuple) else (r, 0, r.shape[1]) for r in rows]
    s = rows[0][0].shape[0]
    ts = min(ts, s)
    assert s % ts == 0
    n_rows, n_consts, n_outs = len(rows), len(consts), len(outs)
    deps = [d for d in deps if d is not None]
    consts = list(consts) + deps

    def body(*refs):
        ins = [r[...] for r in refs[:n_rows + n_consts]]
        res = fn(*ins)
        res = tuple(res) if isinstance(res, (tuple, list)) else (res,)
        out_refs = refs[n_rows + len(consts):]
        for o_ref, val in zip(out_refs[:n_outs], res[:n_outs]):
            o_ref[...] = val.astype(o_ref.dtype)
        if accs:
            first = pl.program_id(0) == 0

            @pl.when(first)
            def _():
                for a_ref, val in zip(out_refs[n_outs:], res[n_outs:]):
                    a_ref[...] = val.astype(F32)

            @pl.when(jnp.logical_not(first))
            def _():
                for a_ref, val in zip(out_refs[n_outs:], res[n_outs:]):
                    a_ref[...] += val.astype(F32)

    in_specs = [pl.BlockSpec((ts, width), lambda i, cb=cb: (i, cb)) for _, cb, width in rows]
    in_specs += [pl.BlockSpec(c.shape, lambda i: (0, 0)) for c in consts]
    out_specs = [pl.BlockSpec((ts, w), lambda i: (i, 0)) for w, _ in outs]
    out_specs += [pl.BlockSpec(tuple(sh), lambda i: (0, 0)) for sh in accs]
    out_shape = [jax.ShapeDtypeStruct((s, w), dt) for w, dt in outs]
    out_shape += [jax.ShapeDtypeStruct(tuple(sh), F32) for sh in accs]
    res = pl.pallas_call(
        body, name=name, grid=(s // ts,), in_specs=in_specs, out_specs=out_specs, out_shape=out_shape,
        compiler_params=_params(("arbitrary",)),
    )(*[a for a, _, _ in rows], *consts)
    return res


def _rowwise_bwd(f, rows, consts, cts, *, row_grads, const_grads, adds=None, ts=512, name, deps=()):
    adds = adds or {}
    n_rows, n_consts, n_cts = len(rows), len(consts), len(cts)
    add_keys = sorted(adds)
    rg = sorted(row_grads)
    cg = sorted(const_grads)

    def fn(*args):
        r = args[:n_rows]
        c = args[n_rows:n_rows + n_consts]
        ct = args[n_rows + n_consts:n_rows + n_consts + n_cts]
        extra = args[n_rows + n_consts + n_cts:]
        outs, vjp = jax.vjp(f, *r, *c)
        outs = tuple(outs) if isinstance(outs, (tuple, list)) else (outs,)
        cot = tuple(g.astype(o.dtype) for g, o in zip(ct, outs))
        grads = vjp(cot if len(cot) > 1 else cot[0])
        res = []
        for i in rg:
            g = grads[i].astype(F32)
            if i in adds:
                g = g + extra[add_keys.index(i)].astype(F32)
            res.append(g)
        for i in cg:
            res.append(grads[n_rows + i])
        return tuple(res)

    rows_all = list(rows) + list(cts) + [adds[i] for i in add_keys]
    def fn2(*args):
        nr = len(rows_all)
        rr, cc = args[:nr], args[nr:]
        return fn(*rr[:n_rows], *cc, *rr[n_rows:])

    outs = [(rows[i][2] if isinstance(rows[i], tuple) else rows[i].shape[1], row_grads[i]) for i in rg]
    accs = [consts[i].shape for i in cg]
    return _rowwise(fn2, rows_all, list(consts), outs, accs, ts=ts, name=name, deps=deps)


def _rms(x, g):
    xf = x.astype(F32)
    return xf * lax.rsqrt(jnp.mean(xf * xf, axis=-1, keepdims=True) + EPS) * g.astype(F32)


def _sigmoid(x):
    return 1.0 / (1.0 + jnp.exp(-x))


def _f_norm(x, g):
    return _rms(x, g).astype(BF16)


def _f_swiglu(gate, up):
    gate, up = gate.astype(F32), up.astype(F32)
    return (gate * _sigmoid(gate) * up).astype(BF16)


def _f_prep1(proj, gq, gkv):
    return _rms(proj[:, :Q_RANK], gq).astype(BF16), _rms(proj[:, Q_RANK:Q_RANK + KV_RANK], gkv).astype(BF16)


KR_BLOCK = (Q_RANK + KV_RANK + SSM_W) // LANES


def _f_prep2(qall, kv, kr, krs, cos, sin, gq, gk):
    kr, krs = kr.astype(F32), krs.astype(F32)
    k_rot = kr * gk[1:2] * cos + krs * gk[2:3] * sin
    k_ss = jnp.sum(kr * kr, axis=-1, keepdims=True)
    q_scale = QK ** -0.5 / LN2
    qs, ks, vs = [], [], []
    for h in range(H):
        qn = qall[:, h * LANES:(h + 1) * LANES].astype(F32)
        qr = qall[:, (H + h) * LANES:(H + h + 1) * LANES].astype(F32)
        qrs = qall[:, (2 * H + h) * LANES:(2 * H + h + 1) * LANES].astype(F32)
        rstd = lax.rsqrt((jnp.sum(qn * qn, axis=-1, keepdims=True) + jnp.sum(qr * qr, axis=-1, keepdims=True)) / QK + EPS)
        rstd = rstd * q_scale
        qs += [qn * gq[0:1] * rstd, (qr * gq[1:2] * cos + qrs * gq[2:3] * sin) * rstd]
        kn = kv[:, 2 * h * LANES:(2 * h + 1) * LANES].astype(F32)
        rstd_k = lax.rsqrt((jnp.sum(kn * kn, axis=-1, keepdims=True) + k_ss) / QK + EPS)
        ks += [kn * gk[0:1] * rstd_k, k_rot * rstd_k]
        vs.append(kv[:, (2 * h + 1) * LANES:(2 * h + 2) * LANES])
    return (jnp.concatenate(qs, axis=-1).astype(BF16), jnp.concatenate(ks, axis=-1).astype(BF16),
            jnp.concatenate(vs, axis=-1).astype(BF16))


def _gelu(x):
    return 0.5 * x * (1.0 + jnp.tanh(math.sqrt(2.0 / math.pi) * (x + 0.044715 * (x * x * x))))


def _f_s5_gelu(yc, u, d):
    return _gelu(yc.astype(F32) + d * u.astype(F32))


def _f_outnorm(o_mla, g, z, b_glu, g_om, g_os):
    y_ssm = g * _sigmoid(z + b_glu)
    return jnp.concatenate([_rms(o_mla, g_om), _rms(y_ssm, g_os)], axis=-1).astype(BF16)


def _f_memk(kvm, gk):
    ks = [_rms(kvm[:, h * XH:(h + 1) * XH], gk) for h in range(H)]
    return jnp.concatenate(ks, axis=-1).astype(BF16), kvm[:, H * XH:].astype(BF16)


def _f_disc(lr, li, log_dt, br, bi):
    dt = jnp.exp(log_dt)
    decay = jnp.exp(lr * dt)
    ar = decay * jnp.cos(li * dt)
    ai = decay * jnp.sin(li * dt)
    den = lr * lr + li * li
    nr = ar - 1.0
    coef_r = (nr * lr + ai * li) / den
    coef_i = (ai * lr - nr * li) / den
    return ar, ai, coef_r * br - coef_i * bi, coef_r * bi + coef_i * br


def _causal_mask(i, j, tq, tk):
    qpos = i * tq + lax.broadcasted_iota(jnp.int32, (tq, tk), 0)
    kpos = j * tk + lax.broadcasted_iota(jnp.int32, (tq, tk), 1)
    return qpos >= kpos


def _attn_fwd(q, k, v, *, t=512):
    s = q.shape[0]
    t = min(t, s)
    nb = s // t

    def body(q_ref, k_ref, v_ref, o_ref, lse_ref, m_sc, l_sc, acc_sc):
        i, j = pl.program_id(1), pl.program_id(2)

        @pl.when(j == 0)
        def _():
            m_sc[...] = jnp.full_like(m_sc, -jnp.inf)
            l_sc[...] = jnp.zeros_like(l_sc)
            acc_sc[...] = jnp.zeros_like(acc_sc)

        def block(diagonal):
            sc = lax.dot_general(q_ref[...], k_ref[...], (((1,), (1,)), ((), ())), preferred_element_type=F32)
            if diagonal:
                sc = jnp.where(_causal_mask(i, j, t, t), sc, -jnp.inf)
            m_old = m_sc[...]
            m_new = jnp.maximum(m_old, jnp.max(sc, axis=-1, keepdims=True))
            p = jnp.exp2(sc - m_new)
            alpha = jnp.exp2(m_old - m_new)
            l_sc[...] = alpha * l_sc[...] + jnp.sum(p, axis=-1, keepdims=True)
            acc_sc[...] = alpha * acc_sc[...] + jnp.dot(p.astype(BF16), v_ref[...], preferred_element_type=F32)
            m_sc[...] = m_new

        pl.when(j < i)(lambda: block(False))

        @pl.when(j == i)
        def _():
            block(True)
            o_ref[...] = acc_sc[...] / l_sc[...]
            lse_ref[...] = jnp.broadcast_to(m_sc[...] + jnp.log2(l_sc[...]), lse_ref.shape)

    kv_map = lambda h, i, j: (jnp.minimum(j, i), h)
    return pl.pallas_call(
        body, name="mla_attn_fwd", grid=(H, nb, nb),
        in_specs=[pl.BlockSpec((t, HQ), lambda h, i, j: (i, h)), pl.BlockSpec((t, HQ), kv_map),
                  pl.BlockSpec((t, VD), kv_map)],
        out_specs=[pl.BlockSpec((t, VD), lambda h, i, j: (i, h)), pl.BlockSpec((t, LANES), lambda h, i, j: (i, h))],
        out_shape=[jax.ShapeDtypeStruct((s, H * VD), F32), jax.ShapeDtypeStruct((s, H * LANES), F32)],
        scratch_shapes=[pltpu.VMEM((t, 1), F32), pltpu.VMEM((t, 1), F32), pltpu.VMEM((t, VD), F32)],
        compiler_params=_params(("parallel", "parallel", "arbitrary")),
    )(q, k, v)


def _attn_probs(q_ref, k_ref, v_ref, do_ref, lse_ref, dl_ref, i, j, t, diagonal):
    sc = lax.dot_general(q_ref[...], k_ref[...], (((1,), (1,)), ((), ())), preferred_element_type=F32)
    p = jnp.exp2(sc - jnp.tile(lse_ref[...], (1, t // LANES)))
    if diagonal:
        p = jnp.where(_causal_mask(i, j, t, t), p, 0.0)
    dp = lax.dot_general(do_ref[...], v_ref[...], (((1,), (1,)), ((), ())), preferred_element_type=F32)
    ds = p * (dp - jnp.tile(dl_ref[...], (1, t // LANES)))
    return p, ds


def _attn_bwd(q, k, v, do, lse, delta, *, t=512):
    s = q.shape[0]
    t = min(t, s)
    nb = s // t

    def body(q_ref, k_ref, v_ref, do_ref, lse_ref, dl_ref, dq_ref, dk_ref, dv_ref, dk_sc, dv_sc):
        j, i = pl.program_id(1), pl.program_id(2)

        @pl.when(jnp.logical_and(i == 0, j == 0))
        def _():
            dq_ref[...] = jnp.zeros_like(dq_ref)

        @pl.when(i == 0)
        def _():
            dk_sc[...] = jnp.zeros_like(dk_sc)
            dv_sc[...] = jnp.zeros_like(dv_sc)

        def block(diagonal):
            p, ds = _attn_probs(q_ref, k_ref, v_ref, do_ref, lse_ref, dl_ref, i, j, t, diagonal)
            dsb = ds.astype(BF16)
            dv_sc[...] += lax.dot_general(p.astype(BF16), do_ref[...], (((0,), (0,)), ((), ())), preferred_element_type=F32)
            dk_sc[...] += lax.dot_general(dsb, q_ref[...], (((0,), (0,)), ((), ())), preferred_element_type=F32)
            rows = pl.ds(pl.multiple_of(i * t, t), t)
            dq_ref[rows, :] += jnp.dot(dsb, k_ref[...], preferred_element_type=F32)

        pl.when(i > j)(lambda: block(False))
        pl.when(i == j)(lambda: block(True))

        @pl.when(i == nb - 1)
        def _():
            dk_ref[...] = (dk_sc[...] * LN2).astype(dk_ref.dtype)
            dv_ref[...] = dv_sc[...].astype(dv_ref.dtype)

        @pl.when(jnp.logical_and(i == nb - 1, j == nb - 1))
        def _():
            dq_ref[...] = dq_ref[...] * LN2

    q_map = lambda h, j, i: (jnp.maximum(i, j), h)
    kv_map = lambda h, j, i: (j, h)
    dq, dk, dv = pl.pallas_call(
        body, name="mla_attn_bwd", grid=(H, nb, nb),
        in_specs=[pl.BlockSpec((t, HQ), q_map), pl.BlockSpec((t, HQ), kv_map), pl.BlockSpec((t, VD), kv_map),
                  pl.BlockSpec((t, VD), q_map), pl.BlockSpec((t, LANES), q_map), pl.BlockSpec((t, LANES), q_map)],
        out_specs=[pl.BlockSpec((s, HQ), lambda h, j, i: (0, h)), pl.BlockSpec((t, HQ), kv_map), pl.BlockSpec((t, VD), kv_map)],
        out_shape=[jax.ShapeDtypeStruct((s, H * HQ), F32), jax.ShapeDtypeStruct((s, H * HQ), BF16),
                   jax.ShapeDtypeStruct((s, H * VD), BF16)],
        scratch_shapes=[pltpu.VMEM((t, HQ), F32), pltpu.VMEM((t, VD), F32)],
        compiler_params=_params(("parallel", "arbitrary", "arbitrary")),
    )(q, k, v, do, lse, delta)
    return dq, dk, dv


def _f_delta(do, o):
    prod = do.astype(F32) * o.astype(F32)
    parts = [jnp.broadcast_to(jnp.sum(prod[:, h * VD:(h + 1) * VD], axis=-1, keepdims=True), (do.shape[0], LANES))
             for h in range(H)]
    return jnp.concatenate(parts, axis=-1), do.astype(BF16)


def _xattn_head(qh, kh, gq):
    qn = _rms(qh, gq) * (XH ** -0.5)
    sc = lax.dot_general(qn.astype(BF16), kh, (((1,), (1,)), ((), ())), preferred_element_type=F32)
    sc = sc - jnp.max(sc, axis=-1, keepdims=True)
    e = jnp.exp(sc)
    return qn, e / jnp.sum(e, axis=-1, keepdims=True)


def _xattn_fwd(q, kn, v, gq, *, ts=512):
    def fn(qb, knb, vb, g):
        outs = []
        for h in range(H):
            sl = slice(h * XH, (h + 1) * XH)
            _, p = _xattn_head(qb[:, sl], knb[:, sl], g)
            outs.append(jnp.dot(p.astype(BF16), vb[:, sl], preferred_element_type=F32))
        return (jnp.concatenate(outs, axis=-1),)

    return _rowwise(fn, [q], [kn, v, gq], [(H * XH, BF16)], ts=ts, name="xattn_fwd")[0]


def _xattn_bwd(q, kn, v, gq, do, *, ts=512):
    def fn(qb, dob, knb, vb, g):
        dqs, dks, dvs = [], [], []
        dg = jnp.zeros((1, XH), F32)
        for h in range(H):
            sl = slice(h * XH, (h + 1) * XH)
            qh, kh, vh, doh = qb[:, sl], knb[:, sl], vb[:, sl], dob[:, sl].astype(BF16)
            qn, p = _xattn_head(qh, kh, g)
            dp = lax.dot_general(doh, vh, (((1,), (1,)), ((), ())), preferred_element_type=F32)
            dvs.append(lax.dot_general(p.astype(BF16), doh, (((0,), (0,)), ((), ())), preferred_element_type=F32))
            ds = (p * (dp - jnp.sum(dp * p, axis=-1, keepdims=True))).astype(BF16)
            dqn = jnp.dot(ds, kh, preferred_element_type=F32)
            dks.append(lax.dot_general(ds, qn.astype(BF16), (((0,), (0,)), ((), ())), preferred_element_type=F32))
            _, vjp_n = jax.vjp(lambda a, b: _rms(a, b) * (XH ** -0.5), qh, g)
            dqh, dgh = vjp_n(dqn)
            dqs.append(dqh)
            dg = dg + dgh
        return (jnp.concatenate(dqs, axis=-1), jnp.concatenate(dks, axis=-1), jnp.concatenate(dvs, axis=-1), dg)

    return _rowwise(fn, [q, do], [kn, v, gq], [(H * XH, BF16)], [kn.shape, v.shape, gq.shape], ts=ts, name="xattn_bwd")


def _cmul(ar, ai, xr, xi):
    return ar * xr - ai * xi, ar * xi + ai * xr


def _scan_in_place(xr_ref, xi_ref, ar, ai, *, reverse):
    s, cw = xr_ref.shape
    c = SCAN_CHUNKS
    tt = s // c
    a_r = jnp.broadcast_to(ar, (c, cw))
    a_i = jnp.broadcast_to(ai, (c, cw))
    zero = jnp.zeros((c, cw), F32)

    def row(step):
        t = (tt - 1 - step) if reverse else step
        return pl.ds(pl.multiple_of(t * c, c), c)

    def local(step, carry):
        sr, si, qr, qi = carry
        r = row(step)
        nr, ni = _cmul(a_r, a_i, sr, si)
        nr, ni = nr + xr_ref[r, :], ni + xi_ref[r, :]
        xr_ref[r, :] = nr
        xi_ref[r, :] = ni
        return (nr, ni) + _cmul(a_r, a_i, qr, qi)

    end_r, end_i, pr, pi = lax.fori_loop(0, tt, local, (zero, zero, jnp.ones((c, cw), F32), zero), unroll=SCAN_UNROLL)

    rows_id = lax.broadcasted_iota(jnp.int32, (c, cw), 0)
    car_r, car_i = zero, zero
    cur_r, cur_i = jnp.zeros((1, cw), F32), jnp.zeros((1, cw), F32)
    order = range(c - 1, -1, -1) if reverse else range(c)
    for kk in order:
        car_r = jnp.where(rows_id == kk, cur_r, car_r)
        car_i = jnp.where(rows_id == kk, cur_i, car_i)
        nr, ni = _cmul(pr[0:1], pi[0:1], cur_r, cur_i)
        cur_r = nr + end_r[kk:kk + 1]
        cur_i = ni + end_i[kk:kk + 1]

    def fix(step, carry):
        qr, qi = _cmul(a_r, a_i, *carry)
        r = row(step)
        dr, di = _cmul(qr, qi, car_r, car_i)
        xr_ref[r, :] += dr
        xi_ref[r, :] += di
        return qr, qi

    lax.fori_loop(0, tt, fix, (jnp.ones((c, cw), F32), zero), unroll=SCAN_UNROLL)


S5_ROWS = 1024


def _s5_scan(v, w_r, w_i, ar, ai, *, reverse, tb, readout=None, name):
    s = v.shape[0]
    g = w_r.shape[0]
    nv, ns = SSM_PACK * SSM_GRP, SSM_PACK * SSM_P
    rows = min(S5_ROWS, s)
    dims = (((1,), (1 if tb else 0,)), ((), ()))
    n_w = 2 if readout is None else 4

    def body(v_ref, ar_ref, ai_ref, *refs):
        w = [r[...] for r in refs[:n_w]]
        xr_ref, xi_ref = refs[n_w:n_w + 2]
        for r0 in range(0, s, rows):
            vb = v_ref[r0:r0 + rows, :].astype(BF16)
            xr_ref[r0:r0 + rows, :] = lax.dot_general(vb, w[0], dims, preferred_element_type=F32)
            xi_ref[r0:r0 + rows, :] = lax.dot_general(vb, w[1], dims, preferred_element_type=F32)
        _scan_in_place(xr_ref, xi_ref, ar_ref[...], ai_ref[...], reverse=reverse)
        if readout is not None:
            y_ref = refs[n_w + 2]
            for r0 in range(0, s, rows):
                y_ref[r0:r0 + rows, :] = (
                    jnp.dot(xr_ref[r0:r0 + rows, :].astype(BF16), w[2], preferred_element_type=F32)
                    + jnp.dot(xi_ref[r0:r0 + rows, :].astype(BF16), w[3], preferred_element_type=F32))

    col = lambda j: (0, j)
    w_spec = lambda a: pl.BlockSpec((None,) + a.shape[1:], lambda j: (j, 0, 0))
    weights = [w_r, w_i] + (list(readout) if readout is not None else [])
    out_specs = [pl.BlockSpec((s, ns), col)] * 2 + ([pl.BlockSpec((s, nv), col)] if readout is not None else [])
    out_shape = [jax.ShapeDtypeStruct((s, g * ns), F32)] * 2 + (
        [jax.ShapeDtypeStruct((s, g * nv), F32)] if readout is not None else [])
    return pl.pallas_call(
        body, name=name, grid=(g,),
        in_specs=[pl.BlockSpec((s, nv), col), pl.BlockSpec((1, ns), col), pl.BlockSpec((1, ns), col)] + [w_spec(a) for a in weights],
        out_specs=out_specs, out_shape=out_shape, compiler_params=_params(("parallel",)),
    )(v, ar, ai, *weights)


def _s5_grads(lam_r, lam_i, xr, xi, u, dyc, du_d, b_r, b_i):
    s = u.shape[0]
    g = b_r.shape[0]
    nv, ns, c = SSM_PACK * SSM_GRP, SSM_PACK * SSM_P, SCAN_CHUNKS
    rows = min(S5_ROWS, s)
    slabs = rows // c
    last_slab = s // c - 1
    nt = (((1,), (1,)), ((), ()))
    tn = (((0,), (0,)), ((), ()))

    def body(lr_ref, li_ref, xr_ref, xi_ref, pr_ref, pi_ref, u_ref, dy_ref, dud_ref, br_ref, bi_ref,
             du_ref, dbr_ref, dbi_ref, dcr_ref, dci_ref, dar_ref, dai_ref):
        first = pl.program_id(1) == 0
        l_r, l_i, x_r, x_i = lr_ref[...], li_ref[...], xr_ref[...], xi_ref[...]
        lrb, lib = l_r.astype(BF16), l_i.astype(BF16)
        du_ref[...] = (dud_ref[...] + lax.dot_general(lrb, br_ref[...], nt, preferred_element_type=F32)
                       + lax.dot_general(lib, bi_ref[...], nt, preferred_element_type=F32))
        ub, dyb = u_ref[...].astype(BF16), dy_ref[...].astype(BF16)
        rows_id = lax.broadcasted_iota(jnp.int32, (c, ns), 0)

        def before(p_ref, x):
            p = p_ref[...]
            p = jnp.where(first, jnp.where(rows_id == 0, 0.0, pltpu.roll(p, 1, 0)), p)
            return jnp.concatenate([p, x[:rows - c]], axis=0)

        xp_r, xp_i = before(pr_ref, x_r), before(pi_ref, x_i)
        parts = (lax.dot_general(ub, lrb, tn, preferred_element_type=F32),
                 lax.dot_general(ub, lib, tn, preferred_element_type=F32),
                 lax.dot_general(x_r.astype(BF16), dyb, tn, preferred_element_type=F32),
                 lax.dot_general(x_i.astype(BF16), dyb, tn, preferred_element_type=F32),
                 jnp.sum(l_r * xp_r + l_i * xp_i, axis=0, keepdims=True),
                 jnp.sum(l_i * xp_r - l_r * xp_i, axis=0, keepdims=True))
        accs = (dbr_ref, dbi_ref, dcr_ref, dci_ref, dar_ref, dai_ref)

        @pl.when(first)
        def _():
            for a_ref, val in zip(accs, parts):
                a_ref[...] = val

        @pl.when(jnp.logical_not(first))
        def _():
            for a_ref, val in zip(accs, parts):
                a_ref[...] += val

    state = pl.BlockSpec((rows, ns), lambda j, k: (k, j))
    chan = pl.BlockSpec((rows, nv), lambda j, k: (k, j))
    slab = pl.BlockSpec((c, ns), lambda j, k: (jnp.where(k == 0, last_slab, k * slabs - 1), j))
    per_b = pl.BlockSpec((None, nv, ns), lambda j, k: (j, 0, 0))
    per_c = pl.BlockSpec((None, ns, nv), lambda j, k: (j, 0, 0))
    per_a = pl.BlockSpec((1, ns), lambda j, k: (0, j))
    return pl.pallas_call(
        body, name="s5_grads", grid=(g, s // rows),
        in_specs=[state, state, state, state, slab, slab, chan, chan, chan, per_b, per_b],
        out_specs=[chan, per_b, per_b, per_c, per_c, per_a, per_a],
        out_shape=[jax.ShapeDtypeStruct((s, g * nv), F32), jax.ShapeDtypeStruct((g, nv, ns), F32),
                   jax.ShapeDtypeStruct((g, nv, ns), F32), jax.ShapeDtypeStruct((g, ns, nv), F32),
                   jax.ShapeDtypeStruct((g, ns, nv), F32), jax.ShapeDtypeStruct((1, g * ns), F32),
                   jax.ShapeDtypeStruct((1, g * ns), F32)],
        compiler_params=_params(("parallel", "arbitrary")),
    )(lam_r, lam_i, xr, xi, xr, xi, u, dyc, du_d, b_r, b_i)


def _mesh_place():
    x, y, c = lax.axis_index("x"), lax.axis_index("y"), lax.axis_index("c")
    peers = []
    for k in range(1, N_DEV):
        px, py, pc = x ^ ((k >> 2) & 1), y ^ ((k >> 1) & 1), c ^ (k & 1)
        peers.append(((px, py, pc), 4 * px + 2 * py + pc))
    return 4 * x + 2 * y + c, peers


class _Exchange:
    SAME_CORE_MASKS = (2, 4, 6)

    def __init__(self, arrays, rows, *, gather, name, after=None, two_level=False):
        self.n_arr, self.rows, self.gather, self.name = len(arrays), rows, gather, name
        self.two_level, self.in_flight = two_level, (1 + len(self.SAME_CORE_MASKS) if two_level else N_DEV - 1)
        n_arr = self.n_arr
        if gather:
            assert all(r % BF16_ROWS == 0 for r in rows)
            lands = [lax.empty((N_DEV * r, a.shape[1]), a.dtype) for a, r in zip(arrays, rows)]
        else:
            lands = [lax.empty((N_DEV - 1,) + (tuple(a.shape) if st is None else (n, a.shape[1])), a.dtype)
                     for a, (st, n) in zip(arrays, rows)]
        has_after = after is not None

        def body(*refs):
            ins, zones = refs[:n_arr], refs[n_arr:2 * n_arr]
            sems = refs[2 * n_arr + has_after:4 * n_arr + has_after]
            token = refs[-1]
            me, peers = _mesh_place()
            for i in range(n_arr):
                for k, (pxyz, pid) in enumerate(peers):
                    if two_level and k + 1 not in (1,) + self.SAME_CORE_MASKS:
                        continue
                    if gather:
                        src = ins[i]
                        dst = zones[i].at[pl.ds(pl.multiple_of(me * rows[i], BF16_ROWS), rows[i])]
                    else:
                        stride, n = rows[i]
                        src = ins[i] if stride is None else ins[i].at[pl.ds(pl.multiple_of(pid * stride, BF16_ROWS), n)]
                        dst = zones[i].at[k]
                    pltpu.make_async_remote_copy(
                        src_ref=src, dst_ref=dst, send_sem=sems[2 * i], recv_sem=sems[2 * i + 1],
                        device_id=pxyz, device_id_type=pl.DeviceIdType.MESH).start()
            token[...] = jnp.zeros_like(token)

        hbm = pl.BlockSpec(memory_space=pltpu.HBM)
        sem = pl.BlockSpec(memory_space=pltpu.SEMAPHORE)
        args = [pltpu.with_memory_space_constraint(a, pltpu.HBM) for a in list(arrays) + lands]
        res = pl.pallas_call(
            body, name=name + "_start",
            in_specs=[hbm] * (2 * n_arr) + ([pl.BlockSpec(memory_space=pl.ANY)] if has_after else []),
            out_specs=[sem] * (2 * n_arr) + [hbm] * (2 * n_arr) + [pl.BlockSpec(memory_space=pltpu.VMEM)],
            out_shape=[pltpu.SemaphoreType.DMA(())] * (2 * n_arr) + [pltpu.HBM(a.shape, a.dtype) for a in args]
            + [jax.ShapeDtypeStruct((8, LANES), F32)],
            input_output_aliases={i: 2 * n_arr + i for i in range(2 * n_arr)},
            compiler_params=pltpu.CompilerParams(has_side_effects=pltpu.SideEffectType.DATAFLOW_SIDE_EFFECTING),
        )(*args, *([after] if has_after else []))
        self.sems, self.thru, self.token = res[:2 * n_arr], res[2 * n_arr:4 * n_arr], res[-1]

    def _wait_all(self, zones, sems):
        myself = (lax.axis_index("x"), lax.axis_index("y"), lax.axis_index("c"))
        for i in range(self.n_arr):
            many = zones[i].at[pl.ds(0, self.in_flight * self.rows[i])] if self.gather else zones[i]
            all_of_them = pltpu.make_async_remote_copy(
                src_ref=many, dst_ref=many, send_sem=sems[2 * i], recv_sem=sems[2 * i + 1],
                device_id=myself, device_id_type=pl.DeviceIdType.MESH)
            all_of_them.wait_recv()
            all_of_them.wait_send()

    def forward(self, after):
        n_arr = self.n_arr

        def body(*refs):
            zones, sems = refs[n_arr:2 * n_arr], refs[2 * n_arr:4 * n_arr]
            new_sems = refs[4 * n_arr + 1:6 * n_arr + 1]
            self._wait_all(zones, sems)
            _, peers = _mesh_place()
            sibling, _ = peers[0]
            for i in range(n_arr):
                for mask in self.SAME_CORE_MASKS:
                    _, pid = peers[mask - 1]
                    block = zones[i].at[pl.ds(pl.multiple_of(pid * self.rows[i], BF16_ROWS), self.rows[i])]
                    pltpu.make_async_remote_copy(
                        src_ref=block, dst_ref=block, send_sem=new_sems[2 * i], recv_sem=new_sems[2 * i + 1],
                        device_id=sibling, device_id_type=pl.DeviceIdType.MESH).start()

        hbm = pl.BlockSpec(memory_space=pltpu.HBM)
        sem = pl.BlockSpec(memory_space=pltpu.SEMAPHORE)
        res = pl.pallas_call(
            body, name=self.name + "_forward",
            in_specs=[hbm] * (2 * n_arr) + [sem] * (2 * n_arr) + [pl.BlockSpec(memory_space=pl.ANY)],
            out_specs=[sem] * (2 * n_arr) + [hbm] * (2 * n_arr),
            out_shape=[pltpu.SemaphoreType.DMA(())] * (2 * n_arr) + [pltpu.HBM(a.shape, a.dtype) for a in self.thru],
            input_output_aliases={i: 2 * n_arr + i for i in range(2 * n_arr)},
            compiler_params=pltpu.CompilerParams(has_side_effects=pltpu.SideEffectType.DATAFLOW_SIDE_EFFECTING),
        )(*self.thru, *self.sems, after)
        self.sems, self.thru = res[:2 * n_arr], res[2 * n_arr:]
        self.two_level, self.in_flight = False, len(self.SAME_CORE_MASKS)

    def wait(self, after):
        n_arr = self.n_arr
        if self.two_level:
            self.forward(after)

        def body(*refs):
            self._wait_all(refs[n_arr:2 * n_arr], refs[2 * n_arr:4 * n_arr])

        hbm = pl.BlockSpec(memory_space=pltpu.HBM)
        sem = pl.BlockSpec(memory_space=pltpu.SEMAPHORE)
        res = pl.pallas_call(
            body, name=self.name + "_wait",
            in_specs=[hbm] * (2 * n_arr) + [sem] * (2 * n_arr) + [pl.BlockSpec(memory_space=pl.ANY)],
            out_specs=[hbm] * (2 * n_arr), out_shape=[pltpu.HBM(a.shape, a.dtype) for a in self.thru],
            input_output_aliases={i: i for i in range(2 * n_arr)},
            compiler_params=pltpu.CompilerParams(has_side_effects=pltpu.SideEffectType.DATAFLOW_SIDE_EFFECTING),
        )(*self.thru, *self.sems, after)
        return res[:n_arr], res[n_arr:]


def _my_slot():
    me = 4 * lax.axis_index("x") + 2 * lax.axis_index("y") + lax.axis_index("c")
    return me.astype(jnp.int32).reshape(1)


def _place_own(gathered, blocks, me, *, name):
    n = len(blocks)

    def body(me_ref, *refs):
        for b_ref, o_ref in zip(refs[:n], refs[2 * n:]):
            o_ref[...] = b_ref[...]

    res = pl.pallas_call(
        body, name=name, out_shape=[jax.ShapeDtypeStruct(g.shape, g.dtype) for g in gathered],
        grid_spec=pltpu.PrefetchScalarGridSpec(
            num_scalar_prefetch=1, grid=(1,),
            in_specs=[pl.BlockSpec(b.shape, lambda i, me_ref: (0, 0)) for b in blocks] + [pl.BlockSpec(memory_space=pl.ANY)] * n,
            out_specs=[pl.BlockSpec(b.shape, lambda i, me_ref: (me_ref[0], 0)) for b in blocks]),
        input_output_aliases={1 + n + i: i for i in range(n)}, compiler_params=_params(("arbitrary",)),
    )(me, *blocks, *gathered)
    return list(res)


def _elementwise_tiles(r, c):
    if r % 128 == 0:
        return 128, c
    return r, (256 if c % 256 == 0 else c)


def _adamw_math(g, w, m, v):
    nm = ADAM_B1 * m + (1.0 - ADAM_B1) * g
    nv = ADAM_B2 * v + (1.0 - ADAM_B2) * (g * g)
    m_hat = nm / (1.0 - ADAM_B1 ** ADAM_STEP)
    v_hat = nv / (1.0 - ADAM_B2 ** ADAM_STEP)
    return -ADAM_LR * (m_hat / (jnp.sqrt(v_hat) + ADAM_EPS) + ADAM_WD * w), nm, nv


def _sum_parts(me_ref, own_ref, p_ref, r):
    own = own_ref[...].astype(F32)
    g = None
    for d in range(N_DEV):
        k = jnp.bitwise_xor(me_ref[0], d)
        term = jnp.where(k == 0, own, p_ref[jnp.maximum(k, 1) - 1].astype(F32))
        g = term if g is None else g + term
    return g[0:r, :]


def _sum_adamw(me, sent, stride, parts, r, w=None, m=None, v=None, *, name):
    _, own_rows, cdim = parts.shape
    assert stride is None or stride == own_rows
    tc = 256 if cdim % 256 == 0 else cdim
    update = w is not None

    def body(me_ref, own_ref, p_ref, *refs):
        g = _sum_parts(me_ref, own_ref, p_ref, r)
        if update:
            w_ref, m_ref, v_ref, g_ref, d_ref, nm_ref, nv_ref = refs
            d_ref[...], nm_ref[...], nv_ref[...] = _adamw_math(g, w_ref[...], m_ref[...], v_ref[...])
        else:
            g_ref, = refs
        g_ref[...] = g

    blk = pl.BlockSpec((r, tc), lambda j, me_ref: (0, j))
    own_spec = pl.BlockSpec((own_rows, tc), (lambda j, me_ref: (0, j)) if stride is None else (lambda j, me_ref: (me_ref[0], j)))
    n_out = 4 if update else 1
    res = pl.pallas_call(
        body, name=name, out_shape=[jax.ShapeDtypeStruct((r, cdim), F32)] * n_out,
        grid_spec=pltpu.PrefetchScalarGridSpec(
            num_scalar_prefetch=1, grid=(cdim // tc,),
            in_specs=[own_spec, pl.BlockSpec((N_DEV - 1, own_rows, tc), lambda j, me_ref: (0, 0, j))]
            + ([blk] * 3 if update else []),
            out_specs=[blk] * n_out),
        compiler_params=_params(("parallel",)),
    )(me, sent, parts, *((w, m, v) if update else ()))
    return list(res)


def _adamw(g, w, m, v, *, name):
    r, cdim = w.shape
    tr, tc = _elementwise_tiles(r, cdim)

    def body(g_ref, w_ref, m_ref, v_ref, d_ref, nm_ref, nv_ref):
        d_ref[...], nm_ref[...], nv_ref[...] = _adamw_math(g_ref[...], w_ref[...], m_ref[...], v_ref[...])

    blk = pl.BlockSpec((tr, tc), lambda i, j: (i, j))
    return list(pl.pallas_call(
        body, name=name, grid=(r // tr, cdim // tc), in_specs=[blk] * 4,
        out_specs=[blk] * 3, out_shape=[jax.ShapeDtypeStruct((r, cdim), F32)] * 3,
        compiler_params=_params(("parallel", "parallel")),
    )(g, w, m, v))


SHARD_ROWS_P = {n: (FF_SHARD_P if 'ffn' in n else IN_SHARD_P if n == 'w_in' else None) for n in SHARDED}


def _to_exchange_layout(name, shard):
    t = shard.T if SHARD_AXIS[name] == 1 else shard
    pad = SHARD_ROWS_P[name]
    return t if pad is None else jnp.pad(t, ((0, pad - t.shape[0]), (0, 0)))


def _expand_w_in(wt):
    wt = wt.reshape(N_DEV, IN_SHARD_P, D)[:, :IN_SHARD].reshape(IN_W, D)
    o = Q_RANK + KV_RANK
    kr1, kr2 = wt[o:o + ROPE // 2], wt[o + ROPE // 2:o + ROPE]
    z = jnp.zeros((LANES - ROPE, D), wt.dtype)
    return jnp.concatenate([wt[:o], wt[o + ROPE:], kr1, kr2, z, -kr2, kr1, z], axis=0)


def _expand_w_uq(wt):
    w = wt.reshape(H, QK, Q_RANK)
    z = jnp.zeros((H, LANES - ROPE, Q_RANK), w.dtype)
    q1, q2 = w[:, NOPE:NOPE + ROPE // 2], w[:, NOPE + ROPE // 2:]
    return jnp.concatenate([w[:, :NOPE].reshape(H * NOPE, Q_RANK),
                            jnp.concatenate([q1, q2, z], axis=1).reshape(H * LANES, Q_RANK),
                            jnp.concatenate([-q2, q1, z], axis=1).reshape(H * LANES, Q_RANK)], axis=0)


def _layout_qk_gain(g):
    g = g.reshape(QK)
    g1, g2, z = g[NOPE:NOPE + ROPE // 2], g[NOPE + ROPE // 2:], jnp.zeros((LANES - ROPE,), g.dtype)
    return jnp.stack([g[:NOPE], jnp.concatenate([g1, g2, z]), jnp.concatenate([g2, g1, z])])


def _rep16(a):
    return jnp.repeat(a, SSM_GRP, axis=0)


def _layout_ssm_in(a_re, a_im, log_dt, b_re, b_im):
    b_r = jnp.transpose(b_re, (0, 2, 1)).reshape(SSM_G * SSM_GRP, SSM_P)
    b_i = jnp.transpose(b_im, (0, 2, 1)).reshape(SSM_G * SSM_GRP, SSM_P)
    ldt = jnp.broadcast_to(log_dt.reshape(SSM_G, 1), (SSM_G, SSM_P))
    return _rep16(a_re), _rep16(a_im), _rep16(ldt), b_r, b_i


def _block_diag_b(bb):
    eye = jnp.eye(SSM_PACK, dtype=bb.dtype)
    b5 = bb.reshape(SSM_G // SSM_PACK, SSM_PACK, SSM_GRP, 1, SSM_P) * eye[None, :, None, :, None]
    return b5.reshape(SSM_G // SSM_PACK, SSM_PACK * SSM_GRP, SSM_PACK * SSM_P)


def _block_diag_c(cc):
    eye = jnp.eye(SSM_PACK, dtype=cc.dtype)
    c5 = jnp.transpose(cc, (0, 2, 1)).reshape(SSM_G // SSM_PACK, SSM_PACK, SSM_P, 1, SSM_GRP) * eye[None, :, None, :, None]
    return c5.reshape(SSM_G // SSM_PACK, SSM_PACK * SSM_P, SSM_PACK * SSM_GRP)


def _time_perm(a, inverse=False):
    s, w = a.shape
    c = SCAN_CHUNKS
    if inverse:
        return jnp.transpose(a.reshape(s // c, c, w), (1, 0, 2)).reshape(s, w)
    return jnp.transpose(a.reshape(c, s // c, w), (1, 0, 2)).reshape(s, w)


class _Weights:
    def __init__(self, groups=(), landed=None, me=None):
        self.groups, self.landed, self.me = list(groups), dict(landed or {}), me

    def get(self, name, after):
        if name not in self.landed:
            names, exchange = next(g for g in self.groups if name in g[0])
            blocks, gathered = exchange.wait(after)
            self.landed.update(zip(names, _place_own(gathered, blocks, self.me, name="place_" + names[0])))
        return self.landed[name]

    def __getitem__(self, name):
        return self.landed[name]

    def prefetch(self, name, after):
        for names, exchange in self.groups:
            if name in names and exchange.two_level:
                exchange.forward(after)


def _ffn_gate_up(h, w_gt, w_ut, *, name, tm=1024, tn=1408):
    s, k = h.shape
    n = w_gt.shape[0]
    tm, tn = min(tm, s), _tile(n, tn)
    dims = (((1,), (1,)), ((), ()))

    def body(h_ref, wg_ref, wu_ref, g_ref, u_ref, a_ref):
        hb = h_ref[...].astype(BF16)
        gate = lax.dot_general(hb, wg_ref[...], dims, preferred_element_type=F32)
        up = lax.dot_general(hb, wu_ref[...], dims, preferred_element_type=F32)
        g_ref[...] = gate.astype(BF16)
        u_ref[...] = up.astype(BF16)
        a_ref[...] = _f_swiglu(gate, up)

    w_spec = pl.BlockSpec((tn, k), lambda j, i: (j, 0))
    o_spec = pl.BlockSpec((tm, tn), lambda j, i: (i, j))
    return pl.pallas_call(
        body, name=name, grid=(n // tn, s // tm), in_specs=[pl.BlockSpec((tm, k), lambda j, i: (i, 0)), w_spec, w_spec],
        out_specs=[o_spec] * 3, out_shape=[jax.ShapeDtypeStruct((s, n), BF16)] * 3,
        compiler_params=_params(("parallel", "parallel")),
    )(h, w_gt, w_ut)


def _ffn_dgate_dup(dx_out, w_d, gate, up, *, name, tm=512, tn=1408, deps=()):
    s, k = dx_out.shape
    n = w_d.shape[0]
    tm, tn = min(tm, s), _tile(n, tn)
    deps = [d for d in deps if d is not None]

    def body(dx_ref, wd_ref, g_ref, u_ref, *refs):
        dg_ref, du_ref = refs[len(deps):]
        dact = 0.5 * lax.dot_general(dx_ref[...].astype(BF16), wd_ref[...], (((1,), (1,)), ((), ())),
                                     preferred_element_type=F32)
        _, vjp = jax.vjp(_f_swiglu, g_ref[...].astype(F32), u_ref[...].astype(F32))
        dgate, dup = vjp(dact.astype(BF16))
        dg_ref[...] = dgate.astype(BF16)
        du_ref[...] = dup.astype(BF16)

    o_spec = pl.BlockSpec((tm, tn), lambda j, i: (i, j))
    return pl.pallas_call(
        body, name=name, grid=(n // tn, s // tm),
        in_specs=[pl.BlockSpec((tm, k), lambda j, i: (i, 0)), pl.BlockSpec((tn, k), lambda j, i: (j, 0)), o_spec, o_spec]
        + [pl.BlockSpec(d.shape, lambda j, i: (0, 0)) for d in deps],
        out_specs=[o_spec] * 2, out_shape=[jax.ShapeDtypeStruct((s, n), BF16)] * 2,
        compiler_params=_params(("parallel", "parallel")),
    )(dx_out, w_d, gate, up, *deps)


def _ffn_dh(dgate, dup, w_gt, w_ut, *, name, tm=512):
    s, k = dgate.shape
    n = w_gt.shape[1]
    tm = min(tm, s)

    def body(dg_ref, du_ref, wg_ref, wu_ref, o_ref):
        o_ref[...] = (jnp.dot(dg_ref[...], wg_ref[...], preferred_element_type=F32)
                      + jnp.dot(du_ref[...], wu_ref[...], preferred_element_type=F32)).astype(o_ref.dtype)

    a_spec = pl.BlockSpec((tm, k), lambda i: (i, 0))
    w_spec = pl.BlockSpec((k, n), lambda i: (0, 0))
    return pl.pallas_call(
        body, name=name, grid=(s // tm,), in_specs=[a_spec, a_spec, w_spec, w_spec],
        out_specs=pl.BlockSpec((tm, n), lambda i: (i, 0)), out_shape=jax.ShapeDtypeStruct((s, n), BF16),
        compiler_params=_params(("parallel",)),
    )(dgate, dup, w_gt, w_ut)


def _ffn_fwd(x, g, wc, tag, deps=(), prefetch=()):
    h = _rowwise(_f_norm, [x], [g], [(D, BF16)], name=tag + "_norm", deps=deps)[0]
    gate, up, act = _ffn_gate_up(h, wc.get(tag + '_w_gate', h), wc[tag + '_w_up'], name=tag + "_gate_up")
    for later in (tag + '_w_down',) + tuple(prefetch):
        wc.prefetch(later, gate)
    x_out = _mm(act, wc.get(tag + '_w_down', act), res=x, scale=0.5, name=tag + "_down")
    return x_out, (h, gate, up, act)


def _ffn_bwd(x, g, wc, saved, dx_out, tag, send, deps=()):
    h, gate, up, act = saved
    w_gt, w_ut, w_d = (wc.get(tag + n, h) for n in ('_w_gate', '_w_up', '_w_down'))
    d_d = _mm(act, dx_out, ta=True, scale=0.5, out_dtype=GRAD_DTYPE, name=tag + "_dwdown", deps=deps)
    token = send({tag + '_w_down': d_d})
    dgate, dup = _ffn_dgate_dup(dx_out, w_d, gate, up, name=tag + "_dgate_dup", deps=[token])
    d_gt = _mm(dgate, h, ta=True, out_dtype=GRAD_DTYPE, name=tag + "_dwgate")
    token = send({tag + '_w_gate': d_gt})
    d_ut = _mm(dup, h, ta=True, out_dtype=GRAD_DTYPE, name=tag + "_dwup", deps=[token])
    token = send({tag + '_w_up': d_ut})
    dh = _ffn_dh(dgate, dup, w_gt, w_ut, name=tag + "_dh")
    dx, dg = _rowwise_bwd(_f_norm, [x], [g], [dh], row_grads={0: F32}, const_grads=[0], adds={0: dx_out},
                          name=tag + "_norm_bwd", deps=[token])
    return dx, dg


def _local_step(x, mem, cos, sin, target, wc, ws, send, deps=(), send_small=None):
    gs = {}

    x1, sv1 = _ffn_fwd(x, ws['ffn1_norm'], wc, "ffn1", deps=deps, prefetch=('w_in',))

    h2 = _rowwise(_f_norm, [x1], [ws['mix_norm']], [(D, BF16)], name="mix_norm")[0]
    w_in_raw, w_uq_raw = wc.get('w_in', h2), wc.get('mla_w_uq', h2)
    w_in_e = _expand_w_in(w_in_raw)
    w_uq_e = _expand_w_uq(w_uq_raw)
    proj = _mm(h2, w_in_e, tb=True, name="w_in")
    c_q, c_kv = _rowwise(_f_prep1, [proj], [ws['q_norm'], ws['kv_norm']], [(Q_RANK, BF16), (KV_RANK, BF16)], name="mla_prep1")
    qall = _mm(c_q, w_uq_e, tb=True, out_dtype=BF16, name="w_uq")
    kv = _mm(c_kv, wc['mla_w_ukv'], tb=True, out_dtype=BF16, name="w_ukv")
    q, k, v = _prep2_fwd(qall, kv, proj, cos, sin, ws['qk_gq'], ws['qk_gk'])
    o_mla, lse = _attn_fwd(q, k, v)
    wc.prefetch('ffn2_w_gate', lse)

    u = proj[:, Q_RANK + KV_RANK:Q_RANK + KV_RANK + SSM_W]
    u_p = _time_perm(u)
    disc_in = [ws['ssm_lr'], ws['ssm_li'], ws['ssm_ldt'], ws['ssm_br'], ws['ssm_bi']]
    ar16, ai16, bbr, bbi = _rowwise(_f_disc, disc_in, [], [(SSM_P, F32)] * 4, name="s5_disc")
    a_r = ar16[::SSM_GRP].reshape(1, SSM_N)
    a_i = ai16[::SSM_GRP].reshape(1, SSM_N)
    bblk_r, bblk_i = _block_diag_b(bbr).astype(BF16), _block_diag_b(bbi).astype(BF16)
    cblk_r, cblk_i = _block_diag_c(ws['ssm_cr']).astype(BF16), _block_diag_c(-ws['ssm_ci']).astype(BF16)
    xr, xi, yc = _s5_scan(u_p, bblk_r, bblk_i, a_r, a_i, reverse=False, tb=False, readout=(cblk_r, cblk_i),
                          name="s5_scan_fwd")
    g_p = _rowwise(_f_s5_gelu, [yc, u_p], [ws['ssm_d']], [(SSM_W, F32)], name="s5_gelu")[0]
    z_p = _mm(g_p, wc['ssm_w_glu'], name="s5_glu")
    g_t, z_t = _time_perm(g_p, inverse=True), _time_perm(z_p, inverse=True)
    on_consts = [ws['ssm_b_glu'], ws['out_norm_mla'], ws['out_norm_ssm']]
    ycat = _rowwise(_f_outnorm, [o_mla, g_t, z_t], on_consts, [(D, BF16)], name="out_norm")[0]
    x2 = _mm(ycat, wc['w_o'], res=x1, name="w_o")

    hx = _rowwise(_f_norm, [x2], [ws['xattn_norm']], [(D, BF16)], name="xattn_norm")[0]
    xq = _mm(hx, wc['xattn_w_q'], name="xattn_q")
    mn = _rowwise(_f_norm, [mem], [ws['mem_norm']], [(D, BF16)], name="mem_norm")[0]
    kvm = _mm(mn, wc['xattn_w_kv'], name="xattn_kv")
    xkn, xv = _rowwise(_f_memk, [kvm], [ws['xattn_k_norm']], [(H * XH, BF16), (H * XH, BF16)], name="xattn_knorm")
    xo = _xattn_fwd(xq, xkn, xv, ws['xattn_q_norm'])
    x3 = _mm(xo, wc['xattn_w_o'], tb=True, res=x2, name="xattn_o")

    x4, sv2 = _ffn_fwd(x3, ws['ffn2_norm'], wc, "ffn2")

    def f_loss(yb, tb):
        err = yb - tb
        return err * (1.0 / D), jnp.broadcast_to(jnp.sum(jnp.sum(err * err, axis=1, keepdims=True), axis=0, keepdims=True) * (0.5 / D), (1, LANES))

    dx4, loss = _rowwise(f_loss, [x4, target], [], [(D, F32)], [(1, LANES)], name="loss")

    dx3, gs['ffn2_norm'] = _ffn_bwd(x3, ws['ffn2_norm'], wc, sv2, dx4, "ffn2", send)

    dxo = _mm(dx3, wc['xattn_w_o'], out_dtype=BF16, name="xattn_o_dx")
    send({'xattn_w_o': _mm(dx3, xo, ta=True, out_dtype=GRAD_DTYPE, name="xattn_o_dw")})
    dxq, dxkn, dxv, gs['xattn_q_norm'] = _xattn_bwd(xq, xkn, xv, ws['xattn_q_norm'], dxo)
    dkvm, gs['xattn_k_norm'] = _rowwise_bwd(_f_memk, [kvm], [ws['xattn_k_norm']], [dxkn, dxv], row_grads={0: BF16},
                                            const_grads=[0], name="xattn_knorm_bwd")
    send({'xattn_w_kv': _mm(mn, dkvm, ta=True, out_dtype=GRAD_DTYPE, name="xattn_kv_dw")})
    dmn = _mm(dkvm, wc['xattn_w_kv'], tb=True, out_dtype=BF16, name="xattn_kv_dx")
    gs['mem_norm'] = _rowwise_bwd(_f_norm, [mem], [ws['mem_norm']], [dmn], row_grads={}, const_grads=[0], name="mem_norm_bwd")[0]
    token = send({'xattn_w_q': _mm(hx, dxq, ta=True, out_dtype=GRAD_DTYPE, name="xattn_q_dw")})
    dhx = _mm(dxq, wc['xattn_w_q'], tb=True, out_dtype=BF16, name="xattn_q_dx")
    dx2, gs['xattn_norm'] = _rowwise_bwd(_f_norm, [x2], [ws['xattn_norm']], [dhx], row_grads={0: F32}, const_grads=[0],
                                         adds={0: dx3}, name="xattn_norm_bwd", deps=[token])

    dycat = _mm(dx2, wc['w_o'], tb=True, out_dtype=BF16, name="w_o_dx")
    send({'w_o': _mm(ycat, dx2, ta=True, out_dtype=GRAD_DTYPE, name="w_o_dw")})
    do_mla, dg_t, dz_t, gs['ssm_b_glu'], gs['out_norm_mla'], gs['out_norm_ssm'] = _rowwise_bwd(
        _f_outnorm, [o_mla, g_t, z_t], on_consts, [dycat], row_grads={0: F32, 1: F32, 2: BF16}, const_grads=[0, 1, 2],
        name="out_norm_bwd")

    dz_p, dg_p = _time_perm(dz_t), _time_perm(dg_t)
    send({'ssm_w_glu': _mm(g_p, dz_p, ta=True, out_dtype=GRAD_DTYPE, name="s5_glu_dw")})
    dg_p = _mm(dz_p, wc['ssm_w_glu'], tb=True, res=dg_p, name="s5_glu_dx")
    dyc, du_d, gs['ssm_d'] = _rowwise_bwd(_f_s5_gelu, [yc, u_p], [ws['ssm_d']], [dg_p], row_grads={0: BF16, 1: F32},
                                          const_grads=[0], name="s5_gelu_bwd")
    lam_r, lam_i = _s5_scan(dyc, cblk_r, cblk_i, a_r, -a_i, reverse=True, tb=True, name="s5_scan_bwd")
    du_p, d_bblk_r, d_bblk_i, d_cblk_r, d_cblk_i, d_ar, d_ai = _s5_grads(lam_r, lam_i, xr, xi, u_p, dyc, du_d,
                                                                        bblk_r, bblk_i)
    du = _time_perm(du_p, inverse=True)
    gs['ssm_cr'] = jax.linear_transpose(_block_diag_c, ws['ssm_cr'])(d_cblk_r)[0]
    gs['ssm_ci'] = -jax.linear_transpose(_block_diag_c, ws['ssm_ci'])(d_cblk_i)[0]
    d_bbr = jax.linear_transpose(_block_diag_b, bbr)(d_bblk_r)[0]
    d_bbi = jax.linear_transpose(_block_diag_b, bbi)(d_bblk_i)[0]
    d_ar16 = jnp.zeros((SSM_G * SSM_GRP, SSM_P), F32).at[::SSM_GRP].set(d_ar.reshape(SSM_G, SSM_P))
    d_ai16 = jnp.zeros((SSM_G * SSM_GRP, SSM_P), F32).at[::SSM_GRP].set(d_ai.reshape(SSM_G, SSM_P))
    gs['ssm_lr'], gs['ssm_li'], gs['ssm_ldt'], gs['ssm_br'], gs['ssm_bi'] = _rowwise_bwd(
        _f_disc, disc_in, [], [d_ar16, d_ai16, d_bbr, d_bbi], row_grads={i: F32 for i in range(5)}, const_grads=[],
        name="s5_disc_bwd")

    delta, do_b = _rowwise(_f_delta, [do_mla, o_mla], [], [(H * LANES, F32), (H * VD, BF16)], name="mla_delta")
    dq, dk, dv = _attn_bwd(q, k, v, do_b, lse, delta)
    dqall, dkv, dkr, dkrs, gs['qk_gq'], gs['qk_gk'] = _prep2_bwd(qall, kv, proj, cos, sin, ws['qk_gq'], ws['qk_gk'], dq, dk, dv)
    d_w_uq_e = _mm(dqall, c_q, ta=True, name="w_uq_dw")
    send({'mla_w_uq': jax.linear_transpose(_expand_w_uq, jax.ShapeDtypeStruct(w_uq_raw.shape, F32))(d_w_uq_e)[0]})
    dc_q = _mm(dqall, w_uq_e, out_dtype=BF16, name="w_uq_dx")
    send({'mla_w_ukv': _mm(dkv, c_kv, ta=True, out_dtype=GRAD_DTYPE, name="w_ukv_dw")})
    dc_kv = _mm(dkv, wc['mla_w_ukv'], out_dtype=BF16, name="w_ukv_dx")

    def f_prep1_bwd(pb, dcq, dckv, dub, dkrb, dkrsb, gq, gkv):
        _, vjp = jax.vjp(_f_prep1, pb[:, :Q_RANK + KV_RANK], gq, gkv)
        dpa, dgq, dgkv = vjp((dcq.astype(BF16), dckv.astype(BF16)))
        return jnp.concatenate([dpa, dub, dkrb, dkrsb], axis=-1), dgq, dgkv

    dproj, gs['q_norm'], gs['kv_norm'] = _rowwise(
        f_prep1_bwd, [proj, dc_q, dc_kv, du, dkr, dkrs], [ws['q_norm'], ws['kv_norm']], [(IN_WP, BF16)],
        [(1, Q_RANK), (1, KV_RANK)], name="mla_prep1_bwd")
    d_w_in_e = _mm(dproj, h2, ta=True, name="w_in_dw")
    token = send({'w_in': jax.linear_transpose(_expand_w_in, jax.ShapeDtypeStruct(w_in_raw.shape, F32))(d_w_in_e)[0]})
    dh2 = _mm(dproj, w_in_e, out_dtype=BF16, name="w_in_dx")
    dx1, gs['mix_norm'] = _rowwise_bwd(_f_norm, [x1], [ws['mix_norm']], [dh2], row_grads={0: F32}, const_grads=[0],
                                       adds={0: dx2}, name="mix_norm_bwd", deps=[token])

    token = send_small(gs, loss) if send_small is not None else None
    dx0, gs['ffn1_norm'] = _ffn_bwd(x, ws['ffn1_norm'], wc, sv1, dx1, "ffn1", send, deps=[token])
    return loss, dx0, gs


def _prep2_rows(qall, kv, proj, cos, sin):
    return [qall, kv, (proj, KR_BLOCK, LANES), (proj, KR_BLOCK + 1, LANES), cos, sin]


def _prep2_fwd(qall, kv, proj, cos, sin, gq, gk):
    return _rowwise(_f_prep2, _prep2_rows(qall, kv, proj, cos, sin), [gq, gk],
                    [(H * HQ, BF16), (H * HQ, BF16), (H * VD, BF16)], ts=512, name="mla_prep2")


def _prep2_bwd(qall, kv, proj, cos, sin, gq, gk, dq, dk, dv):
    return _rowwise_bwd(_f_prep2, _prep2_rows(qall, kv, proj, cos, sin), [gq, gk], [dq, dk, dv],
                        row_grads={0: BF16, 1: BF16, 2: F32, 3: F32}, const_grads=[0, 1], ts=256, name="mla_prep2_bwd")


def _rope_tables(pos):
    half = ROPE // 2
    inv = ROPE_THETA ** (-jnp.arange(half, dtype=F32) / half)
    ang = pos.astype(F32)[:, None] * inv[None, :]
    z = jnp.zeros((pos.shape[0], LANES - ROPE), F32)
    cos, sin = jnp.cos(ang), jnp.sin(ang)
    return jnp.concatenate([cos, cos, z], axis=-1), jnp.concatenate([sin, sin, z], axis=-1)


def _small_layout(p):
    lr, li, ldt, br, bi = _layout_ssm_in(p['ssm_a_re'], p['ssm_a_im'], p['ssm_log_dt'], p['ssm_b_re'], p['ssm_b_im'])
    return {
        'ffn1_norm': p['ffn1_norm'].reshape(1, D), 'mix_norm': p['mix_norm'].reshape(1, D),
        'q_norm': p['mla_q_norm'].reshape(1, Q_RANK), 'kv_norm': p['mla_kv_norm'].reshape(1, KV_RANK),
        'qk_gq': _layout_qk_gain(p['mla_qk_norm_q']), 'qk_gk': _layout_qk_gain(p['mla_qk_norm_k']),
        'ssm_lr': lr, 'ssm_li': li, 'ssm_ldt': ldt, 'ssm_br': br, 'ssm_bi': bi,
        'ssm_cr': p['ssm_c_re'], 'ssm_ci': p['ssm_c_im'], 'ssm_d': p['ssm_d'].reshape(1, SSM_W),
        'ssm_b_glu': p['ssm_b_glu'].reshape(1, SSM_W),
        'out_norm_mla': p['out_norm_mla'].reshape(1, SSM_W), 'out_norm_ssm': p['out_norm_ssm'].reshape(1, SSM_W),
        'xattn_norm': p['xattn_norm'].reshape(1, D), 'mem_norm': p['mem_norm'].reshape(1, D),
        'xattn_q_norm': p['xattn_q_norm'].reshape(1, XH), 'xattn_k_norm': p['xattn_k_norm'].reshape(1, XH),
        'ffn2_norm': p['ffn2_norm'].reshape(1, D),
    }


def _pack(arrs, rows):
    flat = jnp.concatenate([a.reshape(-1) for a in arrs])
    return jnp.pad(flat, (0, rows * D - flat.shape[0])).reshape(rows, D)


def _unpack(flat, shapes):
    flat = flat.reshape(-1)
    out, off = [], 0
    for sh in shapes:
        n = int(np.prod(sh))
        out.append(flat[off:off + n].reshape(sh))
        off += n
    return out


def kernel(x, mem, positions, ffn1_norm, ffn1_w_gate, ffn1_w_up, ffn1_w_down, mix_norm, w_in, mla_q_norm, mla_w_uq, mla_kv_norm, mla_w_ukv, mla_qk_norm_q, mla_qk_norm_k, ssm_a_re, ssm_a_im, ssm_log_dt, ssm_b_re, ssm_b_im, ssm_c_re, ssm_c_im, ssm_d, ssm_w_glu, ssm_b_glu, out_norm_mla, out_norm_ssm, w_o, xattn_norm, mem_norm, xattn_w_q, xattn_w_kv, xattn_q_norm, xattn_k_norm, xattn_w_o, ffn2_norm, ffn2_w_gate, ffn2_w_up, ffn2_w_down, loss_target, m_ffn1_norm, m_ffn1_w_gate, m_ffn1_w_up, m_ffn1_w_down, m_mix_norm, m_w_in, m_mla_q_norm, m_mla_w_uq, m_mla_kv_norm, m_mla_w_ukv, m_mla_qk_norm_q, m_mla_qk_norm_k, m_ssm_a_re, m_ssm_a_im, m_ssm_log_dt, m_ssm_b_re, m_ssm_b_im, m_ssm_c_re, m_ssm_c_im, m_ssm_d, m_ssm_w_glu, m_ssm_b_glu, m_out_norm_mla, m_out_norm_ssm, m_w_o, m_xattn_norm, m_mem_norm, m_xattn_w_q, m_xattn_w_kv, m_xattn_q_norm, m_xattn_k_norm, m_xattn_w_o, m_ffn2_norm, m_ffn2_w_gate, m_ffn2_w_up, m_ffn2_w_down, v_ffn1_norm, v_ffn1_w_gate, v_ffn1_w_up, v_ffn1_w_down, v_mix_norm, v_w_in, v_mla_q_norm, v_mla_w_uq, v_mla_kv_norm, v_mla_w_ukv, v_mla_qk_norm_q, v_mla_qk_norm_k, v_ssm_a_re, v_ssm_a_im, v_ssm_log_dt, v_ssm_b_re, v_ssm_b_im, v_ssm_c_re, v_ssm_c_im, v_ssm_d, v_ssm_w_glu, v_ssm_b_glu, v_out_norm_mla, v_out_norm_ssm, v_w_o, v_xattn_norm, v_mem_norm, v_xattn_w_q, v_xattn_w_kv, v_xattn_q_norm, v_xattn_k_norm, v_xattn_w_o, v_ffn2_norm, v_ffn2_w_gate, v_ffn2_w_up, v_ffn2_w_down):
    args = dict(locals())
    w = {n: args[n] for n in WEIGHTS}
    mom = {n: args['m_' + n] for n in WEIGHTS}
    var = {n: args['v_' + n] for n in WEIGHTS}
    return _step(x, mem, positions, loss_target, w, mom, var)


GATHER_GROUPS = [('ffn1_gu', ['ffn1_w_gate', 'ffn1_w_up']), ('ffn1_down', ['ffn1_w_down']),
                 ('mix', ['w_in', 'mla_w_uq', 'mla_w_ukv', 'ssm_w_glu', 'w_o', 'xattn_w_q', 'xattn_w_kv', 'xattn_w_o']),
                 ('ffn2', ['ffn2_w_gate', 'ffn2_w_up', 'ffn2_w_down'])]
SCATTER_GROUPS = [('ffn2_down', ['ffn2_w_down']), ('ffn2_gate', ['ffn2_w_gate']), ('ffn2_up', ['ffn2_w_up']),
                  ('xattn', ['xattn_w_o', 'xattn_w_kv', 'xattn_w_q']),
                  ('mix', ['w_o', 'ssm_w_glu', 'mla_w_uq', 'mla_w_ukv', 'w_in']),
                  ('ffn1_down', ['ffn1_w_down']), ('ffn1_gate', ['ffn1_w_gate']), ('ffn1_up', ['ffn1_w_up'])]


def _step(x, mem, positions, loss_target, w, mom, var):
    blocks = {n: _to_exchange_layout(n, w[n][0]).astype(BF16) for n in SHARDED}
    gathers, token = [], None
    for tag, names in GATHER_GROUPS:
        ex = _Exchange([blocks[n] for n in names], [blocks[n].shape[0] for n in names], gather=True,
                       name="gather_" + tag, after=token, two_level=True)
        gathers.append((names, ex))
        token = ex.token
    me = _my_slot()
    wc = _Weights(gathers, me=me)

    rows = {n: (blocks[n].shape[0], blocks[n].shape[0]) for n in SHARDED}
    ready, scatters = {}, []

    def send(grads):
        ready.update({n: g.astype(GRAD_DTYPE) for n, g in grads.items()})
        for tag, names in SCATTER_GROUPS:
            if all(n in ready for n in names) and not any(t == tag for t, _, _ in scatters):
                ex = _Exchange([ready[n] for n in names], [rows[n] for n in names], gather=False, name="scatter_" + tag)
                scatters.append((tag, names, ex))
                return ex.token
        return None

    small = {n: w[n][0] for n in SMALL}
    small_shapes = [small[n].shape for n in SMALL]
    n_small = sum(int(np.prod(sh)) for sh in small_shapes) + 1
    rows_small = -(-n_small // (8 * D)) * 8
    small_sent = []

    def send_small(gs, loss):
        known = dict(gs, ffn1_norm=jnp.zeros((1, D), F32))
        g_small = jax.linear_transpose(_small_layout, {n: jax.ShapeDtypeStruct(small[n].shape, F32) for n in SMALL})(known)[0]
        pack = _pack([g_small[n] for n in SMALL] + [loss[0, :1]], rows_small)
        small_sent.append(_Exchange([pack], [(None, rows_small)], gather=False, name="scatter_small"))
        return small_sent[0].token

    ws = _small_layout(small)
    cos, sin = _rope_tables(positions[0])
    loss, dx, gs = _local_step(x[0], mem[0], cos, sin, loss_target[0], wc, ws, send, deps=[token], send_small=send_small)
    pad8 = lambda a: jnp.pad(a.reshape(1, D), ((0, 7), (0, 0)))
    last_ex = _Exchange([pad8(gs['ffn1_norm'])], [(None, 8)], gather=False, name="scatter_last")

    out, after = {}, dx
    for _, names, ex in scatters:
        for n, sent, p in zip(names, *ex.wait(after)):
            r = w[n][0].shape[SHARD_AXIS[n]]
            if SHARD_AXIS[n] == 0:
                out[n] = _sum_adamw(me, sent, rows[n][0], p, r, w[n][0], mom[n][0], var[n][0], name="adamw_" + n)
            else:
                g = _sum_adamw(me, sent, rows[n][0], p, r, name="sum_" + n)[0].T
                out[n] = [g] + _adamw(g, w[n][0], mom[n][0], var[n][0], name="adamw_" + n)
        after = out[names[-1]][1]
    state = [_pack([t[n][0] for n in SMALL], rows_small) for t in (w, mom, var)]
    sent, p = small_sent[0].wait(after)
    small_out = _sum_adamw(me, sent[0], None, p[0], rows_small, *state, name="adamw_small")
    loss_total = small_out[0].reshape(-1)[n_small - 1]
    for n, vals in zip(SMALL, zip(*[_unpack(flat, small_shapes) for flat in small_out])):
        out[n] = vals
    sent, p = last_ex.wait(small_out[1])
    last_out = _sum_adamw(me, sent[0], None, p[0], 8, *[pad8(t['ffn1_norm'][0]) for t in (w, mom, var)], name="adamw_last")
    out['ffn1_norm'] = [o[0] for o in last_out]
    outs = [out[n][i][None] for i in range(4) for n in WEIGHTS]
    return (loss_total, dx[None], *outs)
```

```python
import math

import jax
import jax.numpy as jnp
import numpy as np
from jax import lax
from jax.experimental import pallas as pl
from jax.experimental.pallas import tpu as pltpu

F32 = jnp.float32
BF16 = jnp.bfloat16

N_DEV = 8
D = 1024
D_FF = 2752
D_FFP = 2816
MEM_LEN = 256
H = 4
Q_RANK, KV_RANK, NOPE, ROPE, VD = 384, 256, 128, 64, 128
QK = NOPE + ROPE
HQ = 2 * 128
SSM_W, SSM_G, SSM_GRP, SSM_P = 512, 32, 16, 64
SSM_N = SSM_G * SSM_P
SSM_PACK = 8
IN_W = 1216
IN_WP = 1408
XH = 128
EPS = 1e-6
LN2 = math.log(2.0)
ROPE_THETA = 10000.0
SCAN_CHUNKS = 8
SCAN_UNROLL = 8
ADAM_LR, ADAM_B1, ADAM_B2, ADAM_EPS, ADAM_WD, ADAM_STEP = 0.001, 0.9, 0.999, 1e-08, 0.01, 10

VMEM_LIMIT = 56 * 1024 * 1024
ACC_BYTES = 6 * 1024 * 1024
LANES = 128
BF16_ROWS = 16
GRAD_DTYPE = BF16
FF_SHARD = D_FF // N_DEV
FF_SHARD_P = 352
IN_SHARD = IN_W // N_DEV
IN_SHARD_P = 160

WEIGHTS = ['ffn1_norm', 'ffn1_w_gate', 'ffn1_w_up', 'ffn1_w_down', 'mix_norm', 'w_in', 'mla_q_norm', 'mla_w_uq',
           'mla_kv_norm', 'mla_w_ukv', 'mla_qk_norm_q', 'mla_qk_norm_k', 'ssm_a_re', 'ssm_a_im', 'ssm_log_dt',
           'ssm_b_re', 'ssm_b_im', 'ssm_c_re', 'ssm_c_im', 'ssm_d', 'ssm_w_glu', 'ssm_b_glu', 'out_norm_mla',
           'out_norm_ssm', 'w_o', 'xattn_norm', 'mem_norm', 'xattn_w_q', 'xattn_w_kv', 'xattn_q_norm',
           'xattn_k_norm', 'xattn_w_o', 'ffn2_norm', 'ffn2_w_gate', 'ffn2_w_up', 'ffn2_w_down']
SHARD_AXIS = {'ffn1_w_gate': 1, 'ffn1_w_up': 1, 'ffn1_w_down': 0, 'w_in': 1, 'mla_w_uq': 1, 'mla_w_ukv': 1,
              'ssm_w_glu': 0, 'w_o': 0, 'xattn_w_q': 0, 'xattn_w_kv': 0, 'xattn_w_o': 1,
              'ffn2_w_gate': 1, 'ffn2_w_up': 1, 'ffn2_w_down': 0}
SHARDED = [n for n in WEIGHTS if n in SHARD_AXIS]
SMALL = [n for n in WEIGHTS if n not in SHARD_AXIS]


def _params(sem=None):
    return pltpu.CompilerParams(dimension_semantics=sem, vmem_limit_bytes=VMEM_LIMIT)


def _tile(n, cap):
    if n <= cap:
        return n
    best = n
    for t in range(LANES, cap + 1, LANES):
        if n % t == 0:
            best = t
    return best


def _mm(a, b, *, ta=False, tb=False, out_dtype=F32, res=None, scale=1.0, name, tm_cap=1024, tn_cap=1408, tk_cap=2816,
        deps=()):
    m, k = (a.shape[1], a.shape[0]) if ta else a.shape
    k2, n = (b.shape[1], b.shape[0]) if tb else b.shape
    assert k == k2, (a.shape, b.shape, ta, tb)
    if ta:
        tk_cap = min(tk_cap, 1024)
        tm_cap = 1408
    tm, tn, tk = _tile(m, tm_cap), _tile(n, tn_cap), _tile(k, tk_cap)
    if tm * tn * 4 > ACC_BYTES:
        tn = _tile(n, max(LANES, ACC_BYTES // (4 * tm) // LANES * LANES))
    nk = k // tk
    dims = (((0 if ta else 1,), (1 if tb else 0,)), ((), ()))
    has_res = res is not None

    deps = [d for d in deps if d is not None]

    def body(*refs):
        a_ref, b_ref = refs[:2]
        r_ref = refs[2] if has_res else None
        o_ref, acc_ref = refs[-2:]
        kk = pl.program_id(2)

        @pl.when(kk == 0)
        def _():
            acc_ref[...] = jnp.zeros_like(acc_ref)

        acc_ref[...] += lax.dot_general(a_ref[...].astype(BF16), b_ref[...].astype(BF16), dims,
                                        preferred_element_type=F32)

        @pl.when(kk == nk - 1)
        def _():
            out = acc_ref[...]
            if scale != 1.0:
                out = out * scale
            if has_res:
                out = out + r_ref[...].astype(F32)
            o_ref[...] = out.astype(o_ref.dtype)

    a_spec = pl.BlockSpec((tk, tm), lambda i, j, kk: (kk, i)) if ta else pl.BlockSpec((tm, tk), lambda i, j, kk: (i, kk))
    b_spec = pl.BlockSpec((tn, tk), lambda i, j, kk: (j, kk)) if tb else pl.BlockSpec((tk, tn), lambda i, j, kk: (kk, j))
    o_spec = pl.BlockSpec((tm, tn), lambda i, j, kk: (i, j))
    in_specs = [a_spec, b_spec] + ([o_spec] if has_res else []) + [pl.BlockSpec(d.shape, lambda i, j, kk: (0, 0)) for d in deps]
    args = (a, b) + ((res,) if has_res else ()) + tuple(deps)
    return pl.pallas_call(
        body, name=name, grid=(m // tm, n // tn, nk), in_specs=in_specs, out_specs=o_spec,
        out_shape=jax.ShapeDtypeStruct((m, n), out_dtype), scratch_shapes=[pltpu.VMEM((tm, tn), F32)],
        compiler_params=_params(("parallel", "parallel", "arbitrary")),
    )(*args)


def _rowwise(fn, rows, consts, outs, accs=(), *, ts=1024, name, deps=()):
    rows = [r if isinstance(r, tuple) else (r, 0, r.shape[1]) for r in rows]
    s = rows[0][0].shape[0]
    ts = min(ts, s)
    assert s % ts == 0
    n_rows, n_consts, n_outs = len(rows), len(consts), len(outs)
    deps = [d for d in deps if d is not None]
    consts = list(consts) + deps

    def body(*refs):
        ins = [r[...] for r in refs[:n_rows + n_consts]]
        res = fn(*ins)
        res = tuple(res) if isinstance(res, (tuple, list)) else (res,)
        out_refs = refs[n_rows + len(consts):]
        for o_ref, val in zip(out_refs[:n_outs], res[:n_outs]):
            o_ref[...] = val.astype(o_ref.dtype)
        if accs:
            first = pl.program_id(0) == 0

            @pl.when(first)
            def _():
                for a_ref, val in zip(out_refs[n_outs:], res[n_outs:]):
                    a_ref[...] = val.astype(F32)

            @pl.when(jnp.logical_not(first))
            def _():
                for a_ref, val in zip(out_refs[n_outs:], res[n_outs:]):
                    a_ref[...] += val.astype(F32)

    in_specs = [pl.BlockSpec((ts, width), lambda i, cb=cb: (i, cb)) for _, cb, width in rows]
    in_specs += [pl.BlockSpec(c.shape, lambda i: (0, 0)) for c in consts]
    out_specs = [pl.BlockSpec((ts, w), lambda i: (i, 0)) for w, _ in outs]
    out_specs += [pl.BlockSpec(tuple(sh), lambda i: (0, 0)) for sh in accs]
    out_shape = [jax.ShapeDtypeStruct((s, w), dt) for w, dt in outs]
    out_shape += [jax.ShapeDtypeStruct(tuple(sh), F32) for sh in accs]
    res = pl.pallas_call(
        body, name=name, grid=(s // ts,), in_specs=in_specs, out_specs=out_specs, out_shape=out_shape,
        compiler_params=_params(("arbitrary",)),
    )(*[a for a, _, _ in rows], *consts)
    return res


def _rowwise_bwd(f, rows, consts, cts, *, row_grads, const_grads, adds=None, ts=1024, name, deps=()):
    adds = adds or {}
    n_rows, n_consts, n_cts = len(rows), len(consts), len(cts)
    add_keys = sorted(adds)
    rg = sorted(row_grads)
    cg = sorted(const_grads)

    def fn(*args):
        r = args[:n_rows]
        c = args[n_rows:n_rows + n_consts]
        ct = args[n_rows + n_consts:n_rows + n_consts + n_cts]
        extra = args[n_rows + n_consts + n_cts:]
        outs, vjp = jax.vjp(f, *r, *c)
        outs = tuple(outs) if isinstance(outs, (tuple, list)) else (outs,)
        cot = tuple(g.astype(o.dtype) for g, o in zip(ct, outs))
        grads = vjp(cot if len(cot) > 1 else cot[0])
        res = []
        for i in rg:
            g = grads[i].astype(F32)
            if i in adds:
                g = g + extra[add_keys.index(i)].astype(F32)
            res.append(g)
        for i in cg:
            res.append(grads[n_rows + i])
        return tuple(res)

    rows_all = list(rows) + list(cts) + [adds[i] for i in add_keys]
    def fn2(*args):
        nr = len(rows_all)
        rr, cc = args[:nr], args[nr:]
        return fn(*rr[:n_rows], *cc, *rr[n_rows:])

    outs = [(rows[i][2] if isinstance(rows[i], tuple) else rows[i].shape[1], row_grads[i]) for i in rg]
    accs = [consts[i].shape for i in cg]
    return _rowwise(fn2, rows_all, list(consts), outs, accs, ts=ts, name=name, deps=deps)


def _rms(x, g):
    xf = x.astype(F32)
    return xf * lax.rsqrt(jnp.mean(xf * xf, axis=-1, keepdims=True) + EPS) * g.astype(F32)


def _sigmoid(x):
    return 1.0 / (1.0 + jnp.exp(-x))


def _f_norm(x, g):
    return _rms(x, g).astype(BF16)


def _f_swiglu(gate, up):
    gate, up = gate.astype(F32), up.astype(F32)
    return (gate * _sigmoid(gate) * up).astype(BF16)


def _f_prep1(proj, gq, gkv):
    return _rms(proj[:, :Q_RANK], gq).astype(BF16), _rms(proj[:, Q_RANK:Q_RANK + KV_RANK], gkv).astype(BF16)


KR_BLOCK = (Q_RANK + KV_RANK + SSM_W) // LANES


def _f_prep2(qall, kv, kr, krs, cos, sin, gq, gk):
    kr, krs = kr.astype(F32), krs.astype(F32)
    k_rot = kr * gk[1:2] * cos + krs * gk[2:3] * sin
    k_ss = jnp.sum(kr * kr, axis=-1, keepdims=True)
    q_scale = QK ** -0.5 / LN2
    qs, ks, vs = [], [], []
    for h in range(H):
        qn = qall[:, h * LANES:(h + 1) * LANES].astype(F32)
        qr = qall[:, (H + h) * LANES:(H + h + 1) * LANES].astype(F32)
        qrs = qall[:, (2 * H + h) * LANES:(2 * H + h + 1) * LANES].astype(F32)
        rstd = lax.rsqrt((jnp.sum(qn * qn, axis=-1, keepdims=True) + jnp.sum(qr * qr, axis=-1, keepdims=True)) / QK + EPS)
        rstd = rstd * q_scale
        qs += [qn * gq[0:1] * rstd, (qr * gq[1:2] * cos + qrs * gq[2:3] * sin) * rstd]
        kn = kv[:, 2 * h * LANES:(2 * h + 1) * LANES].astype(F32)
        rstd_k = lax.rsqrt((jnp.sum(kn * kn, axis=-1, keepdims=True) + k_ss) / QK + EPS)
        ks += [kn * gk[0:1] * rstd_k, k_rot * rstd_k]
        vs.append(kv[:, (2 * h + 1) * LANES:(2 * h + 2) * LANES])
    return (jnp.concatenate(qs, axis=-1).astype(BF16), jnp.concatenate(ks, axis=-1).astype(BF16),
            jnp.concatenate(vs, axis=-1).astype(BF16))


def _gelu(x):
    return 0.5 * x * (1.0 + jnp.tanh(math.sqrt(2.0 / math.pi) * (x + 0.044715 * (x * x * x))))


def _f_s5_gelu(yc, u, d):
    return _gelu(yc.astype(F32) + d * u.astype(F32))


def _f_outnorm(o_mla, g, z, b_glu, g_om, g_os):
    y_ssm = g * _sigmoid(z + b_glu)
    return jnp.concatenate([_rms(o_mla, g_om), _rms(y_ssm, g_os)], axis=-1).astype(BF16)


def _f_memk(kvm, gk):
    ks = [_rms(kvm[:, h * XH:(h + 1) * XH], gk) for h in range(H)]
    return jnp.concatenate(ks, axis=-1).astype(BF16), kvm[:, H * XH:].astype(BF16)


def _f_disc(lr, li, log_dt, br, bi):
    dt = jnp.exp(log_dt)
    decay = jnp.exp(lr * dt)
    ar = decay * jnp.cos(li * dt)
    ai = decay * jnp.sin(li * dt)
    den = lr * lr + li * li
    nr = ar - 1.0
    coef_r = (nr * lr + ai * li) / den
    coef_i = (ai * lr - nr * li) / den
    return ar, ai, coef_r * br - coef_i * bi, coef_r * bi + coef_i * br


def _causal_mask(i, j, tq, tk):
    qpos = i * tq + lax.broadcasted_iota(jnp.int32, (tq, tk), 0)
    kpos = j * tk + lax.broadcasted_iota(jnp.int32, (tq, tk), 1)
    return qpos >= kpos


def _attn_fwd(q, k, v, *, t=512):
    s = q.shape[0]
    t = min(t, s)
    nb = s // t

    def body(q_ref, k_ref, v_ref, o_ref, lse_ref, m_sc, l_sc, acc_sc):
        i, j = pl.program_id(1), pl.program_id(2)

        @pl.when(j == 0)
        def _():
            m_sc[...] = jnp.full_like(m_sc, -jnp.inf)
            l_sc[...] = jnp.zeros_like(l_sc)
            acc_sc[...] = jnp.zeros_like(acc_sc)

        def block(diagonal):
            sc = lax.dot_general(q_ref[...], k_ref[...], (((1,), (1,)), ((), ())), preferred_element_type=F32)
            if diagonal:
                sc = jnp.where(_causal_mask(i, j, t, t), sc, -jnp.inf)
            m_old = m_sc[...]
            m_new = jnp.maximum(m_old, jnp.max(sc, axis=-1, keepdims=True))
            p = jnp.exp2(sc - m_new)
            alpha = jnp.exp2(m_old - m_new)
            l_sc[...] = alpha * l_sc[...] + jnp.sum(p, axis=-1, keepdims=True)
            acc_sc[...] = alpha * acc_sc[...] + jnp.dot(p.astype(BF16), v_ref[...], preferred_element_type=F32)
            m_sc[...] = m_new

        pl.when(j < i)(lambda: block(False))

        @pl.when(j == i)
        def _():
            block(True)
            o_ref[...] = acc_sc[...] / l_sc[...]
            lse_ref[...] = jnp.broadcast_to(m_sc[...] + jnp.log2(l_sc[...]), lse_ref.shape)

    kv_map = lambda h, i, j: (jnp.minimum(j, i), h)
    return pl.pallas_call(
        body, name="mla_attn_fwd", grid=(H, nb, nb),
        in_specs=[pl.BlockSpec((t, HQ), lambda h, i, j: (i, h)), pl.BlockSpec((t, HQ), kv_map),
                  pl.BlockSpec((t, VD), kv_map)],
        out_specs=[pl.BlockSpec((t, VD), lambda h, i, j: (i, h)), pl.BlockSpec((t, LANES), lambda h, i, j: (i, h))],
        out_shape=[jax.ShapeDtypeStruct((s, H * VD), F32), jax.ShapeDtypeStruct((s, H * LANES), F32)],
        scratch_shapes=[pltpu.VMEM((t, 1), F32), pltpu.VMEM((t, 1), F32), pltpu.VMEM((t, VD), F32)],
        compiler_params=_params(("parallel", "parallel", "arbitrary")),
    )(q, k, v)


def _attn_probs(q_ref, k_ref, v_ref, do_ref, lse_ref, dl_ref, i, j, t, diagonal):
    sc = lax.dot_general(q_ref[...], k_ref[...], (((1,), (1,)), ((), ())), preferred_element_type=F32)
    p = jnp.exp2(sc - jnp.tile(lse_ref[...], (1, t // LANES)))
    if diagonal:
        p = jnp.where(_causal_mask(i, j, t, t), p, 0.0)
    dp = lax.dot_general(do_ref[...], v_ref[...], (((1,), (1,)), ((), ())), preferred_element_type=F32)
    ds = p * (dp - jnp.tile(dl_ref[...], (1, t // LANES)))
    return p, ds


def _attn_bwd(q, k, v, do, lse, delta, *, t=512):
    s = q.shape[0]
    t = min(t, s)
    nb = s // t

    def body(q_ref, k_ref, v_ref, do_ref, lse_ref, dl_ref, dq_ref, dk_ref, dv_ref, dk_sc, dv_sc):
        j, i = pl.program_id(1), pl.program_id(2)

        @pl.when(jnp.logical_and(i == 0, j == 0))
        def _():
            dq_ref[...] = jnp.zeros_like(dq_ref)

        @pl.when(i == 0)
        def _():
            dk_sc[...] = jnp.zeros_like(dk_sc)
            dv_sc[...] = jnp.zeros_like(dv_sc)

        def block(diagonal):
            p, ds = _attn_probs(q_ref, k_ref, v_ref, do_ref, lse_ref, dl_ref, i, j, t, diagonal)
            dsb = ds.astype(BF16)
            dv_sc[...] += lax.dot_general(p.astype(BF16), do_ref[...], (((0,), (0,)), ((), ())), preferred_element_type=F32)
            dk_sc[...] += lax.dot_general(dsb, q_ref[...], (((0,), (0,)), ((), ())), preferred_element_type=F32)
            rows = pl.ds(pl.multiple_of(i * t, t), t)
            dq_ref[rows, :] += jnp.dot(dsb, k_ref[...], preferred_element_type=F32)

        pl.when(i > j)(lambda: block(False))
        pl.when(i == j)(lambda: block(True))

        @pl.when(i == nb - 1)
        def _():
            dk_ref[...] = (dk_sc[...] * LN2).astype(dk_ref.dtype)
            dv_ref[...] = dv_sc[...].astype(dv_ref.dtype)

        @pl.when(jnp.logical_and(i == nb - 1, j == nb - 1))
        def _():
            dq_ref[...] = dq_ref[...] * LN2

    q_map = lambda h, j, i: (jnp.maximum(i, j), h)
    kv_map = lambda h, j, i: (j, h)
    dq, dk, dv = pl.pallas_call(
        body, name="mla_attn_bwd", grid=(H, nb, nb),
        in_specs=[pl.BlockSpec((t, HQ), q_map), pl.BlockSpec((t, HQ), kv_map), pl.BlockSpec((t, VD), kv_map),
                  pl.BlockSpec((t, VD), q_map), pl.BlockSpec((t, LANES), q_map), pl.BlockSpec((t, LANES), q_map)],
        out_specs=[pl.BlockSpec((s, HQ), lambda h, j, i: (0, h)), pl.BlockSpec((t, HQ), kv_map), pl.BlockSpec((t, VD), kv_map)],
        out_shape=[jax.ShapeDtypeStruct((s, H * HQ), F32), jax.ShapeDtypeStruct((s, H * HQ), BF16),
                   jax.ShapeDtypeStruct((s, H * VD), BF16)],
        scratch_shapes=[pltpu.VMEM((t, HQ), F32), pltpu.VMEM((t, VD), F32)],
        compiler_params=_params(("parallel", "arbitrary", "arbitrary")),
    )(q, k, v, do, lse, delta)
    return dq, dk, dv


def _f_delta(do, o):
    prod = do.astype(F32) * o.astype(F32)
    parts = [jnp.broadcast_to(jnp.sum(prod[:, h * VD:(h + 1) * VD], axis=-1, keepdims=True), (do.shape[0], LANES))
             for h in range(H)]
    return jnp.concatenate(parts, axis=-1), do.astype(BF16)


def _xattn_head(qh, kh, gq):
    qn = _rms(qh, gq) * (XH ** -0.5)
    sc = lax.dot_general(qn.astype(BF16), kh, (((1,), (1,)), ((), ())), preferred_element_type=F32)
    sc = sc - jnp.max(sc, axis=-1, keepdims=True)
    e = jnp.exp(sc)
    return qn, e / jnp.sum(e, axis=-1, keepdims=True)


def _xattn_fwd(q, kn, v, gq, *, ts=1024):
    def fn(qb, knb, vb, g):
        outs = []
        for h in range(H):
            sl = slice(h * XH, (h + 1) * XH)
            _, p = _xattn_head(qb[:, sl], knb[:, sl], g)
            outs.append(jnp.dot(p.astype(BF16), vb[:, sl], preferred_element_type=F32))
        return (jnp.concatenate(outs, axis=-1),)

    return _rowwise(fn, [q], [kn, v, gq], [(H * XH, BF16)], ts=ts, name="xattn_fwd")[0]


def _xattn_bwd(q, kn, v, gq, do, *, ts=1024):
    def fn(qb, dob, knb, vb, g):
        dqs, dks, dvs = [], [], []
        dg = jnp.zeros((1, XH), F32)
        for h in range(H):
            sl = slice(h * XH, (h + 1) * XH)
            qh, kh, vh, doh = qb[:, sl], knb[:, sl], vb[:, sl], dob[:, sl].astype(BF16)
            qn, p = _xattn_head(qh, kh, g)
            dp = lax.dot_general(doh, vh, (((1,), (1,)), ((), ())), preferred_element_type=F32)
            dvs.append(lax.dot_general(p.astype(BF16), doh, (((0,), (0,)), ((), ())), preferred_element_type=F32))
            ds = (p * (dp - jnp.sum(dp * p, axis=-1, keepdims=True))).astype(BF16)
            dqn = jnp.dot(ds, kh, preferred_element_type=F32)
            dks.append(lax.dot_general(ds, qn.astype(BF16), (((0,), (0,)), ((), ())), preferred_element_type=F32))
            _, vjp_n = jax.vjp(lambda a, b: _rms(a, b) * (XH ** -0.5), qh, g)
            dqh, dgh = vjp_n(dqn)
            dqs.append(dqh)
            dg = dg + dgh
        return (jnp.concatenate(dqs, axis=-1), jnp.concatenate(dks, axis=-1), jnp.concatenate(dvs, axis=-1), dg)

    return _rowwise(fn, [q, do], [kn, v, gq], [(H * XH, BF16)], [kn.shape, v.shape, gq.shape], ts=ts, name="xattn_bwd")


def _cmul(ar, ai, xr, xi):
    return ar * xr - ai * xi, ar * xi + ai * xr


def _scan_in_place(xr_ref, xi_ref, ar, ai, *, reverse):
    s, cw = xr_ref.shape
    c = SCAN_CHUNKS
    tt = s // c
    a_r = jnp.broadcast_to(ar, (c, cw))
    a_i = jnp.broadcast_to(ai, (c, cw))
    zero = jnp.zeros((c, cw), F32)

    def row(step):
        t = (tt - 1 - step) if reverse else step
        return pl.ds(pl.multiple_of(t * c, c), c)

    def local(step, carry):
        sr, si, qr, qi = carry
        r = row(step)
        nr, ni = _cmul(a_r, a_i, sr, si)
        nr, ni = nr + xr_ref[r, :], ni + xi_ref[r, :]
        xr_ref[r, :] = nr
        xi_ref[r, :] = ni
        return (nr, ni) + _cmul(a_r, a_i, qr, qi)

    end_r, end_i, pr, pi = lax.fori_loop(0, tt, local, (zero, zero, jnp.ones((c, cw), F32), zero), unroll=SCAN_UNROLL)

    rows_id = lax.broadcasted_iota(jnp.int32, (c, cw), 0)
    car_r, car_i = zero, zero
    cur_r, cur_i = jnp.zeros((1, cw), F32), jnp.zeros((1, cw), F32)
    order = range(c - 1, -1, -1) if reverse else range(c)
    for kk in order:
        car_r = jnp.where(rows_id == kk, cur_r, car_r)
        car_i = jnp.where(rows_id == kk, cur_i, car_i)
        nr, ni = _cmul(pr[0:1], pi[0:1], cur_r, cur_i)
        cur_r = nr + end_r[kk:kk + 1]
        cur_i = ni + end_i[kk:kk + 1]

    def fix(step, carry):
        qr, qi = _cmul(a_r, a_i, *carry)
        r = row(step)
        dr, di = _cmul(qr, qi, car_r, car_i)
        xr_ref[r, :] += dr
        xi_ref[r, :] += di
        return qr, qi

    lax.fori_loop(0, tt, fix, (jnp.ones((c, cw), F32), zero), unroll=SCAN_UNROLL)


S5_ROWS = 1024


def _s5_scan(v, w_r, w_i, ar, ai, *, reverse, tb, readout=None, name):
    s = v.shape[0]
    g = w_r.shape[0]
    nv, ns = SSM_PACK * SSM_GRP, SSM_PACK * SSM_P
    rows = min(S5_ROWS, s)
    dims = (((1,), (1 if tb else 0,)), ((), ()))
    n_w = 2 if readout is None else 4

    def body(v_ref, ar_ref, ai_ref, *refs):
        w = [r[...] for r in refs[:n_w]]
        xr_ref, xi_ref = refs[n_w:n_w + 2]
        for r0 in range(0, s, rows):
            vb = v_ref[r0:r0 + rows, :].astype(BF16)
            xr_ref[r0:r0 + rows, :] = lax.dot_general(vb, w[0], dims, preferred_element_type=F32)
            xi_ref[r0:r0 + rows, :] = lax.dot_general(vb, w[1], dims, preferred_element_type=F32)
        _scan_in_place(xr_ref, xi_ref, ar_ref[...], ai_ref[...], reverse=reverse)
        if readout is not None:
            y_ref = refs[n_w + 2]
            for r0 in range(0, s, rows):
                y_ref[r0:r0 + rows, :] = (
                    jnp.dot(xr_ref[r0:r0 + rows, :].astype(BF16), w[2], preferred_element_type=F32)
                    + jnp.dot(xi_ref[r0:r0 + rows, :].astype(BF16), w[3], preferred_element_type=F32))

    col = lambda j: (0, j)
    w_spec = lambda a: pl.BlockSpec((None,) + a.shape[1:], lambda j: (j, 0, 0))
    weights = [w_r, w_i] + (list(readout) if readout is not None else [])
    out_specs = [pl.BlockSpec((s, ns), col)] * 2 + ([pl.BlockSpec((s, nv), col)] if readout is not None else [])
    out_shape = [jax.ShapeDtypeStruct((s, g * ns), F32)] * 2 + (
        [jax.ShapeDtypeStruct((s, g * nv), F32)] if readout is not None else [])
    return pl.pallas_call(
        body, name=name, grid=(g,),
        in_specs=[pl.BlockSpec((s, nv), col), pl.BlockSpec((1, ns), col), pl.BlockSpec((1, ns), col)] + [w_spec(a) for a in weights],
        out_specs=out_specs, out_shape=out_shape, compiler_params=_params(("parallel",)),
    )(v, ar, ai, *weights)


def _s5_grads(lam_r, lam_i, xr, xi, u, dyc, du_d, b_r, b_i):
    s = u.shape[0]
    g = b_r.shape[0]
    nv, ns, c = SSM_PACK * SSM_GRP, SSM_PACK * SSM_P, SCAN_CHUNKS
    rows = min(S5_ROWS, s)
    slabs = rows // c
    last_slab = s // c - 1
    nt = (((1,), (1,)), ((), ()))
    tn = (((0,), (0,)), ((), ()))

    def body(lr_ref, li_ref, xr_ref, xi_ref, pr_ref, pi_ref, u_ref, dy_ref, dud_ref, br_ref, bi_ref,
             du_ref, dbr_ref, dbi_ref, dcr_ref, dci_ref, dar_ref, dai_ref):
        first = pl.program_id(1) == 0
        l_r, l_i, x_r, x_i = lr_ref[...], li_ref[...], xr_ref[...], xi_ref[...]
        lrb, lib = l_r.astype(BF16), l_i.astype(BF16)
        du_ref[...] = (dud_ref[...] + lax.dot_general(lrb, br_ref[...], nt, preferred_element_type=F32)
                       + lax.dot_general(lib, bi_ref[...], nt, preferred_element_type=F32))
        ub, dyb = u_ref[...].astype(BF16), dy_ref[...].astype(BF16)
        rows_id = lax.broadcasted_iota(jnp.int32, (c, ns), 0)

        def before(p_ref, x):
            p = p_ref[...]
            p = jnp.where(first, jnp.where(rows_id == 0, 0.0, pltpu.roll(p, 1, 0)), p)
            return jnp.concatenate([p, x[:rows - c]], axis=0)

        xp_r, xp_i = before(pr_ref, x_r), before(pi_ref, x_i)
        parts = (lax.dot_general(ub, lrb, tn, preferred_element_type=F32),
                 lax.dot_general(ub, lib, tn, preferred_element_type=F32),
                 lax.dot_general(x_r.astype(BF16), dyb, tn, preferred_element_type=F32),
                 lax.dot_general(x_i.astype(BF16), dyb, tn, preferred_element_type=F32),
                 jnp.sum(l_r * xp_r + l_i * xp_i, axis=0, keepdims=True),
                 jnp.sum(l_i * xp_r - l_r * xp_i, axis=0, keepdims=True))
        accs = (dbr_ref, dbi_ref, dcr_ref, dci_ref, dar_ref, dai_ref)

        @pl.when(first)
        def _():
            for a_ref, val in zip(accs, parts):
                a_ref[...] = val

        @pl.when(jnp.logical_not(first))
        def _():
            for a_ref, val in zip(accs, parts):
                a_ref[...] += val

    state = pl.BlockSpec((rows, ns), lambda j, k: (k, j))
    chan = pl.BlockSpec((rows, nv), lambda j, k: (k, j))
    slab = pl.BlockSpec((c, ns), lambda j, k: (jnp.where(k == 0, last_slab, k * slabs - 1), j))
    per_b = pl.BlockSpec((None, nv, ns), lambda j, k: (j, 0, 0))
    per_c = pl.BlockSpec((None, ns, nv), lambda j, k: (j, 0, 0))
    per_a = pl.BlockSpec((1, ns), lambda j, k: (0, j))
    return pl.pallas_call(
        body, name="s5_grads", grid=(g, s // rows),
        in_specs=[state, state, state, state, slab, slab, chan, chan, chan, per_b, per_b],
        out_specs=[chan, per_b, per_b, per_c, per_c, per_a, per_a],
        out_shape=[jax.ShapeDtypeStruct((s, g * nv), F32), jax.ShapeDtypeStruct((g, nv, ns), F32),
                   jax.ShapeDtypeStruct((g, nv, ns), F32), jax.ShapeDtypeStruct((g, ns, nv), F32),
                   jax.ShapeDtypeStruct((g, ns, nv), F32), jax.ShapeDtypeStruct((1, g * ns), F32),
                   jax.ShapeDtypeStruct((1, g * ns), F32)],
        compiler_params=_params(("parallel", "arbitrary")),
    )(lam_r, lam_i, xr, xi, xr, xi, u, dyc, du_d, b_r, b_i)


def _mesh_place():
    x, y, c = lax.axis_index("x"), lax.axis_index("y"), lax.axis_index("c")
    peers = []
    for k in range(1, N_DEV):
        px, py, pc = x ^ ((k >> 2) & 1), y ^ ((k >> 1) & 1), c ^ (k & 1)
        peers.append(((px, py, pc), 4 * px + 2 * py + pc))
    return 4 * x + 2 * y + c, peers


class _Exchange:
    SAME_CORE_MASKS = (2, 4, 6)

    def __init__(self, arrays, rows, *, gather, name, after=None, two_level=False):
        self.n_arr, self.rows, self.gather, self.name = len(arrays), rows, gather, name
        self.two_level, self.in_flight = two_level, (1 + len(self.SAME_CORE_MASKS) if two_level else N_DEV - 1)
        n_arr = self.n_arr
        if gather:
            assert all(r % BF16_ROWS == 0 for r in rows)
            lands = [lax.empty((N_DEV * r, a.shape[1]), a.dtype) for a, r in zip(arrays, rows)]
        else:
            lands = [lax.empty((N_DEV - 1,) + (tuple(a.shape) if st is None else (n, a.shape[1])), a.dtype)
                     for a, (st, n) in zip(arrays, rows)]
        has_after = after is not None

        def body(*refs):
            ins, zones = refs[:n_arr], refs[n_arr:2 * n_arr]
            sems = refs[2 * n_arr + has_after:4 * n_arr + has_after]
            token = refs[-1]
            me, peers = _mesh_place()
            for i in range(n_arr):
                for k, (pxyz, pid) in enumerate(peers):
                    if two_level and k + 1 not in (1,) + self.SAME_CORE_MASKS:
                        continue
                    if gather:
                        src = ins[i]
                        dst = zones[i].at[pl.ds(pl.multiple_of(me * rows[i], BF16_ROWS), rows[i])]
                    else:
                        stride, n = rows[i]
                        src = ins[i] if stride is None else ins[i].at[pl.ds(pl.multiple_of(pid * stride, BF16_ROWS), n)]
                        dst = zones[i].at[k]
                    pltpu.make_async_remote_copy(
                        src_ref=src, dst_ref=dst, send_sem=sems[2 * i], recv_sem=sems[2 * i + 1],
                        device_id=pxyz, device_id_type=pl.DeviceIdType.MESH).start()
            token[...] = jnp.zeros_like(token)

        hbm = pl.BlockSpec(memory_space=pltpu.HBM)
        sem = pl.BlockSpec(memory_space=pltpu.SEMAPHORE)
        args = [pltpu.with_memory_space_constraint(a, pltpu.HBM) for a in list(arrays) + lands]
        res = pl.pallas_call(
            body, name=name + "_start",
            in_specs=[hbm] * (2 * n_arr) + ([pl.BlockSpec(memory_space=pl.ANY)] if has_after else []),
            out_specs=[sem] * (2 * n_arr) + [hbm] * (2 * n_arr) + [pl.BlockSpec(memory_space=pltpu.VMEM)],
            out_shape=[pltpu.SemaphoreType.DMA(())] * (2 * n_arr) + [pltpu.HBM(a.shape, a.dtype) for a in args]
            + [jax.ShapeDtypeStruct((8, LANES), F32)],
            input_output_aliases={i: 2 * n_arr + i for i in range(2 * n_arr)},
            compiler_params=pltpu.CompilerParams(has_side_effects=pltpu.SideEffectType.DATAFLOW_SIDE_EFFECTING),
        )(*args, *([after] if has_after else []))
        self.sems, self.thru, self.token = res[:2 * n_arr], res[2 * n_arr:4 * n_arr], res[-1]

    def _wait_all(self, zones, sems):
        myself = (lax.axis_index("x"), lax.axis_index("y"), lax.axis_index("c"))
        for i in range(self.n_arr):
            many = zones[i].at[pl.ds(0, self.in_flight * self.rows[i])] if self.gather else zones[i]
            all_of_them = pltpu.make_async_remote_copy(
                src_ref=many, dst_ref=many, send_sem=sems[2 * i], recv_sem=sems[2 * i + 1],
                device_id=myself, device_id_type=pl.DeviceIdType.MESH)
            all_of_them.wait_recv()
            all_of_them.wait_send()

    def forward(self, after):
        n_arr = self.n_arr

        def body(*refs):
            zones, sems = refs[n_arr:2 * n_arr], refs[2 * n_arr:4 * n_arr]
            new_sems = refs[4 * n_arr + 1:6 * n_arr + 1]
            self._wait_all(zones, sems)
            _, peers = _mesh_place()
            sibling, _ = peers[0]
            for i in range(n_arr):
                for mask in self.SAME_CORE_MASKS:
                    _, pid = peers[mask - 1]
                    block = zones[i].at[pl.ds(pl.multiple_of(pid * self.rows[i], BF16_ROWS), self.rows[i])]
                    pltpu.make_async_remote_copy(
                        src_ref=block, dst_ref=block, send_sem=new_sems[2 * i], recv_sem=new_sems[2 * i + 1],
                        device_id=sibling, device_id_type=pl.DeviceIdType.MESH).start()

        hbm = pl.BlockSpec(memory_space=pltpu.HBM)
        sem = pl.BlockSpec(memory_space=pltpu.SEMAPHORE)
        res = pl.pallas_call(
            body, name=self.name + "_forward",
            in_specs=[hbm] * (2 * n_arr) + [sem] * (2 * n_arr) + [pl.BlockSpec(memory_space=pl.ANY)],
            out_specs=[sem] * (2 * n_arr) + [hbm] * (2 * n_arr),
            out_shape=[pltpu.SemaphoreType.DMA(())] * (2 * n_arr) + [pltpu.HBM(a.shape, a.dtype) for a in self.thru],
            input_output_aliases={i: 2 * n_arr + i for i in range(2 * n_arr)},
            compiler_params=pltpu.CompilerParams(has_side_effects=pltpu.SideEffectType.DATAFLOW_SIDE_EFFECTING),
        )(*self.thru, *self.sems, after)
        self.sems, self.thru = res[:2 * n_arr], res[2 * n_arr:]
        self.two_level, self.in_flight = False, len(self.SAME_CORE_MASKS)

    def wait(self, after):
        n_arr = self.n_arr
        if self.two_level:
            self.forward(after)

        def body(*refs):
            self._wait_all(refs[n_arr:2 * n_arr], refs[2 * n_arr:4 * n_arr])

        hbm = pl.BlockSpec(memory_space=pltpu.HBM)
        sem = pl.BlockSpec(memory_space=pltpu.SEMAPHORE)
        res = pl.pallas_call(
            body, name=self.name + "_wait",
            in_specs=[hbm] * (2 * n_arr) + [sem] * (2 * n_arr) + [pl.BlockSpec(memory_space=pl.ANY)],
            out_specs=[hbm] * (2 * n_arr), out_shape=[pltpu.HBM(a.shape, a.dtype) for a in self.thru],
            input_output_aliases={i: i for i in range(2 * n_arr)},
            compiler_params=pltpu.CompilerParams(has_side_effects=pltpu.SideEffectType.DATAFLOW_SIDE_EFFECTING),
        )(*self.thru, *self.sems, after)
        return res[:n_arr], res[n_arr:]


def _my_slot():
    me = 4 * lax.axis_index("x") + 2 * lax.axis_index("y") + lax.axis_index("c")
    return me.astype(jnp.int32).reshape(1)


def _place_own(gathered, blocks, me, *, name):
    n = len(blocks)

    def body(me_ref, *refs):
        for b_ref, o_ref in zip(refs[:n], refs[2 * n:]):
            o_ref[...] = b_ref[...]

    res = pl.pallas_call(
        body, name=name, out_shape=[jax.ShapeDtypeStruct(g.shape, g.dtype) for g in gathered],
        grid_spec=pltpu.PrefetchScalarGridSpec(
            num_scalar_prefetch=1, grid=(1,),
            in_specs=[pl.BlockSpec(b.shape, lambda i, me_ref: (0, 0)) for b in blocks] + [pl.BlockSpec(memory_space=pl.ANY)] * n,
            out_specs=[pl.BlockSpec(b.shape, lambda i, me_ref: (me_ref[0], 0)) for b in blocks]),
        input_output_aliases={1 + n + i: i for i in range(n)}, compiler_params=_params(("arbitrary",)),
    )(me, *blocks, *gathered)
    return list(res)


def _elementwise_tiles(r, c):
    if r % 128 == 0:
        return 128, c
    return r, (256 if c % 256 == 0 else c)


def _adamw_math(g, w, m, v):
    nm = ADAM_B1 * m + (1.0 - ADAM_B1) * g
    nv = ADAM_B2 * v + (1.0 - ADAM_B2) * (g * g)
    m_hat = nm / (1.0 - ADAM_B1 ** ADAM_STEP)
    v_hat = nv / (1.0 - ADAM_B2 ** ADAM_STEP)
    return -ADAM_LR * (m_hat / (jnp.sqrt(v_hat) + ADAM_EPS) + ADAM_WD * w), nm, nv


def _sum_parts(me_ref, own_ref, p_ref, r):
    own = own_ref[...].astype(F32)
    g = None
    for d in range(N_DEV):
        k = jnp.bitwise_xor(me_ref[0], d)
        term = jnp.where(k == 0, own, p_ref[jnp.maximum(k, 1) - 1].astype(F32))
        g = term if g is None else g + term
    return g[0:r, :]


def _sum_adamw(me, sent, stride, parts, r, w=None, m=None, v=None, *, name):
    _, own_rows, cdim = parts.shape
    assert stride is None or stride == own_rows
    tc = 512 if cdim % 512 == 0 else cdim
    update = w is not None

    def body(me_ref, own_ref, p_ref, *refs):
        g = _sum_parts(me_ref, own_ref, p_ref, r)
        if update:
            w_ref, m_ref, v_ref, g_ref, d_ref, nm_ref, nv_ref = refs
            d_ref[...], nm_ref[...], nv_ref[...] = _adamw_math(g, w_ref[...], m_ref[...], v_ref[...])
        else:
            g_ref, = refs
        g_ref[...] = g

    blk = pl.BlockSpec((r, tc), lambda j, me_ref: (0, j))
    own_spec = pl.BlockSpec((own_rows, tc), (lambda j, me_ref: (0, j)) if stride is None else (lambda j, me_ref: (me_ref[0], j)))
    n_out = 4 if update else 1
    res = pl.pallas_call(
        body, name=name, out_shape=[jax.ShapeDtypeStruct((r, cdim), F32)] * n_out,
        grid_spec=pltpu.PrefetchScalarGridSpec(
            num_scalar_prefetch=1, grid=(cdim // tc,),
            in_specs=[own_spec, pl.BlockSpec((N_DEV - 1, own_rows, tc), lambda j, me_ref: (0, 0, j))]
            + ([blk] * 3 if update else []),
            out_specs=[blk] * n_out),
        compiler_params=_params(("parallel",)),
    )(me, sent, parts, *((w, m, v) if update else ()))
    return list(res)


def _adamw(g, w, m, v, *, name):
    r, cdim = w.shape
    tr, tc = _elementwise_tiles(r, cdim)

    def body(g_ref, w_ref, m_ref, v_ref, d_ref, nm_ref, nv_ref):
        d_ref[...], nm_ref[...], nv_ref[...] = _adamw_math(g_ref[...], w_ref[...], m_ref[...], v_ref[...])

    blk = pl.BlockSpec((tr, tc), lambda i, j: (i, j))
    return list(pl.pallas_call(
        body, name=name, grid=(r // tr, cdim // tc), in_specs=[blk] * 4,
        out_specs=[blk] * 3, out_shape=[jax.ShapeDtypeStruct((r, cdim), F32)] * 3,
        compiler_params=_params(("parallel", "parallel")),
    )(g, w, m, v))


SHARD_ROWS_P = {n: (FF_SHARD_P if 'ffn' in n else IN_SHARD_P if n == 'w_in' else None) for n in SHARDED}


def _to_exchange_layout(name, shard):
    t = shard.T if SHARD_AXIS[name] == 1 else shard
    pad = SHARD_ROWS_P[name]
    return t if pad is None else jnp.pad(t, ((0, pad - t.shape[0]), (0, 0)))


def _expand_w_in(wt):
    wt = wt.reshape(N_DEV, IN_SHARD_P, D)[:, :IN_SHARD].reshape(IN_W, D)
    o = Q_RANK + KV_RANK
    kr1, kr2 = wt[o:o + ROPE // 2], wt[o + ROPE // 2:o + ROPE]
    z = jnp.zeros((LANES - ROPE, D), wt.dtype)
    return jnp.concatenate([wt[:o], wt[o + ROPE:], kr1, kr2, z, -kr2, kr1, z], axis=0)


def _expand_w_uq(wt):
    w = wt.reshape(H, QK, Q_RANK)
    z = jnp.zeros((H, LANES - ROPE, Q_RANK), w.dtype)
    q1, q2 = w[:, NOPE:NOPE + ROPE // 2], w[:, NOPE + ROPE // 2:]
    return jnp.concatenate([w[:, :NOPE].reshape(H * NOPE, Q_RANK),
                            jnp.concatenate([q1, q2, z], axis=1).reshape(H * LANES, Q_RANK),
                            jnp.concatenate([-q2, q1, z], axis=1).reshape(H * LANES, Q_RANK)], axis=0)


def _layout_qk_gain(g):
    g = g.reshape(QK)
    g1, g2, z = g[NOPE:NOPE + ROPE // 2], g[NOPE + ROPE // 2:], jnp.zeros((LANES - ROPE,), g.dtype)
    return jnp.stack([g[:NOPE], jnp.concatenate([g1, g2, z]), jnp.concatenate([g2, g1, z])])


def _rep16(a):
    return jnp.repeat(a, SSM_GRP, axis=0)


def _layout_ssm_in(a_re, a_im, log_dt, b_re, b_im):
    b_r = jnp.transpose(b_re, (0, 2, 1)).reshape(SSM_G * SSM_GRP, SSM_P)
    b_i = jnp.transpose(b_im, (0, 2, 1)).reshape(SSM_G * SSM_GRP, SSM_P)
    ldt = jnp.broadcast_to(log_dt.reshape(SSM_G, 1), (SSM_G, SSM_P))
    return _rep16(a_re), _rep16(a_im), _rep16(ldt), b_r, b_i


def _block_diag_b(bb):
    eye = jnp.eye(SSM_PACK, dtype=bb.dtype)
    b5 = bb.reshape(SSM_G // SSM_PACK, SSM_PACK, SSM_GRP, 1, SSM_P) * eye[None, :, None, :, None]
    return b5.reshape(SSM_G // SSM_PACK, SSM_PACK * SSM_GRP, SSM_PACK * SSM_P)


def _block_diag_c(cc):
    eye = jnp.eye(SSM_PACK, dtype=cc.dtype)
    c5 = jnp.transpose(cc, (0, 2, 1)).reshape(SSM_G // SSM_PACK, SSM_PACK, SSM_P, 1, SSM_GRP) * eye[None, :, None, :, None]
    return c5.reshape(SSM_G // SSM_PACK, SSM_PACK * SSM_P, SSM_PACK * SSM_GRP)


def _time_perm(a, inverse=False):
    s, w = a.shape
    c = SCAN_CHUNKS
    if inverse:
        return jnp.transpose(a.reshape(s // c, c, w), (1, 0, 2)).reshape(s, w)
    return jnp.transpose(a.reshape(c, s // c, w), (1, 0, 2)).reshape(s, w)


class _Weights:
    def __init__(self, groups=(), landed=None, me=None):
        self.groups, self.landed, self.me = list(groups), dict(landed or {}), me

    def get(self, name, after):
        if name not in self.landed:
            names, exchange = next(g for g in self.groups if name in g[0])
            blocks, gathered = exchange.wait(after)
            self.landed.update(zip(names, _place_own(gathered, blocks, self.me, name="place_" + names[0])))
        return self.landed[name]

    def __getitem__(self, name):
        return self.landed[name]

    def prefetch(self, name, after):
        for names, exchange in self.groups:
            if name in names and exchange.two_level:
                exchange.forward(after)


def _ffn_gate_up(h, w_gt, w_ut, *, name, tm=1024, tn=1408):
    s, k = h.shape
    n = w_gt.shape[0]
    tm, tn = min(tm, s), _tile(n, tn)
    dims = (((1,), (1,)), ((), ()))

    def body(h_ref, wg_ref, wu_ref, g_ref, u_ref, a_ref):
        hb = h_ref[...].astype(BF16)
        gate = lax.dot_general(hb, wg_ref[...], dims, preferred_element_type=F32)
        up = lax.dot_general(hb, wu_ref[...], dims, preferred_element_type=F32)
        g_ref[...] = gate.astype(BF16)
        u_ref[...] = up.astype(BF16)
        a_ref[...] = _f_swiglu(gate, up)

    w_spec = pl.BlockSpec((tn, k), lambda j, i: (j, 0))
    o_spec = pl.BlockSpec((tm, tn), lambda j, i: (i, j))
    return pl.pallas_call(
        body, name=name, grid=(n // tn, s // tm), in_specs=[pl.BlockSpec((tm, k), lambda j, i: (i, 0)), w_spec, w_spec],
        out_specs=[o_spec] * 3, out_shape=[jax.ShapeDtypeStruct((s, n), BF16)] * 3,
        compiler_params=_params(("parallel", "parallel")),
    )(h, w_gt, w_ut)


def _ffn_dgate_dup(dx_out, w_d, gate, up, *, name, tm=512, tn=1408, deps=()):
    s, k = dx_out.shape
    n = w_d.shape[0]
    tm, tn = min(tm, s), _tile(n, tn)
    deps = [d for d in deps if d is not None]

    def body(dx_ref, wd_ref, g_ref, u_ref, *refs):
        dg_ref, du_ref = refs[len(deps):]
        dact = 0.5 * lax.dot_general(dx_ref[...].astype(BF16), wd_ref[...], (((1,), (1,)), ((), ())),
                                     preferred_element_type=F32)
        _, vjp = jax.vjp(_f_swiglu, g_ref[...].astype(F32), u_ref[...].astype(F32))
        dgate, dup = vjp(dact.astype(BF16))
        dg_ref[...] = dgate.astype(BF16)
        du_ref[...] = dup.astype(BF16)

    o_spec = pl.BlockSpec((tm, tn), lambda j, i: (i, j))
    return pl.pallas_call(
        body, name=name, grid=(n // tn, s // tm),
        in_specs=[pl.BlockSpec((tm, k), lambda j, i: (i, 0)), pl.BlockSpec((tn, k), lambda j, i: (j, 0)), o_spec, o_spec]
        + [pl.BlockSpec(d.shape, lambda j, i: (0, 0)) for d in deps],
        out_specs=[o_spec] * 2, out_shape=[jax.ShapeDtypeStruct((s, n), BF16)] * 2,
        compiler_params=_params(("parallel", "parallel")),
    )(dx_out, w_d, gate, up, *deps)


def _ffn_dh(dgate, dup, w_gt, w_ut, *, name, tm=512):
    s, k = dgate.shape
    n = w_gt.shape[1]
    tm = min(tm, s)

    def body(dg_ref, du_ref, wg_ref, wu_ref, o_ref):
        o_ref[...] = (jnp.dot(dg_ref[...], wg_ref[...], preferred_element_type=F32)
                      + jnp.dot(du_ref[...], wu_ref[...], preferred_element_type=F32)).astype(o_ref.dtype)

    a_spec = pl.BlockSpec((tm, k), lambda i: (i, 0))
    w_spec = pl.BlockSpec((k, n), lambda i: (0, 0))
    return pl.pallas_call(
        body, name=name, grid=(s // tm,), in_specs=[a_spec, a_spec, w_spec, w_spec],
        out_specs=pl.BlockSpec((tm, n), lambda i: (i, 0)), out_shape=jax.ShapeDtypeStruct((s, n), BF16),
        compiler_params=_params(("parallel",)),
    )(dgate, dup, w_gt, w_ut)


def _ffn_fwd(x, g, wc, tag, deps=(), prefetch=()):
    h = _rowwise(_f_norm, [x], [g], [(D, BF16)], name=tag + "_norm", deps=deps)[0]
    gate, up, act = _ffn_gate_up(h, wc.get(tag + '_w_gate', h), wc[tag + '_w_up'], name=tag + "_gate_up")
    for later in (tag + '_w_down',) + tuple(prefetch):
        wc.prefetch(later, gate)
    x_out = _mm(act, wc.get(tag + '_w_down', act), res=x, scale=0.5, name=tag + "_down")
    return x_out, (h, gate, up, act)


def _ffn_bwd(x, g, wc, saved, dx_out, tag, send, deps=()):
    h, gate, up, act = saved
    w_gt, w_ut, w_d = (wc.get(tag + n, h) for n in ('_w_gate', '_w_up', '_w_down'))
    d_d = _mm(act, dx_out, ta=True, scale=0.5, out_dtype=GRAD_DTYPE, name=tag + "_dwdown", deps=deps)
    token = send({tag + '_w_down': d_d})
    dgate, dup = _ffn_dgate_dup(dx_out, w_d, gate, up, name=tag + "_dgate_dup", deps=[token])
    d_gt = _mm(dgate, h, ta=True, out_dtype=GRAD_DTYPE, name=tag + "_dwgate")
    token = send({tag + '_w_gate': d_gt})
    d_ut = _mm(dup, h, ta=True, out_dtype=GRAD_DTYPE, name=tag + "_dwup", deps=[token])
    token = send({tag + '_w_up': d_ut})
    dh = _ffn_dh(dgate, dup, w_gt, w_ut, name=tag + "_dh")
    dx, dg = _rowwise_bwd(_f_norm, [x], [g], [dh], row_grads={0: F32}, const_grads=[0], adds={0: dx_out},
                          name=tag + "_norm_bwd", deps=[token])
    return dx, dg


def _local_step(x, mem, cos, sin, target, wc, ws, send, deps=(), send_small=None):
    gs = {}

    x1, sv1 = _ffn_fwd(x, ws['ffn1_norm'], wc, "ffn1", deps=deps, prefetch=('w_in',))

    h2 = _rowwise(_f_norm, [x1], [ws['mix_norm']], [(D, BF16)], name="mix_norm")[0]
    w_in_raw, w_uq_raw = wc.get('w_in', h2), wc.get('mla_w_uq', h2)
    w_in_e = _expand_w_in(w_in_raw)
    w_uq_e = _expand_w_uq(w_uq_raw)
    proj = _mm(h2, w_in_e, tb=True, name="w_in")
    c_q, c_kv = _rowwise(_f_prep1, [proj], [ws['q_norm'], ws['kv_norm']], [(Q_RANK, BF16), (KV_RANK, BF16)], name="mla_prep1")
    qall = _mm(c_q, w_uq_e, tb=True, out_dtype=BF16, name="w_uq")
    kv = _mm(c_kv, wc['mla_w_ukv'], tb=True, out_dtype=BF16, name="w_ukv")
    q, k, v = _prep2_fwd(qall, kv, proj, cos, sin, ws['qk_gq'], ws['qk_gk'])
    o_mla, lse = _attn_fwd(q, k, v)
    wc.prefetch('ffn2_w_gate', lse)

    u = proj[:, Q_RANK + KV_RANK:Q_RANK + KV_RANK + SSM_W]
    u_p = _time_perm(u)
    disc_in = [ws['ssm_lr'], ws['ssm_li'], ws['ssm_ldt'], ws['ssm_br'], ws['ssm_bi']]
    ar16, ai16, bbr, bbi = _rowwise(_f_disc, disc_in, [], [(SSM_P, F32)] * 4, name="s5_disc")
    a_r = ar16[::SSM_GRP].reshape(1, SSM_N)
    a_i = ai16[::SSM_GRP].reshape(1, SSM_N)
    bblk_r, bblk_i = _block_diag_b(bbr).astype(BF16), _block_diag_b(bbi).astype(BF16)
    cblk_r, cblk_i = _block_diag_c(ws['ssm_cr']).astype(BF16), _block_diag_c(-ws['ssm_ci']).astype(BF16)
    xr, xi, yc = _s5_scan(u_p, bblk_r, bblk_i, a_r, a_i, reverse=False, tb=False, readout=(cblk_r, cblk_i),
                          name="s5_scan_fwd")
    g_p = _rowwise(_f_s5_gelu, [yc, u_p], [ws['ssm_d']], [(SSM_W, F32)], name="s5_gelu")[0]
    z_p = _mm(g_p, wc['ssm_w_glu'], name="s5_glu")
    g_t, z_t = _time_perm(g_p, inverse=True), _time_perm(z_p, inverse=True)
    on_consts = [ws['ssm_b_glu'], ws['out_norm_mla'], ws['out_norm_ssm']]
    ycat = _rowwise(_f_outnorm, [o_mla, g_t, z_t], on_consts, [(D, BF16)], name="out_norm")[0]
    x2 = _mm(ycat, wc['w_o'], res=x1, name="w_o")

    hx = _rowwise(_f_norm, [x2], [ws['xattn_norm']], [(D, BF16)], name="xattn_norm")[0]
    xq = _mm(hx, wc['xattn_w_q'], name="xattn_q")
    mn = _rowwise(_f_norm, [mem], [ws['mem_norm']], [(D, BF16)], name="mem_norm")[0]
    kvm = _mm(mn, wc['xattn_w_kv'], name="xattn_kv")
    xkn, xv = _rowwise(_f_memk, [kvm], [ws['xattn_k_norm']], [(H * XH, BF16), (H * XH, BF16)], name="xattn_knorm")
    xo = _xattn_fwd(xq, xkn, xv, ws['xattn_q_norm'])
    x3 = _mm(xo, wc['xattn_w_o'], tb=True, res=x2, name="xattn_o")

    x4, sv2 = _ffn_fwd(x3, ws['ffn2_norm'], wc, "ffn2")

    def f_loss(yb, tb):
        err = yb - tb
        return err * (1.0 / D), jnp.broadcast_to(jnp.sum(jnp.sum(err * err, axis=1, keepdims=True), axis=0, keepdims=True) * (0.5 / D), (1, LANES))

    dx4, loss = _rowwise(f_loss, [x4, target], [], [(D, F32)], [(1, LANES)], name="loss")

    dx3, gs['ffn2_norm'] = _ffn_bwd(x3, ws['ffn2_norm'], wc, sv2, dx4, "ffn2", send)

    dxo = _mm(dx3, wc['xattn_w_o'], out_dtype=BF16, name="xattn_o_dx")
    send({'xattn_w_o': _mm(dx3, xo, ta=True, out_dtype=GRAD_DTYPE, name="xattn_o_dw")})
    dxq, dxkn, dxv, gs['xattn_q_norm'] = _xattn_bwd(xq, xkn, xv, ws['xattn_q_norm'], dxo)
    dkvm, gs['xattn_k_norm'] = _rowwise_bwd(_f_memk, [kvm], [ws['xattn_k_norm']], [dxkn, dxv], row_grads={0: BF16},
                                            const_grads=[0], name="xattn_knorm_bwd")
    send({'xattn_w_kv': _mm(mn, dkvm, ta=True, out_dtype=GRAD_DTYPE, name="xattn_kv_dw")})
    dmn = _mm(dkvm, wc['xattn_w_kv'], tb=True, out_dtype=BF16, name="xattn_kv_dx")
    gs['mem_norm'] = _rowwise_bwd(_f_norm, [mem], [ws['mem_norm']], [dmn], row_grads={}, const_grads=[0], name="mem_norm_bwd")[0]
    token = send({'xattn_w_q': _mm(hx, dxq, ta=True, out_dtype=GRAD_DTYPE, name="xattn_q_dw")})
    dhx = _mm(dxq, wc['xattn_w_q'], tb=True, out_dtype=BF16, name="xattn_q_dx")
    dx2, gs['xattn_norm'] = _rowwise_bwd(_f_norm, [x2], [ws['xattn_norm']], [dhx], row_grads={0: F32}, const_grads=[0],
                                         adds={0: dx3}, name="xattn_norm_bwd", deps=[token])

    dycat = _mm(dx2, wc['w_o'], tb=True, out_dtype=BF16, name="w_o_dx")
    send({'w_o': _mm(ycat, dx2, ta=True, out_dtype=GRAD_DTYPE, name="w_o_dw")})
    do_mla, dg_t, dz_t, gs['ssm_b_glu'], gs['out_norm_mla'], gs['out_norm_ssm'] = _rowwise_bwd(
        _f_outnorm, [o_mla, g_t, z_t], on_consts, [dycat], row_grads={0: F32, 1: F32, 2: BF16}, const_grads=[0, 1, 2],
        name="out_norm_bwd")

    dz_p, dg_p = _time_perm(dz_t), _time_perm(dg_t)
    send({'ssm_w_glu': _mm(g_p, dz_p, ta=True, out_dtype=GRAD_DTYPE, name="s5_glu_dw")})
    dg_p = _mm(dz_p, wc['ssm_w_glu'], tb=True, res=dg_p, name="s5_glu_dx")
    dyc, du_d, gs['ssm_d'] = _rowwise_bwd(_f_s5_gelu, [yc, u_p], [ws['ssm_d']], [dg_p], row_grads={0: BF16, 1: F32},
                                          const_grads=[0], name="s5_gelu_bwd")
    lam_r, lam_i = _s5_scan(dyc, cblk_r, cblk_i, a_r, -a_i, reverse=True, tb=True, name="s5_scan_bwd")
    du_p, d_bblk_r, d_bblk_i, d_cblk_r, d_cblk_i, d_ar, d_ai = _s5_grads(lam_r, lam_i, xr, xi, u_p, dyc, du_d,
                                                                        bblk_r, bblk_i)
    du = _time_perm(du_p, inverse=True)
    gs['ssm_cr'] = jax.linear_transpose(_block_diag_c, ws['ssm_cr'])(d_cblk_r)[0]
    gs['ssm_ci'] = -jax.linear_transpose(_block_diag_c, ws['ssm_ci'])(d_cblk_i)[0]
    d_bbr = jax.linear_transpose(_block_diag_b, bbr)(d_bblk_r)[0]
    d_bbi = jax.linear_transpose(_block_diag_b, bbi)(d_bblk_i)[0]
    d_ar16 = jnp.zeros((SSM_G * SSM_GRP, SSM_P), F32).at[::SSM_GRP].set(d_ar.reshape(SSM_G, SSM_P))
    d_ai16 = jnp.zeros((SSM_G * SSM_GRP, SSM_P), F32).at[::SSM_GRP].set(d_ai.reshape(SSM_G, SSM_P))
    gs['ssm_lr'], gs['ssm_li'], gs['ssm_ldt'], gs['ssm_br'], gs['ssm_bi'] = _rowwise_bwd(
        _f_disc, disc_in, [], [d_ar16, d_ai16, d_bbr, d_bbi], row_grads={i: F32 for i in range(5)}, const_grads=[],
        name="s5_disc_bwd")

    delta, do_b = _rowwise(_f_delta, [do_mla, o_mla], [], [(H * LANES, F32), (H * VD, BF16)], name="mla_delta")
    dq, dk, dv = _attn_bwd(q, k, v, do_b, lse, delta)
    dqall, dkv, dkr, dkrs, gs['qk_gq'], gs['qk_gk'] = _prep2_bwd(qall, kv, proj, cos, sin, ws['qk_gq'], ws['qk_gk'], dq, dk, dv)
    d_w_uq_e = _mm(dqall, c_q, ta=True, name="w_uq_dw")
    send({'mla_w_uq': jax.linear_transpose(_expand_w_uq, jax.ShapeDtypeStruct(w_uq_raw.shape, F32))(d_w_uq_e)[0]})
    dc_q = _mm(dqall, w_uq_e, out_dtype=BF16, name="w_uq_dx")
    send({'mla_w_ukv': _mm(dkv, c_kv, ta=True, out_dtype=GRAD_DTYPE, name="w_ukv_dw")})
    dc_kv = _mm(dkv, wc['mla_w_ukv'], out_dtype=BF16, name="w_ukv_dx")

    def f_prep1_bwd(pb, dcq, dckv, dub, dkrb, dkrsb, gq, gkv):
        _, vjp = jax.vjp(_f_prep1, pb[:, :Q_RANK + KV_RANK], gq, gkv)
        dpa, dgq, dgkv = vjp((dcq.astype(BF16), dckv.astype(BF16)))
        return jnp.concatenate([dpa, dub, dkrb, dkrsb], axis=-1), dgq, dgkv

    dproj, gs['q_norm'], gs['kv_norm'] = _rowwise(
        f_prep1_bwd, [proj, dc_q, dc_kv, du, dkr, dkrs], [ws['q_norm'], ws['kv_norm']], [(IN_WP, BF16)],
        [(1, Q_RANK), (1, KV_RANK)], name="mla_prep1_bwd")
    d_w_in_e = _mm(dproj, h2, ta=True, name="w_in_dw")
    token = send({'w_in': jax.linear_transpose(_expand_w_in, jax.ShapeDtypeStruct(w_in_raw.shape, F32))(d_w_in_e)[0]})
    dh2 = _mm(dproj, w_in_e, out_dtype=BF16, name="w_in_dx")
    dx1, gs['mix_norm'] = _rowwise_bwd(_f_norm, [x1], [ws['mix_norm']], [dh2], row_grads={0: F32}, const_grads=[0],
                                       adds={0: dx2}, name="mix_norm_bwd", deps=[token])

    token = send_small(gs, loss) if send_small is not None else None
    dx0, gs['ffn1_norm'] = _ffn_bwd(x, ws['ffn1_norm'], wc, sv1, dx1, "ffn1", send, deps=[token])
    return loss, dx0, gs


def _prep2_rows(qall, kv, proj, cos, sin):
    return [qall, kv, (proj, KR_BLOCK, LANES), (proj, KR_BLOCK + 1, LANES), cos, sin]


def _prep2_fwd(qall, kv, proj, cos, sin, gq, gk):
    return _rowwise(_f_prep2, _prep2_rows(qall, kv, proj, cos, sin), [gq, gk],
                    [(H * HQ, BF16), (H * HQ, BF16), (H * VD, BF16)], ts=512, name="mla_prep2")


def _prep2_bwd(qall, kv, proj, cos, sin, gq, gk, dq, dk, dv):
    return _rowwise_bwd(_f_prep2, _prep2_rows(qall, kv, proj, cos, sin), [gq, gk], [dq, dk, dv],
                        row_grads={0: BF16, 1: BF16, 2: F32, 3: F32}, const_grads=[0, 1], ts=512, name="mla_prep2_bwd")


def _rope_tables(pos):
    half = ROPE // 2
    inv = ROPE_THETA ** (-jnp.arange(half, dtype=F32) / half)
    ang = pos.astype(F32)[:, None] * inv[None, :]
    z = jnp.zeros((pos.shape[0], LANES - ROPE), F32)
    cos, sin = jnp.cos(ang), jnp.sin(ang)
    return jnp.concatenate([cos, cos, z], axis=-1), jnp.concatenate([sin, sin, z], axis=-1)


def _small_layout(p):
    lr, li, ldt, br, bi = _layout_ssm_in(p['ssm_a_re'], p['ssm_a_im'], p['ssm_log_dt'], p['ssm_b_re'], p['ssm_b_im'])
    return {
        'ffn1_norm': p['ffn1_norm'].reshape(1, D), 'mix_norm': p['mix_norm'].reshape(1, D),
        'q_norm': p['mla_q_norm'].reshape(1, Q_RANK), 'kv_norm': p['mla_kv_norm'].reshape(1, KV_RANK),
        'qk_gq': _layout_qk_gain(p['mla_qk_norm_q']), 'qk_gk': _layout_qk_gain(p['mla_qk_norm_k']),
        'ssm_lr': lr, 'ssm_li': li, 'ssm_ldt': ldt, 'ssm_br': br, 'ssm_bi': bi,
        'ssm_cr': p['ssm_c_re'], 'ssm_ci': p['ssm_c_im'], 'ssm_d': p['ssm_d'].reshape(1, SSM_W),
        'ssm_b_glu': p['ssm_b_glu'].reshape(1, SSM_W),
        'out_norm_mla': p['out_norm_mla'].reshape(1, SSM_W), 'out_norm_ssm': p['out_norm_ssm'].reshape(1, SSM_W),
        'xattn_norm': p['xattn_norm'].reshape(1, D), 'mem_norm': p['mem_norm'].reshape(1, D),
        'xattn_q_norm': p['xattn_q_norm'].reshape(1, XH), 'xattn_k_norm': p['xattn_k_norm'].reshape(1, XH),
        'ffn2_norm': p['ffn2_norm'].reshape(1, D),
    }


def _pack(arrs, rows):
    flat = jnp.concatenate([a.reshape(-1) for a in arrs])
    return jnp.pad(flat, (0, rows * D - flat.shape[0])).reshape(rows, D)


def _unpack(flat, shapes):
    flat = flat.reshape(-1)
    out, off = [], 0
    for sh in shapes:
        n = int(np.prod(sh))
        out.append(flat[off:off + n].reshape(sh))
        off += n
    return out


def kernel(x, mem, positions, ffn1_norm, ffn1_w_gate, ffn1_w_up, ffn1_w_down, mix_norm, w_in, mla_q_norm, mla_w_uq, mla_kv_norm, mla_w_ukv, mla_qk_norm_q, mla_qk_norm_k, ssm_a_re, ssm_a_im, ssm_log_dt, ssm_b_re, ssm_b_im, ssm_c_re, ssm_c_im, ssm_d, ssm_w_glu, ssm_b_glu, out_norm_mla, out_norm_ssm, w_o, xattn_norm, mem_norm, xattn_w_q, xattn_w_kv, xattn_q_norm, xattn_k_norm, xattn_w_o, ffn2_norm, ffn2_w_gate, ffn2_w_up, ffn2_w_down, loss_target, m_ffn1_norm, m_ffn1_w_gate, m_ffn1_w_up, m_ffn1_w_down, m_mix_norm, m_w_in, m_mla_q_norm, m_mla_w_uq, m_mla_kv_norm, m_mla_w_ukv, m_mla_qk_norm_q, m_mla_qk_norm_k, m_ssm_a_re, m_ssm_a_im, m_ssm_log_dt, m_ssm_b_re, m_ssm_b_im, m_ssm_c_re, m_ssm_c_im, m_ssm_d, m_ssm_w_glu, m_ssm_b_glu, m_out_norm_mla, m_out_norm_ssm, m_w_o, m_xattn_norm, m_mem_norm, m_xattn_w_q, m_xattn_w_kv, m_xattn_q_norm, m_xattn_k_norm, m_xattn_w_o, m_ffn2_norm, m_ffn2_w_gate, m_ffn2_w_up, m_ffn2_w_down, v_ffn1_norm, v_ffn1_w_gate, v_ffn1_w_up, v_ffn1_w_down, v_mix_norm, v_w_in, v_mla_q_norm, v_mla_w_uq, v_mla_kv_norm, v_mla_w_ukv, v_mla_qk_norm_q, v_mla_qk_norm_k, v_ssm_a_re, v_ssm_a_im, v_ssm_log_dt, v_ssm_b_re, v_ssm_b_im, v_ssm_c_re, v_ssm_c_im, v_ssm_d, v_ssm_w_glu, v_ssm_b_glu, v_out_norm_mla, v_out_norm_ssm, v_w_o, v_xattn_norm, v_mem_norm, v_xattn_w_q, v_xattn_w_kv, v_xattn_q_norm, v_xattn_k_norm, v_xattn_w_o, v_ffn2_norm, v_ffn2_w_gate, v_ffn2_w_up, v_ffn2_w_down):
    args = dict(locals())
    w = {n: args[n] for n in WEIGHTS}
    mom = {n: args['m_' + n] for n in WEIGHTS}
    var = {n: args['v_' + n] for n in WEIGHTS}
    return _step(x, mem, positions, loss_target, w, mom, var)


GATHER_GROUPS = [('ffn1_gu', ['ffn1_w_gate', 'ffn1_w_up']), ('ffn1_down', ['ffn1_w_down']),
                 ('mix', ['w_in', 'mla_w_uq', 'mla_w_ukv', 'ssm_w_glu', 'w_o', 'xattn_w_q', 'xattn_w_kv', 'xattn_w_o']),
                 ('ffn2', ['ffn2_w_gate', 'ffn2_w_up', 'ffn2_w_down'])]
SCATTER_GROUPS = [('ffn2_down', ['ffn2_w_down']), ('ffn2_gate', ['ffn2_w_gate']), ('ffn2_up', ['ffn2_w_up']),
                  ('xattn', ['xattn_w_o', 'xattn_w_kv', 'xattn_w_q']),
                  ('mix', ['w_o', 'ssm_w_glu', 'mla_w_uq', 'mla_w_ukv', 'w_in']),
                  ('ffn1_down', ['ffn1_w_down']), ('ffn1_gate', ['ffn1_w_gate']), ('ffn1_up', ['ffn1_w_up'])]


def _step(x, mem, positions, loss_target, w, mom, var):
    blocks = {n: _to_exchange_layout(n, w[n][0]).astype(BF16) for n in SHARDED}
    gathers, token = [], None
    for tag, names in GATHER_GROUPS:
        ex = _Exchange([blocks[n] for n in names], [blocks[n].shape[0] for n in names], gather=True,
                       name="gather_" + tag, after=token, two_level=True)
        gathers.append((names, ex))
        token = ex.token
    me = _my_slot()
    wc = _Weights(gathers, me=me)

    rows = {n: (blocks[n].shape[0], blocks[n].shape[0]) for n in SHARDED}
    ready, scatters = {}, []

    def send(grads):
        ready.update({n: g.astype(GRAD_DTYPE) for n, g in grads.items()})
        for tag, names in SCATTER_GROUPS:
            if all(n in ready for n in names) and not any(t == tag for t, _, _ in scatters):
                ex = _Exchange([ready[n] for n in names], [rows[n] for n in names], gather=False, name="scatter_" + tag)
                scatters.append((tag, names, ex))
                return ex.token
        return None

    small = {n: w[n][0] for n in SMALL}
    small_shapes = [small[n].shape for n in SMALL]
    n_small = sum(int(np.prod(sh)) for sh in small_shapes) + 1
    rows_small = -(-n_small // (8 * D)) * 8
    small_sent = []

    def send_small(gs, loss):
        known = dict(gs, ffn1_norm=jnp.zeros((1, D), F32))
        g_small = jax.linear_transpose(_small_layout, {n: jax.ShapeDtypeStruct(small[n].shape, F32) for n in SMALL})(known)[0]
        pack = _pack([g_small[n] for n in SMALL] + [loss[0, :1]], rows_small)
        small_sent.append(_Exchange([pack], [(None, rows_small)], gather=False, name="scatter_small"))
        return small_sent[0].token

    ws = _small_layout(small)
    cos, sin = _rope_tables(positions[0])
    loss, dx, gs = _local_step(x[0], mem[0], cos, sin, loss_target[0], wc, ws, send, deps=[token], send_small=send_small)
    pad8 = lambda a: jnp.pad(a.reshape(1, D), ((0, 7), (0, 0)))
    last_ex = _Exchange([pad8(gs['ffn1_norm'])], [(None, 8)], gather=False, name="scatter_last")

    out, after = {}, dx
    for _, names, ex in scatters:
        for n, sent, p in zip(names, *ex.wait(after)):
            r = w[n][0].shape[SHARD_AXIS[n]]
            if SHARD_AXIS[n] == 0:
                out[n] = _sum_adamw(me, sent, rows[n][0], p, r, w[n][0], mom[n][0], var[n][0], name="adamw_" + n)
            else:
                g = _sum_adamw(me, sent, rows[n][0], p, r, name="sum_" + n)[0].T
                out[n] = [g] + _adamw(g, w[n][0], mom[n][0], var[n][0], name="adamw_" + n)
        after = out[names[-1]][1]
    state = [_pack([t[n][0] for n in SMALL], rows_small) for t in (w, mom, var)]
    sent, p = small_sent[0].wait(after)
    small_out = _sum_adamw(me, sent[0], None, p[0], rows_small, *state, name="adamw_small")
    loss_total = small_out[0].reshape(-1)[n_small - 1]
    for n, vals in zip(SMALL, zip(*[_unpack(flat, small_shapes) for flat in small_out])):
        out[n] = vals
    sent, p = last_ex.wait(small_out[1])
    last_out = _sum_adamw(me, sent[0], None, p[0], 8, *[pad8(t['ffn1_norm'][0]) for t in (w, mom, var)], name="adamw_last")
    out['ffn1_norm'] = [o[0] for o in last_out]
    outs = [out[n][i][None] for i in range(4) for n in WEIGHTS]
    return (loss_total, dx[None], *outs)
```
